```python
import math
import jax, jax.numpy as jnp
from jax import lax
import numpy as np

D_MODEL = 1024
BATCH = 2
SEQ = 8192
DEPTH = 2

GRID_W = 64
CTX_LEN = 256
D_MIX = D_MODEL
N_MIXERS = 4
D_GROUP = D_MIX // N_MIXERS
D_FF = 2816
N_MOD = 9
EPS = 1e-6

POOL_WINDOWS = (2, 4, 8, 16)
POOL_CH = D_GROUP // len(POOL_WINDOWS)

HY_ORDER = 2
HY_SHORT = 3
HY_BANDS = 16
HY_EMB = 1 + 2 * HY_BANDS
HY_FFN = 64
HY_FILTER_CH = HY_ORDER * 2 * D_GROUP
HY_DECAY_MIN = -math.log(1e-2) / 1.5
HY_DECAY_MAX = -math.log(1e-2) / 0.3

GLA_HEADS = 4
GLA_DK = D_GROUP // 2 // GLA_HEADS
GLA_DV = D_GROUP // GLA_HEADS
GLA_QK = GLA_HEADS * GLA_DK
GLA_LOWRANK = 16
GLA_TAU = 16.0
GLA_CHUNK = 64

CONV_WIDTH = 31

COL_K = 0
COL_V = COL_K + GLA_QK
COL_GF = COL_V + D_GROUP
COL_GB = COL_GF + GLA_LOWRANK
COL_Q = COL_GB + GLA_LOWRANK
COL_R = COL_Q + GLA_QK
COL_POOL = COL_R + D_GROUP
COL_HY = COL_POOL + D_GROUP
COL_CONV = COL_HY + 3 * D_GROUP
P_IN = COL_CONV + 2 * D_GROUP

kernel_name = 'hybrid_prefix_diffusion_block'


def rms_norm(x, g):
    xf = x.astype(jnp.float32)
    y = xf * lax.rsqrt(jnp.mean(xf * xf, axis=-1, keepdims=True) + EPS)
    return (y * g.astype(jnp.float32)).astype(x.dtype)


def modulate(h, shift, scale):
    return h * (1 + scale) + shift


def swiglu(h, wi, wo):
    a, g = jnp.split(h @ wi, 2, axis=-1)
    return (jax.nn.silu(g) * a) @ wo


def half_ffn(h, shift, scale, gate, g, wi, wo):
    return h + 0.5 * gate * swiglu(modulate(rms_norm(h, g), shift, scale), wi, wo)


def depthwise_conv(u, w, b):
    ch = u.shape[-1]
    y = lax.conv_general_dilated(u, w[:, None, :].astype(u.dtype), window_strides=(1,), padding='SAME',
                                 dimension_numbers=('NWC', 'WIO', 'NWC'), feature_group_count=ch)
    return y + b


def box_bounds(n, w):
    pos = jnp.arange(n)
    lo = jnp.clip(pos - w // 2, 0, n)
    hi = jnp.clip(pos - w // 2 + w, 0, n)
    return lo, hi


def pool_1d(u, w):
    bsz, n, ch = u.shape
    csum = jnp.concatenate([jnp.zeros((bsz, 1, ch), u.dtype), jnp.cumsum(u, axis=1)], axis=1)
    lo, hi = box_bounds(n, w)
    cnt = (hi - lo).astype(jnp.float32)[None, :, None]
    return (csum[:, hi] - csum[:, lo]) / cnt


def pool_2d(u, w, rows):
    bsz, n, ch = u.shape
    g = u.reshape(bsz, rows, GRID_W, ch)
    sat = jnp.pad(jnp.cumsum(jnp.cumsum(g, axis=1), axis=2), ((0, 0), (1, 0), (1, 0), (0, 0)))
    rlo, rhi = box_bounds(rows, w)
    clo, chi = box_bounds(GRID_W, w)

    def corner(ri, ci):
        return sat[:, ri][:, :, ci]

    s = corner(rhi, chi) - corner(rlo, chi) - corner(rhi, clo) + corner(rlo, clo)
    cnt = ((rhi - rlo)[:, None] * (chi - clo)[None, :]).astype(jnp.float32)[None, :, :, None]
    return (s / cnt).reshape(bsz, n, ch)


def pool_mixer(u, w_lin, scale, rows):
    uf = u.astype(jnp.float32)
    outs = []
    for gi, w in enumerate(POOL_WINDOWS):
        ug = uf[..., gi * POOL_CH:(gi + 1) * POOL_CH]
        pooled = pool_1d(ug, w) if rows is None else pool_2d(ug, w, rows)
        outs.append((pooled - ug).astype(u.dtype) @ w_lin[gi])
    return jnp.concatenate(outs, axis=-1) * scale


def hyena_filter_spectra(n, w1, b1, w2, b2, w3, deltas):
    f32 = jnp.float32
    t = jnp.linspace(0.0, 1.0, n, dtype=f32)[:, None]
    wpos = (2.0 * math.pi / n) * jnp.arange(n, dtype=f32)[:, None]
    bands = jnp.linspace(1e-4, HY_BANDS - 1, HY_BANDS, dtype=f32)[None, :]
    z = jnp.concatenate([t, jnp.cos(bands * wpos), -jnp.sin(bands * wpos)], axis=-1)
    h = jnp.sin(z @ w1.astype(f32) + b1.astype(f32))
    h = jnp.sin(h @ w2.astype(f32) + b2.astype(f32))
    h = (h @ w3.astype(f32)) * jnp.exp(-t * jnp.abs(deltas.astype(f32)))
    h = h.reshape(n, HY_ORDER, 2, D_GROUP)
    h = h / (jnp.sum(jnp.abs(h), axis=(0, 2), keepdims=True) + EPS)
    h_fwd, h_bwd = h[:, :, 0], h[:, :, 1]
    two_sided = jnp.concatenate([h_fwd, jnp.zeros((1, HY_ORDER, D_GROUP), f32), h_bwd[:0:-1]], axis=0)
    return jnp.fft.rfft(two_sided, axis=0)


def fft_long_conv(u, spec, bias):
    n = u.shape[1]
    uf = u.astype(jnp.float32)
    y = jnp.fft.irfft(jnp.fft.rfft(uf, n=2 * n, axis=1) * spec[None], n=2 * n, axis=1)[:, :n]
    return (y + uf * bias.astype(jnp.float32)).astype(u.dtype)


def hyena_mixer(u, p):
    n = u.shape[1]
    uc = depthwise_conv(u, p['hy_short_w'], p['hy_short_b'])
    v, x1, x2 = jnp.split(uc, 3, axis=-1)
    spec = hyena_filter_spectra(n, p['hy_w1'], p['hy_b1'], p['hy_w2'], p['hy_b2'], p['hy_w3'], p['hy_deltas'])
    z = x1 * fft_long_conv(v, spec[:, 0], p['hy_bias'][0])
    return x2 * fft_long_conv(z, spec[:, 1], p['hy_bias'][1])


def gla_heads(t, dh):
    bsz, n, _ = t.shape
    return t.reshape(bsz, n, GLA_HEADS, dh).transpose(0, 2, 1, 3)


def flip_seq(t):
    return jnp.flip(t, axis=2)


def gla_log_decay(g, w, bias):
    a = g.astype(jnp.float32) @ w.astype(jnp.float32) + bias.astype(jnp.float32)
    return gla_heads(jax.nn.log_sigmoid(a) / GLA_TAU, GLA_DK)


def gla_kv_decay(u, gw_f, gb_f, gw_b, gb_b):
    k = gla_heads(u[..., COL_K:COL_V].astype(jnp.float32), GLA_DK)
    v = gla_heads(u[..., COL_V:COL_GF].astype(jnp.float32), GLA_DV)
    log_f = gla_log_decay(u[..., COL_GF:COL_GB], gw_f, gb_f)
    log_b = gla_log_decay(u[..., COL_GB:COL_Q], gw_b, gb_b)
    return ((k, v, log_f), (flip_seq(k), flip_seq(v), flip_seq(log_b)))


def gla_chunk_states(k, v, log_a, s0):
    bsz, nh, n, dk = k.shape
    dv = v.shape[-1]
    nc = n // GLA_CHUNK
    k = k.reshape(bsz, nh, nc, GLA_CHUNK, dk)
    v = v.reshape(bsz, nh, nc, GLA_CHUNK, dv)
    b = jnp.cumsum(log_a.reshape(bsz, nh, nc, GLA_CHUNK, dk), axis=3)
    b_last = b[:, :, :, -1:, :]
    upd = jnp.einsum('bhncd,bhnce->bhnde', k * jnp.exp(b_last - b), v)
    dec = jnp.exp(b_last[:, :, :, 0, :])

    def step(s, inp):
        d, u = inp
        return d[..., None] * s + u, s

    s_final, s_before = lax.scan(step, s0, (jnp.moveaxis(dec, 2, 0), jnp.moveaxis(upd, 2, 0)))
    return jnp.moveaxis(s_before, 0, 2), s_final, b


def gla_readout(q, k, v, b, s_before):
    bsz, nh, n, dk = q.shape
    dv = v.shape[-1]
    nc = n // GLA_CHUNK
    qe = q.reshape(bsz, nh, nc, GLA_CHUNK, dk) * jnp.exp(b)
    ke = k.reshape(bsz, nh, nc, GLA_CHUNK, dk) * jnp.exp(-b)
    v = v.reshape(bsz, nh, nc, GLA_CHUNK, dv)
    lower = jnp.tril(jnp.ones((GLA_CHUNK, GLA_CHUNK), bool))
    a = jnp.where(lower, jnp.einsum('bhncd,bhnjd->bhncj', qe, ke), 0.0)
    o = jnp.einsum('bhncj,bhnje->bhnce', a, v) + jnp.einsum('bhncd,bhnde->bhnce', qe, s_before)
    return o.reshape(bsz, nh, n, dv)


def gla_states(dirs, s0_f, s0_b):
    (kf, vf, lf), (kb, vb, lb) = dirs
    return (gla_chunk_states(kf, vf, lf, s0_f), gla_chunk_states(kb, vb, lb, s0_b))


def gla_output(u, dirs, states, norm_g):
    q = gla_heads(u[..., COL_Q:COL_R].astype(jnp.float32), GLA_DK) * (GLA_DK ** -0.5)
    (kf, vf, _), (kb, vb, _) = dirs
    (sf, _, bf), (sb, _, bb) = states
    o = gla_readout(q, kf, vf, bf, sf) + flip_seq(gla_readout(flip_seq(q), kb, vb, bb, sb))
    o = o * lax.rsqrt(jnp.mean(o * o, axis=-1, keepdims=True) + EPS) * norm_g.astype(jnp.float32)
    bsz, _, n, _ = o.shape
    o = o.transpose(0, 2, 1, 3).reshape(bsz, n, D_GROUP).astype(u.dtype)
    return o * jax.nn.silu(u[..., COL_R:COL_POOL])


def conformer_conv(u, dw_w, dw_b, ln_g, ln_b):
    a, g = jnp.split(u, 2, axis=-1)
    h = depthwise_conv(a * jax.nn.sigmoid(g), dw_w, dw_b)
    hf = h.astype(jnp.float32)
    mu = jnp.mean(hf, axis=-1, keepdims=True)
    var = jnp.mean(jnp.square(hf - mu), axis=-1, keepdims=True)
    hn = (hf - mu) * lax.rsqrt(var + EPS) * ln_g.astype(jnp.float32) + ln_b.astype(jnp.float32)
    return jax.nn.silu(hn).astype(u.dtype)


def mix_tokens(u, rows, dirs, states, p):
    pool = pool_mixer(u[..., COL_POOL:COL_HY], p['pool_w'], p['pool_scale'], rows)
    hy = hyena_mixer(u[..., COL_HY:COL_CONV], p)
    gla = gla_output(u, dirs, states, p['gla_norm'])
    conv = conformer_conv(u[..., COL_CONV:P_IN], p['conv_dw_w'], p['conv_dw_b'], p['conv_ln_g'], p['conv_ln_b'])
    return jnp.concatenate([pool, hy, gla, conv], axis=-1) @ p['w_out']


def hybrid_layer(x, ctx, c, c_ctx, p, last):
    d = D_MODEL
    mod_x = (jax.nn.silu(c) @ p['ada_w'] + p['ada_b'])[:, None, :]
    sx = jnp.split(mod_x, N_MOD, axis=-1)
    n_mod_c = 5 if last else N_MOD
    mod_c = jax.nn.silu(c_ctx) @ p['ada_w'][:, :n_mod_c * d] + p['ada_b'][:n_mod_c * d]
    sc = jnp.split(mod_c, n_mod_c, axis=-1)
    x = half_ffn(x, sx[0], sx[1], sx[2], p['ffn1_norm'], p['ffn1_wi'], p['ffn1_wo'])
    ctx = half_ffn(ctx, sc[0], sc[1], sc[2], p['ffn1_norm'], p['ffn1_wi'], p['ffn1_wo'])
    u_x = modulate(rms_norm(x, p['mix_norm']), sx[3], sx[4]) @ p['w_in']
    w_in_c = p['w_in'][:, :COL_Q] if last else p['w_in']
    u_c = modulate(rms_norm(ctx, p['mix_norm']), sc[3], sc[4]) @ w_in_c
    bsz = x.shape[0]
    zero_state = jnp.zeros((bsz, GLA_HEADS, GLA_DK, GLA_DV), jnp.float32)
    dirs_c = gla_kv_decay(u_c, p['gla_gw_f'], p['gla_gb_f'], p['gla_gw_b'], p['gla_gb_b'])
    states_c = gla_states(dirs_c, zero_state, zero_state)
    dirs_x = gla_kv_decay(u_x, p['gla_gw_f'], p['gla_gb_f'], p['gla_gw_b'], p['gla_gb_b'])
    states_x = gla_states(dirs_x, states_c[0][1], states_c[1][1])
    rows = x.shape[1] // GRID_W
    x = x + sx[5] * mix_tokens(u_x, rows, dirs_x, states_x, p)
    x = half_ffn(x, sx[6], sx[7], sx[8], p['ffn2_norm'], p['ffn2_wi'], p['ffn2_wo'])
    if not last:
        ctx = ctx + sc[5] * mix_tokens(u_c, None, dirs_c, states_c, p)
        ctx = half_ffn(ctx, sc[6], sc[7], sc[8], p['ffn2_norm'], p['ffn2_wi'], p['ffn2_wo'])
    return x, ctx


def setup_inputs(seed: int = 0) -> dict:
    key = jax.random.key(seed)
    ks = iter(jax.random.split(key, 48))
    f32 = jnp.float32

    def nrm(shape, scale):
        return scale * jax.random.normal(next(ks), shape, f32)

    d = D_MODEL
    L = DEPTH
    return {
        'x': nrm((BATCH, SEQ, d), 1.0),
        'c': nrm((BATCH, d), 1.0),
        'ctx': nrm((BATCH, CTX_LEN, d), 1.0),
        'c_ctx': nrm((d,), 1.0),
        'ada_w': nrm((L, d, N_MOD * d), 0.5 * d ** -0.5),
        'ada_b': nrm((L, N_MOD * d), 0.02),
        'ffn1_norm': 1.0 + nrm((L, d), 0.1),
        'ffn1_wi': nrm((L, d, 2 * D_FF), d ** -0.5),
        'ffn1_wo': nrm((L, D_FF, d), D_FF ** -0.5),
        'mix_norm': 1.0 + nrm((L, d), 0.1),
        'w_in': nrm((L, d, P_IN), d ** -0.5),
        'w_out': nrm((L, D_MIX, d), D_MIX ** -0.5),
        'pool_w': nrm((L, len(POOL_WINDOWS), POOL_CH, POOL_CH), POOL_CH ** -0.5),
        'pool_scale': 1.0 + nrm((L, D_GROUP), 0.1),
        'hy_short_w': nrm((L, HY_SHORT, 3 * D_GROUP), HY_SHORT ** -0.5),
        'hy_short_b': nrm((L, 3 * D_GROUP), 0.02),
        'hy_w1': nrm((L, HY_EMB, HY_FFN), HY_EMB ** -0.5),
        'hy_b1': nrm((L, HY_FFN), 0.1),
        'hy_w2': nrm((L, HY_FFN, HY_FFN), HY_FFN ** -0.5),
        'hy_b2': nrm((L, HY_FFN), 0.1),
        'hy_w3': nrm((L, HY_FFN, HY_FILTER_CH), HY_FFN ** -0.5),
        'hy_deltas': jnp.linspace(HY_DECAY_MIN, HY_DECAY_MAX, HY_FILTER_CH, dtype=f32)[None, :] * (1.0 + nrm((L, HY_FILTER_CH), 0.1)),
        'hy_bias': nrm((L, HY_ORDER, D_GROUP), 1.0),
        'gla_gw_f': nrm((L, GLA_LOWRANK, GLA_QK), GLA_LOWRANK ** -0.5),
        'gla_gb_f': nrm((L, GLA_QK), 0.5),
        'gla_gw_b': nrm((L, GLA_LOWRANK, GLA_QK), GLA_LOWRANK ** -0.5),
        'gla_gb_b': nrm((L, GLA_QK), 0.5),
        'gla_norm': 1.0 + nrm((L, GLA_DV), 0.1),
        'conv_dw_w': nrm((L, CONV_WIDTH, D_GROUP), CONV_WIDTH ** -0.5),
        'conv_dw_b': nrm((L, D_GROUP), 0.02),
        'conv_ln_g': 1.0 + nrm((L, D_GROUP), 0.1),
        'conv_ln_b': nrm((L, D_GROUP), 0.02),
        'ffn2_norm': 1.0 + nrm((L, d), 0.1),
        'ffn2_wi': nrm((L, d, 2 * D_FF), d ** -0.5),
        'ffn2_wo': nrm((L, D_FF, d), D_FF ** -0.5),
        'final_norm': 1.0 + nrm((d,), 0.1),
    }


def reference(x, c, ctx, c_ctx, ada_w, ada_b, ffn1_norm, ffn1_wi, ffn1_wo, mix_norm, w_in, w_out,
              pool_w, pool_scale, hy_short_w, hy_short_b, hy_w1, hy_b1, hy_w2, hy_b2, hy_w3, hy_deltas,
              hy_bias, gla_gw_f, gla_gb_f, gla_gw_b, gla_gb_b, gla_norm, conv_dw_w, conv_dw_b, conv_ln_g,
              conv_ln_b, ffn2_norm, ffn2_wi, ffn2_wo, final_norm):
    for l in range(DEPTH):
        p = {
            'ada_w': ada_w[l], 'ada_b': ada_b[l],
            'ffn1_norm': ffn1_norm[l], 'ffn1_wi': ffn1_wi[l], 'ffn1_wo': ffn1_wo[l],
            'mix_norm': mix_norm[l], 'w_in': w_in[l], 'w_out': w_out[l],
            'pool_w': pool_w[l], 'pool_scale': pool_scale[l],
            'hy_short_w': hy_short_w[l], 'hy_short_b': hy_short_b[l],
            'hy_w1': hy_w1[l], 'hy_b1': hy_b1[l], 'hy_w2': hy_w2[l], 'hy_b2': hy_b2[l], 'hy_w3': hy_w3[l],
            'hy_deltas': hy_deltas[l], 'hy_bias': hy_bias[l],
            'gla_gw_f': gla_gw_f[l], 'gla_gb_f': gla_gb_f[l], 'gla_gw_b': gla_gw_b[l], 'gla_gb_b': gla_gb_b[l],
            'gla_norm': gla_norm[l],
            'conv_dw_w': conv_dw_w[l], 'conv_dw_b': conv_dw_b[l], 'conv_ln_g': conv_ln_g[l], 'conv_ln_b': conv_ln_b[l],
            'ffn2_norm': ffn2_norm[l], 'ffn2_wi': ffn2_wi[l], 'ffn2_wo': ffn2_wo[l],
        }
        x, ctx = hybrid_layer(x, ctx, c, c_ctx, p, l == DEPTH - 1)
    return rms_norm(x, final_norm)
```

```python
import functools
import math

import numpy as np
import jax
import jax.numpy as jnp
from jax import lax
from jax.experimental import pallas as pl
from jax.experimental.pallas import tpu as pltpu

F32 = jnp.float32
BF16 = jnp.bfloat16
HI = lax.Precision.HIGHEST

D_MODEL = 1024
GRID_W = 64
D_GROUP = 256
D_FF = 2816
N_MOD = 9
EPS = 1e-6
POOL_WINDOWS = (2, 4, 8, 16)
POOL_CH = 64
HY_BANDS = 16
HY_EMB = 1 + 2 * HY_BANDS
HY_FFN = 64
HY_FILTER_CH = 4 * D_GROUP
GLA_HEADS = 4
GLA_DK = 32
GLA_DV = 64
GLA_QK = 128
GLA_LOWRANK = 16
GLA_TAU = 16.0
CHUNK = 64
CONV_WIDTH = 31
HY_SHORT = 3

COL_K = 0
COL_V = COL_K + GLA_QK
COL_GF = COL_V + D_GROUP
COL_GB = COL_GF + GLA_LOWRANK
COL_Q = COL_GB + GLA_LOWRANK
COL_R = COL_Q + GLA_QK
COL_POOL = COL_R + D_GROUP
COL_HY = COL_POOL + D_GROUP
COL_CONV = COL_HY + 3 * D_GROUP
P_IN = COL_CONV + 2 * D_GROUP

LANE = 128
DFT2 = 128
MIB = 1024 * 1024


def _cparams(sem, vmem_mib=None):
    kw = dict(dimension_semantics=sem)
    if vmem_mib is not None:
        kw["vmem_limit_bytes"] = vmem_mib * MIB
    return pltpu.CompilerParams(**kw)


def _silu(x):
    return x * jax.nn.sigmoid(x)


def _rms_mod(h, g, m):
    y = h * lax.rsqrt(jnp.mean(h * h, axis=-1, keepdims=True) + EPS) * g
    return y * (1.0 + m[1:2, :]) + m[0:1, :]


def _iota(shape, dim):
    return lax.broadcasted_iota(jnp.int32, shape, dim)


def _mod_body(c_ref, w_ref, b_ref, o_ref):
    a = _silu(c_ref[...])
    o_ref[0] = jnp.dot(a, w_ref[0], precision=HI, preferred_element_type=F32) + b_ref[0]


def _modulation(cc, ada_w, ada_b):
    nl, d, nm = ada_w.shape
    tn = 1024
    return pl.pallas_call(
        _mod_body,
        out_shape=jax.ShapeDtypeStruct((nl, 8, nm), F32),
        grid=(nl, nm // tn),
        in_specs=[pl.BlockSpec((8, d), lambda l, j: (0, 0)),
                  pl.BlockSpec((1, d, tn), lambda l, j: (l, 0, j)),
                  pl.BlockSpec((1, 1, tn), lambda l, j: (l, 0, j))],
        out_specs=pl.BlockSpec((1, 8, tn), lambda l, j: (l, 0, j)),
        compiler_params=_cparams(("parallel", "parallel")),
        name="adaln_mod",
    )(cc, ada_w, ada_b.reshape(nl, 1, nm))


def _ffn_body(h_ref, m_ref, g_ref, wa_ref, wg_ref, wo_ref, fn_ref, o_ref, xn_ref, acc_ref, *, nf, final):
    f = pl.program_id(1)

    @pl.when(f == 0)
    def _():
        xn_ref[...] = _rms_mod(h_ref[...], g_ref[...], m_ref[0]).astype(BF16)
        acc_ref[...] = jnp.zeros_like(acc_ref)

    xn = xn_ref[...]
    a = jnp.dot(xn, wa_ref[...], preferred_element_type=F32)
    g = jnp.dot(xn, wg_ref[...], preferred_element_type=F32)
    hm = (_silu(g) * a).astype(BF16)
    acc_ref[...] += jnp.dot(hm, wo_ref[...], preferred_element_type=F32)

    @pl.when(f == nf - 1)
    def _():
        out = h_ref[...] + (0.5 * m_ref[0][2:3, :]) * acc_ref[...]
        if final:
            out = out * lax.rsqrt(jnp.mean(out * out, axis=-1, keepdims=True) + EPS) * fn_ref[...]
        o_ref[...] = out


def _ffn(h, m, norm_g, wi, wo, *, tm, tiles_per_seq, final_g=None):
    n, d = h.shape
    ff = wo.shape[0]
    tf = ff // 2
    nf = ff // tf
    final = final_g is not None
    fg = final_g if final else norm_g
    return pl.pallas_call(
        functools.partial(_ffn_body, nf=nf, final=final),
        out_shape=jax.ShapeDtypeStruct((n, d), F32),
        grid=(n // tm, nf),
        in_specs=[pl.BlockSpec((tm, d), lambda i, f: (i, 0)),
                  pl.BlockSpec((1, 8, d), lambda i, f: (i // tiles_per_seq, 0, 0)),
                  pl.BlockSpec((1, d), lambda i, f: (0, 0)),
                  pl.BlockSpec((d, tf), lambda i, f: (0, f)),
                  pl.BlockSpec((d, tf), lambda i, f: (0, f + nf)),
                  pl.BlockSpec((tf, d), lambda i, f: (f, 0)),
                  pl.BlockSpec((1, d), lambda i, f: (0, 0))],
        out_specs=pl.BlockSpec((tm, d), lambda i, f: (i, 0)),
        scratch_shapes=[pltpu.VMEM((tm, d), BF16), pltpu.VMEM((tm, d), F32)],
        compiler_params=_cparams(("parallel", "arbitrary"), 48),
        name="half_ffn",
    )(h, m, norm_g.reshape(1, d), wi, wi, wo, fg.reshape(1, d))


def _proj_body(x_ref, m_ref, g_ref, *rest, nparts):
    w_refs = rest[:nparts]
    wg_ref, gw_ref, gb_ref = rest[nparts:nparts + 3]
    o_refs = rest[nparts + 3:]
    xn = _rms_mod(x_ref[...], g_ref[...], m_ref[0]).astype(BF16)
    for w_ref, o_ref in zip(w_refs, o_refs[:nparts]):
        o_ref[...] = jnp.dot(xn, w_ref[...], preferred_element_type=F32)
    ug = jnp.dot(xn, wg_ref[...], preferred_element_type=F32)
    a = jnp.dot(ug, gw_ref[...], precision=HI, preferred_element_type=F32) + gb_ref[...]
    ls = (jnp.minimum(a, 0.0) - jnp.log(1.0 + jnp.exp(-jnp.abs(a)))) * (1.0 / GLA_TAU)
    o_refs[nparts][...] = ls[:, :GLA_QK]
    o_refs[nparts + 1][...] = ls[:, GLA_QK:]


def _proj(x, m, norm_g, w_parts, w_gate, gw, gb, *, tm, tiles_per_seq):
    n, d = x.shape
    nparts = len(w_parts)
    widths = [w.shape[1] for w in w_parts]
    in_specs = [pl.BlockSpec((tm, d), lambda i: (i, 0)),
                pl.BlockSpec((1, 8, d), lambda i: (i // tiles_per_seq, 0, 0)),
                pl.BlockSpec((1, d), lambda i: (0, 0))]
    in_specs += [pl.BlockSpec((d, wd), lambda i: (0, 0)) for wd in widths]
    in_specs += [pl.BlockSpec((d, LANE), lambda i: (0, 0)),
                 pl.BlockSpec((LANE, 2 * GLA_QK), lambda i: (0, 0)),
                 pl.BlockSpec((1, 2 * GLA_QK), lambda i: (0, 0))]
    out_w = widths + [GLA_QK, GLA_QK]
    return pl.pallas_call(
        functools.partial(_proj_body, nparts=nparts),
        out_shape=[jax.ShapeDtypeStruct((n, wd), F32) for wd in out_w],
        grid=(n // tm,),
        in_specs=in_specs,
        out_specs=[pl.BlockSpec((tm, wd), lambda i: (i, 0)) for wd in out_w],
        compiler_params=_cparams(("parallel",), 48),
        name="in_proj",
    )(x, m, norm_g.reshape(1, d), *w_parts, w_gate, gw, gb)


def _tri_masks():
    r = _iota((CHUNK, CHUNK), 0)
    c = _iota((CHUNK, CHUNK), 1)
    return (r >= c).astype(F32), (r <= c).astype(F32)


def _gla_state_body(kf, vf, lf, kb, vb, lb, s0f, s0b, sf_o, sb_o, ff_o, fb_o, stf, stb, *, cpb):
    i = pl.program_id(1)

    @pl.when(i == 0)
    def _():
        stf[...] = s0f[0]
        stb[...] = s0b[0]

    ltri, utri = _tri_masks()
    sshape = (GLA_HEADS * GLA_DV, GLA_QK)
    bmask = (_iota(sshape, 0) >> 6) == (_iota(sshape, 1) >> 5)
    tn_dims = (((0,), (0,)), ((), ()))

    def step(ci, tri, last, k_ref, v_ref, l_ref, st, s_o):
        sl = pl.ds(pl.multiple_of(ci * CHUNK, CHUNK), CHUNK)
        b = jnp.dot(tri, l_ref[sl, :], precision=HI, preferred_element_type=F32)
        bl = b[last:last + 1, :]
        kd = (k_ref[sl, :] * jnp.exp(bl - b)).astype(BF16)
        upd = lax.dot_general(v_ref[sl, :].astype(BF16), kd, tn_dims, preferred_element_type=F32)
        s = st[...]
        s_o[0, ci] = s.astype(BF16)
        st[...] = s * jnp.exp(bl) + jnp.where(bmask, upd, 0.0)

    def fwd(ci, carry):
        step(ci, ltri, CHUNK - 1, kf, vf, lf, stf, sf_o)
        return carry

    def bwd(cj, carry):
        step(cpb - 1 - cj, utri, 0, kb, vb, lb, stb, sb_o)
        return carry

    lax.fori_loop(0, cpb, fwd, 0)
    lax.fori_loop(0, cpb, bwd, 0)
    ff_o[0] = stf[...]
    fb_o[0] = stb[...]


def _gla_states(k, v, lf, lb, s0f, s0b, *, bsz, seq, cpb):
    bt = cpb * CHUNK
    nb = seq // bt
    nc = seq // CHUNK
    srow = GLA_HEADS * GLA_DV

    def tf(b, i):
        return (b * nb + i, 0)

    def tb(b, i):
        return (b * nb + nb - 1 - i, 0)

    sblk = pl.BlockSpec((1, srow, GLA_QK), lambda b, i: (b, 0, 0))
    return pl.pallas_call(
        functools.partial(_gla_state_body, cpb=cpb),
        out_shape=[jax.ShapeDtypeStruct((bsz, nc, srow, GLA_QK), BF16),
                   jax.ShapeDtypeStruct((bsz, nc, srow, GLA_QK), BF16),
                   jax.ShapeDtypeStruct((bsz, srow, GLA_QK), F32),
                   jax.ShapeDtypeStruct((bsz, srow, GLA_QK), F32)],
        grid=(bsz, nb),
        in_specs=[pl.BlockSpec((bt, GLA_QK), tf), pl.BlockSpec((bt, D_GROUP), tf), pl.BlockSpec((bt, GLA_QK), tf),
                  pl.BlockSpec((bt, GLA_QK), tb), pl.BlockSpec((bt, D_GROUP), tb), pl.BlockSpec((bt, GLA_QK), tb),
                  sblk, sblk],
        out_specs=[pl.BlockSpec((1, cpb, srow, GLA_QK), lambda b, i: (b, i, 0, 0)),
                   pl.BlockSpec((1, cpb, srow, GLA_QK), lambda b, i: (b, nb - 1 - i, 0, 0)),
                   sblk, sblk],
        scratch_shapes=[pltpu.VMEM((srow, GLA_QK), F32), pltpu.VMEM((srow, GLA_QK), F32)],
        compiler_params=_cparams(("parallel", "arbitrary")),
        name="gla_states",
    )(k, v, lf, k, v, lb, s0f, s0b)


def _gla_read_body(k_ref, v_ref, q_ref, r_ref, lf_ref, lb_ref, sf_ref, sb_ref, ng_ref, o_ref, *, cpb):
    ltri, utri = _tri_masks()
    hrows = GLA_HEADS * CHUNK
    cpos = _iota((hrows, CHUNK), 0) & (CHUNK - 1)
    ccol = _iota((hrows, CHUNK), 1)
    lowm = cpos >= ccol
    upm = cpos <= ccol
    hq = (_iota((hrows, GLA_QK), 0) >> 6) == (_iota((hrows, GLA_QK), 1) >> 5)
    ho = (_iota((hrows, D_GROUP), 0) >> 6) == (_iota((hrows, D_GROUP), 1) >> 6)
    bavg = jnp.where((_iota((D_GROUP, D_GROUP), 0) >> 6) == (_iota((D_GROUP, D_GROUP), 1) >> 6),
                     1.0 / GLA_DV, 0.0).astype(F32)
    nt_dims = (((1,), (1,)), ((), ()))
    ng = ng_ref[...]

    def body(ci, carry):
        sl = pl.ds(pl.multiple_of(ci * CHUNK, CHUNK), CHUNK)
        bf_ = jnp.dot(ltri, lf_ref[sl, :], precision=HI, preferred_element_type=F32)
        bb_ = jnp.dot(utri, lb_ref[sl, :], precision=HI, preferred_element_type=F32)
        qs = q_ref[sl, :] * (GLA_DK ** -0.5)
        kk = k_ref[sl, :]
        qef = qs * jnp.exp(bf_)
        qeb = qs * jnp.exp(bb_)
        kef = (kk * jnp.exp(-bf_)).astype(BF16)
        keb = (kk * jnp.exp(-bb_)).astype(BF16)
        qf4 = jnp.where(hq, jnp.concatenate([qef] * GLA_HEADS, axis=0), 0.0).astype(BF16)
        qb4 = jnp.where(hq, jnp.concatenate([qeb] * GLA_HEADS, axis=0), 0.0).astype(BF16)
        af = lax.dot_general(qf4, kef, nt_dims, preferred_element_type=F32)
        ab = lax.dot_general(qb4, keb, nt_dims, preferred_element_type=F32)
        att = (jnp.where(lowm, af, 0.0) + jnp.where(upm, ab, 0.0)).astype(BF16)
        oall = jnp.dot(att, v_ref[sl, :].astype(BF16), preferred_element_type=F32)
        om = jnp.where(ho, oall, 0.0)
        o = om[0:CHUNK] + om[CHUNK:2 * CHUNK] + om[2 * CHUNK:3 * CHUNK] + om[3 * CHUNK:4 * CHUNK]
        o = o + lax.dot_general(qef.astype(BF16), sf_ref[0, ci], nt_dims, preferred_element_type=F32)
        o = o + lax.dot_general(qeb.astype(BF16), sb_ref[0, ci], nt_dims, preferred_element_type=F32)
        ms = jnp.dot(o * o, bavg, precision=HI, preferred_element_type=F32)
        o_ref[sl, :] = o * lax.rsqrt(ms + EPS) * ng * _silu(r_ref[sl, :])
        return carry

    lax.fori_loop(0, cpb, body, 0)


def _gla_read(k, v, q, r, lf, lb, sf, sb, ng, *, bsz, seq, cpb):
    bt = cpb * CHUNK
    nb = seq // bt
    srow = GLA_HEADS * GLA_DV
    n = bsz * seq

    def tk(i):
        return (i, 0)

    sspec = pl.BlockSpec((1, cpb, srow, GLA_QK), lambda i: (i // nb, i % nb, 0, 0))
    return pl.pallas_call(
        functools.partial(_gla_read_body, cpb=cpb),
        out_shape=jax.ShapeDtypeStruct((n, D_GROUP), F32),
        grid=(bsz * nb,),
        in_specs=[pl.BlockSpec((bt, GLA_QK), tk), pl.BlockSpec((bt, D_GROUP), tk),
                  pl.BlockSpec((bt, GLA_QK), tk), pl.BlockSpec((bt, D_GROUP), tk),
                  pl.BlockSpec((bt, GLA_QK), tk), pl.BlockSpec((bt, GLA_QK), tk),
                  sspec, sspec, pl.BlockSpec((1, D_GROUP), lambda i: (0, 0))],
        out_specs=pl.BlockSpec((bt, D_GROUP), tk),
        compiler_params=_cparams(("parallel",)),
        name="gla_readout",
    )(k, v, q, r, lf, lb, sf, sb, ng)


def _box_matrix(n, w):
    pos = np.arange(n)
    lo = np.clip(pos - w // 2, 0, n)
    hi = np.clip(pos - w // 2 + w, 0, n)
    col = np.arange(n)[None, :]
    return ((col >= lo[:, None]) & (col < hi[:, None])).astype(np.float32)


def _lane_windows(shape):
    w = jnp.left_shift(2, _iota(shape, 1) >> 6)
    return w, w >> 1


def _box_count(pos, w, half, n):
    return jnp.minimum(pos - half + w, n) - jnp.maximum(pos - half, 0)


def _split_bf16(x):
    hi = x.astype(BF16)
    lo = (x - hi.astype(F32)).astype(BF16)
    return hi, lo


def _pool2d_body(cur_ref, prev_ref, next_ref, pc_ref, w_ref, sc_ref, o_ref, ycol, *, tiles, rows):
    i = pl.program_id(1)
    tm = cur_ref.shape[0]
    hb = prev_ref.shape[0]
    sub = 2 * GRID_W
    pflag = jnp.where(i > 0, 1.0, 0.0)
    nflag = jnp.where(i < tiles - 1, 1.0, 0.0)

    def colpool(x):
        hi, lo = _split_bf16(x)
        halves = []
        for half in range(2):
            lsl = slice(half * LANE, (half + 1) * LANE)
            ys = []
            for wi in (2 * half, 2 * half + 1):
                p = pc_ref[wi]
                ys.append(jnp.dot(p, hi[:, lsl], preferred_element_type=F32)
                          + jnp.dot(p, lo[:, lsl], preferred_element_type=F32))
            lane = _iota((sub, LANE), 1)
            halves.append(jnp.where(lane < POOL_CH, ys[0], ys[1]))
        return jnp.concatenate(halves, axis=1)

    for s in range(hb // sub):
        ycol[s * sub:(s + 1) * sub, :] = colpool(prev_ref[s * sub:(s + 1) * sub, :] * pflag)
    for s in range(tm // sub):
        ycol[hb + s * sub:hb + (s + 1) * sub, :] = colpool(cur_ref[s * sub:(s + 1) * sub, :])
    for s in range(hb // sub):
        ycol[hb + tm + s * sub:hb + tm + (s + 1) * sub, :] = colpool(next_ref[s * sub:(s + 1) * sub, :] * nflag)

    rc = 256
    wl, half = _lane_windows((rc, D_GROUP))
    for r0 in range(0, tm, rc):
        def rows_at(dd):
            return ycol[hb + r0 + GRID_W * dd:hb + r0 + GRID_W * dd + rc, :]

        acc = rows_at(-1) + rows_at(0)
        z2 = acc
        acc = acc + rows_at(-2) + rows_at(1)
        z4 = acc
        for dd in (-4, -3, 2, 3):
            acc = acc + rows_at(dd)
        z8 = acc
        for dd in (-8, -7, -6, -5, 4, 5, 6, 7):
            acc = acc + rows_at(dd)
        z16 = acc
        z = jnp.where(wl == 2, z2, jnp.where(wl == 4, z4, jnp.where(wl == 8, z8, z16)))
        tok = _iota((rc, D_GROUP), 0) + (i * tm + r0)
        rcnt = _box_count(tok >> 6, wl, half, rows)
        ccnt = _box_count(tok & (GRID_W - 1), wl, half, GRID_W)
        pooled = z / (rcnt * ccnt).astype(F32)
        dlt = (pooled - cur_ref[r0:r0 + rc, :]).astype(BF16)
        o_ref[r0:r0 + rc, :] = jnp.dot(dlt, w_ref[...], preferred_element_type=F32) * sc_ref[...]


def _pool2d(u, wbd, scale, *, bsz, seq):
    tm = 1024
    hb = 512
    tiles = seq // tm
    r = tm // hb
    nhb = seq // hb
    pc = np.stack([np.kron(np.eye(2, dtype=np.float32), _box_matrix(GRID_W, w)) for w in POOL_WINDOWS])
    return pl.pallas_call(
        functools.partial(_pool2d_body, tiles=tiles, rows=seq // GRID_W),
        out_shape=jax.ShapeDtypeStruct((bsz * seq, D_GROUP), F32),
        grid=(bsz, tiles),
        in_specs=[pl.BlockSpec((tm, D_GROUP), lambda b, i: (b * tiles + i, 0)),
                  pl.BlockSpec((hb, D_GROUP), lambda b, i: (b * nhb + jnp.maximum(i * r - 1, 0), 0)),
                  pl.BlockSpec((hb, D_GROUP), lambda b, i: (b * nhb + jnp.minimum(i * r + r, nhb - 1), 0)),
                  pl.BlockSpec((4, 2 * GRID_W, 2 * GRID_W), lambda b, i: (0, 0, 0)),
                  pl.BlockSpec((D_GROUP, D_GROUP), lambda b, i: (0, 0)),
                  pl.BlockSpec((1, D_GROUP), lambda b, i: (0, 0))],
        out_specs=pl.BlockSpec((tm, D_GROUP), lambda b, i: (b * tiles + i, 0)),
        scratch_shapes=[pltpu.VMEM((tm + 2 * hb, D_GROUP), F32)],
        compiler_params=_cparams(("parallel", "parallel")),
        name="pool2d",
    )(u, u, u, jnp.asarray(pc, BF16), wbd, scale)


def _pool1d_body(x_ref, p_ref, w_ref, sc_ref, o_ref):
    x = x_ref[...]
    n = x.shape[0]
    hi, lo = _split_bf16(x)
    ys = [jnp.dot(p_ref[wi], hi, preferred_element_type=F32) + jnp.dot(p_ref[wi], lo, preferred_element_type=F32)
          for wi in range(4)]
    wl, half = _lane_windows((n, D_GROUP))
    z = jnp.where(wl == 2, ys[0], jnp.where(wl == 4, ys[1], jnp.where(wl == 8, ys[2], ys[3])))
    cnt = _box_count(_iota((n, D_GROUP), 0), wl, half, n)
    dlt = (z / cnt.astype(F32) - x).astype(BF16)
    o_ref[...] = jnp.dot(dlt, w_ref[...], preferred_element_type=F32) * sc_ref[...]


def _pool1d(u, wbd, scale, *, bsz, seq):
    pm = np.stack([_box_matrix(seq, w) for w in POOL_WINDOWS])
    return pl.pallas_call(
        _pool1d_body,
        out_shape=jax.ShapeDtypeStruct((bsz * seq, D_GROUP), F32),
        grid=(bsz,),
        in_specs=[pl.BlockSpec((seq, D_GROUP), lambda b: (b, 0)),
                  pl.BlockSpec((4, seq, seq), lambda b: (0, 0, 0)),
                  pl.BlockSpec((D_GROUP, D_GROUP), lambda b: (0, 0)),
                  pl.BlockSpec((1, D_GROUP), lambda b: (0, 0))],
        out_specs=pl.BlockSpec((seq, D_GROUP), lambda b: (b, 0)),
        compiler_params=_cparams(("parallel",)),
        name="pool1d",
    )(u, jnp.asarray(pm, BF16), wbd, scale)


def _fill_halo(buf, cur, prev, nxt, i, tps, hb, tm, pre):
    first = (i % tps) == 0
    last = (i % tps) == tps - 1
    buf[0:hb, :] = jnp.where(first, 0.0, pre(prev[...]))
    buf[hb:hb + tm, :] = pre(cur[...])
    buf[hb + tm:hb + tm + hb, :] = jnp.where(last, 0.0, pre(nxt[...]))


def _dwconv(buf, w_ref, r0, rc, taps, off):
    acc = buf[r0 + off:r0 + off + rc, :] * w_ref[0:1, :]
    for j in range(1, taps):
        acc = acc + buf[r0 + off + j:r0 + off + j + rc, :] * w_ref[j:j + 1, :]
    return acc


def _conf_body(cur, prev, nxt, w_ref, b_ref, lg_ref, lb_ref, o_ref, buf, *, tps):
    i = pl.program_id(0)
    tm = cur.shape[0]
    hb = prev.shape[0]

    def glu(u):
        return u[:, :D_GROUP] * jax.nn.sigmoid(u[:, D_GROUP:])

    _fill_halo(buf, cur, prev, nxt, i, tps, hb, tm, glu)
    rc = 128
    off = hb - (CONV_WIDTH - 1) // 2
    for r0 in range(0, tm, rc):
        h = _dwconv(buf, w_ref, r0, rc, CONV_WIDTH, off) + b_ref[...]
        mu = jnp.mean(h, axis=-1, keepdims=True)
        hc = h - mu
        var = jnp.mean(hc * hc, axis=-1, keepdims=True)
        o_ref[r0:r0 + rc, :] = _silu(hc * lax.rsqrt(var + EPS) * lg_ref[...] + lb_ref[...])


def _halo_specs(tm, hb, width, nrows):
    r = tm // hb
    nhb = nrows // hb
    return [pl.BlockSpec((tm, width), lambda i: (i, 0)),
            pl.BlockSpec((hb, width), lambda i: (jnp.maximum(i * r - 1, 0), 0)),
            pl.BlockSpec((hb, width), lambda i: (jnp.minimum(i * r + r, nhb - 1), 0))]


def _conformer(u, w, b, lg, lb, *, seq, tm):
    n = u.shape[0]
    hb = 16
    vec = pl.BlockSpec((1, D_GROUP), lambda i: (0, 0))
    return pl.pallas_call(
        functools.partial(_conf_body, tps=seq // tm),
        out_shape=jax.ShapeDtypeStruct((n, D_GROUP), F32),
        grid=(n // tm,),
        in_specs=_halo_specs(tm, hb, 2 * D_GROUP, n) + [pl.BlockSpec((CONV_WIDTH, D_GROUP), lambda i: (0, 0)),
                                                        vec, vec, vec],
        out_specs=pl.BlockSpec((tm, D_GROUP), lambda i: (i, 0)),
        scratch_shapes=[pltpu.VMEM((tm + 2 * hb, D_GROUP), F32)],
        compiler_params=_cparams(("parallel",)),
        name="conformer_conv",
    )(u, u, u, w, b.reshape(1, -1), lg.reshape(1, -1), lb.reshape(1, -1))


def _short_body(cur, prev, nxt, w_ref, b_ref, v_ref, x1_ref, x2_ref, buf, *, tps):
    i = pl.program_id(0)
    tm = cur.shape[0]
    hb = prev.shape[0]
    _fill_halo(buf, cur, prev, nxt, i, tps, hb, tm, lambda u: u)
    rc = 128
    off = hb - (HY_SHORT - 1) // 2
    for r0 in range(0, tm, rc):
        uc = _dwconv(buf, w_ref, r0, rc, HY_SHORT, off) + b_ref[...]
        v_ref[r0:r0 + rc, :] = uc[:, :D_GROUP]
        x1_ref[r0:r0 + rc, :] = uc[:, D_GROUP:2 * D_GROUP]
        x2_ref[r0:r0 + rc, :] = uc[:, 2 * D_GROUP:]


def _hy_short(u, w, b, *, seq, tm):
    n = u.shape[0]
    hb = 8
    wd = 3 * D_GROUP
    ospec = pl.BlockSpec((tm, D_GROUP), lambda i: (i, 0))
    return pl.pallas_call(
        functools.partial(_short_body, tps=seq // tm),
        out_shape=[jax.ShapeDtypeStruct((n, D_GROUP), F32)] * 3,
        grid=(n // tm,),
        in_specs=_halo_specs(tm, hb, wd, n) + [pl.BlockSpec((HY_SHORT, wd), lambda i: (0, 0)),
                                               pl.BlockSpec((1, wd), lambda i: (0, 0))],
        out_specs=[ospec, ospec, ospec],
        scratch_shapes=[pltpu.VMEM((tm + 2 * hb, wd), F32)],
        compiler_params=_cparams(("parallel",)),
        name="hyena_short_conv",
    )(u, u, u, w, b.reshape(1, -1))


def _filter_features(n):
    i = np.arange(n, dtype=np.float64)
    t = np.linspace(0.0, 1.0, n, dtype=np.float32).astype(np.float64)
    wpos = ((2.0 * math.pi / n) * np.arange(n, dtype=np.float32)).astype(np.float32)
    bands = np.linspace(1e-4, HY_BANDS - 1, HY_BANDS, dtype=np.float32)
    arg = (bands[None, :] * wpos[:, None]).astype(np.float32).astype(np.float64)
    z = np.zeros((n, 64), np.float32)
    z[:, 0] = t
    z[:, 1:1 + HY_BANDS] = np.cos(arg)
    z[:, 1 + HY_BANDS:HY_EMB] = -np.sin(arg)
    del i
    return z


def _filter_body(z_ref, w1_ref, b1_ref, w2_ref, b2_ref, w3_ref, d_ref, h_ref, s_ref):
    i = pl.program_id(0)
    z = z_ref[...]
    h = jnp.sin(jnp.dot(z, w1_ref[...], precision=HI, preferred_element_type=F32) + b1_ref[...])
    h = jnp.sin(jnp.dot(h, w2_ref[...], precision=HI, preferred_element_type=F32) + b2_ref[...])
    h = jnp.dot(h, w3_ref[...], precision=HI, preferred_element_type=F32)
    h = h * jnp.exp(-z[:, 0:1] * jnp.abs(d_ref[...]))
    h_ref[...] = h

    @pl.when(i == 0)
    def _():
        s_ref[...] = jnp.zeros_like(s_ref)

    s_ref[...] += jnp.sum(jnp.abs(h), axis=0, keepdims=True)


def _hy_filter(n, w1, b1, w2, b2, w3, deltas):
    tm = min(n, 512)
    z = jnp.asarray(_filter_features(n))
    w1p = jnp.zeros((64, HY_FFN), F32).at[:HY_EMB].set(w1)
    full = lambda shape: pl.BlockSpec(shape, lambda i: (0, 0))
    return pl.pallas_call(
        _filter_body,
        out_shape=[jax.ShapeDtypeStruct((n, HY_FILTER_CH), F32), jax.ShapeDtypeStruct((1, HY_FILTER_CH), F32)],
        grid=(n // tm,),
        in_specs=[pl.BlockSpec((tm, 64), lambda i: (i, 0)), full((64, HY_FFN)), full((1, HY_FFN)),
                  full((HY_FFN, HY_FFN)), full((1, HY_FFN)), full((HY_FFN, HY_FILTER_CH)),
                  full((1, HY_FILTER_CH))],
        out_specs=[pl.BlockSpec((tm, HY_FILTER_CH), lambda i: (i, 0)), full((1, HY_FILTER_CH))],
        compiler_params=_cparams(("arbitrary",)),
        name="hyena_filter_mlp",
    )(z, w1p, b1.reshape(1, -1), w2, b2.reshape(1, -1), w3, deltas.reshape(1, -1))


@functools.lru_cache(maxsize=None)
def _dft_tables(n1):
    nn = n1 * DFT2
    j = np.arange(DFT2)
    th = 2.0 * np.pi * ((np.outer(j, j)) % DFT2) / DFT2
    fr, fi = np.cos(th), -np.sin(th)
    m_fwd = np.block([[fr, -fi], [fi, fr]])
    m_inv = np.block([[fr, fi], [-fi, fr]])
    k1 = np.arange(n1)
    t1 = np.arange(n1 // 2)
    th1 = 2.0 * np.pi * (np.outer(k1, t1) % n1) / n1
    ar, ai = np.cos(th1), -np.sin(th1)
    a_fwd = np.block([[ar, -ai], [ai, ar]])
    a_real = np.concatenate([ar, ai], axis=0)
    tr = (n1 - 1 - t1) % n1
    thr = 2.0 * np.pi * (np.outer(k1, tr) % n1) / n1
    a_rev = np.concatenate([np.cos(thr), -np.sin(thr)], axis=0)
    tz = (n1 - t1) % n1
    thz = 2.0 * np.pi * (np.outer(k1, tz) % n1) / n1
    a_rev0 = np.concatenate([np.cos(thz), -np.sin(thz)], axis=0)
    a_rev0[:, 0] = 0.0
    th2 = 2.0 * np.pi * (np.outer(t1, k1) % n1) / n1
    cr, ci = np.cos(th2) / nn, np.sin(th2) / nn
    a_inv = np.block([[cr, -ci], [ci, cr]])
    tht = 2.0 * np.pi * (np.outer(k1, j) % nn) / nn
    twc = np.broadcast_to(np.cos(tht)[:, :, None], (n1, DFT2, LANE)).astype(np.float32)
    tws = np.broadcast_to(np.sin(tht)[:, :, None], (n1, DFT2, LANE)).astype(np.float32)
    f32 = lambda a: np.ascontiguousarray(a, dtype=np.float32)
    return dict(m_fwd=f32(m_fwd), m_inv=f32(m_inv), a_fwd=f32(a_fwd), a_real=f32(a_real), a_rev=f32(a_rev),
                a_rev0=f32(a_rev0), a_inv=f32(a_inv), twc=np.ascontiguousarray(twc), tws=np.ascontiguousarray(tws))


def _fft_a_body(x_ref, m_ref, o_ref):
    n1 = o_ref.shape[1]
    x = jnp.concatenate([x_ref[0], x_ref[1]], axis=0).astype(BF16)
    a = jnp.dot(m_ref[...], x, preferred_element_type=F32)
    o_ref[0] = a[:n1]
    o_ref[1] = a[n1:]


def _fft_a(x, m, *, n1, tn):
    ln = x.shape[2]
    return pl.pallas_call(
        _fft_a_body,
        out_shape=jax.ShapeDtypeStruct((2, n1, ln), F32),
        grid=(ln // tn,),
        in_specs=[pl.BlockSpec((2, n1 // 2, tn), lambda j: (0, 0, j)),
                  pl.BlockSpec((2 * n1, n1), lambda j: (0, 0))],
        out_specs=pl.BlockSpec((2, n1, tn), lambda j: (0, 0, j)),
        compiler_params=_cparams(("parallel",)),
        name="fft_stage_a",
    )(x, m)


def _twiddle(c128, s128, width):
    reps = width // LANE
    if reps == 1:
        return c128, s128
    return jnp.concatenate([c128] * reps, axis=1), jnp.concatenate([s128] * reps, axis=1)


def _fft_b_body(a_ref, tc_ref, ts_ref, mf_ref, mi_ref, g_ref, o_ref):
    ar, ai = a_ref[0, 0], a_ref[1, 0]
    c, s = _twiddle(tc_ref[0], ts_ref[0], ar.shape[1])
    x = jnp.concatenate([ar * c + ai * s, ai * c - ar * s], axis=0).astype(BF16)
    xf = jnp.dot(mf_ref[...], x, preferred_element_type=F32)
    xr, xi = xf[:DFT2], xf[DFT2:]
    gr, gi = g_ref[0, 0], g_ref[1, 0]
    y = jnp.concatenate([xr * gr - xi * gi, xr * gi + xi * gr], axis=0).astype(BF16)
    bf = jnp.dot(mi_ref[...], y, preferred_element_type=F32)
    br, bi = bf[:DFT2], bf[DFT2:]
    o_ref[0, 0] = br * c - bi * s
    o_ref[1, 0] = bi * c + br * s


def _fft_b(a, g, order, tabs, *, n1):
    c = a.shape[3]
    blk = pl.BlockSpec((2, 1, DFT2, c), lambda k: (0, k, 0, 0))
    tw = pl.BlockSpec((1, DFT2, LANE), lambda k: (k, 0, 0))
    mat = pl.BlockSpec((2 * DFT2, 2 * DFT2), lambda k: (0, 0))
    return pl.pallas_call(
        _fft_b_body,
        out_shape=jax.ShapeDtypeStruct(a.shape, F32),
        grid=(n1,),
        in_specs=[blk, tw, tw, mat, mat, pl.BlockSpec((2, 1, DFT2, c), lambda k: (0, k, 0, order))],
        out_specs=blk,
        compiler_params=_cparams(("parallel",)),
        name="fft_stage_b",
    )(a, tabs["twc"], tabs["tws"], tabs["m_fwd"], tabs["m_inv"], g)


def _fft_bf_body(a_ref, tc_ref, ts_ref, mf_ref, o_ref):
    ar, ai = a_ref[0, 0], a_ref[1, 0]
    c, s = _twiddle(tc_ref[0], ts_ref[0], ar.shape[1])
    x = jnp.concatenate([ar * c + ai * s, ai * c - ar * s], axis=0).astype(BF16)
    xf = jnp.dot(mf_ref[...], x, preferred_element_type=F32)
    o_ref[0, 0] = xf[:DFT2]
    o_ref[1, 0] = xf[DFT2:]


def _fft_b_forward(a, tabs, *, n1):
    c = a.shape[3]
    blk = pl.BlockSpec((2, 1, DFT2, c), lambda k: (0, k, 0, 0))
    tw = pl.BlockSpec((1, DFT2, LANE), lambda k: (k, 0, 0))
    return pl.pallas_call(
        _fft_bf_body,
        out_shape=jax.ShapeDtypeStruct(a.shape, F32),
        grid=(n1,),
        in_specs=[blk, tw, tw, pl.BlockSpec((2 * DFT2, 2 * DFT2), lambda k: (0, 0))],
        out_specs=blk,
        compiler_params=_cparams(("parallel",)),
        name="fft_stage_b_filter",
    )(a, tabs["twc"], tabs["tws"], tabs["m_fwd"])


def _fft_ai_body(b_ref, m_ref, v_ref, x_ref, bias_ref, o_ref):
    h = o_ref.shape[1]
    b = jnp.concatenate([b_ref[0], b_ref[1]], axis=0).astype(BF16)
    y = jnp.dot(m_ref[...], b, preferred_element_type=F32)
    bias = bias_ref[...]
    o_ref[0] = x_ref[0] * (y[:h] + v_ref[0] * bias)
    o_ref[1] = x_ref[1] * (y[h:] + v_ref[1] * bias)


def _fft_a_inv(b, m, v, xm, bias_t, *, n1, tn):
    ln = b.shape[2]
    half = pl.BlockSpec((2, n1 // 2, tn), lambda j: (0, 0, j))
    return pl.pallas_call(
        _fft_ai_body,
        out_shape=jax.ShapeDtypeStruct((2, n1 // 2, ln), F32),
        grid=(ln // tn,),
        in_specs=[pl.BlockSpec((2, n1, tn), lambda j: (0, 0, j)),
                  pl.BlockSpec((n1, 2 * n1), lambda j: (0, 0)),
                  half, half, pl.BlockSpec((1, tn), lambda j: (0, 0))],
        out_specs=half,
        compiler_params=_cparams(("parallel",)),
        name="fft_stage_a_inv",
    )(b, m, v, xm, bias_t)


def _filt_a_body(hj_ref, hz_ref, hr_ref, s_ref, mf_ref, mr_ref, mz_ref, o_ref, *, tj):
    jb = pl.program_id(0)
    n1 = o_ref.shape[1]
    fch = HY_FILTER_CH
    s = s_ref[...]
    mrev0 = jnp.where(jb == 0, mz_ref[...], mr_ref[...])
    for e in range(tj):
        src = hz_ref if e == 0 else hr_ref
        me = 0 if e == 0 else tj - e
        mrev = mrev0 if e == 0 else mr_ref[...]
        for o in range(2):
            c0 = 2 * D_GROUP * o
            inv = 1.0 / (s[:, c0:c0 + D_GROUP] + s[:, c0 + D_GROUP:c0 + 2 * D_GROUP] + EPS)
            hf = hj_ref[:, e * fch + c0:e * fch + c0 + D_GROUP].astype(BF16)
            hb = src[:, me * fch + c0 + D_GROUP:me * fch + c0 + 2 * D_GROUP].astype(BF16)
            a = (jnp.dot(mf_ref[...], hf, preferred_element_type=F32)
                 + jnp.dot(mrev, hb, preferred_element_type=F32)) * inv
            l0 = e * 2 * D_GROUP + o * D_GROUP
            o_ref[0, :, l0:l0 + D_GROUP] = a[:n1]
            o_ref[1, :, l0:l0 + D_GROUP] = a[n1:]


def _filt_a(hraw, colsum, tabs, *, n1):
    tj = 8
    nj = DFT2 // tj
    h2 = hraw.reshape(n1 // 2, DFT2 * HY_FILTER_CH)
    wblk = tj * HY_FILTER_CH
    mat = pl.BlockSpec((2 * n1, n1 // 2), lambda j: (0, 0))
    return pl.pallas_call(
        functools.partial(_filt_a_body, tj=tj),
        out_shape=jax.ShapeDtypeStruct((2, n1, DFT2 * 2 * D_GROUP), F32),
        grid=(nj,),
        in_specs=[pl.BlockSpec((n1 // 2, wblk), lambda j: (0, j)),
                  pl.BlockSpec((n1 // 2, wblk), lambda j: (0, (nj - j) % nj)),
                  pl.BlockSpec((n1 // 2, wblk), lambda j: (0, nj - 1 - j)),
                  pl.BlockSpec((1, HY_FILTER_CH), lambda j: (0, 0)), mat, mat, mat],
        out_specs=pl.BlockSpec((2, n1, tj * 2 * D_GROUP), lambda j: (0, 0, j)),
        compiler_params=_cparams(("parallel",), 48),
        name="filter_stage_a",
    )(h2, h2, h2, colsum, tabs["a_real"], tabs["a_rev"], tabs["a_rev0"])


def _hyena_long(v, x1, x2, hraw, colsum, bias, *, seq):
    n1 = 2 * seq // DFT2
    tabs = {k: (jnp.asarray(a, BF16) if a.ndim == 2 else jnp.asarray(a)) for k, a in _dft_tables(n1).items()}
    ln = DFT2 * D_GROUP
    tn = min(ln, 2048)
    fa = _filt_a(hraw, colsum, tabs, n1=n1)
    g = _fft_b_forward(fa.reshape(2, n1, DFT2, 2 * D_GROUP), tabs, n1=n1)
    shp = (2, n1 // 2, ln)
    v3, x13, x23 = v.reshape(shp), x1.reshape(shp), x2.reshape(shp)
    z = v3
    for order, xm in ((0, x13), (1, x23)):
        a = _fft_a(z, tabs["a_fwd"], n1=n1, tn=tn)
        b = _fft_b(a.reshape(2, n1, DFT2, D_GROUP), g, order, tabs, n1=n1)
        bias_t = jnp.tile(bias[order].reshape(1, D_GROUP), (1, tn // D_GROUP))
        z = _fft_a_inv(b.reshape(2, n1, ln), tabs["a_inv"], z, xm, bias_t, n1=n1, tn=tn)
    return z.reshape(2 * seq, D_GROUP)


def _hyena_ctx_body(v_ref, x1_ref, x2_ref, h_ref, s_ref, bias_ref, cm_ref, sm_ref, ct_ref, st_ref, o_ref):
    n = v_ref.shape[1]
    cm, sm, ct, st = cm_ref[...], sm_ref[...], ct_ref[...], st_ref[...]
    s = s_ref[...]
    row0 = _iota((n, D_GROUP), 0) == 0
    dot = lambda a, b: jnp.dot(a, b.astype(BF16), preferred_element_type=F32)
    zr, zi = v_ref[0], v_ref[1]
    for order, xm in ((0, x1_ref), (1, x2_ref)):
        c0 = 2 * D_GROUP * order
        inv = 1.0 / (s[:, c0:c0 + D_GROUP] + s[:, c0 + D_GROUP:c0 + 2 * D_GROUP] + EPS)
        hf = h_ref[:, c0:c0 + D_GROUP] * inv
        hb = jnp.where(row0, 0.0, h_ref[:, c0 + D_GROUP:c0 + 2 * D_GROUP] * inv)
        gr = dot(cm, hf + hb)
        gi = dot(sm, hb - hf)
        xr = dot(cm, zr) + dot(sm, zi)
        xi = dot(cm, zi) - dot(sm, zr)
        yr = xr * gr - xi * gi
        yi = xr * gi + xi * gr
        cr = dot(ct, yr) - dot(st, yi)
        ci = dot(ct, yi) + dot(st, yr)
        bias = bias_ref[order:order + 1, :]
        zr = xm[0] * (cr + zr * bias)
        zi = xm[1] * (ci + zi * bias)
    o_ref[0] = zr
    o_ref[1] = zi


def _hyena_ctx(v, x1, x2, hraw, colsum, bias, *, seq):
    nn = 2 * seq
    k = np.arange(nn)
    t = np.arange(seq)
    th = 2.0 * np.pi * (np.outer(k, t) % nn) / nn
    cm, sm = np.cos(th), np.sin(th)
    consts = [jnp.asarray(a, BF16) for a in (cm, sm, cm.T / nn, sm.T / nn)]
    shp = (2, seq, D_GROUP)
    full3 = pl.BlockSpec(shp, lambda i: (0, 0, 0))
    f2 = lambda a: pl.BlockSpec(a.shape, lambda i: (0, 0))
    args = [hraw, colsum, bias] + consts
    out = pl.pallas_call(
        _hyena_ctx_body,
        out_shape=jax.ShapeDtypeStruct(shp, F32),
        grid=(1,),
        in_specs=[full3, full3, full3] + [f2(a) for a in args],
        out_specs=full3,
        compiler_params=_cparams(("arbitrary",)),
        name="hyena_ctx",
    )(v.reshape(shp), x1.reshape(shp), x2.reshape(shp), *args)
    return out.reshape(2 * seq, D_GROUP)


def _out_body(x_ref, m_ref, p_ref, h_ref, g_ref, c_ref, w_ref, o_ref):
    acc = jnp.dot(p_ref[...].astype(BF16), w_ref[0:D_GROUP, :], preferred_element_type=F32)
    acc += jnp.dot(h_ref[...].astype(BF16), w_ref[D_GROUP:2 * D_GROUP, :], preferred_element_type=F32)
    acc += jnp.dot(g_ref[...].astype(BF16), w_ref[2 * D_GROUP:3 * D_GROUP, :], preferred_element_type=F32)
    acc += jnp.dot(c_ref[...].astype(BF16), w_ref[3 * D_GROUP:, :], preferred_element_type=F32)
    o_ref[...] = x_ref[...] + m_ref[0][2:3, :] * acc


def _out_proj(x, m, parts, w_out, *, tm, tiles_per_seq):
    n, d = x.shape
    pspec = pl.BlockSpec((tm, D_GROUP), lambda i: (i, 0))
    return pl.pallas_call(
        _out_body,
        out_shape=jax.ShapeDtypeStruct((n, d), F32),
        grid=(n // tm,),
        in_specs=[pl.BlockSpec((tm, d), lambda i: (i, 0)),
                  pl.BlockSpec((1, 8, d), lambda i: (i // tiles_per_seq, 0, 0)),
                  pspec, pspec, pspec, pspec, pl.BlockSpec((d, d), lambda i: (0, 0))],
        out_specs=pl.BlockSpec((tm, d), lambda i: (i, 0)),
        compiler_params=_cparams(("parallel",)),
        name="out_proj",
    )(x, m, *parts, w_out)


def _mod_rows(mod_l, rows, first, count):
    m = mod_l[rows, first:first + count, :]
    return jnp.pad(m, ((0, 0), (0, 8 - count), (0, 0)))


def _split_w_in(w_in):
    wb = w_in.astype(BF16)
    parts = dict(k=wb[:, COL_K:COL_V], v=wb[:, COL_V:COL_GF], q=wb[:, COL_Q:COL_R], r=wb[:, COL_R:COL_POOL],
                 pool=wb[:, COL_POOL:COL_HY], hy=wb[:, COL_HY:COL_CONV], conv=wb[:, COL_CONV:P_IN])
    gate = jnp.pad(wb[:, COL_GF:COL_Q], ((0, 0), (0, LANE - 2 * GLA_LOWRANK)))
    return parts, gate


def _mix(u, p, *, bsz, seq, is_ctx, states, hraw, colsum):
    sf, sb = states
    cpb = min(8, seq // CHUNK)
    if is_ctx:
        pool = _pool1d(u["pool"], p["pool_wbd"], p["pool_scale"], bsz=bsz, seq=seq)
    else:
        pool = _pool2d(u["pool"], p["pool_wbd"], p["pool_scale"], bsz=bsz, seq=seq)
    tmc = min(seq, 1024)
    v, x1, x2 = _hy_short(u["hy"], p["hy_short_w"], p["hy_short_b"], seq=seq, tm=tmc)
    if is_ctx:
        hy = _hyena_ctx(v, x1, x2, hraw, colsum, p["hy_bias"], seq=seq)
    else:
        hy = _hyena_long(v, x1, x2, hraw, colsum, p["hy_bias"], seq=seq)
    gla = _gla_read(u["k"], u["v"], u["q"], u["r"], u["lf"], u["lb"], sf, sb, p["gla_ng"],
                    bsz=bsz, seq=seq, cpb=cpb)
    conv = _conformer(u["conv"], p["conv_dw_w"], p["conv_dw_b"], p["conv_ln_g"], p["conv_ln_b"], seq=seq, tm=tmc)
    return [pool, hy, gla, conv]


def kernel(x, c, ctx, c_ctx, ada_w, ada_b, ffn1_norm, ffn1_wi, ffn1_wo, mix_norm, w_in, w_out, pool_w, pool_scale, hy_short_w, hy_short_b, hy_w1, hy_b1, hy_w2, hy_b2, hy_w3, hy_deltas, hy_bias, gla_gw_f, gla_gb_f, gla_gw_b, gla_gb_b, gla_norm, conv_dw_w, conv_dw_b, conv_ln_g, conv_ln_b, ffn2_norm, ffn2_wi, ffn2_wo, final_norm):
    bsz, seq, d = x.shape
    clen = ctx.shape[1]
    depth = ada_w.shape[0]
    assert bsz == 2, "the Hyena transform packs exactly two batch rows into one complex signal"
    xs = x.reshape(bsz * seq, d)
    cs = ctx.reshape(bsz * clen, d)
    cc = jnp.zeros((8, d), F32).at[0:bsz].set(c).at[bsz].set(c_ctx)
    mod = _modulation(cc, ada_w, ada_b).reshape(depth, 8, N_MOD, d)
    xrows = np.arange(bsz)
    crows = np.full((1,), bsz)
    tmx = 512
    tps_x = seq // tmx
    tmc = bsz * clen
    zero_state = jnp.zeros((bsz, GLA_HEADS * GLA_DV, GLA_QK), F32)

    for l in range(depth):
        last = l == depth - 1
        ml = mod[l]
        wparts, wgate = _split_w_in(w_in[l])
        gw = jnp.zeros((LANE, 2 * GLA_QK), F32)
        gw = gw.at[0:GLA_LOWRANK, 0:GLA_QK].set(gla_gw_f[l])
        gw = gw.at[GLA_LOWRANK:2 * GLA_LOWRANK, GLA_QK:].set(gla_gw_b[l])
        gb = jnp.concatenate([gla_gb_f[l], gla_gb_b[l]]).reshape(1, -1)
        wbd = jnp.zeros((D_GROUP, D_GROUP), F32)
        for gi in range(len(POOL_WINDOWS)):
            wbd = wbd.at[gi * POOL_CH:(gi + 1) * POOL_CH, gi * POOL_CH:(gi + 1) * POOL_CH].set(pool_w[l, gi])
        p = dict(pool_wbd=wbd.astype(BF16), pool_scale=pool_scale[l].reshape(1, -1),
                 hy_short_w=hy_short_w[l], hy_short_b=hy_short_b[l], hy_bias=hy_bias[l],
                 gla_ng=jnp.tile(gla_norm[l], GLA_HEADS).reshape(1, -1),
                 conv_dw_w=conv_dw_w[l], conv_dw_b=conv_dw_b[l], conv_ln_g=conv_ln_g[l], conv_ln_b=conv_ln_b[l])
        wi1, wo1 = ffn1_wi[l].astype(BF16), ffn1_wo[l].astype(BF16)
        wi2, wo2 = ffn2_wi[l].astype(BF16), ffn2_wo[l].astype(BF16)
        wout = w_out[l].astype(BF16)
        names = ["k", "v", "q", "r", "pool", "hy", "conv"]

        xs = _ffn(xs, _mod_rows(ml, xrows, 0, 3), ffn1_norm[l], wi1, wo1, tm=tmx, tiles_per_seq=tps_x)
        cs = _ffn(cs, _mod_rows(ml, crows, 0, 3), ffn1_norm[l], wi1, wo1, tm=tmc, tiles_per_seq=1)

        mx = _mod_rows(ml, xrows, 3, 3)
        mc = _mod_rows(ml, crows, 3, 3)
        outs = _proj(xs, mx, mix_norm[l], [wparts[nm] for nm in names], wgate, gw, gb, tm=tmx, tiles_per_seq=tps_x)
        ux = dict(zip(names + ["lf", "lb"], outs))
        cnames = ["k", "v"] if last else names
        outs = _proj(cs, mc, mix_norm[l], [wparts[nm] for nm in cnames], wgate, gw, gb, tm=tmc, tiles_per_seq=1)
        uc = dict(zip(cnames + ["lf", "lb"], outs))

        ccpb = clen // CHUNK
        sfc, sbc, finf, finb = _gla_states(uc["k"], uc["v"], uc["lf"], uc["lb"], zero_state, zero_state,
                                           bsz=bsz, seq=clen, cpb=ccpb)
        sfx, sbx, _, _ = _gla_states(ux["k"], ux["v"], ux["lf"], ux["lb"], finf, finb, bsz=bsz, seq=seq, cpb=8)

        hraw, colsum = _hy_filter(seq, hy_w1[l], hy_b1[l], hy_w2[l], hy_b2[l], hy_w3[l], hy_deltas[l])
        mixed = _mix(ux, p, bsz=bsz, seq=seq, is_ctx=False, states=(sfx, sbx), hraw=hraw, colsum=colsum)
        xs = _out_proj(xs, mx, mixed, wout, tm=tmx, tiles_per_seq=tps_x)
        xs = _ffn(xs, _mod_rows(ml, xrows, 6, 3), ffn2_norm[l], wi2, wo2, tm=tmx, tiles_per_seq=tps_x,
                  final_g=final_norm if last else None)
        if not last:
            hraw_c, colsum_c = _hy_filter(clen, hy_w1[l], hy_b1[l], hy_w2[l], hy_b2[l], hy_w3[l], hy_deltas[l])
            mixed = _mix(uc, p, bsz=bsz, seq=clen, is_ctx=True, states=(sfc, sbc), hraw=hraw_c, colsum=colsum_c)
            cs = _out_proj(cs, mc, mixed, wout, tm=tmc, tiles_per_seq=1)
            cs = _ffn(cs, _mod_rows(ml, crows, 6, 3), ffn2_norm[l], wi2, wo2, tm=tmc, tiles_per_seq=1)
    return xs.reshape(bsz, seq, d)
```

```python
import functools
import math

import numpy as np
import jax
import jax.numpy as jnp
from jax import lax
from jax.experimental import pallas as pl
from jax.experimental.pallas import tpu as pltpu

F32 = jnp.float32
BF16 = jnp.bfloat16
HI = lax.Precision.HIGHEST

D_MODEL = 1024
GRID_W = 64
D_GROUP = 256
D_FF = 2816
N_MOD = 9
EPS = 1e-6
POOL_WINDOWS = (2, 4, 8, 16)
POOL_CH = 64
HY_BANDS = 16
HY_EMB = 1 + 2 * HY_BANDS
HY_FFN = 64
HY_FILTER_CH = 4 * D_GROUP
GLA_HEADS = 4
GLA_DK = 32
GLA_DV = 64
GLA_QK = 128
GLA_LOWRANK = 16
GLA_TAU = 16.0
CHUNK = 64
CONV_WIDTH = 31
HY_SHORT = 3

COL_K = 0
COL_V = COL_K + GLA_QK
COL_GF = COL_V + D_GROUP
COL_GB = COL_GF + GLA_LOWRANK
COL_Q = COL_GB + GLA_LOWRANK
COL_R = COL_Q + GLA_QK
COL_POOL = COL_R + D_GROUP
COL_HY = COL_POOL + D_GROUP
COL_CONV = COL_HY + 3 * D_GROUP
P_IN = COL_CONV + 2 * D_GROUP

LANE = 128
DFT2 = 128
MIB = 1024 * 1024


def _cparams(sem, vmem_mib=None):
    kw = dict(dimension_semantics=sem)
    if vmem_mib is not None:
        kw["vmem_limit_bytes"] = vmem_mib * MIB
    return pltpu.CompilerParams(**kw)


def _silu(x):
    return x * jax.nn.sigmoid(x)


def _rms_mod(h, g, m):
    y = h * lax.rsqrt(jnp.mean(h * h, axis=-1, keepdims=True) + EPS) * g
    return y * (1.0 + m[1:2, :]) + m[0:1, :]


def _iota(shape, dim):
    return lax.broadcasted_iota(jnp.int32, shape, dim)


def _mod_body(c_ref, w_ref, b_ref, o_ref):
    a = _silu(c_ref[...])
    o_ref[0] = jnp.dot(a, w_ref[0], precision=HI, preferred_element_type=F32) + b_ref[0]


def _modulation(cc, ada_w, ada_b):
    nl, d, nm = ada_w.shape
    tn = 1024
    return pl.pallas_call(
        _mod_body,
        out_shape=jax.ShapeDtypeStruct((nl, 8, nm), F32),
        grid=(nl, nm // tn),
        in_specs=[pl.BlockSpec((8, d), lambda l, j: (0, 0)),
                  pl.BlockSpec((1, d, tn), lambda l, j: (l, 0, j)),
                  pl.BlockSpec((1, 1, tn), lambda l, j: (l, 0, j))],
        out_specs=pl.BlockSpec((1, 8, tn), lambda l, j: (l, 0, j)),
        compiler_params=_cparams(("parallel", "parallel")),
        name="adaln_mod",
    )(cc, ada_w, ada_b.reshape(nl, 1, nm))


def _ffn_body(h_ref, m_ref, g_ref, wa_ref, wg_ref, wo_ref, fn_ref, o_ref, xn_ref, acc_ref, *, nf, final):
    f = pl.program_id(1)

    @pl.when(f == 0)
    def _():
        xn_ref[...] = _rms_mod(h_ref[...], g_ref[...], m_ref[0]).astype(BF16)
        acc_ref[...] = jnp.zeros_like(acc_ref)

    xn = xn_ref[...]
    a = jnp.dot(xn, wa_ref[...], preferred_element_type=F32)
    g = jnp.dot(xn, wg_ref[...], preferred_element_type=F32)
    hm = (_silu(g) * a).astype(BF16)
    acc_ref[...] += jnp.dot(hm, wo_ref[...], preferred_element_type=F32)

    @pl.when(f == nf - 1)
    def _():
        out = h_ref[...] + (0.5 * m_ref[0][2:3, :]) * acc_ref[...]
        if final:
            out = out * lax.rsqrt(jnp.mean(out * out, axis=-1, keepdims=True) + EPS) * fn_ref[...]
        o_ref[...] = out


def _ffn(h, m, norm_g, wi, wo, *, tm, tiles_per_seq, final_g=None):
    n, d = h.shape
    ff = wo.shape[0]
    tf = ff // 2
    nf = ff // tf
    final = final_g is not None
    fg = final_g if final else norm_g
    return pl.pallas_call(
        functools.partial(_ffn_body, nf=nf, final=final),
        out_shape=jax.ShapeDtypeStruct((n, d), F32),
        grid=(n // tm, nf),
        in_specs=[pl.BlockSpec((tm, d), lambda i, f: (i, 0)),
                  pl.BlockSpec((1, 8, d), lambda i, f: (i // tiles_per_seq, 0, 0)),
                  pl.BlockSpec((1, d), lambda i, f: (0, 0)),
                  pl.BlockSpec((d, tf), lambda i, f: (0, f)),
                  pl.BlockSpec((d, tf), lambda i, f: (0, f + nf)),
                  pl.BlockSpec((tf, d), lambda i, f: (f, 0)),
                  pl.BlockSpec((1, d), lambda i, f: (0, 0))],
        out_specs=pl.BlockSpec((tm, d), lambda i, f: (i, 0)),
        scratch_shapes=[pltpu.VMEM((tm, d), BF16), pltpu.VMEM((tm, d), F32)],
        compiler_params=_cparams(("parallel", "arbitrary"), 48),
        name="half_ffn",
    )(h, m, norm_g.reshape(1, d), wi, wi, wo, fg.reshape(1, d))


def _proj_body(x_ref, m_ref, g_ref, *rest, nparts):
    w_refs = rest[:nparts]
    wg_ref, gw_ref, gb_ref = rest[nparts:nparts + 3]
    o_refs = rest[nparts + 3:]
    xn = _rms_mod(x_ref[...], g_ref[...], m_ref[0]).astype(BF16)
    for w_ref, o_ref in zip(w_refs, o_refs[:nparts]):
        o_ref[...] = jnp.dot(xn, w_ref[...], preferred_element_type=F32)
    ug = jnp.dot(xn, wg_ref[...], preferred_element_type=F32)
    a = jnp.dot(ug, gw_ref[...], precision=HI, preferred_element_type=F32) + gb_ref[...]
    ls = (jnp.minimum(a, 0.0) - jnp.log(1.0 + jnp.exp(-jnp.abs(a)))) * (1.0 / GLA_TAU)
    o_refs[nparts][...] = ls[:, :GLA_QK]
    o_refs[nparts + 1][...] = ls[:, GLA_QK:]


def _proj(x, m, norm_g, w_parts, w_gate, gw, gb, *, tm, tiles_per_seq):
    n, d = x.shape
    nparts = len(w_parts)
    widths = [w.shape[1] for w in w_parts]
    in_specs = [pl.BlockSpec((tm, d), lambda i: (i, 0)),
                pl.BlockSpec((1, 8, d), lambda i: (i // tiles_per_seq, 0, 0)),
                pl.BlockSpec((1, d), lambda i: (0, 0))]
    in_specs += [pl.BlockSpec((d, wd), lambda i: (0, 0)) for wd in widths]
    in_specs += [pl.BlockSpec((d, LANE), lambda i: (0, 0)),
                 pl.BlockSpec((LANE, 2 * GLA_QK), lambda i: (0, 0)),
                 pl.BlockSpec((1, 2 * GLA_QK), lambda i: (0, 0))]
    out_w = widths + [GLA_QK, GLA_QK]
    return pl.pallas_call(
        functools.partial(_proj_body, nparts=nparts),
        out_shape=[jax.ShapeDtypeStruct((n, wd), F32) for wd in out_w],
        grid=(n // tm,),
        in_specs=in_specs,
        out_specs=[pl.BlockSpec((tm, wd), lambda i: (i, 0)) for wd in out_w],
        compiler_params=_cparams(("parallel",), 48),
        name="in_proj",
    )(x, m, norm_g.reshape(1, d), *w_parts, w_gate, gw, gb)


@functools.lru_cache(maxsize=None)
def _chunk_matrices(bt):
    r = np.arange(bt)
    same = (r[:, None] // CHUNK) == (r[None, :] // CHUNK)
    low = same & (r[:, None] >= r[None, :])
    up = same & (r[:, None] <= r[None, :])
    return tuple(np.ascontiguousarray(m, dtype=np.float32) for m in (low, up, same))


def _chunk_sum(mat, hi, lo):
    return jnp.dot(mat, hi, preferred_element_type=F32) + jnp.dot(mat, lo, preferred_element_type=F32)


def _gla_state_body(kf, vf, lf, kb, vb, lb, tl, tu, on, s0f, s0b, sf_o, sb_o, ff_o, fb_o, stf, stb, *, cpb):
    i = pl.program_id(1)

    @pl.when(i == 0)
    def _():
        stf[...] = s0f[0]
        stb[...] = s0b[0]

    sshape = (GLA_HEADS * GLA_DV, GLA_QK)
    bmask = (_iota(sshape, 0) >> 6) == (_iota(sshape, 1) >> 5)
    tn_dims = (((0,), (0,)), ((), ()))
    ones = on[...]

    def direction(k_ref, v_ref, l_ref, tri_ref, st, s_o, order):
        hi, lo = _split_bf16(l_ref[...])
        b = _chunk_sum(tri_ref[...], hi, lo)
        tot = _chunk_sum(ones, hi, lo)
        kd = (k_ref[...] * jnp.exp(tot - b)).astype(BF16)
        dec = jnp.exp(tot)
        vb16 = v_ref[...].astype(BF16)
        s = st[...]
        for ci in order:
            sl = slice(ci * CHUNK, (ci + 1) * CHUNK)
            upd = lax.dot_general(vb16[sl], kd[sl], tn_dims, preferred_element_type=F32)
            s_o[0, ci] = s.astype(BF16)
            s = s * dec[ci * CHUNK:ci * CHUNK + 1, :] + jnp.where(bmask, upd, 0.0)
        st[...] = s

    direction(kf, vf, lf, tl, stf, sf_o, range(cpb))
    direction(kb, vb, lb, tu, stb, sb_o, range(cpb - 1, -1, -1))
    ff_o[0] = stf[...]
    fb_o[0] = stb[...]


def _gla_states(k, v, lf, lb, s0f, s0b, *, bsz, seq, cpb):
    bt = cpb * CHUNK
    nb = seq // bt
    nc = seq // CHUNK
    srow = GLA_HEADS * GLA_DV

    def tf(b, i):
        return (b * nb + i, 0)

    def tb(b, i):
        return (b * nb + nb - 1 - i, 0)

    sblk = pl.BlockSpec((1, srow, GLA_QK), lambda b, i: (b, 0, 0))
    cmat = pl.BlockSpec((bt, bt), lambda b, i: (0, 0))
    tl, tu, on = (jnp.asarray(m).astype(BF16) for m in _chunk_matrices(bt))
    return pl.pallas_call(
        functools.partial(_gla_state_body, cpb=cpb),
        out_shape=[jax.ShapeDtypeStruct((bsz, nc, srow, GLA_QK), BF16),
                   jax.ShapeDtypeStruct((bsz, nc, srow, GLA_QK), BF16),
                   jax.ShapeDtypeStruct((bsz, srow, GLA_QK), F32),
                   jax.ShapeDtypeStruct((bsz, srow, GLA_QK), F32)],
        grid=(bsz, nb),
        in_specs=[pl.BlockSpec((bt, GLA_QK), tf), pl.BlockSpec((bt, D_GROUP), tf), pl.BlockSpec((bt, GLA_QK), tf),
                  pl.BlockSpec((bt, GLA_QK), tb), pl.BlockSpec((bt, D_GROUP), tb), pl.BlockSpec((bt, GLA_QK), tb),
                  cmat, cmat, cmat, sblk, sblk],
        out_specs=[pl.BlockSpec((1, cpb, srow, GLA_QK), lambda b, i: (b, i, 0, 0)),
                   pl.BlockSpec((1, cpb, srow, GLA_QK), lambda b, i: (b, nb - 1 - i, 0, 0)),
                   sblk, sblk],
        scratch_shapes=[pltpu.VMEM((srow, GLA_QK), F32), pltpu.VMEM((srow, GLA_QK), F32)],
        compiler_params=_cparams(("parallel", "arbitrary")),
        name="gla_states",
    )(k, v, lf, k, v, lb, tl, tu, on, s0f, s0b)


def _gla_read_body(k_ref, v_ref, q_ref, r_ref, lf_ref, lb_ref, tl, tu, sf_ref, sb_ref, ng_ref, o_ref, *, cpb):
    hrows = GLA_HEADS * CHUNK
    cpos = _iota((hrows, CHUNK), 0) & (CHUNK - 1)
    ccol = _iota((hrows, CHUNK), 1)
    lowm = cpos >= ccol
    upm = cpos <= ccol
    hq = (_iota((hrows, GLA_QK), 0) >> 6) == (_iota((hrows, GLA_QK), 1) >> 5)
    ho = (_iota((hrows, D_GROUP), 0) >> 6) == (_iota((hrows, D_GROUP), 1) >> 6)
    bavg = jnp.where((_iota((D_GROUP, D_GROUP), 0) >> 6) == (_iota((D_GROUP, D_GROUP), 1) >> 6),
                     1.0 / GLA_DV, 0.0).astype(BF16)
    nt_dims = (((1,), (1,)), ((), ()))

    bf_ = _chunk_sum(tl[...], *_split_bf16(lf_ref[...]))
    bb_ = _chunk_sum(tu[...], *_split_bf16(lb_ref[...]))
    qs = q_ref[...] * (GLA_DK ** -0.5)
    kk = k_ref[...]
    qef = (qs * jnp.exp(bf_)).astype(BF16)
    qeb = (qs * jnp.exp(bb_)).astype(BF16)
    kef = (kk * jnp.exp(-bf_)).astype(BF16)
    keb = (kk * jnp.exp(-bb_)).astype(BF16)
    vb16 = v_ref[...].astype(BF16)
    zero = jnp.zeros((), BF16)
    for ci in range(cpb):
        sl = slice(ci * CHUNK, (ci + 1) * CHUNK)
        qf4 = jnp.where(hq, jnp.concatenate([qef[sl]] * GLA_HEADS, axis=0), zero)
        qb4 = jnp.where(hq, jnp.concatenate([qeb[sl]] * GLA_HEADS, axis=0), zero)
        af = lax.dot_general(qf4, kef[sl], nt_dims, preferred_element_type=F32)
        ab = lax.dot_general(qb4, keb[sl], nt_dims, preferred_element_type=F32)
        att = (jnp.where(lowm, af, 0.0) + jnp.where(upm, ab, 0.0)).astype(BF16)
        oall = jnp.dot(att, vb16[sl], preferred_element_type=F32)
        om = jnp.where(ho, oall, 0.0)
        o = om[0:CHUNK] + om[CHUNK:2 * CHUNK] + om[2 * CHUNK:3 * CHUNK] + om[3 * CHUNK:4 * CHUNK]
        qcat = jnp.concatenate([qef[sl], qeb[sl]], axis=1)
        scat = jnp.concatenate([sf_ref[0, ci], sb_ref[0, ci]], axis=1)
        o_ref[sl, :] = o + lax.dot_general(qcat, scat, nt_dims, preferred_element_type=F32)
    o = o_ref[...]
    ms = _chunk_sum_rhs(o * o, bavg)
    o_ref[...] = o * lax.rsqrt(ms + EPS) * ng_ref[...] * _silu(r_ref[...])


def _chunk_sum_rhs(x, mat):
    hi, lo = _split_bf16(x)
    return jnp.dot(hi, mat, preferred_element_type=F32) + jnp.dot(lo, mat, preferred_element_type=F32)


def _gla_read(k, v, q, r, lf, lb, sf, sb, ng, *, bsz, seq, cpb):
    bt = cpb * CHUNK
    nb = seq // bt
    srow = GLA_HEADS * GLA_DV
    n = bsz * seq

    def tk(i):
        return (i, 0)

    sspec = pl.BlockSpec((1, cpb, srow, GLA_QK), lambda i: (i // nb, i % nb, 0, 0))
    cmat = pl.BlockSpec((bt, bt), lambda i: (0, 0))
    tl, tu, _ = (jnp.asarray(m).astype(BF16) for m in _chunk_matrices(bt))
    return pl.pallas_call(
        functools.partial(_gla_read_body, cpb=cpb),
        out_shape=jax.ShapeDtypeStruct((n, D_GROUP), F32),
        grid=(bsz * nb,),
        in_specs=[pl.BlockSpec((bt, GLA_QK), tk), pl.BlockSpec((bt, D_GROUP), tk),
                  pl.BlockSpec((bt, GLA_QK), tk), pl.BlockSpec((bt, D_GROUP), tk),
                  pl.BlockSpec((bt, GLA_QK), tk), pl.BlockSpec((bt, GLA_QK), tk),
                  cmat, cmat, sspec, sspec, pl.BlockSpec((1, D_GROUP), lambda i: (0, 0))],
        out_specs=pl.BlockSpec((bt, D_GROUP), tk),
        compiler_params=_cparams(("parallel",)),
        name="gla_readout",
    )(k, v, q, r, lf, lb, tl, tu, sf, sb, ng)


def _box_matrix(n, w):
    pos = np.arange(n)
    lo = np.clip(pos - w // 2, 0, n)
    hi = np.clip(pos - w // 2 + w, 0, n)
    col = np.arange(n)[None, :]
    return ((col >= lo[:, None]) & (col < hi[:, None])).astype(np.float32)


def _lane_windows(shape):
    w = jnp.left_shift(2, _iota(shape, 1) >> 6)
    return w, w >> 1


def _box_count(pos, w, half, n):
    return jnp.minimum(pos - half + w, n) - jnp.maximum(pos - half, 0)


def _split_bf16(x):
    hi = x.astype(BF16)
    lo = (x - hi.astype(F32)).astype(BF16)
    return hi, lo


def _pool2d_body(cur_ref, prev_ref, next_ref, pc_ref, w_ref, sc_ref, o_ref, ycol, *, tiles, rows):
    i = pl.program_id(1)
    tm = cur_ref.shape[0]
    hb = prev_ref.shape[0]
    sub = 2 * GRID_W
    pflag = jnp.where(i > 0, 1.0, 0.0)
    nflag = jnp.where(i < tiles - 1, 1.0, 0.0)

    def colpool(x):
        hi, lo = _split_bf16(x)
        halves = []
        for half in range(2):
            lsl = slice(half * LANE, (half + 1) * LANE)
            ys = []
            for wi in (2 * half, 2 * half + 1):
                p = pc_ref[wi]
                ys.append(jnp.dot(p, hi[:, lsl], preferred_element_type=F32)
                          + jnp.dot(p, lo[:, lsl], preferred_element_type=F32))
            lane = _iota((sub, LANE), 1)
            halves.append(jnp.where(lane < POOL_CH, ys[0], ys[1]))
        return jnp.concatenate(halves, axis=1)

    for s in range(hb // sub):
        ycol[s * sub:(s + 1) * sub, :] = colpool(prev_ref[s * sub:(s + 1) * sub, :] * pflag)
    for s in range(tm // sub):
        ycol[hb + s * sub:hb + (s + 1) * sub, :] = colpool(cur_ref[s * sub:(s + 1) * sub, :])
    for s in range(hb // sub):
        ycol[hb + tm + s * sub:hb + tm + (s + 1) * sub, :] = colpool(next_ref[s * sub:(s + 1) * sub, :] * nflag)

    rc = 256
    wl, half = _lane_windows((rc, D_GROUP))
    for r0 in range(0, tm, rc):
        def rows_at(dd):
            return ycol[hb + r0 + GRID_W * dd:hb + r0 + GRID_W * dd + rc, :]

        acc = rows_at(-1) + rows_at(0)
        z2 = acc
        acc = acc + rows_at(-2) + rows_at(1)
        z4 = acc
        for dd in (-4, -3, 2, 3):
            acc = acc + rows_at(dd)
        z8 = acc
        for dd in (-8, -7, -6, -5, 4, 5, 6, 7):
            acc = acc + rows_at(dd)
        z16 = acc
        z = jnp.where(wl == 2, z2, jnp.where(wl == 4, z4, jnp.where(wl == 8, z8, z16)))
        tok = _iota((rc, D_GROUP), 0) + (i * tm + r0)
        rcnt = _box_count(tok >> 6, wl, half, rows)
        ccnt = _box_count(tok & (GRID_W - 1), wl, half, GRID_W)
        pooled = z / (rcnt * ccnt).astype(F32)
        dlt = (pooled - cur_ref[r0:r0 + rc, :]).astype(BF16)
        o_ref[r0:r0 + rc, :] = jnp.dot(dlt, w_ref[...], preferred_element_type=F32) * sc_ref[...]


def _pool2d(u, wbd, scale, *, bsz, seq):
    tm = 1024
    hb = 512
    tiles = seq // tm
    r = tm // hb
    nhb = seq // hb
    pc = np.stack([np.kron(np.eye(2, dtype=np.float32), _box_matrix(GRID_W, w)) for w in POOL_WINDOWS])
    return pl.pallas_call(
        functools.partial(_pool2d_body, tiles=tiles, rows=seq // GRID_W),
        out_shape=jax.ShapeDtypeStruct((bsz * seq, D_GROUP), F32),
        grid=(bsz, tiles),
        in_specs=[pl.BlockSpec((tm, D_GROUP), lambda b, i: (b * tiles + i, 0)),
                  pl.BlockSpec((hb, D_GROUP), lambda b, i: (b * nhb + jnp.maximum(i * r - 1, 0), 0)),
                  pl.BlockSpec((hb, D_GROUP), lambda b, i: (b * nhb + jnp.minimum(i * r + r, nhb - 1), 0)),
                  pl.BlockSpec((4, 2 * GRID_W, 2 * GRID_W), lambda b, i: (0, 0, 0)),
                  pl.BlockSpec((D_GROUP, D_GROUP), lambda b, i: (0, 0)),
                  pl.BlockSpec((1, D_GROUP), lambda b, i: (0, 0))],
        out_specs=pl.BlockSpec((tm, D_GROUP), lambda b, i: (b * tiles + i, 0)),
        scratch_shapes=[pltpu.VMEM((tm + 2 * hb, D_GROUP), F32)],
        compiler_params=_cparams(("parallel", "parallel")),
        name="pool2d",
    )(u, u, u, jnp.asarray(pc, BF16), wbd, scale)


def _pool1d_body(x_ref, p_ref, w_ref, sc_ref, o_ref):
    x = x_ref[...]
    n = x.shape[0]
    hi, lo = _split_bf16(x)
    ys = [jnp.dot(p_ref[wi], hi, preferred_element_type=F32) + jnp.dot(p_ref[wi], lo, preferred_element_type=F32)
          for wi in range(4)]
    wl, half = _lane_windows((n, D_GROUP))
    z = jnp.where(wl == 2, ys[0], jnp.where(wl == 4, ys[1], jnp.where(wl == 8, ys[2], ys[3])))
    cnt = _box_count(_iota((n, D_GROUP), 0), wl, half, n)
    dlt = (z / cnt.astype(F32) - x).astype(BF16)
    o_ref[...] = jnp.dot(dlt, w_ref[...], preferred_element_type=F32) * sc_ref[...]


def _pool1d(u, wbd, scale, *, bsz, seq):
    pm = np.stack([_box_matrix(seq, w) for w in POOL_WINDOWS])
    return pl.pallas_call(
        _pool1d_body,
        out_shape=jax.ShapeDtypeStruct((bsz * seq, D_GROUP), F32),
        grid=(bsz,),
        in_specs=[pl.BlockSpec((seq, D_GROUP), lambda b: (b, 0)),
                  pl.BlockSpec((4, seq, seq), lambda b: (0, 0, 0)),
                  pl.BlockSpec((D_GROUP, D_GROUP), lambda b: (0, 0)),
                  pl.BlockSpec((1, D_GROUP), lambda b: (0, 0))],
        out_specs=pl.BlockSpec((seq, D_GROUP), lambda b: (b, 0)),
        compiler_params=_cparams(("parallel",)),
        name="pool1d",
    )(u, jnp.asarray(pm, BF16), wbd, scale)


def _fill_halo(buf, cur, prev, nxt, i, tps, hb, tm, pre):
    first = (i % tps) == 0
    last = (i % tps) == tps - 1
    buf[0:hb, :] = jnp.where(first, 0.0, pre(prev[...]))
    buf[hb:hb + tm, :] = pre(cur[...])
    buf[hb + tm:hb + tm + hb, :] = jnp.where(last, 0.0, pre(nxt[...]))


def _dwconv(buf, w_ref, r0, rc, taps, off):
    acc = buf[r0 + off:r0 + off + rc, :] * w_ref[0:1, :]
    for j in range(1, taps):
        acc = acc + buf[r0 + off + j:r0 + off + j + rc, :] * w_ref[j:j + 1, :]
    return acc


def _conf_body(cur, prev, nxt, w_ref, b_ref, lg_ref, lb_ref, o_ref, buf, *, tps):
    i = pl.program_id(0)
    tm = cur.shape[0]
    hb = prev.shape[0]

    def glu(u):
        return u[:, :D_GROUP] * jax.nn.sigmoid(u[:, D_GROUP:])

    _fill_halo(buf, cur, prev, nxt, i, tps, hb, tm, glu)
    rc = 128
    off = hb - (CONV_WIDTH - 1) // 2
    for r0 in range(0, tm, rc):
        h = _dwconv(buf, w_ref, r0, rc, CONV_WIDTH, off) + b_ref[...]
        mu = jnp.mean(h, axis=-1, keepdims=True)
        hc = h - mu
        var = jnp.mean(hc * hc, axis=-1, keepdims=True)
        o_ref[r0:r0 + rc, :] = _silu(hc * lax.rsqrt(var + EPS) * lg_ref[...] + lb_ref[...])


def _halo_specs(tm, hb, width, nrows):
    r = tm // hb
    nhb = nrows // hb
    return [pl.BlockSpec((tm, width), lambda i: (i, 0)),
            pl.BlockSpec((hb, width), lambda i: (jnp.maximum(i * r - 1, 0), 0)),
            pl.BlockSpec((hb, width), lambda i: (jnp.minimum(i * r + r, nhb - 1), 0))]


def _conformer(u, w, b, lg, lb, *, seq, tm):
    n = u.shape[0]
    hb = 16
    vec = pl.BlockSpec((1, D_GROUP), lambda i: (0, 0))
    return pl.pallas_call(
        functools.partial(_conf_body, tps=seq // tm),
        out_shape=jax.ShapeDtypeStruct((n, D_GROUP), F32),
        grid=(n // tm,),
        in_specs=_halo_specs(tm, hb, 2 * D_GROUP, n) + [pl.BlockSpec((CONV_WIDTH, D_GROUP), lambda i: (0, 0)),
                                                        vec, vec, vec],
        out_specs=pl.BlockSpec((tm, D_GROUP), lambda i: (i, 0)),
        scratch_shapes=[pltpu.VMEM((tm + 2 * hb, D_GROUP), F32)],
        compiler_params=_cparams(("parallel",)),
        name="conformer_conv",
    )(u, u, u, w, b.reshape(1, -1), lg.reshape(1, -1), lb.reshape(1, -1))


def _short_body(cur, prev, nxt, w_ref, b_ref, v_ref, x1_ref, x2_ref, buf, *, tps):
    i = pl.program_id(0)
    tm = cur.shape[0]
    hb = prev.shape[0]
    _fill_halo(buf, cur, prev, nxt, i, tps, hb, tm, lambda u: u)
    rc = 128
    off = hb - (HY_SHORT - 1) // 2
    for r0 in range(0, tm, rc):
        uc = _dwconv(buf, w_ref, r0, rc, HY_SHORT, off) + b_ref[...]
        v_ref[r0:r0 + rc, :] = uc[:, :D_GROUP]
        x1_ref[r0:r0 + rc, :] = uc[:, D_GROUP:2 * D_GROUP]
        x2_ref[r0:r0 + rc, :] = uc[:, 2 * D_GROUP:]


def _hy_short(u, w, b, *, seq, tm):
    n = u.shape[0]
    hb = 8
    wd = 3 * D_GROUP
    ospec = pl.BlockSpec((tm, D_GROUP), lambda i: (i, 0))
    return pl.pallas_call(
        functools.partial(_short_body, tps=seq // tm),
        out_shape=[jax.ShapeDtypeStruct((n, D_GROUP), F32)] * 3,
        grid=(n // tm,),
        in_specs=_halo_specs(tm, hb, wd, n) + [pl.BlockSpec((HY_SHORT, wd), lambda i: (0, 0)),
                                               pl.BlockSpec((1, wd), lambda i: (0, 0))],
        out_specs=[ospec, ospec, ospec],
        scratch_shapes=[pltpu.VMEM((tm + 2 * hb, wd), F32)],
        compiler_params=_cparams(("parallel",)),
        name="hyena_short_conv",
    )(u, u, u, w, b.reshape(1, -1))


def _filter_features(n):
    i = np.arange(n, dtype=np.float64)
    t = np.linspace(0.0, 1.0, n, dtype=np.float32).astype(np.float64)
    wpos = ((2.0 * math.pi / n) * np.arange(n, dtype=np.float32)).astype(np.float32)
    bands = np.linspace(1e-4, HY_BANDS - 1, HY_BANDS, dtype=np.float32)
    arg = (bands[None, :] * wpos[:, None]).astype(np.float32).astype(np.float64)
    z = np.zeros((n, 64), np.float32)
    z[:, 0] = t
    z[:, 1:1 + HY_BANDS] = np.cos(arg)
    z[:, 1 + HY_BANDS:HY_EMB] = -np.sin(arg)
    del i
    return z


def _filter_body(z_ref, w1_ref, b1_ref, w2_ref, b2_ref, w3_ref, d_ref, hf_ref, hb_ref, s_ref):
    i = pl.program_id(0)
    z = z_ref[...]
    h = jnp.sin(jnp.dot(z, w1_ref[...], precision=HI, preferred_element_type=F32) + b1_ref[...])
    h = jnp.sin(jnp.dot(h, w2_ref[...], precision=HI, preferred_element_type=F32) + b2_ref[...])
    h = jnp.dot(h, w3_ref[...], precision=HI, preferred_element_type=F32)
    h = h * jnp.exp(-z[:, 0:1] * jnp.abs(d_ref[...]))
    for o in range(2):
        c0 = 2 * D_GROUP * o
        hf_ref[:, o * D_GROUP:(o + 1) * D_GROUP] = h[:, c0:c0 + D_GROUP]
        hb_ref[:, o * D_GROUP:(o + 1) * D_GROUP] = h[:, c0 + D_GROUP:c0 + 2 * D_GROUP]

    @pl.when(i == 0)
    def _():
        s_ref[...] = jnp.zeros_like(s_ref)

    s_ref[...] += jnp.sum(jnp.abs(h), axis=0, keepdims=True)


def _hy_filter(n, w1, b1, w2, b2, w3, deltas):
    tm = min(n, 512)
    z = jnp.asarray(_filter_features(n))
    w1p = jnp.zeros((64, HY_FFN), F32).at[:HY_EMB].set(w1)
    full = lambda shape: pl.BlockSpec(shape, lambda i: (0, 0))
    return pl.pallas_call(
        _filter_body,
        out_shape=[jax.ShapeDtypeStruct((n, 2 * D_GROUP), F32), jax.ShapeDtypeStruct((n, 2 * D_GROUP), F32),
                   jax.ShapeDtypeStruct((1, HY_FILTER_CH), F32)],
        grid=(n // tm,),
        in_specs=[pl.BlockSpec((tm, 64), lambda i: (i, 0)), full((64, HY_FFN)), full((1, HY_FFN)),
                  full((HY_FFN, HY_FFN)), full((1, HY_FFN)), full((HY_FFN, HY_FILTER_CH)),
                  full((1, HY_FILTER_CH))],
        out_specs=[pl.BlockSpec((tm, 2 * D_GROUP), lambda i: (i, 0)), pl.BlockSpec((tm, 2 * D_GROUP), lambda i: (i, 0)),
                   full((1, HY_FILTER_CH))],
        compiler_params=_cparams(("arbitrary",)),
        name="hyena_filter_mlp",
    )(z, w1p, b1.reshape(1, -1), w2, b2.reshape(1, -1), w3, deltas.reshape(1, -1))


@functools.lru_cache(maxsize=None)
def _dft_tables(n1):
    nn = n1 * DFT2
    j = np.arange(DFT2)
    th = 2.0 * np.pi * ((np.outer(j, j)) % DFT2) / DFT2
    fr, fi = np.cos(th), -np.sin(th)
    m_fwd = np.block([[fr, -fi], [fi, fr]])
    m_inv = np.block([[fr, fi], [-fi, fr]])
    hh = n1 // 2
    k1 = np.arange(n1)[None, :, None]
    t1 = np.arange(hh)[None, None, :]
    t2 = np.arange(DFT2)[:, None, None]
    ph = 2.0 * np.pi * ((k1 * (DFT2 * t1 + t2)) % nn) / nn
    ar, ai = np.cos(ph), -np.sin(ph)
    a_fwd = np.concatenate([np.concatenate([ar, -ai], axis=2), np.concatenate([ai, ar], axis=2)], axis=1)
    a_real = np.concatenate([ar, ai], axis=1)
    t1r = np.where(t2 >= 1, n1 - 1 - t1, (n1 - t1) % n1)
    phr = 2.0 * np.pi * ((k1 * (DFT2 * t1r + t2)) % nn) / nn
    a_rev = np.concatenate([np.cos(phr), -np.sin(phr)], axis=1)
    a_rev[0, :, 0] = 0.0
    pht = np.transpose(ph, (0, 2, 1))
    cr, ci = np.cos(pht) / nn, np.sin(pht) / nn
    a_inv = np.concatenate([np.concatenate([cr, -ci], axis=2), np.concatenate([ci, cr], axis=2)], axis=1)
    f32 = lambda a: np.ascontiguousarray(a, dtype=np.float32)
    return dict(m_fwd=f32(m_fwd), m_inv=f32(m_inv), a_fwd=f32(a_fwd), a_real=f32(a_real), a_rev=f32(a_rev),
                a_inv=f32(a_inv))


def _fft_a_body(x_ref, m_ref, o_ref, *, tj):
    n1 = o_ref.shape[1]
    for e in range(tj):
        x = jnp.concatenate([x_ref[0, :, e, :], x_ref[1, :, e, :]], axis=0).astype(BF16)
        a = jnp.dot(m_ref[e], x, preferred_element_type=F32)
        o_ref[0, :, e, :] = a[:n1]
        o_ref[1, :, e, :] = a[n1:]


def _fft_a(x4, m, *, n1, tj):
    c = x4.shape[3]
    return pl.pallas_call(
        functools.partial(_fft_a_body, tj=tj),
        out_shape=jax.ShapeDtypeStruct((2, n1, DFT2, c), F32),
        grid=(DFT2 // tj,),
        in_specs=[pl.BlockSpec((2, n1 // 2, tj, c), lambda j: (0, 0, j, 0)),
                  pl.BlockSpec((tj, 2 * n1, n1), lambda j: (j, 0, 0))],
        out_specs=pl.BlockSpec((2, n1, tj, c), lambda j: (0, 0, j, 0)),
        compiler_params=_cparams(("parallel",)),
        name="fft_stage_a",
    )(x4, m)


def _fft_b_body(a_ref, mf_ref, mi_ref, g_ref, o_ref, *, kg):
    for kk in range(kg):
        x = jnp.concatenate([a_ref[0, kk], a_ref[1, kk]], axis=0).astype(BF16)
        xf = jnp.dot(mf_ref[...], x, preferred_element_type=F32)
        xr, xi = xf[:DFT2], xf[DFT2:]
        gr, gi = g_ref[0, kk].astype(F32), g_ref[1, kk].astype(F32)
        y = jnp.concatenate([xr * gr - xi * gi, xr * gi + xi * gr], axis=0).astype(BF16)
        bf = jnp.dot(mi_ref[...], y, preferred_element_type=F32)
        o_ref[0, kk] = bf[:DFT2]
        o_ref[1, kk] = bf[DFT2:]


def _fft_b(a, g, order, tabs, *, n1, kg):
    c = a.shape[3]
    blk = pl.BlockSpec((2, kg, DFT2, c), lambda k: (0, k, 0, 0))
    mat = pl.BlockSpec((2 * DFT2, 2 * DFT2), lambda k: (0, 0))
    return pl.pallas_call(
        functools.partial(_fft_b_body, kg=kg),
        out_shape=jax.ShapeDtypeStruct(a.shape, F32),
        grid=(n1 // kg,),
        in_specs=[blk, mat, mat, pl.BlockSpec((2, kg, DFT2, c), lambda k: (0, k, 0, order))],
        out_specs=blk,
        compiler_params=_cparams(("parallel",)),
        name="fft_stage_b",
    )(a, tabs["m_fwd"], tabs["m_inv"], g)


def _fft_bf_body(a_ref, mf_ref, o_ref, *, kg):
    for kk in range(kg):
        x = jnp.concatenate([a_ref[0, kk], a_ref[1, kk]], axis=0).astype(BF16)
        xf = jnp.dot(mf_ref[...], x, preferred_element_type=F32)
        o_ref[0, kk] = xf[:DFT2].astype(BF16)
        o_ref[1, kk] = xf[DFT2:].astype(BF16)


def _fft_b_forward(a, tabs, *, n1, kg):
    c = a.shape[3]
    blk = pl.BlockSpec((2, kg, DFT2, c), lambda k: (0, k, 0, 0))
    return pl.pallas_call(
        functools.partial(_fft_bf_body, kg=kg),
        out_shape=jax.ShapeDtypeStruct(a.shape, BF16),
        grid=(n1 // kg,),
        in_specs=[blk, pl.BlockSpec((2 * DFT2, 2 * DFT2), lambda k: (0, 0))],
        out_specs=blk,
        compiler_params=_cparams(("parallel",)),
        name="fft_stage_b_filter",
    )(a, tabs["m_fwd"])


def _fft_ai_body(b_ref, m_ref, v_ref, x_ref, bias_ref, o_ref, *, tj):
    h = o_ref.shape[1]
    for e in range(tj):
        b = jnp.concatenate([b_ref[0, :, e, :], b_ref[1, :, e, :]], axis=0).astype(BF16)
        y = jnp.dot(m_ref[e], b, preferred_element_type=F32)
        o_ref[0, :, e, :] = y[:h]
        o_ref[1, :, e, :] = y[h:]
    o_ref[...] = x_ref[...] * (o_ref[...] + v_ref[...] * bias_ref[...])


def _fft_a_inv(b, m, v4, xm4, bias, *, n1, tj):
    c = b.shape[3]
    half = pl.BlockSpec((2, n1 // 2, tj, c), lambda j: (0, 0, j, 0))
    return pl.pallas_call(
        functools.partial(_fft_ai_body, tj=tj),
        out_shape=jax.ShapeDtypeStruct((2, n1 // 2, DFT2, c), F32),
        grid=(DFT2 // tj,),
        in_specs=[pl.BlockSpec((2, n1, tj, c), lambda j: (0, 0, j, 0)),
                  pl.BlockSpec((tj, n1, 2 * n1), lambda j: (j, 0, 0)),
                  half, half, pl.BlockSpec((1, c), lambda j: (0, 0))],
        out_specs=half,
        compiler_params=_cparams(("parallel",)),
        name="fft_stage_a_inv",
    )(b, m, v4, xm4, bias)


def _filt_a_body(hj_ref, hz_ref, hr_ref, s_ref, mf_ref, mr_ref, o_ref, *, tj):
    n1 = o_ref.shape[1]
    s = s_ref[...]
    for e in range(tj):
        src = hz_ref if e == 0 else hr_ref
        me = 0 if e == 0 else tj - e
        for o in range(2):
            c0 = 2 * D_GROUP * o
            inv = 1.0 / (s[:, c0:c0 + D_GROUP] + s[:, c0 + D_GROUP:c0 + 2 * D_GROUP] + EPS)
            lsl = slice(o * D_GROUP, (o + 1) * D_GROUP)
            hf = hj_ref[:, e, lsl].astype(BF16)
            hb = src[:, me, lsl].astype(BF16)
            a = (jnp.dot(mf_ref[e], hf, preferred_element_type=F32)
                 + jnp.dot(mr_ref[e], hb, preferred_element_type=F32)) * inv
            o_ref[0, :, e, lsl] = a[:n1]
            o_ref[1, :, e, lsl] = a[n1:]


def _filt_a(hf, hb, colsum, tabs, *, n1):
    tj = 8
    nj = DFT2 // tj
    wd = 2 * D_GROUP
    shp = (n1 // 2, DFT2, wd)
    blk = lambda fn: pl.BlockSpec((n1 // 2, tj, wd), fn)
    mat = pl.BlockSpec((tj, 2 * n1, n1 // 2), lambda j: (j, 0, 0))
    return pl.pallas_call(
        functools.partial(_filt_a_body, tj=tj),
        out_shape=jax.ShapeDtypeStruct((2, n1, DFT2, wd), F32),
        grid=(nj,),
        in_specs=[blk(lambda j: (0, j, 0)), blk(lambda j: (0, (nj - j) % nj, 0)), blk(lambda j: (0, nj - 1 - j, 0)),
                  pl.BlockSpec((1, HY_FILTER_CH), lambda j: (0, 0)), mat, mat],
        out_specs=pl.BlockSpec((2, n1, tj, wd), lambda j: (0, 0, j, 0)),
        compiler_params=_cparams(("parallel",)),
        name="filter_stage_a",
    )(hf.reshape(shp), hb.reshape(shp), hb.reshape(shp), colsum, tabs["a_real"], tabs["a_rev"])


def _hyena_long(v, x1, x2, hf, hb, colsum, bias, *, seq):
    n1 = 2 * seq // DFT2
    tabs = {k: jnp.asarray(a).astype(BF16) for k, a in _dft_tables(n1).items()}
    tj = 8
    kg = 4
    fa = _filt_a(hf, hb, colsum, tabs, n1=n1)
    g = _fft_b_forward(fa, tabs, n1=n1, kg=kg // 2)
    shp = (2, n1 // 2, DFT2, D_GROUP)
    z = v.reshape(shp)
    for order, xm in ((0, x1.reshape(shp)), (1, x2.reshape(shp))):
        a = _fft_a(z, tabs["a_fwd"], n1=n1, tj=tj)
        b = _fft_b(a, g, order, tabs, n1=n1, kg=kg)
        z = _fft_a_inv(b, tabs["a_inv"], z, xm, bias[order].reshape(1, D_GROUP), n1=n1, tj=tj)
    return z.reshape(2 * seq, D_GROUP)


def _hyena_ctx_body(v_ref, x1_ref, x2_ref, hf_ref, hb_ref, s_ref, bias_ref, cm_ref, sm_ref, ct_ref, st_ref, o_ref):
    n = v_ref.shape[1]
    cm, sm, ct, st = cm_ref[...], sm_ref[...], ct_ref[...], st_ref[...]
    s = s_ref[...]
    row0 = _iota((n, D_GROUP), 0) == 0
    dot = lambda a, b: jnp.dot(a, b.astype(BF16), preferred_element_type=F32)
    zr, zi = v_ref[0], v_ref[1]
    for order, xm in ((0, x1_ref), (1, x2_ref)):
        c0 = 2 * D_GROUP * order
        inv = 1.0 / (s[:, c0:c0 + D_GROUP] + s[:, c0 + D_GROUP:c0 + 2 * D_GROUP] + EPS)
        lsl = slice(order * D_GROUP, (order + 1) * D_GROUP)
        hf = hf_ref[:, lsl] * inv
        hb = jnp.where(row0, 0.0, hb_ref[:, lsl] * inv)
        gr = dot(cm, hf + hb)
        gi = dot(sm, hb - hf)
        xr = dot(cm, zr) + dot(sm, zi)
        xi = dot(cm, zi) - dot(sm, zr)
        yr = xr * gr - xi * gi
        yi = xr * gi + xi * gr
        cr = dot(ct, yr) - dot(st, yi)
        ci = dot(ct, yi) + dot(st, yr)
        bias = bias_ref[order:order + 1, :]
        zr = xm[0] * (cr + zr * bias)
        zi = xm[1] * (ci + zi * bias)
    o_ref[0] = zr
    o_ref[1] = zi


def _hyena_ctx(v, x1, x2, hf, hb, colsum, bias, *, seq):
    nn = 2 * seq
    k = np.arange(nn)
    t = np.arange(seq)
    th = 2.0 * np.pi * (np.outer(k, t) % nn) / nn
    cm, sm = np.cos(th), np.sin(th)
    consts = [jnp.asarray(a, F32).astype(BF16) for a in (cm, sm, cm.T / nn, sm.T / nn)]
    shp = (2, seq, D_GROUP)
    full3 = pl.BlockSpec(shp, lambda i: (0, 0, 0))
    f2 = lambda a: pl.BlockSpec(a.shape, lambda i: (0, 0))
    args = [hf, hb, colsum, bias] + consts
    out = pl.pallas_call(
        _hyena_ctx_body,
        out_shape=jax.ShapeDtypeStruct(shp, F32),
        grid=(1,),
        in_specs=[full3, full3, full3] + [f2(a) for a in args],
        out_specs=full3,
        compiler_params=_cparams(("arbitrary",)),
        name="hyena_ctx",
    )(v.reshape(shp), x1.reshape(shp), x2.reshape(shp), *args)
    return out.reshape(2 * seq, D_GROUP)


def _out_body(x_ref, m_ref, p_ref, h_ref, g_ref, c_ref, w_ref, o_ref):
    acc = jnp.dot(p_ref[...].astype(BF16), w_ref[0:D_GROUP, :], preferred_element_type=F32)
    acc += jnp.dot(h_ref[...].astype(BF16), w_ref[D_GROUP:2 * D_GROUP, :], preferred_element_type=F32)
    acc += jnp.dot(g_ref[...].astype(BF16), w_ref[2 * D_GROUP:3 * D_GROUP, :], preferred_element_type=F32)
    acc += jnp.dot(c_ref[...].astype(BF16), w_ref[3 * D_GROUP:, :], preferred_element_type=F32)
    o_ref[...] = x_ref[...] + m_ref[0][2:3, :] * acc


def _out_proj(x, m, parts, w_out, *, tm, tiles_per_seq):
    n, d = x.shape
    pspec = pl.BlockSpec((tm, D_GROUP), lambda i: (i, 0))
    return pl.pallas_call(
        _out_body,
        out_shape=jax.ShapeDtypeStruct((n, d), F32),
        grid=(n // tm,),
        in_specs=[pl.BlockSpec((tm, d), lambda i: (i, 0)),
                  pl.BlockSpec((1, 8, d), lambda i: (i // tiles_per_seq, 0, 0)),
                  pspec, pspec, pspec, pspec, pl.BlockSpec((d, d), lambda i: (0, 0))],
        out_specs=pl.BlockSpec((tm, d), lambda i: (i, 0)),
        compiler_params=_cparams(("parallel",)),
        name="out_proj",
    )(x, m, *parts, w_out)


def _mod_rows(mod_l, rows, first, count):
    m = mod_l[rows, first:first + count, :]
    return jnp.pad(m, ((0, 0), (0, 8 - count), (0, 0)))


def _split_w_in(w_in):
    wb = w_in.astype(BF16)
    parts = dict(k=wb[:, COL_K:COL_V], v=wb[:, COL_V:COL_GF], q=wb[:, COL_Q:COL_R], r=wb[:, COL_R:COL_POOL],
                 pool=wb[:, COL_POOL:COL_HY], hy=wb[:, COL_HY:COL_CONV], conv=wb[:, COL_CONV:P_IN])
    gate = jnp.pad(wb[:, COL_GF:COL_Q], ((0, 0), (0, LANE - 2 * GLA_LOWRANK)))
    return parts, gate


def _mix(u, p, *, bsz, seq, is_ctx, states, filt):
    sf, sb = states
    cpb = min(8, seq // CHUNK)
    if is_ctx:
        pool = _pool1d(u["pool"], p["pool_wbd"], p["pool_scale"], bsz=bsz, seq=seq)
    else:
        pool = _pool2d(u["pool"], p["pool_wbd"], p["pool_scale"], bsz=bsz, seq=seq)
    tmc = min(seq, 1024)
    v, x1, x2 = _hy_short(u["hy"], p["hy_short_w"], p["hy_short_b"], seq=seq, tm=tmc)
    if is_ctx:
        hy = _hyena_ctx(v, x1, x2, *filt, p["hy_bias"], seq=seq)
    else:
        hy = _hyena_long(v, x1, x2, *filt, p["hy_bias"], seq=seq)
    gla = _gla_read(u["k"], u["v"], u["q"], u["r"], u["lf"], u["lb"], sf, sb, p["gla_ng"],
                    bsz=bsz, seq=seq, cpb=cpb)
    conv = _conformer(u["conv"], p["conv_dw_w"], p["conv_dw_b"], p["conv_ln_g"], p["conv_ln_b"], seq=seq, tm=tmc)
    return [pool, hy, gla, conv]


def kernel(x, c, ctx, c_ctx, ada_w, ada_b, ffn1_norm, ffn1_wi, ffn1_wo, mix_norm, w_in, w_out, pool_w, pool_scale, hy_short_w, hy_short_b, hy_w1, hy_b1, hy_w2, hy_b2, hy_w3, hy_deltas, hy_bias, gla_gw_f, gla_gb_f, gla_gw_b, gla_gb_b, gla_norm, conv_dw_w, conv_dw_b, conv_ln_g, conv_ln_b, ffn2_norm, ffn2_wi, ffn2_wo, final_norm):
    bsz, seq, d = x.shape
    clen = ctx.shape[1]
    depth = ada_w.shape[0]
    assert bsz == 2, "the Hyena transform packs exactly two batch rows into one complex signal"
    xs = x.reshape(bsz * seq, d)
    cs = ctx.reshape(bsz * clen, d)
    cc = jnp.zeros((8, d), F32).at[0:bsz].set(c).at[bsz].set(c_ctx)
    mod = _modulation(cc, ada_w, ada_b).reshape(depth, 8, N_MOD, d)
    xrows = np.arange(bsz)
    crows = np.full((1,), bsz)
    tmx = 512
    tps_x = seq // tmx
    tmc = bsz * clen
    zero_state = jnp.zeros((bsz, GLA_HEADS * GLA_DV, GLA_QK), F32)

    for l in range(depth):
        last = l == depth - 1
        ml = mod[l]
        wparts, wgate = _split_w_in(w_in[l])
        gw = jnp.zeros((LANE, 2 * GLA_QK), F32)
        gw = gw.at[0:GLA_LOWRANK, 0:GLA_QK].set(gla_gw_f[l])
        gw = gw.at[GLA_LOWRANK:2 * GLA_LOWRANK, GLA_QK:].set(gla_gw_b[l])
        gb = jnp.concatenate([gla_gb_f[l], gla_gb_b[l]]).reshape(1, -1)
        wbd = jnp.zeros((D_GROUP, D_GROUP), F32)
        for gi in range(len(POOL_WINDOWS)):
            wbd = wbd.at[gi * POOL_CH:(gi + 1) * POOL_CH, gi * POOL_CH:(gi + 1) * POOL_CH].set(pool_w[l, gi])
        p = dict(pool_wbd=wbd.astype(BF16), pool_scale=pool_scale[l].reshape(1, -1),
                 hy_short_w=hy_short_w[l], hy_short_b=hy_short_b[l], hy_bias=hy_bias[l],
                 gla_ng=jnp.tile(gla_norm[l], GLA_HEADS).reshape(1, -1),
                 conv_dw_w=conv_dw_w[l], conv_dw_b=conv_dw_b[l], conv_ln_g=conv_ln_g[l], conv_ln_b=conv_ln_b[l])
        wi1, wo1 = ffn1_wi[l].astype(BF16), ffn1_wo[l].astype(BF16)
        wi2, wo2 = ffn2_wi[l].astype(BF16), ffn2_wo[l].astype(BF16)
        wout = w_out[l].astype(BF16)
        names = ["k", "v", "q", "r", "pool", "hy", "conv"]

        xs = _ffn(xs, _mod_rows(ml, xrows, 0, 3), ffn1_norm[l], wi1, wo1, tm=tmx, tiles_per_seq=tps_x)
        cs = _ffn(cs, _mod_rows(ml, crows, 0, 3), ffn1_norm[l], wi1, wo1, tm=tmc, tiles_per_seq=1)

        mx = _mod_rows(ml, xrows, 3, 3)
        mc = _mod_rows(ml, crows, 3, 3)
        outs = _proj(xs, mx, mix_norm[l], [wparts[nm] for nm in names], wgate, gw, gb, tm=tmx, tiles_per_seq=tps_x)
        ux = dict(zip(names + ["lf", "lb"], outs))
        cnames = ["k", "v"] if last else names
        outs = _proj(cs, mc, mix_norm[l], [wparts[nm] for nm in cnames], wgate, gw, gb, tm=tmc, tiles_per_seq=1)
        uc = dict(zip(cnames + ["lf", "lb"], outs))

        ccpb = clen // CHUNK
        sfc, sbc, finf, finb = _gla_states(uc["k"], uc["v"], uc["lf"], uc["lb"], zero_state, zero_state,
                                           bsz=bsz, seq=clen, cpb=ccpb)
        sfx, sbx, _, _ = _gla_states(ux["k"], ux["v"], ux["lf"], ux["lb"], finf, finb, bsz=bsz, seq=seq, cpb=8)

        filt = _hy_filter(seq, hy_w1[l], hy_b1[l], hy_w2[l], hy_b2[l], hy_w3[l], hy_deltas[l])
        mixed = _mix(ux, p, bsz=bsz, seq=seq, is_ctx=False, states=(sfx, sbx), filt=filt)
        xs = _out_proj(xs, mx, mixed, wout, tm=tmx, tiles_per_seq=tps_x)
        xs = _ffn(xs, _mod_rows(ml, xrows, 6, 3), ffn2_norm[l], wi2, wo2, tm=tmx, tiles_per_seq=tps_x,
                  final_g=final_norm if last else None)
        if not last:
            filt_c = _hy_filter(clen, hy_w1[l], hy_b1[l], hy_w2[l], hy_b2[l], hy_w3[l], hy_deltas[l])
            mixed = _mix(uc, p, bsz=bsz, seq=clen, is_ctx=True, states=(sfc, sbc), filt=filt_c)
            cs = _out_proj(cs, mc, mixed, wout, tm=tmc, tiles_per_seq=1)
            cs = _ffn(cs, _mod_rows(ml, crows, 6, 3), ffn2_norm[l], wi2, wo2, tm=tmc, tiles_per_seq=1)
    return xs.reshape(bsz, seq, d)
```

```python
import functools
import math

import numpy as np
import jax
import jax.numpy as jnp
from jax import lax
from jax.experimental import pallas as pl
from jax.experimental.pallas import tpu as pltpu

F32 = jnp.float32
BF16 = jnp.bfloat16
HI = lax.Precision.HIGHEST

D_MODEL = 1024
GRID_W = 64
D_GROUP = 256
D_FF = 2816
N_MOD = 9
EPS = 1e-6
POOL_WINDOWS = (2, 4, 8, 16)
POOL_CH = 64
HY_BANDS = 16
HY_EMB = 1 + 2 * HY_BANDS
HY_FFN = 64
HY_FILTER_CH = 4 * D_GROUP
GLA_HEADS = 4
GLA_DK = 32
GLA_DV = 64
GLA_QK = 128
GLA_LOWRANK = 16
GLA_TAU = 16.0
CHUNK = 64
CONV_WIDTH = 31
HY_SHORT = 3

COL_K = 0
COL_V = COL_K + GLA_QK
COL_GF = COL_V + D_GROUP
COL_GB = COL_GF + GLA_LOWRANK
COL_Q = COL_GB + GLA_LOWRANK
COL_R = COL_Q + GLA_QK
COL_POOL = COL_R + D_GROUP
COL_HY = COL_POOL + D_GROUP
COL_CONV = COL_HY + 3 * D_GROUP
P_IN = COL_CONV + 2 * D_GROUP

LANE = 128
DFT2 = 128
MIB = 1024 * 1024


def _cparams(sem, vmem_mib=None):
    kw = dict(dimension_semantics=sem)
    if vmem_mib is not None:
        kw["vmem_limit_bytes"] = vmem_mib * MIB
    return pltpu.CompilerParams(**kw)


def _silu(x):
    return x * jax.nn.sigmoid(x)


def _rms_mod(h, g, m):
    y = h * lax.rsqrt(jnp.mean(h * h, axis=-1, keepdims=True) + EPS) * g
    return y * (1.0 + m[1:2, :]) + m[0:1, :]


def _iota(shape, dim):
    return lax.broadcasted_iota(jnp.int32, shape, dim)


def _mod_body(c_ref, w_ref, b_ref, o_ref):
    a = _silu(c_ref[...])
    o_ref[0] = jnp.dot(a, w_ref[0], precision=HI, preferred_element_type=F32) + b_ref[0]


def _modulation(cc, ada_w, ada_b):
    nl, d, nm = ada_w.shape
    tn = 1024
    return pl.pallas_call(
        _mod_body,
        out_shape=jax.ShapeDtypeStruct((nl, 8, nm), F32),
        grid=(nl, nm // tn),
        in_specs=[pl.BlockSpec((8, d), lambda l, j: (0, 0)),
                  pl.BlockSpec((1, d, tn), lambda l, j: (l, 0, j)),
                  pl.BlockSpec((1, 1, tn), lambda l, j: (l, 0, j))],
        out_specs=pl.BlockSpec((1, 8, tn), lambda l, j: (l, 0, j)),
        compiler_params=_cparams(("parallel", "parallel")),
        name="adaln_mod",
    )(cc, ada_w, ada_b.reshape(nl, 1, nm))


FF_CHUNK = 256
ROW_SUB = 256


def _swiglu(xn, wi_ref, wo_ref, hm_ref, rows):
    ff = wo_ref.shape[0]
    for c in range(0, ff, FF_CHUNK):
        a = jnp.dot(xn, wi_ref[:, c:c + FF_CHUNK], preferred_element_type=F32)
        g = jnp.dot(xn, wi_ref[:, ff + c:ff + c + FF_CHUNK], preferred_element_type=F32)
        hm_ref[rows, c:c + FF_CHUNK] = (_silu(g) * a).astype(BF16)
    return jnp.dot(hm_ref[rows, :], wo_ref[...], preferred_element_type=F32)


def _resident(shape):
    nd = len(shape)
    return pl.BlockSpec(shape, lambda i: (0,) * nd, pipeline_mode=pl.Buffered(1))


def _gla_log_decay(xn, wg_ref, gw_ref, gb_ref):
    ug = jnp.dot(xn, wg_ref[...], preferred_element_type=F32)
    a = jnp.dot(ug, gw_ref[...], precision=HI, preferred_element_type=F32) + gb_ref[...]
    return (jnp.minimum(a, 0.0) - jnp.log(1.0 + jnp.exp(-jnp.abs(a)))) * (1.0 / GLA_TAU)


def _ffn_proj_body(h_ref, m_ref, g1_ref, wi_ref, wo_ref, g2_ref, *rest, nparts):
    w_refs = rest[:nparts]
    wg_ref, gw_ref, gb_ref = rest[nparts:nparts + 3]
    x_ref = rest[nparts + 3]
    o_refs = rest[nparts + 4:2 * nparts + 6]
    hm_ref = rest[-1]
    m = m_ref[0]
    for r0 in range(0, h_ref.shape[0], ROW_SUB):
        rows = slice(r0, r0 + ROW_SUB)
        h = h_ref[rows, :]
        xn = _rms_mod(h, g1_ref[...], m[0:2]).astype(BF16)
        x1 = h + (0.5 * m[2:3, :]) * _swiglu(xn, wi_ref, wo_ref, hm_ref, rows)
        x_ref[rows, :] = x1
        xn2 = _rms_mod(x1, g2_ref[...], m[3:5]).astype(BF16)
        for w_ref, o_ref in zip(w_refs, o_refs[:nparts]):
            o_ref[rows, :] = jnp.dot(xn2, w_ref[...], preferred_element_type=F32).astype(o_ref.dtype)
        ls = _gla_log_decay(xn2, wg_ref, gw_ref, gb_ref)
        o_refs[nparts][rows, :] = ls[:, :GLA_QK]
        o_refs[nparts + 1][rows, :] = ls[:, GLA_QK:]


def _ffn_proj(h, m, g1, wi, wo, g2, w_parts, w_gate, gw, gb, *, tm, tiles_per_seq):
    n, d = h.shape
    ff = wo.shape[0]
    nparts = len(w_parts)
    widths = [w.shape[1] for w in w_parts]
    tok = lambda wd: pl.BlockSpec((tm, wd), lambda i: (i, 0))
    in_specs = [tok(d), pl.BlockSpec((1, 8, d), lambda i: (i // tiles_per_seq, 0, 0)), _resident((1, d)),
                _resident((d, 2 * ff)), _resident((ff, d)), _resident((1, d))]
    in_specs += [_resident((d, wd)) for wd in widths]
    in_specs += [_resident((d, LANE)), _resident((LANE, 2 * GLA_QK)), _resident((1, 2 * GLA_QK))]
    out_shape = [jax.ShapeDtypeStruct((n, d), F32)]
    out_shape += [jax.ShapeDtypeStruct((n, wd), BF16) for wd in widths]
    out_shape += [jax.ShapeDtypeStruct((n, GLA_QK), F32)] * 2
    return pl.pallas_call(
        functools.partial(_ffn_proj_body, nparts=nparts),
        out_shape=out_shape,
        grid=(n // tm,),
        in_specs=in_specs,
        out_specs=[tok(d)] + [tok(wd) for wd in widths] + [tok(GLA_QK)] * 2,
        scratch_shapes=[pltpu.VMEM((tm, ff), BF16)],
        compiler_params=_cparams(("parallel",), 56),
        name="ffn1_in_proj",
    )(h, m, g1.reshape(1, d), wi, wo, g2.reshape(1, d), *w_parts, w_gate, gw, gb)


def _out_ffn_body(x_ref, m_ref, p_ref, hy_ref, gl_ref, cv_ref, wout_ref, g_ref, wi_ref, wo_ref, fn_ref, o_ref,
                  hm_ref, *, final):
    m = m_ref[0]
    for r0 in range(0, x_ref.shape[0], ROW_SUB):
        rows = slice(r0, r0 + ROW_SUB)
        acc = jnp.dot(p_ref[rows, :].astype(BF16), wout_ref[0:D_GROUP, :], preferred_element_type=F32)
        acc += jnp.dot(hy_ref[rows, :].astype(BF16), wout_ref[D_GROUP:2 * D_GROUP, :], preferred_element_type=F32)
        acc += jnp.dot(gl_ref[rows, :].astype(BF16), wout_ref[2 * D_GROUP:3 * D_GROUP, :],
                       preferred_element_type=F32)
        acc += jnp.dot(cv_ref[rows, :].astype(BF16), wout_ref[3 * D_GROUP:, :], preferred_element_type=F32)
        x2 = x_ref[rows, :] + m[0:1, :] * acc
        xn = _rms_mod(x2, g_ref[...], m[1:3]).astype(BF16)
        out = x2 + (0.5 * m[3:4, :]) * _swiglu(xn, wi_ref, wo_ref, hm_ref, rows)
        if final:
            out = out * lax.rsqrt(jnp.mean(out * out, axis=-1, keepdims=True) + EPS) * fn_ref[...]
        o_ref[rows, :] = out


def _out_ffn(x, m, parts, w_out, g, wi, wo, *, tm, tiles_per_seq, final_g=None):
    n, d = x.shape
    ff = wo.shape[0]
    final = final_g is not None
    fg = final_g if final else g
    tok = lambda wd: pl.BlockSpec((tm, wd), lambda i: (i, 0))
    return pl.pallas_call(
        functools.partial(_out_ffn_body, final=final),
        out_shape=jax.ShapeDtypeStruct((n, d), F32),
        grid=(n // tm,),
        in_specs=[tok(d), pl.BlockSpec((1, 8, d), lambda i: (i // tiles_per_seq, 0, 0)),
                  tok(D_GROUP), tok(D_GROUP), tok(D_GROUP), tok(D_GROUP), _resident((d, d)), _resident((1, d)),
                  _resident((d, 2 * ff)), _resident((ff, d)), _resident((1, d))],
        out_specs=tok(d),
        scratch_shapes=[pltpu.VMEM((tm, ff), BF16)],
        compiler_params=_cparams(("parallel",), 56),
        name="out_proj_ffn2",
    )(x, m, *parts, w_out, g.reshape(1, d), wi, wo, fg.reshape(1, d))


@functools.lru_cache(maxsize=None)
def _chunk_matrices(bt):
    r = np.arange(bt)
    same = (r[:, None] // CHUNK) == (r[None, :] // CHUNK)
    low = same & (r[:, None] >= r[None, :])
    up = same & (r[:, None] <= r[None, :])
    return tuple(np.ascontiguousarray(m, dtype=np.float32) for m in (low, up, same))


def _chunk_sum(mat, hi, lo):
    return jnp.dot(mat, hi, preferred_element_type=F32) + jnp.dot(mat, lo, preferred_element_type=F32)


def _gla_state_body(kf, vf, lf, kb, vb, lb, tl, tu, on, s0f, s0b, sf_o, sb_o, ff_o, fb_o, stf, stb, *, cpb):
    i = pl.program_id(1)

    @pl.when(i == 0)
    def _():
        stf[...] = s0f[0]
        stb[...] = s0b[0]

    sshape = (GLA_HEADS * GLA_DV, GLA_QK)
    bmask = (_iota(sshape, 0) >> 6) == (_iota(sshape, 1) >> 5)
    tn_dims = (((0,), (0,)), ((), ()))
    ones = on[...]

    def direction(k_ref, v_ref, l_ref, tri_ref, st, s_o, order):
        hi, lo = _split_bf16(l_ref[...])
        b = _chunk_sum(tri_ref[...], hi, lo)
        tot = _chunk_sum(ones, hi, lo)
        kd = (k_ref[...] * jnp.exp(tot - b)).astype(BF16)
        dec = jnp.exp(tot)
        vb16 = v_ref[...].astype(BF16)
        s = st[...]
        for ci in order:
            sl = slice(ci * CHUNK, (ci + 1) * CHUNK)
            upd = lax.dot_general(vb16[sl], kd[sl], tn_dims, preferred_element_type=F32)
            s_o[0, ci] = s.astype(BF16)
            s = s * dec[ci * CHUNK:ci * CHUNK + 1, :] + jnp.where(bmask, upd, 0.0)
        st[...] = s

    direction(kf, vf, lf, tl, stf, sf_o, range(cpb))
    direction(kb, vb, lb, tu, stb, sb_o, range(cpb - 1, -1, -1))
    ff_o[0] = stf[...]
    fb_o[0] = stb[...]


def _gla_states(k, v, lf, lb, s0f, s0b, *, bsz, seq, cpb):
    bt = cpb * CHUNK
    nb = seq // bt
    nc = seq // CHUNK
    srow = GLA_HEADS * GLA_DV

    def tf(b, i):
        return (b * nb + i, 0)

    def tb(b, i):
        return (b * nb + nb - 1 - i, 0)

    sblk = pl.BlockSpec((1, srow, GLA_QK), lambda b, i: (b, 0, 0))
    cmat = pl.BlockSpec((bt, bt), lambda b, i: (0, 0))
    tl, tu, on = (jnp.asarray(m).astype(BF16) for m in _chunk_matrices(bt))
    return pl.pallas_call(
        functools.partial(_gla_state_body, cpb=cpb),
        out_shape=[jax.ShapeDtypeStruct((bsz, nc, srow, GLA_QK), BF16),
                   jax.ShapeDtypeStruct((bsz, nc, srow, GLA_QK), BF16),
                   jax.ShapeDtypeStruct((bsz, srow, GLA_QK), F32),
                   jax.ShapeDtypeStruct((bsz, srow, GLA_QK), F32)],
        grid=(bsz, nb),
        in_specs=[pl.BlockSpec((bt, GLA_QK), tf), pl.BlockSpec((bt, D_GROUP), tf), pl.BlockSpec((bt, GLA_QK), tf),
                  pl.BlockSpec((bt, GLA_QK), tb), pl.BlockSpec((bt, D_GROUP), tb), pl.BlockSpec((bt, GLA_QK), tb),
                  cmat, cmat, cmat, sblk, sblk],
        out_specs=[pl.BlockSpec((1, cpb, srow, GLA_QK), lambda b, i: (b, i, 0, 0)),
                   pl.BlockSpec((1, cpb, srow, GLA_QK), lambda b, i: (b, nb - 1 - i, 0, 0)),
                   sblk, sblk],
        scratch_shapes=[pltpu.VMEM((srow, GLA_QK), F32), pltpu.VMEM((srow, GLA_QK), F32)],
        compiler_params=_cparams(("parallel", "arbitrary")),
        name="gla_states",
    )(k, v, lf, k, v, lb, tl, tu, on, s0f, s0b)


def _gla_read_body(k_ref, v_ref, q_ref, r_ref, lf_ref, lb_ref, tl, tu, sf_ref, sb_ref, ng_ref, o_ref, *, cpb):
    hrows = GLA_HEADS * CHUNK
    cpos = _iota((hrows, CHUNK), 0) & (CHUNK - 1)
    ccol = _iota((hrows, CHUNK), 1)
    lowm = cpos >= ccol
    upm = cpos <= ccol
    hq = (_iota((hrows, GLA_QK), 0) >> 6) == (_iota((hrows, GLA_QK), 1) >> 5)
    ho = (_iota((hrows, D_GROUP), 0) >> 6) == (_iota((hrows, D_GROUP), 1) >> 6)
    bavg = jnp.where((_iota((D_GROUP, D_GROUP), 0) >> 6) == (_iota((D_GROUP, D_GROUP), 1) >> 6),
                     1.0 / GLA_DV, 0.0).astype(BF16)
    nt_dims = (((1,), (1,)), ((), ()))

    bf_ = _chunk_sum(tl[...], *_split_bf16(lf_ref[...]))
    bb_ = _chunk_sum(tu[...], *_split_bf16(lb_ref[...]))
    qs = q_ref[...].astype(F32) * (GLA_DK ** -0.5)
    kk = k_ref[...]
    qef = (qs * jnp.exp(bf_)).astype(BF16)
    qeb = (qs * jnp.exp(bb_)).astype(BF16)
    kef = (kk * jnp.exp(-bf_)).astype(BF16)
    keb = (kk * jnp.exp(-bb_)).astype(BF16)
    vb16 = v_ref[...].astype(BF16)
    zero = jnp.zeros((), BF16)
    for ci in range(cpb):
        sl = slice(ci * CHUNK, (ci + 1) * CHUNK)
        qf4 = jnp.where(hq, jnp.concatenate([qef[sl]] * GLA_HEADS, axis=0), zero)
        qb4 = jnp.where(hq, jnp.concatenate([qeb[sl]] * GLA_HEADS, axis=0), zero)
        af = lax.dot_general(qf4, kef[sl], nt_dims, preferred_element_type=F32)
        ab = lax.dot_general(qb4, keb[sl], nt_dims, preferred_element_type=F32)
        att = (jnp.where(lowm, af, 0.0) + jnp.where(upm, ab, 0.0)).astype(BF16)
        oall = jnp.dot(att, vb16[sl], preferred_element_type=F32)
        om = jnp.where(ho, oall, 0.0)
        o = om[0:CHUNK] + om[CHUNK:2 * CHUNK] + om[2 * CHUNK:3 * CHUNK] + om[3 * CHUNK:4 * CHUNK]
        qcat = jnp.concatenate([qef[sl], qeb[sl]], axis=1)
        scat = jnp.concatenate([sf_ref[0, ci], sb_ref[0, ci]], axis=1)
        o_ref[sl, :] = o + lax.dot_general(qcat, scat, nt_dims, preferred_element_type=F32)
    o = o_ref[...]
    ms = _chunk_sum_rhs(o * o, bavg)
    o_ref[...] = o * lax.rsqrt(ms + EPS) * ng_ref[...] * _silu(r_ref[...].astype(F32))


def _chunk_sum_rhs(x, mat):
    hi, lo = _split_bf16(x)
    return jnp.dot(hi, mat, preferred_element_type=F32) + jnp.dot(lo, mat, preferred_element_type=F32)


def _gla_read(k, v, q, r, lf, lb, sf, sb, ng, *, bsz, seq, cpb):
    bt = cpb * CHUNK
    nb = seq // bt
    srow = GLA_HEADS * GLA_DV
    n = bsz * seq

    def tk(i):
        return (i, 0)

    sspec = pl.BlockSpec((1, cpb, srow, GLA_QK), lambda i: (i // nb, i % nb, 0, 0))
    cmat = pl.BlockSpec((bt, bt), lambda i: (0, 0))
    tl, tu, _ = (jnp.asarray(m).astype(BF16) for m in _chunk_matrices(bt))
    return pl.pallas_call(
        functools.partial(_gla_read_body, cpb=cpb),
        out_shape=jax.ShapeDtypeStruct((n, D_GROUP), F32),
        grid=(bsz * nb,),
        in_specs=[pl.BlockSpec((bt, GLA_QK), tk), pl.BlockSpec((bt, D_GROUP), tk),
                  pl.BlockSpec((bt, GLA_QK), tk), pl.BlockSpec((bt, D_GROUP), tk),
                  pl.BlockSpec((bt, GLA_QK), tk), pl.BlockSpec((bt, GLA_QK), tk),
                  cmat, cmat, sspec, sspec, pl.BlockSpec((1, D_GROUP), lambda i: (0, 0))],
        out_specs=pl.BlockSpec((bt, D_GROUP), tk),
        compiler_params=_cparams(("parallel",)),
        name="gla_readout",
    )(k, v, q, r, lf, lb, tl, tu, sf, sb, ng)


def _box_matrix(n, w):
    pos = np.arange(n)
    lo = np.clip(pos - w // 2, 0, n)
    hi = np.clip(pos - w // 2 + w, 0, n)
    col = np.arange(n)[None, :]
    return ((col >= lo[:, None]) & (col < hi[:, None])).astype(np.float32)


def _lane_windows(shape):
    w = jnp.left_shift(2, _iota(shape, 1) >> 6)
    return w, w >> 1


def _box_count(pos, w, half, n):
    return jnp.minimum(pos - half + w, n) - jnp.maximum(pos - half, 0)


def _split_bf16(x):
    hi = x.astype(BF16)
    lo = (x - hi.astype(F32)).astype(BF16)
    return hi, lo


def _pool2d_body(cur_ref, prev_ref, next_ref, pc_ref, w_ref, sc_ref, o_ref, ycol, *, tiles, rows):
    i = pl.program_id(1)
    tm = cur_ref.shape[0]
    hb = prev_ref.shape[0]
    sub = 2 * GRID_W
    pflag = jnp.where(i > 0, 1.0, 0.0)
    nflag = jnp.where(i < tiles - 1, 1.0, 0.0)

    def colpool(x):
        halves = []
        for half in range(2):
            lsl = slice(half * LANE, (half + 1) * LANE)
            ys = [jnp.dot(pc_ref[wi], x[:, lsl], preferred_element_type=F32) for wi in (2 * half, 2 * half + 1)]
            lane = _iota((sub, LANE), 1)
            halves.append(jnp.where(lane < POOL_CH, ys[0], ys[1]))
        return jnp.concatenate(halves, axis=1)

    for s in range(hb // sub):
        ycol[s * sub:(s + 1) * sub, :] = colpool(prev_ref[s * sub:(s + 1) * sub, :]) * pflag
    for s in range(tm // sub):
        ycol[hb + s * sub:hb + (s + 1) * sub, :] = colpool(cur_ref[s * sub:(s + 1) * sub, :])
    for s in range(hb // sub):
        ycol[hb + tm + s * sub:hb + tm + (s + 1) * sub, :] = colpool(next_ref[s * sub:(s + 1) * sub, :]) * nflag

    rc = 256
    wl, half = _lane_windows((rc, D_GROUP))
    for r0 in range(0, tm, rc):
        def rows_at(dd):
            return ycol[hb + r0 + GRID_W * dd:hb + r0 + GRID_W * dd + rc, :]

        acc = rows_at(-1) + rows_at(0)
        z2 = acc
        acc = acc + rows_at(-2) + rows_at(1)
        z4 = acc
        for dd in (-4, -3, 2, 3):
            acc = acc + rows_at(dd)
        z8 = acc
        for dd in (-8, -7, -6, -5, 4, 5, 6, 7):
            acc = acc + rows_at(dd)
        z16 = acc
        z = jnp.where(wl == 2, z2, jnp.where(wl == 4, z4, jnp.where(wl == 8, z8, z16)))
        tok = _iota((rc, D_GROUP), 0) + (i * tm + r0)
        rcnt = _box_count(tok >> 6, wl, half, rows)
        ccnt = _box_count(tok & (GRID_W - 1), wl, half, GRID_W)
        pooled = z / (rcnt * ccnt).astype(F32)
        dlt = (pooled - cur_ref[r0:r0 + rc, :]).astype(BF16)
        o_ref[r0:r0 + rc, :] = jnp.dot(dlt, w_ref[...], preferred_element_type=F32) * sc_ref[...]


def _pool2d(u, wbd, scale, *, bsz, seq):
    tm = 1024
    hb = 512
    tiles = seq // tm
    r = tm // hb
    nhb = seq // hb
    pc = np.stack([np.kron(np.eye(2, dtype=np.float32), _box_matrix(GRID_W, w)) for w in POOL_WINDOWS])
    return pl.pallas_call(
        functools.partial(_pool2d_body, tiles=tiles, rows=seq // GRID_W),
        out_shape=jax.ShapeDtypeStruct((bsz * seq, D_GROUP), F32),
        grid=(bsz, tiles),
        in_specs=[pl.BlockSpec((tm, D_GROUP), lambda b, i: (b * tiles + i, 0)),
                  pl.BlockSpec((hb, D_GROUP), lambda b, i: (b * nhb + jnp.maximum(i * r - 1, 0), 0)),
                  pl.BlockSpec((hb, D_GROUP), lambda b, i: (b * nhb + jnp.minimum(i * r + r, nhb - 1), 0)),
                  pl.BlockSpec((4, 2 * GRID_W, 2 * GRID_W), lambda b, i: (0, 0, 0)),
                  pl.BlockSpec((D_GROUP, D_GROUP), lambda b, i: (0, 0)),
                  pl.BlockSpec((1, D_GROUP), lambda b, i: (0, 0))],
        out_specs=pl.BlockSpec((tm, D_GROUP), lambda b, i: (b * tiles + i, 0)),
        scratch_shapes=[pltpu.VMEM((tm + 2 * hb, D_GROUP), F32)],
        compiler_params=_cparams(("parallel", "parallel")),
        name="pool2d",
    )(u, u, u, jnp.asarray(pc, BF16), wbd, scale)


def _pool1d_body(x_ref, p_ref, w_ref, sc_ref, o_ref):
    x = x_ref[...]
    n = x.shape[0]
    ys = [jnp.dot(p_ref[wi], x, preferred_element_type=F32) for wi in range(4)]
    wl, half = _lane_windows((n, D_GROUP))
    z = jnp.where(wl == 2, ys[0], jnp.where(wl == 4, ys[1], jnp.where(wl == 8, ys[2], ys[3])))
    cnt = _box_count(_iota((n, D_GROUP), 0), wl, half, n)
    dlt = (z / cnt.astype(F32) - x).astype(BF16)
    o_ref[...] = jnp.dot(dlt, w_ref[...], preferred_element_type=F32) * sc_ref[...]


def _pool1d(u, wbd, scale, *, bsz, seq):
    pm = np.stack([_box_matrix(seq, w) for w in POOL_WINDOWS])
    return pl.pallas_call(
        _pool1d_body,
        out_shape=jax.ShapeDtypeStruct((bsz * seq, D_GROUP), F32),
        grid=(bsz,),
        in_specs=[pl.BlockSpec((seq, D_GROUP), lambda b: (b, 0)),
                  pl.BlockSpec((4, seq, seq), lambda b: (0, 0, 0)),
                  pl.BlockSpec((D_GROUP, D_GROUP), lambda b: (0, 0)),
                  pl.BlockSpec((1, D_GROUP), lambda b: (0, 0))],
        out_specs=pl.BlockSpec((seq, D_GROUP), lambda b: (b, 0)),
        compiler_params=_cparams(("parallel",)),
        name="pool1d",
    )(u, jnp.asarray(pm, BF16), wbd, scale)


def _fill_halo(buf, cur, prev, nxt, i, tps, hb, tm, pre):
    first = (i % tps) == 0
    last = (i % tps) == tps - 1
    buf[0:hb, :] = jnp.where(first, 0.0, pre(prev[...]))
    buf[hb:hb + tm, :] = pre(cur[...])
    buf[hb + tm:hb + tm + hb, :] = jnp.where(last, 0.0, pre(nxt[...]))


def _dwconv(buf, w_ref, r0, rc, taps, off):
    acc = buf[r0 + off:r0 + off + rc, :] * w_ref[0:1, :]
    for j in range(1, taps):
        acc = acc + buf[r0 + off + j:r0 + off + j + rc, :] * w_ref[j:j + 1, :]
    return acc


def _conf_body(cur, prev, nxt, w_ref, b_ref, lg_ref, lb_ref, o_ref, buf, *, tps):
    i = pl.program_id(0)
    tm = cur.shape[0]
    hb = prev.shape[0]

    def glu(u):
        u = u.astype(F32)
        return u[:, :D_GROUP] * jax.nn.sigmoid(u[:, D_GROUP:])

    _fill_halo(buf, cur, prev, nxt, i, tps, hb, tm, glu)
    rc = 128
    off = hb - (CONV_WIDTH - 1) // 2
    for r0 in range(0, tm, rc):
        h = _dwconv(buf, w_ref, r0, rc, CONV_WIDTH, off) + b_ref[...]
        mu = jnp.mean(h, axis=-1, keepdims=True)
        hc = h - mu
        var = jnp.mean(hc * hc, axis=-1, keepdims=True)
        o_ref[r0:r0 + rc, :] = _silu(hc * lax.rsqrt(var + EPS) * lg_ref[...] + lb_ref[...])


def _halo_specs(tm, hb, width, nrows):
    r = tm // hb
    nhb = nrows // hb
    return [pl.BlockSpec((tm, width), lambda i: (i, 0)),
            pl.BlockSpec((hb, width), lambda i: (jnp.maximum(i * r - 1, 0), 0)),
            pl.BlockSpec((hb, width), lambda i: (jnp.minimum(i * r + r, nhb - 1), 0))]


def _conformer(u, w, b, lg, lb, *, seq, tm):
    n = u.shape[0]
    hb = 16
    vec = pl.BlockSpec((1, D_GROUP), lambda i: (0, 0))
    return pl.pallas_call(
        functools.partial(_conf_body, tps=seq // tm),
        out_shape=jax.ShapeDtypeStruct((n, D_GROUP), F32),
        grid=(n // tm,),
        in_specs=_halo_specs(tm, hb, 2 * D_GROUP, n) + [pl.BlockSpec((CONV_WIDTH, D_GROUP), lambda i: (0, 0)),
                                                        vec, vec, vec],
        out_specs=pl.BlockSpec((tm, D_GROUP), lambda i: (i, 0)),
        scratch_shapes=[pltpu.VMEM((tm + 2 * hb, D_GROUP), F32)],
        compiler_params=_cparams(("parallel",)),
        name="conformer_conv",
    )(u, u, u, w, b.reshape(1, -1), lg.reshape(1, -1), lb.reshape(1, -1))


def _short_body(cur, prev, nxt, w_ref, b_ref, v_ref, x1_ref, x2_ref, buf, *, tps):
    i = pl.program_id(0)
    tm = cur.shape[0]
    hb = prev.shape[0]
    _fill_halo(buf, cur, prev, nxt, i, tps, hb, tm, lambda u: u.astype(F32))
    rc = 128
    off = hb - (HY_SHORT - 1) // 2
    for r0 in range(0, tm, rc):
        uc = _dwconv(buf, w_ref, r0, rc, HY_SHORT, off) + b_ref[...]
        v_ref[r0:r0 + rc, :] = uc[:, :D_GROUP]
        x1_ref[r0:r0 + rc, :] = uc[:, D_GROUP:2 * D_GROUP]
        x2_ref[r0:r0 + rc, :] = uc[:, 2 * D_GROUP:]


def _hy_short(u, w, b, *, seq, tm):
    n = u.shape[0]
    hb = 16
    wd = 3 * D_GROUP
    ospec = pl.BlockSpec((tm, D_GROUP), lambda i: (i, 0))
    return pl.pallas_call(
        functools.partial(_short_body, tps=seq // tm),
        out_shape=[jax.ShapeDtypeStruct((n, D_GROUP), F32)] * 3,
        grid=(n // tm,),
        in_specs=_halo_specs(tm, hb, wd, n) + [pl.BlockSpec((HY_SHORT, wd), lambda i: (0, 0)),
                                               pl.BlockSpec((1, wd), lambda i: (0, 0))],
        out_specs=[ospec, ospec, ospec],
        scratch_shapes=[pltpu.VMEM((tm + 2 * hb, wd), F32)],
        compiler_params=_cparams(("parallel",)),
        name="hyena_short_conv",
    )(u, u, u, w, b.reshape(1, -1))


def _filter_features(n):
    i = np.arange(n, dtype=np.float64)
    t = np.linspace(0.0, 1.0, n, dtype=np.float32).astype(np.float64)
    wpos = ((2.0 * math.pi / n) * np.arange(n, dtype=np.float32)).astype(np.float32)
    bands = np.linspace(1e-4, HY_BANDS - 1, HY_BANDS, dtype=np.float32)
    arg = (bands[None, :] * wpos[:, None]).astype(np.float32).astype(np.float64)
    z = np.zeros((n, 64), np.float32)
    z[:, 0] = t
    z[:, 1:1 + HY_BANDS] = np.cos(arg)
    z[:, 1 + HY_BANDS:HY_EMB] = -np.sin(arg)
    del i
    return z


def _filter_body(z_ref, w1_ref, b1_ref, w2_ref, b2_ref, w3_ref, d_ref, hf_ref, hb_ref, s_ref):
    i = pl.program_id(0)
    z = z_ref[...]
    h = jnp.sin(jnp.dot(z, w1_ref[...], precision=HI, preferred_element_type=F32) + b1_ref[...])
    h = jnp.sin(jnp.dot(h, w2_ref[...], precision=HI, preferred_element_type=F32) + b2_ref[...])
    h = jnp.dot(h, w3_ref[...], precision=HI, preferred_element_type=F32)
    h = h * jnp.exp(-z[:, 0:1] * jnp.abs(d_ref[...]))
    for o in range(2):
        c0 = 2 * D_GROUP * o
        hf_ref[:, o * D_GROUP:(o + 1) * D_GROUP] = h[:, c0:c0 + D_GROUP]
        hb_ref[:, o * D_GROUP:(o + 1) * D_GROUP] = h[:, c0 + D_GROUP:c0 + 2 * D_GROUP]

    @pl.when(i == 0)
    def _():
        s_ref[...] = jnp.zeros_like(s_ref)

    s_ref[...] += jnp.sum(jnp.abs(h), axis=0, keepdims=True)


def _hy_filter(n, w1, b1, w2, b2, w3, deltas):
    tm = min(n, 512)
    z = jnp.asarray(_filter_features(n))
    w1p = jnp.zeros((64, HY_FFN), F32).at[:HY_EMB].set(w1)
    full = lambda shape: pl.BlockSpec(shape, lambda i: (0, 0))
    return pl.pallas_call(
        _filter_body,
        out_shape=[jax.ShapeDtypeStruct((n, 2 * D_GROUP), F32), jax.ShapeDtypeStruct((n, 2 * D_GROUP), F32),
                   jax.ShapeDtypeStruct((1, HY_FILTER_CH), F32)],
        grid=(n // tm,),
        in_specs=[pl.BlockSpec((tm, 64), lambda i: (i, 0)), full((64, HY_FFN)), full((1, HY_FFN)),
                  full((HY_FFN, HY_FFN)), full((1, HY_FFN)), full((HY_FFN, HY_FILTER_CH)),
                  full((1, HY_FILTER_CH))],
        out_specs=[pl.BlockSpec((tm, 2 * D_GROUP), lambda i: (i, 0)), pl.BlockSpec((tm, 2 * D_GROUP), lambda i: (i, 0)),
                   full((1, HY_FILTER_CH))],
        compiler_params=_cparams(("arbitrary",)),
        name="hyena_filter_mlp",
    )(z, w1p, b1.reshape(1, -1), w2, b2.reshape(1, -1), w3, deltas.reshape(1, -1))


@functools.lru_cache(maxsize=None)
def _dft_tables(n1):
    nn = n1 * DFT2
    j = np.arange(DFT2)
    th = 2.0 * np.pi * ((np.outer(j, j)) % DFT2) / DFT2
    fr, fi = np.cos(th), -np.sin(th)
    m_fwd = np.block([[fr, -fi], [fi, fr]])
    m_inv = np.block([[fr, fi], [-fi, fr]])
    hh = n1 // 2
    k1 = np.arange(n1)[None, :, None]
    t1 = np.arange(hh)[None, None, :]
    t2 = np.arange(DFT2)[:, None, None]
    ph = 2.0 * np.pi * ((k1 * (DFT2 * t1 + t2)) % nn) / nn
    ar, ai = np.cos(ph), -np.sin(ph)
    a_fwd = np.concatenate([np.concatenate([ar, -ai], axis=2), np.concatenate([ai, ar], axis=2)], axis=1)
    a_real = np.concatenate([ar, ai], axis=1)
    t1r = np.where(t2 >= 1, n1 - 1 - t1, (n1 - t1) % n1)
    phr = 2.0 * np.pi * ((k1 * (DFT2 * t1r + t2)) % nn) / nn
    a_rev = np.concatenate([np.cos(phr), -np.sin(phr)], axis=1)
    a_rev[0, :, 0] = 0.0
    pht = np.transpose(ph, (0, 2, 1))
    cr, ci = np.cos(pht) / nn, np.sin(pht) / nn
    a_inv = np.concatenate([np.concatenate([cr, -ci], axis=2), np.concatenate([ci, cr], axis=2)], axis=1)
    f32 = lambda a: np.ascontiguousarray(a, dtype=np.float32)
    return dict(m_fwd=f32(m_fwd), m_inv=f32(m_inv), a_fwd=f32(a_fwd), a_real=f32(a_real), a_rev=f32(a_rev),
                a_inv=f32(a_inv))


def _fft_a_body(x_ref, m_ref, o_ref, *, tj):
    n1 = o_ref.shape[1]
    for e in range(tj):
        x = jnp.concatenate([x_ref[0, :, e, :], x_ref[1, :, e, :]], axis=0).astype(BF16)
        a = jnp.dot(m_ref[e], x, preferred_element_type=F32)
        o_ref[0, :, e, :] = a[:n1]
        o_ref[1, :, e, :] = a[n1:]


def _fft_a(x4, m, *, n1, tj):
    c = x4.shape[3]
    return pl.pallas_call(
        functools.partial(_fft_a_body, tj=tj),
        out_shape=jax.ShapeDtypeStruct((2, n1, DFT2, c), F32),
        grid=(DFT2 // tj,),
        in_specs=[pl.BlockSpec((2, n1 // 2, tj, c), lambda j: (0, 0, j, 0)),
                  pl.BlockSpec((tj, 2 * n1, n1), lambda j: (j, 0, 0))],
        out_specs=pl.BlockSpec((2, n1, tj, c), lambda j: (0, 0, j, 0)),
        compiler_params=_cparams(("parallel",)),
        name="fft_stage_a",
    )(x4, m)


def _fft_b_body(a_ref, mf_ref, mi_ref, g_ref, o_ref, *, kg):
    for kk in range(kg):
        x = jnp.concatenate([a_ref[0, kk], a_ref[1, kk]], axis=0).astype(BF16)
        xf = jnp.dot(mf_ref[...], x, preferred_element_type=F32)
        xr, xi = xf[:DFT2], xf[DFT2:]
        gr, gi = g_ref[0, kk].astype(F32), g_ref[1, kk].astype(F32)
        y = jnp.concatenate([xr * gr - xi * gi, xr * gi + xi * gr], axis=0).astype(BF16)
        bf = jnp.dot(mi_ref[...], y, preferred_element_type=F32)
        o_ref[0, kk] = bf[:DFT2]
        o_ref[1, kk] = bf[DFT2:]


def _fft_b(a, g, order, tabs, *, n1, kg):
    c = a.shape[3]
    blk = pl.BlockSpec((2, kg, DFT2, c), lambda k: (0, k, 0, 0))
    mat = pl.BlockSpec((2 * DFT2, 2 * DFT2), lambda k: (0, 0))
    return pl.pallas_call(
        functools.partial(_fft_b_body, kg=kg),
        out_shape=jax.ShapeDtypeStruct(a.shape, F32),
        grid=(n1 // kg,),
        in_specs=[blk, mat, mat, pl.BlockSpec((2, kg, DFT2, c), lambda k: (0, k, 0, order))],
        out_specs=blk,
        compiler_params=_cparams(("parallel",)),
        name="fft_stage_b",
    )(a, tabs["m_fwd"], tabs["m_inv"], g)


def _fft_bf_body(a_ref, mf_ref, o_ref, *, kg):
    for kk in range(kg):
        x = jnp.concatenate([a_ref[0, kk], a_ref[1, kk]], axis=0).astype(BF16)
        xf = jnp.dot(mf_ref[...], x, preferred_element_type=F32)
        o_ref[0, kk] = xf[:DFT2].astype(BF16)
        o_ref[1, kk] = xf[DFT2:].astype(BF16)


def _fft_b_forward(a, tabs, *, n1, kg):
    c = a.shape[3]
    blk = pl.BlockSpec((2, kg, DFT2, c), lambda k: (0, k, 0, 0))
    return pl.pallas_call(
        functools.partial(_fft_bf_body, kg=kg),
        out_shape=jax.ShapeDtypeStruct(a.shape, BF16),
        grid=(n1 // kg,),
        in_specs=[blk, pl.BlockSpec((2 * DFT2, 2 * DFT2), lambda k: (0, 0))],
        out_specs=blk,
        compiler_params=_cparams(("parallel",)),
        name="fft_stage_b_filter",
    )(a, tabs["m_fwd"])


def _fft_ai_body(b_ref, m_ref, v_ref, x_ref, bias_ref, o_ref, *, tj):
    h = o_ref.shape[1]
    for e in range(tj):
        b = jnp.concatenate([b_ref[0, :, e, :], b_ref[1, :, e, :]], axis=0).astype(BF16)
        y = jnp.dot(m_ref[e], b, preferred_element_type=F32)
        o_ref[0, :, e, :] = y[:h]
        o_ref[1, :, e, :] = y[h:]
    o_ref[...] = x_ref[...] * (o_ref[...] + v_ref[...] * bias_ref[...])


def _fft_a_inv(b, m, v4, xm4, bias, *, n1, tj):
    c = b.shape[3]
    half = pl.BlockSpec((2, n1 // 2, tj, c), lambda j: (0, 0, j, 0))
    return pl.pallas_call(
        functools.partial(_fft_ai_body, tj=tj),
        out_shape=jax.ShapeDtypeStruct((2, n1 // 2, DFT2, c), F32),
        grid=(DFT2 // tj,),
        in_specs=[pl.BlockSpec((2, n1, tj, c), lambda j: (0, 0, j, 0)),
                  pl.BlockSpec((tj, n1, 2 * n1), lambda j: (j, 0, 0)),
                  half, half, pl.BlockSpec((1, c), lambda j: (0, 0))],
        out_specs=half,
        compiler_params=_cparams(("parallel",)),
        name="fft_stage_a_inv",
    )(b, m, v4, xm4, bias)


def _filt_a_body(hj_ref, hz_ref, hr_ref, s_ref, mf_ref, mr_ref, o_ref, *, tj):
    n1 = o_ref.shape[1]
    s = s_ref[...]
    for e in range(tj):
        src = hz_ref if e == 0 else hr_ref
        me = 0 if e == 0 else tj - e
        for o in range(2):
            c0 = 2 * D_GROUP * o
            inv = 1.0 / (s[:, c0:c0 + D_GROUP] + s[:, c0 + D_GROUP:c0 + 2 * D_GROUP] + EPS)
            lsl = slice(o * D_GROUP, (o + 1) * D_GROUP)
            hf = hj_ref[:, e, lsl].astype(BF16)
            hb = src[:, me, lsl].astype(BF16)
            a = (jnp.dot(mf_ref[e], hf, preferred_element_type=F32)
                 + jnp.dot(mr_ref[e], hb, preferred_element_type=F32)) * inv
            o_ref[0, :, e, lsl] = a[:n1]
            o_ref[1, :, e, lsl] = a[n1:]


def _filt_a(hf, hb, colsum, tabs, *, n1):
    tj = 8
    nj = DFT2 // tj
    wd = 2 * D_GROUP
    shp = (n1 // 2, DFT2, wd)
    blk = lambda fn: pl.BlockSpec((n1 // 2, tj, wd), fn)
    mat = pl.BlockSpec((tj, 2 * n1, n1 // 2), lambda j: (j, 0, 0))
    return pl.pallas_call(
        functools.partial(_filt_a_body, tj=tj),
        out_shape=jax.ShapeDtypeStruct((2, n1, DFT2, wd), F32),
        grid=(nj,),
        in_specs=[blk(lambda j: (0, j, 0)), blk(lambda j: (0, (nj - j) % nj, 0)), blk(lambda j: (0, nj - 1 - j, 0)),
                  pl.BlockSpec((1, HY_FILTER_CH), lambda j: (0, 0)), mat, mat],
        out_specs=pl.BlockSpec((2, n1, tj, wd), lambda j: (0, 0, j, 0)),
        compiler_params=_cparams(("parallel",)),
        name="filter_stage_a",
    )(hf.reshape(shp), hb.reshape(shp), hb.reshape(shp), colsum, tabs["a_real"], tabs["a_rev"])


def _hyena_long(v, x1, x2, hf, hb, colsum, bias, *, seq):
    n1 = 2 * seq // DFT2
    tabs = {k: jnp.asarray(a).astype(BF16) for k, a in _dft_tables(n1).items()}
    tj = 8
    kg = 4
    fa = _filt_a(hf, hb, colsum, tabs, n1=n1)
    g = _fft_b_forward(fa, tabs, n1=n1, kg=kg // 2)
    shp = (2, n1 // 2, DFT2, D_GROUP)
    z = v.reshape(shp)
    for order, xm in ((0, x1.reshape(shp)), (1, x2.reshape(shp))):
        a = _fft_a(z, tabs["a_fwd"], n1=n1, tj=tj)
        b = _fft_b(a, g, order, tabs, n1=n1, kg=kg)
        z = _fft_a_inv(b, tabs["a_inv"], z, xm, bias[order].reshape(1, D_GROUP), n1=n1, tj=tj)
    return z.reshape(2 * seq, D_GROUP)


def _hyena_ctx_body(v_ref, x1_ref, x2_ref, hf_ref, hb_ref, s_ref, bias_ref, cm_ref, sm_ref, ct_ref, st_ref, o_ref):
    n = v_ref.shape[1]
    cm, sm, ct, st = cm_ref[...], sm_ref[...], ct_ref[...], st_ref[...]
    s = s_ref[...]
    row0 = _iota((n, D_GROUP), 0) == 0
    dot = lambda a, b: jnp.dot(a, b.astype(BF16), preferred_element_type=F32)
    zr, zi = v_ref[0], v_ref[1]
    for order, xm in ((0, x1_ref), (1, x2_ref)):
        c0 = 2 * D_GROUP * order
        inv = 1.0 / (s[:, c0:c0 + D_GROUP] + s[:, c0 + D_GROUP:c0 + 2 * D_GROUP] + EPS)
        lsl = slice(order * D_GROUP, (order + 1) * D_GROUP)
        hf = hf_ref[:, lsl] * inv
        hb = jnp.where(row0, 0.0, hb_ref[:, lsl] * inv)
        gr = dot(cm, hf + hb)
        gi = dot(sm, hb - hf)
        xr = dot(cm, zr) + dot(sm, zi)
        xi = dot(cm, zi) - dot(sm, zr)
        yr = xr * gr - xi * gi
        yi = xr * gi + xi * gr
        cr = dot(ct, yr) - dot(st, yi)
        ci = dot(ct, yi) + dot(st, yr)
        bias = bias_ref[order:order + 1, :]
        zr = xm[0] * (cr + zr * bias)
        zi = xm[1] * (ci + zi * bias)
    o_ref[0] = zr
    o_ref[1] = zi


def _hyena_ctx(v, x1, x2, hf, hb, colsum, bias, *, seq):
    nn = 2 * seq
    k = np.arange(nn)
    t = np.arange(seq)
    th = 2.0 * np.pi * (np.outer(k, t) % nn) / nn
    cm, sm = np.cos(th), np.sin(th)
    consts = [jnp.asarray(a, F32).astype(BF16) for a in (cm, sm, cm.T / nn, sm.T / nn)]
    shp = (2, seq, D_GROUP)
    full3 = pl.BlockSpec(shp, lambda i: (0, 0, 0))
    f2 = lambda a: pl.BlockSpec(a.shape, lambda i: (0, 0))
    args = [hf, hb, colsum, bias] + consts
    out = pl.pallas_call(
        _hyena_ctx_body,
        out_shape=jax.ShapeDtypeStruct(shp, F32),
        grid=(1,),
        in_specs=[full3, full3, full3] + [f2(a) for a in args],
        out_specs=full3,
        compiler_params=_cparams(("arbitrary",)),
        name="hyena_ctx",
    )(v.reshape(shp), x1.reshape(shp), x2.reshape(shp), *args)
    return out.reshape(2 * seq, D_GROUP)


def _mod_rows(mod_l, rows, first, count):
    m = mod_l[rows, first:first + count, :]
    return jnp.pad(m, ((0, 0), (0, 8 - count), (0, 0)))


def _split_w_in(w_in):
    wb = w_in.astype(BF16)
    parts = dict(k=wb[:, COL_K:COL_V], v=wb[:, COL_V:COL_GF], q=wb[:, COL_Q:COL_R], r=wb[:, COL_R:COL_POOL],
                 pool=wb[:, COL_POOL:COL_HY], hy=wb[:, COL_HY:COL_CONV], conv=wb[:, COL_CONV:P_IN])
    gate = jnp.pad(wb[:, COL_GF:COL_Q], ((0, 0), (0, LANE - 2 * GLA_LOWRANK)))
    return parts, gate


def _mix(u, p, *, bsz, seq, is_ctx, states, filt):
    sf, sb = states
    cpb = min(8, seq // CHUNK)
    if is_ctx:
        pool = _pool1d(u["pool"], p["pool_wbd"], p["pool_scale"], bsz=bsz, seq=seq)
    else:
        pool = _pool2d(u["pool"], p["pool_wbd"], p["pool_scale"], bsz=bsz, seq=seq)
    tmc = min(seq, 1024)
    v, x1, x2 = _hy_short(u["hy"], p["hy_short_w"], p["hy_short_b"], seq=seq, tm=tmc)
    if is_ctx:
        hy = _hyena_ctx(v, x1, x2, *filt, p["hy_bias"], seq=seq)
    else:
        hy = _hyena_long(v, x1, x2, *filt, p["hy_bias"], seq=seq)
    gla = _gla_read(u["k"], u["v"], u["q"], u["r"], u["lf"], u["lb"], sf, sb, p["gla_ng"],
                    bsz=bsz, seq=seq, cpb=cpb)
    conv = _conformer(u["conv"], p["conv_dw_w"], p["conv_dw_b"], p["conv_ln_g"], p["conv_ln_b"], seq=seq, tm=tmc)
    return [pool, hy, gla, conv]


def kernel(x, c, ctx, c_ctx, ada_w, ada_b, ffn1_norm, ffn1_wi, ffn1_wo, mix_norm, w_in, w_out, pool_w, pool_scale, hy_short_w, hy_short_b, hy_w1, hy_b1, hy_w2, hy_b2, hy_w3, hy_deltas, hy_bias, gla_gw_f, gla_gb_f, gla_gw_b, gla_gb_b, gla_norm, conv_dw_w, conv_dw_b, conv_ln_g, conv_ln_b, ffn2_norm, ffn2_wi, ffn2_wo, final_norm):
    bsz, seq, d = x.shape
    clen = ctx.shape[1]
    depth = ada_w.shape[0]
    assert bsz == 2, "the Hyena transform packs exactly two batch rows into one complex signal"
    xs = x.reshape(bsz * seq, d)
    cs = ctx.reshape(bsz * clen, d)
    cc = jnp.zeros((8, d), F32).at[0:bsz].set(c).at[bsz].set(c_ctx)
    mod = _modulation(cc, ada_w, ada_b).reshape(depth, 8, N_MOD, d)
    xrows = np.arange(bsz)
    crows = np.full((1,), bsz)
    tmx = 512
    tps_x = seq // tmx
    tmc = bsz * clen
    zero_state = jnp.zeros((bsz, GLA_HEADS * GLA_DV, GLA_QK), F32)

    for l in range(depth):
        last = l == depth - 1
        ml = mod[l]
        wparts, wgate = _split_w_in(w_in[l])
        gw = jnp.zeros((LANE, 2 * GLA_QK), F32)
        gw = gw.at[0:GLA_LOWRANK, 0:GLA_QK].set(gla_gw_f[l])
        gw = gw.at[GLA_LOWRANK:2 * GLA_LOWRANK, GLA_QK:].set(gla_gw_b[l])
        gb = jnp.concatenate([gla_gb_f[l], gla_gb_b[l]]).reshape(1, -1)
        wbd = jnp.zeros((D_GROUP, D_GROUP), F32)
        for gi in range(len(POOL_WINDOWS)):
            wbd = wbd.at[gi * POOL_CH:(gi + 1) * POOL_CH, gi * POOL_CH:(gi + 1) * POOL_CH].set(pool_w[l, gi])
        p = dict(pool_wbd=wbd.astype(BF16), pool_scale=pool_scale[l].reshape(1, -1),
                 hy_short_w=hy_short_w[l], hy_short_b=hy_short_b[l], hy_bias=hy_bias[l],
                 gla_ng=jnp.tile(gla_norm[l], GLA_HEADS).reshape(1, -1),
                 conv_dw_w=conv_dw_w[l], conv_dw_b=conv_dw_b[l], conv_ln_g=conv_ln_g[l], conv_ln_b=conv_ln_b[l])
        wi1, wo1 = ffn1_wi[l].astype(BF16), ffn1_wo[l].astype(BF16)
        wi2, wo2 = ffn2_wi[l].astype(BF16), ffn2_wo[l].astype(BF16)
        wout = w_out[l].astype(BF16)
        names = ["k", "v", "q", "r", "pool", "hy", "conv"]

        xs, *outs = _ffn_proj(xs, _mod_rows(ml, xrows, 0, 5), ffn1_norm[l], wi1, wo1, mix_norm[l],
                              [wparts[nm] for nm in names], wgate, gw, gb, tm=tmx, tiles_per_seq=tps_x)
        ux = dict(zip(names + ["lf", "lb"], outs))
        cnames = ["k", "v"] if last else names
        cs, *outs = _ffn_proj(cs, _mod_rows(ml, crows, 0, 5), ffn1_norm[l], wi1, wo1, mix_norm[l],
                              [wparts[nm] for nm in cnames], wgate, gw, gb, tm=tmc, tiles_per_seq=1)
        uc = dict(zip(cnames + ["lf", "lb"], outs))

        ccpb = clen // CHUNK
        sfc, sbc, finf, finb = _gla_states(uc["k"], uc["v"], uc["lf"], uc["lb"], zero_state, zero_state,
                                           bsz=bsz, seq=clen, cpb=ccpb)
        sfx, sbx, _, _ = _gla_states(ux["k"], ux["v"], ux["lf"], ux["lb"], finf, finb, bsz=bsz, seq=seq, cpb=8)

        filt = _hy_filter(seq, hy_w1[l], hy_b1[l], hy_w2[l], hy_b2[l], hy_w3[l], hy_deltas[l])
        mixed = _mix(ux, p, bsz=bsz, seq=seq, is_ctx=False, states=(sfx, sbx), filt=filt)
        xs = _out_ffn(xs, _mod_rows(ml, xrows, 5, 4), mixed, wout, ffn2_norm[l], wi2, wo2, tm=tmx,
                      tiles_per_seq=tps_x, final_g=final_norm if last else None)
        if not last:
            filt_c = _hy_filter(clen, hy_w1[l], hy_b1[l], hy_w2[l], hy_b2[l], hy_w3[l], hy_deltas[l])
            mixed = _mix(uc, p, bsz=bsz, seq=clen, is_ctx=True, states=(sfc, sbc), filt=filt_c)
            cs = _out_ffn(cs, _mod_rows(ml, crows, 5, 4), mixed, wout, ffn2_norm[l], wi2, wo2, tm=tmc,
                          tiles_per_seq=1)
    return xs.reshape(bsz, seq, d)
```

```python
import functools
import math

import numpy as np
import jax
import jax.numpy as jnp
from jax import lax
from jax.experimental import pallas as pl
from jax.experimental.pallas import tpu as pltpu

F32 = jnp.float32
BF16 = jnp.bfloat16
HI = lax.Precision.HIGHEST

D_MODEL = 1024
GRID_W = 64
D_GROUP = 256
D_FF = 2816
N_MOD = 9
EPS = 1e-6
POOL_WINDOWS = (2, 4, 8, 16)
POOL_CH = 64
HY_BANDS = 16
HY_EMB = 1 + 2 * HY_BANDS
HY_FFN = 64
HY_FILTER_CH = 4 * D_GROUP
GLA_HEADS = 4
GLA_DK = 32
GLA_DV = 64
GLA_QK = 128
GLA_LOWRANK = 16
GLA_TAU = 16.0
CHUNK = 64
CONV_WIDTH = 31
HY_SHORT = 3

COL_K = 0
COL_V = COL_K + GLA_QK
COL_GF = COL_V + D_GROUP
COL_GB = COL_GF + GLA_LOWRANK
COL_Q = COL_GB + GLA_LOWRANK
COL_R = COL_Q + GLA_QK
COL_POOL = COL_R + D_GROUP
COL_HY = COL_POOL + D_GROUP
COL_CONV = COL_HY + 3 * D_GROUP
P_IN = COL_CONV + 2 * D_GROUP

LANE = 128
DFT2 = 128
MIB = 1024 * 1024


def _cparams(sem, vmem_mib=None):
    kw = dict(dimension_semantics=sem)
    if vmem_mib is not None:
        kw["vmem_limit_bytes"] = vmem_mib * MIB
    return pltpu.CompilerParams(**kw)


def _silu(x):
    return x * jax.nn.sigmoid(x)


def _rms_mod(h, g, m):
    y = h * lax.rsqrt(jnp.mean(h * h, axis=-1, keepdims=True) + EPS) * g
    return y * (1.0 + m[1:2, :]) + m[0:1, :]


def _iota(shape, dim):
    return lax.broadcasted_iota(jnp.int32, shape, dim)


def _mod_body(c_ref, w_ref, b_ref, o_ref):
    a = _silu(c_ref[...])
    o_ref[0] = jnp.dot(a, w_ref[0], precision=HI, preferred_element_type=F32) + b_ref[0]


def _modulation(cc, ada_w, ada_b):
    nl, d, nm = ada_w.shape
    tn = 1024
    return pl.pallas_call(
        _mod_body,
        out_shape=jax.ShapeDtypeStruct((nl, 8, nm), F32),
        grid=(nl, nm // tn),
        in_specs=[pl.BlockSpec((8, d), lambda l, j: (0, 0)),
                  pl.BlockSpec((1, d, tn), lambda l, j: (l, 0, j)),
                  pl.BlockSpec((1, 1, tn), lambda l, j: (l, 0, j))],
        out_specs=pl.BlockSpec((1, 8, tn), lambda l, j: (l, 0, j)),
        compiler_params=_cparams(("parallel", "parallel")),
        name="adaln_mod",
    )(cc, ada_w, ada_b.reshape(nl, 1, nm))


FF_CHUNK = 256
ROW_SUB = 256


def _swiglu(xn, wi_ref, wo_ref, hm_ref, rows):
    ff = wo_ref.shape[0]
    for c in range(0, ff, FF_CHUNK):
        a = jnp.dot(xn, wi_ref[:, c:c + FF_CHUNK], preferred_element_type=F32)
        g = jnp.dot(xn, wi_ref[:, ff + c:ff + c + FF_CHUNK], preferred_element_type=F32)
        hm_ref[rows, c:c + FF_CHUNK] = (_silu(g) * a).astype(BF16)
    return jnp.dot(hm_ref[rows, :], wo_ref[...], preferred_element_type=F32)


def _resident(shape):
    nd = len(shape)
    return pl.BlockSpec(shape, lambda i: (0,) * nd, pipeline_mode=pl.Buffered(1))


def _gla_log_decay(xn, wg_ref, gw_ref, gb_ref):
    ug = jnp.dot(xn, wg_ref[...], preferred_element_type=F32)
    a = jnp.dot(ug, gw_ref[...], precision=HI, preferred_element_type=F32) + gb_ref[...]
    return (jnp.minimum(a, 0.0) - jnp.log(1.0 + jnp.exp(-jnp.abs(a)))) * (1.0 / GLA_TAU)


def _ffn_proj_body(h_ref, m_ref, g1_ref, wi_ref, wo_ref, g2_ref, *rest, nparts):
    w_refs = rest[:nparts]
    wg_ref, gw_ref, gb_ref, tl_ref = rest[nparts:nparts + 4]
    x_ref = rest[nparts + 4]
    o_refs = rest[nparts + 5:2 * nparts + 7]
    hm_ref = rest[-1]
    m = m_ref[0]
    for r0 in range(0, h_ref.shape[0], ROW_SUB):
        rows = slice(r0, r0 + ROW_SUB)
        h = h_ref[rows, :]
        xn = _rms_mod(h, g1_ref[...], m[0:2]).astype(BF16)
        x1 = h + (0.5 * m[2:3, :]) * _swiglu(xn, wi_ref, wo_ref, hm_ref, rows)
        x_ref[rows, :] = x1
        xn2 = _rms_mod(x1, g2_ref[...], m[3:5]).astype(BF16)
        for w_ref, o_ref in zip(w_refs, o_refs[:nparts]):
            o_ref[rows, :] = jnp.dot(xn2, w_ref[...], preferred_element_type=F32).astype(o_ref.dtype)
        ls = _gla_log_decay(xn2, wg_ref, gw_ref, gb_ref)
        pre = _chunk_sum(tl_ref[...], *_split_bf16(ls))
        o_refs[nparts][rows, :] = pre[:, :GLA_QK]
        pb = pre[:, GLA_QK:]
        tot = jnp.concatenate([jnp.broadcast_to(pb[c0 + CHUNK - 1:c0 + CHUNK, :], (CHUNK, GLA_QK))
                               for c0 in range(0, ROW_SUB, CHUNK)], axis=0)
        o_refs[nparts + 1][rows, :] = tot - pb + ls[:, GLA_QK:]


def _ffn_proj(h, m, g1, wi, wo, g2, w_parts, w_gate, gw, gb, *, tm, tiles_per_seq):
    n, d = h.shape
    ff = wo.shape[0]
    nparts = len(w_parts)
    widths = [w.shape[1] for w in w_parts]
    tok = lambda wd: pl.BlockSpec((tm, wd), lambda i: (i, 0))
    in_specs = [tok(d), pl.BlockSpec((1, 8, d), lambda i: (i // tiles_per_seq, 0, 0)), _resident((1, d)),
                _resident((d, 2 * ff)), _resident((ff, d)), _resident((1, d))]
    in_specs += [_resident((d, wd)) for wd in widths]
    in_specs += [_resident((d, LANE)), _resident((LANE, 2 * GLA_QK)), _resident((1, 2 * GLA_QK)),
                 _resident((ROW_SUB, ROW_SUB))]
    tl = jnp.asarray(_chunk_matrices(ROW_SUB)[0]).astype(BF16)
    out_shape = [jax.ShapeDtypeStruct((n, d), F32)]
    out_shape += [jax.ShapeDtypeStruct((n, wd), BF16) for wd in widths]
    out_shape += [jax.ShapeDtypeStruct((n, GLA_QK), F32)] * 2
    return pl.pallas_call(
        functools.partial(_ffn_proj_body, nparts=nparts),
        out_shape=out_shape,
        grid=(n // tm,),
        in_specs=in_specs,
        out_specs=[tok(d)] + [tok(wd) for wd in widths] + [tok(GLA_QK)] * 2,
        scratch_shapes=[pltpu.VMEM((tm, ff), BF16)],
        compiler_params=_cparams(("parallel",), 56),
        name="ffn1_in_proj",
    )(h, m, g1.reshape(1, d), wi, wo, g2.reshape(1, d), *w_parts, w_gate, gw, gb, tl)


def _out_ffn_body(x_ref, m_ref, p_ref, hy_ref, gl_ref, cv_ref, wout_ref, g_ref, wi_ref, wo_ref, fn_ref, o_ref,
                  hm_ref, *, final):
    m = m_ref[0]
    for r0 in range(0, x_ref.shape[0], ROW_SUB):
        rows = slice(r0, r0 + ROW_SUB)
        acc = jnp.dot(p_ref[rows, :].astype(BF16), wout_ref[0:D_GROUP, :], preferred_element_type=F32)
        acc += jnp.dot(hy_ref[rows, :].astype(BF16), wout_ref[D_GROUP:2 * D_GROUP, :], preferred_element_type=F32)
        acc += jnp.dot(gl_ref[rows, :].astype(BF16), wout_ref[2 * D_GROUP:3 * D_GROUP, :],
                       preferred_element_type=F32)
        acc += jnp.dot(cv_ref[rows, :].astype(BF16), wout_ref[3 * D_GROUP:, :], preferred_element_type=F32)
        x2 = x_ref[rows, :] + m[0:1, :] * acc
        xn = _rms_mod(x2, g_ref[...], m[1:3]).astype(BF16)
        out = x2 + (0.5 * m[3:4, :]) * _swiglu(xn, wi_ref, wo_ref, hm_ref, rows)
        if final:
            out = out * lax.rsqrt(jnp.mean(out * out, axis=-1, keepdims=True) + EPS) * fn_ref[...]
        o_ref[rows, :] = out


def _out_ffn(x, m, parts, w_out, g, wi, wo, *, tm, tiles_per_seq, final_g=None):
    n, d = x.shape
    ff = wo.shape[0]
    final = final_g is not None
    fg = final_g if final else g
    tok = lambda wd: pl.BlockSpec((tm, wd), lambda i: (i, 0))
    return pl.pallas_call(
        functools.partial(_out_ffn_body, final=final),
        out_shape=jax.ShapeDtypeStruct((n, d), F32),
        grid=(n // tm,),
        in_specs=[tok(d), pl.BlockSpec((1, 8, d), lambda i: (i // tiles_per_seq, 0, 0)),
                  tok(D_GROUP), tok(D_GROUP), tok(D_GROUP), tok(D_GROUP), _resident((d, d)), _resident((1, d)),
                  _resident((d, 2 * ff)), _resident((ff, d)), _resident((1, d))],
        out_specs=tok(d),
        scratch_shapes=[pltpu.VMEM((tm, ff), BF16)],
        compiler_params=_cparams(("parallel",), 56),
        name="out_proj_ffn2",
    )(x, m, *parts, w_out, g.reshape(1, d), wi, wo, fg.reshape(1, d))


@functools.lru_cache(maxsize=None)
def _chunk_matrices(bt):
    r = np.arange(bt)
    same = (r[:, None] // CHUNK) == (r[None, :] // CHUNK)
    low = same & (r[:, None] >= r[None, :])
    up = same & (r[:, None] <= r[None, :])
    return tuple(np.ascontiguousarray(m, dtype=np.float32) for m in (low, up, same))


def _chunk_sum(mat, hi, lo):
    return jnp.dot(mat, hi, preferred_element_type=F32) + jnp.dot(mat, lo, preferred_element_type=F32)


def _gla_state_body(kf, vf, bf, kb, vb, bb, s0f, s0b, sf_o, sb_o, ff_o, fb_o, stf, stb, *, cpb):
    i = pl.program_id(1)

    @pl.when(i == 0)
    def _():
        stf[...] = s0f[0]
        stb[...] = s0b[0]

    sshape = (GLA_HEADS * GLA_DV, GLA_QK)
    bmask = (_iota(sshape, 0) >> 6) == (_iota(sshape, 1) >> 5)
    tn_dims = (((0,), (0,)), ((), ()))

    def direction(k_ref, v_ref, b_ref, last, st, s_o, order):
        s = st[...]
        for ci in order:
            sl = slice(ci * CHUNK, (ci + 1) * CHUNK)
            b = b_ref[sl, :]
            tot = b[last:last + 1, :]
            kd = (k_ref[sl, :] * jnp.exp(tot - b)).astype(BF16)
            upd = lax.dot_general(v_ref[sl, :].astype(BF16), kd, tn_dims, preferred_element_type=F32)
            s_o[0, ci] = s.astype(BF16)
            s = s * jnp.exp(tot) + jnp.where(bmask, upd, 0.0)
        st[...] = s

    direction(kf, vf, bf, CHUNK - 1, stf, sf_o, range(cpb))
    direction(kb, vb, bb, 0, stb, sb_o, range(cpb - 1, -1, -1))
    ff_o[0] = stf[...]
    fb_o[0] = stb[...]


def _gla_states(k, v, lf, lb, s0f, s0b, *, bsz, seq, cpb):
    bt = cpb * CHUNK
    nb = seq // bt
    nc = seq // CHUNK
    srow = GLA_HEADS * GLA_DV

    def tf(b, i):
        return (b * nb + i, 0)

    def tb(b, i):
        return (b * nb + nb - 1 - i, 0)

    sblk = pl.BlockSpec((1, srow, GLA_QK), lambda b, i: (b, 0, 0))
    return pl.pallas_call(
        functools.partial(_gla_state_body, cpb=cpb),
        out_shape=[jax.ShapeDtypeStruct((bsz, nc, srow, GLA_QK), BF16),
                   jax.ShapeDtypeStruct((bsz, nc, srow, GLA_QK), BF16),
                   jax.ShapeDtypeStruct((bsz, srow, GLA_QK), F32),
                   jax.ShapeDtypeStruct((bsz, srow, GLA_QK), F32)],
        grid=(bsz, nb),
        in_specs=[pl.BlockSpec((bt, GLA_QK), tf), pl.BlockSpec((bt, D_GROUP), tf), pl.BlockSpec((bt, GLA_QK), tf),
                  pl.BlockSpec((bt, GLA_QK), tb), pl.BlockSpec((bt, D_GROUP), tb), pl.BlockSpec((bt, GLA_QK), tb),
                  sblk, sblk],
        out_specs=[pl.BlockSpec((1, cpb, srow, GLA_QK), lambda b, i: (b, i, 0, 0)),
                   pl.BlockSpec((1, cpb, srow, GLA_QK), lambda b, i: (b, nb - 1 - i, 0, 0)),
                   sblk, sblk],
        scratch_shapes=[pltpu.VMEM((srow, GLA_QK), F32), pltpu.VMEM((srow, GLA_QK), F32)],
        compiler_params=_cparams(("parallel", "arbitrary")),
        name="gla_states",
    )(k, v, lf, k, v, lb, s0f, s0b)


def _gla_read_body(k_ref, v_ref, q_ref, r_ref, bf_ref, bb_ref, sf_ref, sb_ref, ng_ref, o_ref, *, cpb):
    hrows = GLA_HEADS * CHUNK
    cpos = _iota((hrows, CHUNK), 0) & (CHUNK - 1)
    ccol = _iota((hrows, CHUNK), 1)
    lowm = cpos >= ccol
    upm = cpos <= ccol
    hq = (_iota((hrows, GLA_QK), 0) >> 6) == (_iota((hrows, GLA_QK), 1) >> 5)
    ho = (_iota((hrows, D_GROUP), 0) >> 6) == (_iota((hrows, D_GROUP), 1) >> 6)
    bavg = jnp.where((_iota((D_GROUP, D_GROUP), 0) >> 6) == (_iota((D_GROUP, D_GROUP), 1) >> 6),
                     1.0 / GLA_DV, 0.0).astype(BF16)
    nt_dims = (((1,), (1,)), ((), ()))

    bf_ = bf_ref[...]
    bb_ = bb_ref[...]
    qs = q_ref[...].astype(F32) * (GLA_DK ** -0.5)
    kk = k_ref[...]
    qef = (qs * jnp.exp(bf_)).astype(BF16)
    qeb = (qs * jnp.exp(bb_)).astype(BF16)
    kef = (kk * jnp.exp(-bf_)).astype(BF16)
    keb = (kk * jnp.exp(-bb_)).astype(BF16)
    vb16 = v_ref[...].astype(BF16)
    zero = jnp.zeros((), BF16)
    for ci in range(cpb):
        sl = slice(ci * CHUNK, (ci + 1) * CHUNK)
        qf4 = jnp.where(hq, jnp.concatenate([qef[sl]] * GLA_HEADS, axis=0), zero)
        qb4 = jnp.where(hq, jnp.concatenate([qeb[sl]] * GLA_HEADS, axis=0), zero)
        af = lax.dot_general(qf4, kef[sl], nt_dims, preferred_element_type=F32)
        ab = lax.dot_general(qb4, keb[sl], nt_dims, preferred_element_type=F32)
        att = (jnp.where(lowm, af, 0.0) + jnp.where(upm, ab, 0.0)).astype(BF16)
        oall = jnp.dot(att, vb16[sl], preferred_element_type=F32)
        om = jnp.where(ho, oall, 0.0)
        o = om[0:CHUNK] + om[CHUNK:2 * CHUNK] + om[2 * CHUNK:3 * CHUNK] + om[3 * CHUNK:4 * CHUNK]
        qcat = jnp.concatenate([qef[sl], qeb[sl]], axis=1)
        scat = jnp.concatenate([sf_ref[0, ci], sb_ref[0, ci]], axis=1)
        o_ref[sl, :] = o + lax.dot_general(qcat, scat, nt_dims, preferred_element_type=F32)
    o = o_ref[...]
    ms = _chunk_sum_rhs(o * o, bavg)
    o_ref[...] = o * lax.rsqrt(ms + EPS) * ng_ref[...] * _silu(r_ref[...].astype(F32))


def _chunk_sum_rhs(x, mat):
    hi, lo = _split_bf16(x)
    return jnp.dot(hi, mat, preferred_element_type=F32) + jnp.dot(lo, mat, preferred_element_type=F32)


def _gla_read(k, v, q, r, lf, lb, sf, sb, ng, *, bsz, seq, cpb):
    bt = cpb * CHUNK
    nb = seq // bt
    srow = GLA_HEADS * GLA_DV
    n = bsz * seq

    def tk(i):
        return (i, 0)

    sspec = pl.BlockSpec((1, cpb, srow, GLA_QK), lambda i: (i // nb, i % nb, 0, 0))
    return pl.pallas_call(
        functools.partial(_gla_read_body, cpb=cpb),
        out_shape=jax.ShapeDtypeStruct((n, D_GROUP), F32),
        grid=(bsz * nb,),
        in_specs=[pl.BlockSpec((bt, GLA_QK), tk), pl.BlockSpec((bt, D_GROUP), tk),
                  pl.BlockSpec((bt, GLA_QK), tk), pl.BlockSpec((bt, D_GROUP), tk),
                  pl.BlockSpec((bt, GLA_QK), tk), pl.BlockSpec((bt, GLA_QK), tk),
                  sspec, sspec, pl.BlockSpec((1, D_GROUP), lambda i: (0, 0))],
        out_specs=pl.BlockSpec((bt, D_GROUP), tk),
        compiler_params=_cparams(("parallel",)),
        name="gla_readout",
    )(k, v, q, r, lf, lb, sf, sb, ng)


def _box_matrix(n, w):
    pos = np.arange(n)
    lo = np.clip(pos - w // 2, 0, n)
    hi = np.clip(pos - w // 2 + w, 0, n)
    col = np.arange(n)[None, :]
    return ((col >= lo[:, None]) & (col < hi[:, None])).astype(np.float32)


def _lane_windows(shape):
    w = jnp.left_shift(2, _iota(shape, 1) >> 6)
    return w, w >> 1


def _box_count(pos, w, half, n):
    return jnp.minimum(pos - half + w, n) - jnp.maximum(pos - half, 0)


def _split_bf16(x):
    hi = x.astype(BF16)
    lo = (x - hi.astype(F32)).astype(BF16)
    return hi, lo


def _pool2d_body(cur_ref, prev_ref, next_ref, pc_ref, w_ref, sc_ref, o_ref, ycol, *, tiles, rows):
    i = pl.program_id(1)
    tm = cur_ref.shape[0]
    hb = prev_ref.shape[0]
    sub = 2 * GRID_W
    pflag = jnp.where(i > 0, 1.0, 0.0)
    nflag = jnp.where(i < tiles - 1, 1.0, 0.0)

    def colpool(x):
        halves = []
        for half in range(2):
            lsl = slice(half * LANE, (half + 1) * LANE)
            ys = [jnp.dot(pc_ref[wi], x[:, lsl], preferred_element_type=F32) for wi in (2 * half, 2 * half + 1)]
            lane = _iota((sub, LANE), 1)
            halves.append(jnp.where(lane < POOL_CH, ys[0], ys[1]))
        return jnp.concatenate(halves, axis=1)

    for s in range(hb // sub):
        ycol[s * sub:(s + 1) * sub, :] = colpool(prev_ref[s * sub:(s + 1) * sub, :]) * pflag
    for s in range(tm // sub):
        ycol[hb + s * sub:hb + (s + 1) * sub, :] = colpool(cur_ref[s * sub:(s + 1) * sub, :])
    for s in range(hb // sub):
        ycol[hb + tm + s * sub:hb + tm + (s + 1) * sub, :] = colpool(next_ref[s * sub:(s + 1) * sub, :]) * nflag

    rc = 256
    wl, half = _lane_windows((rc, D_GROUP))
    for r0 in range(0, tm, rc):
        def rows_at(dd):
            return ycol[hb + r0 + GRID_W * dd:hb + r0 + GRID_W * dd + rc, :]

        acc = rows_at(-1) + rows_at(0)
        z2 = acc
        acc = acc + rows_at(-2) + rows_at(1)
        z4 = acc
        for dd in (-4, -3, 2, 3):
            acc = acc + rows_at(dd)
        z8 = acc
        for dd in (-8, -7, -6, -5, 4, 5, 6, 7):
            acc = acc + rows_at(dd)
        z16 = acc
        z = jnp.where(wl == 2, z2, jnp.where(wl == 4, z4, jnp.where(wl == 8, z8, z16)))
        tok = _iota((rc, D_GROUP), 0) + (i * tm + r0)
        rcnt = _box_count(tok >> 6, wl, half, rows)
        ccnt = _box_count(tok & (GRID_W - 1), wl, half, GRID_W)
        pooled = z / (rcnt * ccnt).astype(F32)
        dlt = (pooled - cur_ref[r0:r0 + rc, :]).astype(BF16)
        o_ref[r0:r0 + rc, :] = jnp.dot(dlt, w_ref[...], preferred_element_type=F32) * sc_ref[...]


def _pool2d(u, wbd, scale, *, bsz, seq):
    tm = 1024
    hb = 512
    tiles = seq // tm
    r = tm // hb
    nhb = seq // hb
    pc = np.stack([np.kron(np.eye(2, dtype=np.float32), _box_matrix(GRID_W, w)) for w in POOL_WINDOWS])
    return pl.pallas_call(
        functools.partial(_pool2d_body, tiles=tiles, rows=seq // GRID_W),
        out_shape=jax.ShapeDtypeStruct((bsz * seq, D_GROUP), F32),
        grid=(bsz, tiles),
        in_specs=[pl.BlockSpec((tm, D_GROUP), lambda b, i: (b * tiles + i, 0)),
                  pl.BlockSpec((hb, D_GROUP), lambda b, i: (b * nhb + jnp.maximum(i * r - 1, 0), 0)),
                  pl.BlockSpec((hb, D_GROUP), lambda b, i: (b * nhb + jnp.minimum(i * r + r, nhb - 1), 0)),
                  pl.BlockSpec((4, 2 * GRID_W, 2 * GRID_W), lambda b, i: (0, 0, 0)),
                  pl.BlockSpec((D_GROUP, D_GROUP), lambda b, i: (0, 0)),
                  pl.BlockSpec((1, D_GROUP), lambda b, i: (0, 0))],
        out_specs=pl.BlockSpec((tm, D_GROUP), lambda b, i: (b * tiles + i, 0)),
        scratch_shapes=[pltpu.VMEM((tm + 2 * hb, D_GROUP), F32)],
        compiler_params=_cparams(("parallel", "parallel")),
        name="pool2d",
    )(u, u, u, jnp.asarray(pc, BF16), wbd, scale)


def _pool1d_body(x_ref, p_ref, w_ref, sc_ref, o_ref):
    x = x_ref[...]
    n = x.shape[0]
    ys = [jnp.dot(p_ref[wi], x, preferred_element_type=F32) for wi in range(4)]
    wl, half = _lane_windows((n, D_GROUP))
    z = jnp.where(wl == 2, ys[0], jnp.where(wl == 4, ys[1], jnp.where(wl == 8, ys[2], ys[3])))
    cnt = _box_count(_iota((n, D_GROUP), 0), wl, half, n)
    dlt = (z / cnt.astype(F32) - x).astype(BF16)
    o_ref[...] = jnp.dot(dlt, w_ref[...], preferred_element_type=F32) * sc_ref[...]


def _pool1d(u, wbd, scale, *, bsz, seq):
    pm = np.stack([_box_matrix(seq, w) for w in POOL_WINDOWS])
    return pl.pallas_call(
        _pool1d_body,
        out_shape=jax.ShapeDtypeStruct((bsz * seq, D_GROUP), F32),
        grid=(bsz,),
        in_specs=[pl.BlockSpec((seq, D_GROUP), lambda b: (b, 0)),
                  pl.BlockSpec((4, seq, seq), lambda b: (0, 0, 0)),
                  pl.BlockSpec((D_GROUP, D_GROUP), lambda b: (0, 0)),
                  pl.BlockSpec((1, D_GROUP), lambda b: (0, 0))],
        out_specs=pl.BlockSpec((seq, D_GROUP), lambda b: (b, 0)),
        compiler_params=_cparams(("parallel",)),
        name="pool1d",
    )(u, jnp.asarray(pm, BF16), wbd, scale)


def _fill_halo(buf, cur, prev, nxt, i, tps, hb, tm, pre):
    first = (i % tps) == 0
    last = (i % tps) == tps - 1
    buf[0:hb, :] = jnp.where(first, 0.0, pre(prev[...]))
    buf[hb:hb + tm, :] = pre(cur[...])
    buf[hb + tm:hb + tm + hb, :] = jnp.where(last, 0.0, pre(nxt[...]))


SUBLANES = 8


def _tap_phases(taps, off):
    return sorted({(off + j) % SUBLANES for j in range(taps)} - {0})


def _fill_phases(sh, buf, phases):
    rows = sh.shape[1]
    for slot, s in enumerate(phases):
        sh[slot, :, :] = buf[s:s + rows, :]


def _dwconv(buf, sh, phases, w_ref, r0, rc, taps, off):
    acc = None
    for j in range(taps):
        s, q = (off + j) % SUBLANES, (off + j) // SUBLANES
        lo = r0 + SUBLANES * q
        src = buf[lo:lo + rc, :] if s == 0 else sh[phases.index(s), lo:lo + rc, :]
        term = src * w_ref[j:j + 1, :]
        acc = term if acc is None else acc + term
    return acc


def _conf_body(cur, prev, nxt, w_ref, b_ref, lg_ref, lb_ref, o_ref, buf, sh, *, tps):
    i = pl.program_id(0)
    tm = cur.shape[0]
    hb = prev.shape[0]

    def glu(u):
        u = u.astype(F32)
        return u[:, :D_GROUP] * jax.nn.sigmoid(u[:, D_GROUP:])

    _fill_halo(buf, cur, prev, nxt, i, tps, hb, tm, glu)
    rc = 128
    off = hb - (CONV_WIDTH - 1) // 2
    phases = _tap_phases(CONV_WIDTH, off)
    _fill_phases(sh, buf, phases)
    for r0 in range(0, tm, rc):
        h = _dwconv(buf, sh, phases, w_ref, r0, rc, CONV_WIDTH, off) + b_ref[...]
        mu = jnp.mean(h, axis=-1, keepdims=True)
        hc = h - mu
        var = jnp.mean(hc * hc, axis=-1, keepdims=True)
        o_ref[r0:r0 + rc, :] = _silu(hc * lax.rsqrt(var + EPS) * lg_ref[...] + lb_ref[...])


def _halo_specs(tm, hb, width, nrows):
    r = tm // hb
    nhb = nrows // hb
    return [pl.BlockSpec((tm, width), lambda i: (i, 0)),
            pl.BlockSpec((hb, width), lambda i: (jnp.maximum(i * r - 1, 0), 0)),
            pl.BlockSpec((hb, width), lambda i: (jnp.minimum(i * r + r, nhb - 1), 0))]


def _conformer(u, w, b, lg, lb, *, seq, tm):
    n = u.shape[0]
    hb = 16
    nph = len(_tap_phases(CONV_WIDTH, hb - (CONV_WIDTH - 1) // 2))
    vec = pl.BlockSpec((1, D_GROUP), lambda i: (0, 0))
    return pl.pallas_call(
        functools.partial(_conf_body, tps=seq // tm),
        out_shape=jax.ShapeDtypeStruct((n, D_GROUP), F32),
        grid=(n // tm,),
        in_specs=_halo_specs(tm, hb, 2 * D_GROUP, n) + [pl.BlockSpec((CONV_WIDTH, D_GROUP), lambda i: (0, 0)),
                                                        vec, vec, vec],
        out_specs=pl.BlockSpec((tm, D_GROUP), lambda i: (i, 0)),
        scratch_shapes=[pltpu.VMEM((tm + 2 * hb, D_GROUP), F32),
                        pltpu.VMEM((nph, tm + 2 * hb - SUBLANES, D_GROUP), F32)],
        compiler_params=_cparams(("parallel",)),
        name="conformer_conv",
    )(u, u, u, w, b.reshape(1, -1), lg.reshape(1, -1), lb.reshape(1, -1))


def _short_body(cur, prev, nxt, w_ref, b_ref, v_ref, x1_ref, x2_ref, buf, sh, *, tps):
    i = pl.program_id(0)
    tm = cur.shape[0]
    hb = prev.shape[0]
    _fill_halo(buf, cur, prev, nxt, i, tps, hb, tm, lambda u: u.astype(F32))
    rc = 128
    off = hb - (HY_SHORT - 1) // 2
    phases = _tap_phases(HY_SHORT, off)
    _fill_phases(sh, buf, phases)
    for r0 in range(0, tm, rc):
        uc = _dwconv(buf, sh, phases, w_ref, r0, rc, HY_SHORT, off) + b_ref[...]
        v_ref[r0:r0 + rc, :] = uc[:, :D_GROUP]
        x1_ref[r0:r0 + rc, :] = uc[:, D_GROUP:2 * D_GROUP]
        x2_ref[r0:r0 + rc, :] = uc[:, 2 * D_GROUP:]


def _hy_short(u, w, b, *, seq, tm):
    n = u.shape[0]
    hb = 16
    wd = 3 * D_GROUP
    nph = len(_tap_phases(HY_SHORT, hb - (HY_SHORT - 1) // 2))
    ospec = pl.BlockSpec((tm, D_GROUP), lambda i: (i, 0))
    return pl.pallas_call(
        functools.partial(_short_body, tps=seq // tm),
        out_shape=[jax.ShapeDtypeStruct((n, D_GROUP), F32)] * 3,
        grid=(n // tm,),
        in_specs=_halo_specs(tm, hb, wd, n) + [pl.BlockSpec((HY_SHORT, wd), lambda i: (0, 0)),
                                               pl.BlockSpec((1, wd), lambda i: (0, 0))],
        out_specs=[ospec, ospec, ospec],
        scratch_shapes=[pltpu.VMEM((tm + 2 * hb, wd), F32),
                        pltpu.VMEM((nph, tm + 2 * hb - SUBLANES, wd), F32)],
        compiler_params=_cparams(("parallel",)),
        name="hyena_short_conv",
    )(u, u, u, w, b.reshape(1, -1))


def _filter_features(n):
    i = np.arange(n, dtype=np.float64)
    t = np.linspace(0.0, 1.0, n, dtype=np.float32).astype(np.float64)
    wpos = ((2.0 * math.pi / n) * np.arange(n, dtype=np.float32)).astype(np.float32)
    bands = np.linspace(1e-4, HY_BANDS - 1, HY_BANDS, dtype=np.float32)
    arg = (bands[None, :] * wpos[:, None]).astype(np.float32).astype(np.float64)
    z = np.zeros((n, 64), np.float32)
    z[:, 0] = t
    z[:, 1:1 + HY_BANDS] = np.cos(arg)
    z[:, 1 + HY_BANDS:HY_EMB] = -np.sin(arg)
    del i
    return z


def _filter_body(z_ref, w1_ref, b1_ref, w2_ref, b2_ref, w3a_ref, w3b_ref, d_ref, hf_ref, hb_ref, s_ref):
    i = pl.program_id(0)
    half = z_ref.shape[0]
    z = z_ref[...]
    h = jnp.sin(jnp.dot(z, w1_ref[...], precision=HI, preferred_element_type=F32) + b1_ref[...])
    h = jnp.sin(jnp.dot(h, w2_ref[...], precision=HI, preferred_element_type=F32) + b2_ref[...])
    absd = jnp.abs(d_ref[...])
    tot = None
    for part, (w3_ref, tcol) in enumerate(((w3a_ref, 0), (w3b_ref, 64))):
        hp = jnp.dot(h, w3_ref[...], precision=HI, preferred_element_type=F32)
        hp = hp * jnp.exp(-z[:, tcol:tcol + 1] * absd)
        rows = slice(part * half, (part + 1) * half)
        for o in range(2):
            c0 = 2 * D_GROUP * o
            hf_ref[rows, o * D_GROUP:(o + 1) * D_GROUP] = hp[:, c0:c0 + D_GROUP]
            hb_ref[rows, o * D_GROUP:(o + 1) * D_GROUP] = hp[:, c0 + D_GROUP:c0 + 2 * D_GROUP]
        part_sum = jnp.sum(jnp.abs(hp), axis=0, keepdims=True)
        tot = part_sum if tot is None else tot + part_sum

    @pl.when(i == 0)
    def _():
        s_ref[...] = jnp.zeros_like(s_ref)

    s_ref[...] += tot


def _hy_filter(n, w1, b1, w2, b2, w3, deltas, *, t2_major):
    tm = min(n, 512)
    half = tm // 2
    z = _filter_features(n)
    if t2_major:
        z = z.reshape(n // DFT2, DFT2, 64).transpose(1, 0, 2).reshape(n, 64)
    zt = z.reshape(n // tm, 2, half, 64)
    z2 = jnp.asarray(np.concatenate([zt[:, 0], zt[:, 1]], axis=-1).reshape(n // 2, 2 * 64))
    w1p = jnp.zeros((64, HY_FFN), F32).at[:HY_EMB].set(w1)
    zero = jnp.zeros((64, HY_FFN), F32)
    bd = lambda w: jnp.concatenate([jnp.concatenate([w, zero], axis=1), jnp.concatenate([zero, w], axis=1)], axis=0)
    zero3 = jnp.zeros_like(w3)
    full = lambda shape: pl.BlockSpec(shape, lambda i: (0, 0))
    tok = pl.BlockSpec((tm, 2 * D_GROUP), lambda i: (i, 0))
    return pl.pallas_call(
        _filter_body,
        out_shape=[jax.ShapeDtypeStruct((n, 2 * D_GROUP), F32), jax.ShapeDtypeStruct((n, 2 * D_GROUP), F32),
                   jax.ShapeDtypeStruct((1, HY_FILTER_CH), F32)],
        grid=(n // tm,),
        in_specs=[pl.BlockSpec((half, 2 * 64), lambda i: (i, 0)), full((2 * 64, 2 * HY_FFN)), full((1, 2 * HY_FFN)),
                  full((2 * HY_FFN, 2 * HY_FFN)), full((1, 2 * HY_FFN)), full((2 * HY_FFN, HY_FILTER_CH)),
                  full((2 * HY_FFN, HY_FILTER_CH)), full((1, HY_FILTER_CH))],
        out_specs=[tok, tok, full((1, HY_FILTER_CH))],
        compiler_params=_cparams(("arbitrary",)),
        name="hyena_filter_mlp",
    )(z2, bd(w1p), jnp.tile(b1.reshape(1, -1), (1, 2)), bd(w2), jnp.tile(b2.reshape(1, -1), (1, 2)),
      jnp.concatenate([w3, zero3], axis=0), jnp.concatenate([zero3, w3], axis=0), deltas.reshape(1, -1))


@functools.lru_cache(maxsize=None)
def _dft_tables(n1):
    nn = n1 * DFT2
    j = np.arange(DFT2)
    th = 2.0 * np.pi * ((np.outer(j, j)) % DFT2) / DFT2
    fr, fi = np.cos(th), -np.sin(th)
    m_fwd = np.block([[fr, -fi], [fi, fr]])
    m_inv = np.block([[fr, fi], [-fi, fr]])
    hh = n1 // 2
    k1 = np.arange(n1)[None, :, None]
    t1 = np.arange(hh)[None, None, :]
    t2 = np.arange(DFT2)[:, None, None]
    ph = 2.0 * np.pi * ((k1 * (DFT2 * t1 + t2)) % nn) / nn
    ar, ai = np.cos(ph), -np.sin(ph)
    a_fwd = np.concatenate([np.concatenate([ar, -ai], axis=2), np.concatenate([ai, ar], axis=2)], axis=1)
    a_real = np.concatenate([ar, ai], axis=1)
    t1r = np.where(t2 >= 1, n1 - 1 - t1, (n1 - t1) % n1)
    phr = 2.0 * np.pi * ((k1 * (DFT2 * t1r + t2)) % nn) / nn
    a_rev = np.concatenate([np.cos(phr), -np.sin(phr)], axis=1)
    a_rev[0, :, 0] = 0.0
    pht = np.transpose(ph, (0, 2, 1))
    cr, ci = np.cos(pht) / nn, np.sin(pht) / nn
    a_inv = np.concatenate([np.concatenate([cr, -ci], axis=2), np.concatenate([ci, cr], axis=2)], axis=1)
    f32 = lambda a: np.ascontiguousarray(a, dtype=np.float32)
    return dict(m_fwd=f32(m_fwd), m_inv=f32(m_inv), a_fwd=f32(a_fwd), a_real=f32(a_real), a_rev=f32(a_rev),
                a_inv=f32(a_inv))


def _fft_a_body(x_ref, m_ref, o_ref, *, tj):
    n1 = o_ref.shape[2]
    for e in range(tj):
        x = jnp.concatenate([x_ref[0, :, e, :], x_ref[1, :, e, :]], axis=0).astype(BF16)
        a = jnp.dot(m_ref[e], x, preferred_element_type=F32)
        o_ref[0, e] = a[:n1]
        o_ref[1, e] = a[n1:]


def _fft_a(x4, m, *, n1, tj):
    c = x4.shape[3]
    return pl.pallas_call(
        functools.partial(_fft_a_body, tj=tj),
        out_shape=jax.ShapeDtypeStruct((2, DFT2, n1, c), F32),
        grid=(DFT2 // tj,),
        in_specs=[pl.BlockSpec((2, n1 // 2, tj, c), lambda j: (0, 0, j, 0)),
                  pl.BlockSpec((tj, 2 * n1, n1), lambda j: (j, 0, 0))],
        out_specs=pl.BlockSpec((2, tj, n1, c), lambda j: (0, j, 0, 0)),
        compiler_params=_cparams(("parallel",)),
        name="fft_stage_a",
    )(x4, m)


def _fft_b_body(a_ref, mf_ref, mi_ref, g_ref, o_ref, *, kg):
    for kk in range(kg):
        x = jnp.concatenate([a_ref[0, :, kk, :], a_ref[1, :, kk, :]], axis=0).astype(BF16)
        xf = jnp.dot(mf_ref[...], x, preferred_element_type=F32)
        xr, xi = xf[:DFT2], xf[DFT2:]
        gr, gi = g_ref[0, kk].astype(F32), g_ref[1, kk].astype(F32)
        y = jnp.concatenate([xr * gr - xi * gi, xr * gi + xi * gr], axis=0).astype(BF16)
        bf = jnp.dot(mi_ref[...], y, preferred_element_type=F32)
        o_ref[0, kk] = bf[:DFT2]
        o_ref[1, kk] = bf[DFT2:]


def _fft_b(a, g, order, tabs, *, n1):
    c = a.shape[3]
    kg = SUBLANES
    blk = pl.BlockSpec((2, kg, DFT2, c), lambda k: (0, k, 0, 0))
    mat = pl.BlockSpec((2 * DFT2, 2 * DFT2), lambda k: (0, 0))
    return pl.pallas_call(
        functools.partial(_fft_b_body, kg=kg),
        out_shape=jax.ShapeDtypeStruct((2, n1, DFT2, c), F32),
        grid=(n1 // kg,),
        in_specs=[pl.BlockSpec((2, DFT2, kg, c), lambda k: (0, 0, k, 0)), mat, mat,
                  pl.BlockSpec((2, kg, DFT2, c), lambda k: (0, k, 0, order))],
        out_specs=blk,
        compiler_params=_cparams(("parallel",)),
        name="fft_stage_b",
    )(a, tabs["m_fwd"], tabs["m_inv"], g)


def _fft_bf_body(a_ref, mf_ref, o_ref, *, kg):
    for kk in range(kg):
        x = jnp.concatenate([a_ref[0, :, kk, :], a_ref[1, :, kk, :]], axis=0).astype(BF16)
        xf = jnp.dot(mf_ref[...], x, preferred_element_type=F32)
        o_ref[0, kk] = xf[:DFT2].astype(BF16)
        o_ref[1, kk] = xf[DFT2:].astype(BF16)


def _fft_b_forward(a, tabs, *, n1):
    c = a.shape[3]
    kg = SUBLANES
    return pl.pallas_call(
        functools.partial(_fft_bf_body, kg=kg),
        out_shape=jax.ShapeDtypeStruct((2, n1, DFT2, c), BF16),
        grid=(n1 // kg,),
        in_specs=[pl.BlockSpec((2, DFT2, kg, c), lambda k: (0, 0, k, 0)),
                  pl.BlockSpec((2 * DFT2, 2 * DFT2), lambda k: (0, 0))],
        out_specs=pl.BlockSpec((2, kg, DFT2, c), lambda k: (0, k, 0, 0)),
        compiler_params=_cparams(("parallel",)),
        name="fft_stage_b_filter",
    )(a, tabs["m_fwd"])


def _fft_ai_body(b_ref, m_ref, v_ref, x_ref, bias_ref, o_ref, *, tj):
    h = o_ref.shape[1]
    for e in range(tj):
        b = jnp.concatenate([b_ref[0, :, e, :], b_ref[1, :, e, :]], axis=0).astype(BF16)
        y = jnp.dot(m_ref[e], b, preferred_element_type=F32)
        o_ref[0, :, e, :] = y[:h]
        o_ref[1, :, e, :] = y[h:]
    o_ref[...] = x_ref[...] * (o_ref[...] + v_ref[...] * bias_ref[...])


def _fft_a_inv(b, m, v4, xm4, bias, *, n1, tj):
    c = b.shape[3]
    half = pl.BlockSpec((2, n1 // 2, tj, c), lambda j: (0, 0, j, 0))
    return pl.pallas_call(
        functools.partial(_fft_ai_body, tj=tj),
        out_shape=jax.ShapeDtypeStruct((2, n1 // 2, DFT2, c), F32),
        grid=(DFT2 // tj,),
        in_specs=[pl.BlockSpec((2, n1, tj, c), lambda j: (0, 0, j, 0)),
                  pl.BlockSpec((tj, n1, 2 * n1), lambda j: (j, 0, 0)),
                  half, half, pl.BlockSpec((1, c), lambda j: (0, 0))],
        out_specs=half,
        compiler_params=_cparams(("parallel",)),
        name="fft_stage_a_inv",
    )(b, m, v4, xm4, bias)


def _filt_a_body(hj_ref, hz_ref, hr_ref, s_ref, mf_ref, mr_ref, o_ref, *, tj):
    n1 = o_ref.shape[2]
    s = s_ref[...]
    for e in range(tj):
        src = hz_ref if e == 0 else hr_ref
        me = 0 if e == 0 else tj - e
        for o in range(2):
            c0 = 2 * D_GROUP * o
            inv = 1.0 / (s[:, c0:c0 + D_GROUP] + s[:, c0 + D_GROUP:c0 + 2 * D_GROUP] + EPS)
            lsl = slice(o * D_GROUP, (o + 1) * D_GROUP)
            hf = hj_ref[e, :, lsl].astype(BF16)
            hb = src[me, :, lsl].astype(BF16)
            a = (jnp.dot(mf_ref[e], hf, preferred_element_type=F32)
                 + jnp.dot(mr_ref[e], hb, preferred_element_type=F32)) * inv
            o_ref[0, e, :, lsl] = a[:n1]
            o_ref[1, e, :, lsl] = a[n1:]


def _filt_a(hf, hb, colsum, tabs, *, n1):
    tj = 8
    nj = DFT2 // tj
    wd = 2 * D_GROUP
    shp = (DFT2, n1 // 2, wd)
    blk = lambda fn: pl.BlockSpec((tj, n1 // 2, wd), fn)
    mat = pl.BlockSpec((tj, 2 * n1, n1 // 2), lambda j: (j, 0, 0))
    return pl.pallas_call(
        functools.partial(_filt_a_body, tj=tj),
        out_shape=jax.ShapeDtypeStruct((2, DFT2, n1, wd), F32),
        grid=(nj,),
        in_specs=[blk(lambda j: (j, 0, 0)), blk(lambda j: ((nj - j) % nj, 0, 0)), blk(lambda j: (nj - 1 - j, 0, 0)),
                  pl.BlockSpec((1, HY_FILTER_CH), lambda j: (0, 0)), mat, mat],
        out_specs=pl.BlockSpec((2, tj, n1, wd), lambda j: (0, j, 0, 0)),
        compiler_params=_cparams(("parallel",)),
        name="filter_stage_a",
    )(hf.reshape(shp), hb.reshape(shp), hb.reshape(shp), colsum, tabs["a_real"], tabs["a_rev"])


def _hyena_long(v, x1, x2, hf, hb, colsum, bias, *, seq):
    n1 = 2 * seq // DFT2
    tabs = {k: jnp.asarray(a).astype(BF16) for k, a in _dft_tables(n1).items()}
    tj = 8
    fa = _filt_a(hf, hb, colsum, tabs, n1=n1)
    g = _fft_b_forward(fa, tabs, n1=n1)
    shp = (2, n1 // 2, DFT2, D_GROUP)
    z = v.reshape(shp)
    for order, xm in ((0, x1.reshape(shp)), (1, x2.reshape(shp))):
        a = _fft_a(z, tabs["a_fwd"], n1=n1, tj=tj)
        b = _fft_b(a, g, order, tabs, n1=n1)
        z = _fft_a_inv(b, tabs["a_inv"], z, xm, bias[order].reshape(1, D_GROUP), n1=n1, tj=tj)
    return z.reshape(2 * seq, D_GROUP)


def _hyena_ctx_body(v_ref, x1_ref, x2_ref, hf_ref, hb_ref, s_ref, bias_ref, cm_ref, sm_ref, ct_ref, st_ref, o_ref):
    n = v_ref.shape[1]
    cm, sm, ct, st = cm_ref[...], sm_ref[...], ct_ref[...], st_ref[...]
    s = s_ref[...]
    row0 = _iota((n, D_GROUP), 0) == 0
    dot = lambda a, b: jnp.dot(a, b.astype(BF16), preferred_element_type=F32)
    zr, zi = v_ref[0], v_ref[1]
    for order, xm in ((0, x1_ref), (1, x2_ref)):
        c0 = 2 * D_GROUP * order
        inv = 1.0 / (s[:, c0:c0 + D_GROUP] + s[:, c0 + D_GROUP:c0 + 2 * D_GROUP] + EPS)
        lsl = slice(order * D_GROUP, (order + 1) * D_GROUP)
        hf = hf_ref[:, lsl] * inv
        hb = jnp.where(row0, 0.0, hb_ref[:, lsl] * inv)
        gr = dot(cm, hf + hb)
        gi = dot(sm, hb - hf)
        xr = dot(cm, zr) + dot(sm, zi)
        xi = dot(cm, zi) - dot(sm, zr)
        yr = xr * gr - xi * gi
        yi = xr * gi + xi * gr
        cr = dot(ct, yr) - dot(st, yi)
        ci = dot(ct, yi) + dot(st, yr)
        bias = bias_ref[order:order + 1, :]
        zr = xm[0] * (cr + zr * bias)
        zi = xm[1] * (ci + zi * bias)
    o_ref[0] = zr
    o_ref[1] = zi


def _hyena_ctx(v, x1, x2, hf, hb, colsum, bias, *, seq):
    nn = 2 * seq
    k = np.arange(nn)
    t = np.arange(seq)
    th = 2.0 * np.pi * (np.outer(k, t) % nn) / nn
    cm, sm = np.cos(th), np.sin(th)
    consts = [jnp.asarray(a, F32).astype(BF16) for a in (cm, sm, cm.T / nn, sm.T / nn)]
    shp = (2, seq, D_GROUP)
    full3 = pl.BlockSpec(shp, lambda i: (0, 0, 0))
    f2 = lambda a: pl.BlockSpec(a.shape, lambda i: (0, 0))
    args = [hf, hb, colsum, bias] + consts
    out = pl.pallas_call(
        _hyena_ctx_body,
        out_shape=jax.ShapeDtypeStruct(shp, F32),
        grid=(1,),
        in_specs=[full3, full3, full3] + [f2(a) for a in args],
        out_specs=full3,
        compiler_params=_cparams(("arbitrary",)),
        name="hyena_ctx",
    )(v.reshape(shp), x1.reshape(shp), x2.reshape(shp), *args)
    return out.reshape(2 * seq, D_GROUP)


def _mod_rows(mod_l, rows, first, count):
    m = mod_l[rows, first:first + count, :]
    return jnp.pad(m, ((0, 0), (0, 8 - count), (0, 0)))


def _split_w_in(w_in):
    wb = w_in.astype(BF16)
    parts = dict(k=wb[:, COL_K:COL_V], v=wb[:, COL_V:COL_GF], q=wb[:, COL_Q:COL_R], r=wb[:, COL_R:COL_POOL],
                 pool=wb[:, COL_POOL:COL_HY], hy=wb[:, COL_HY:COL_CONV], conv=wb[:, COL_CONV:P_IN])
    gate = jnp.pad(wb[:, COL_GF:COL_Q], ((0, 0), (0, LANE - 2 * GLA_LOWRANK)))
    return parts, gate


def _mix(u, p, *, bsz, seq, is_ctx, states, filt):
    sf, sb = states
    cpb = min(8, seq // CHUNK)
    if is_ctx:
        pool = _pool1d(u["pool"], p["pool_wbd"], p["pool_scale"], bsz=bsz, seq=seq)
    else:
        pool = _pool2d(u["pool"], p["pool_wbd"], p["pool_scale"], bsz=bsz, seq=seq)
    tmc = min(seq, 1024)
    v, x1, x2 = _hy_short(u["hy"], p["hy_short_w"], p["hy_short_b"], seq=seq, tm=tmc)
    if is_ctx:
        hy = _hyena_ctx(v, x1, x2, *filt, p["hy_bias"], seq=seq)
    else:
        hy = _hyena_long(v, x1, x2, *filt, p["hy_bias"], seq=seq)
    gla = _gla_read(u["k"], u["v"], u["q"], u["r"], u["lf"], u["lb"], sf, sb, p["gla_ng"],
                    bsz=bsz, seq=seq, cpb=cpb)
    conv = _conformer(u["conv"], p["conv_dw_w"], p["conv_dw_b"], p["conv_ln_g"], p["conv_ln_b"], seq=seq, tm=tmc)
    return [pool, hy, gla, conv]


def kernel(x, c, ctx, c_ctx, ada_w, ada_b, ffn1_norm, ffn1_wi, ffn1_wo, mix_norm, w_in, w_out, pool_w, pool_scale, hy_short_w, hy_short_b, hy_w1, hy_b1, hy_w2, hy_b2, hy_w3, hy_deltas, hy_bias, gla_gw_f, gla_gb_f, gla_gw_b, gla_gb_b, gla_norm, conv_dw_w, conv_dw_b, conv_ln_g, conv_ln_b, ffn2_norm, ffn2_wi, ffn2_wo, final_norm):
    bsz, seq, d = x.shape
    clen = ctx.shape[1]
    depth = ada_w.shape[0]
    assert bsz == 2, "the Hyena transform packs exactly two batch rows into one complex signal"
    xs = x.reshape(bsz * seq, d)
    cs = ctx.reshape(bsz * clen, d)
    cc = jnp.zeros((8, d), F32).at[0:bsz].set(c).at[bsz].set(c_ctx)
    mod = _modulation(cc, ada_w, ada_b).reshape(depth, 8, N_MOD, d)
    xrows = np.arange(bsz)
    crows = np.full((1,), bsz)
    tmx = 512
    tps_x = seq // tmx
    tmc = bsz * clen
    zero_state = jnp.zeros((bsz, GLA_HEADS * GLA_DV, GLA_QK), F32)

    for l in range(depth):
        last = l == depth - 1
        ml = mod[l]
        wparts, wgate = _split_w_in(w_in[l])
        gw = jnp.zeros((LANE, 2 * GLA_QK), F32)
        gw = gw.at[0:GLA_LOWRANK, 0:GLA_QK].set(gla_gw_f[l])
        gw = gw.at[GLA_LOWRANK:2 * GLA_LOWRANK, GLA_QK:].set(gla_gw_b[l])
        gb = jnp.concatenate([gla_gb_f[l], gla_gb_b[l]]).reshape(1, -1)
        wbd = jnp.zeros((D_GROUP, D_GROUP), F32)
        for gi in range(len(POOL_WINDOWS)):
            wbd = wbd.at[gi * POOL_CH:(gi + 1) * POOL_CH, gi * POOL_CH:(gi + 1) * POOL_CH].set(pool_w[l, gi])
        p = dict(pool_wbd=wbd.astype(BF16), pool_scale=pool_scale[l].reshape(1, -1),
                 hy_short_w=hy_short_w[l], hy_short_b=hy_short_b[l], hy_bias=hy_bias[l],
                 gla_ng=jnp.tile(gla_norm[l], GLA_HEADS).reshape(1, -1),
                 conv_dw_w=conv_dw_w[l], conv_dw_b=conv_dw_b[l], conv_ln_g=conv_ln_g[l], conv_ln_b=conv_ln_b[l])
        wi1, wo1 = ffn1_wi[l].astype(BF16), ffn1_wo[l].astype(BF16)
        wi2, wo2 = ffn2_wi[l].astype(BF16), ffn2_wo[l].astype(BF16)
        wout = w_out[l].astype(BF16)
        names = ["k", "v", "q", "r", "pool", "hy", "conv"]

        xs, *outs = _ffn_proj(xs, _mod_rows(ml, xrows, 0, 5), ffn1_norm[l], wi1, wo1, mix_norm[l],
                              [wparts[nm] for nm in names], wgate, gw, gb, tm=tmx, tiles_per_seq=tps_x)
        ux = dict(zip(names + ["lf", "lb"], outs))
        cnames = ["k", "v"] if last else names
        cs, *outs = _ffn_proj(cs, _mod_rows(ml, crows, 0, 5), ffn1_norm[l], wi1, wo1, mix_norm[l],
                              [wparts[nm] for nm in cnames], wgate, gw, gb, tm=tmc, tiles_per_seq=1)
        uc = dict(zip(cnames + ["lf", "lb"], outs))

        ccpb = clen // CHUNK
        sfc, sbc, finf, finb = _gla_states(uc["k"], uc["v"], uc["lf"], uc["lb"], zero_state, zero_state,
                                           bsz=bsz, seq=clen, cpb=ccpb)
        sfx, sbx, _, _ = _gla_states(ux["k"], ux["v"], ux["lf"], ux["lb"], finf, finb, bsz=bsz, seq=seq, cpb=8)

        filt = _hy_filter(seq, hy_w1[l], hy_b1[l], hy_w2[l], hy_b2[l], hy_w3[l], hy_deltas[l], t2_major=True)
        mixed = _mix(ux, p, bsz=bsz, seq=seq, is_ctx=False, states=(sfx, sbx), filt=filt)
        xs = _out_ffn(xs, _mod_rows(ml, xrows, 5, 4), mixed, wout, ffn2_norm[l], wi2, wo2, tm=tmx,
                      tiles_per_seq=tps_x, final_g=final_norm if last else None)
        if not last:
            filt_c = _hy_filter(clen, hy_w1[l], hy_b1[l], hy_w2[l], hy_b2[l], hy_w3[l], hy_deltas[l],
                                t2_major=False)
            mixed = _mix(uc, p, bsz=bsz, seq=clen, is_ctx=True, states=(sfc, sbc), filt=filt_c)
            cs = _out_ffn(cs, _mod_rows(ml, crows, 5, 4), mixed, wout, ffn2_norm[l], wi2, wo2, tm=tmc,
                          tiles_per_seq=1)
    return xs.reshape(bsz, seq, d)
```

```python
import functools
import math

import numpy as np
import jax
import jax.numpy as jnp
from jax import lax
from jax.experimental import pallas as pl
from jax.experimental.pallas import tpu as pltpu

F32 = jnp.float32
BF16 = jnp.bfloat16
HI = lax.Precision.HIGHEST

D_MODEL = 1024
GRID_W = 64
D_GROUP = 256
D_FF = 2816
N_MOD = 9
EPS = 1e-6
POOL_WINDOWS = (2, 4, 8, 16)
POOL_CH = 64
HY_BANDS = 16
HY_EMB = 1 + 2 * HY_BANDS
HY_FFN = 64
HY_FILTER_CH = 4 * D_GROUP
GLA_HEADS = 4
GLA_DK = 32
GLA_DV = 64
GLA_QK = 128
GLA_LOWRANK = 16
GLA_TAU = 16.0
CHUNK = 64
CONV_WIDTH = 31
HY_SHORT = 3

COL_K = 0
COL_V = COL_K + GLA_QK
COL_GF = COL_V + D_GROUP
COL_GB = COL_GF + GLA_LOWRANK
COL_Q = COL_GB + GLA_LOWRANK
COL_R = COL_Q + GLA_QK
COL_POOL = COL_R + D_GROUP
COL_HY = COL_POOL + D_GROUP
COL_CONV = COL_HY + 3 * D_GROUP
P_IN = COL_CONV + 2 * D_GROUP

LANE = 128
DFT2 = 128
MIB = 1024 * 1024


def _cparams(sem, vmem_mib=None):
    kw = dict(dimension_semantics=sem)
    if vmem_mib is not None:
        kw["vmem_limit_bytes"] = vmem_mib * MIB
    return pltpu.CompilerParams(**kw)


def _silu(x):
    return x * jax.nn.sigmoid(x)


def _rms_mod(h, g, m):
    y = h * lax.rsqrt(jnp.mean(h * h, axis=-1, keepdims=True) + EPS) * g
    return y * (1.0 + m[1:2, :]) + m[0:1, :]


def _iota(shape, dim):
    return lax.broadcasted_iota(jnp.int32, shape, dim)


MOD_ROWS = 3


def _mod_body(at_ref, w_ref, b_ref, o_ref):
    d, tn = w_ref.shape[1], w_ref.shape[2]
    at = _silu(at_ref[...])
    cols = [jnp.broadcast_to(at[:, r:r + 1], (d, LANE)).reshape(d // SUBLANES, SUBLANES, LANE)
            for r in range(MOD_ROWS)]
    o_ref[0] = jnp.zeros((8, tn), F32) + b_ref[0]
    for j in range(tn // LANE):
        lsl = slice(j * LANE, (j + 1) * LANE)
        w3 = w_ref[0, :, lsl].reshape(d // SUBLANES, SUBLANES, LANE)
        for r in range(MOD_ROWS):
            part = jnp.sum(w3 * cols[r], axis=0)
            o_ref[0, r:r + 1, lsl] += jnp.sum(part, axis=0, keepdims=True)


def _modulation(cc, ada_w, ada_b):
    nl, d, nm = ada_w.shape
    tn = 1024
    return pl.pallas_call(
        _mod_body,
        out_shape=jax.ShapeDtypeStruct((nl, 8, nm), F32),
        grid=(nl, nm // tn),
        in_specs=[pl.BlockSpec((d, 8), lambda l, j: (0, 0)),
                  pl.BlockSpec((1, d, tn), lambda l, j: (l, 0, j)),
                  pl.BlockSpec((1, 1, tn), lambda l, j: (l, 0, j))],
        out_specs=pl.BlockSpec((1, 8, tn), lambda l, j: (l, 0, j)),
        compiler_params=_cparams(("parallel", "parallel")),
        name="adaln_mod",
    )(cc.T, ada_w, ada_b.reshape(nl, 1, nm))


FF_CHUNK = 256
ROW_SUB = 256


def _swiglu(xn, wi_ref, wo_ref, hm_ref, rows):
    ff = wo_ref.shape[0]
    for c in range(0, ff, FF_CHUNK):
        a = jnp.dot(xn, wi_ref[:, c:c + FF_CHUNK], preferred_element_type=F32)
        g = jnp.dot(xn, wi_ref[:, ff + c:ff + c + FF_CHUNK], preferred_element_type=F32)
        hm_ref[rows, c:c + FF_CHUNK] = (_silu(g) * a).astype(BF16)
    return jnp.dot(hm_ref[rows, :], wo_ref[...], preferred_element_type=F32)


def _resident(shape):
    nd = len(shape)
    return pl.BlockSpec(shape, lambda i: (0,) * nd, pipeline_mode=pl.Buffered(1))


def _resident_layer(shape, layer):
    nd = len(shape)
    return pl.BlockSpec((None,) + tuple(shape), lambda i: (layer,) + (0,) * nd, pipeline_mode=pl.Buffered(1))


def _gla_log_decay(xn, wg_ref, gw_ref, gb_ref):
    ug = jnp.dot(xn, wg_ref[...], preferred_element_type=F32)
    a = jnp.dot(ug, gw_ref[...], precision=HI, preferred_element_type=F32) + gb_ref[...]
    return (jnp.minimum(a, 0.0) - jnp.log(1.0 + jnp.exp(-jnp.abs(a)))) * (1.0 / GLA_TAU)


def _ffn_proj_body(h_ref, m_ref, g1_ref, wi_ref, wo_ref, g2_ref, *rest, nparts):
    w_refs = rest[:nparts]
    wg_ref, gw_ref, gb_ref, tl_ref = rest[nparts:nparts + 4]
    x_ref = rest[nparts + 4]
    o_refs = rest[nparts + 5:2 * nparts + 7]
    hm_ref = rest[-1]
    m = m_ref[0]
    for r0 in range(0, h_ref.shape[0], ROW_SUB):
        rows = slice(r0, r0 + ROW_SUB)
        h = h_ref[rows, :]
        xn = _rms_mod(h, g1_ref[...], m[0:2]).astype(BF16)
        x1 = h + (0.5 * m[2:3, :]) * _swiglu(xn, wi_ref, wo_ref, hm_ref, rows)
        x_ref[rows, :] = x1
        xn2 = _rms_mod(x1, g2_ref[...], m[3:5]).astype(BF16)
        for w_ref, o_ref in zip(w_refs, o_refs[:nparts]):
            o_ref[rows, :] = jnp.dot(xn2, w_ref[...], preferred_element_type=F32).astype(o_ref.dtype)
        ls = _gla_log_decay(xn2, wg_ref, gw_ref, gb_ref)
        pre = _chunk_sum(tl_ref[...], *_split_bf16(ls))
        o_refs[nparts][rows, :] = pre[:, :GLA_QK]
        pb = pre[:, GLA_QK:]
        tot = jnp.concatenate([jnp.broadcast_to(pb[c0 + CHUNK - 1:c0 + CHUNK, :], (CHUNK, GLA_QK))
                               for c0 in range(0, ROW_SUB, CHUNK)], axis=0)
        o_refs[nparts + 1][rows, :] = tot - pb + ls[:, GLA_QK:]


def _ffn_proj(h, m, g1, wi, wo, g2, w_parts, w_gate, gw, gb, *, layer, tm, tiles_per_seq):
    n, d = h.shape
    ff = wo.shape[1]
    nparts = len(w_parts)
    widths = [w.shape[1] for w in w_parts]
    tok = lambda wd: pl.BlockSpec((tm, wd), lambda i: (i, 0))
    in_specs = [tok(d), pl.BlockSpec((1, 8, d), lambda i: (i // tiles_per_seq, 0, 0)), _resident((1, d)),
                _resident_layer((d, 2 * ff), layer), _resident_layer((ff, d), layer), _resident((1, d))]
    in_specs += [_resident((d, wd)) for wd in widths]
    in_specs += [_resident((d, LANE)), _resident((LANE, 2 * GLA_QK)), _resident((1, 2 * GLA_QK)),
                 _resident((ROW_SUB, ROW_SUB))]
    tl = jnp.asarray(_chunk_matrices(ROW_SUB)[0]).astype(BF16)
    out_shape = [jax.ShapeDtypeStruct((n, d), F32)]
    out_shape += [jax.ShapeDtypeStruct((n, wd), BF16) for wd in widths]
    out_shape += [jax.ShapeDtypeStruct((n, GLA_QK), F32)] * 2
    return pl.pallas_call(
        functools.partial(_ffn_proj_body, nparts=nparts),
        out_shape=out_shape,
        grid=(n // tm,),
        in_specs=in_specs,
        out_specs=[tok(d)] + [tok(wd) for wd in widths] + [tok(GLA_QK)] * 2,
        scratch_shapes=[pltpu.VMEM((tm, ff), BF16)],
        compiler_params=_cparams(("parallel",), 56),
        name="ffn1_in_proj",
    )(h, m, g1.reshape(1, d), wi, wo, g2.reshape(1, d), *w_parts, w_gate, gw, gb, tl)


def _out_ffn_body(x_ref, m_ref, p_ref, hy_ref, gl_ref, cv_ref, wout_ref, g_ref, wi_ref, wo_ref, fn_ref, o_ref,
                  hm_ref, *, final):
    m = m_ref[0]
    for r0 in range(0, x_ref.shape[0], ROW_SUB):
        rows = slice(r0, r0 + ROW_SUB)
        acc = jnp.dot(p_ref[rows, :].astype(BF16), wout_ref[0:D_GROUP, :], preferred_element_type=F32)
        acc += jnp.dot(hy_ref[rows, :].astype(BF16), wout_ref[D_GROUP:2 * D_GROUP, :], preferred_element_type=F32)
        acc += jnp.dot(gl_ref[rows, :].astype(BF16), wout_ref[2 * D_GROUP:3 * D_GROUP, :],
                       preferred_element_type=F32)
        acc += jnp.dot(cv_ref[rows, :].astype(BF16), wout_ref[3 * D_GROUP:, :], preferred_element_type=F32)
        x2 = x_ref[rows, :] + m[0:1, :] * acc
        xn = _rms_mod(x2, g_ref[...], m[1:3]).astype(BF16)
        out = x2 + (0.5 * m[3:4, :]) * _swiglu(xn, wi_ref, wo_ref, hm_ref, rows)
        if final:
            out = out * lax.rsqrt(jnp.mean(out * out, axis=-1, keepdims=True) + EPS) * fn_ref[...]
        o_ref[rows, :] = out


def _out_ffn(x, m, parts, w_out, g, wi, wo, *, layer, tm, tiles_per_seq, final_g=None):
    n, d = x.shape
    ff = wo.shape[1]
    final = final_g is not None
    fg = final_g if final else g
    tok = lambda wd: pl.BlockSpec((tm, wd), lambda i: (i, 0))
    return pl.pallas_call(
        functools.partial(_out_ffn_body, final=final),
        out_shape=jax.ShapeDtypeStruct((n, d), F32),
        grid=(n // tm,),
        in_specs=[tok(d), pl.BlockSpec((1, 8, d), lambda i: (i // tiles_per_seq, 0, 0)),
                  tok(D_GROUP), tok(D_GROUP), tok(D_GROUP), tok(D_GROUP), _resident_layer((d, d), layer),
                  _resident((1, d)), _resident_layer((d, 2 * ff), layer), _resident_layer((ff, d), layer),
                  _resident((1, d))],
        out_specs=tok(d),
        scratch_shapes=[pltpu.VMEM((tm, ff), BF16)],
        compiler_params=_cparams(("parallel",), 56),
        name="out_proj_ffn2",
    )(x, m, *parts, w_out, g.reshape(1, d), wi, wo, fg.reshape(1, d))


@functools.lru_cache(maxsize=None)
def _chunk_matrices(bt):
    r = np.arange(bt)
    same = (r[:, None] // CHUNK) == (r[None, :] // CHUNK)
    low = same & (r[:, None] >= r[None, :])
    up = same & (r[:, None] <= r[None, :])
    return tuple(np.ascontiguousarray(m, dtype=np.float32) for m in (low, up, same))


def _chunk_sum(mat, hi, lo):
    return jnp.dot(mat, hi, preferred_element_type=F32) + jnp.dot(mat, lo, preferred_element_type=F32)


def _gla_state_body(kf, vf, bf, kb, vb, bb, s0f, s0b, sf_o, sb_o, ff_o, fb_o, stf, stb, *, cpb):
    i = pl.program_id(1)

    @pl.when(i == 0)
    def _():
        stf[...] = s0f[0]
        stb[...] = s0b[0]

    sshape = (GLA_HEADS * GLA_DV, GLA_QK)
    bmask = (_iota(sshape, 0) >> 6) == (_iota(sshape, 1) >> 5)
    tn_dims = (((0,), (0,)), ((), ()))

    def direction(k_ref, v_ref, b_ref, last, st, s_o, order):
        s = st[...]
        for ci in order:
            sl = slice(ci * CHUNK, (ci + 1) * CHUNK)
            b = b_ref[sl, :]
            tot = b[last:last + 1, :]
            kd = (k_ref[sl, :] * jnp.exp(tot - b)).astype(BF16)
            upd = lax.dot_general(v_ref[sl, :].astype(BF16), kd, tn_dims, preferred_element_type=F32)
            s_o[0, ci] = s.astype(BF16)
            s = s * jnp.exp(tot) + jnp.where(bmask, upd, 0.0)
        st[...] = s

    direction(kf, vf, bf, CHUNK - 1, stf, sf_o, range(cpb))
    direction(kb, vb, bb, 0, stb, sb_o, range(cpb - 1, -1, -1))
    ff_o[0] = stf[...]
    fb_o[0] = stb[...]


def _gla_states(k, v, lf, lb, s0f, s0b, *, bsz, seq, cpb):
    bt = cpb * CHUNK
    nb = seq // bt
    nc = seq // CHUNK
    srow = GLA_HEADS * GLA_DV

    def tf(b, i):
        return (b * nb + i, 0)

    def tb(b, i):
        return (b * nb + nb - 1 - i, 0)

    sblk = pl.BlockSpec((1, srow, GLA_QK), lambda b, i: (b, 0, 0))
    return pl.pallas_call(
        functools.partial(_gla_state_body, cpb=cpb),
        out_shape=[jax.ShapeDtypeStruct((bsz, nc, srow, GLA_QK), BF16),
                   jax.ShapeDtypeStruct((bsz, nc, srow, GLA_QK), BF16),
                   jax.ShapeDtypeStruct((bsz, srow, GLA_QK), F32),
                   jax.ShapeDtypeStruct((bsz, srow, GLA_QK), F32)],
        grid=(bsz, nb),
        in_specs=[pl.BlockSpec((bt, GLA_QK), tf), pl.BlockSpec((bt, D_GROUP), tf), pl.BlockSpec((bt, GLA_QK), tf),
                  pl.BlockSpec((bt, GLA_QK), tb), pl.BlockSpec((bt, D_GROUP), tb), pl.BlockSpec((bt, GLA_QK), tb),
                  sblk, sblk],
        out_specs=[pl.BlockSpec((1, cpb, srow, GLA_QK), lambda b, i: (b, i, 0, 0)),
                   pl.BlockSpec((1, cpb, srow, GLA_QK), lambda b, i: (b, nb - 1 - i, 0, 0)),
                   sblk, sblk],
        scratch_shapes=[pltpu.VMEM((srow, GLA_QK), F32), pltpu.VMEM((srow, GLA_QK), F32)],
        compiler_params=_cparams(("parallel", "arbitrary")),
        name="gla_states",
    )(k, v, lf, k, v, lb, s0f, s0b)


def _gla_read_body(k_ref, v_ref, q_ref, r_ref, bf_ref, bb_ref, sf_ref, sb_ref, ng_ref, o_ref, *, cpb):
    hrows = GLA_HEADS * CHUNK
    cpos = _iota((hrows, CHUNK), 0) & (CHUNK - 1)
    ccol = _iota((hrows, CHUNK), 1)
    lowm = cpos >= ccol
    upm = cpos <= ccol
    hq = (_iota((hrows, GLA_QK), 0) >> 6) == (_iota((hrows, GLA_QK), 1) >> 5)
    ho = (_iota((hrows, D_GROUP), 0) >> 6) == (_iota((hrows, D_GROUP), 1) >> 6)
    bavg = jnp.where((_iota((D_GROUP, D_GROUP), 0) >> 6) == (_iota((D_GROUP, D_GROUP), 1) >> 6),
                     1.0 / GLA_DV, 0.0).astype(BF16)
    nt_dims = (((1,), (1,)), ((), ()))

    bf_ = bf_ref[...]
    bb_ = bb_ref[...]
    qs = q_ref[...].astype(F32) * (GLA_DK ** -0.5)
    kk = k_ref[...]
    qef = (qs * jnp.exp(bf_)).astype(BF16)
    qeb = (qs * jnp.exp(bb_)).astype(BF16)
    kef = (kk * jnp.exp(-bf_)).astype(BF16)
    keb = (kk * jnp.exp(-bb_)).astype(BF16)
    vb16 = v_ref[...].astype(BF16)
    zero = jnp.zeros((), BF16)
    for ci in range(cpb):
        sl = slice(ci * CHUNK, (ci + 1) * CHUNK)
        qf4 = jnp.where(hq, jnp.concatenate([qef[sl]] * GLA_HEADS, axis=0), zero)
        qb4 = jnp.where(hq, jnp.concatenate([qeb[sl]] * GLA_HEADS, axis=0), zero)
        af = lax.dot_general(qf4, kef[sl], nt_dims, preferred_element_type=F32)
        ab = lax.dot_general(qb4, keb[sl], nt_dims, preferred_element_type=F32)
        att = (jnp.where(lowm, af, 0.0) + jnp.where(upm, ab, 0.0)).astype(BF16)
        oall = jnp.dot(att, vb16[sl], preferred_element_type=F32)
        om = jnp.where(ho, oall, 0.0)
        o = om[0:CHUNK] + om[CHUNK:2 * CHUNK] + om[2 * CHUNK:3 * CHUNK] + om[3 * CHUNK:4 * CHUNK]
        qcat = jnp.concatenate([qef[sl], qeb[sl]], axis=1)
        scat = jnp.concatenate([sf_ref[0, ci], sb_ref[0, ci]], axis=1)
        o_ref[sl, :] = o + lax.dot_general(qcat, scat, nt_dims, preferred_element_type=F32)
    o = o_ref[...]
    ms = _chunk_sum_rhs(o * o, bavg)
    o_ref[...] = o * lax.rsqrt(ms + EPS) * ng_ref[...] * _silu(r_ref[...].astype(F32))


def _chunk_sum_rhs(x, mat):
    hi, lo = _split_bf16(x)
    return jnp.dot(hi, mat, preferred_element_type=F32) + jnp.dot(lo, mat, preferred_element_type=F32)


def _gla_read(k, v, q, r, lf, lb, sf, sb, ng, *, bsz, seq, cpb):
    bt = cpb * CHUNK
    nb = seq // bt
    srow = GLA_HEADS * GLA_DV
    n = bsz * seq

    def tk(i):
        return (i, 0)

    sspec = pl.BlockSpec((1, cpb, srow, GLA_QK), lambda i: (i // nb, i % nb, 0, 0))
    return pl.pallas_call(
        functools.partial(_gla_read_body, cpb=cpb),
        out_shape=jax.ShapeDtypeStruct((n, D_GROUP), F32),
        grid=(bsz * nb,),
        in_specs=[pl.BlockSpec((bt, GLA_QK), tk), pl.BlockSpec((bt, D_GROUP), tk),
                  pl.BlockSpec((bt, GLA_QK), lambda i: (i, 1)), pl.BlockSpec((bt, D_GROUP), tk),
                  pl.BlockSpec((bt, GLA_QK), tk), pl.BlockSpec((bt, GLA_QK), tk),
                  sspec, sspec, pl.BlockSpec((1, D_GROUP), lambda i: (0, 0))],
        out_specs=pl.BlockSpec((bt, D_GROUP), tk),
        compiler_params=_cparams(("parallel",)),
        name="gla_readout",
    )(k, v, q, r, lf, lb, sf, sb, ng)


def _box_matrix(n, w):
    pos = np.arange(n)
    lo = np.clip(pos - w // 2, 0, n)
    hi = np.clip(pos - w // 2 + w, 0, n)
    col = np.arange(n)[None, :]
    return ((col >= lo[:, None]) & (col < hi[:, None])).astype(np.float32)


def _lane_windows(shape):
    w = jnp.left_shift(2, _iota(shape, 1) >> 6)
    return w, w >> 1


def _box_count(pos, w, half, n):
    return jnp.minimum(pos - half + w, n) - jnp.maximum(pos - half, 0)


def _split_bf16(x):
    hi = x.astype(BF16)
    lo = (x - hi.astype(F32)).astype(BF16)
    return hi, lo


def _pool2d_body(cur_ref, prev_ref, next_ref, pc_ref, w_ref, sc_ref, o_ref, ycol, *, tiles, rows):
    i = pl.program_id(1)
    tm = cur_ref.shape[0]
    hb = prev_ref.shape[0]
    sub = 2 * GRID_W
    pflag = jnp.where(i > 0, 1.0, 0.0)
    nflag = jnp.where(i < tiles - 1, 1.0, 0.0)

    def colpool(x):
        halves = []
        for half in range(2):
            lsl = slice(half * LANE, (half + 1) * LANE)
            ys = [jnp.dot(pc_ref[wi], x[:, lsl], preferred_element_type=F32) for wi in (2 * half, 2 * half + 1)]
            lane = _iota((sub, LANE), 1)
            halves.append(jnp.where(lane < POOL_CH, ys[0], ys[1]))
        return jnp.concatenate(halves, axis=1)

    for s in range(hb // sub):
        ycol[s * sub:(s + 1) * sub, :] = colpool(prev_ref[s * sub:(s + 1) * sub, :]) * pflag
    for s in range(tm // sub):
        ycol[hb + s * sub:hb + (s + 1) * sub, :] = colpool(cur_ref[s * sub:(s + 1) * sub, :])
    for s in range(hb // sub):
        ycol[hb + tm + s * sub:hb + tm + (s + 1) * sub, :] = colpool(next_ref[s * sub:(s + 1) * sub, :]) * nflag

    rc = 256
    wl, half = _lane_windows((rc, D_GROUP))
    for r0 in range(0, tm, rc):
        def rows_at(dd):
            return ycol[hb + r0 + GRID_W * dd:hb + r0 + GRID_W * dd + rc, :]

        acc = rows_at(-1) + rows_at(0)
        z2 = acc
        acc = acc + rows_at(-2) + rows_at(1)
        z4 = acc
        for dd in (-4, -3, 2, 3):
            acc = acc + rows_at(dd)
        z8 = acc
        for dd in (-8, -7, -6, -5, 4, 5, 6, 7):
            acc = acc + rows_at(dd)
        z16 = acc
        z = jnp.where(wl == 2, z2, jnp.where(wl == 4, z4, jnp.where(wl == 8, z8, z16)))
        tok = _iota((rc, D_GROUP), 0) + (i * tm + r0)
        rcnt = _box_count(tok >> 6, wl, half, rows)
        ccnt = _box_count(tok & (GRID_W - 1), wl, half, GRID_W)
        pooled = z / (rcnt * ccnt).astype(F32)
        dlt = (pooled - cur_ref[r0:r0 + rc, :]).astype(BF16)
        o_ref[r0:r0 + rc, :] = jnp.dot(dlt, w_ref[...], preferred_element_type=F32) * sc_ref[...]


def _pool2d(u, wbd, scale, *, bsz, seq):
    tm = 1024
    hb = 512
    tiles = seq // tm
    r = tm // hb
    nhb = seq // hb
    pc = np.stack([np.kron(np.eye(2, dtype=np.float32), _box_matrix(GRID_W, w)) for w in POOL_WINDOWS])
    return pl.pallas_call(
        functools.partial(_pool2d_body, tiles=tiles, rows=seq // GRID_W),
        out_shape=jax.ShapeDtypeStruct((bsz * seq, D_GROUP), F32),
        grid=(bsz, tiles),
        in_specs=[pl.BlockSpec((tm, D_GROUP), lambda b, i: (b * tiles + i, 0)),
                  pl.BlockSpec((hb, D_GROUP), lambda b, i: (b * nhb + jnp.maximum(i * r - 1, 0), 0)),
                  pl.BlockSpec((hb, D_GROUP), lambda b, i: (b * nhb + jnp.minimum(i * r + r, nhb - 1), 0)),
                  pl.BlockSpec((4, 2 * GRID_W, 2 * GRID_W), lambda b, i: (0, 0, 0)),
                  pl.BlockSpec((D_GROUP, D_GROUP), lambda b, i: (0, 0)),
                  pl.BlockSpec((1, D_GROUP), lambda b, i: (0, 0))],
        out_specs=pl.BlockSpec((tm, D_GROUP), lambda b, i: (b * tiles + i, 0)),
        scratch_shapes=[pltpu.VMEM((tm + 2 * hb, D_GROUP), F32)],
        compiler_params=_cparams(("parallel", "parallel")),
        name="pool2d",
    )(u, u, u, jnp.asarray(pc, BF16), wbd, scale)


def _pool1d_body(x_ref, p_ref, w_ref, sc_ref, o_ref):
    x = x_ref[...]
    n = x.shape[0]
    ys = [jnp.dot(p_ref[wi], x, preferred_element_type=F32) for wi in range(4)]
    wl, half = _lane_windows((n, D_GROUP))
    z = jnp.where(wl == 2, ys[0], jnp.where(wl == 4, ys[1], jnp.where(wl == 8, ys[2], ys[3])))
    cnt = _box_count(_iota((n, D_GROUP), 0), wl, half, n)
    dlt = (z / cnt.astype(F32) - x).astype(BF16)
    o_ref[...] = jnp.dot(dlt, w_ref[...], preferred_element_type=F32) * sc_ref[...]


def _pool1d(u, wbd, scale, *, bsz, seq):
    pm = np.stack([_box_matrix(seq, w) for w in POOL_WINDOWS])
    return pl.pallas_call(
        _pool1d_body,
        out_shape=jax.ShapeDtypeStruct((bsz * seq, D_GROUP), F32),
        grid=(bsz,),
        in_specs=[pl.BlockSpec((seq, D_GROUP), lambda b: (b, 0)),
                  pl.BlockSpec((4, seq, seq), lambda b: (0, 0, 0)),
                  pl.BlockSpec((D_GROUP, D_GROUP), lambda b: (0, 0)),
                  pl.BlockSpec((1, D_GROUP), lambda b: (0, 0))],
        out_specs=pl.BlockSpec((seq, D_GROUP), lambda b: (b, 0)),
        compiler_params=_cparams(("parallel",)),
        name="pool1d",
    )(u, jnp.asarray(pm, BF16), wbd, scale)


def _fill_halo(buf, cur, prev, nxt, i, tps, hb, tm, pre):
    first = (i % tps) == 0
    last = (i % tps) == tps - 1
    buf[0:hb, :] = jnp.where(first, 0.0, pre(prev[...]))
    buf[hb:hb + tm, :] = pre(cur[...])
    buf[hb + tm:hb + tm + hb, :] = jnp.where(last, 0.0, pre(nxt[...]))


SUBLANES = 8


def _tap_phases(taps, off):
    return sorted({(off + j) % SUBLANES for j in range(taps)} - {0})


def _fill_phases(sh, buf, phases):
    rows = sh.shape[1]
    for slot, s in enumerate(phases):
        sh[slot, :, :] = buf[s:s + rows, :]


def _dwconv(buf, sh, phases, w_ref, r0, rc, taps, off):
    acc = None
    for j in range(taps):
        s, q = (off + j) % SUBLANES, (off + j) // SUBLANES
        lo = r0 + SUBLANES * q
        src = buf[lo:lo + rc, :] if s == 0 else sh[phases.index(s), lo:lo + rc, :]
        term = src * w_ref[j:j + 1, :]
        acc = term if acc is None else acc + term
    return acc


def _conf_body(cur, prev, nxt, w_ref, b_ref, lg_ref, lb_ref, o_ref, buf, sh, *, tps):
    i = pl.program_id(0)
    tm = cur.shape[0]
    hb = prev.shape[0]

    def glu(u):
        u = u.astype(F32)
        return u[:, :D_GROUP] * jax.nn.sigmoid(u[:, D_GROUP:])

    _fill_halo(buf, cur, prev, nxt, i, tps, hb, tm, glu)
    rc = 128
    off = hb - (CONV_WIDTH - 1) // 2
    phases = _tap_phases(CONV_WIDTH, off)
    _fill_phases(sh, buf, phases)
    for r0 in range(0, tm, rc):
        h = _dwconv(buf, sh, phases, w_ref, r0, rc, CONV_WIDTH, off) + b_ref[...]
        mu = jnp.mean(h, axis=-1, keepdims=True)
        hc = h - mu
        var = jnp.mean(hc * hc, axis=-1, keepdims=True)
        o_ref[r0:r0 + rc, :] = _silu(hc * lax.rsqrt(var + EPS) * lg_ref[...] + lb_ref[...])


def _halo_specs(tm, hb, width, nrows):
    r = tm // hb
    nhb = nrows // hb
    return [pl.BlockSpec((tm, width), lambda i: (i, 0)),
            pl.BlockSpec((hb, width), lambda i: (jnp.maximum(i * r - 1, 0), 0)),
            pl.BlockSpec((hb, width), lambda i: (jnp.minimum(i * r + r, nhb - 1), 0))]


def _conformer(u, w, b, lg, lb, *, seq, tm):
    n = u.shape[0]
    hb = 16
    nph = len(_tap_phases(CONV_WIDTH, hb - (CONV_WIDTH - 1) // 2))
    vec = pl.BlockSpec((1, D_GROUP), lambda i: (0, 0))
    return pl.pallas_call(
        functools.partial(_conf_body, tps=seq // tm),
        out_shape=jax.ShapeDtypeStruct((n, D_GROUP), F32),
        grid=(n // tm,),
        in_specs=_halo_specs(tm, hb, 2 * D_GROUP, n) + [pl.BlockSpec((CONV_WIDTH, D_GROUP), lambda i: (0, 0)),
                                                        vec, vec, vec],
        out_specs=pl.BlockSpec((tm, D_GROUP), lambda i: (i, 0)),
        scratch_shapes=[pltpu.VMEM((tm + 2 * hb, D_GROUP), F32),
                        pltpu.VMEM((nph, tm + 2 * hb - SUBLANES, D_GROUP), F32)],
        compiler_params=_cparams(("parallel",)),
        name="conformer_conv",
    )(u, u, u, w, b.reshape(1, -1), lg.reshape(1, -1), lb.reshape(1, -1))


def _short_body(cur, prev, nxt, w_ref, b_ref, v_ref, x1_ref, x2_ref, buf, sh, *, tps):
    i = pl.program_id(0)
    tm = cur.shape[0]
    hb = prev.shape[0]
    _fill_halo(buf, cur, prev, nxt, i, tps, hb, tm, lambda u: u.astype(F32))
    rc = 128
    off = hb - (HY_SHORT - 1) // 2
    phases = _tap_phases(HY_SHORT, off)
    _fill_phases(sh, buf, phases)
    for r0 in range(0, tm, rc):
        uc = _dwconv(buf, sh, phases, w_ref, r0, rc, HY_SHORT, off) + b_ref[...]
        v_ref[r0:r0 + rc, :] = uc[:, :D_GROUP]
        x1_ref[r0:r0 + rc, :] = uc[:, D_GROUP:2 * D_GROUP]
        x2_ref[r0:r0 + rc, :] = uc[:, 2 * D_GROUP:]


def _hy_short(u, w, b, *, seq, tm):
    n = u.shape[0]
    hb = 16
    wd = 3 * D_GROUP
    nph = len(_tap_phases(HY_SHORT, hb - (HY_SHORT - 1) // 2))
    ospec = pl.BlockSpec((tm, D_GROUP), lambda i: (i, 0))
    return pl.pallas_call(
        functools.partial(_short_body, tps=seq // tm),
        out_shape=[jax.ShapeDtypeStruct((n, D_GROUP), F32)] * 3,
        grid=(n // tm,),
        in_specs=_halo_specs(tm, hb, wd, n) + [pl.BlockSpec((HY_SHORT, wd), lambda i: (0, 0)),
                                               pl.BlockSpec((1, wd), lambda i: (0, 0))],
        out_specs=[ospec, ospec, ospec],
        scratch_shapes=[pltpu.VMEM((tm + 2 * hb, wd), F32),
                        pltpu.VMEM((nph, tm + 2 * hb - SUBLANES, wd), F32)],
        compiler_params=_cparams(("parallel",)),
        name="hyena_short_conv",
    )(u, u, u, w, b.reshape(1, -1))


def _filter_features(n):
    i = np.arange(n, dtype=np.float64)
    t = np.linspace(0.0, 1.0, n, dtype=np.float32).astype(np.float64)
    wpos = ((2.0 * math.pi / n) * np.arange(n, dtype=np.float32)).astype(np.float32)
    bands = np.linspace(1e-4, HY_BANDS - 1, HY_BANDS, dtype=np.float32)
    arg = (bands[None, :] * wpos[:, None]).astype(np.float32).astype(np.float64)
    z = np.zeros((n, 64), np.float32)
    z[:, 0] = t
    z[:, 1:1 + HY_BANDS] = np.cos(arg)
    z[:, 1 + HY_BANDS:HY_EMB] = -np.sin(arg)
    del i
    return z


def _filter_body(z_ref, w1_ref, b1_ref, w2_ref, b2_ref, w3a_ref, w3b_ref, d_ref, hf_ref, hb_ref, s_ref):
    i = pl.program_id(0)
    half = z_ref.shape[0]
    z = z_ref[...]
    h = jnp.sin(jnp.dot(z, w1_ref[...], precision=HI, preferred_element_type=F32) + b1_ref[...])
    h = jnp.sin(jnp.dot(h, w2_ref[...], precision=HI, preferred_element_type=F32) + b2_ref[...])
    absd = jnp.abs(d_ref[...])
    h = h.astype(BF16)
    tot = None
    for part, (w3_ref, tcol) in enumerate(((w3a_ref, 0), (w3b_ref, 64))):
        hp = jnp.dot(h, w3_ref[...], preferred_element_type=F32)
        hp = hp * jnp.exp(-z[:, tcol:tcol + 1] * absd)
        rows = slice(part * half, (part + 1) * half)
        for o in range(2):
            c0 = 2 * D_GROUP * o
            hf_ref[rows, o * D_GROUP:(o + 1) * D_GROUP] = hp[:, c0:c0 + D_GROUP]
            hb_ref[rows, o * D_GROUP:(o + 1) * D_GROUP] = hp[:, c0 + D_GROUP:c0 + 2 * D_GROUP]
        part_sum = jnp.sum(jnp.abs(hp), axis=0, keepdims=True)
        tot = part_sum if tot is None else tot + part_sum

    @pl.when(i == 0)
    def _():
        s_ref[...] = jnp.zeros_like(s_ref)

    s_ref[...] += tot


def _hy_filter(n, w1, b1, w2, b2, w3, deltas, *, t2_major):
    tm = min(n, 512)
    half = tm // 2
    z = _filter_features(n)
    if t2_major:
        z = z.reshape(n // DFT2, DFT2, 64).transpose(1, 0, 2).reshape(n, 64)
    zt = z.reshape(n // tm, 2, half, 64)
    z2 = jnp.asarray(np.concatenate([zt[:, 0], zt[:, 1]], axis=-1).reshape(n // 2, 2 * 64))
    w1p = jnp.zeros((64, HY_FFN), F32).at[:HY_EMB].set(w1)
    zero = jnp.zeros((64, HY_FFN), F32)
    bd = lambda w: jnp.concatenate([jnp.concatenate([w, zero], axis=1), jnp.concatenate([zero, w], axis=1)], axis=0)
    w3 = w3.astype(BF16)
    zero3 = jnp.zeros_like(w3)
    full = lambda shape: pl.BlockSpec(shape, lambda i: (0, 0))
    tok = pl.BlockSpec((tm, 2 * D_GROUP), lambda i: (i, 0))
    return pl.pallas_call(
        _filter_body,
        out_shape=[jax.ShapeDtypeStruct((n, 2 * D_GROUP), F32), jax.ShapeDtypeStruct((n, 2 * D_GROUP), F32),
                   jax.ShapeDtypeStruct((1, HY_FILTER_CH), F32)],
        grid=(n // tm,),
        in_specs=[pl.BlockSpec((half, 2 * 64), lambda i: (i, 0)), full((2 * 64, 2 * HY_FFN)), full((1, 2 * HY_FFN)),
                  full((2 * HY_FFN, 2 * HY_FFN)), full((1, 2 * HY_FFN)), full((2 * HY_FFN, HY_FILTER_CH)),
                  full((2 * HY_FFN, HY_FILTER_CH)), full((1, HY_FILTER_CH))],
        out_specs=[tok, tok, full((1, HY_FILTER_CH))],
        compiler_params=_cparams(("arbitrary",)),
        name="hyena_filter_mlp",
    )(z2, bd(w1p), jnp.tile(b1.reshape(1, -1), (1, 2)), bd(w2), jnp.tile(b2.reshape(1, -1), (1, 2)),
      jnp.concatenate([w3, zero3], axis=0), jnp.concatenate([zero3, w3], axis=0), deltas.reshape(1, -1))


@functools.lru_cache(maxsize=None)
def _dft_tables(n1):
    nn = n1 * DFT2
    j = np.arange(DFT2)
    th = 2.0 * np.pi * ((np.outer(j, j)) % DFT2) / DFT2
    fr, fi = np.cos(th), -np.sin(th)
    m_fwd = np.block([[fr, -fi], [fi, fr]])
    m_inv = np.block([[fr, fi], [-fi, fr]])
    hh = n1 // 2
    k1 = np.arange(n1)[None, :, None]
    t1 = np.arange(hh)[None, None, :]
    t2 = np.arange(DFT2)[:, None, None]
    ph = 2.0 * np.pi * ((k1 * (DFT2 * t1 + t2)) % nn) / nn
    ar, ai = np.cos(ph), -np.sin(ph)
    a_fwd = np.concatenate([np.concatenate([ar, -ai], axis=2), np.concatenate([ai, ar], axis=2)], axis=1)
    a_real = np.concatenate([ar, ai], axis=1)
    t1r = np.where(t2 >= 1, n1 - 1 - t1, (n1 - t1) % n1)
    phr = 2.0 * np.pi * ((k1 * (DFT2 * t1r + t2)) % nn) / nn
    a_rev = np.concatenate([np.cos(phr), -np.sin(phr)], axis=1)
    a_rev[0, :, 0] = 0.0
    pht = np.transpose(ph, (0, 2, 1))
    cr, ci = np.cos(pht) / nn, np.sin(pht) / nn
    a_inv = np.concatenate([np.concatenate([cr, -ci], axis=2), np.concatenate([ci, cr], axis=2)], axis=1)
    f32 = lambda a: np.ascontiguousarray(a, dtype=np.float32)
    return dict(m_fwd=f32(m_fwd), m_inv=f32(m_inv), a_fwd=f32(a_fwd), a_real=f32(a_real), a_rev=f32(a_rev),
                a_inv=f32(a_inv))


def _fft_a_body(x_ref, m_ref, o_ref, *, tj):
    n1 = o_ref.shape[2]
    for e in range(tj):
        x = jnp.concatenate([x_ref[0, :, e, :], x_ref[1, :, e, :]], axis=0).astype(BF16)
        a = jnp.dot(m_ref[e], x, preferred_element_type=F32)
        o_ref[0, e] = a[:n1]
        o_ref[1, e] = a[n1:]


def _fft_a(x4, m, *, n1, tj):
    c = x4.shape[3]
    return pl.pallas_call(
        functools.partial(_fft_a_body, tj=tj),
        out_shape=jax.ShapeDtypeStruct((2, DFT2, n1, c), F32),
        grid=(DFT2 // tj,),
        in_specs=[pl.BlockSpec((2, n1 // 2, tj, c), lambda j: (0, 0, j, 0)),
                  pl.BlockSpec((tj, 2 * n1, n1), lambda j: (j, 0, 0))],
        out_specs=pl.BlockSpec((2, tj, n1, c), lambda j: (0, j, 0, 0)),
        compiler_params=_cparams(("parallel",)),
        name="fft_stage_a",
    )(x4, m)


def _fft_b_body(a_ref, mf_ref, mi_ref, g_ref, o_ref, *, kg):
    for kk in range(kg):
        x = jnp.concatenate([a_ref[0, :, kk, :], a_ref[1, :, kk, :]], axis=0).astype(BF16)
        xf = jnp.dot(mf_ref[...], x, preferred_element_type=F32)
        xr, xi = xf[:DFT2], xf[DFT2:]
        gr, gi = g_ref[0, kk].astype(F32), g_ref[1, kk].astype(F32)
        y = jnp.concatenate([xr * gr - xi * gi, xr * gi + xi * gr], axis=0).astype(BF16)
        bf = jnp.dot(mi_ref[...], y, preferred_element_type=F32)
        o_ref[0, kk] = bf[:DFT2]
        o_ref[1, kk] = bf[DFT2:]


def _fft_b(a, g, order, tabs, *, n1):
    c = a.shape[3]
    kg = SUBLANES
    blk = pl.BlockSpec((2, kg, DFT2, c), lambda k: (0, k, 0, 0))
    mat = pl.BlockSpec((2 * DFT2, 2 * DFT2), lambda k: (0, 0))
    return pl.pallas_call(
        functools.partial(_fft_b_body, kg=kg),
        out_shape=jax.ShapeDtypeStruct((2, n1, DFT2, c), F32),
        grid=(n1 // kg,),
        in_specs=[pl.BlockSpec((2, DFT2, kg, c), lambda k: (0, 0, k, 0)), mat, mat,
                  pl.BlockSpec((2, kg, DFT2, c), lambda k: (0, k, 0, order))],
        out_specs=blk,
        compiler_params=_cparams(("parallel",)),
        name="fft_stage_b",
    )(a, tabs["m_fwd"], tabs["m_inv"], g)


def _fft_bf_body(a_ref, mf_ref, o_ref, *, kg):
    for kk in range(kg):
        x = jnp.concatenate([a_ref[0, :, kk, :], a_ref[1, :, kk, :]], axis=0).astype(BF16)
        xf = jnp.dot(mf_ref[...], x, preferred_element_type=F32)
        o_ref[0, kk] = xf[:DFT2].astype(BF16)
        o_ref[1, kk] = xf[DFT2:].astype(BF16)


def _fft_b_forward(a, tabs, *, n1):
    c = a.shape[3]
    kg = SUBLANES
    return pl.pallas_call(
        functools.partial(_fft_bf_body, kg=kg),
        out_shape=jax.ShapeDtypeStruct((2, n1, DFT2, c), BF16),
        grid=(n1 // kg,),
        in_specs=[pl.BlockSpec((2, DFT2, kg, c), lambda k: (0, 0, k, 0)),
                  pl.BlockSpec((2 * DFT2, 2 * DFT2), lambda k: (0, 0))],
        out_specs=pl.BlockSpec((2, kg, DFT2, c), lambda k: (0, k, 0, 0)),
        compiler_params=_cparams(("parallel",)),
        name="fft_stage_b_filter",
    )(a, tabs["m_fwd"])


def _fft_ai_body(b_ref, m_ref, v_ref, x_ref, bias_ref, o_ref, *, tj):
    h = o_ref.shape[1]
    for e in range(tj):
        b = jnp.concatenate([b_ref[0, :, e, :], b_ref[1, :, e, :]], axis=0).astype(BF16)
        y = jnp.dot(m_ref[e], b, preferred_element_type=F32)
        o_ref[0, :, e, :] = y[:h]
        o_ref[1, :, e, :] = y[h:]
    o_ref[...] = x_ref[...] * (o_ref[...] + v_ref[...] * bias_ref[...])


def _fft_a_inv(b, m, v4, xm4, bias, *, n1, tj):
    c = b.shape[3]
    half = pl.BlockSpec((2, n1 // 2, tj, c), lambda j: (0, 0, j, 0))
    return pl.pallas_call(
        functools.partial(_fft_ai_body, tj=tj),
        out_shape=jax.ShapeDtypeStruct((2, n1 // 2, DFT2, c), F32),
        grid=(DFT2 // tj,),
        in_specs=[pl.BlockSpec((2, n1, tj, c), lambda j: (0, 0, j, 0)),
                  pl.BlockSpec((tj, n1, 2 * n1), lambda j: (j, 0, 0)),
                  half, half, pl.BlockSpec((1, c), lambda j: (0, 0))],
        out_specs=half,
        compiler_params=_cparams(("parallel",)),
        name="fft_stage_a_inv",
    )(b, m, v4, xm4, bias)


def _filt_a_body(hj_ref, hz_ref, hr_ref, s_ref, mf_ref, mr_ref, o_ref, *, tj):
    n1 = o_ref.shape[2]
    s = s_ref[...]
    for e in range(tj):
        src = hz_ref if e == 0 else hr_ref
        me = 0 if e == 0 else tj - e
        for o in range(2):
            c0 = 2 * D_GROUP * o
            inv = 1.0 / (s[:, c0:c0 + D_GROUP] + s[:, c0 + D_GROUP:c0 + 2 * D_GROUP] + EPS)
            lsl = slice(o * D_GROUP, (o + 1) * D_GROUP)
            hf = hj_ref[e, :, lsl].astype(BF16)
            hb = src[me, :, lsl].astype(BF16)
            a = (jnp.dot(mf_ref[e], hf, preferred_element_type=F32)
                 + jnp.dot(mr_ref[e], hb, preferred_element_type=F32)) * inv
            o_ref[0, e, :, lsl] = a[:n1]
            o_ref[1, e, :, lsl] = a[n1:]


def _filt_a(hf, hb, colsum, tabs, *, n1):
    tj = 8
    nj = DFT2 // tj
    wd = 2 * D_GROUP
    shp = (DFT2, n1 // 2, wd)
    blk = lambda fn: pl.BlockSpec((tj, n1 // 2, wd), fn)
    mat = pl.BlockSpec((tj, 2 * n1, n1 // 2), lambda j: (j, 0, 0))
    return pl.pallas_call(
        functools.partial(_filt_a_body, tj=tj),
        out_shape=jax.ShapeDtypeStruct((2, DFT2, n1, wd), F32),
        grid=(nj,),
        in_specs=[blk(lambda j: (j, 0, 0)), blk(lambda j: ((nj - j) % nj, 0, 0)), blk(lambda j: (nj - 1 - j, 0, 0)),
                  pl.BlockSpec((1, HY_FILTER_CH), lambda j: (0, 0)), mat, mat],
        out_specs=pl.BlockSpec((2, tj, n1, wd), lambda j: (0, j, 0, 0)),
        compiler_params=_cparams(("parallel",)),
        name="filter_stage_a",
    )(hf.reshape(shp), hb.reshape(shp), hb.reshape(shp), colsum, tabs["a_real"], tabs["a_rev"])


def _hyena_long(v, x1, x2, hf, hb, colsum, bias, *, seq):
    n1 = 2 * seq // DFT2
    tabs = {k: jnp.asarray(a).astype(BF16) for k, a in _dft_tables(n1).items()}
    tj = 8
    fa = _filt_a(hf, hb, colsum, tabs, n1=n1)
    g = _fft_b_forward(fa, tabs, n1=n1)
    shp = (2, n1 // 2, DFT2, D_GROUP)
    z = v.reshape(shp)
    for order, xm in ((0, x1.reshape(shp)), (1, x2.reshape(shp))):
        a = _fft_a(z, tabs["a_fwd"], n1=n1, tj=tj)
        b = _fft_b(a, g, order, tabs, n1=n1)
        z = _fft_a_inv(b, tabs["a_inv"], z, xm, bias[order].reshape(1, D_GROUP), n1=n1, tj=tj)
    return z.reshape(2 * seq, D_GROUP)


def _hyena_ctx_body(v_ref, x1_ref, x2_ref, hf_ref, hb_ref, s_ref, bias_ref, cm_ref, sm_ref, ct_ref, st_ref, o_ref):
    n = v_ref.shape[1]
    cm, sm, ct, st = cm_ref[...], sm_ref[...], ct_ref[...], st_ref[...]
    s = s_ref[...]
    row0 = _iota((n, D_GROUP), 0) == 0
    dot = lambda a, b: jnp.dot(a, b.astype(BF16), preferred_element_type=F32)
    zr, zi = v_ref[0], v_ref[1]
    for order, xm in ((0, x1_ref), (1, x2_ref)):
        c0 = 2 * D_GROUP * order
        inv = 1.0 / (s[:, c0:c0 + D_GROUP] + s[:, c0 + D_GROUP:c0 + 2 * D_GROUP] + EPS)
        lsl = slice(order * D_GROUP, (order + 1) * D_GROUP)
        hf = hf_ref[:, lsl] * inv
        hb = jnp.where(row0, 0.0, hb_ref[:, lsl] * inv)
        gr = dot(cm, hf + hb)
        gi = dot(sm, hb - hf)
        xr = dot(cm, zr) + dot(sm, zi)
        xi = dot(cm, zi) - dot(sm, zr)
        yr = xr * gr - xi * gi
        yi = xr * gi + xi * gr
        cr = dot(ct, yr) - dot(st, yi)
        ci = dot(ct, yi) + dot(st, yr)
        bias = bias_ref[order:order + 1, :]
        zr = xm[0] * (cr + zr * bias)
        zi = xm[1] * (ci + zi * bias)
    o_ref[0] = zr
    o_ref[1] = zi


def _hyena_ctx(v, x1, x2, hf, hb, colsum, bias, *, seq):
    nn = 2 * seq
    k = np.arange(nn)
    t = np.arange(seq)
    th = 2.0 * np.pi * (np.outer(k, t) % nn) / nn
    cm, sm = np.cos(th), np.sin(th)
    consts = [jnp.asarray(a, F32).astype(BF16) for a in (cm, sm, cm.T / nn, sm.T / nn)]
    shp = (2, seq, D_GROUP)
    full3 = pl.BlockSpec(shp, lambda i: (0, 0, 0))
    f2 = lambda a: pl.BlockSpec(a.shape, lambda i: (0, 0))
    args = [hf, hb, colsum, bias] + consts
    out = pl.pallas_call(
        _hyena_ctx_body,
        out_shape=jax.ShapeDtypeStruct(shp, F32),
        grid=(1,),
        in_specs=[full3, full3, full3] + [f2(a) for a in args],
        out_specs=full3,
        compiler_params=_cparams(("arbitrary",)),
        name="hyena_ctx",
    )(v.reshape(shp), x1.reshape(shp), x2.reshape(shp), *args)
    return out.reshape(2 * seq, D_GROUP)


def _mod_rows(mod_l, rows, first, count):
    m = mod_l[rows, first:first + count, :]
    return jnp.pad(m, ((0, 0), (0, 8 - count), (0, 0)))


def _split_w_in(w_in):
    wb = w_in.astype(BF16)
    parts = dict(kq=jnp.concatenate([wb[:, COL_K:COL_V], wb[:, COL_Q:COL_R]], axis=1), v=wb[:, COL_V:COL_GF],
                 r=wb[:, COL_R:COL_POOL], pool=wb[:, COL_POOL:COL_HY], hy=wb[:, COL_HY:COL_CONV],
                 conv=wb[:, COL_CONV:P_IN])
    gate = jnp.pad(wb[:, COL_GF:COL_Q], ((0, 0), (0, LANE - 2 * GLA_LOWRANK)))
    return parts, gate


def _mix(u, p, *, bsz, seq, is_ctx, states, filt):
    sf, sb = states
    cpb = min(8, seq // CHUNK)
    if is_ctx:
        pool = _pool1d(u["pool"], p["pool_wbd"], p["pool_scale"], bsz=bsz, seq=seq)
    else:
        pool = _pool2d(u["pool"], p["pool_wbd"], p["pool_scale"], bsz=bsz, seq=seq)
    tmc = min(seq, 1024)
    v, x1, x2 = _hy_short(u["hy"], p["hy_short_w"], p["hy_short_b"], seq=seq, tm=tmc)
    if is_ctx:
        hy = _hyena_ctx(v, x1, x2, *filt, p["hy_bias"], seq=seq)
    else:
        hy = _hyena_long(v, x1, x2, *filt, p["hy_bias"], seq=seq)
    gla = _gla_read(u["kq"], u["v"], u["kq"], u["r"], u["lf"], u["lb"], sf, sb, p["gla_ng"],
                    bsz=bsz, seq=seq, cpb=cpb)
    conv = _conformer(u["conv"], p["conv_dw_w"], p["conv_dw_b"], p["conv_ln_g"], p["conv_ln_b"], seq=seq, tm=tmc)
    return [pool, hy, gla, conv]


def kernel(x, c, ctx, c_ctx, ada_w, ada_b, ffn1_norm, ffn1_wi, ffn1_wo, mix_norm, w_in, w_out, pool_w, pool_scale, hy_short_w, hy_short_b, hy_w1, hy_b1, hy_w2, hy_b2, hy_w3, hy_deltas, hy_bias, gla_gw_f, gla_gb_f, gla_gw_b, gla_gb_b, gla_norm, conv_dw_w, conv_dw_b, conv_ln_g, conv_ln_b, ffn2_norm, ffn2_wi, ffn2_wo, final_norm):
    bsz, seq, d = x.shape
    clen = ctx.shape[1]
    depth = ada_w.shape[0]
    assert bsz == 2, "the Hyena transform packs exactly two batch rows into one complex signal"
    xs = x.reshape(bsz * seq, d)
    cs = ctx.reshape(bsz * clen, d)
    cc = jnp.zeros((8, d), F32).at[0:bsz].set(c).at[bsz].set(c_ctx)
    mod = _modulation(cc, ada_w, ada_b).reshape(depth, 8, N_MOD, d)
    xrows = np.arange(bsz)
    crows = np.full((1,), bsz)
    tmx = 512
    tps_x = seq // tmx
    tmc = bsz * clen
    zero_state = jnp.zeros((bsz, GLA_HEADS * GLA_DV, GLA_QK), F32)
    wi1, wo1, wi2, wo2, wout = (w.astype(BF16) for w in (ffn1_wi, ffn1_wo, ffn2_wi, ffn2_wo, w_out))

    for l in range(depth):
        last = l == depth - 1
        ml = mod[l]
        wparts, wgate = _split_w_in(w_in[l])
        gw = jnp.zeros((LANE, 2 * GLA_QK), F32)
        gw = gw.at[0:GLA_LOWRANK, 0:GLA_QK].set(gla_gw_f[l])
        gw = gw.at[GLA_LOWRANK:2 * GLA_LOWRANK, GLA_QK:].set(gla_gw_b[l])
        gb = jnp.concatenate([gla_gb_f[l], gla_gb_b[l]]).reshape(1, -1)
        wbd = jnp.zeros((D_GROUP, D_GROUP), F32)
        for gi in range(len(POOL_WINDOWS)):
            wbd = wbd.at[gi * POOL_CH:(gi + 1) * POOL_CH, gi * POOL_CH:(gi + 1) * POOL_CH].set(pool_w[l, gi])
        p = dict(pool_wbd=wbd.astype(BF16), pool_scale=pool_scale[l].reshape(1, -1),
                 hy_short_w=hy_short_w[l], hy_short_b=hy_short_b[l], hy_bias=hy_bias[l],
                 gla_ng=jnp.tile(gla_norm[l], GLA_HEADS).reshape(1, -1),
                 conv_dw_w=conv_dw_w[l], conv_dw_b=conv_dw_b[l], conv_ln_g=conv_ln_g[l], conv_ln_b=conv_ln_b[l])
        names = ["kq", "v", "r", "pool", "hy", "conv"]

        xs, *outs = _ffn_proj(xs, _mod_rows(ml, xrows, 0, 5), ffn1_norm[l], wi1, wo1, mix_norm[l],
                              [wparts[nm] for nm in names], wgate, gw, gb, layer=l, tm=tmx, tiles_per_seq=tps_x)
        ux = dict(zip(names + ["lf", "lb"], outs))
        cnames = ["kq", "v"] if last else names
        cs, *outs = _ffn_proj(cs, _mod_rows(ml, crows, 0, 5), ffn1_norm[l], wi1, wo1, mix_norm[l],
                              [wparts[nm] for nm in cnames], wgate, gw, gb, layer=l, tm=tmc, tiles_per_seq=1)
        uc = dict(zip(cnames + ["lf", "lb"], outs))

        ccpb = clen // CHUNK
        sfc, sbc, finf, finb = _gla_states(uc["kq"], uc["v"], uc["lf"], uc["lb"], zero_state, zero_state,
                                           bsz=bsz, seq=clen, cpb=ccpb)
        sfx, sbx, _, _ = _gla_states(ux["kq"], ux["v"], ux["lf"], ux["lb"], finf, finb, bsz=bsz, seq=seq,
                                     cpb=16)

        filt = _hy_filter(seq, hy_w1[l], hy_b1[l], hy_w2[l], hy_b2[l], hy_w3[l], hy_deltas[l], t2_major=True)
        mixed = _mix(ux, p, bsz=bsz, seq=seq, is_ctx=False, states=(sfx, sbx), filt=filt)
        xs = _out_ffn(xs, _mod_rows(ml, xrows, 5, 4), mixed, wout, ffn2_norm[l], wi2, wo2, layer=l, tm=tmx,
                      tiles_per_seq=tps_x, final_g=final_norm if last else None)
        if not last:
            filt_c = _hy_filter(clen, hy_w1[l], hy_b1[l], hy_w2[l], hy_b2[l], hy_w3[l], hy_deltas[l],
                                t2_major=False)
            mixed = _mix(uc, p, bsz=bsz, seq=clen, is_ctx=True, states=(sfc, sbc), filt=filt_c)
            cs = _out_ffn(cs, _mod_rows(ml, crows, 5, 4), mixed, wout, ffn2_norm[l], wi2, wo2, layer=l, tm=tmc,
                          tiles_per_seq=1)
    return xs.reshape(bsz, seq, d)
```

```python
import functools
import math

import numpy as np
import jax
import jax.numpy as jnp
from jax import lax
from jax.experimental import pallas as pl
from jax.experimental.pallas import tpu as pltpu

F32 = jnp.float32
BF16 = jnp.bfloat16
HI = lax.Precision.HIGHEST

D_MODEL = 1024
GRID_W = 64
D_GROUP = 256
D_FF = 2816
N_MOD = 9
EPS = 1e-6
POOL_WINDOWS = (2, 4, 8, 16)
POOL_CH = 64
HY_BANDS = 16
HY_EMB = 1 + 2 * HY_BANDS
HY_FFN = 64
HY_FILTER_CH = 4 * D_GROUP
GLA_HEADS = 4
GLA_DK = 32
GLA_DV = 64
GLA_QK = 128
GLA_LOWRANK = 16
GLA_TAU = 16.0
CHUNK = 64
CONV_WIDTH = 31
HY_SHORT = 3

COL_K = 0
COL_V = COL_K + GLA_QK
COL_GF = COL_V + D_GROUP
COL_GB = COL_GF + GLA_LOWRANK
COL_Q = COL_GB + GLA_LOWRANK
COL_R = COL_Q + GLA_QK
COL_POOL = COL_R + D_GROUP
COL_HY = COL_POOL + D_GROUP
COL_CONV = COL_HY + 3 * D_GROUP
P_IN = COL_CONV + 2 * D_GROUP

LANE = 128
DFT2 = 128
MIB = 1024 * 1024


def _cparams(sem, vmem_mib=None):
    kw = dict(dimension_semantics=sem)
    if vmem_mib is not None:
        kw["vmem_limit_bytes"] = vmem_mib * MIB
    return pltpu.CompilerParams(**kw)


def _silu(x):
    return x * jax.nn.sigmoid(x)


def _rms_mod(h, g, m):
    y = h * lax.rsqrt(jnp.mean(h * h, axis=-1, keepdims=True) + EPS) * g
    return y * (1.0 + m[1:2, :]) + m[0:1, :]


def _iota(shape, dim):
    return lax.broadcasted_iota(jnp.int32, shape, dim)


MOD_ROWS = 3


def _mod_body(at_ref, w_ref, b_ref, o_ref):
    d, tn = w_ref.shape[1], w_ref.shape[2]
    at = _silu(at_ref[...])
    cols = [jnp.broadcast_to(at[:, r:r + 1], (d, LANE)).reshape(d // SUBLANES, SUBLANES, LANE)
            for r in range(MOD_ROWS)]
    o_ref[0] = jnp.zeros((8, tn), F32) + b_ref[0]
    for j in range(tn // LANE):
        lsl = slice(j * LANE, (j + 1) * LANE)
        w3 = w_ref[0, :, lsl].reshape(d // SUBLANES, SUBLANES, LANE)
        for r in range(MOD_ROWS):
            part = jnp.sum(w3 * cols[r], axis=0)
            o_ref[0, r:r + 1, lsl] += jnp.sum(part, axis=0, keepdims=True)


def _modulation(cc, ada_w, ada_b):
    nl, d, nm = ada_w.shape
    tn = 1024
    return pl.pallas_call(
        _mod_body,
        out_shape=jax.ShapeDtypeStruct((nl, 8, nm), F32),
        grid=(nl, nm // tn),
        in_specs=[pl.BlockSpec((d, 8), lambda l, j: (0, 0)),
                  pl.BlockSpec((1, d, tn), lambda l, j: (l, 0, j)),
                  pl.BlockSpec((1, 1, tn), lambda l, j: (l, 0, j))],
        out_specs=pl.BlockSpec((1, 8, tn), lambda l, j: (l, 0, j)),
        compiler_params=_cparams(("parallel", "parallel")),
        name="adaln_mod",
    )(cc.T, ada_w, ada_b.reshape(nl, 1, nm))


FF_CHUNK = 256
ROW_SUB = 256


def _swiglu(xn, wi_ref, wo_ref, hm_ref, rows):
    ff = wo_ref.shape[0]
    for c in range(0, ff, FF_CHUNK):
        a = jnp.dot(xn, wi_ref[:, c:c + FF_CHUNK], preferred_element_type=F32)
        g = jnp.dot(xn, wi_ref[:, ff + c:ff + c + FF_CHUNK], preferred_element_type=F32)
        hm_ref[rows, c:c + FF_CHUNK] = (_silu(g) * a).astype(BF16)
    return jnp.dot(hm_ref[rows, :], wo_ref[...], preferred_element_type=F32)


def _resident(shape):
    nd = len(shape)
    return pl.BlockSpec(shape, lambda i: (0,) * nd, pipeline_mode=pl.Buffered(1))


def _resident_layer(shape, layer):
    nd = len(shape)
    return pl.BlockSpec((None,) + tuple(shape), lambda i: (layer,) + (0,) * nd, pipeline_mode=pl.Buffered(1))


def _gate_fold_body(wg_ref, gw_ref, o_ref):
    o_ref[...] = jnp.dot(wg_ref[...], gw_ref[...], precision=HI, preferred_element_type=F32)


def _gate_fold(w_gate, gw):
    d = w_gate.shape[0]
    return pl.pallas_call(
        _gate_fold_body,
        out_shape=jax.ShapeDtypeStruct((d, gw.shape[1]), F32),
        grid=(1,),
        in_specs=[pl.BlockSpec(w_gate.shape, lambda i: (0, 0)), pl.BlockSpec(gw.shape, lambda i: (0, 0))],
        out_specs=pl.BlockSpec((d, gw.shape[1]), lambda i: (0, 0)),
        name="gate_fold",
    )(w_gate, gw)


def _chunk_prefix(x):
    pos = _iota(x.shape, 0) & (CHUNK - 1)
    shift = 1
    while shift < CHUNK:
        x = x + jnp.where(pos >= shift, pltpu.roll(x, shift, 0), 0.0)
        shift *= 2
    return x


def _ffn_proj_body(h_ref, m_ref, g1_ref, wi_ref, wo_ref, g2_ref, *rest, nparts):
    w_refs = rest[:nparts]
    gw_ref, gb_ref = rest[nparts:nparts + 2]
    x_ref = rest[nparts + 2]
    o_refs = rest[nparts + 3:2 * nparts + 5]
    hm_ref = rest[-1]
    m = m_ref[0]
    for r0 in range(0, h_ref.shape[0], ROW_SUB):
        rows = slice(r0, r0 + ROW_SUB)
        h = h_ref[rows, :]
        xn = _rms_mod(h, g1_ref[...], m[0:2]).astype(BF16)
        x1 = h + (0.5 * m[2:3, :]) * _swiglu(xn, wi_ref, wo_ref, hm_ref, rows)
        x_ref[rows, :] = x1
        xn2 = _rms_mod(x1, g2_ref[...], m[3:5]).astype(BF16)
        for w_ref, o_ref in zip(w_refs, o_refs[:nparts]):
            o_ref[rows, :] = jnp.dot(xn2, w_ref[...], preferred_element_type=F32).astype(o_ref.dtype)
        a = jnp.dot(xn2, gw_ref[...], preferred_element_type=F32) + gb_ref[...]
        ls = (jnp.minimum(a, 0.0) - jnp.log(1.0 + jnp.exp(-jnp.abs(a)))) * (1.0 / GLA_TAU)
        pre = _chunk_prefix(ls)
        o_refs[nparts][rows, :] = pre[:, :GLA_QK]
        pb = pre[:, GLA_QK:]
        tot = jnp.concatenate([jnp.broadcast_to(pb[c0 + CHUNK - 1:c0 + CHUNK, :], (CHUNK, GLA_QK))
                               for c0 in range(0, ROW_SUB, CHUNK)], axis=0)
        o_refs[nparts + 1][rows, :] = tot - pb + ls[:, GLA_QK:]


def _ffn_proj(h, m, g1, wi, wo, g2, w_parts, gw, gb, *, layer, tm, tiles_per_seq):
    n, d = h.shape
    ff = wo.shape[1]
    nparts = len(w_parts)
    widths = [w.shape[1] for w in w_parts]
    tok = lambda wd: pl.BlockSpec((tm, wd), lambda i: (i, 0))
    in_specs = [tok(d), pl.BlockSpec((1, 8, d), lambda i: (i // tiles_per_seq, 0, 0)), _resident((1, d)),
                _resident_layer((d, 2 * ff), layer), _resident_layer((ff, d), layer), _resident((1, d))]
    in_specs += [_resident((d, wd)) for wd in widths]
    in_specs += [_resident((d, 2 * GLA_QK)), _resident((1, 2 * GLA_QK))]
    out_shape = [jax.ShapeDtypeStruct((n, d), F32)]
    out_shape += [jax.ShapeDtypeStruct((n, wd), BF16) for wd in widths]
    out_shape += [jax.ShapeDtypeStruct((n, GLA_QK), F32)] * 2
    return pl.pallas_call(
        functools.partial(_ffn_proj_body, nparts=nparts),
        out_shape=out_shape,
        grid=(n // tm,),
        in_specs=in_specs,
        out_specs=[tok(d)] + [tok(wd) for wd in widths] + [tok(GLA_QK)] * 2,
        scratch_shapes=[pltpu.VMEM((tm, ff), BF16)],
        compiler_params=_cparams(("parallel",), 56),
        name="ffn1_in_proj",
    )(h, m, g1.reshape(1, d), wi, wo, g2.reshape(1, d), *w_parts, gw, gb)


def _out_ffn_body(x_ref, m_ref, p_ref, hy_ref, gl_ref, cv_ref, wout_ref, g_ref, wi_ref, wo_ref, fn_ref, o_ref,
                  hm_ref, *, final):
    m = m_ref[0]
    for r0 in range(0, x_ref.shape[0], ROW_SUB):
        rows = slice(r0, r0 + ROW_SUB)
        acc = jnp.dot(p_ref[rows, :].astype(BF16), wout_ref[0:D_GROUP, :], preferred_element_type=F32)
        acc += jnp.dot(hy_ref[rows, :].astype(BF16), wout_ref[D_GROUP:2 * D_GROUP, :], preferred_element_type=F32)
        acc += jnp.dot(gl_ref[rows, :].astype(BF16), wout_ref[2 * D_GROUP:3 * D_GROUP, :],
                       preferred_element_type=F32)
        acc += jnp.dot(cv_ref[rows, :].astype(BF16), wout_ref[3 * D_GROUP:, :], preferred_element_type=F32)
        x2 = x_ref[rows, :] + m[0:1, :] * acc
        xn = _rms_mod(x2, g_ref[...], m[1:3]).astype(BF16)
        out = x2 + (0.5 * m[3:4, :]) * _swiglu(xn, wi_ref, wo_ref, hm_ref, rows)
        if final:
            out = out * lax.rsqrt(jnp.mean(out * out, axis=-1, keepdims=True) + EPS) * fn_ref[...]
        o_ref[rows, :] = out


def _out_ffn(x, m, parts, w_out, g, wi, wo, *, layer, tm, tiles_per_seq, final_g=None):
    n, d = x.shape
    ff = wo.shape[1]
    final = final_g is not None
    fg = final_g if final else g
    tok = lambda wd: pl.BlockSpec((tm, wd), lambda i: (i, 0))
    return pl.pallas_call(
        functools.partial(_out_ffn_body, final=final),
        out_shape=jax.ShapeDtypeStruct((n, d), F32),
        grid=(n // tm,),
        in_specs=[tok(d), pl.BlockSpec((1, 8, d), lambda i: (i // tiles_per_seq, 0, 0)),
                  tok(D_GROUP), tok(D_GROUP), tok(D_GROUP), tok(D_GROUP), _resident_layer((d, d), layer),
                  _resident((1, d)), _resident_layer((d, 2 * ff), layer), _resident_layer((ff, d), layer),
                  _resident((1, d))],
        out_specs=tok(d),
        scratch_shapes=[pltpu.VMEM((tm, ff), BF16)],
        compiler_params=_cparams(("parallel",), 56),
        name="out_proj_ffn2",
    )(x, m, *parts, w_out, g.reshape(1, d), wi, wo, fg.reshape(1, d))


def _gla_state_body(kf, vf, bf, kb, vb, bb, s0f, s0b, sf_o, sb_o, ff_o, fb_o, stf, stb, *, cpb):
    i = pl.program_id(1)

    @pl.when(i == 0)
    def _():
        stf[...] = s0f[0]
        stb[...] = s0b[0]

    sshape = (GLA_HEADS * GLA_DV, GLA_QK)
    bmask = (_iota(sshape, 0) >> 6) == (_iota(sshape, 1) >> 5)
    tn_dims = (((0,), (0,)), ((), ()))

    def direction(k_ref, v_ref, b_ref, last, st, s_o, order):
        s = st[...]
        for ci in order:
            sl = slice(ci * CHUNK, (ci + 1) * CHUNK)
            b = b_ref[sl, :]
            tot = b[last:last + 1, :]
            kd = (k_ref[sl, :] * jnp.exp(tot - b)).astype(BF16)
            upd = lax.dot_general(v_ref[sl, :].astype(BF16), kd, tn_dims, preferred_element_type=F32)
            s_o[0, ci] = s.astype(BF16)
            s = s * jnp.exp(tot) + jnp.where(bmask, upd, 0.0)
        st[...] = s

    direction(kf, vf, bf, CHUNK - 1, stf, sf_o, range(cpb))
    direction(kb, vb, bb, 0, stb, sb_o, range(cpb - 1, -1, -1))
    ff_o[0] = stf[...]
    fb_o[0] = stb[...]


def _gla_states(k, v, lf, lb, s0f, s0b, *, bsz, seq, cpb):
    bt = cpb * CHUNK
    nb = seq // bt
    nc = seq // CHUNK
    srow = GLA_HEADS * GLA_DV

    def tf(b, i):
        return (b * nb + i, 0)

    def tb(b, i):
        return (b * nb + nb - 1 - i, 0)

    sblk = pl.BlockSpec((1, srow, GLA_QK), lambda b, i: (b, 0, 0))
    return pl.pallas_call(
        functools.partial(_gla_state_body, cpb=cpb),
        out_shape=[jax.ShapeDtypeStruct((bsz, nc, srow, GLA_QK), BF16),
                   jax.ShapeDtypeStruct((bsz, nc, srow, GLA_QK), BF16),
                   jax.ShapeDtypeStruct((bsz, srow, GLA_QK), F32),
                   jax.ShapeDtypeStruct((bsz, srow, GLA_QK), F32)],
        grid=(bsz, nb),
        in_specs=[pl.BlockSpec((bt, GLA_QK), tf), pl.BlockSpec((bt, D_GROUP), tf), pl.BlockSpec((bt, GLA_QK), tf),
                  pl.BlockSpec((bt, GLA_QK), tb), pl.BlockSpec((bt, D_GROUP), tb), pl.BlockSpec((bt, GLA_QK), tb),
                  sblk, sblk],
        out_specs=[pl.BlockSpec((1, cpb, srow, GLA_QK), lambda b, i: (b, i, 0, 0)),
                   pl.BlockSpec((1, cpb, srow, GLA_QK), lambda b, i: (b, nb - 1 - i, 0, 0)),
                   sblk, sblk],
        scratch_shapes=[pltpu.VMEM((srow, GLA_QK), F32), pltpu.VMEM((srow, GLA_QK), F32)],
        compiler_params=_cparams(("parallel", "arbitrary")),
        name="gla_states",
    )(k, v, lf, k, v, lb, s0f, s0b)


def _gla_read_body(k_ref, v_ref, q_ref, r_ref, bf_ref, bb_ref, sf_ref, sb_ref, ng_ref, o_ref, *, cpb):
    hrows = GLA_HEADS * CHUNK
    cpos = _iota((hrows, CHUNK), 0) & (CHUNK - 1)
    ccol = _iota((hrows, CHUNK), 1)
    lowm = cpos >= ccol
    upm = cpos <= ccol
    hq = (_iota((hrows, GLA_QK), 0) >> 6) == (_iota((hrows, GLA_QK), 1) >> 5)
    ho = (_iota((hrows, D_GROUP), 0) >> 6) == (_iota((hrows, D_GROUP), 1) >> 6)
    bavg = jnp.where((_iota((D_GROUP, D_GROUP), 0) >> 6) == (_iota((D_GROUP, D_GROUP), 1) >> 6),
                     1.0 / GLA_DV, 0.0).astype(BF16)
    nt_dims = (((1,), (1,)), ((), ()))

    bf_ = bf_ref[...]
    bb_ = bb_ref[...]
    qs = q_ref[...].astype(F32) * (GLA_DK ** -0.5)
    kk = k_ref[...]
    qef = (qs * jnp.exp(bf_)).astype(BF16)
    qeb = (qs * jnp.exp(bb_)).astype(BF16)
    kef = (kk * jnp.exp(-bf_)).astype(BF16)
    keb = (kk * jnp.exp(-bb_)).astype(BF16)
    vb16 = v_ref[...].astype(BF16)
    zero = jnp.zeros((), BF16)
    for ci in range(cpb):
        sl = slice(ci * CHUNK, (ci + 1) * CHUNK)
        qf4 = jnp.where(hq, jnp.concatenate([qef[sl]] * GLA_HEADS, axis=0), zero)
        qb4 = jnp.where(hq, jnp.concatenate([qeb[sl]] * GLA_HEADS, axis=0), zero)
        af = lax.dot_general(qf4, kef[sl], nt_dims, preferred_element_type=F32)
        ab = lax.dot_general(qb4, keb[sl], nt_dims, preferred_element_type=F32)
        att = (jnp.where(lowm, af, 0.0) + jnp.where(upm, ab, 0.0)).astype(BF16)
        oall = jnp.dot(att, vb16[sl], preferred_element_type=F32)
        om = jnp.where(ho, oall, 0.0)
        o = om[0:CHUNK] + om[CHUNK:2 * CHUNK] + om[2 * CHUNK:3 * CHUNK] + om[3 * CHUNK:4 * CHUNK]
        qcat = jnp.concatenate([qef[sl], qeb[sl]], axis=1)
        scat = jnp.concatenate([sf_ref[0, ci], sb_ref[0, ci]], axis=1)
        o_ref[sl, :] = o + lax.dot_general(qcat, scat, nt_dims, preferred_element_type=F32)
    o = o_ref[...]
    ms = _chunk_sum_rhs(o * o, bavg)
    o_ref[...] = o * lax.rsqrt(ms + EPS) * ng_ref[...] * _silu(r_ref[...].astype(F32))


def _chunk_sum_rhs(x, mat):
    hi, lo = _split_bf16(x)
    return jnp.dot(hi, mat, preferred_element_type=F32) + jnp.dot(lo, mat, preferred_element_type=F32)


def _gla_read(k, v, q, r, lf, lb, sf, sb, ng, *, bsz, seq, cpb):
    bt = cpb * CHUNK
    nb = seq // bt
    srow = GLA_HEADS * GLA_DV
    n = bsz * seq

    def tk(i):
        return (i, 0)

    sspec = pl.BlockSpec((1, cpb, srow, GLA_QK), lambda i: (i // nb, i % nb, 0, 0))
    return pl.pallas_call(
        functools.partial(_gla_read_body, cpb=cpb),
        out_shape=jax.ShapeDtypeStruct((n, D_GROUP), F32),
        grid=(bsz * nb,),
        in_specs=[pl.BlockSpec((bt, GLA_QK), tk), pl.BlockSpec((bt, D_GROUP), tk),
                  pl.BlockSpec((bt, GLA_QK), lambda i: (i, 1)), pl.BlockSpec((bt, D_GROUP), tk),
                  pl.BlockSpec((bt, GLA_QK), tk), pl.BlockSpec((bt, GLA_QK), tk),
                  sspec, sspec, pl.BlockSpec((1, D_GROUP), lambda i: (0, 0))],
        out_specs=pl.BlockSpec((bt, D_GROUP), tk),
        compiler_params=_cparams(("parallel",)),
        name="gla_readout",
    )(k, v, q, r, lf, lb, sf, sb, ng)


def _box_matrix(n, w):
    pos = np.arange(n)
    lo = np.clip(pos - w // 2, 0, n)
    hi = np.clip(pos - w // 2 + w, 0, n)
    col = np.arange(n)[None, :]
    return ((col >= lo[:, None]) & (col < hi[:, None])).astype(np.float32)


def _lane_windows(shape):
    w = jnp.left_shift(2, _iota(shape, 1) >> 6)
    return w, w >> 1


def _box_count(pos, w, half, n):
    return jnp.minimum(pos - half + w, n) - jnp.maximum(pos - half, 0)


def _split_bf16(x):
    hi = x.astype(BF16)
    lo = (x - hi.astype(F32)).astype(BF16)
    return hi, lo


def _pool2d_body(cur_ref, prev_ref, next_ref, pc_ref, w_ref, sc_ref, o_ref, ycol, *, tiles, rows):
    i = pl.program_id(1)
    tm = cur_ref.shape[0]
    hb = prev_ref.shape[0]
    sub = 2 * GRID_W
    pflag = jnp.where(i > 0, 1.0, 0.0)
    nflag = jnp.where(i < tiles - 1, 1.0, 0.0)

    def colpool(x):
        halves = []
        for half in range(2):
            lsl = slice(half * LANE, (half + 1) * LANE)
            ys = [jnp.dot(pc_ref[wi], x[:, lsl], preferred_element_type=F32) for wi in (2 * half, 2 * half + 1)]
            lane = _iota((sub, LANE), 1)
            halves.append(jnp.where(lane < POOL_CH, ys[0], ys[1]))
        return jnp.concatenate(halves, axis=1)

    for s in range(hb // sub):
        ycol[s * sub:(s + 1) * sub, :] = colpool(prev_ref[s * sub:(s + 1) * sub, :]) * pflag
    for s in range(tm // sub):
        ycol[hb + s * sub:hb + (s + 1) * sub, :] = colpool(cur_ref[s * sub:(s + 1) * sub, :])
    for s in range(hb // sub):
        ycol[hb + tm + s * sub:hb + tm + (s + 1) * sub, :] = colpool(next_ref[s * sub:(s + 1) * sub, :]) * nflag

    rc = 256
    wl, half = _lane_windows((rc, D_GROUP))
    narrow = _iota((rc, LANE), 1) < POOL_CH
    for r0 in range(0, tm, rc):
        def band(lo, hi, lsl):
            base = hb + r0
            acc = ycol[base + GRID_W * lo:base + GRID_W * lo + rc, lsl]
            for dd in range(lo + 1, hi):
                acc = acc + ycol[base + GRID_W * dd:base + GRID_W * dd + rc, lsl]
            return acc

        left, right = slice(0, LANE), slice(LANE, 2 * LANE)
        z2 = band(-1, 1, left)
        z4 = z2 + band(-2, -1, left) + band(1, 2, left)
        z8 = band(-4, 4, right)
        z16 = z8 + band(-8, -4, right) + band(4, 8, right)
        z = jnp.concatenate([jnp.where(narrow, z2, z4), jnp.where(narrow, z8, z16)], axis=1)
        tok = _iota((rc, D_GROUP), 0) + (i * tm + r0)
        rcnt = _box_count(tok >> 6, wl, half, rows)
        ccnt = _box_count(tok & (GRID_W - 1), wl, half, GRID_W)
        pooled = z / (rcnt * ccnt).astype(F32)
        dlt = (pooled - cur_ref[r0:r0 + rc, :]).astype(BF16)
        o_ref[r0:r0 + rc, :] = jnp.dot(dlt, w_ref[...], preferred_element_type=F32) * sc_ref[...]


def _pool2d(u, wbd, scale, *, bsz, seq):
    tm = 1024
    hb = 512
    tiles = seq // tm
    r = tm // hb
    nhb = seq // hb
    pc = np.stack([np.kron(np.eye(2, dtype=np.float32), _box_matrix(GRID_W, w)) for w in POOL_WINDOWS])
    return pl.pallas_call(
        functools.partial(_pool2d_body, tiles=tiles, rows=seq // GRID_W),
        out_shape=jax.ShapeDtypeStruct((bsz * seq, D_GROUP), F32),
        grid=(bsz, tiles),
        in_specs=[pl.BlockSpec((tm, D_GROUP), lambda b, i: (b * tiles + i, 0)),
                  pl.BlockSpec((hb, D_GROUP), lambda b, i: (b * nhb + jnp.maximum(i * r - 1, 0), 0)),
                  pl.BlockSpec((hb, D_GROUP), lambda b, i: (b * nhb + jnp.minimum(i * r + r, nhb - 1), 0)),
                  pl.BlockSpec((4, 2 * GRID_W, 2 * GRID_W), lambda b, i: (0, 0, 0)),
                  pl.BlockSpec((D_GROUP, D_GROUP), lambda b, i: (0, 0)),
                  pl.BlockSpec((1, D_GROUP), lambda b, i: (0, 0))],
        out_specs=pl.BlockSpec((tm, D_GROUP), lambda b, i: (b * tiles + i, 0)),
        scratch_shapes=[pltpu.VMEM((tm + 2 * hb, D_GROUP), F32)],
        compiler_params=_cparams(("parallel", "parallel")),
        name="pool2d",
    )(u, u, u, jnp.asarray(pc, BF16), wbd, scale)


def _pool1d_body(x_ref, p_ref, w_ref, sc_ref, o_ref):
    x = x_ref[...]
    n = x.shape[0]
    ys = [jnp.dot(p_ref[wi], x, preferred_element_type=F32) for wi in range(4)]
    wl, half = _lane_windows((n, D_GROUP))
    z = jnp.where(wl == 2, ys[0], jnp.where(wl == 4, ys[1], jnp.where(wl == 8, ys[2], ys[3])))
    cnt = _box_count(_iota((n, D_GROUP), 0), wl, half, n)
    dlt = (z / cnt.astype(F32) - x).astype(BF16)
    o_ref[...] = jnp.dot(dlt, w_ref[...], preferred_element_type=F32) * sc_ref[...]


def _pool1d(u, wbd, scale, *, bsz, seq):
    pm = np.stack([_box_matrix(seq, w) for w in POOL_WINDOWS])
    return pl.pallas_call(
        _pool1d_body,
        out_shape=jax.ShapeDtypeStruct((bsz * seq, D_GROUP), F32),
        grid=(bsz,),
        in_specs=[pl.BlockSpec((seq, D_GROUP), lambda b: (b, 0)),
                  pl.BlockSpec((4, seq, seq), lambda b: (0, 0, 0)),
                  pl.BlockSpec((D_GROUP, D_GROUP), lambda b: (0, 0)),
                  pl.BlockSpec((1, D_GROUP), lambda b: (0, 0))],
        out_specs=pl.BlockSpec((seq, D_GROUP), lambda b: (b, 0)),
        compiler_params=_cparams(("parallel",)),
        name="pool1d",
    )(u, jnp.asarray(pm, BF16), wbd, scale)


def _fill_halo(buf, cur, prev, nxt, i, tps, hb, tm, pre):
    first = (i % tps) == 0
    last = (i % tps) == tps - 1
    buf[0:hb, :] = jnp.where(first, 0.0, pre(prev[...]))
    buf[hb:hb + tm, :] = pre(cur[...])
    buf[hb + tm:hb + tm + hb, :] = jnp.where(last, 0.0, pre(nxt[...]))


SUBLANES = 8


def _tap_phases(taps, off):
    return sorted({(off + j) % SUBLANES for j in range(taps)} - {0})


def _fill_phases(sh, buf, phases):
    rows = sh.shape[1]
    for slot, s in enumerate(phases):
        sh[slot, :, :] = buf[s:s + rows, :]


def _dwconv(buf, sh, phases, w_ref, r0, rc, taps, off):
    acc = None
    for j in range(taps):
        s, q = (off + j) % SUBLANES, (off + j) // SUBLANES
        lo = r0 + SUBLANES * q
        src = buf[lo:lo + rc, :] if s == 0 else sh[phases.index(s), lo:lo + rc, :]
        term = src * w_ref[j:j + 1, :]
        acc = term if acc is None else acc + term
    return acc


def _conf_body(cur, prev, nxt, w_ref, b_ref, lg_ref, lb_ref, o_ref, buf, sh, *, tps):
    i = pl.program_id(0)
    tm = cur.shape[0]
    hb = prev.shape[0]

    def glu(u):
        u = u.astype(F32)
        return u[:, :D_GROUP] * jax.nn.sigmoid(u[:, D_GROUP:])

    _fill_halo(buf, cur, prev, nxt, i, tps, hb, tm, glu)
    rc = 128
    off = hb - (CONV_WIDTH - 1) // 2
    phases = _tap_phases(CONV_WIDTH, off)
    _fill_phases(sh, buf, phases)
    for r0 in range(0, tm, rc):
        h = _dwconv(buf, sh, phases, w_ref, r0, rc, CONV_WIDTH, off) + b_ref[...]
        mu = jnp.mean(h, axis=-1, keepdims=True)
        hc = h - mu
        var = jnp.mean(hc * hc, axis=-1, keepdims=True)
        o_ref[r0:r0 + rc, :] = _silu(hc * lax.rsqrt(var + EPS) * lg_ref[...] + lb_ref[...])


def _halo_specs(tm, hb, width, nrows):
    r = tm // hb
    nhb = nrows // hb
    return [pl.BlockSpec((tm, width), lambda i: (i, 0)),
            pl.BlockSpec((hb, width), lambda i: (jnp.maximum(i * r - 1, 0), 0)),
            pl.BlockSpec((hb, width), lambda i: (jnp.minimum(i * r + r, nhb - 1), 0))]


def _conformer(u, w, b, lg, lb, *, seq, tm):
    n = u.shape[0]
    hb = 16
    nph = len(_tap_phases(CONV_WIDTH, hb - (CONV_WIDTH - 1) // 2))
    vec = pl.BlockSpec((1, D_GROUP), lambda i: (0, 0))
    return pl.pallas_call(
        functools.partial(_conf_body, tps=seq // tm),
        out_shape=jax.ShapeDtypeStruct((n, D_GROUP), F32),
        grid=(n // tm,),
        in_specs=_halo_specs(tm, hb, 2 * D_GROUP, n) + [pl.BlockSpec((CONV_WIDTH, D_GROUP), lambda i: (0, 0)),
                                                        vec, vec, vec],
        out_specs=pl.BlockSpec((tm, D_GROUP), lambda i: (i, 0)),
        scratch_shapes=[pltpu.VMEM((tm + 2 * hb, D_GROUP), F32),
                        pltpu.VMEM((nph, tm + 2 * hb - SUBLANES, D_GROUP), F32)],
        compiler_params=_cparams(("parallel",)),
        name="conformer_conv",
    )(u, u, u, w, b.reshape(1, -1), lg.reshape(1, -1), lb.reshape(1, -1))


def _short_body(cur, prev, nxt, w_ref, b_ref, v_ref, x1_ref, x2_ref, buf, sh, *, tps):
    i = pl.program_id(0)
    tm = cur.shape[0]
    hb = prev.shape[0]
    _fill_halo(buf, cur, prev, nxt, i, tps, hb, tm, lambda u: u.astype(F32))
    rc = 128
    off = hb - (HY_SHORT - 1) // 2
    phases = _tap_phases(HY_SHORT, off)
    _fill_phases(sh, buf, phases)
    for r0 in range(0, tm, rc):
        uc = _dwconv(buf, sh, phases, w_ref, r0, rc, HY_SHORT, off) + b_ref[...]
        v_ref[r0:r0 + rc, :] = uc[:, :D_GROUP]
        x1_ref[r0:r0 + rc, :] = uc[:, D_GROUP:2 * D_GROUP]
        x2_ref[r0:r0 + rc, :] = uc[:, 2 * D_GROUP:]


def _hy_short(u, w, b, *, seq, tm):
    n = u.shape[0]
    hb = 16
    wd = 3 * D_GROUP
    nph = len(_tap_phases(HY_SHORT, hb - (HY_SHORT - 1) // 2))
    ospec = pl.BlockSpec((tm, D_GROUP), lambda i: (i, 0))
    return pl.pallas_call(
        functools.partial(_short_body, tps=seq // tm),
        out_shape=[jax.ShapeDtypeStruct((n, D_GROUP), F32)] * 3,
        grid=(n // tm,),
        in_specs=_halo_specs(tm, hb, wd, n) + [pl.BlockSpec((HY_SHORT, wd), lambda i: (0, 0)),
                                               pl.BlockSpec((1, wd), lambda i: (0, 0))],
        out_specs=[ospec, ospec, ospec],
        scratch_shapes=[pltpu.VMEM((tm + 2 * hb, wd), F32),
                        pltpu.VMEM((nph, tm + 2 * hb - SUBLANES, wd), F32)],
        compiler_params=_cparams(("parallel",)),
        name="hyena_short_conv",
    )(u, u, u, w, b.reshape(1, -1))


def _filter_features(n):
    i = np.arange(n, dtype=np.float64)
    t = np.linspace(0.0, 1.0, n, dtype=np.float32).astype(np.float64)
    wpos = ((2.0 * math.pi / n) * np.arange(n, dtype=np.float32)).astype(np.float32)
    bands = np.linspace(1e-4, HY_BANDS - 1, HY_BANDS, dtype=np.float32)
    arg = (bands[None, :] * wpos[:, None]).astype(np.float32).astype(np.float64)
    z = np.zeros((n, 64), np.float32)
    z[:, 0] = t
    z[:, 1:1 + HY_BANDS] = np.cos(arg)
    z[:, 1 + HY_BANDS:HY_EMB] = -np.sin(arg)
    del i
    return z


def _filter_body(z_ref, w1_ref, b1_ref, w2_ref, b2_ref, w3a_ref, w3b_ref, d_ref, hf_ref, hb_ref, s_ref):
    i = pl.program_id(0)
    half = z_ref.shape[0]
    z = z_ref[...]
    h = jnp.sin(jnp.dot(z, w1_ref[...], precision=HI, preferred_element_type=F32) + b1_ref[...])
    h = jnp.sin(jnp.dot(h, w2_ref[...], precision=HI, preferred_element_type=F32) + b2_ref[...])
    absd = jnp.abs(d_ref[...])
    h = h.astype(BF16)
    tot = None
    for part, (w3_ref, tcol) in enumerate(((w3a_ref, 0), (w3b_ref, 64))):
        hp = jnp.dot(h, w3_ref[...], preferred_element_type=F32)
        hp = hp * jnp.exp(-z[:, tcol:tcol + 1] * absd)
        rows = slice(part * half, (part + 1) * half)
        for o in range(2):
            c0 = 2 * D_GROUP * o
            hf_ref[rows, o * D_GROUP:(o + 1) * D_GROUP] = hp[:, c0:c0 + D_GROUP]
            hb_ref[rows, o * D_GROUP:(o + 1) * D_GROUP] = hp[:, c0 + D_GROUP:c0 + 2 * D_GROUP]
        part_sum = jnp.sum(jnp.abs(hp), axis=0, keepdims=True)
        tot = part_sum if tot is None else tot + part_sum

    @pl.when(i == 0)
    def _():
        s_ref[...] = jnp.zeros_like(s_ref)

    s_ref[...] += tot


def _hy_filter(n, w1, b1, w2, b2, w3, deltas, *, t2_major):
    tm = min(n, 512)
    half = tm // 2
    z = _filter_features(n)
    if t2_major:
        z = z.reshape(n // DFT2, DFT2, 64).transpose(1, 0, 2).reshape(n, 64)
    zt = z.reshape(n // tm, 2, half, 64)
    z2 = jnp.asarray(np.concatenate([zt[:, 0], zt[:, 1]], axis=-1).reshape(n // 2, 2 * 64))
    w1p = jnp.zeros((64, HY_FFN), F32).at[:HY_EMB].set(w1)
    zero = jnp.zeros((64, HY_FFN), F32)
    bd = lambda w: jnp.concatenate([jnp.concatenate([w, zero], axis=1), jnp.concatenate([zero, w], axis=1)], axis=0)
    w3 = w3.astype(BF16)
    zero3 = jnp.zeros_like(w3)
    full = lambda shape: pl.BlockSpec(shape, lambda i: (0, 0))
    tok = pl.BlockSpec((tm, 2 * D_GROUP), lambda i: (i, 0))
    return pl.pallas_call(
        _filter_body,
        out_shape=[jax.ShapeDtypeStruct((n, 2 * D_GROUP), F32), jax.ShapeDtypeStruct((n, 2 * D_GROUP), F32),
                   jax.ShapeDtypeStruct((1, HY_FILTER_CH), F32)],
        grid=(n // tm,),
        in_specs=[pl.BlockSpec((half, 2 * 64), lambda i: (i, 0)), full((2 * 64, 2 * HY_FFN)), full((1, 2 * HY_FFN)),
                  full((2 * HY_FFN, 2 * HY_FFN)), full((1, 2 * HY_FFN)), full((2 * HY_FFN, HY_FILTER_CH)),
                  full((2 * HY_FFN, HY_FILTER_CH)), full((1, HY_FILTER_CH))],
        out_specs=[tok, tok, full((1, HY_FILTER_CH))],
        compiler_params=_cparams(("arbitrary",)),
        name="hyena_filter_mlp",
    )(z2, bd(w1p), jnp.tile(b1.reshape(1, -1), (1, 2)), bd(w2), jnp.tile(b2.reshape(1, -1), (1, 2)),
      jnp.concatenate([w3, zero3], axis=0), jnp.concatenate([zero3, w3], axis=0), deltas.reshape(1, -1))


@functools.lru_cache(maxsize=None)
def _dft_tables(n1):
    nn = n1 * DFT2
    j = np.arange(DFT2)
    th = 2.0 * np.pi * ((np.outer(j, j)) % DFT2) / DFT2
    fr, fi = np.cos(th), -np.sin(th)
    m_fwd = np.block([[fr, -fi], [fi, fr]])
    m_inv = np.block([[fr, fi], [-fi, fr]])
    hh = n1 // 2
    k1 = np.arange(n1)[None, :, None]
    t1 = np.arange(hh)[None, None, :]
    t2 = np.arange(DFT2)[:, None, None]
    ph = 2.0 * np.pi * ((k1 * (DFT2 * t1 + t2)) % nn) / nn
    ar, ai = np.cos(ph), -np.sin(ph)
    a_fwd = np.concatenate([np.concatenate([ar, -ai], axis=2), np.concatenate([ai, ar], axis=2)], axis=1)
    a_real = np.concatenate([ar, ai], axis=1)
    t1r = np.where(t2 >= 1, n1 - 1 - t1, (n1 - t1) % n1)
    phr = 2.0 * np.pi * ((k1 * (DFT2 * t1r + t2)) % nn) / nn
    a_rev = np.concatenate([np.cos(phr), -np.sin(phr)], axis=1)
    a_rev[0, :, 0] = 0.0
    pht = np.transpose(ph, (0, 2, 1))
    cr, ci = np.cos(pht) / nn, np.sin(pht) / nn
    a_inv = np.concatenate([np.concatenate([cr, -ci], axis=2), np.concatenate([ci, cr], axis=2)], axis=1)
    f32 = lambda a: np.ascontiguousarray(a, dtype=np.float32)
    return dict(m_fwd=f32(m_fwd), m_inv=f32(m_inv), a_fwd=f32(a_fwd), a_real=f32(a_real), a_rev=f32(a_rev),
                a_inv=f32(a_inv))


def _fft_a_body(x_ref, m_ref, o_ref, *, tj):
    n1 = o_ref.shape[2]
    for e in range(tj):
        x = jnp.concatenate([x_ref[0, :, e, :], x_ref[1, :, e, :]], axis=0).astype(BF16)
        a = jnp.dot(m_ref[e], x, preferred_element_type=F32)
        o_ref[0, e] = a[:n1]
        o_ref[1, e] = a[n1:]


def _fft_a(x4, m, *, n1, tj):
    c = x4.shape[3]
    return pl.pallas_call(
        functools.partial(_fft_a_body, tj=tj),
        out_shape=jax.ShapeDtypeStruct((2, DFT2, n1, c), F32),
        grid=(DFT2 // tj,),
        in_specs=[pl.BlockSpec((2, n1 // 2, tj, c), lambda j: (0, 0, j, 0)),
                  pl.BlockSpec((tj, 2 * n1, n1), lambda j: (j, 0, 0))],
        out_specs=pl.BlockSpec((2, tj, n1, c), lambda j: (0, j, 0, 0)),
        compiler_params=_cparams(("parallel",)),
        name="fft_stage_a",
    )(x4, m)


def _fft_b_body(a_ref, mf_ref, mi_ref, g_ref, o_ref, *, kg):
    for kk in range(kg):
        x = jnp.concatenate([a_ref[0, :, kk, :], a_ref[1, :, kk, :]], axis=0).astype(BF16)
        xf = jnp.dot(mf_ref[...], x, preferred_element_type=F32)
        xr, xi = xf[:DFT2], xf[DFT2:]
        gr, gi = g_ref[0, kk].astype(F32), g_ref[1, kk].astype(F32)
        y = jnp.concatenate([xr * gr - xi * gi, xr * gi + xi * gr], axis=0).astype(BF16)
        bf = jnp.dot(mi_ref[...], y, preferred_element_type=F32)
        o_ref[0, kk] = bf[:DFT2]
        o_ref[1, kk] = bf[DFT2:]


def _fft_b(a, g, order, tabs, *, n1):
    c = a.shape[3]
    kg = SUBLANES
    blk = pl.BlockSpec((2, kg, DFT2, c), lambda k: (0, k, 0, 0))
    mat = pl.BlockSpec((2 * DFT2, 2 * DFT2), lambda k: (0, 0))
    return pl.pallas_call(
        functools.partial(_fft_b_body, kg=kg),
        out_shape=jax.ShapeDtypeStruct((2, n1, DFT2, c), F32),
        grid=(n1 // kg,),
        in_specs=[pl.BlockSpec((2, DFT2, kg, c), lambda k: (0, 0, k, 0)), mat, mat,
                  pl.BlockSpec((2, kg, DFT2, c), lambda k: (0, k, 0, order))],
        out_specs=blk,
        compiler_params=_cparams(("parallel",)),
        name="fft_stage_b",
    )(a, tabs["m_fwd"], tabs["m_inv"], g)


def _fft_bf_body(a_ref, mf_ref, o_ref, *, kg):
    for kk in range(kg):
        x = jnp.concatenate([a_ref[0, :, kk, :], a_ref[1, :, kk, :]], axis=0).astype(BF16)
        xf = jnp.dot(mf_ref[...], x, preferred_element_type=F32)
        o_ref[0, kk] = xf[:DFT2].astype(BF16)
        o_ref[1, kk] = xf[DFT2:].astype(BF16)


def _fft_b_forward(a, tabs, *, n1):
    c = a.shape[3]
    kg = SUBLANES
    return pl.pallas_call(
        functools.partial(_fft_bf_body, kg=kg),
        out_shape=jax.ShapeDtypeStruct((2, n1, DFT2, c), BF16),
        grid=(n1 // kg,),
        in_specs=[pl.BlockSpec((2, DFT2, kg, c), lambda k: (0, 0, k, 0)),
                  pl.BlockSpec((2 * DFT2, 2 * DFT2), lambda k: (0, 0))],
        out_specs=pl.BlockSpec((2, kg, DFT2, c), lambda k: (0, k, 0, 0)),
        compiler_params=_cparams(("parallel",)),
        name="fft_stage_b_filter",
    )(a, tabs["m_fwd"])


def _fft_ai_body(b_ref, m_ref, v_ref, x_ref, bias_ref, o_ref, *, tj):
    h = o_ref.shape[1]
    for e in range(tj):
        b = jnp.concatenate([b_ref[0, :, e, :], b_ref[1, :, e, :]], axis=0).astype(BF16)
        y = jnp.dot(m_ref[e], b, preferred_element_type=F32)
        o_ref[0, :, e, :] = y[:h]
        o_ref[1, :, e, :] = y[h:]
    o_ref[...] = x_ref[...] * (o_ref[...] + v_ref[...] * bias_ref[...])


def _fft_a_inv(b, m, v4, xm4, bias, *, n1, tj):
    c = b.shape[3]
    half = pl.BlockSpec((2, n1 // 2, tj, c), lambda j: (0, 0, j, 0))
    return pl.pallas_call(
        functools.partial(_fft_ai_body, tj=tj),
        out_shape=jax.ShapeDtypeStruct((2, n1 // 2, DFT2, c), F32),
        grid=(DFT2 // tj,),
        in_specs=[pl.BlockSpec((2, n1, tj, c), lambda j: (0, 0, j, 0)),
                  pl.BlockSpec((tj, n1, 2 * n1), lambda j: (j, 0, 0)),
                  half, half, pl.BlockSpec((1, c), lambda j: (0, 0))],
        out_specs=half,
        compiler_params=_cparams(("parallel",)),
        name="fft_stage_a_inv",
    )(b, m, v4, xm4, bias)


def _filt_a_body(hj_ref, hz_ref, hr_ref, s_ref, mf_ref, mr_ref, o_ref, *, tj):
    n1 = o_ref.shape[2]
    s = s_ref[...]
    for e in range(tj):
        src = hz_ref if e == 0 else hr_ref
        me = 0 if e == 0 else tj - e
        for o in range(2):
            c0 = 2 * D_GROUP * o
            inv = 1.0 / (s[:, c0:c0 + D_GROUP] + s[:, c0 + D_GROUP:c0 + 2 * D_GROUP] + EPS)
            lsl = slice(o * D_GROUP, (o + 1) * D_GROUP)
            hf = hj_ref[e, :, lsl].astype(BF16)
            hb = src[me, :, lsl].astype(BF16)
            a = (jnp.dot(mf_ref[e], hf, preferred_element_type=F32)
                 + jnp.dot(mr_ref[e], hb, preferred_element_type=F32)) * inv
            o_ref[0, e, :, lsl] = a[:n1]
            o_ref[1, e, :, lsl] = a[n1:]


def _filt_a(hf, hb, colsum, tabs, *, n1):
    tj = 8
    nj = DFT2 // tj
    wd = 2 * D_GROUP
    shp = (DFT2, n1 // 2, wd)
    blk = lambda fn: pl.BlockSpec((tj, n1 // 2, wd), fn)
    mat = pl.BlockSpec((tj, 2 * n1, n1 // 2), lambda j: (j, 0, 0))
    return pl.pallas_call(
        functools.partial(_filt_a_body, tj=tj),
        out_shape=jax.ShapeDtypeStruct((2, DFT2, n1, wd), F32),
        grid=(nj,),
        in_specs=[blk(lambda j: (j, 0, 0)), blk(lambda j: ((nj - j) % nj, 0, 0)), blk(lambda j: (nj - 1 - j, 0, 0)),
                  pl.BlockSpec((1, HY_FILTER_CH), lambda j: (0, 0)), mat, mat],
        out_specs=pl.BlockSpec((2, tj, n1, wd), lambda j: (0, j, 0, 0)),
        compiler_params=_cparams(("parallel",)),
        name="filter_stage_a",
    )(hf.reshape(shp), hb.reshape(shp), hb.reshape(shp), colsum, tabs["a_real"], tabs["a_rev"])


def _hyena_long(v, x1, x2, hf, hb, colsum, bias, *, seq):
    n1 = 2 * seq // DFT2
    tabs = {k: jnp.asarray(a).astype(BF16) for k, a in _dft_tables(n1).items()}
    tj = 8
    fa = _filt_a(hf, hb, colsum, tabs, n1=n1)
    g = _fft_b_forward(fa, tabs, n1=n1)
    shp = (2, n1 // 2, DFT2, D_GROUP)
    z = v.reshape(shp)
    for order, xm in ((0, x1.reshape(shp)), (1, x2.reshape(shp))):
        a = _fft_a(z, tabs["a_fwd"], n1=n1, tj=tj)
        b = _fft_b(a, g, order, tabs, n1=n1)
        z = _fft_a_inv(b, tabs["a_inv"], z, xm, bias[order].reshape(1, D_GROUP), n1=n1, tj=tj)
    return z.reshape(2 * seq, D_GROUP)


def _hyena_ctx_body(v_ref, x1_ref, x2_ref, hf_ref, hb_ref, s_ref, bias_ref, cm_ref, sm_ref, ct_ref, st_ref, o_ref):
    n = v_ref.shape[1]
    cm, sm, ct, st = cm_ref[...], sm_ref[...], ct_ref[...], st_ref[...]
    s = s_ref[...]
    row0 = _iota((n, D_GROUP), 0) == 0
    dot = lambda a, b: jnp.dot(a, b.astype(BF16), preferred_element_type=F32)
    zr, zi = v_ref[0], v_ref[1]
    for order, xm in ((0, x1_ref), (1, x2_ref)):
        c0 = 2 * D_GROUP * order
        inv = 1.0 / (s[:, c0:c0 + D_GROUP] + s[:, c0 + D_GROUP:c0 + 2 * D_GROUP] + EPS)
        lsl = slice(order * D_GROUP, (order + 1) * D_GROUP)
        hf = hf_ref[:, lsl] * inv
        hb = jnp.where(row0, 0.0, hb_ref[:, lsl] * inv)
        gr = dot(cm, hf + hb)
        gi = dot(sm, hb - hf)
        xr = dot(cm, zr) + dot(sm, zi)
        xi = dot(cm, zi) - dot(sm, zr)
        yr = xr * gr - xi * gi
        yi = xr * gi + xi * gr
        cr = dot(ct, yr) - dot(st, yi)
        ci = dot(ct, yi) + dot(st, yr)
        bias = bias_ref[order:order + 1, :]
        zr = xm[0] * (cr + zr * bias)
        zi = xm[1] * (ci + zi * bias)
    o_ref[0] = zr
    o_ref[1] = zi


def _hyena_ctx(v, x1, x2, hf, hb, colsum, bias, *, seq):
    nn = 2 * seq
    k = np.arange(nn)
    t = np.arange(seq)
    th = 2.0 * np.pi * (np.outer(k, t) % nn) / nn
    cm, sm = np.cos(th), np.sin(th)
    consts = [jnp.asarray(a, F32).astype(BF16) for a in (cm, sm, cm.T / nn, sm.T / nn)]
    shp = (2, seq, D_GROUP)
    full3 = pl.BlockSpec(shp, lambda i: (0, 0, 0))
    f2 = lambda a: pl.BlockSpec(a.shape, lambda i: (0, 0))
    args = [hf, hb, colsum, bias] + consts
    out = pl.pallas_call(
        _hyena_ctx_body,
        out_shape=jax.ShapeDtypeStruct(shp, F32),
        grid=(1,),
        in_specs=[full3, full3, full3] + [f2(a) for a in args],
        out_specs=full3,
        compiler_params=_cparams(("arbitrary",)),
        name="hyena_ctx",
    )(v.reshape(shp), x1.reshape(shp), x2.reshape(shp), *args)
    return out.reshape(2 * seq, D_GROUP)


def _mod_rows(mod_l, rows, first, count):
    m = mod_l[rows, first:first + count, :]
    return jnp.pad(m, ((0, 0), (0, 8 - count), (0, 0)))


def _split_w_in(w_in):
    wb = w_in.astype(BF16)
    parts = dict(kq=jnp.concatenate([wb[:, COL_K:COL_V], wb[:, COL_Q:COL_R]], axis=1), v=wb[:, COL_V:COL_GF],
                 r=wb[:, COL_R:COL_POOL], pool=wb[:, COL_POOL:COL_HY], hy=wb[:, COL_HY:COL_CONV],
                 conv=wb[:, COL_CONV:P_IN])
    gate = jnp.pad(w_in[:, COL_GF:COL_Q], ((0, 0), (0, LANE - 2 * GLA_LOWRANK)))
    return parts, gate


def _mix(u, p, *, bsz, seq, is_ctx, states, filt):
    sf, sb = states
    cpb = min(8, seq // CHUNK)
    if is_ctx:
        pool = _pool1d(u["pool"], p["pool_wbd"], p["pool_scale"], bsz=bsz, seq=seq)
    else:
        pool = _pool2d(u["pool"], p["pool_wbd"], p["pool_scale"], bsz=bsz, seq=seq)
    tmc = min(seq, 1024)
    v, x1, x2 = _hy_short(u["hy"], p["hy_short_w"], p["hy_short_b"], seq=seq, tm=tmc)
    if is_ctx:
        hy = _hyena_ctx(v, x1, x2, *filt, p["hy_bias"], seq=seq)
    else:
        hy = _hyena_long(v, x1, x2, *filt, p["hy_bias"], seq=seq)
    gla = _gla_read(u["kq"], u["v"], u["kq"], u["r"], u["lf"], u["lb"], sf, sb, p["gla_ng"],
                    bsz=bsz, seq=seq, cpb=cpb)
    conv = _conformer(u["conv"], p["conv_dw_w"], p["conv_dw_b"], p["conv_ln_g"], p["conv_ln_b"], seq=seq, tm=tmc)
    return [pool, hy, gla, conv]


def kernel(x, c, ctx, c_ctx, ada_w, ada_b, ffn1_norm, ffn1_wi, ffn1_wo, mix_norm, w_in, w_out, pool_w, pool_scale, hy_short_w, hy_short_b, hy_w1, hy_b1, hy_w2, hy_b2, hy_w3, hy_deltas, hy_bias, gla_gw_f, gla_gb_f, gla_gw_b, gla_gb_b, gla_norm, conv_dw_w, conv_dw_b, conv_ln_g, conv_ln_b, ffn2_norm, ffn2_wi, ffn2_wo, final_norm):
    bsz, seq, d = x.shape
    clen = ctx.shape[1]
    depth = ada_w.shape[0]
    assert bsz == 2, "the Hyena transform packs exactly two batch rows into one complex signal"
    xs = x.reshape(bsz * seq, d)
    cs = ctx.reshape(bsz * clen, d)
    cc = jnp.zeros((8, d), F32).at[0:bsz].set(c).at[bsz].set(c_ctx)
    mod = _modulation(cc, ada_w, ada_b).reshape(depth, 8, N_MOD, d)
    xrows = np.arange(bsz)
    crows = np.full((1,), bsz)
    tmx = 512
    tps_x = seq // tmx
    tmc = bsz * clen
    zero_state = jnp.zeros((bsz, GLA_HEADS * GLA_DV, GLA_QK), F32)
    wi1, wo1, wi2, wo2, wout = (w.astype(BF16) for w in (ffn1_wi, ffn1_wo, ffn2_wi, ffn2_wo, w_out))

    for l in range(depth):
        last = l == depth - 1
        ml = mod[l]
        wparts, wgate = _split_w_in(w_in[l])
        gw = jnp.zeros((LANE, 2 * GLA_QK), F32)
        gw = gw.at[0:GLA_LOWRANK, 0:GLA_QK].set(gla_gw_f[l])
        gw = gw.at[GLA_LOWRANK:2 * GLA_LOWRANK, GLA_QK:].set(gla_gw_b[l])
        gb = jnp.concatenate([gla_gb_f[l], gla_gb_b[l]]).reshape(1, -1)
        gw = _gate_fold(wgate, gw).astype(BF16)
        wbd = jnp.zeros((D_GROUP, D_GROUP), F32)
        for gi in range(len(POOL_WINDOWS)):
            wbd = wbd.at[gi * POOL_CH:(gi + 1) * POOL_CH, gi * POOL_CH:(gi + 1) * POOL_CH].set(pool_w[l, gi])
        p = dict(pool_wbd=wbd.astype(BF16), pool_scale=pool_scale[l].reshape(1, -1),
                 hy_short_w=hy_short_w[l], hy_short_b=hy_short_b[l], hy_bias=hy_bias[l],
                 gla_ng=jnp.tile(gla_norm[l], GLA_HEADS).reshape(1, -1),
                 conv_dw_w=conv_dw_w[l], conv_dw_b=conv_dw_b[l], conv_ln_g=conv_ln_g[l], conv_ln_b=conv_ln_b[l])
        names = ["kq", "v", "r", "pool", "hy", "conv"]

        xs, *outs = _ffn_proj(xs, _mod_rows(ml, xrows, 0, 5), ffn1_norm[l], wi1, wo1, mix_norm[l],
                              [wparts[nm] for nm in names], gw, gb, layer=l, tm=tmx, tiles_per_seq=tps_x)
        ux = dict(zip(names + ["lf", "lb"], outs))
        cnames = ["kq", "v"] if last else names
        cs, *outs = _ffn_proj(cs, _mod_rows(ml, crows, 0, 5), ffn1_norm[l], wi1, wo1, mix_norm[l],
                              [wparts[nm] for nm in cnames], gw, gb, layer=l, tm=tmc, tiles_per_seq=1)
        uc = dict(zip(cnames + ["lf", "lb"], outs))

        ccpb = clen // CHUNK
        sfc, sbc, finf, finb = _gla_states(uc["kq"], uc["v"], uc["lf"], uc["lb"], zero_state, zero_state,
                                           bsz=bsz, seq=clen, cpb=ccpb)
        sfx, sbx, _, _ = _gla_states(ux["kq"], ux["v"], ux["lf"], ux["lb"], finf, finb, bsz=bsz, seq=seq,
                                     cpb=16)

        filt = _hy_filter(seq, hy_w1[l], hy_b1[l], hy_w2[l], hy_b2[l], hy_w3[l], hy_deltas[l], t2_major=True)
        mixed = _mix(ux, p, bsz=bsz, seq=seq, is_ctx=False, states=(sfx, sbx), filt=filt)
        xs = _out_ffn(xs, _mod_rows(ml, xrows, 5, 4), mixed, wout, ffn2_norm[l], wi2, wo2, layer=l, tm=tmx,
                      tiles_per_seq=tps_x, final_g=final_norm if last else None)
        if not last:
            filt_c = _hy_filter(clen, hy_w1[l], hy_b1[l], hy_w2[l], hy_b2[l], hy_w3[l], hy_deltas[l],
                                t2_major=False)
            mixed = _mix(uc, p, bsz=bsz, seq=clen, is_ctx=True, states=(sfc, sbc), filt=filt_c)
            cs = _out_ffn(cs, _mod_rows(ml, crows, 5, 4), mixed, wout, ffn2_norm[l], wi2, wo2, layer=l, tm=tmc,
                          tiles_per_seq=1)
    return xs.reshape(bsz, seq, d)
```

```python
import functools
import math

import numpy as np
import jax
import jax.numpy as jnp
from jax import lax
from jax.experimental import pallas as pl
from jax.experimental.pallas import tpu as pltpu

F32 = jnp.float32
BF16 = jnp.bfloat16
HI = lax.Precision.HIGHEST

D_MODEL = 1024
GRID_W = 64
D_GROUP = 256
D_FF = 2816
N_MOD = 9
EPS = 1e-6
POOL_WINDOWS = (2, 4, 8, 16)
POOL_CH = 64
HY_BANDS = 16
HY_EMB = 1 + 2 * HY_BANDS
HY_FFN = 64
HY_FILTER_CH = 4 * D_GROUP
GLA_HEADS = 4
GLA_DK = 32
GLA_DV = 64
GLA_QK = 128
GLA_LOWRANK = 16
GLA_TAU = 16.0
CHUNK = 64
CONV_WIDTH = 31
HY_SHORT = 3

COL_K = 0
COL_V = COL_K + GLA_QK
COL_GF = COL_V + D_GROUP
COL_GB = COL_GF + GLA_LOWRANK
COL_Q = COL_GB + GLA_LOWRANK
COL_R = COL_Q + GLA_QK
COL_POOL = COL_R + D_GROUP
COL_HY = COL_POOL + D_GROUP
COL_CONV = COL_HY + 3 * D_GROUP
P_IN = COL_CONV + 2 * D_GROUP

LANE = 128
DFT2 = 128
FFT_GROUP = 16
MIB = 1024 * 1024


def _cparams(sem, vmem_mib=None):
    kw = dict(dimension_semantics=sem)
    if vmem_mib is not None:
        kw["vmem_limit_bytes"] = vmem_mib * MIB
    return pltpu.CompilerParams(**kw)


def _silu(x):
    return x * jax.nn.sigmoid(x)


def _rms_mod(h, g, m):
    y = h * lax.rsqrt(jnp.mean(h * h, axis=-1, keepdims=True) + EPS) * g
    return y * (1.0 + m[1:2, :]) + m[0:1, :]


def _iota(shape, dim):
    return lax.broadcasted_iota(jnp.int32, shape, dim)


MOD_ROWS = 3


def _mod_body(at_ref, w_ref, b_ref, o_ref):
    d, tn = w_ref.shape[1], w_ref.shape[2]
    at = _silu(at_ref[...])
    cols = [jnp.broadcast_to(at[:, r:r + 1], (d, LANE)).reshape(d // SUBLANES, SUBLANES, LANE)
            for r in range(MOD_ROWS)]
    o_ref[0] = jnp.zeros((8, tn), F32) + b_ref[0]
    for j in range(tn // LANE):
        lsl = slice(j * LANE, (j + 1) * LANE)
        w3 = w_ref[0, :, lsl].reshape(d // SUBLANES, SUBLANES, LANE)
        for r in range(MOD_ROWS):
            part = jnp.sum(w3 * cols[r], axis=0)
            o_ref[0, r:r + 1, lsl] += jnp.sum(part, axis=0, keepdims=True)


def _modulation(cc, ada_w, ada_b):
    nl, d, nm = ada_w.shape
    tn = 2304
    return pl.pallas_call(
        _mod_body,
        out_shape=jax.ShapeDtypeStruct((nl, 8, nm), F32),
        grid=(nl, nm // tn),
        in_specs=[pl.BlockSpec((d, 8), lambda l, j: (0, 0)),
                  pl.BlockSpec((1, d, tn), lambda l, j: (l, 0, j)),
                  pl.BlockSpec((1, 1, tn), lambda l, j: (l, 0, j))],
        out_specs=pl.BlockSpec((1, 8, tn), lambda l, j: (l, 0, j)),
        compiler_params=_cparams(("parallel", "parallel"), 40),
        name="adaln_mod",
    )(cc.T, ada_w, ada_b.reshape(nl, 1, nm))


FF_CHUNK = 256
ROW_SUB = 256


def _swiglu(xn, wi_ref, wo_ref, hm_ref, rows):
    ff = wo_ref.shape[0]
    for c in range(0, ff, FF_CHUNK):
        a = jnp.dot(xn, wi_ref[:, c:c + FF_CHUNK], preferred_element_type=F32)
        g = jnp.dot(xn, wi_ref[:, ff + c:ff + c + FF_CHUNK], preferred_element_type=F32)
        hm_ref[rows, c:c + FF_CHUNK] = (_silu(g) * a).astype(BF16)
    return jnp.dot(hm_ref[rows, :], wo_ref[...], preferred_element_type=F32)


def _resident(shape):
    nd = len(shape)
    return pl.BlockSpec(shape, lambda i: (0,) * nd, pipeline_mode=pl.Buffered(1))


def _resident_layer(shape, layer):
    nd = len(shape)
    return pl.BlockSpec((None,) + tuple(shape), lambda i: (layer,) + (0,) * nd, pipeline_mode=pl.Buffered(1))


def _gate_fold_body(wg_ref, gw_ref, o_ref):
    o_ref[...] = jnp.dot(wg_ref[...], gw_ref[...], precision=HI, preferred_element_type=F32)


def _gate_fold(w_gate, gw):
    d = w_gate.shape[0]
    return pl.pallas_call(
        _gate_fold_body,
        out_shape=jax.ShapeDtypeStruct((d, gw.shape[1]), F32),
        grid=(1,),
        in_specs=[pl.BlockSpec(w_gate.shape, lambda i: (0, 0)), pl.BlockSpec(gw.shape, lambda i: (0, 0))],
        out_specs=pl.BlockSpec((d, gw.shape[1]), lambda i: (0, 0)),
        name="gate_fold",
    )(w_gate, gw)


def _chunk_prefix(x):
    pos = _iota(x.shape, 0) & (CHUNK - 1)
    shift = 1
    while shift < CHUNK:
        x = x + jnp.where(pos >= shift, pltpu.roll(x, shift, 0), 0.0)
        shift *= 2
    return x


def _ffn_proj_body(h_ref, m_ref, g1_ref, wi_ref, wo_ref, g2_ref, *rest, nparts):
    w_refs = rest[:nparts]
    gw_ref, gb_ref = rest[nparts:nparts + 2]
    x_ref = rest[nparts + 2]
    o_refs = rest[nparts + 3:2 * nparts + 5]
    hm_ref = rest[-1]
    m = m_ref[0]
    for r0 in range(0, h_ref.shape[0], ROW_SUB):
        rows = slice(r0, r0 + ROW_SUB)
        h = h_ref[rows, :]
        xn = _rms_mod(h, g1_ref[...], m[0:2]).astype(BF16)
        x1 = h + (0.5 * m[2:3, :]) * _swiglu(xn, wi_ref, wo_ref, hm_ref, rows)
        x_ref[rows, :] = x1
        xn2 = _rms_mod(x1, g2_ref[...], m[3:5]).astype(BF16)
        for w_ref, o_ref in zip(w_refs, o_refs[:nparts]):
            o_ref[rows, :] = jnp.dot(xn2, w_ref[...], preferred_element_type=F32).astype(o_ref.dtype)
        a = jnp.dot(xn2, gw_ref[...], preferred_element_type=F32) + gb_ref[...]
        ls = (jnp.minimum(a, 0.0) - jnp.log(1.0 + jnp.exp(-jnp.abs(a)))) * (1.0 / GLA_TAU)
        pre = _chunk_prefix(ls)
        o_refs[nparts][rows, :] = pre[:, :GLA_QK]
        pb = pre[:, GLA_QK:]
        tot = jnp.concatenate([jnp.broadcast_to(pb[c0 + CHUNK - 1:c0 + CHUNK, :], (CHUNK, GLA_QK))
                               for c0 in range(0, ROW_SUB, CHUNK)], axis=0)
        o_refs[nparts + 1][rows, :] = tot - pb + ls[:, GLA_QK:]


def _ffn_proj(h, m, g1, wi, wo, g2, w_parts, gw, gb, *, layer, tm, tiles_per_seq):
    n, d = h.shape
    ff = wo.shape[1]
    nparts = len(w_parts)
    widths = [w.shape[1] for w in w_parts]
    tok = lambda wd: pl.BlockSpec((tm, wd), lambda i: (i, 0))
    in_specs = [tok(d), pl.BlockSpec((1, 8, d), lambda i: (i // tiles_per_seq, 0, 0)), _resident((1, d)),
                _resident_layer((d, 2 * ff), layer), _resident_layer((ff, d), layer), _resident((1, d))]
    in_specs += [_resident((d, wd)) for wd in widths]
    in_specs += [_resident((d, 2 * GLA_QK)), _resident((1, 2 * GLA_QK))]
    out_shape = [jax.ShapeDtypeStruct((n, d), F32)]
    out_shape += [jax.ShapeDtypeStruct((n, wd), BF16) for wd in widths]
    out_shape += [jax.ShapeDtypeStruct((n, GLA_QK), F32)] * 2
    return pl.pallas_call(
        functools.partial(_ffn_proj_body, nparts=nparts),
        out_shape=out_shape,
        grid=(n // tm,),
        in_specs=in_specs,
        out_specs=[tok(d)] + [tok(wd) for wd in widths] + [tok(GLA_QK)] * 2,
        scratch_shapes=[pltpu.VMEM((tm, ff), BF16)],
        compiler_params=_cparams(("parallel",), 56),
        name="ffn1_in_proj",
    )(h, m, g1.reshape(1, d), wi, wo, g2.reshape(1, d), *w_parts, gw, gb)


def _out_ffn_body(x_ref, m_ref, p_ref, hy_ref, gl_ref, cv_ref, wout_ref, g_ref, wi_ref, wo_ref, fn_ref, o_ref,
                  hm_ref, *, final):
    m = m_ref[0]
    for r0 in range(0, x_ref.shape[0], ROW_SUB):
        rows = slice(r0, r0 + ROW_SUB)
        acc = jnp.dot(p_ref[rows, :].astype(BF16), wout_ref[0:D_GROUP, :], preferred_element_type=F32)
        acc += jnp.dot(hy_ref[rows, :].astype(BF16), wout_ref[D_GROUP:2 * D_GROUP, :], preferred_element_type=F32)
        acc += jnp.dot(gl_ref[rows, :].astype(BF16), wout_ref[2 * D_GROUP:3 * D_GROUP, :],
                       preferred_element_type=F32)
        acc += jnp.dot(cv_ref[rows, :].astype(BF16), wout_ref[3 * D_GROUP:, :], preferred_element_type=F32)
        x2 = x_ref[rows, :] + m[0:1, :] * acc
        xn = _rms_mod(x2, g_ref[...], m[1:3]).astype(BF16)
        out = x2 + (0.5 * m[3:4, :]) * _swiglu(xn, wi_ref, wo_ref, hm_ref, rows)
        if final:
            out = out * lax.rsqrt(jnp.mean(out * out, axis=-1, keepdims=True) + EPS) * fn_ref[...]
        o_ref[rows, :] = out


def _out_ffn(x, m, parts, w_out, g, wi, wo, *, layer, tm, tiles_per_seq, final_g=None):
    n, d = x.shape
    ff = wo.shape[1]
    final = final_g is not None
    fg = final_g if final else g
    tok = lambda wd: pl.BlockSpec((tm, wd), lambda i: (i, 0))
    return pl.pallas_call(
        functools.partial(_out_ffn_body, final=final),
        out_shape=jax.ShapeDtypeStruct((n, d), F32),
        grid=(n // tm,),
        in_specs=[tok(d), pl.BlockSpec((1, 8, d), lambda i: (i // tiles_per_seq, 0, 0)),
                  tok(D_GROUP), tok(D_GROUP), tok(D_GROUP), tok(D_GROUP), _resident_layer((d, d), layer),
                  _resident((1, d)), _resident_layer((d, 2 * ff), layer), _resident_layer((ff, d), layer),
                  _resident((1, d))],
        out_specs=tok(d),
        scratch_shapes=[pltpu.VMEM((tm, ff), BF16)],
        compiler_params=_cparams(("parallel",), 56),
        name="out_proj_ffn2",
    )(x, m, *parts, w_out, g.reshape(1, d), wi, wo, fg.reshape(1, d))


def _gla_state_body(kf, vf, bf, kb, vb, bb, s0f, s0b, sf_o, sb_o, ff_o, fb_o, stf, stb, *, cpb):
    i = pl.program_id(1)

    @pl.when(i == 0)
    def _():
        stf[...] = s0f[0]
        stb[...] = s0b[0]

    sshape = (GLA_HEADS * GLA_DV, GLA_QK)
    bmask = (_iota(sshape, 0) >> 6) == (_iota(sshape, 1) >> 5)
    tn_dims = (((0,), (0,)), ((), ()))

    def direction(k_ref, v_ref, b_ref, last, st, s_o, order):
        s = st[...]
        for ci in order:
            sl = slice(ci * CHUNK, (ci + 1) * CHUNK)
            b = b_ref[sl, :]
            tot = b[last:last + 1, :]
            kd = (k_ref[sl, :] * jnp.exp(tot - b)).astype(BF16)
            upd = lax.dot_general(v_ref[sl, :].astype(BF16), kd, tn_dims, preferred_element_type=F32)
            s_o[0, ci] = s.astype(BF16)
            s = s * jnp.exp(tot) + jnp.where(bmask, upd, 0.0)
        st[...] = s

    direction(kf, vf, bf, CHUNK - 1, stf, sf_o, range(cpb))
    direction(kb, vb, bb, 0, stb, sb_o, range(cpb - 1, -1, -1))
    ff_o[0] = stf[...]
    fb_o[0] = stb[...]


def _gla_states(k, v, lf, lb, s0f, s0b, *, bsz, seq, cpb):
    bt = cpb * CHUNK
    nb = seq // bt
    nc = seq // CHUNK
    srow = GLA_HEADS * GLA_DV

    def tf(b, i):
        return (b * nb + i, 0)

    def tb(b, i):
        return (b * nb + nb - 1 - i, 0)

    sblk = pl.BlockSpec((1, srow, GLA_QK), lambda b, i: (b, 0, 0))
    return pl.pallas_call(
        functools.partial(_gla_state_body, cpb=cpb),
        out_shape=[jax.ShapeDtypeStruct((bsz, nc, srow, GLA_QK), BF16),
                   jax.ShapeDtypeStruct((bsz, nc, srow, GLA_QK), BF16),
                   jax.ShapeDtypeStruct((bsz, srow, GLA_QK), F32),
                   jax.ShapeDtypeStruct((bsz, srow, GLA_QK), F32)],
        grid=(bsz, nb),
        in_specs=[pl.BlockSpec((bt, GLA_QK), tf), pl.BlockSpec((bt, D_GROUP), tf), pl.BlockSpec((bt, GLA_QK), tf),
                  pl.BlockSpec((bt, GLA_QK), tb), pl.BlockSpec((bt, D_GROUP), tb), pl.BlockSpec((bt, GLA_QK), tb),
                  sblk, sblk],
        out_specs=[pl.BlockSpec((1, cpb, srow, GLA_QK), lambda b, i: (b, i, 0, 0)),
                   pl.BlockSpec((1, cpb, srow, GLA_QK), lambda b, i: (b, nb - 1 - i, 0, 0)),
                   sblk, sblk],
        scratch_shapes=[pltpu.VMEM((srow, GLA_QK), F32), pltpu.VMEM((srow, GLA_QK), F32)],
        compiler_params=_cparams(("parallel", "arbitrary")),
        name="gla_states",
    )(k, v, lf, k, v, lb, s0f, s0b)


def _gla_read_body(k_ref, v_ref, q_ref, r_ref, bf_ref, bb_ref, sf_ref, sb_ref, ng_ref, o_ref, acc_ref, *, cpb):
    hrows = GLA_HEADS * CHUNK
    cpos = _iota((hrows, CHUNK), 0) & (CHUNK - 1)
    ccol = _iota((hrows, CHUNK), 1)
    lowm = cpos >= ccol
    upm = cpos <= ccol
    hq = (_iota((hrows, GLA_QK), 0) >> 6) == (_iota((hrows, GLA_QK), 1) >> 5)
    ho = (_iota((hrows, D_GROUP), 0) >> 6) == (_iota((hrows, D_GROUP), 1) >> 6)
    bavg = jnp.where((_iota((D_GROUP, D_GROUP), 0) >> 6) == (_iota((D_GROUP, D_GROUP), 1) >> 6),
                     1.0 / GLA_DV, 0.0).astype(BF16)
    nt_dims = (((1,), (1,)), ((), ()))

    bf_ = bf_ref[...]
    bb_ = bb_ref[...]
    qs = q_ref[...].astype(F32) * (GLA_DK ** -0.5)
    kk = k_ref[...]
    qef = (qs * jnp.exp(bf_)).astype(BF16)
    qeb = (qs * jnp.exp(bb_)).astype(BF16)
    kef = (kk * jnp.exp(-bf_)).astype(BF16)
    keb = (kk * jnp.exp(-bb_)).astype(BF16)
    vb16 = v_ref[...].astype(BF16)
    zero = jnp.zeros((), BF16)
    for ci in range(cpb):
        sl = slice(ci * CHUNK, (ci + 1) * CHUNK)
        qf4 = jnp.where(hq, jnp.concatenate([qef[sl]] * GLA_HEADS, axis=0), zero)
        qb4 = jnp.where(hq, jnp.concatenate([qeb[sl]] * GLA_HEADS, axis=0), zero)
        af = lax.dot_general(qf4, kef[sl], nt_dims, preferred_element_type=F32)
        ab = lax.dot_general(qb4, keb[sl], nt_dims, preferred_element_type=F32)
        att = (jnp.where(lowm, af, 0.0) + jnp.where(upm, ab, 0.0)).astype(BF16)
        oall = jnp.dot(att, vb16[sl], preferred_element_type=F32)
        om = jnp.where(ho, oall, 0.0)
        o = om[0:CHUNK] + om[CHUNK:2 * CHUNK] + om[2 * CHUNK:3 * CHUNK] + om[3 * CHUNK:4 * CHUNK]
        qcat = jnp.concatenate([qef[sl], qeb[sl]], axis=1)
        scat = jnp.concatenate([sf_ref[0, ci], sb_ref[0, ci]], axis=1)
        acc_ref[sl, :] = o + lax.dot_general(qcat, scat, nt_dims, preferred_element_type=F32)
    o = acc_ref[...]
    ms = _chunk_sum_rhs(o * o, bavg)
    o_ref[...] = (o * lax.rsqrt(ms + EPS) * ng_ref[...] * _silu(r_ref[...].astype(F32))).astype(o_ref.dtype)


def _chunk_sum_rhs(x, mat):
    hi, lo = _split_bf16(x)
    return jnp.dot(hi, mat, preferred_element_type=F32) + jnp.dot(lo, mat, preferred_element_type=F32)


def _gla_read(k, v, q, r, lf, lb, sf, sb, ng, *, bsz, seq, cpb):
    bt = cpb * CHUNK
    nb = seq // bt
    srow = GLA_HEADS * GLA_DV
    n = bsz * seq

    def tk(i):
        return (i, 0)

    sspec = pl.BlockSpec((1, cpb, srow, GLA_QK), lambda i: (i // nb, i % nb, 0, 0))
    return pl.pallas_call(
        functools.partial(_gla_read_body, cpb=cpb),
        out_shape=jax.ShapeDtypeStruct((n, D_GROUP), BF16),
        scratch_shapes=[pltpu.VMEM((bt, D_GROUP), F32)],
        grid=(bsz * nb,),
        in_specs=[pl.BlockSpec((bt, GLA_QK), tk), pl.BlockSpec((bt, D_GROUP), tk),
                  pl.BlockSpec((bt, GLA_QK), lambda i: (i, 1)), pl.BlockSpec((bt, D_GROUP), tk),
                  pl.BlockSpec((bt, GLA_QK), tk), pl.BlockSpec((bt, GLA_QK), tk),
                  sspec, sspec, pl.BlockSpec((1, D_GROUP), lambda i: (0, 0))],
        out_specs=pl.BlockSpec((bt, D_GROUP), tk),
        compiler_params=_cparams(("parallel",)),
        name="gla_readout",
    )(k, v, q, r, lf, lb, sf, sb, ng)


def _box_matrix(n, w):
    pos = np.arange(n)
    lo = np.clip(pos - w // 2, 0, n)
    hi = np.clip(pos - w // 2 + w, 0, n)
    col = np.arange(n)[None, :]
    return ((col >= lo[:, None]) & (col < hi[:, None])).astype(np.float32)


def _lane_windows(shape):
    w = jnp.left_shift(2, _iota(shape, 1) >> 6)
    return w, w >> 1


def _box_count(pos, w, half, n):
    return jnp.minimum(pos - half + w, n) - jnp.maximum(pos - half, 0)


def _split_bf16(x):
    hi = x.astype(BF16)
    lo = (x - hi.astype(F32)).astype(BF16)
    return hi, lo


def _pool2d_body(cur_ref, prev_ref, next_ref, pc_ref, w_ref, sc_ref, o_ref, ycol, *, tiles, rows):
    i = pl.program_id(1)
    tm = cur_ref.shape[0]
    hb = prev_ref.shape[0]
    sub = 2 * GRID_W
    pflag = jnp.where(i > 0, 1.0, 0.0)
    nflag = jnp.where(i < tiles - 1, 1.0, 0.0)

    def colpool(x):
        halves = []
        for half in range(2):
            lsl = slice(half * LANE, (half + 1) * LANE)
            ys = [jnp.dot(pc_ref[wi], x[:, lsl], preferred_element_type=F32) for wi in (2 * half, 2 * half + 1)]
            lane = _iota((sub, LANE), 1)
            halves.append(jnp.where(lane < POOL_CH, ys[0], ys[1]))
        return jnp.concatenate(halves, axis=1)

    for s in range(hb // sub):
        ycol[s * sub:(s + 1) * sub, :] = colpool(prev_ref[s * sub:(s + 1) * sub, :]) * pflag
    for s in range(tm // sub):
        ycol[hb + s * sub:hb + (s + 1) * sub, :] = colpool(cur_ref[s * sub:(s + 1) * sub, :])
    for s in range(hb // sub):
        ycol[hb + tm + s * sub:hb + tm + (s + 1) * sub, :] = colpool(next_ref[s * sub:(s + 1) * sub, :]) * nflag

    rc = 256
    wl, half = _lane_windows((rc, D_GROUP))
    narrow = _iota((rc, LANE), 1) < POOL_CH
    for r0 in range(0, tm, rc):
        def band(lo, hi, lsl):
            base = hb + r0
            acc = ycol[base + GRID_W * lo:base + GRID_W * lo + rc, lsl]
            for dd in range(lo + 1, hi):
                acc = acc + ycol[base + GRID_W * dd:base + GRID_W * dd + rc, lsl]
            return acc

        left, right = slice(0, LANE), slice(LANE, 2 * LANE)
        z2 = band(-1, 1, left)
        z4 = z2 + band(-2, -1, left) + band(1, 2, left)
        z8 = band(-4, 4, right)
        z16 = z8 + band(-8, -4, right) + band(4, 8, right)
        z = jnp.concatenate([jnp.where(narrow, z2, z4), jnp.where(narrow, z8, z16)], axis=1)
        tok = _iota((rc, D_GROUP), 0) + (i * tm + r0)
        rcnt = _box_count(tok >> 6, wl, half, rows)
        ccnt = _box_count(tok & (GRID_W - 1), wl, half, GRID_W)
        pooled = z / (rcnt * ccnt).astype(F32)
        dlt = (pooled - cur_ref[r0:r0 + rc, :]).astype(BF16)
        o_ref[r0:r0 + rc, :] = (jnp.dot(dlt, w_ref[...], preferred_element_type=F32) * sc_ref[...]).astype(o_ref.dtype)


def _pool2d(u, wbd, scale, *, bsz, seq):
    tm = 1024
    hb = 512
    tiles = seq // tm
    r = tm // hb
    nhb = seq // hb
    pc = np.stack([np.kron(np.eye(2, dtype=np.float32), _box_matrix(GRID_W, w)) for w in POOL_WINDOWS])
    return pl.pallas_call(
        functools.partial(_pool2d_body, tiles=tiles, rows=seq // GRID_W),
        out_shape=jax.ShapeDtypeStruct((bsz * seq, D_GROUP), BF16),
        grid=(bsz, tiles),
        in_specs=[pl.BlockSpec((tm, D_GROUP), lambda b, i: (b * tiles + i, 0)),
                  pl.BlockSpec((hb, D_GROUP), lambda b, i: (b * nhb + jnp.maximum(i * r - 1, 0), 0)),
                  pl.BlockSpec((hb, D_GROUP), lambda b, i: (b * nhb + jnp.minimum(i * r + r, nhb - 1), 0)),
                  pl.BlockSpec((4, 2 * GRID_W, 2 * GRID_W), lambda b, i: (0, 0, 0)),
                  pl.BlockSpec((D_GROUP, D_GROUP), lambda b, i: (0, 0)),
                  pl.BlockSpec((1, D_GROUP), lambda b, i: (0, 0))],
        out_specs=pl.BlockSpec((tm, D_GROUP), lambda b, i: (b * tiles + i, 0)),
        scratch_shapes=[pltpu.VMEM((tm + 2 * hb, D_GROUP), F32)],
        compiler_params=_cparams(("parallel", "parallel")),
        name="pool2d",
    )(u, u, u, jnp.asarray(pc, BF16), wbd, scale)


def _pool1d_body(x_ref, p_ref, w_ref, sc_ref, o_ref):
    x = x_ref[...]
    n = x.shape[0]
    ys = [jnp.dot(p_ref[wi], x, preferred_element_type=F32) for wi in range(4)]
    wl, half = _lane_windows((n, D_GROUP))
    z = jnp.where(wl == 2, ys[0], jnp.where(wl == 4, ys[1], jnp.where(wl == 8, ys[2], ys[3])))
    cnt = _box_count(_iota((n, D_GROUP), 0), wl, half, n)
    dlt = (z / cnt.astype(F32) - x).astype(BF16)
    o_ref[...] = (jnp.dot(dlt, w_ref[...], preferred_element_type=F32) * sc_ref[...]).astype(o_ref.dtype)


def _pool1d(u, wbd, scale, *, bsz, seq):
    pm = np.stack([_box_matrix(seq, w) for w in POOL_WINDOWS])
    return pl.pallas_call(
        _pool1d_body,
        out_shape=jax.ShapeDtypeStruct((bsz * seq, D_GROUP), BF16),
        grid=(bsz,),
        in_specs=[pl.BlockSpec((seq, D_GROUP), lambda b: (b, 0)),
                  pl.BlockSpec((4, seq, seq), lambda b: (0, 0, 0)),
                  pl.BlockSpec((D_GROUP, D_GROUP), lambda b: (0, 0)),
                  pl.BlockSpec((1, D_GROUP), lambda b: (0, 0))],
        out_specs=pl.BlockSpec((seq, D_GROUP), lambda b: (b, 0)),
        compiler_params=_cparams(("parallel",)),
        name="pool1d",
    )(u, jnp.asarray(pm, BF16), wbd, scale)


def _fill_halo(buf, cur, prev, nxt, i, tps, hb, tm, pre):
    first = (i % tps) == 0
    last = (i % tps) == tps - 1
    buf[0:hb, :] = jnp.where(first, 0.0, pre(prev[...]))
    buf[hb:hb + tm, :] = pre(cur[...])
    buf[hb + tm:hb + tm + hb, :] = jnp.where(last, 0.0, pre(nxt[...]))


SUBLANES = 8


def _tap_phases(taps, off):
    return sorted({(off + j) % SUBLANES for j in range(taps)} - {0})


def _fill_phases(sh, buf, phases):
    rows = sh.shape[1]
    for slot, s in enumerate(phases):
        sh[slot, :, :] = buf[s:s + rows, :]


def _dwconv(buf, sh, phases, w_ref, r0, rc, taps, off):
    acc = None
    for j in range(taps):
        s, q = (off + j) % SUBLANES, (off + j) // SUBLANES
        lo = r0 + SUBLANES * q
        src = buf[lo:lo + rc, :] if s == 0 else sh[phases.index(s), lo:lo + rc, :]
        term = src * w_ref[j:j + 1, :]
        acc = term if acc is None else acc + term
    return acc


def _conf_body(cur, prev, nxt, w_ref, b_ref, lg_ref, lb_ref, o_ref, buf, sh, *, tps):
    i = pl.program_id(0)
    tm = cur.shape[0]
    hb = prev.shape[0]

    def glu(u):
        u = u.astype(F32)
        return u[:, :D_GROUP] * jax.nn.sigmoid(u[:, D_GROUP:])

    _fill_halo(buf, cur, prev, nxt, i, tps, hb, tm, glu)
    rc = 128
    off = hb - (CONV_WIDTH - 1) // 2
    phases = _tap_phases(CONV_WIDTH, off)
    _fill_phases(sh, buf, phases)
    for r0 in range(0, tm, rc):
        h = _dwconv(buf, sh, phases, w_ref, r0, rc, CONV_WIDTH, off) + b_ref[...]
        mu = jnp.mean(h, axis=-1, keepdims=True)
        hc = h - mu
        var = jnp.mean(hc * hc, axis=-1, keepdims=True)
        o_ref[r0:r0 + rc, :] = _silu(hc * lax.rsqrt(var + EPS) * lg_ref[...] + lb_ref[...]).astype(o_ref.dtype)


def _halo_specs(tm, hb, width, nrows):
    r = tm // hb
    nhb = nrows // hb
    return [pl.BlockSpec((tm, width), lambda i: (i, 0)),
            pl.BlockSpec((hb, width), lambda i: (jnp.maximum(i * r - 1, 0), 0)),
            pl.BlockSpec((hb, width), lambda i: (jnp.minimum(i * r + r, nhb - 1), 0))]


def _conformer(u, w, b, lg, lb, *, seq, tm):
    n = u.shape[0]
    hb = 16
    nph = len(_tap_phases(CONV_WIDTH, hb - (CONV_WIDTH - 1) // 2))
    vec = pl.BlockSpec((1, D_GROUP), lambda i: (0, 0))
    return pl.pallas_call(
        functools.partial(_conf_body, tps=seq // tm),
        out_shape=jax.ShapeDtypeStruct((n, D_GROUP), BF16),
        grid=(n // tm,),
        in_specs=_halo_specs(tm, hb, 2 * D_GROUP, n) + [pl.BlockSpec((CONV_WIDTH, D_GROUP), lambda i: (0, 0)),
                                                        vec, vec, vec],
        out_specs=pl.BlockSpec((tm, D_GROUP), lambda i: (i, 0)),
        scratch_shapes=[pltpu.VMEM((tm + 2 * hb, D_GROUP), F32),
                        pltpu.VMEM((nph, tm + 2 * hb - SUBLANES, D_GROUP), F32)],
        compiler_params=_cparams(("parallel",)),
        name="conformer_conv",
    )(u, u, u, w, b.reshape(1, -1), lg.reshape(1, -1), lb.reshape(1, -1))


def _short_body(cur, prev, nxt, w_ref, b_ref, v_ref, x1_ref, x2_ref, buf, sh, *, tps):
    i = pl.program_id(0)
    tm = cur.shape[0]
    hb = prev.shape[0]
    _fill_halo(buf, cur, prev, nxt, i, tps, hb, tm, lambda u: u.astype(F32))
    rc = 128
    off = hb - (HY_SHORT - 1) // 2
    phases = _tap_phases(HY_SHORT, off)
    _fill_phases(sh, buf, phases)
    for r0 in range(0, tm, rc):
        uc = _dwconv(buf, sh, phases, w_ref, r0, rc, HY_SHORT, off) + b_ref[...]
        v_ref[r0:r0 + rc, :] = uc[:, :D_GROUP]
        x1_ref[r0:r0 + rc, :] = uc[:, D_GROUP:2 * D_GROUP]
        x2_ref[r0:r0 + rc, :] = uc[:, 2 * D_GROUP:]


def _hy_short(u, w, b, *, seq, tm):
    n = u.shape[0]
    hb = 16
    wd = 3 * D_GROUP
    nph = len(_tap_phases(HY_SHORT, hb - (HY_SHORT - 1) // 2))
    ospec = pl.BlockSpec((tm, D_GROUP), lambda i: (i, 0))
    return pl.pallas_call(
        functools.partial(_short_body, tps=seq // tm),
        out_shape=[jax.ShapeDtypeStruct((n, D_GROUP), F32)] * 3,
        grid=(n // tm,),
        in_specs=_halo_specs(tm, hb, wd, n) + [pl.BlockSpec((HY_SHORT, wd), lambda i: (0, 0)),
                                               pl.BlockSpec((1, wd), lambda i: (0, 0))],
        out_specs=[ospec, ospec, ospec],
        scratch_shapes=[pltpu.VMEM((tm + 2 * hb, wd), F32),
                        pltpu.VMEM((nph, tm + 2 * hb - SUBLANES, wd), F32)],
        compiler_params=_cparams(("parallel",)),
        name="hyena_short_conv",
    )(u, u, u, w, b.reshape(1, -1))


def _filter_features(n):
    i = np.arange(n, dtype=np.float64)
    t = np.linspace(0.0, 1.0, n, dtype=np.float32).astype(np.float64)
    wpos = ((2.0 * math.pi / n) * np.arange(n, dtype=np.float32)).astype(np.float32)
    bands = np.linspace(1e-4, HY_BANDS - 1, HY_BANDS, dtype=np.float32)
    arg = (bands[None, :] * wpos[:, None]).astype(np.float32).astype(np.float64)
    z = np.zeros((n, 64), np.float32)
    z[:, 0] = t
    z[:, 1:1 + HY_BANDS] = np.cos(arg)
    z[:, 1 + HY_BANDS:HY_EMB] = -np.sin(arg)
    del i
    return z


def _filter_body(z_ref, w1_ref, b1_ref, w2_ref, b2_ref, w3a_ref, w3b_ref, d_ref, hf_ref, hb_ref, s_ref):
    i = pl.program_id(0)
    half = z_ref.shape[0]
    z = z_ref[...]
    h = jnp.sin(jnp.dot(z, w1_ref[...], precision=HI, preferred_element_type=F32) + b1_ref[...])
    h = jnp.sin(jnp.dot(h, w2_ref[...], precision=HI, preferred_element_type=F32) + b2_ref[...])
    absd = jnp.abs(d_ref[...])
    h = h.astype(BF16)
    tot = None
    for part, (w3_ref, tcol) in enumerate(((w3a_ref, 0), (w3b_ref, 64))):
        hp = jnp.dot(h, w3_ref[...], preferred_element_type=F32)
        hp = hp * jnp.exp(-z[:, tcol:tcol + 1] * absd)
        rows = slice(part * half, (part + 1) * half)
        for o in range(2):
            c0 = 2 * D_GROUP * o
            hf_ref[rows, o * D_GROUP:(o + 1) * D_GROUP] = hp[:, c0:c0 + D_GROUP]
            hb_ref[rows, o * D_GROUP:(o + 1) * D_GROUP] = hp[:, c0 + D_GROUP:c0 + 2 * D_GROUP]
        part_sum = jnp.sum(jnp.abs(hp), axis=0, keepdims=True)
        tot = part_sum if tot is None else tot + part_sum

    @pl.when(i == 0)
    def _():
        s_ref[...] = jnp.zeros_like(s_ref)

    s_ref[...] += tot


def _hy_filter(n, w1, b1, w2, b2, w3, deltas, *, t2_major):
    tm = min(n, 512)
    half = tm // 2
    z = _filter_features(n)
    if t2_major:
        z = z.reshape(n // DFT2, DFT2, 64).transpose(1, 0, 2).reshape(n, 64)
    zt = z.reshape(n // tm, 2, half, 64)
    z2 = jnp.asarray(np.concatenate([zt[:, 0], zt[:, 1]], axis=-1).reshape(n // 2, 2 * 64))
    w1p = jnp.zeros((64, HY_FFN), F32).at[:HY_EMB].set(w1)
    zero = jnp.zeros((64, HY_FFN), F32)
    bd = lambda w: jnp.concatenate([jnp.concatenate([w, zero], axis=1), jnp.concatenate([zero, w], axis=1)], axis=0)
    w3 = w3.astype(BF16)
    zero3 = jnp.zeros_like(w3)
    full = lambda shape: pl.BlockSpec(shape, lambda i: (0, 0))
    tok = pl.BlockSpec((tm, 2 * D_GROUP), lambda i: (i, 0))
    return pl.pallas_call(
        _filter_body,
        out_shape=[jax.ShapeDtypeStruct((n, 2 * D_GROUP), F32), jax.ShapeDtypeStruct((n, 2 * D_GROUP), F32),
                   jax.ShapeDtypeStruct((1, HY_FILTER_CH), F32)],
        grid=(n // tm,),
        in_specs=[pl.BlockSpec((half, 2 * 64), lambda i: (i, 0)), full((2 * 64, 2 * HY_FFN)), full((1, 2 * HY_FFN)),
                  full((2 * HY_FFN, 2 * HY_FFN)), full((1, 2 * HY_FFN)), full((2 * HY_FFN, HY_FILTER_CH)),
                  full((2 * HY_FFN, HY_FILTER_CH)), full((1, HY_FILTER_CH))],
        out_specs=[tok, tok, full((1, HY_FILTER_CH))],
        compiler_params=_cparams(("arbitrary",)),
        name="hyena_filter_mlp",
    )(z2, bd(w1p), jnp.tile(b1.reshape(1, -1), (1, 2)), bd(w2), jnp.tile(b2.reshape(1, -1), (1, 2)),
      jnp.concatenate([w3, zero3], axis=0), jnp.concatenate([zero3, w3], axis=0), deltas.reshape(1, -1))


@functools.lru_cache(maxsize=None)
def _dft_tables(n1):
    nn = n1 * DFT2
    j = np.arange(DFT2)
    th = 2.0 * np.pi * ((np.outer(j, j)) % DFT2) / DFT2
    fr, fi = np.cos(th), -np.sin(th)
    m_fwd = np.block([[fr, -fi], [fi, fr]])
    m_inv = np.block([[fr, fi], [-fi, fr]])
    hh = n1 // 2
    k1 = np.arange(n1)[None, :, None]
    t1 = np.arange(hh)[None, None, :]
    t2 = np.arange(DFT2)[:, None, None]
    ph = 2.0 * np.pi * ((k1 * (DFT2 * t1 + t2)) % nn) / nn
    ar, ai = np.cos(ph), -np.sin(ph)
    a_fwd = np.concatenate([np.concatenate([ar, -ai], axis=2), np.concatenate([ai, ar], axis=2)], axis=1)
    a_real = np.concatenate([ar, ai], axis=1)
    t1r = np.where(t2 >= 1, n1 - 1 - t1, (n1 - t1) % n1)
    phr = 2.0 * np.pi * ((k1 * (DFT2 * t1r + t2)) % nn) / nn
    a_rev = np.concatenate([np.cos(phr), -np.sin(phr)], axis=1)
    a_rev[0, :, 0] = 0.0
    pht = np.transpose(ph, (0, 2, 1))
    cr, ci = np.cos(pht) / nn, np.sin(pht) / nn
    a_inv = np.concatenate([np.concatenate([cr, -ci], axis=2), np.concatenate([ci, cr], axis=2)], axis=1)
    f32 = lambda a: np.ascontiguousarray(a, dtype=np.float32)
    return dict(m_fwd=f32(m_fwd), m_inv=f32(m_inv), a_fwd=f32(a_fwd), a_real=f32(a_real), a_rev=f32(a_rev),
                a_inv=f32(a_inv))


def _fft_a_body(x_ref, m_ref, o_ref, *, tj):
    n1 = o_ref.shape[2]
    for e in range(tj):
        x = jnp.concatenate([x_ref[0, :, e, :], x_ref[1, :, e, :]], axis=0).astype(BF16)
        a = jnp.dot(m_ref[e], x, preferred_element_type=F32)
        o_ref[0, e] = a[:n1]
        o_ref[1, e] = a[n1:]


def _fft_a(x4, m, *, n1, tj):
    c = x4.shape[3]
    return pl.pallas_call(
        functools.partial(_fft_a_body, tj=tj),
        out_shape=jax.ShapeDtypeStruct((2, DFT2, n1, c), F32),
        grid=(DFT2 // tj,),
        in_specs=[pl.BlockSpec((2, n1 // 2, tj, c), lambda j: (0, 0, j, 0)),
                  pl.BlockSpec((tj, 2 * n1, n1), lambda j: (j, 0, 0))],
        out_specs=pl.BlockSpec((2, tj, n1, c), lambda j: (0, j, 0, 0)),
        compiler_params=_cparams(("parallel",), 48),
        name="fft_stage_a",
    )(x4, m)


def _fft_b_body(a_ref, mf_ref, mi_ref, g_ref, o_ref, *, kg):
    for kk in range(kg):
        x = jnp.concatenate([a_ref[0, :, kk, :], a_ref[1, :, kk, :]], axis=0).astype(BF16)
        xf = jnp.dot(mf_ref[...], x, preferred_element_type=F32)
        xr, xi = xf[:DFT2], xf[DFT2:]
        gr, gi = g_ref[0, kk].astype(F32), g_ref[1, kk].astype(F32)
        y = jnp.concatenate([xr * gr - xi * gi, xr * gi + xi * gr], axis=0).astype(BF16)
        bf = jnp.dot(mi_ref[...], y, preferred_element_type=F32)
        o_ref[0, kk] = bf[:DFT2]
        o_ref[1, kk] = bf[DFT2:]


def _fft_b(a, g, order, tabs, *, n1):
    c = a.shape[3]
    kg = FFT_GROUP
    blk = pl.BlockSpec((2, kg, DFT2, c), lambda k: (0, k, 0, 0))
    mat = pl.BlockSpec((2 * DFT2, 2 * DFT2), lambda k: (0, 0))
    return pl.pallas_call(
        functools.partial(_fft_b_body, kg=kg),
        out_shape=jax.ShapeDtypeStruct((2, n1, DFT2, c), F32),
        grid=(n1 // kg,),
        in_specs=[pl.BlockSpec((2, DFT2, kg, c), lambda k: (0, 0, k, 0)), mat, mat,
                  pl.BlockSpec((2, kg, DFT2, c), lambda k: (0, k, 0, order))],
        out_specs=blk,
        compiler_params=_cparams(("parallel",), 48),
        name="fft_stage_b",
    )(a, tabs["m_fwd"], tabs["m_inv"], g)


def _fft_bf_body(a_ref, mf_ref, o_ref, *, kg):
    for kk in range(kg):
        x = jnp.concatenate([a_ref[0, :, kk, :], a_ref[1, :, kk, :]], axis=0).astype(BF16)
        xf = jnp.dot(mf_ref[...], x, preferred_element_type=F32)
        o_ref[0, kk] = xf[:DFT2].astype(BF16)
        o_ref[1, kk] = xf[DFT2:].astype(BF16)


def _fft_b_forward(a, tabs, *, n1):
    c = a.shape[3]
    kg = FFT_GROUP
    return pl.pallas_call(
        functools.partial(_fft_bf_body, kg=kg),
        out_shape=jax.ShapeDtypeStruct((2, n1, DFT2, c), BF16),
        grid=(n1 // kg,),
        in_specs=[pl.BlockSpec((2, DFT2, kg, c), lambda k: (0, 0, k, 0)),
                  pl.BlockSpec((2 * DFT2, 2 * DFT2), lambda k: (0, 0))],
        out_specs=pl.BlockSpec((2, kg, DFT2, c), lambda k: (0, k, 0, 0)),
        compiler_params=_cparams(("parallel",), 48),
        name="fft_stage_b_filter",
    )(a, tabs["m_fwd"])


def _fft_ai_body(b_ref, m_ref, v_ref, x_ref, bias_ref, o_ref, *, tj):
    h = o_ref.shape[1]
    for e in range(tj):
        b = jnp.concatenate([b_ref[0, :, e, :], b_ref[1, :, e, :]], axis=0).astype(BF16)
        y = jnp.dot(m_ref[e], b, preferred_element_type=F32)
        o_ref[0, :, e, :] = y[:h]
        o_ref[1, :, e, :] = y[h:]
    o_ref[...] = x_ref[...] * (o_ref[...] + v_ref[...] * bias_ref[...])


def _fft_a_inv(b, m, v4, xm4, bias, *, n1, tj):
    c = b.shape[3]
    half = pl.BlockSpec((2, n1 // 2, tj, c), lambda j: (0, 0, j, 0))
    return pl.pallas_call(
        functools.partial(_fft_ai_body, tj=tj),
        out_shape=jax.ShapeDtypeStruct((2, n1 // 2, DFT2, c), F32),
        grid=(DFT2 // tj,),
        in_specs=[pl.BlockSpec((2, n1, tj, c), lambda j: (0, 0, j, 0)),
                  pl.BlockSpec((tj, n1, 2 * n1), lambda j: (j, 0, 0)),
                  half, half, pl.BlockSpec((1, c), lambda j: (0, 0))],
        out_specs=half,
        compiler_params=_cparams(("parallel",), 48),
        name="fft_stage_a_inv",
    )(b, m, v4, xm4, bias)


def _filt_a_body(hj_ref, hz_ref, hr_ref, s_ref, mf_ref, mr_ref, o_ref, *, tj):
    n1 = o_ref.shape[2]
    s = s_ref[...]
    for e in range(tj):
        src = hz_ref if e == 0 else hr_ref
        me = 0 if e == 0 else tj - e
        for o in range(2):
            c0 = 2 * D_GROUP * o
            inv = 1.0 / (s[:, c0:c0 + D_GROUP] + s[:, c0 + D_GROUP:c0 + 2 * D_GROUP] + EPS)
            lsl = slice(o * D_GROUP, (o + 1) * D_GROUP)
            hf = hj_ref[e, :, lsl].astype(BF16)
            hb = src[me, :, lsl].astype(BF16)
            a = (jnp.dot(mf_ref[e], hf, preferred_element_type=F32)
                 + jnp.dot(mr_ref[e], hb, preferred_element_type=F32)) * inv
            o_ref[0, e, :, lsl] = a[:n1]
            o_ref[1, e, :, lsl] = a[n1:]


def _filt_a(hf, hb, colsum, tabs, *, n1):
    tj = FFT_GROUP
    nj = DFT2 // tj
    wd = 2 * D_GROUP
    shp = (DFT2, n1 // 2, wd)
    blk = lambda fn: pl.BlockSpec((tj, n1 // 2, wd), fn)
    mat = pl.BlockSpec((tj, 2 * n1, n1 // 2), lambda j: (j, 0, 0))
    return pl.pallas_call(
        functools.partial(_filt_a_body, tj=tj),
        out_shape=jax.ShapeDtypeStruct((2, DFT2, n1, wd), F32),
        grid=(nj,),
        in_specs=[blk(lambda j: (j, 0, 0)), blk(lambda j: ((nj - j) % nj, 0, 0)), blk(lambda j: (nj - 1 - j, 0, 0)),
                  pl.BlockSpec((1, HY_FILTER_CH), lambda j: (0, 0)), mat, mat],
        out_specs=pl.BlockSpec((2, tj, n1, wd), lambda j: (0, j, 0, 0)),
        compiler_params=_cparams(("parallel",), 48),
        name="filter_stage_a",
    )(hf.reshape(shp), hb.reshape(shp), hb.reshape(shp), colsum, tabs["a_real"], tabs["a_rev"])


def _hyena_long(v, x1, x2, hf, hb, colsum, bias, *, seq):
    n1 = 2 * seq // DFT2
    tabs = {k: jnp.asarray(a).astype(BF16) for k, a in _dft_tables(n1).items()}
    tj = FFT_GROUP
    fa = _filt_a(hf, hb, colsum, tabs, n1=n1)
    g = _fft_b_forward(fa, tabs, n1=n1)
    shp = (2, n1 // 2, DFT2, D_GROUP)
    z = v.reshape(shp)
    for order, xm in ((0, x1.reshape(shp)), (1, x2.reshape(shp))):
        a = _fft_a(z, tabs["a_fwd"], n1=n1, tj=tj)
        b = _fft_b(a, g, order, tabs, n1=n1)
        z = _fft_a_inv(b, tabs["a_inv"], z, xm, bias[order].reshape(1, D_GROUP), n1=n1, tj=tj)
    return z.reshape(2 * seq, D_GROUP)


def _hyena_ctx_body(v_ref, x1_ref, x2_ref, hf_ref, hb_ref, s_ref, bias_ref, cm_ref, sm_ref, ct_ref, st_ref, o_ref):
    n = v_ref.shape[1]
    cm, sm, ct, st = cm_ref[...], sm_ref[...], ct_ref[...], st_ref[...]
    s = s_ref[...]
    row0 = _iota((n, D_GROUP), 0) == 0
    dot = lambda a, b: jnp.dot(a, b.astype(BF16), preferred_element_type=F32)
    zr, zi = v_ref[0], v_ref[1]
    for order, xm in ((0, x1_ref), (1, x2_ref)):
        c0 = 2 * D_GROUP * order
        inv = 1.0 / (s[:, c0:c0 + D_GROUP] + s[:, c0 + D_GROUP:c0 + 2 * D_GROUP] + EPS)
        lsl = slice(order * D_GROUP, (order + 1) * D_GROUP)
        hf = hf_ref[:, lsl] * inv
        hb = jnp.where(row0, 0.0, hb_ref[:, lsl] * inv)
        gr = dot(cm, hf + hb)
        gi = dot(sm, hb - hf)
        xr = dot(cm, zr) + dot(sm, zi)
        xi = dot(cm, zi) - dot(sm, zr)
        yr = xr * gr - xi * gi
        yi = xr * gi + xi * gr
        cr = dot(ct, yr) - dot(st, yi)
        ci = dot(ct, yi) + dot(st, yr)
        bias = bias_ref[order:order + 1, :]
        zr = xm[0] * (cr + zr * bias)
        zi = xm[1] * (ci + zi * bias)
    o_ref[0] = zr
    o_ref[1] = zi


def _hyena_ctx(v, x1, x2, hf, hb, colsum, bias, *, seq):
    nn = 2 * seq
    k = np.arange(nn)
    t = np.arange(seq)
    th = 2.0 * np.pi * (np.outer(k, t) % nn) / nn
    cm, sm = np.cos(th), np.sin(th)
    consts = [jnp.asarray(a, F32).astype(BF16) for a in (cm, sm, cm.T / nn, sm.T / nn)]
    shp = (2, seq, D_GROUP)
    full3 = pl.BlockSpec(shp, lambda i: (0, 0, 0))
    f2 = lambda a: pl.BlockSpec(a.shape, lambda i: (0, 0))
    args = [hf, hb, colsum, bias] + consts
    out = pl.pallas_call(
        _hyena_ctx_body,
        out_shape=jax.ShapeDtypeStruct(shp, F32),
        grid=(1,),
        in_specs=[full3, full3, full3] + [f2(a) for a in args],
        out_specs=full3,
        compiler_params=_cparams(("arbitrary",)),
        name="hyena_ctx",
    )(v.reshape(shp), x1.reshape(shp), x2.reshape(shp), *args)
    return out.reshape(2 * seq, D_GROUP)


def _mod_rows(mod_l, rows, first, count):
    m = mod_l[rows, first:first + count, :]
    return jnp.pad(m, ((0, 0), (0, 8 - count), (0, 0)))


def _split_w_in(w_in):
    wb = w_in.astype(BF16)
    parts = dict(kq=jnp.concatenate([wb[:, COL_K:COL_V], wb[:, COL_Q:COL_R]], axis=1), v=wb[:, COL_V:COL_GF],
                 r=wb[:, COL_R:COL_POOL], pool=wb[:, COL_POOL:COL_HY], hy=wb[:, COL_HY:COL_CONV],
                 conv=wb[:, COL_CONV:P_IN])
    gate = jnp.pad(w_in[:, COL_GF:COL_Q], ((0, 0), (0, LANE - 2 * GLA_LOWRANK)))
    return parts, gate


def _mix(u, p, *, bsz, seq, is_ctx, states, filt):
    sf, sb = states
    cpb = min(8, seq // CHUNK)
    if is_ctx:
        pool = _pool1d(u["pool"], p["pool_wbd"], p["pool_scale"], bsz=bsz, seq=seq)
    else:
        pool = _pool2d(u["pool"], p["pool_wbd"], p["pool_scale"], bsz=bsz, seq=seq)
    tmc = min(seq, 1024)
    v, x1, x2 = _hy_short(u["hy"], p["hy_short_w"], p["hy_short_b"], seq=seq, tm=tmc)
    if is_ctx:
        hy = _hyena_ctx(v, x1, x2, *filt, p["hy_bias"], seq=seq)
    else:
        hy = _hyena_long(v, x1, x2, *filt, p["hy_bias"], seq=seq)
    gla = _gla_read(u["kq"], u["v"], u["kq"], u["r"], u["lf"], u["lb"], sf, sb, p["gla_ng"],
                    bsz=bsz, seq=seq, cpb=cpb)
    conv = _conformer(u["conv"], p["conv_dw_w"], p["conv_dw_b"], p["conv_ln_g"], p["conv_ln_b"], seq=seq, tm=tmc)
    return [pool, hy, gla, conv]


def kernel(x, c, ctx, c_ctx, ada_w, ada_b, ffn1_norm, ffn1_wi, ffn1_wo, mix_norm, w_in, w_out, pool_w, pool_scale, hy_short_w, hy_short_b, hy_w1, hy_b1, hy_w2, hy_b2, hy_w3, hy_deltas, hy_bias, gla_gw_f, gla_gb_f, gla_gw_b, gla_gb_b, gla_norm, conv_dw_w, conv_dw_b, conv_ln_g, conv_ln_b, ffn2_norm, ffn2_wi, ffn2_wo, final_norm):
    bsz, seq, d = x.shape
    clen = ctx.shape[1]
    depth = ada_w.shape[0]
    assert bsz == 2, "the Hyena transform packs exactly two batch rows into one complex signal"
    xs = x.reshape(bsz * seq, d)
    cs = ctx.reshape(bsz * clen, d)
    cc = jnp.zeros((8, d), F32).at[0:bsz].set(c).at[bsz].set(c_ctx)
    mod = _modulation(cc, ada_w, ada_b).reshape(depth, 8, N_MOD, d)
    xrows = np.arange(bsz)
    crows = np.full((1,), bsz)
    tmx = 512
    tps_x = seq // tmx
    tmc = bsz * clen
    zero_state = jnp.zeros((bsz, GLA_HEADS * GLA_DV, GLA_QK), F32)
    wi1, wo1, wi2, wo2, wout = (w.astype(BF16) for w in (ffn1_wi, ffn1_wo, ffn2_wi, ffn2_wo, w_out))

    for l in range(depth):
        last = l == depth - 1
        ml = mod[l]
        wparts, wgate = _split_w_in(w_in[l])
        gw = jnp.zeros((LANE, 2 * GLA_QK), F32)
        gw = gw.at[0:GLA_LOWRANK, 0:GLA_QK].set(gla_gw_f[l])
        gw = gw.at[GLA_LOWRANK:2 * GLA_LOWRANK, GLA_QK:].set(gla_gw_b[l])
        gb = jnp.concatenate([gla_gb_f[l], gla_gb_b[l]]).reshape(1, -1)
        gw = _gate_fold(wgate, gw).astype(BF16)
        wbd = jnp.zeros((D_GROUP, D_GROUP), F32)
        for gi in range(len(POOL_WINDOWS)):
            wbd = wbd.at[gi * POOL_CH:(gi + 1) * POOL_CH, gi * POOL_CH:(gi + 1) * POOL_CH].set(pool_w[l, gi])
        p = dict(pool_wbd=wbd.astype(BF16), pool_scale=pool_scale[l].reshape(1, -1),
                 hy_short_w=hy_short_w[l], hy_short_b=hy_short_b[l], hy_bias=hy_bias[l],
                 gla_ng=jnp.tile(gla_norm[l], GLA_HEADS).reshape(1, -1),
                 conv_dw_w=conv_dw_w[l], conv_dw_b=conv_dw_b[l], conv_ln_g=conv_ln_g[l], conv_ln_b=conv_ln_b[l])
        names = ["kq", "v", "r", "pool", "hy", "conv"]

        xs, *outs = _ffn_proj(xs, _mod_rows(ml, xrows, 0, 5), ffn1_norm[l], wi1, wo1, mix_norm[l],
                              [wparts[nm] for nm in names], gw, gb, layer=l, tm=tmx, tiles_per_seq=tps_x)
        ux = dict(zip(names + ["lf", "lb"], outs))
        cnames = ["kq", "v"] if last else names
        cs, *outs = _ffn_proj(cs, _mod_rows(ml, crows, 0, 5), ffn1_norm[l], wi1, wo1, mix_norm[l],
                              [wparts[nm] for nm in cnames], gw, gb, layer=l, tm=tmc, tiles_per_seq=1)
        uc = dict(zip(cnames + ["lf", "lb"], outs))

        ccpb = clen // CHUNK
        sfc, sbc, finf, finb = _gla_states(uc["kq"], uc["v"], uc["lf"], uc["lb"], zero_state, zero_state,
                                           bsz=bsz, seq=clen, cpb=ccpb)
        sfx, sbx, _, _ = _gla_states(ux["kq"], ux["v"], ux["lf"], ux["lb"], finf, finb, bsz=bsz, seq=seq,
                                     cpb=16)

        filt = _hy_filter(seq, hy_w1[l], hy_b1[l], hy_w2[l], hy_b2[l], hy_w3[l], hy_deltas[l], t2_major=True)
        mixed = _mix(ux, p, bsz=bsz, seq=seq, is_ctx=False, states=(sfx, sbx), filt=filt)
        xs = _out_ffn(xs, _mod_rows(ml, xrows, 5, 4), mixed, wout, ffn2_norm[l], wi2, wo2, layer=l, tm=tmx,
                      tiles_per_seq=tps_x, final_g=final_norm if last else None)
        if not last:
            filt_c = _hy_filter(clen, hy_w1[l], hy_b1[l], hy_w2[l], hy_b2[l], hy_w3[l], hy_deltas[l],
                                t2_major=False)
            mixed = _mix(uc, p, bsz=bsz, seq=clen, is_ctx=True, states=(sfc, sbc), filt=filt_c)
            cs = _out_ffn(cs, _mod_rows(ml, crows, 5, 4), mixed, wout, ffn2_norm[l], wi2, wo2, layer=l, tm=tmc,
                          tiles_per_seq=1)
    return xs.reshape(bsz, seq, d)
```

```python
import functools
import math

import numpy as np
import jax
import jax.numpy as jnp
from jax import lax
from jax.experimental import pallas as pl
from jax.experimental.pallas import tpu as pltpu

F32 = jnp.float32
BF16 = jnp.bfloat16
HI = lax.Precision.HIGHEST

D_MODEL = 1024
GRID_W = 64
D_GROUP = 256
D_FF = 2816
N_MOD = 9
EPS = 1e-6
POOL_WINDOWS = (2, 4, 8, 16)
POOL_CH = 64
HY_BANDS = 16
HY_EMB = 1 + 2 * HY_BANDS
HY_FFN = 64
HY_FILTER_CH = 4 * D_GROUP
GLA_HEADS = 4
GLA_DK = 32
GLA_DV = 64
GLA_QK = 128
GLA_LOWRANK = 16
GLA_TAU = 16.0
CHUNK = 64
CONV_WIDTH = 31
HY_SHORT = 3

COL_K = 0
COL_V = COL_K + GLA_QK
COL_GF = COL_V + D_GROUP
COL_GB = COL_GF + GLA_LOWRANK
COL_Q = COL_GB + GLA_LOWRANK
COL_R = COL_Q + GLA_QK
COL_POOL = COL_R + D_GROUP
COL_HY = COL_POOL + D_GROUP
COL_CONV = COL_HY + 3 * D_GROUP
P_IN = COL_CONV + 2 * D_GROUP

LANE = 128
DFT2 = 128
FFT_GROUP = 16
MIB = 1024 * 1024


def _cparams(sem, vmem_mib=None):
    kw = dict(dimension_semantics=sem)
    if vmem_mib is not None:
        kw["vmem_limit_bytes"] = vmem_mib * MIB
    return pltpu.CompilerParams(**kw)


def _silu(x):
    return x * jax.nn.sigmoid(x)


def _rms_mod(h, g, m):
    y = h * lax.rsqrt(jnp.mean(h * h, axis=-1, keepdims=True) + EPS) * g
    return y * (1.0 + m[1:2, :]) + m[0:1, :]


def _iota(shape, dim):
    return lax.broadcasted_iota(jnp.int32, shape, dim)


MOD_ROWS = 3


def _mod_body(at_ref, w_ref, b_ref, o_ref):
    d, tn = w_ref.shape[1], w_ref.shape[2]
    at = _silu(at_ref[...])
    cols = [jnp.broadcast_to(at[:, r:r + 1], (d, LANE)).reshape(d // SUBLANES, SUBLANES, LANE)
            for r in range(MOD_ROWS)]
    o_ref[0] = jnp.zeros((8, tn), F32) + b_ref[0]
    for j in range(tn // LANE):
        lsl = slice(j * LANE, (j + 1) * LANE)
        w3 = w_ref[0, :, lsl].reshape(d // SUBLANES, SUBLANES, LANE)
        for r in range(MOD_ROWS):
            part = jnp.sum(w3 * cols[r], axis=0)
            o_ref[0, r:r + 1, lsl] += jnp.sum(part, axis=0, keepdims=True)


def _modulation(cc, ada_w, ada_b):
    nl, d, nm = ada_w.shape
    tn = 2304
    return pl.pallas_call(
        _mod_body,
        out_shape=jax.ShapeDtypeStruct((nl, 8, nm), F32),
        grid=(nl, nm // tn),
        in_specs=[pl.BlockSpec((d, 8), lambda l, j: (0, 0)),
                  pl.BlockSpec((1, d, tn), lambda l, j: (l, 0, j)),
                  pl.BlockSpec((1, 1, tn), lambda l, j: (l, 0, j))],
        out_specs=pl.BlockSpec((1, 8, tn), lambda l, j: (l, 0, j)),
        compiler_params=_cparams(("parallel", "parallel"), 40),
        name="adaln_mod",
    )(cc.T, ada_w, ada_b.reshape(nl, 1, nm))


FF_CHUNK = 256
ROW_SUB = 256


def _swiglu(xn, wi_ref, wo_ref, hm_ref, rows):
    ff = wo_ref.shape[0]
    for c in range(0, ff, FF_CHUNK):
        a = jnp.dot(xn, wi_ref[:, c:c + FF_CHUNK], preferred_element_type=F32)
        g = jnp.dot(xn, wi_ref[:, ff + c:ff + c + FF_CHUNK], preferred_element_type=F32)
        hm_ref[rows, c:c + FF_CHUNK] = (_silu(g) * a).astype(BF16)
    return jnp.dot(hm_ref[rows, :], wo_ref[...], preferred_element_type=F32)


def _resident(shape):
    nd = len(shape)
    return pl.BlockSpec(shape, lambda i: (0,) * nd, pipeline_mode=pl.Buffered(1))


def _resident_layer(shape, layer):
    nd = len(shape)
    return pl.BlockSpec((None,) + tuple(shape), lambda i: (layer,) + (0,) * nd, pipeline_mode=pl.Buffered(1))


def _gate_fold_body(wg_ref, gw_ref, o_ref):
    o_ref[...] = jnp.dot(wg_ref[...], gw_ref[...], precision=HI, preferred_element_type=F32)


def _gate_fold(w_gate, gw):
    d = w_gate.shape[0]
    return pl.pallas_call(
        _gate_fold_body,
        out_shape=jax.ShapeDtypeStruct((d, gw.shape[1]), F32),
        grid=(1,),
        in_specs=[pl.BlockSpec(w_gate.shape, lambda i: (0, 0)), pl.BlockSpec(gw.shape, lambda i: (0, 0))],
        out_specs=pl.BlockSpec((d, gw.shape[1]), lambda i: (0, 0)),
        name="gate_fold",
    )(w_gate, gw)


def _chunk_prefix(x):
    pos = _iota(x.shape, 0) & (CHUNK - 1)
    shift = 1
    while shift < CHUNK:
        x = x + jnp.where(pos >= shift, pltpu.roll(x, shift, 0), 0.0)
        shift *= 2
    return x


def _ffn_proj_body(h_ref, m_ref, g1_ref, wi_ref, wo_ref, g2_ref, *rest, nparts):
    w_refs = rest[:nparts]
    gw_ref, gb_ref = rest[nparts:nparts + 2]
    x_ref = rest[nparts + 2]
    o_refs = rest[nparts + 3:2 * nparts + 5]
    hm_ref = rest[-1]
    m = m_ref[0]
    for r0 in range(0, h_ref.shape[0], ROW_SUB):
        rows = slice(r0, r0 + ROW_SUB)
        h = h_ref[rows, :]
        xn = _rms_mod(h, g1_ref[...], m[0:2]).astype(BF16)
        x1 = h + (0.5 * m[2:3, :]) * _swiglu(xn, wi_ref, wo_ref, hm_ref, rows)
        x_ref[rows, :] = x1
        xn2 = _rms_mod(x1, g2_ref[...], m[3:5]).astype(BF16)
        for w_ref, o_ref in zip(w_refs, o_refs[:nparts]):
            o_ref[rows, :] = jnp.dot(xn2, w_ref[...], preferred_element_type=F32).astype(o_ref.dtype)
        a = jnp.dot(xn2, gw_ref[...], preferred_element_type=F32) + gb_ref[...]
        ls = (jnp.minimum(a, 0.0) - jnp.log(1.0 + jnp.exp(-jnp.abs(a)))) * (1.0 / GLA_TAU)
        pre = _chunk_prefix(ls)
        o_refs[nparts][rows, :] = pre[:, :GLA_QK]
        pb = pre[:, GLA_QK:]
        tot = jnp.concatenate([jnp.broadcast_to(pb[c0 + CHUNK - 1:c0 + CHUNK, :], (CHUNK, GLA_QK))
                               for c0 in range(0, ROW_SUB, CHUNK)], axis=0)
        o_refs[nparts + 1][rows, :] = tot - pb + ls[:, GLA_QK:]


def _ffn_proj(h, m, g1, wi, wo, g2, w_parts, gw, gb, *, layer, tm, tiles_per_seq):
    n, d = h.shape
    ff = wo.shape[1]
    nparts = len(w_parts)
    widths = [w.shape[1] for w in w_parts]
    tok = lambda wd: pl.BlockSpec((tm, wd), lambda i: (i, 0))
    in_specs = [tok(d), pl.BlockSpec((1, 8, d), lambda i: (i // tiles_per_seq, 0, 0)), _resident((1, d)),
                _resident_layer((d, 2 * ff), layer), _resident_layer((ff, d), layer), _resident((1, d))]
    in_specs += [_resident((d, wd)) for wd in widths]
    in_specs += [_resident((d, 2 * GLA_QK)), _resident((1, 2 * GLA_QK))]
    out_shape = [jax.ShapeDtypeStruct((n, d), F32)]
    out_shape += [jax.ShapeDtypeStruct((n, wd), BF16) for wd in widths]
    out_shape += [jax.ShapeDtypeStruct((n, GLA_QK), F32)] * 2
    return pl.pallas_call(
        functools.partial(_ffn_proj_body, nparts=nparts),
        out_shape=out_shape,
        grid=(n // tm,),
        in_specs=in_specs,
        out_specs=[tok(d)] + [tok(wd) for wd in widths] + [tok(GLA_QK)] * 2,
        scratch_shapes=[pltpu.VMEM((tm, ff), BF16)],
        compiler_params=_cparams(("parallel",), 56),
        name="ffn1_in_proj",
    )(h, m, g1.reshape(1, d), wi, wo, g2.reshape(1, d), *w_parts, gw, gb)


def _out_ffn_body(x_ref, m_ref, p_ref, hy_ref, gl_ref, cv_ref, wout_ref, g_ref, wi_ref, wo_ref, fn_ref, o_ref,
                  hm_ref, *, final):
    m = m_ref[0]
    for r0 in range(0, x_ref.shape[0], ROW_SUB):
        rows = slice(r0, r0 + ROW_SUB)
        acc = jnp.dot(p_ref[rows, :].astype(BF16), wout_ref[0:D_GROUP, :], preferred_element_type=F32)
        acc += jnp.dot(hy_ref[rows, :].astype(BF16), wout_ref[D_GROUP:2 * D_GROUP, :], preferred_element_type=F32)
        acc += jnp.dot(gl_ref[rows, :].astype(BF16), wout_ref[2 * D_GROUP:3 * D_GROUP, :],
                       preferred_element_type=F32)
        acc += jnp.dot(cv_ref[rows, :].astype(BF16), wout_ref[3 * D_GROUP:, :], preferred_element_type=F32)
        x2 = x_ref[rows, :] + m[0:1, :] * acc
        xn = _rms_mod(x2, g_ref[...], m[1:3]).astype(BF16)
        out = x2 + (0.5 * m[3:4, :]) * _swiglu(xn, wi_ref, wo_ref, hm_ref, rows)
        if final:
            out = out * lax.rsqrt(jnp.mean(out * out, axis=-1, keepdims=True) + EPS) * fn_ref[...]
        o_ref[rows, :] = out


def _out_ffn(x, m, parts, w_out, g, wi, wo, *, layer, tm, tiles_per_seq, final_g=None):
    n, d = x.shape
    ff = wo.shape[1]
    final = final_g is not None
    fg = final_g if final else g
    tok = lambda wd: pl.BlockSpec((tm, wd), lambda i: (i, 0))
    return pl.pallas_call(
        functools.partial(_out_ffn_body, final=final),
        out_shape=jax.ShapeDtypeStruct((n, d), F32),
        grid=(n // tm,),
        in_specs=[tok(d), pl.BlockSpec((1, 8, d), lambda i: (i // tiles_per_seq, 0, 0)),
                  tok(D_GROUP), tok(D_GROUP), tok(D_GROUP), tok(D_GROUP), _resident_layer((d, d), layer),
                  _resident((1, d)), _resident_layer((d, 2 * ff), layer), _resident_layer((ff, d), layer),
                  _resident((1, d))],
        out_specs=tok(d),
        scratch_shapes=[pltpu.VMEM((tm, ff), BF16)],
        compiler_params=_cparams(("parallel",), 56),
        name="out_proj_ffn2",
    )(x, m, *parts, w_out, g.reshape(1, d), wi, wo, fg.reshape(1, d))


def _gla_state_body(kf, vf, bf, kb, vb, bb, s0f, s0b, sf_o, sb_o, ff_o, fb_o, stf, stb, *, cpb):
    i = pl.program_id(1)

    @pl.when(i == 0)
    def _():
        stf[...] = s0f[0]
        stb[...] = s0b[0]

    sshape = (GLA_HEADS * GLA_DV, GLA_QK)
    bmask = (_iota(sshape, 0) >> 6) == (_iota(sshape, 1) >> 5)
    tn_dims = (((0,), (0,)), ((), ()))

    def direction(k_ref, v_ref, b_ref, last, st, s_o, order):
        s = st[...]
        for ci in order:
            sl = slice(ci * CHUNK, (ci + 1) * CHUNK)
            b = b_ref[sl, :]
            tot = b[last:last + 1, :]
            kd = (k_ref[sl, :] * jnp.exp(tot - b)).astype(BF16)
            upd = lax.dot_general(v_ref[sl, :].astype(BF16), kd, tn_dims, preferred_element_type=F32)
            s_o[0, ci] = s.astype(BF16)
            s = s * jnp.exp(tot) + jnp.where(bmask, upd, 0.0)
        st[...] = s

    direction(kf, vf, bf, CHUNK - 1, stf, sf_o, range(cpb))
    direction(kb, vb, bb, 0, stb, sb_o, range(cpb - 1, -1, -1))
    ff_o[0] = stf[...]
    fb_o[0] = stb[...]


def _gla_states(k, v, lf, lb, s0f, s0b, *, bsz, seq, cpb):
    bt = cpb * CHUNK
    nb = seq // bt
    nc = seq // CHUNK
    srow = GLA_HEADS * GLA_DV

    def tf(b, i):
        return (b * nb + i, 0)

    def tb(b, i):
        return (b * nb + nb - 1 - i, 0)

    sblk = pl.BlockSpec((1, srow, GLA_QK), lambda b, i: (b, 0, 0))
    return pl.pallas_call(
        functools.partial(_gla_state_body, cpb=cpb),
        out_shape=[jax.ShapeDtypeStruct((bsz, nc, srow, GLA_QK), BF16),
                   jax.ShapeDtypeStruct((bsz, nc, srow, GLA_QK), BF16),
                   jax.ShapeDtypeStruct((bsz, srow, GLA_QK), F32),
                   jax.ShapeDtypeStruct((bsz, srow, GLA_QK), F32)],
        grid=(bsz, nb),
        in_specs=[pl.BlockSpec((bt, GLA_QK), tf), pl.BlockSpec((bt, D_GROUP), tf), pl.BlockSpec((bt, GLA_QK), tf),
                  pl.BlockSpec((bt, GLA_QK), tb), pl.BlockSpec((bt, D_GROUP), tb), pl.BlockSpec((bt, GLA_QK), tb),
                  sblk, sblk],
        out_specs=[pl.BlockSpec((1, cpb, srow, GLA_QK), lambda b, i: (b, i, 0, 0)),
                   pl.BlockSpec((1, cpb, srow, GLA_QK), lambda b, i: (b, nb - 1 - i, 0, 0)),
                   sblk, sblk],
        scratch_shapes=[pltpu.VMEM((srow, GLA_QK), F32), pltpu.VMEM((srow, GLA_QK), F32)],
        compiler_params=_cparams(("parallel", "arbitrary")),
        name="gla_states",
    )(k, v, lf, k, v, lb, s0f, s0b)


def _gla_read_body(k_ref, v_ref, q_ref, r_ref, bf_ref, bb_ref, sf_ref, sb_ref, ng_ref, o_ref, acc_ref, *, cpb):
    hrows = GLA_HEADS * CHUNK
    cpos = _iota((hrows, CHUNK), 0) & (CHUNK - 1)
    ccol = _iota((hrows, CHUNK), 1)
    lowm = cpos >= ccol
    upm = cpos <= ccol
    hq = (_iota((hrows, GLA_QK), 0) >> 6) == (_iota((hrows, GLA_QK), 1) >> 5)
    ho = (_iota((hrows, D_GROUP), 0) >> 6) == (_iota((hrows, D_GROUP), 1) >> 6)
    bavg = jnp.where((_iota((D_GROUP, D_GROUP), 0) >> 6) == (_iota((D_GROUP, D_GROUP), 1) >> 6),
                     1.0 / GLA_DV, 0.0).astype(BF16)
    nt_dims = (((1,), (1,)), ((), ()))

    bf_ = bf_ref[...]
    bb_ = bb_ref[...]
    qs = q_ref[...].astype(F32) * (GLA_DK ** -0.5)
    kk = k_ref[...]
    qef = (qs * jnp.exp(bf_)).astype(BF16)
    qeb = (qs * jnp.exp(bb_)).astype(BF16)
    kef = (kk * jnp.exp(-bf_)).astype(BF16)
    keb = (kk * jnp.exp(-bb_)).astype(BF16)
    vb16 = v_ref[...].astype(BF16)
    zero = jnp.zeros((), BF16)
    for ci in range(cpb):
        sl = slice(ci * CHUNK, (ci + 1) * CHUNK)
        qf4 = jnp.where(hq, jnp.concatenate([qef[sl]] * GLA_HEADS, axis=0), zero)
        qb4 = jnp.where(hq, jnp.concatenate([qeb[sl]] * GLA_HEADS, axis=0), zero)
        af = lax.dot_general(qf4, kef[sl], nt_dims, preferred_element_type=F32)
        ab = lax.dot_general(qb4, keb[sl], nt_dims, preferred_element_type=F32)
        att = (jnp.where(lowm, af, 0.0) + jnp.where(upm, ab, 0.0)).astype(BF16)
        oall = jnp.dot(att, vb16[sl], preferred_element_type=F32)
        om = jnp.where(ho, oall, 0.0)
        o = om[0:CHUNK] + om[CHUNK:2 * CHUNK] + om[2 * CHUNK:3 * CHUNK] + om[3 * CHUNK:4 * CHUNK]
        o = o + lax.dot_general(qef[sl], sf_ref[0, ci], nt_dims, preferred_element_type=F32)
        acc_ref[sl, :] = o + lax.dot_general(qeb[sl], sb_ref[0, ci], nt_dims, preferred_element_type=F32)
    o = acc_ref[...]
    ms = _chunk_sum_rhs(o * o, bavg)
    o_ref[...] = (o * lax.rsqrt(ms + EPS) * ng_ref[...] * _silu(r_ref[...].astype(F32))).astype(o_ref.dtype)


def _chunk_sum_rhs(x, mat):
    hi, lo = _split_bf16(x)
    return jnp.dot(hi, mat, preferred_element_type=F32) + jnp.dot(lo, mat, preferred_element_type=F32)


def _gla_read(k, v, q, r, lf, lb, sf, sb, ng, *, bsz, seq, cpb):
    bt = cpb * CHUNK
    nb = seq // bt
    srow = GLA_HEADS * GLA_DV
    n = bsz * seq

    def tk(i):
        return (i, 0)

    sspec = pl.BlockSpec((1, cpb, srow, GLA_QK), lambda i: (i // nb, i % nb, 0, 0))
    return pl.pallas_call(
        functools.partial(_gla_read_body, cpb=cpb),
        out_shape=jax.ShapeDtypeStruct((n, D_GROUP), BF16),
        scratch_shapes=[pltpu.VMEM((bt, D_GROUP), F32)],
        grid=(bsz * nb,),
        in_specs=[pl.BlockSpec((bt, GLA_QK), tk), pl.BlockSpec((bt, D_GROUP), tk),
                  pl.BlockSpec((bt, GLA_QK), lambda i: (i, 1)), pl.BlockSpec((bt, D_GROUP), tk),
                  pl.BlockSpec((bt, GLA_QK), tk), pl.BlockSpec((bt, GLA_QK), tk),
                  sspec, sspec, pl.BlockSpec((1, D_GROUP), lambda i: (0, 0))],
        out_specs=pl.BlockSpec((bt, D_GROUP), tk),
        compiler_params=_cparams(("parallel",)),
        name="gla_readout",
    )(k, v, q, r, lf, lb, sf, sb, ng)


def _box_matrix(n, w):
    pos = np.arange(n)
    lo = np.clip(pos - w // 2, 0, n)
    hi = np.clip(pos - w // 2 + w, 0, n)
    col = np.arange(n)[None, :]
    return ((col >= lo[:, None]) & (col < hi[:, None])).astype(np.float32)


def _lane_windows(shape):
    w = jnp.left_shift(2, _iota(shape, 1) >> 6)
    return w, w >> 1


def _box_count(pos, w, half, n):
    return jnp.minimum(pos - half + w, n) - jnp.maximum(pos - half, 0)


def _split_bf16(x):
    hi = x.astype(BF16)
    lo = (x - hi.astype(F32)).astype(BF16)
    return hi, lo


def _pool2d_body(cur_ref, prev_ref, next_ref, pc_ref, w_ref, sc_ref, o_ref, ycol, *, tiles, rows):
    i = pl.program_id(1)
    tm = cur_ref.shape[0]
    hb = prev_ref.shape[0]
    sub = 2 * GRID_W
    pflag = jnp.where(i > 0, 1.0, 0.0)
    nflag = jnp.where(i < tiles - 1, 1.0, 0.0)

    def colpool(x):
        halves = []
        for half in range(2):
            lsl = slice(half * LANE, (half + 1) * LANE)
            ys = [jnp.dot(pc_ref[wi], x[:, lsl], preferred_element_type=F32) for wi in (2 * half, 2 * half + 1)]
            lane = _iota((sub, LANE), 1)
            halves.append(jnp.where(lane < POOL_CH, ys[0], ys[1]))
        return jnp.concatenate(halves, axis=1)

    for s in range(hb // sub):
        ycol[s * sub:(s + 1) * sub, :] = colpool(prev_ref[s * sub:(s + 1) * sub, :]) * pflag
    for s in range(tm // sub):
        ycol[hb + s * sub:hb + (s + 1) * sub, :] = colpool(cur_ref[s * sub:(s + 1) * sub, :])
    for s in range(hb // sub):
        ycol[hb + tm + s * sub:hb + tm + (s + 1) * sub, :] = colpool(next_ref[s * sub:(s + 1) * sub, :]) * nflag

    rc = 256
    wl, half = _lane_windows((rc, D_GROUP))
    narrow = _iota((rc, LANE), 1) < POOL_CH
    for r0 in range(0, tm, rc):
        def band(lo, hi, lsl):
            base = hb + r0
            acc = ycol[base + GRID_W * lo:base + GRID_W * lo + rc, lsl]
            for dd in range(lo + 1, hi):
                acc = acc + ycol[base + GRID_W * dd:base + GRID_W * dd + rc, lsl]
            return acc

        left, right = slice(0, LANE), slice(LANE, 2 * LANE)
        z2 = band(-1, 1, left)
        z4 = z2 + band(-2, -1, left) + band(1, 2, left)
        z8 = band(-4, 4, right)
        z16 = z8 + band(-8, -4, right) + band(4, 8, right)
        z = jnp.concatenate([jnp.where(narrow, z2, z4), jnp.where(narrow, z8, z16)], axis=1)
        tok = _iota((rc, D_GROUP), 0) + (i * tm + r0)
        rcnt = _box_count(tok >> 6, wl, half, rows)
        ccnt = _box_count(tok & (GRID_W - 1), wl, half, GRID_W)
        pooled = z / (rcnt * ccnt).astype(F32)
        dlt = (pooled - cur_ref[r0:r0 + rc, :]).astype(BF16)
        o_ref[r0:r0 + rc, :] = (jnp.dot(dlt, w_ref[...], preferred_element_type=F32) * sc_ref[...]).astype(o_ref.dtype)


def _pool2d(u, wbd, scale, *, bsz, seq):
    tm = 1024
    hb = 512
    tiles = seq // tm
    r = tm // hb
    nhb = seq // hb
    pc = np.stack([np.kron(np.eye(2, dtype=np.float32), _box_matrix(GRID_W, w)) for w in POOL_WINDOWS])
    return pl.pallas_call(
        functools.partial(_pool2d_body, tiles=tiles, rows=seq // GRID_W),
        out_shape=jax.ShapeDtypeStruct((bsz * seq, D_GROUP), BF16),
        grid=(bsz, tiles),
        in_specs=[pl.BlockSpec((tm, D_GROUP), lambda b, i: (b * tiles + i, 0)),
                  pl.BlockSpec((hb, D_GROUP), lambda b, i: (b * nhb + jnp.maximum(i * r - 1, 0), 0)),
                  pl.BlockSpec((hb, D_GROUP), lambda b, i: (b * nhb + jnp.minimum(i * r + r, nhb - 1), 0)),
                  pl.BlockSpec((4, 2 * GRID_W, 2 * GRID_W), lambda b, i: (0, 0, 0)),
                  pl.BlockSpec((D_GROUP, D_GROUP), lambda b, i: (0, 0)),
                  pl.BlockSpec((1, D_GROUP), lambda b, i: (0, 0))],
        out_specs=pl.BlockSpec((tm, D_GROUP), lambda b, i: (b * tiles + i, 0)),
        scratch_shapes=[pltpu.VMEM((tm + 2 * hb, D_GROUP), F32)],
        compiler_params=_cparams(("parallel", "parallel")),
        name="pool2d",
    )(u, u, u, jnp.asarray(pc, BF16), wbd, scale)


def _pool1d_body(x_ref, p_ref, w_ref, sc_ref, o_ref):
    x = x_ref[...]
    n = x.shape[0]
    ys = [jnp.dot(p_ref[wi], x, preferred_element_type=F32) for wi in range(4)]
    wl, half = _lane_windows((n, D_GROUP))
    z = jnp.where(wl == 2, ys[0], jnp.where(wl == 4, ys[1], jnp.where(wl == 8, ys[2], ys[3])))
    cnt = _box_count(_iota((n, D_GROUP), 0), wl, half, n)
    dlt = (z / cnt.astype(F32) - x).astype(BF16)
    o_ref[...] = (jnp.dot(dlt, w_ref[...], preferred_element_type=F32) * sc_ref[...]).astype(o_ref.dtype)


def _pool1d(u, wbd, scale, *, bsz, seq):
    pm = np.stack([_box_matrix(seq, w) for w in POOL_WINDOWS])
    return pl.pallas_call(
        _pool1d_body,
        out_shape=jax.ShapeDtypeStruct((bsz * seq, D_GROUP), BF16),
        grid=(bsz,),
        in_specs=[pl.BlockSpec((seq, D_GROUP), lambda b: (b, 0)),
                  pl.BlockSpec((4, seq, seq), lambda b: (0, 0, 0)),
                  pl.BlockSpec((D_GROUP, D_GROUP), lambda b: (0, 0)),
                  pl.BlockSpec((1, D_GROUP), lambda b: (0, 0))],
        out_specs=pl.BlockSpec((seq, D_GROUP), lambda b: (b, 0)),
        compiler_params=_cparams(("parallel",)),
        name="pool1d",
    )(u, jnp.asarray(pm, BF16), wbd, scale)


def _fill_halo(buf, cur, prev, nxt, i, tps, hb, tm, pre):
    first = (i % tps) == 0
    last = (i % tps) == tps - 1
    buf[0:hb, :] = jnp.where(first, 0.0, pre(prev[...]))
    buf[hb:hb + tm, :] = pre(cur[...])
    buf[hb + tm:hb + tm + hb, :] = jnp.where(last, 0.0, pre(nxt[...]))


SUBLANES = 8


def _tap_phases(taps, off):
    return sorted({(off + j) % SUBLANES for j in range(taps)} - {0})


def _fill_phases(sh, buf, phases):
    rows = sh.shape[1]
    for slot, s in enumerate(phases):
        sh[slot, :, :] = buf[s:s + rows, :]


def _dwconv(buf, sh, phases, w_ref, r0, rc, taps, off):
    acc = None
    for j in range(taps):
        s, q = (off + j) % SUBLANES, (off + j) // SUBLANES
        lo = r0 + SUBLANES * q
        src = buf[lo:lo + rc, :] if s == 0 else sh[phases.index(s), lo:lo + rc, :]
        term = src * w_ref[j:j + 1, :]
        acc = term if acc is None else acc + term
    return acc


def _conf_body(cur, prev, nxt, w_ref, b_ref, lg_ref, lb_ref, o_ref, buf, sh, *, tps):
    i = pl.program_id(0)
    tm = cur.shape[0]
    hb = prev.shape[0]

    def glu(u):
        u = u.astype(F32)
        return u[:, :D_GROUP] * jax.nn.sigmoid(u[:, D_GROUP:])

    _fill_halo(buf, cur, prev, nxt, i, tps, hb, tm, glu)
    rc = 128
    off = hb - (CONV_WIDTH - 1) // 2
    phases = _tap_phases(CONV_WIDTH, off)
    _fill_phases(sh, buf, phases)
    for r0 in range(0, tm, rc):
        h = _dwconv(buf, sh, phases, w_ref, r0, rc, CONV_WIDTH, off) + b_ref[...]
        mu = jnp.mean(h, axis=-1, keepdims=True)
        hc = h - mu
        var = jnp.mean(hc * hc, axis=-1, keepdims=True)
        o_ref[r0:r0 + rc, :] = _silu(hc * lax.rsqrt(var + EPS) * lg_ref[...] + lb_ref[...]).astype(o_ref.dtype)


def _halo_specs(tm, hb, width, nrows):
    r = tm // hb
    nhb = nrows // hb
    return [pl.BlockSpec((tm, width), lambda i: (i, 0)),
            pl.BlockSpec((hb, width), lambda i: (jnp.maximum(i * r - 1, 0), 0)),
            pl.BlockSpec((hb, width), lambda i: (jnp.minimum(i * r + r, nhb - 1), 0))]


def _conformer(u, w, b, lg, lb, *, seq, tm):
    n = u.shape[0]
    hb = 16
    nph = len(_tap_phases(CONV_WIDTH, hb - (CONV_WIDTH - 1) // 2))
    vec = pl.BlockSpec((1, D_GROUP), lambda i: (0, 0))
    return pl.pallas_call(
        functools.partial(_conf_body, tps=seq // tm),
        out_shape=jax.ShapeDtypeStruct((n, D_GROUP), BF16),
        grid=(n // tm,),
        in_specs=_halo_specs(tm, hb, 2 * D_GROUP, n) + [pl.BlockSpec((CONV_WIDTH, D_GROUP), lambda i: (0, 0)),
                                                        vec, vec, vec],
        out_specs=pl.BlockSpec((tm, D_GROUP), lambda i: (i, 0)),
        scratch_shapes=[pltpu.VMEM((tm + 2 * hb, D_GROUP), F32),
                        pltpu.VMEM((nph, tm + 2 * hb - SUBLANES, D_GROUP), F32)],
        compiler_params=_cparams(("parallel",)),
        name="conformer_conv",
    )(u, u, u, w, b.reshape(1, -1), lg.reshape(1, -1), lb.reshape(1, -1))


def _short_body(cur, prev, nxt, w_ref, b_ref, v_ref, x1_ref, x2_ref, buf, sh, *, tps):
    i = pl.program_id(0)
    tm = cur.shape[0]
    hb = prev.shape[0]
    _fill_halo(buf, cur, prev, nxt, i, tps, hb, tm, lambda u: u.astype(F32))
    rc = 128
    off = hb - (HY_SHORT - 1) // 2
    phases = _tap_phases(HY_SHORT, off)
    _fill_phases(sh, buf, phases)
    for r0 in range(0, tm, rc):
        uc = _dwconv(buf, sh, phases, w_ref, r0, rc, HY_SHORT, off) + b_ref[...]
        v_ref[r0:r0 + rc, :] = uc[:, :D_GROUP]
        x1_ref[r0:r0 + rc, :] = uc[:, D_GROUP:2 * D_GROUP].astype(x1_ref.dtype)
        x2_ref[r0:r0 + rc, :] = uc[:, 2 * D_GROUP:].astype(x2_ref.dtype)


def _hy_short(u, w, b, *, seq, tm):
    n = u.shape[0]
    hb = 16
    wd = 3 * D_GROUP
    nph = len(_tap_phases(HY_SHORT, hb - (HY_SHORT - 1) // 2))
    ospec = pl.BlockSpec((tm, D_GROUP), lambda i: (i, 0))
    return pl.pallas_call(
        functools.partial(_short_body, tps=seq // tm),
        out_shape=[jax.ShapeDtypeStruct((n, D_GROUP), dt) for dt in (F32, BF16, BF16)],
        grid=(n // tm,),
        in_specs=_halo_specs(tm, hb, wd, n) + [pl.BlockSpec((HY_SHORT, wd), lambda i: (0, 0)),
                                               pl.BlockSpec((1, wd), lambda i: (0, 0))],
        out_specs=[ospec, ospec, ospec],
        scratch_shapes=[pltpu.VMEM((tm + 2 * hb, wd), F32),
                        pltpu.VMEM((nph, tm + 2 * hb - SUBLANES, wd), F32)],
        compiler_params=_cparams(("parallel",)),
        name="hyena_short_conv",
    )(u, u, u, w, b.reshape(1, -1))


def _filter_features(n):
    i = np.arange(n, dtype=np.float64)
    t = np.linspace(0.0, 1.0, n, dtype=np.float32).astype(np.float64)
    wpos = ((2.0 * math.pi / n) * np.arange(n, dtype=np.float32)).astype(np.float32)
    bands = np.linspace(1e-4, HY_BANDS - 1, HY_BANDS, dtype=np.float32)
    arg = (bands[None, :] * wpos[:, None]).astype(np.float32).astype(np.float64)
    z = np.zeros((n, 64), np.float32)
    z[:, 0] = t
    z[:, 1:1 + HY_BANDS] = np.cos(arg)
    z[:, 1 + HY_BANDS:HY_EMB] = -np.sin(arg)
    del i
    return z


def _filter_body(z_ref, w1_ref, b1_ref, w2_ref, b2_ref, w3a_ref, w3b_ref, d_ref, hf_ref, hb_ref, s_ref):
    i = pl.program_id(0)
    half = z_ref.shape[0]
    z = z_ref[...]
    h = jnp.sin(jnp.dot(z, w1_ref[...], precision=HI, preferred_element_type=F32) + b1_ref[...])
    h = jnp.sin(jnp.dot(h, w2_ref[...], precision=HI, preferred_element_type=F32) + b2_ref[...])
    absd = jnp.abs(d_ref[...])
    h = h.astype(BF16)
    tot = None
    for part, (w3_ref, tcol) in enumerate(((w3a_ref, 0), (w3b_ref, 64))):
        hp = jnp.dot(h, w3_ref[...], preferred_element_type=F32)
        hp = hp * jnp.exp(-z[:, tcol:tcol + 1] * absd)
        rows = slice(part * half, (part + 1) * half)
        for o in range(2):
            c0 = 2 * D_GROUP * o
            hf_ref[rows, o * D_GROUP:(o + 1) * D_GROUP] = hp[:, c0:c0 + D_GROUP]
            hb_ref[rows, o * D_GROUP:(o + 1) * D_GROUP] = hp[:, c0 + D_GROUP:c0 + 2 * D_GROUP]
        part_sum = jnp.sum(jnp.abs(hp), axis=0, keepdims=True)
        tot = part_sum if tot is None else tot + part_sum

    @pl.when(i == 0)
    def _():
        s_ref[...] = jnp.zeros_like(s_ref)

    s_ref[...] += tot


def _hy_filter(n, w1, b1, w2, b2, w3, deltas, *, t2_major):
    tm = min(n, 512)
    half = tm // 2
    z = _filter_features(n)
    if t2_major:
        z = z.reshape(n // DFT2, DFT2, 64).transpose(1, 0, 2).reshape(n, 64)
    zt = z.reshape(n // tm, 2, half, 64)
    z2 = jnp.asarray(np.concatenate([zt[:, 0], zt[:, 1]], axis=-1).reshape(n // 2, 2 * 64))
    w1p = jnp.zeros((64, HY_FFN), F32).at[:HY_EMB].set(w1)
    zero = jnp.zeros((64, HY_FFN), F32)
    bd = lambda w: jnp.concatenate([jnp.concatenate([w, zero], axis=1), jnp.concatenate([zero, w], axis=1)], axis=0)
    w3 = w3.astype(BF16)
    zero3 = jnp.zeros_like(w3)
    full = lambda shape: pl.BlockSpec(shape, lambda i: (0, 0))
    tok = pl.BlockSpec((tm, 2 * D_GROUP), lambda i: (i, 0))
    return pl.pallas_call(
        _filter_body,
        out_shape=[jax.ShapeDtypeStruct((n, 2 * D_GROUP), F32), jax.ShapeDtypeStruct((n, 2 * D_GROUP), F32),
                   jax.ShapeDtypeStruct((1, HY_FILTER_CH), F32)],
        grid=(n // tm,),
        in_specs=[pl.BlockSpec((half, 2 * 64), lambda i: (i, 0)), full((2 * 64, 2 * HY_FFN)), full((1, 2 * HY_FFN)),
                  full((2 * HY_FFN, 2 * HY_FFN)), full((1, 2 * HY_FFN)), full((2 * HY_FFN, HY_FILTER_CH)),
                  full((2 * HY_FFN, HY_FILTER_CH)), full((1, HY_FILTER_CH))],
        out_specs=[tok, tok, full((1, HY_FILTER_CH))],
        compiler_params=_cparams(("arbitrary",)),
        name="hyena_filter_mlp",
    )(z2, bd(w1p), jnp.tile(b1.reshape(1, -1), (1, 2)), bd(w2), jnp.tile(b2.reshape(1, -1), (1, 2)),
      jnp.concatenate([w3, zero3], axis=0), jnp.concatenate([zero3, w3], axis=0), deltas.reshape(1, -1))


@functools.lru_cache(maxsize=None)
def _dft_tables(n1):
    nn = n1 * DFT2
    j = np.arange(DFT2)
    th = 2.0 * np.pi * ((np.outer(j, j)) % DFT2) / DFT2
    fr, fi = np.cos(th), -np.sin(th)
    m_fwd = np.block([[fr, -fi], [fi, fr]])
    m_inv = np.block([[fr, fi], [-fi, fr]])
    hh = n1 // 2
    k1 = np.arange(n1)[None, :, None]
    t1 = np.arange(hh)[None, None, :]
    t2 = np.arange(DFT2)[:, None, None]
    ph = 2.0 * np.pi * ((k1 * (DFT2 * t1 + t2)) % nn) / nn
    ar, ai = np.cos(ph), -np.sin(ph)
    a_fwd = np.concatenate([np.concatenate([ar, -ai], axis=2), np.concatenate([ai, ar], axis=2)], axis=1)
    a_real = np.concatenate([ar, ai], axis=1)
    t1r = np.where(t2 >= 1, n1 - 1 - t1, (n1 - t1) % n1)
    phr = 2.0 * np.pi * ((k1 * (DFT2 * t1r + t2)) % nn) / nn
    a_rev = np.concatenate([np.cos(phr), -np.sin(phr)], axis=1)
    a_rev[0, :, 0] = 0.0
    pht = np.transpose(ph, (0, 2, 1))
    cr, ci = np.cos(pht) / nn, np.sin(pht) / nn
    a_inv = np.concatenate([np.concatenate([cr, -ci], axis=2), np.concatenate([ci, cr], axis=2)], axis=1)
    f32 = lambda a: np.ascontiguousarray(a, dtype=np.float32)
    return dict(m_fwd=f32(m_fwd), m_inv=f32(m_inv), a_fwd=f32(a_fwd), a_real=f32(a_real), a_rev=f32(a_rev),
                a_inv=f32(a_inv))


def _fft_a_body(x_ref, m_ref, o_ref, *, tj):
    n1 = o_ref.shape[2]
    for e in range(tj):
        x = jnp.concatenate([x_ref[0, :, e, :], x_ref[1, :, e, :]], axis=0).astype(BF16)
        a = jnp.dot(m_ref[e], x, preferred_element_type=F32)
        o_ref[0, e] = a[:n1]
        o_ref[1, e] = a[n1:]


def _fft_a(x4, m, *, n1, tj):
    c = x4.shape[3]
    return pl.pallas_call(
        functools.partial(_fft_a_body, tj=tj),
        out_shape=jax.ShapeDtypeStruct((2, DFT2, n1, c), F32),
        grid=(DFT2 // tj,),
        in_specs=[pl.BlockSpec((2, n1 // 2, tj, c), lambda j: (0, 0, j, 0)),
                  pl.BlockSpec((tj, 2 * n1, n1), lambda j: (j, 0, 0))],
        out_specs=pl.BlockSpec((2, tj, n1, c), lambda j: (0, j, 0, 0)),
        compiler_params=_cparams(("parallel",), 48),
        name="fft_stage_a",
    )(x4, m)


def _fft_b_body(a_ref, mf_ref, mi_ref, g_ref, o_ref, *, kg):
    for kk in range(kg):
        x = jnp.concatenate([a_ref[0, :, kk, :], a_ref[1, :, kk, :]], axis=0).astype(BF16)
        xf = jnp.dot(mf_ref[...], x, preferred_element_type=F32)
        xr, xi = xf[:DFT2], xf[DFT2:]
        gr, gi = g_ref[0, kk].astype(F32), g_ref[1, kk].astype(F32)
        y = jnp.concatenate([xr * gr - xi * gi, xr * gi + xi * gr], axis=0).astype(BF16)
        bf = jnp.dot(mi_ref[...], y, preferred_element_type=F32)
        o_ref[0, kk] = bf[:DFT2]
        o_ref[1, kk] = bf[DFT2:]


def _fft_b(a, g, order, tabs, *, n1):
    c = a.shape[3]
    kg = FFT_GROUP
    blk = pl.BlockSpec((2, kg, DFT2, c), lambda k: (0, k, 0, 0))
    mat = pl.BlockSpec((2 * DFT2, 2 * DFT2), lambda k: (0, 0))
    return pl.pallas_call(
        functools.partial(_fft_b_body, kg=kg),
        out_shape=jax.ShapeDtypeStruct((2, n1, DFT2, c), F32),
        grid=(n1 // kg,),
        in_specs=[pl.BlockSpec((2, DFT2, kg, c), lambda k: (0, 0, k, 0)), mat, mat,
                  pl.BlockSpec((2, kg, DFT2, c), lambda k: (0, k, 0, order))],
        out_specs=blk,
        compiler_params=_cparams(("parallel",), 48),
        name="fft_stage_b",
    )(a, tabs["m_fwd"], tabs["m_inv"], g)


def _fft_bf_body(a_ref, mf_ref, o_ref, *, kg):
    for kk in range(kg):
        x = jnp.concatenate([a_ref[0, :, kk, :], a_ref[1, :, kk, :]], axis=0).astype(BF16)
        xf = jnp.dot(mf_ref[...], x, preferred_element_type=F32)
        o_ref[0, kk] = xf[:DFT2].astype(BF16)
        o_ref[1, kk] = xf[DFT2:].astype(BF16)


def _fft_b_forward(a, tabs, *, n1):
    c = a.shape[3]
    kg = FFT_GROUP
    return pl.pallas_call(
        functools.partial(_fft_bf_body, kg=kg),
        out_shape=jax.ShapeDtypeStruct((2, n1, DFT2, c), BF16),
        grid=(n1 // kg,),
        in_specs=[pl.BlockSpec((2, DFT2, kg, c), lambda k: (0, 0, k, 0)),
                  pl.BlockSpec((2 * DFT2, 2 * DFT2), lambda k: (0, 0))],
        out_specs=pl.BlockSpec((2, kg, DFT2, c), lambda k: (0, k, 0, 0)),
        compiler_params=_cparams(("parallel",), 48),
        name="fft_stage_b_filter",
    )(a, tabs["m_fwd"])


def _fft_ai_body(b_ref, m_ref, v_ref, x_ref, bias_ref, o_ref, *, tj):
    h = o_ref.shape[1]
    for e in range(tj):
        b = jnp.concatenate([b_ref[0, :, e, :], b_ref[1, :, e, :]], axis=0).astype(BF16)
        y = jnp.dot(m_ref[e], b, preferred_element_type=F32)
        o_ref[0, :, e, :] = y[:h]
        o_ref[1, :, e, :] = y[h:]
    o_ref[...] = x_ref[...] * (o_ref[...] + v_ref[...] * bias_ref[...])


def _fft_a_inv(b, m, v4, xm4, bias, *, n1, tj):
    c = b.shape[3]
    half = pl.BlockSpec((2, n1 // 2, tj, c), lambda j: (0, 0, j, 0))
    return pl.pallas_call(
        functools.partial(_fft_ai_body, tj=tj),
        out_shape=jax.ShapeDtypeStruct((2, n1 // 2, DFT2, c), F32),
        grid=(DFT2 // tj,),
        in_specs=[pl.BlockSpec((2, n1, tj, c), lambda j: (0, 0, j, 0)),
                  pl.BlockSpec((tj, n1, 2 * n1), lambda j: (j, 0, 0)),
                  half, half, pl.BlockSpec((1, c), lambda j: (0, 0))],
        out_specs=half,
        compiler_params=_cparams(("parallel",), 48),
        name="fft_stage_a_inv",
    )(b, m, v4, xm4, bias)


def _filt_a_body(hj_ref, hz_ref, hr_ref, s_ref, mf_ref, mr_ref, o_ref, *, tj):
    n1 = o_ref.shape[2]
    s = s_ref[...]
    for e in range(tj):
        src = hz_ref if e == 0 else hr_ref
        me = 0 if e == 0 else tj - e
        for o in range(2):
            c0 = 2 * D_GROUP * o
            inv = 1.0 / (s[:, c0:c0 + D_GROUP] + s[:, c0 + D_GROUP:c0 + 2 * D_GROUP] + EPS)
            lsl = slice(o * D_GROUP, (o + 1) * D_GROUP)
            hf = hj_ref[e, :, lsl].astype(BF16)
            hb = src[me, :, lsl].astype(BF16)
            a = (jnp.dot(mf_ref[e], hf, preferred_element_type=F32)
                 + jnp.dot(mr_ref[e], hb, preferred_element_type=F32)) * inv
            o_ref[0, e, :, lsl] = a[:n1]
            o_ref[1, e, :, lsl] = a[n1:]


def _filt_a(hf, hb, colsum, tabs, *, n1):
    tj = FFT_GROUP
    nj = DFT2 // tj
    wd = 2 * D_GROUP
    shp = (DFT2, n1 // 2, wd)
    blk = lambda fn: pl.BlockSpec((tj, n1 // 2, wd), fn)
    mat = pl.BlockSpec((tj, 2 * n1, n1 // 2), lambda j: (j, 0, 0))
    return pl.pallas_call(
        functools.partial(_filt_a_body, tj=tj),
        out_shape=jax.ShapeDtypeStruct((2, DFT2, n1, wd), F32),
        grid=(nj,),
        in_specs=[blk(lambda j: (j, 0, 0)), blk(lambda j: ((nj - j) % nj, 0, 0)), blk(lambda j: (nj - 1 - j, 0, 0)),
                  pl.BlockSpec((1, HY_FILTER_CH), lambda j: (0, 0)), mat, mat],
        out_specs=pl.BlockSpec((2, tj, n1, wd), lambda j: (0, j, 0, 0)),
        compiler_params=_cparams(("parallel",), 48),
        name="filter_stage_a",
    )(hf.reshape(shp), hb.reshape(shp), hb.reshape(shp), colsum, tabs["a_real"], tabs["a_rev"])


def _hyena_long(v, x1, x2, hf, hb, colsum, bias, *, seq):
    n1 = 2 * seq // DFT2
    tabs = {k: jnp.asarray(a).astype(BF16) for k, a in _dft_tables(n1).items()}
    tj = FFT_GROUP
    fa = _filt_a(hf, hb, colsum, tabs, n1=n1)
    g = _fft_b_forward(fa, tabs, n1=n1)
    shp = (2, n1 // 2, DFT2, D_GROUP)
    z = v.reshape(shp)
    for order, xm in ((0, x1.reshape(shp)), (1, x2.reshape(shp))):
        a = _fft_a(z, tabs["a_fwd"], n1=n1, tj=tj)
        b = _fft_b(a, g, order, tabs, n1=n1)
        z = _fft_a_inv(b, tabs["a_inv"], z, xm, bias[order].reshape(1, D_GROUP), n1=n1, tj=tj)
    return z.reshape(2 * seq, D_GROUP)


def _hyena_ctx_body(v_ref, x1_ref, x2_ref, hf_ref, hb_ref, s_ref, bias_ref, cm_ref, sm_ref, ct_ref, st_ref, o_ref):
    n = v_ref.shape[1]
    cm, sm, ct, st = cm_ref[...], sm_ref[...], ct_ref[...], st_ref[...]
    s = s_ref[...]
    row0 = _iota((n, D_GROUP), 0) == 0
    dot = lambda a, b: jnp.dot(a, b.astype(BF16), preferred_element_type=F32)
    zr, zi = v_ref[0], v_ref[1]
    for order, xm in ((0, x1_ref), (1, x2_ref)):
        c0 = 2 * D_GROUP * order
        inv = 1.0 / (s[:, c0:c0 + D_GROUP] + s[:, c0 + D_GROUP:c0 + 2 * D_GROUP] + EPS)
        lsl = slice(order * D_GROUP, (order + 1) * D_GROUP)
        hf = hf_ref[:, lsl] * inv
        hb = jnp.where(row0, 0.0, hb_ref[:, lsl] * inv)
        gr = dot(cm, hf + hb)
        gi = dot(sm, hb - hf)
        xr = dot(cm, zr) + dot(sm, zi)
        xi = dot(cm, zi) - dot(sm, zr)
        yr = xr * gr - xi * gi
        yi = xr * gi + xi * gr
        cr = dot(ct, yr) - dot(st, yi)
        ci = dot(ct, yi) + dot(st, yr)
        bias = bias_ref[order:order + 1, :]
        zr = xm[0] * (cr + zr * bias)
        zi = xm[1] * (ci + zi * bias)
    o_ref[0] = zr
    o_ref[1] = zi


def _hyena_ctx(v, x1, x2, hf, hb, colsum, bias, *, seq):
    nn = 2 * seq
    k = np.arange(nn)
    t = np.arange(seq)
    th = 2.0 * np.pi * (np.outer(k, t) % nn) / nn
    cm, sm = np.cos(th), np.sin(th)
    consts = [jnp.asarray(a, F32).astype(BF16) for a in (cm, sm, cm.T / nn, sm.T / nn)]
    shp = (2, seq, D_GROUP)
    full3 = pl.BlockSpec(shp, lambda i: (0, 0, 0))
    f2 = lambda a: pl.BlockSpec(a.shape, lambda i: (0, 0))
    args = [hf, hb, colsum, bias] + consts
    out = pl.pallas_call(
        _hyena_ctx_body,
        out_shape=jax.ShapeDtypeStruct(shp, F32),
        grid=(1,),
        in_specs=[full3, full3, full3] + [f2(a) for a in args],
        out_specs=full3,
        compiler_params=_cparams(("arbitrary",)),
        name="hyena_ctx",
    )(v.reshape(shp), x1.reshape(shp), x2.reshape(shp), *args)
    return out.reshape(2 * seq, D_GROUP)


def _mod_rows(mod_l, rows, first, count):
    m = mod_l[rows, first:first + count, :]
    return jnp.pad(m, ((0, 0), (0, 8 - count), (0, 0)))


def _split_w_in(w_in):
    wb = w_in.astype(BF16)
    parts = dict(kq=jnp.concatenate([wb[:, COL_K:COL_V], wb[:, COL_Q:COL_R]], axis=1), v=wb[:, COL_V:COL_GF],
                 r=wb[:, COL_R:COL_POOL], pool=wb[:, COL_POOL:COL_HY], hy=wb[:, COL_HY:COL_CONV],
                 conv=wb[:, COL_CONV:P_IN])
    gate = jnp.pad(w_in[:, COL_GF:COL_Q], ((0, 0), (0, LANE - 2 * GLA_LOWRANK)))
    return parts, gate


def _mix(u, p, *, bsz, seq, is_ctx, states, filt):
    sf, sb = states
    cpb = min(8, seq // CHUNK)
    if is_ctx:
        pool = _pool1d(u["pool"], p["pool_wbd"], p["pool_scale"], bsz=bsz, seq=seq)
    else:
        pool = _pool2d(u["pool"], p["pool_wbd"], p["pool_scale"], bsz=bsz, seq=seq)
    tmc = min(seq, 1024)
    v, x1, x2 = _hy_short(u["hy"], p["hy_short_w"], p["hy_short_b"], seq=seq, tm=tmc)
    if is_ctx:
        hy = _hyena_ctx(v, x1, x2, *filt, p["hy_bias"], seq=seq)
    else:
        hy = _hyena_long(v, x1, x2, *filt, p["hy_bias"], seq=seq)
    gla = _gla_read(u["kq"], u["v"], u["kq"], u["r"], u["lf"], u["lb"], sf, sb, p["gla_ng"],
                    bsz=bsz, seq=seq, cpb=cpb)
    conv = _conformer(u["conv"], p["conv_dw_w"], p["conv_dw_b"], p["conv_ln_g"], p["conv_ln_b"], seq=seq, tm=tmc)
    return [pool, hy, gla, conv]


def kernel(x, c, ctx, c_ctx, ada_w, ada_b, ffn1_norm, ffn1_wi, ffn1_wo, mix_norm, w_in, w_out, pool_w, pool_scale, hy_short_w, hy_short_b, hy_w1, hy_b1, hy_w2, hy_b2, hy_w3, hy_deltas, hy_bias, gla_gw_f, gla_gb_f, gla_gw_b, gla_gb_b, gla_norm, conv_dw_w, conv_dw_b, conv_ln_g, conv_ln_b, ffn2_norm, ffn2_wi, ffn2_wo, final_norm):
    bsz, seq, d = x.shape
    clen = ctx.shape[1]
    depth = ada_w.shape[0]
    assert bsz == 2, "the Hyena transform packs exactly two batch rows into one complex signal"
    xs = x.reshape(bsz * seq, d)
    cs = ctx.reshape(bsz * clen, d)
    cc = jnp.zeros((8, d), F32).at[0:bsz].set(c).at[bsz].set(c_ctx)
    mod = _modulation(cc, ada_w, ada_b).reshape(depth, 8, N_MOD, d)
    xrows = np.arange(bsz)
    crows = np.full((1,), bsz)
    tmx = 512
    tps_x = seq // tmx
    tmc = bsz * clen
    zero_state = jnp.zeros((bsz, GLA_HEADS * GLA_DV, GLA_QK), F32)
    wi1, wo1, wi2, wo2, wout = (w.astype(BF16) for w in (ffn1_wi, ffn1_wo, ffn2_wi, ffn2_wo, w_out))

    for l in range(depth):
        last = l == depth - 1
        ml = mod[l]
        wparts, wgate = _split_w_in(w_in[l])
        gw = jnp.zeros((LANE, 2 * GLA_QK), F32)
        gw = gw.at[0:GLA_LOWRANK, 0:GLA_QK].set(gla_gw_f[l])
        gw = gw.at[GLA_LOWRANK:2 * GLA_LOWRANK, GLA_QK:].set(gla_gw_b[l])
        gb = jnp.concatenate([gla_gb_f[l], gla_gb_b[l]]).reshape(1, -1)
        gw = _gate_fold(wgate, gw).astype(BF16)
        wbd = jnp.zeros((D_GROUP, D_GROUP), F32)
        for gi in range(len(POOL_WINDOWS)):
            wbd = wbd.at[gi * POOL_CH:(gi + 1) * POOL_CH, gi * POOL_CH:(gi + 1) * POOL_CH].set(pool_w[l, gi])
        p = dict(pool_wbd=wbd.astype(BF16), pool_scale=pool_scale[l].reshape(1, -1),
                 hy_short_w=hy_short_w[l], hy_short_b=hy_short_b[l], hy_bias=hy_bias[l],
                 gla_ng=jnp.tile(gla_norm[l], GLA_HEADS).reshape(1, -1),
                 conv_dw_w=conv_dw_w[l], conv_dw_b=conv_dw_b[l], conv_ln_g=conv_ln_g[l], conv_ln_b=conv_ln_b[l])
        names = ["kq", "v", "r", "pool", "hy", "conv"]

        xs, *outs = _ffn_proj(xs, _mod_rows(ml, xrows, 0, 5), ffn1_norm[l], wi1, wo1, mix_norm[l],
                              [wparts[nm] for nm in names], gw, gb, layer=l, tm=tmx, tiles_per_seq=tps_x)
        ux = dict(zip(names + ["lf", "lb"], outs))
        cnames = ["kq", "v"] if last else names
        cs, *outs = _ffn_proj(cs, _mod_rows(ml, crows, 0, 5), ffn1_norm[l], wi1, wo1, mix_norm[l],
                              [wparts[nm] for nm in cnames], gw, gb, layer=l, tm=tmc, tiles_per_seq=1)
        uc = dict(zip(cnames + ["lf", "lb"], outs))

        ccpb = clen // CHUNK
        sfc, sbc, finf, finb = _gla_states(uc["kq"], uc["v"], uc["lf"], uc["lb"], zero_state, zero_state,
                                           bsz=bsz, seq=clen, cpb=ccpb)
        sfx, sbx, _, _ = _gla_states(ux["kq"], ux["v"], ux["lf"], ux["lb"], finf, finb, bsz=bsz, seq=seq,
                                     cpb=16)

        filt = _hy_filter(seq, hy_w1[l], hy_b1[l], hy_w2[l], hy_b2[l], hy_w3[l], hy_deltas[l], t2_major=True)
        mixed = _mix(ux, p, bsz=bsz, seq=seq, is_ctx=False, states=(sfx, sbx), filt=filt)
        xs = _out_ffn(xs, _mod_rows(ml, xrows, 5, 4), mixed, wout, ffn2_norm[l], wi2, wo2, layer=l, tm=tmx,
                      tiles_per_seq=tps_x, final_g=final_norm if last else None)
        if not last:
            filt_c = _hy_filter(clen, hy_w1[l], hy_b1[l], hy_w2[l], hy_b2[l], hy_w3[l], hy_deltas[l],
                                t2_major=False)
            mixed = _mix(uc, p, bsz=bsz, seq=clen, is_ctx=True, states=(sfc, sbc), filt=filt_c)
            cs = _out_ffn(cs, _mod_rows(ml, crows, 5, 4), mixed, wout, ffn2_norm[l], wi2, wo2, layer=l, tm=tmc,
                          tiles_per_seq=1)
    return xs.reshape(bsz, seq, d)
```

```python
import functools
import math

import numpy as np
import jax
import jax.numpy as jnp
from jax import lax
from jax.experimental import pallas as pl
from jax.experimental.pallas import tpu as pltpu

F32 = jnp.float32
BF16 = jnp.bfloat16
HI = lax.Precision.HIGHEST

D_MODEL = 1024
GRID_W = 64
D_GROUP = 256
D_FF = 2816
N_MOD = 9
EPS = 1e-6
POOL_WINDOWS = (2, 4, 8, 16)
POOL_CH = 64
HY_BANDS = 16
HY_EMB = 1 + 2 * HY_BANDS
HY_FFN = 64
HY_FILTER_CH = 4 * D_GROUP
GLA_HEADS = 4
GLA_DK = 32
GLA_DV = 64
GLA_QK = 128
GLA_LOWRANK = 16
GLA_TAU = 16.0
CHUNK = 64
CONV_WIDTH = 31
HY_SHORT = 3

COL_K = 0
COL_V = COL_K + GLA_QK
COL_GF = COL_V + D_GROUP
COL_GB = COL_GF + GLA_LOWRANK
COL_Q = COL_GB + GLA_LOWRANK
COL_R = COL_Q + GLA_QK
COL_POOL = COL_R + D_GROUP
COL_HY = COL_POOL + D_GROUP
COL_CONV = COL_HY + 3 * D_GROUP
P_IN = COL_CONV + 2 * D_GROUP

LANE = 128
DFT2 = 128
FFT_GROUP = 16
MIB = 1024 * 1024


def _cparams(sem, vmem_mib=None):
    kw = dict(dimension_semantics=sem)
    if vmem_mib is not None:
        kw["vmem_limit_bytes"] = vmem_mib * MIB
    return pltpu.CompilerParams(**kw)


def _silu(x):
    return x * jax.nn.sigmoid(x)


def _rms_mod(h, g, m):
    y = h * lax.rsqrt(jnp.mean(h * h, axis=-1, keepdims=True) + EPS) * g
    return y * (1.0 + m[1:2, :]) + m[0:1, :]


def _iota(shape, dim):
    return lax.broadcasted_iota(jnp.int32, shape, dim)


MOD_ROWS = 3


def _mod_body(at_ref, w_ref, b_ref, o_ref):
    d, tn = w_ref.shape[1], w_ref.shape[2]
    at = _silu(at_ref[...])
    cols = [jnp.broadcast_to(at[:, r:r + 1], (d, LANE)).reshape(d // SUBLANES, SUBLANES, LANE)
            for r in range(MOD_ROWS)]
    o_ref[0] = jnp.zeros((8, tn), F32) + b_ref[0]
    for j in range(tn // LANE):
        lsl = slice(j * LANE, (j + 1) * LANE)
        w3 = w_ref[0, :, lsl].reshape(d // SUBLANES, SUBLANES, LANE)
        for r in range(MOD_ROWS):
            part = jnp.sum(w3 * cols[r], axis=0)
            o_ref[0, r:r + 1, lsl] += jnp.sum(part, axis=0, keepdims=True)


def _modulation(cc, ada_w, ada_b):
    nl, d, nm = ada_w.shape
    tn = 2304
    return pl.pallas_call(
        _mod_body,
        out_shape=jax.ShapeDtypeStruct((nl, 8, nm), F32),
        grid=(nl, nm // tn),
        in_specs=[pl.BlockSpec((d, 8), lambda l, j: (0, 0)),
                  pl.BlockSpec((1, d, tn), lambda l, j: (l, 0, j)),
                  pl.BlockSpec((1, 1, tn), lambda l, j: (l, 0, j))],
        out_specs=pl.BlockSpec((1, 8, tn), lambda l, j: (l, 0, j)),
        compiler_params=_cparams(("parallel", "parallel"), 40),
        name="adaln_mod",
    )(cc.T, ada_w, ada_b.reshape(nl, 1, nm))


FF_CHUNK = 256
ROW_SUB = 256


def _swiglu(xn, wi_ref, wo_ref, hm_ref, rows):
    ff = wo_ref.shape[0]
    for c in range(0, ff, FF_CHUNK):
        a = jnp.dot(xn, wi_ref[:, c:c + FF_CHUNK], preferred_element_type=F32)
        g = jnp.dot(xn, wi_ref[:, ff + c:ff + c + FF_CHUNK], preferred_element_type=F32)
        hm_ref[rows, c:c + FF_CHUNK] = (_silu(g) * a).astype(BF16)
    return jnp.dot(hm_ref[rows, :], wo_ref[...], preferred_element_type=F32)


def _resident(shape):
    nd = len(shape)
    return pl.BlockSpec(shape, lambda i: (0,) * nd, pipeline_mode=pl.Buffered(1))


def _resident_layer(shape, layer):
    nd = len(shape)
    return pl.BlockSpec((None,) + tuple(shape), lambda i: (layer,) + (0,) * nd, pipeline_mode=pl.Buffered(1))


def _gate_fold_body(wg_ref, gw_ref, o_ref):
    o_ref[...] = jnp.dot(wg_ref[...], gw_ref[...], precision=HI, preferred_element_type=F32)


def _gate_fold(w_gate, gw):
    d = w_gate.shape[0]
    return pl.pallas_call(
        _gate_fold_body,
        out_shape=jax.ShapeDtypeStruct((d, gw.shape[1]), F32),
        grid=(1,),
        in_specs=[pl.BlockSpec(w_gate.shape, lambda i: (0, 0)), pl.BlockSpec(gw.shape, lambda i: (0, 0))],
        out_specs=pl.BlockSpec((d, gw.shape[1]), lambda i: (0, 0)),
        name="gate_fold",
    )(w_gate, gw)


def _chunk_prefix(x):
    pos = _iota(x.shape, 0) & (CHUNK - 1)
    shift = 1
    while shift < CHUNK:
        x = x + jnp.where(pos >= shift, pltpu.roll(x, shift, 0), 0.0)
        shift *= 2
    return x


def _ffn_proj_body(h_ref, m_ref, g1_ref, wi_ref, wo_ref, g2_ref, *rest, nparts):
    w_refs = rest[:nparts]
    gw_ref, gb_ref = rest[nparts:nparts + 2]
    x_ref = rest[nparts + 2]
    o_refs = rest[nparts + 3:2 * nparts + 5]
    hm_ref = rest[-1]
    m = m_ref[0]
    for r0 in range(0, h_ref.shape[0], ROW_SUB):
        rows = slice(r0, r0 + ROW_SUB)
        h = h_ref[rows, :]
        xn = _rms_mod(h, g1_ref[...], m[0:2]).astype(BF16)
        x1 = h + (0.5 * m[2:3, :]) * _swiglu(xn, wi_ref, wo_ref, hm_ref, rows)
        x_ref[rows, :] = x1
        xn2 = _rms_mod(x1, g2_ref[...], m[3:5]).astype(BF16)
        for w_ref, o_ref in zip(w_refs, o_refs[:nparts]):
            o_ref[rows, :] = jnp.dot(xn2, w_ref[...], preferred_element_type=F32).astype(o_ref.dtype)
        a = jnp.dot(xn2, gw_ref[...], preferred_element_type=F32) + gb_ref[...]
        ls = (jnp.minimum(a, 0.0) - jnp.log(1.0 + jnp.exp(-jnp.abs(a)))) * (1.0 / GLA_TAU)
        pre = _chunk_prefix(ls)
        o_refs[nparts][rows, :] = pre[:, :GLA_QK]
        pb = pre[:, GLA_QK:]
        tot = jnp.concatenate([jnp.broadcast_to(pb[c0 + CHUNK - 1:c0 + CHUNK, :], (CHUNK, GLA_QK))
                               for c0 in range(0, ROW_SUB, CHUNK)], axis=0)
        o_refs[nparts + 1][rows, :] = tot - pb + ls[:, GLA_QK:]


def _mod_spec(d, layer, row0, tiles_per_seq):
    return pl.BlockSpec((None, 1, N_MOD, d), lambda i: (layer, row0 + i // tiles_per_seq, 0, 0))


def _ffn_proj(h, mod, g1, wi, wo, g2, w_parts, gw, gb, *, layer, row0, tm, tiles_per_seq):
    n, d = h.shape
    ff = wo.shape[1]
    nparts = len(w_parts)
    widths = [w.shape[1] for w in w_parts]
    tok = lambda wd: pl.BlockSpec((tm, wd), lambda i: (i, 0))
    in_specs = [tok(d), _mod_spec(d, layer, row0, tiles_per_seq), _resident((1, d)),
                _resident_layer((d, 2 * ff), layer), _resident_layer((ff, d), layer), _resident((1, d))]
    in_specs += [_resident((d, wd)) for wd in widths]
    in_specs += [_resident((d, 2 * GLA_QK)), _resident((1, 2 * GLA_QK))]
    out_shape = [jax.ShapeDtypeStruct((n, d), F32)]
    out_shape += [jax.ShapeDtypeStruct((n, wd), BF16) for wd in widths]
    out_shape += [jax.ShapeDtypeStruct((n, GLA_QK), F32)] * 2
    return pl.pallas_call(
        functools.partial(_ffn_proj_body, nparts=nparts),
        out_shape=out_shape,
        grid=(n // tm,),
        in_specs=in_specs,
        out_specs=[tok(d)] + [tok(wd) for wd in widths] + [tok(GLA_QK)] * 2,
        scratch_shapes=[pltpu.VMEM((tm, ff), BF16)],
        compiler_params=_cparams(("parallel",), 56),
        name="ffn1_in_proj",
    )(h, mod, g1.reshape(1, d), wi, wo, g2.reshape(1, d), *w_parts, gw, gb)


def _out_ffn_body(x_ref, m_ref, p_ref, hy_ref, gl_ref, cv_ref, wout_ref, g_ref, wi_ref, wo_ref, fn_ref, o_ref,
                  hm_ref, *, final):
    m = m_ref[0]
    for r0 in range(0, x_ref.shape[0], ROW_SUB):
        rows = slice(r0, r0 + ROW_SUB)
        acc = jnp.dot(p_ref[rows, :].astype(BF16), wout_ref[0:D_GROUP, :], preferred_element_type=F32)
        acc += jnp.dot(hy_ref[rows, :].astype(BF16), wout_ref[D_GROUP:2 * D_GROUP, :], preferred_element_type=F32)
        acc += jnp.dot(gl_ref[rows, :].astype(BF16), wout_ref[2 * D_GROUP:3 * D_GROUP, :],
                       preferred_element_type=F32)
        acc += jnp.dot(cv_ref[rows, :].astype(BF16), wout_ref[3 * D_GROUP:, :], preferred_element_type=F32)
        x2 = x_ref[rows, :] + m[5:6, :] * acc
        xn = _rms_mod(x2, g_ref[...], m[6:8]).astype(BF16)
        out = x2 + (0.5 * m[8:9, :]) * _swiglu(xn, wi_ref, wo_ref, hm_ref, rows)
        if final:
            out = out * lax.rsqrt(jnp.mean(out * out, axis=-1, keepdims=True) + EPS) * fn_ref[...]
        o_ref[rows, :] = out


def _out_ffn(x, mod, parts, w_out, g, wi, wo, *, layer, row0, tm, tiles_per_seq, final_g=None):
    n, d = x.shape
    ff = wo.shape[1]
    final = final_g is not None
    fg = final_g if final else g
    tok = lambda wd: pl.BlockSpec((tm, wd), lambda i: (i, 0))
    return pl.pallas_call(
        functools.partial(_out_ffn_body, final=final),
        out_shape=jax.ShapeDtypeStruct((n, d), F32),
        grid=(n // tm,),
        in_specs=[tok(d), _mod_spec(d, layer, row0, tiles_per_seq),
                  tok(D_GROUP), tok(D_GROUP), tok(D_GROUP), tok(D_GROUP), _resident_layer((d, d), layer),
                  _resident((1, d)), _resident_layer((d, 2 * ff), layer), _resident_layer((ff, d), layer),
                  _resident((1, d))],
        out_specs=tok(d),
        scratch_shapes=[pltpu.VMEM((tm, ff), BF16)],
        compiler_params=_cparams(("parallel",), 56),
        name="out_proj_ffn2",
    )(x, mod, *parts, w_out, g.reshape(1, d), wi, wo, fg.reshape(1, d))


def _gla_state_body(kf, vf, bf, kb, vb, bb, s0f, s0b, sf_o, sb_o, ff_o, fb_o, stf, stb, *, cpb):
    i = pl.program_id(1)

    @pl.when(i == 0)
    def _():
        stf[...] = s0f[0]
        stb[...] = s0b[0]

    sshape = (GLA_HEADS * GLA_DV, GLA_QK)
    bmask = (_iota(sshape, 0) >> 6) == (_iota(sshape, 1) >> 5)
    tn_dims = (((0,), (0,)), ((), ()))

    def direction(k_ref, v_ref, b_ref, last, st, s_o, order):
        s = st[...]
        for ci in order:
            sl = slice(ci * CHUNK, (ci + 1) * CHUNK)
            b = b_ref[sl, :]
            tot = b[last:last + 1, :]
            kd = (k_ref[sl, :] * jnp.exp(tot - b)).astype(BF16)
            upd = lax.dot_general(v_ref[sl, :].astype(BF16), kd, tn_dims, preferred_element_type=F32)
            s_o[0, ci] = s.astype(BF16)
            s = s * jnp.exp(tot) + jnp.where(bmask, upd, 0.0)
        st[...] = s

    direction(kf, vf, bf, CHUNK - 1, stf, sf_o, range(cpb))
    direction(kb, vb, bb, 0, stb, sb_o, range(cpb - 1, -1, -1))
    ff_o[0] = stf[...]
    fb_o[0] = stb[...]


def _gla_states(k, v, lf, lb, s0f, s0b, *, bsz, seq, cpb):
    bt = cpb * CHUNK
    nb = seq // bt
    nc = seq // CHUNK
    srow = GLA_HEADS * GLA_DV

    def tf(b, i):
        return (b * nb + i, 0)

    def tb(b, i):
        return (b * nb + nb - 1 - i, 0)

    sblk = pl.BlockSpec((1, srow, GLA_QK), lambda b, i: (b, 0, 0))
    return pl.pallas_call(
        functools.partial(_gla_state_body, cpb=cpb),
        out_shape=[jax.ShapeDtypeStruct((bsz, nc, srow, GLA_QK), BF16),
                   jax.ShapeDtypeStruct((bsz, nc, srow, GLA_QK), BF16),
                   jax.ShapeDtypeStruct((bsz, srow, GLA_QK), F32),
                   jax.ShapeDtypeStruct((bsz, srow, GLA_QK), F32)],
        grid=(bsz, nb),
        in_specs=[pl.BlockSpec((bt, GLA_QK), tf), pl.BlockSpec((bt, D_GROUP), tf), pl.BlockSpec((bt, GLA_QK), tf),
                  pl.BlockSpec((bt, GLA_QK), tb), pl.BlockSpec((bt, D_GROUP), tb), pl.BlockSpec((bt, GLA_QK), tb),
                  sblk, sblk],
        out_specs=[pl.BlockSpec((1, cpb, srow, GLA_QK), lambda b, i: (b, i, 0, 0)),
                   pl.BlockSpec((1, cpb, srow, GLA_QK), lambda b, i: (b, nb - 1 - i, 0, 0)),
                   sblk, sblk],
        scratch_shapes=[pltpu.VMEM((srow, GLA_QK), F32), pltpu.VMEM((srow, GLA_QK), F32)],
        compiler_params=_cparams(("parallel", "arbitrary")),
        name="gla_states",
    )(k, v, lf, k, v, lb, s0f, s0b)


def _gla_read_body(k_ref, v_ref, q_ref, r_ref, bf_ref, bb_ref, sf_ref, sb_ref, ng_ref, o_ref, acc_ref, *, cpb):
    hrows = GLA_HEADS * CHUNK
    cpos = _iota((hrows, CHUNK), 0) & (CHUNK - 1)
    ccol = _iota((hrows, CHUNK), 1)
    lowm = cpos >= ccol
    upm = cpos <= ccol
    hq = (_iota((hrows, GLA_QK), 0) >> 6) == (_iota((hrows, GLA_QK), 1) >> 5)
    ho = (_iota((hrows, D_GROUP), 0) >> 6) == (_iota((hrows, D_GROUP), 1) >> 6)
    bavg = jnp.where((_iota((D_GROUP, D_GROUP), 0) >> 6) == (_iota((D_GROUP, D_GROUP), 1) >> 6),
                     1.0 / GLA_DV, 0.0).astype(BF16)
    nt_dims = (((1,), (1,)), ((), ()))

    bf_ = bf_ref[...]
    bb_ = bb_ref[...]
    qs = q_ref[...].astype(F32) * (GLA_DK ** -0.5)
    kk = k_ref[...]
    qef = (qs * jnp.exp(bf_)).astype(BF16)
    qeb = (qs * jnp.exp(bb_)).astype(BF16)
    kef = (kk * jnp.exp(-bf_)).astype(BF16)
    keb = (kk * jnp.exp(-bb_)).astype(BF16)
    vb16 = v_ref[...].astype(BF16)
    zero = jnp.zeros((), BF16)
    for ci in range(cpb):
        sl = slice(ci * CHUNK, (ci + 1) * CHUNK)
        qf4 = jnp.where(hq, jnp.concatenate([qef[sl]] * GLA_HEADS, axis=0), zero)
        qb4 = jnp.where(hq, jnp.concatenate([qeb[sl]] * GLA_HEADS, axis=0), zero)
        af = lax.dot_general(qf4, kef[sl], nt_dims, preferred_element_type=F32)
        ab = lax.dot_general(qb4, keb[sl], nt_dims, preferred_element_type=F32)
        att = (jnp.where(lowm, af, 0.0) + jnp.where(upm, ab, 0.0)).astype(BF16)
        oall = jnp.dot(att, vb16[sl], preferred_element_type=F32)
        om = jnp.where(ho, oall, 0.0)
        o = om[0:CHUNK] + om[CHUNK:2 * CHUNK] + om[2 * CHUNK:3 * CHUNK] + om[3 * CHUNK:4 * CHUNK]
        qcat = jnp.concatenate([qef[sl], qeb[sl]], axis=1)
        scat = jnp.concatenate([sf_ref[0, ci], sb_ref[0, ci]], axis=1)
        acc_ref[sl, :] = o + lax.dot_general(qcat, scat, nt_dims, preferred_element_type=F32)
    o = acc_ref[...]
    ms = _chunk_sum_rhs(o * o, bavg)
    o_ref[...] = (o * lax.rsqrt(ms + EPS) * ng_ref[...] * _silu(r_ref[...].astype(F32))).astype(o_ref.dtype)


def _chunk_sum_rhs(x, mat):
    hi, lo = _split_bf16(x)
    return jnp.dot(hi, mat, preferred_element_type=F32) + jnp.dot(lo, mat, preferred_element_type=F32)


def _gla_read(k, v, q, r, lf, lb, sf, sb, ng, *, bsz, seq, cpb):
    bt = cpb * CHUNK
    nb = seq // bt
    srow = GLA_HEADS * GLA_DV
    n = bsz * seq

    def tk(i):
        return (i, 0)

    sspec = pl.BlockSpec((1, cpb, srow, GLA_QK), lambda i: (i // nb, i % nb, 0, 0))
    return pl.pallas_call(
        functools.partial(_gla_read_body, cpb=cpb),
        out_shape=jax.ShapeDtypeStruct((n, D_GROUP), BF16),
        scratch_shapes=[pltpu.VMEM((bt, D_GROUP), F32)],
        grid=(bsz * nb,),
        in_specs=[pl.BlockSpec((bt, GLA_QK), tk), pl.BlockSpec((bt, D_GROUP), tk),
                  pl.BlockSpec((bt, GLA_QK), lambda i: (i, 1)), pl.BlockSpec((bt, D_GROUP), tk),
                  pl.BlockSpec((bt, GLA_QK), tk), pl.BlockSpec((bt, GLA_QK), tk),
                  sspec, sspec, pl.BlockSpec((1, D_GROUP), lambda i: (0, 0))],
        out_specs=pl.BlockSpec((bt, D_GROUP), tk),
        compiler_params=_cparams(("parallel",)),
        name="gla_readout",
    )(k, v, q, r, lf, lb, sf, sb, ng)


def _box_matrix(n, w):
    pos = np.arange(n)
    lo = np.clip(pos - w // 2, 0, n)
    hi = np.clip(pos - w // 2 + w, 0, n)
    col = np.arange(n)[None, :]
    return ((col >= lo[:, None]) & (col < hi[:, None])).astype(np.float32)


def _lane_windows(shape):
    w = jnp.left_shift(2, _iota(shape, 1) >> 6)
    return w, w >> 1


def _box_count(pos, w, half, n):
    return jnp.minimum(pos - half + w, n) - jnp.maximum(pos - half, 0)


def _split_bf16(x):
    hi = x.astype(BF16)
    lo = (x - hi.astype(F32)).astype(BF16)
    return hi, lo


def _pool2d_body(cur_ref, prev_ref, next_ref, pc_ref, w_ref, sc_ref, o_ref, ycol, *, tiles, rows):
    i = pl.program_id(1)
    tm = cur_ref.shape[0]
    hb = prev_ref.shape[0]
    sub = 2 * GRID_W
    pflag = jnp.where(i > 0, 1.0, 0.0)
    nflag = jnp.where(i < tiles - 1, 1.0, 0.0)

    def colpool(x):
        halves = []
        for half in range(2):
            lsl = slice(half * LANE, (half + 1) * LANE)
            ys = [jnp.dot(pc_ref[wi], x[:, lsl], preferred_element_type=F32) for wi in (2 * half, 2 * half + 1)]
            lane = _iota((sub, LANE), 1)
            halves.append(jnp.where(lane < POOL_CH, ys[0], ys[1]))
        return jnp.concatenate(halves, axis=1)

    for s in range(hb // sub):
        ycol[s * sub:(s + 1) * sub, :] = colpool(prev_ref[s * sub:(s + 1) * sub, :]) * pflag
    for s in range(tm // sub):
        ycol[hb + s * sub:hb + (s + 1) * sub, :] = colpool(cur_ref[s * sub:(s + 1) * sub, :])
    for s in range(hb // sub):
        ycol[hb + tm + s * sub:hb + tm + (s + 1) * sub, :] = colpool(next_ref[s * sub:(s + 1) * sub, :]) * nflag

    rc = 256
    wl, half = _lane_windows((rc, D_GROUP))
    narrow = _iota((rc, LANE), 1) < POOL_CH
    for r0 in range(0, tm, rc):
        def band(lo, hi, lsl):
            base = hb + r0
            acc = ycol[base + GRID_W * lo:base + GRID_W * lo + rc, lsl]
            for dd in range(lo + 1, hi):
                acc = acc + ycol[base + GRID_W * dd:base + GRID_W * dd + rc, lsl]
            return acc

        left, right = slice(0, LANE), slice(LANE, 2 * LANE)
        z2 = band(-1, 1, left)
        z4 = z2 + band(-2, -1, left) + band(1, 2, left)
        z8 = band(-4, 4, right)
        z16 = z8 + band(-8, -4, right) + band(4, 8, right)
        z = jnp.concatenate([jnp.where(narrow, z2, z4), jnp.where(narrow, z8, z16)], axis=1)
        tok = _iota((rc, D_GROUP), 0) + (i * tm + r0)
        rcnt = _box_count(tok >> 6, wl, half, rows)
        ccnt = _box_count(tok & (GRID_W - 1), wl, half, GRID_W)
        pooled = z / (rcnt * ccnt).astype(F32)
        dlt = (pooled - cur_ref[r0:r0 + rc, :]).astype(BF16)
        o_ref[r0:r0 + rc, :] = (jnp.dot(dlt, w_ref[...], preferred_element_type=F32) * sc_ref[...]).astype(o_ref.dtype)


def _pool2d(u, wbd, scale, *, bsz, seq):
    tm = 1024
    hb = 512
    tiles = seq // tm
    r = tm // hb
    nhb = seq // hb
    pc = np.stack([np.kron(np.eye(2, dtype=np.float32), _box_matrix(GRID_W, w)) for w in POOL_WINDOWS])
    return pl.pallas_call(
        functools.partial(_pool2d_body, tiles=tiles, rows=seq // GRID_W),
        out_shape=jax.ShapeDtypeStruct((bsz * seq, D_GROUP), BF16),
        grid=(bsz, tiles),
        in_specs=[pl.BlockSpec((tm, D_GROUP), lambda b, i: (b * tiles + i, 0)),
                  pl.BlockSpec((hb, D_GROUP), lambda b, i: (b * nhb + jnp.maximum(i * r - 1, 0), 0)),
                  pl.BlockSpec((hb, D_GROUP), lambda b, i: (b * nhb + jnp.minimum(i * r + r, nhb - 1), 0)),
                  pl.BlockSpec((4, 2 * GRID_W, 2 * GRID_W), lambda b, i: (0, 0, 0)),
                  pl.BlockSpec((D_GROUP, D_GROUP), lambda b, i: (0, 0)),
                  pl.BlockSpec((1, D_GROUP), lambda b, i: (0, 0))],
        out_specs=pl.BlockSpec((tm, D_GROUP), lambda b, i: (b * tiles + i, 0)),
        scratch_shapes=[pltpu.VMEM((tm + 2 * hb, D_GROUP), F32)],
        compiler_params=_cparams(("parallel", "parallel")),
        name="pool2d",
    )(u, u, u, jnp.asarray(pc, BF16), wbd, scale)


def _pool1d_body(x_ref, p_ref, w_ref, sc_ref, o_ref):
    x = x_ref[...]
    n = x.shape[0]
    ys = [jnp.dot(p_ref[wi], x, preferred_element_type=F32) for wi in range(4)]
    wl, half = _lane_windows((n, D_GROUP))
    z = jnp.where(wl == 2, ys[0], jnp.where(wl == 4, ys[1], jnp.where(wl == 8, ys[2], ys[3])))
    cnt = _box_count(_iota((n, D_GROUP), 0), wl, half, n)
    dlt = (z / cnt.astype(F32) - x).astype(BF16)
    o_ref[...] = (jnp.dot(dlt, w_ref[...], preferred_element_type=F32) * sc_ref[...]).astype(o_ref.dtype)


def _pool1d(u, wbd, scale, *, bsz, seq):
    pm = np.stack([_box_matrix(seq, w) for w in POOL_WINDOWS])
    return pl.pallas_call(
        _pool1d_body,
        out_shape=jax.ShapeDtypeStruct((bsz * seq, D_GROUP), BF16),
        grid=(bsz,),
        in_specs=[pl.BlockSpec((seq, D_GROUP), lambda b: (b, 0)),
                  pl.BlockSpec((4, seq, seq), lambda b: (0, 0, 0)),
                  pl.BlockSpec((D_GROUP, D_GROUP), lambda b: (0, 0)),
                  pl.BlockSpec((1, D_GROUP), lambda b: (0, 0))],
        out_specs=pl.BlockSpec((seq, D_GROUP), lambda b: (b, 0)),
        compiler_params=_cparams(("parallel",)),
        name="pool1d",
    )(u, jnp.asarray(pm, BF16), wbd, scale)


def _fill_halo(buf, cur, prev, nxt, i, tps, hb, tm, pre):
    first = (i % tps) == 0
    last = (i % tps) == tps - 1
    buf[0:hb, :] = jnp.where(first, 0.0, pre(prev[...]))
    buf[hb:hb + tm, :] = pre(cur[...])
    buf[hb + tm:hb + tm + hb, :] = jnp.where(last, 0.0, pre(nxt[...]))


SUBLANES = 8


def _tap_phases(taps, off):
    return sorted({(off + j) % SUBLANES for j in range(taps)} - {0})


def _fill_phases(sh, buf, phases):
    rows = sh.shape[1]
    for slot, s in enumerate(phases):
        sh[slot, :, :] = buf[s:s + rows, :]


def _dwconv(buf, sh, phases, w_ref, r0, rc, taps, off):
    acc = None
    for j in range(taps):
        s, q = (off + j) % SUBLANES, (off + j) // SUBLANES
        lo = r0 + SUBLANES * q
        src = buf[lo:lo + rc, :] if s == 0 else sh[phases.index(s), lo:lo + rc, :]
        term = src * w_ref[j:j + 1, :]
        acc = term if acc is None else acc + term
    return acc


def _conf_body(cur, prev, nxt, w_ref, b_ref, lg_ref, lb_ref, o_ref, buf, sh, *, tps):
    i = pl.program_id(0)
    tm = cur.shape[0]
    hb = prev.shape[0]

    def glu(u):
        u = u.astype(F32)
        return u[:, :D_GROUP] * jax.nn.sigmoid(u[:, D_GROUP:])

    _fill_halo(buf, cur, prev, nxt, i, tps, hb, tm, glu)
    rc = 128
    off = hb - (CONV_WIDTH - 1) // 2
    phases = _tap_phases(CONV_WIDTH, off)
    _fill_phases(sh, buf, phases)
    for r0 in range(0, tm, rc):
        h = _dwconv(buf, sh, phases, w_ref, r0, rc, CONV_WIDTH, off) + b_ref[...]
        mu = jnp.mean(h, axis=-1, keepdims=True)
        hc = h - mu
        var = jnp.mean(hc * hc, axis=-1, keepdims=True)
        o_ref[r0:r0 + rc, :] = _silu(hc * lax.rsqrt(var + EPS) * lg_ref[...] + lb_ref[...]).astype(o_ref.dtype)


def _halo_specs(tm, hb, width, nrows):
    r = tm // hb
    nhb = nrows // hb
    return [pl.BlockSpec((tm, width), lambda i: (i, 0)),
            pl.BlockSpec((hb, width), lambda i: (jnp.maximum(i * r - 1, 0), 0)),
            pl.BlockSpec((hb, width), lambda i: (jnp.minimum(i * r + r, nhb - 1), 0))]


def _conformer(u, w, b, lg, lb, *, seq, tm):
    n = u.shape[0]
    hb = 16
    nph = len(_tap_phases(CONV_WIDTH, hb - (CONV_WIDTH - 1) // 2))
    vec = pl.BlockSpec((1, D_GROUP), lambda i: (0, 0))
    return pl.pallas_call(
        functools.partial(_conf_body, tps=seq // tm),
        out_shape=jax.ShapeDtypeStruct((n, D_GROUP), BF16),
        grid=(n // tm,),
        in_specs=_halo_specs(tm, hb, 2 * D_GROUP, n) + [pl.BlockSpec((CONV_WIDTH, D_GROUP), lambda i: (0, 0)),
                                                        vec, vec, vec],
        out_specs=pl.BlockSpec((tm, D_GROUP), lambda i: (i, 0)),
        scratch_shapes=[pltpu.VMEM((tm + 2 * hb, D_GROUP), F32),
                        pltpu.VMEM((nph, tm + 2 * hb - SUBLANES, D_GROUP), F32)],
        compiler_params=_cparams(("parallel",)),
        name="conformer_conv",
    )(u, u, u, w, b.reshape(1, -1), lg.reshape(1, -1), lb.reshape(1, -1))


def _short_body(cur, prev, nxt, w_ref, b_ref, v_ref, x1_ref, x2_ref, buf, sh, *, tps):
    i = pl.program_id(0)
    tm = cur.shape[0]
    hb = prev.shape[0]
    _fill_halo(buf, cur, prev, nxt, i, tps, hb, tm, lambda u: u.astype(F32))
    rc = 128
    off = hb - (HY_SHORT - 1) // 2
    phases = _tap_phases(HY_SHORT, off)
    _fill_phases(sh, buf, phases)
    for r0 in range(0, tm, rc):
        uc = _dwconv(buf, sh, phases, w_ref, r0, rc, HY_SHORT, off) + b_ref[...]
        v_ref[r0:r0 + rc, :] = uc[:, :D_GROUP]
        x1_ref[r0:r0 + rc, :] = uc[:, D_GROUP:2 * D_GROUP].astype(x1_ref.dtype)
        x2_ref[r0:r0 + rc, :] = uc[:, 2 * D_GROUP:].astype(x2_ref.dtype)


def _hy_short(u, w, b, *, seq, tm):
    n = u.shape[0]
    hb = 16
    wd = 3 * D_GROUP
    nph = len(_tap_phases(HY_SHORT, hb - (HY_SHORT - 1) // 2))
    ospec = pl.BlockSpec((tm, D_GROUP), lambda i: (i, 0))
    return pl.pallas_call(
        functools.partial(_short_body, tps=seq // tm),
        out_shape=[jax.ShapeDtypeStruct((n, D_GROUP), dt) for dt in (F32, BF16, BF16)],
        grid=(n // tm,),
        in_specs=_halo_specs(tm, hb, wd, n) + [pl.BlockSpec((HY_SHORT, wd), lambda i: (0, 0)),
                                               pl.BlockSpec((1, wd), lambda i: (0, 0))],
        out_specs=[ospec, ospec, ospec],
        scratch_shapes=[pltpu.VMEM((tm + 2 * hb, wd), F32),
                        pltpu.VMEM((nph, tm + 2 * hb - SUBLANES, wd), F32)],
        compiler_params=_cparams(("parallel",)),
        name="hyena_short_conv",
    )(u, u, u, w, b.reshape(1, -1))


def _filter_features(n):
    i = np.arange(n, dtype=np.float64)
    t = np.linspace(0.0, 1.0, n, dtype=np.float32).astype(np.float64)
    wpos = ((2.0 * math.pi / n) * np.arange(n, dtype=np.float32)).astype(np.float32)
    bands = np.linspace(1e-4, HY_BANDS - 1, HY_BANDS, dtype=np.float32)
    arg = (bands[None, :] * wpos[:, None]).astype(np.float32).astype(np.float64)
    z = np.zeros((n, 64), np.float32)
    z[:, 0] = t
    z[:, 1:1 + HY_BANDS] = np.cos(arg)
    z[:, 1 + HY_BANDS:HY_EMB] = -np.sin(arg)
    del i
    return z


def _filter_body(z_ref, w1_ref, b1_ref, w2_ref, b2_ref, w3a_ref, w3b_ref, d_ref, hf_ref, hb_ref, s_ref):
    i = pl.program_id(0)
    half = z_ref.shape[0]
    z = z_ref[...]
    h = jnp.sin(jnp.dot(z, w1_ref[...], precision=HI, preferred_element_type=F32) + b1_ref[...])
    h = jnp.sin(jnp.dot(h, w2_ref[...], precision=HI, preferred_element_type=F32) + b2_ref[...])
    absd = jnp.abs(d_ref[...])
    h = h.astype(BF16)
    tot = None
    for part, (w3_ref, tcol) in enumerate(((w3a_ref, 0), (w3b_ref, 64))):
        hp = jnp.dot(h, w3_ref[...], preferred_element_type=F32)
        hp = hp * jnp.exp(-z[:, tcol:tcol + 1] * absd)
        rows = slice(part * half, (part + 1) * half)
        for o in range(2):
            c0 = 2 * D_GROUP * o
            hf_ref[rows, o * D_GROUP:(o + 1) * D_GROUP] = hp[:, c0:c0 + D_GROUP].astype(hf_ref.dtype)
            hb_ref[rows, o * D_GROUP:(o + 1) * D_GROUP] = hp[:, c0 + D_GROUP:c0 + 2 * D_GROUP].astype(hb_ref.dtype)
        part_sum = jnp.sum(jnp.abs(hp), axis=0, keepdims=True)
        tot = part_sum if tot is None else tot + part_sum

    @pl.when(i == 0)
    def _():
        s_ref[...] = jnp.zeros_like(s_ref)

    s_ref[...] += tot


def _hy_filter(n, w1, b1, w2, b2, w3, deltas, *, t2_major):
    tm = min(n, 512)
    half = tm // 2
    z = _filter_features(n)
    if t2_major:
        z = z.reshape(n // DFT2, DFT2, 64).transpose(1, 0, 2).reshape(n, 64)
    zt = z.reshape(n // tm, 2, half, 64)
    z2 = jnp.asarray(np.concatenate([zt[:, 0], zt[:, 1]], axis=-1).reshape(n // 2, 2 * 64))
    w1p = jnp.zeros((64, HY_FFN), F32).at[:HY_EMB].set(w1)
    zero = jnp.zeros((64, HY_FFN), F32)
    bd = lambda w: jnp.concatenate([jnp.concatenate([w, zero], axis=1), jnp.concatenate([zero, w], axis=1)], axis=0)
    w3 = w3.astype(BF16)
    zero3 = jnp.zeros_like(w3)
    full = lambda shape: pl.BlockSpec(shape, lambda i: (0, 0))
    tok = pl.BlockSpec((tm, 2 * D_GROUP), lambda i: (i, 0))
    return pl.pallas_call(
        _filter_body,
        out_shape=[jax.ShapeDtypeStruct((n, 2 * D_GROUP), BF16), jax.ShapeDtypeStruct((n, 2 * D_GROUP), BF16),
                   jax.ShapeDtypeStruct((1, HY_FILTER_CH), F32)],
        grid=(n // tm,),
        in_specs=[pl.BlockSpec((half, 2 * 64), lambda i: (i, 0)), full((2 * 64, 2 * HY_FFN)), full((1, 2 * HY_FFN)),
                  full((2 * HY_FFN, 2 * HY_FFN)), full((1, 2 * HY_FFN)), full((2 * HY_FFN, HY_FILTER_CH)),
                  full((2 * HY_FFN, HY_FILTER_CH)), full((1, HY_FILTER_CH))],
        out_specs=[tok, tok, full((1, HY_FILTER_CH))],
        compiler_params=_cparams(("arbitrary",)),
        name="hyena_filter_mlp",
    )(z2, bd(w1p), jnp.tile(b1.reshape(1, -1), (1, 2)), bd(w2), jnp.tile(b2.reshape(1, -1), (1, 2)),
      jnp.concatenate([w3, zero3], axis=0), jnp.concatenate([zero3, w3], axis=0), deltas.reshape(1, -1))


@functools.lru_cache(maxsize=None)
def _dft_tables(n1):
    nn = n1 * DFT2
    j = np.arange(DFT2)
    th = 2.0 * np.pi * ((np.outer(j, j)) % DFT2) / DFT2
    fr, fi = np.cos(th), -np.sin(th)
    m_fwd = np.block([[fr, -fi], [fi, fr]])
    m_inv = np.block([[fr, fi], [-fi, fr]])
    hh = n1 // 2
    k1 = np.arange(n1)[None, :, None]
    t1 = np.arange(hh)[None, None, :]
    t2 = np.arange(DFT2)[:, None, None]
    ph = 2.0 * np.pi * ((k1 * (DFT2 * t1 + t2)) % nn) / nn
    ar, ai = np.cos(ph), -np.sin(ph)
    a_fwd = np.concatenate([np.concatenate([ar, -ai], axis=2), np.concatenate([ai, ar], axis=2)], axis=1)
    a_real = np.concatenate([ar, ai], axis=1)
    t1r = np.where(t2 >= 1, n1 - 1 - t1, (n1 - t1) % n1)
    phr = 2.0 * np.pi * ((k1 * (DFT2 * t1r + t2)) % nn) / nn
    a_rev = np.concatenate([np.cos(phr), -np.sin(phr)], axis=1)
    a_rev[0, :, 0] = 0.0
    pht = np.transpose(ph, (0, 2, 1))
    cr, ci = np.cos(pht) / nn, np.sin(pht) / nn
    a_inv = np.concatenate([np.concatenate([cr, -ci], axis=2), np.concatenate([ci, cr], axis=2)], axis=1)
    f32 = lambda a: np.ascontiguousarray(a, dtype=np.float32)
    return dict(m_fwd=f32(m_fwd), m_inv=f32(m_inv), a_fwd=f32(a_fwd), a_real=f32(a_real), a_rev=f32(a_rev),
                a_inv=f32(a_inv))


def _fft_a_body(x_ref, m_ref, o_ref, *, tj):
    n1 = o_ref.shape[2]
    for e in range(tj):
        x = jnp.concatenate([x_ref[0, :, e, :], x_ref[1, :, e, :]], axis=0).astype(BF16)
        a = jnp.dot(m_ref[e], x, preferred_element_type=F32)
        o_ref[0, e] = a[:n1]
        o_ref[1, e] = a[n1:]


def _fft_a(x4, m, *, n1, tj):
    c = x4.shape[3]
    return pl.pallas_call(
        functools.partial(_fft_a_body, tj=tj),
        out_shape=jax.ShapeDtypeStruct((2, DFT2, n1, c), F32),
        grid=(DFT2 // tj,),
        in_specs=[pl.BlockSpec((2, n1 // 2, tj, c), lambda j: (0, 0, j, 0)),
                  pl.BlockSpec((tj, 2 * n1, n1), lambda j: (j, 0, 0))],
        out_specs=pl.BlockSpec((2, tj, n1, c), lambda j: (0, j, 0, 0)),
        compiler_params=_cparams(("parallel",), 48),
        name="fft_stage_a",
    )(x4, m)


def _fft_b_body(a_ref, mf_ref, mi_ref, g_ref, o_ref, *, kg):
    for kk in range(kg):
        x = jnp.concatenate([a_ref[0, :, kk, :], a_ref[1, :, kk, :]], axis=0).astype(BF16)
        xf = jnp.dot(mf_ref[...], x, preferred_element_type=F32)
        xr, xi = xf[:DFT2], xf[DFT2:]
        gr, gi = g_ref[0, kk].astype(F32), g_ref[1, kk].astype(F32)
        y = jnp.concatenate([xr * gr - xi * gi, xr * gi + xi * gr], axis=0).astype(BF16)
        bf = jnp.dot(mi_ref[...], y, preferred_element_type=F32)
        o_ref[0, kk] = bf[:DFT2]
        o_ref[1, kk] = bf[DFT2:]


def _fft_b(a, g, order, tabs, *, n1):
    c = a.shape[3]
    kg = FFT_GROUP
    blk = pl.BlockSpec((2, kg, DFT2, c), lambda k: (0, k, 0, 0))
    mat = pl.BlockSpec((2 * DFT2, 2 * DFT2), lambda k: (0, 0))
    return pl.pallas_call(
        functools.partial(_fft_b_body, kg=kg),
        out_shape=jax.ShapeDtypeStruct((2, n1, DFT2, c), F32),
        grid=(n1 // kg,),
        in_specs=[pl.BlockSpec((2, DFT2, kg, c), lambda k: (0, 0, k, 0)), mat, mat,
                  pl.BlockSpec((2, kg, DFT2, c), lambda k: (0, k, 0, order))],
        out_specs=blk,
        compiler_params=_cparams(("parallel",), 48),
        name="fft_stage_b",
    )(a, tabs["m_fwd"], tabs["m_inv"], g)


def _fft_bf_body(a_ref, mf_ref, o_ref, *, kg):
    for kk in range(kg):
        x = jnp.concatenate([a_ref[0, :, kk, :], a_ref[1, :, kk, :]], axis=0).astype(BF16)
        xf = jnp.dot(mf_ref[...], x, preferred_element_type=F32)
        o_ref[0, kk] = xf[:DFT2].astype(BF16)
        o_ref[1, kk] = xf[DFT2:].astype(BF16)


def _fft_b_forward(a, tabs, *, n1):
    c = a.shape[3]
    kg = FFT_GROUP
    return pl.pallas_call(
        functools.partial(_fft_bf_body, kg=kg),
        out_shape=jax.ShapeDtypeStruct((2, n1, DFT2, c), BF16),
        grid=(n1 // kg,),
        in_specs=[pl.BlockSpec((2, DFT2, kg, c), lambda k: (0, 0, k, 0)),
                  pl.BlockSpec((2 * DFT2, 2 * DFT2), lambda k: (0, 0))],
        out_specs=pl.BlockSpec((2, kg, DFT2, c), lambda k: (0, k, 0, 0)),
        compiler_params=_cparams(("parallel",), 48),
        name="fft_stage_b_filter",
    )(a, tabs["m_fwd"])


def _fft_ai_body(b_ref, m_ref, v_ref, x_ref, bias_ref, o_ref, *, tj):
    h = o_ref.shape[1]
    for e in range(tj):
        b = jnp.concatenate([b_ref[0, :, e, :], b_ref[1, :, e, :]], axis=0).astype(BF16)
        y = jnp.dot(m_ref[e], b, preferred_element_type=F32)
        o_ref[0, :, e, :] = y[:h]
        o_ref[1, :, e, :] = y[h:]
    o_ref[...] = x_ref[...] * (o_ref[...] + v_ref[...] * bias_ref[...])


def _fft_a_inv(b, m, v4, xm4, bias, *, n1, tj):
    c = b.shape[3]
    half = pl.BlockSpec((2, n1 // 2, tj, c), lambda j: (0, 0, j, 0))
    return pl.pallas_call(
        functools.partial(_fft_ai_body, tj=tj),
        out_shape=jax.ShapeDtypeStruct((2, n1 // 2, DFT2, c), F32),
        grid=(DFT2 // tj,),
        in_specs=[pl.BlockSpec((2, n1, tj, c), lambda j: (0, 0, j, 0)),
                  pl.BlockSpec((tj, n1, 2 * n1), lambda j: (j, 0, 0)),
                  half, half, pl.BlockSpec((1, c), lambda j: (0, 0))],
        out_specs=half,
        compiler_params=_cparams(("parallel",), 48),
        name="fft_stage_a_inv",
    )(b, m, v4, xm4, bias)


def _filt_a_body(hj_ref, hz_ref, hr_ref, s_ref, mf_ref, mr_ref, o_ref, *, tj):
    n1 = o_ref.shape[2]
    s = s_ref[...]
    for e in range(tj):
        src = hz_ref if e == 0 else hr_ref
        me = 0 if e == 0 else tj - e
        for o in range(2):
            c0 = 2 * D_GROUP * o
            inv = 1.0 / (s[:, c0:c0 + D_GROUP] + s[:, c0 + D_GROUP:c0 + 2 * D_GROUP] + EPS)
            lsl = slice(o * D_GROUP, (o + 1) * D_GROUP)
            hf = hj_ref[e, :, lsl].astype(BF16)
            hb = src[me, :, lsl].astype(BF16)
            a = (jnp.dot(mf_ref[e], hf, preferred_element_type=F32)
                 + jnp.dot(mr_ref[e], hb, preferred_element_type=F32)) * inv
            o_ref[0, e, :, lsl] = a[:n1]
            o_ref[1, e, :, lsl] = a[n1:]


def _filt_a(hf, hb, colsum, tabs, *, n1):
    tj = FFT_GROUP
    nj = DFT2 // tj
    wd = 2 * D_GROUP
    shp = (DFT2, n1 // 2, wd)
    blk = lambda fn: pl.BlockSpec((tj, n1 // 2, wd), fn)
    mat = pl.BlockSpec((tj, 2 * n1, n1 // 2), lambda j: (j, 0, 0))
    return pl.pallas_call(
        functools.partial(_filt_a_body, tj=tj),
        out_shape=jax.ShapeDtypeStruct((2, DFT2, n1, wd), F32),
        grid=(nj,),
        in_specs=[blk(lambda j: (j, 0, 0)), blk(lambda j: ((nj - j) % nj, 0, 0)), blk(lambda j: (nj - 1 - j, 0, 0)),
                  pl.BlockSpec((1, HY_FILTER_CH), lambda j: (0, 0)), mat, mat],
        out_specs=pl.BlockSpec((2, tj, n1, wd), lambda j: (0, j, 0, 0)),
        compiler_params=_cparams(("parallel",), 48),
        name="filter_stage_a",
    )(hf.reshape(shp), hb.reshape(shp), hb.reshape(shp), colsum, tabs["a_real"], tabs["a_rev"])


def _hyena_long(v, x1, x2, hf, hb, colsum, bias, *, seq):
    n1 = 2 * seq // DFT2
    tabs = {k: jnp.asarray(a).astype(BF16) for k, a in _dft_tables(n1).items()}
    tj = FFT_GROUP
    fa = _filt_a(hf, hb, colsum, tabs, n1=n1)
    g = _fft_b_forward(fa, tabs, n1=n1)
    shp = (2, n1 // 2, DFT2, D_GROUP)
    z = v.reshape(shp)
    for order, xm in ((0, x1.reshape(shp)), (1, x2.reshape(shp))):
        a = _fft_a(z, tabs["a_fwd"], n1=n1, tj=tj)
        b = _fft_b(a, g, order, tabs, n1=n1)
        z = _fft_a_inv(b, tabs["a_inv"], z, xm, bias[order].reshape(1, D_GROUP), n1=n1, tj=tj)
    return z.reshape(2 * seq, D_GROUP)


def _hyena_ctx_body(v_ref, x1_ref, x2_ref, hf_ref, hb_ref, s_ref, bias_ref, cm_ref, sm_ref, ct_ref, st_ref, o_ref):
    n = v_ref.shape[1]
    cm, sm, ct, st = cm_ref[...], sm_ref[...], ct_ref[...], st_ref[...]
    s = s_ref[...]
    row0 = _iota((n, D_GROUP), 0) == 0
    dot = lambda a, b: jnp.dot(a, b.astype(BF16), preferred_element_type=F32)
    zr, zi = v_ref[0], v_ref[1]
    for order, xm in ((0, x1_ref), (1, x2_ref)):
        c0 = 2 * D_GROUP * order
        inv = 1.0 / (s[:, c0:c0 + D_GROUP] + s[:, c0 + D_GROUP:c0 + 2 * D_GROUP] + EPS)
        lsl = slice(order * D_GROUP, (order + 1) * D_GROUP)
        hf = hf_ref[:, lsl] * inv
        hb = jnp.where(row0, 0.0, hb_ref[:, lsl] * inv)
        gr = dot(cm, hf + hb)
        gi = dot(sm, hb - hf)
        xr = dot(cm, zr) + dot(sm, zi)
        xi = dot(cm, zi) - dot(sm, zr)
        yr = xr * gr - xi * gi
        yi = xr * gi + xi * gr
        cr = dot(ct, yr) - dot(st, yi)
        ci = dot(ct, yi) + dot(st, yr)
        bias = bias_ref[order:order + 1, :]
        zr = xm[0] * (cr + zr * bias)
        zi = xm[1] * (ci + zi * bias)
    o_ref[0] = zr
    o_ref[1] = zi


def _hyena_ctx(v, x1, x2, hf, hb, colsum, bias, *, seq):
    nn = 2 * seq
    k = np.arange(nn)
    t = np.arange(seq)
    th = 2.0 * np.pi * (np.outer(k, t) % nn) / nn
    cm, sm = np.cos(th), np.sin(th)
    consts = [jnp.asarray(a, F32).astype(BF16) for a in (cm, sm, cm.T / nn, sm.T / nn)]
    shp = (2, seq, D_GROUP)
    full3 = pl.BlockSpec(shp, lambda i: (0, 0, 0))
    f2 = lambda a: pl.BlockSpec(a.shape, lambda i: (0, 0))
    args = [hf, hb, colsum, bias] + consts
    out = pl.pallas_call(
        _hyena_ctx_body,
        out_shape=jax.ShapeDtypeStruct(shp, F32),
        grid=(1,),
        in_specs=[full3, full3, full3] + [f2(a) for a in args],
        out_specs=full3,
        compiler_params=_cparams(("arbitrary",)),
        name="hyena_ctx",
    )(v.reshape(shp), x1.reshape(shp), x2.reshape(shp), *args)
    return out.reshape(2 * seq, D_GROUP)


def _split_w_in(w_in):
    wb = w_in.astype(BF16)
    parts = dict(kq=jnp.concatenate([wb[:, COL_K:COL_V], wb[:, COL_Q:COL_R]], axis=1), v=wb[:, COL_V:COL_GF],
                 r=wb[:, COL_R:COL_POOL], pool=wb[:, COL_POOL:COL_HY], hy=wb[:, COL_HY:COL_CONV],
                 conv=wb[:, COL_CONV:P_IN])
    gate = jnp.pad(w_in[:, COL_GF:COL_Q], ((0, 0), (0, LANE - 2 * GLA_LOWRANK)))
    return parts, gate


def _mix(u, p, *, bsz, seq, is_ctx, states, filt):
    sf, sb = states
    cpb = min(8, seq // CHUNK)
    if is_ctx:
        pool = _pool1d(u["pool"], p["pool_wbd"], p["pool_scale"], bsz=bsz, seq=seq)
    else:
        pool = _pool2d(u["pool"], p["pool_wbd"], p["pool_scale"], bsz=bsz, seq=seq)
    tmc = min(seq, 1024)
    v, x1, x2 = _hy_short(u["hy"], p["hy_short_w"], p["hy_short_b"], seq=seq, tm=tmc)
    if is_ctx:
        hy = _hyena_ctx(v, x1, x2, *filt, p["hy_bias"], seq=seq)
    else:
        hy = _hyena_long(v, x1, x2, *filt, p["hy_bias"], seq=seq)
    gla = _gla_read(u["kq"], u["v"], u["kq"], u["r"], u["lf"], u["lb"], sf, sb, p["gla_ng"],
                    bsz=bsz, seq=seq, cpb=cpb)
    conv = _conformer(u["conv"], p["conv_dw_w"], p["conv_dw_b"], p["conv_ln_g"], p["conv_ln_b"], seq=seq, tm=tmc)
    return [pool, hy, gla, conv]


def kernel(x, c, ctx, c_ctx, ada_w, ada_b, ffn1_norm, ffn1_wi, ffn1_wo, mix_norm, w_in, w_out, pool_w, pool_scale, hy_short_w, hy_short_b, hy_w1, hy_b1, hy_w2, hy_b2, hy_w3, hy_deltas, hy_bias, gla_gw_f, gla_gb_f, gla_gw_b, gla_gb_b, gla_norm, conv_dw_w, conv_dw_b, conv_ln_g, conv_ln_b, ffn2_norm, ffn2_wi, ffn2_wo, final_norm):
    bsz, seq, d = x.shape
    clen = ctx.shape[1]
    depth = ada_w.shape[0]
    assert bsz == 2, "the Hyena transform packs exactly two batch rows into one complex signal"
    xs = x.reshape(bsz * seq, d)
    cs = ctx.reshape(bsz * clen, d)
    cc = jnp.concatenate([c, c_ctx[None, :], jnp.zeros((8 - bsz - 1, d), F32)], axis=0)
    mod = _modulation(cc, ada_w, ada_b).reshape(depth, 8, N_MOD, d)
    crow = bsz
    head_eye = np.kron(np.eye(len(POOL_WINDOWS), dtype=np.float32), np.ones((POOL_CH, POOL_CH), np.float32))
    tmx = 512
    tps_x = seq // tmx
    tmc = bsz * clen
    zero_state = jnp.zeros((bsz, GLA_HEADS * GLA_DV, GLA_QK), F32)
    wi1, wo1, wi2, wo2, wout = (w.astype(BF16) for w in (ffn1_wi, ffn1_wo, ffn2_wi, ffn2_wo, w_out))

    for l in range(depth):
        last = l == depth - 1
        wparts, wgate = _split_w_in(w_in[l])
        zlow = jnp.zeros((GLA_LOWRANK, GLA_QK), F32)
        gw = jnp.concatenate([jnp.concatenate([gla_gw_f[l], zlow], axis=1),
                              jnp.concatenate([zlow, gla_gw_b[l]], axis=1),
                              jnp.zeros((LANE - 2 * GLA_LOWRANK, 2 * GLA_QK), F32)], axis=0)
        gb = jnp.concatenate([gla_gb_f[l], gla_gb_b[l]]).reshape(1, -1)
        gw = _gate_fold(wgate, gw).astype(BF16)
        wbd = jnp.tile(pool_w[l].reshape(D_GROUP, POOL_CH), (1, len(POOL_WINDOWS))) * head_eye
        p = dict(pool_wbd=wbd.astype(BF16), pool_scale=pool_scale[l].reshape(1, -1),
                 hy_short_w=hy_short_w[l], hy_short_b=hy_short_b[l], hy_bias=hy_bias[l],
                 gla_ng=jnp.tile(gla_norm[l], GLA_HEADS).reshape(1, -1),
                 conv_dw_w=conv_dw_w[l], conv_dw_b=conv_dw_b[l], conv_ln_g=conv_ln_g[l], conv_ln_b=conv_ln_b[l])
        names = ["kq", "v", "r", "pool", "hy", "conv"]

        xs, *outs = _ffn_proj(xs, mod, ffn1_norm[l], wi1, wo1, mix_norm[l], [wparts[nm] for nm in names], gw, gb,
                              layer=l, row0=0, tm=tmx, tiles_per_seq=tps_x)
        ux = dict(zip(names + ["lf", "lb"], outs))
        cnames = ["kq", "v"] if last else names
        cs, *outs = _ffn_proj(cs, mod, ffn1_norm[l], wi1, wo1, mix_norm[l], [wparts[nm] for nm in cnames], gw, gb,
                              layer=l, row0=crow, tm=tmc, tiles_per_seq=1)
        uc = dict(zip(cnames + ["lf", "lb"], outs))

        ccpb = clen // CHUNK
        sfc, sbc, finf, finb = _gla_states(uc["kq"], uc["v"], uc["lf"], uc["lb"], zero_state, zero_state,
                                           bsz=bsz, seq=clen, cpb=ccpb)
        sfx, sbx, _, _ = _gla_states(ux["kq"], ux["v"], ux["lf"], ux["lb"], finf, finb, bsz=bsz, seq=seq,
                                     cpb=16)

        filt = _hy_filter(seq, hy_w1[l], hy_b1[l], hy_w2[l], hy_b2[l], hy_w3[l], hy_deltas[l], t2_major=True)
        mixed = _mix(ux, p, bsz=bsz, seq=seq, is_ctx=False, states=(sfx, sbx), filt=filt)
        xs = _out_ffn(xs, mod, mixed, wout, ffn2_norm[l], wi2, wo2, layer=l, row0=0, tm=tmx,
                      tiles_per_seq=tps_x, final_g=final_norm if last else None)
        if not last:
            filt_c = _hy_filter(clen, hy_w1[l], hy_b1[l], hy_w2[l], hy_b2[l], hy_w3[l], hy_deltas[l],
                                t2_major=False)
            mixed = _mix(uc, p, bsz=bsz, seq=clen, is_ctx=True, states=(sfc, sbc), filt=filt_c)
            cs = _out_ffn(cs, mod, mixed, wout, ffn2_norm[l], wi2, wo2, layer=l, row0=crow, tm=tmc,
                          tiles_per_seq=1)
    return xs.reshape(bsz, seq, d)
```

```python
import functools
import math

import numpy as np
import jax
import jax.numpy as jnp
from jax import lax
from jax.experimental import pallas as pl
from jax.experimental.pallas import tpu as pltpu

F32 = jnp.float32
BF16 = jnp.bfloat16
HI = lax.Precision.HIGHEST

D_MODEL = 1024
GRID_W = 64
D_GROUP = 256
D_FF = 2816
N_MOD = 9
EPS = 1e-6
POOL_WINDOWS = (2, 4, 8, 16)
POOL_CH = 64
HY_BANDS = 16
HY_EMB = 1 + 2 * HY_BANDS
HY_FFN = 64
HY_FILTER_CH = 4 * D_GROUP
GLA_HEADS = 4
GLA_DK = 32
GLA_DV = 64
GLA_QK = 128
GLA_LOWRANK = 16
GLA_TAU = 16.0
CHUNK = 64
CONV_WIDTH = 31
HY_SHORT = 3

COL_K = 0
COL_V = COL_K + GLA_QK
COL_GF = COL_V + D_GROUP
COL_GB = COL_GF + GLA_LOWRANK
COL_Q = COL_GB + GLA_LOWRANK
COL_R = COL_Q + GLA_QK
COL_POOL = COL_R + D_GROUP
COL_HY = COL_POOL + D_GROUP
COL_CONV = COL_HY + 3 * D_GROUP
P_IN = COL_CONV + 2 * D_GROUP

LANE = 128
DFT2 = 128
FFT_GROUP = 16
FFT_GROUP_DATA = 32
MIB = 1024 * 1024


def _cparams(sem, vmem_mib=None):
    kw = dict(dimension_semantics=sem)
    if vmem_mib is not None:
        kw["vmem_limit_bytes"] = vmem_mib * MIB
    return pltpu.CompilerParams(**kw)


def _silu(x):
    return x * jax.nn.sigmoid(x)


def _rms_mod(h, g, m):
    y = h * lax.rsqrt(jnp.mean(h * h, axis=-1, keepdims=True) + EPS) * g
    return y * (1.0 + m[1:2, :]) + m[0:1, :]


def _iota(shape, dim):
    return lax.broadcasted_iota(jnp.int32, shape, dim)


MOD_ROWS = 3


def _mod_body(at_ref, w_ref, b_ref, o_ref):
    d, tn = w_ref.shape[1], w_ref.shape[2]
    at = _silu(at_ref[...])
    cols = [jnp.broadcast_to(at[:, r:r + 1], (d, LANE)).reshape(d // SUBLANES, SUBLANES, LANE)
            for r in range(MOD_ROWS)]
    o_ref[0] = jnp.zeros((8, tn), F32) + b_ref[0]
    for j in range(tn // LANE):
        lsl = slice(j * LANE, (j + 1) * LANE)
        w3 = w_ref[0, :, lsl].reshape(d // SUBLANES, SUBLANES, LANE)
        for r in range(MOD_ROWS):
            part = jnp.sum(w3 * cols[r], axis=0)
            o_ref[0, r:r + 1, lsl] += jnp.sum(part, axis=0, keepdims=True)


def _modulation(cc, ada_w, ada_b):
    nl, d, nm = ada_w.shape
    tn = 2304
    return pl.pallas_call(
        _mod_body,
        out_shape=jax.ShapeDtypeStruct((nl, 8, nm), F32),
        grid=(nl, nm // tn),
        in_specs=[pl.BlockSpec((d, 8), lambda l, j: (0, 0)),
                  pl.BlockSpec((1, d, tn), lambda l, j: (l, 0, j)),
                  pl.BlockSpec((1, 1, tn), lambda l, j: (l, 0, j))],
        out_specs=pl.BlockSpec((1, 8, tn), lambda l, j: (l, 0, j)),
        compiler_params=_cparams(("parallel", "parallel"), 40),
        name="adaln_mod",
    )(cc.T, ada_w, ada_b.reshape(nl, 1, nm))


FF_CHUNK = 256
ROW_SUB = 256


def _swiglu(xn, wi_ref, wo_ref, hm_ref, rows):
    ff = wo_ref.shape[0]
    for c in range(0, ff, FF_CHUNK):
        a = jnp.dot(xn, wi_ref[:, c:c + FF_CHUNK], preferred_element_type=F32)
        g = jnp.dot(xn, wi_ref[:, ff + c:ff + c + FF_CHUNK], preferred_element_type=F32)
        hm_ref[rows, c:c + FF_CHUNK] = (_silu(g) * a).astype(BF16)
    return jnp.dot(hm_ref[rows, :], wo_ref[...], preferred_element_type=F32)


def _resident(shape):
    nd = len(shape)
    return pl.BlockSpec(shape, lambda i: (0,) * nd, pipeline_mode=pl.Buffered(1))


def _resident_layer(shape, layer):
    nd = len(shape)
    return pl.BlockSpec((None,) + tuple(shape), lambda i: (layer,) + (0,) * nd, pipeline_mode=pl.Buffered(1))


def _gate_fold_body(wg_ref, gw_ref, o_ref):
    o_ref[...] = jnp.dot(wg_ref[...], gw_ref[...], precision=HI, preferred_element_type=F32)


def _gate_fold(w_gate, gw):
    d = w_gate.shape[0]
    return pl.pallas_call(
        _gate_fold_body,
        out_shape=jax.ShapeDtypeStruct((d, gw.shape[1]), F32),
        grid=(1,),
        in_specs=[pl.BlockSpec(w_gate.shape, lambda i: (0, 0)), pl.BlockSpec(gw.shape, lambda i: (0, 0))],
        out_specs=pl.BlockSpec((d, gw.shape[1]), lambda i: (0, 0)),
        name="gate_fold",
    )(w_gate, gw)


def _chunk_prefix(x):
    pos = _iota(x.shape, 0) & (CHUNK - 1)
    shift = 1
    while shift < CHUNK:
        x = x + jnp.where(pos >= shift, pltpu.roll(x, shift, 0), 0.0)
        shift *= 2
    return x


def _ffn_proj_body(h_ref, m_ref, g1_ref, wi_ref, wo_ref, g2_ref, *rest, nparts):
    w_refs = rest[:nparts]
    gw_ref, gb_ref = rest[nparts:nparts + 2]
    x_ref = rest[nparts + 2]
    o_refs = rest[nparts + 3:2 * nparts + 5]
    hm_ref = rest[-1]
    m = m_ref[0]
    for r0 in range(0, h_ref.shape[0], ROW_SUB):
        rows = slice(r0, r0 + ROW_SUB)
        h = h_ref[rows, :]
        xn = _rms_mod(h, g1_ref[...], m[0:2]).astype(BF16)
        x1 = h + (0.5 * m[2:3, :]) * _swiglu(xn, wi_ref, wo_ref, hm_ref, rows)
        x_ref[rows, :] = x1
        xn2 = _rms_mod(x1, g2_ref[...], m[3:5]).astype(BF16)
        for w_ref, o_ref in zip(w_refs, o_refs[:nparts]):
            o_ref[rows, :] = jnp.dot(xn2, w_ref[...], preferred_element_type=F32).astype(o_ref.dtype)
        a = jnp.dot(xn2, gw_ref[...], preferred_element_type=F32) + gb_ref[...]
        ls = (jnp.minimum(a, 0.0) - jnp.log(1.0 + jnp.exp(-jnp.abs(a)))) * (1.0 / GLA_TAU)
        pre = _chunk_prefix(ls)
        o_refs[nparts][rows, :] = pre[:, :GLA_QK]
        pb = pre[:, GLA_QK:]
        tot = jnp.concatenate([jnp.broadcast_to(pb[c0 + CHUNK - 1:c0 + CHUNK, :], (CHUNK, GLA_QK))
                               for c0 in range(0, ROW_SUB, CHUNK)], axis=0)
        o_refs[nparts + 1][rows, :] = tot - pb + ls[:, GLA_QK:]


def _mod_spec(d, layer, row0, tiles_per_seq):
    return pl.BlockSpec((None, 1, N_MOD, d), lambda i: (layer, row0 + i // tiles_per_seq, 0, 0))


def _ffn_proj(h, mod, g1, wi, wo, g2, w_parts, gw, gb, *, layer, row0, tm, tiles_per_seq):
    n, d = h.shape
    ff = wo.shape[1]
    nparts = len(w_parts)
    widths = [w.shape[1] for w in w_parts]
    tok = lambda wd: pl.BlockSpec((tm, wd), lambda i: (i, 0))
    in_specs = [tok(d), _mod_spec(d, layer, row0, tiles_per_seq), _resident((1, d)),
                _resident_layer((d, 2 * ff), layer), _resident_layer((ff, d), layer), _resident((1, d))]
    in_specs += [_resident((d, wd)) for wd in widths]
    in_specs += [_resident((d, 2 * GLA_QK)), _resident((1, 2 * GLA_QK))]
    out_shape = [jax.ShapeDtypeStruct((n, d), F32)]
    out_shape += [jax.ShapeDtypeStruct((n, wd), BF16) for wd in widths]
    out_shape += [jax.ShapeDtypeStruct((n, GLA_QK), F32)] * 2
    return pl.pallas_call(
        functools.partial(_ffn_proj_body, nparts=nparts),
        out_shape=out_shape,
        grid=(n // tm,),
        in_specs=in_specs,
        out_specs=[tok(d)] + [tok(wd) for wd in widths] + [tok(GLA_QK)] * 2,
        scratch_shapes=[pltpu.VMEM((tm, ff), BF16)],
        compiler_params=_cparams(("parallel",), 56),
        name="ffn1_in_proj",
    )(h, mod, g1.reshape(1, d), wi, wo, g2.reshape(1, d), *w_parts, gw, gb)


def _out_ffn_body(x_ref, m_ref, p_ref, hy_ref, gl_ref, cv_ref, wout_ref, g_ref, wi_ref, wo_ref, fn_ref, o_ref,
                  hm_ref, *, final):
    m = m_ref[0]
    for r0 in range(0, x_ref.shape[0], ROW_SUB):
        rows = slice(r0, r0 + ROW_SUB)
        acc = jnp.dot(p_ref[rows, :].astype(BF16), wout_ref[0:D_GROUP, :], preferred_element_type=F32)
        acc += jnp.dot(hy_ref[rows, :].astype(BF16), wout_ref[D_GROUP:2 * D_GROUP, :], preferred_element_type=F32)
        acc += jnp.dot(gl_ref[rows, :].astype(BF16), wout_ref[2 * D_GROUP:3 * D_GROUP, :],
                       preferred_element_type=F32)
        acc += jnp.dot(cv_ref[rows, :].astype(BF16), wout_ref[3 * D_GROUP:, :], preferred_element_type=F32)
        x2 = x_ref[rows, :] + m[5:6, :] * acc
        xn = _rms_mod(x2, g_ref[...], m[6:8]).astype(BF16)
        out = x2 + (0.5 * m[8:9, :]) * _swiglu(xn, wi_ref, wo_ref, hm_ref, rows)
        if final:
            out = out * lax.rsqrt(jnp.mean(out * out, axis=-1, keepdims=True) + EPS) * fn_ref[...]
        o_ref[rows, :] = out


def _out_ffn(x, mod, parts, w_out, g, wi, wo, *, layer, row0, tm, tiles_per_seq, final_g=None):
    n, d = x.shape
    ff = wo.shape[1]
    final = final_g is not None
    fg = final_g if final else g
    tok = lambda wd: pl.BlockSpec((tm, wd), lambda i: (i, 0))
    return pl.pallas_call(
        functools.partial(_out_ffn_body, final=final),
        out_shape=jax.ShapeDtypeStruct((n, d), F32),
        grid=(n // tm,),
        in_specs=[tok(d), _mod_spec(d, layer, row0, tiles_per_seq),
                  tok(D_GROUP), tok(D_GROUP), tok(D_GROUP), tok(D_GROUP), _resident_layer((d, d), layer),
                  _resident((1, d)), _resident_layer((d, 2 * ff), layer), _resident_layer((ff, d), layer),
                  _resident((1, d))],
        out_specs=tok(d),
        scratch_shapes=[pltpu.VMEM((tm, ff), BF16)],
        compiler_params=_cparams(("parallel",), 56),
        name="out_proj_ffn2",
    )(x, mod, *parts, w_out, g.reshape(1, d), wi, wo, fg.reshape(1, d))


def _gla_state_body(kf, vf, bf, kb, vb, bb, s0f, s0b, sf_o, sb_o, ff_o, fb_o, stf, stb, *, cpb):
    i = pl.program_id(1)

    @pl.when(i == 0)
    def _():
        stf[...] = s0f[0]
        stb[...] = s0b[0]

    sshape = (GLA_HEADS * GLA_DV, GLA_QK)
    bmask = (_iota(sshape, 0) >> 6) == (_iota(sshape, 1) >> 5)
    tn_dims = (((0,), (0,)), ((), ()))

    def direction(k_ref, v_ref, b_ref, last, st, s_o, order):
        s = st[...]
        for ci in order:
            sl = slice(ci * CHUNK, (ci + 1) * CHUNK)
            b = b_ref[sl, :]
            tot = b[last:last + 1, :]
            kd = (k_ref[sl, :] * jnp.exp(tot - b)).astype(BF16)
            upd = lax.dot_general(v_ref[sl, :].astype(BF16), kd, tn_dims, preferred_element_type=F32)
            s_o[0, ci] = s.astype(BF16)
            s = s * jnp.exp(tot) + jnp.where(bmask, upd, 0.0)
        st[...] = s

    direction(kf, vf, bf, CHUNK - 1, stf, sf_o, range(cpb))
    direction(kb, vb, bb, 0, stb, sb_o, range(cpb - 1, -1, -1))
    ff_o[0] = stf[...]
    fb_o[0] = stb[...]


def _gla_states(k, v, lf, lb, s0f, s0b, *, bsz, seq, cpb):
    bt = cpb * CHUNK
    nb = seq // bt
    nc = seq // CHUNK
    srow = GLA_HEADS * GLA_DV

    def tf(b, i):
        return (b * nb + i, 0)

    def tb(b, i):
        return (b * nb + nb - 1 - i, 0)

    sblk = pl.BlockSpec((1, srow, GLA_QK), lambda b, i: (b, 0, 0))
    return pl.pallas_call(
        functools.partial(_gla_state_body, cpb=cpb),
        out_shape=[jax.ShapeDtypeStruct((bsz, nc, srow, GLA_QK), BF16),
                   jax.ShapeDtypeStruct((bsz, nc, srow, GLA_QK), BF16),
                   jax.ShapeDtypeStruct((bsz, srow, GLA_QK), F32),
                   jax.ShapeDtypeStruct((bsz, srow, GLA_QK), F32)],
        grid=(bsz, nb),
        in_specs=[pl.BlockSpec((bt, GLA_QK), tf), pl.BlockSpec((bt, D_GROUP), tf), pl.BlockSpec((bt, GLA_QK), tf),
                  pl.BlockSpec((bt, GLA_QK), tb), pl.BlockSpec((bt, D_GROUP), tb), pl.BlockSpec((bt, GLA_QK), tb),
                  sblk, sblk],
        out_specs=[pl.BlockSpec((1, cpb, srow, GLA_QK), lambda b, i: (b, i, 0, 0)),
                   pl.BlockSpec((1, cpb, srow, GLA_QK), lambda b, i: (b, nb - 1 - i, 0, 0)),
                   sblk, sblk],
        scratch_shapes=[pltpu.VMEM((srow, GLA_QK), F32), pltpu.VMEM((srow, GLA_QK), F32)],
        compiler_params=_cparams(("parallel", "arbitrary")),
        name="gla_states",
    )(k, v, lf, k, v, lb, s0f, s0b)


def _gla_read_body(k_ref, v_ref, q_ref, r_ref, bf_ref, bb_ref, sf_ref, sb_ref, ng_ref, o_ref, acc_ref, *, cpb):
    hrows = GLA_HEADS * CHUNK
    cpos = _iota((hrows, CHUNK), 0) & (CHUNK - 1)
    ccol = _iota((hrows, CHUNK), 1)
    lowm = cpos >= ccol
    upm = cpos <= ccol
    hq = (_iota((hrows, GLA_QK), 0) >> 6) == (_iota((hrows, GLA_QK), 1) >> 5)
    ho = (_iota((hrows, D_GROUP), 0) >> 6) == (_iota((hrows, D_GROUP), 1) >> 6)
    bavg = jnp.where((_iota((D_GROUP, D_GROUP), 0) >> 6) == (_iota((D_GROUP, D_GROUP), 1) >> 6),
                     1.0 / GLA_DV, 0.0).astype(BF16)
    nt_dims = (((1,), (1,)), ((), ()))

    bf_ = bf_ref[...]
    bb_ = bb_ref[...]
    qs = q_ref[...].astype(F32) * (GLA_DK ** -0.5)
    kk = k_ref[...]
    qef = (qs * jnp.exp(bf_)).astype(BF16)
    qeb = (qs * jnp.exp(bb_)).astype(BF16)
    kef = (kk * jnp.exp(-bf_)).astype(BF16)
    keb = (kk * jnp.exp(-bb_)).astype(BF16)
    vb16 = v_ref[...].astype(BF16)
    zero = jnp.zeros((), BF16)
    for ci in range(cpb):
        sl = slice(ci * CHUNK, (ci + 1) * CHUNK)
        qf4 = jnp.where(hq, jnp.concatenate([qef[sl]] * GLA_HEADS, axis=0), zero)
        qb4 = jnp.where(hq, jnp.concatenate([qeb[sl]] * GLA_HEADS, axis=0), zero)
        af = lax.dot_general(qf4, kef[sl], nt_dims, preferred_element_type=F32)
        ab = lax.dot_general(qb4, keb[sl], nt_dims, preferred_element_type=F32)
        att = (jnp.where(lowm, af, 0.0) + jnp.where(upm, ab, 0.0)).astype(BF16)
        oall = jnp.dot(att, vb16[sl], preferred_element_type=F32)
        om = jnp.where(ho, oall, 0.0)
        o = om[0:CHUNK] + om[CHUNK:2 * CHUNK] + om[2 * CHUNK:3 * CHUNK] + om[3 * CHUNK:4 * CHUNK]
        qcat = jnp.concatenate([qef[sl], qeb[sl]], axis=1)
        scat = jnp.concatenate([sf_ref[0, ci], sb_ref[0, ci]], axis=1)
        acc_ref[sl, :] = o + lax.dot_general(qcat, scat, nt_dims, preferred_element_type=F32)
    o = acc_ref[...]
    ms = _chunk_sum_rhs(o * o, bavg)
    o_ref[...] = (o * lax.rsqrt(ms + EPS) * ng_ref[...] * _silu(r_ref[...].astype(F32))).astype(o_ref.dtype)


def _chunk_sum_rhs(x, mat):
    hi, lo = _split_bf16(x)
    return jnp.dot(hi, mat, preferred_element_type=F32) + jnp.dot(lo, mat, preferred_element_type=F32)


def _gla_read(k, v, q, r, lf, lb, sf, sb, ng, *, bsz, seq, cpb):
    bt = cpb * CHUNK
    nb = seq // bt
    srow = GLA_HEADS * GLA_DV
    n = bsz * seq

    def tk(i):
        return (i, 0)

    sspec = pl.BlockSpec((1, cpb, srow, GLA_QK), lambda i: (i // nb, i % nb, 0, 0))
    return pl.pallas_call(
        functools.partial(_gla_read_body, cpb=cpb),
        out_shape=jax.ShapeDtypeStruct((n, D_GROUP), BF16),
        scratch_shapes=[pltpu.VMEM((bt, D_GROUP), F32)],
        grid=(bsz * nb,),
        in_specs=[pl.BlockSpec((bt, GLA_QK), tk), pl.BlockSpec((bt, D_GROUP), tk),
                  pl.BlockSpec((bt, GLA_QK), lambda i: (i, 1)), pl.BlockSpec((bt, D_GROUP), tk),
                  pl.BlockSpec((bt, GLA_QK), tk), pl.BlockSpec((bt, GLA_QK), tk),
                  sspec, sspec, pl.BlockSpec((1, D_GROUP), lambda i: (0, 0))],
        out_specs=pl.BlockSpec((bt, D_GROUP), tk),
        compiler_params=_cparams(("parallel",)),
        name="gla_readout",
    )(k, v, q, r, lf, lb, sf, sb, ng)


def _box_matrix(n, w):
    pos = np.arange(n)
    lo = np.clip(pos - w // 2, 0, n)
    hi = np.clip(pos - w // 2 + w, 0, n)
    col = np.arange(n)[None, :]
    return ((col >= lo[:, None]) & (col < hi[:, None])).astype(np.float32)


def _lane_windows(shape):
    w = jnp.left_shift(2, _iota(shape, 1) >> 6)
    return w, w >> 1


def _box_count(pos, w, half, n):
    return jnp.minimum(pos - half + w, n) - jnp.maximum(pos - half, 0)


def _split_bf16(x):
    hi = x.astype(BF16)
    lo = (x - hi.astype(F32)).astype(BF16)
    return hi, lo


def _pool2d_body(cur_ref, prev_ref, next_ref, pc_ref, w_ref, sc_ref, o_ref, ycol, *, tiles, rows):
    i = pl.program_id(1)
    tm = cur_ref.shape[0]
    hb = prev_ref.shape[0]
    sub = 2 * GRID_W
    pflag = jnp.where(i > 0, 1.0, 0.0)
    nflag = jnp.where(i < tiles - 1, 1.0, 0.0)

    def colpool(x):
        halves = []
        for half in range(2):
            lsl = slice(half * LANE, (half + 1) * LANE)
            ys = [jnp.dot(pc_ref[wi], x[:, lsl], preferred_element_type=F32) for wi in (2 * half, 2 * half + 1)]
            lane = _iota((sub, LANE), 1)
            halves.append(jnp.where(lane < POOL_CH, ys[0], ys[1]))
        return jnp.concatenate(halves, axis=1)

    for s in range(hb // sub):
        ycol[s * sub:(s + 1) * sub, :] = colpool(prev_ref[s * sub:(s + 1) * sub, :]) * pflag
    for s in range(tm // sub):
        ycol[hb + s * sub:hb + (s + 1) * sub, :] = colpool(cur_ref[s * sub:(s + 1) * sub, :])
    for s in range(hb // sub):
        ycol[hb + tm + s * sub:hb + tm + (s + 1) * sub, :] = colpool(next_ref[s * sub:(s + 1) * sub, :]) * nflag

    rc = 256
    wl, half = _lane_windows((rc, D_GROUP))
    narrow = _iota((rc, LANE), 1) < POOL_CH
    for r0 in range(0, tm, rc):
        def band(lo, hi, lsl):
            base = hb + r0
            acc = ycol[base + GRID_W * lo:base + GRID_W * lo + rc, lsl]
            for dd in range(lo + 1, hi):
                acc = acc + ycol[base + GRID_W * dd:base + GRID_W * dd + rc, lsl]
            return acc

        left, right = slice(0, LANE), slice(LANE, 2 * LANE)
        z2 = band(-1, 1, left)
        z4 = z2 + band(-2, -1, left) + band(1, 2, left)
        z8 = band(-4, 4, right)
        z16 = z8 + band(-8, -4, right) + band(4, 8, right)
        z = jnp.concatenate([jnp.where(narrow, z2, z4), jnp.where(narrow, z8, z16)], axis=1)
        tok = _iota((rc, D_GROUP), 0) + (i * tm + r0)
        rcnt = _box_count(tok >> 6, wl, half, rows)
        ccnt = _box_count(tok & (GRID_W - 1), wl, half, GRID_W)
        pooled = z / (rcnt * ccnt).astype(F32)
        dlt = (pooled - cur_ref[r0:r0 + rc, :]).astype(BF16)
        o_ref[r0:r0 + rc, :] = (jnp.dot(dlt, w_ref[...], preferred_element_type=F32) * sc_ref[...]).astype(o_ref.dtype)


def _pool2d(u, wbd, scale, *, bsz, seq):
    tm = 1024
    hb = 512
    tiles = seq // tm
    r = tm // hb
    nhb = seq // hb
    pc = np.stack([np.kron(np.eye(2, dtype=np.float32), _box_matrix(GRID_W, w)) for w in POOL_WINDOWS])
    return pl.pallas_call(
        functools.partial(_pool2d_body, tiles=tiles, rows=seq // GRID_W),
        out_shape=jax.ShapeDtypeStruct((bsz * seq, D_GROUP), BF16),
        grid=(bsz, tiles),
        in_specs=[pl.BlockSpec((tm, D_GROUP), lambda b, i: (b * tiles + i, 0)),
                  pl.BlockSpec((hb, D_GROUP), lambda b, i: (b * nhb + jnp.maximum(i * r - 1, 0), 0)),
                  pl.BlockSpec((hb, D_GROUP), lambda b, i: (b * nhb + jnp.minimum(i * r + r, nhb - 1), 0)),
                  pl.BlockSpec((4, 2 * GRID_W, 2 * GRID_W), lambda b, i: (0, 0, 0)),
                  pl.BlockSpec((D_GROUP, D_GROUP), lambda b, i: (0, 0)),
                  pl.BlockSpec((1, D_GROUP), lambda b, i: (0, 0))],
        out_specs=pl.BlockSpec((tm, D_GROUP), lambda b, i: (b * tiles + i, 0)),
        scratch_shapes=[pltpu.VMEM((tm + 2 * hb, D_GROUP), F32)],
        compiler_params=_cparams(("parallel", "parallel")),
        name="pool2d",
    )(u, u, u, jnp.asarray(pc, BF16), wbd, scale)


def _pool1d_body(x_ref, p_ref, w_ref, sc_ref, o_ref):
    x = x_ref[...]
    n = x.shape[0]
    ys = [jnp.dot(p_ref[wi], x, preferred_element_type=F32) for wi in range(4)]
    wl, half = _lane_windows((n, D_GROUP))
    z = jnp.where(wl == 2, ys[0], jnp.where(wl == 4, ys[1], jnp.where(wl == 8, ys[2], ys[3])))
    cnt = _box_count(_iota((n, D_GROUP), 0), wl, half, n)
    dlt = (z / cnt.astype(F32) - x).astype(BF16)
    o_ref[...] = (jnp.dot(dlt, w_ref[...], preferred_element_type=F32) * sc_ref[...]).astype(o_ref.dtype)


def _pool1d(u, wbd, scale, *, bsz, seq):
    pm = np.stack([_box_matrix(seq, w) for w in POOL_WINDOWS])
    return pl.pallas_call(
        _pool1d_body,
        out_shape=jax.ShapeDtypeStruct((bsz * seq, D_GROUP), BF16),
        grid=(bsz,),
        in_specs=[pl.BlockSpec((seq, D_GROUP), lambda b: (b, 0)),
                  pl.BlockSpec((4, seq, seq), lambda b: (0, 0, 0)),
                  pl.BlockSpec((D_GROUP, D_GROUP), lambda b: (0, 0)),
                  pl.BlockSpec((1, D_GROUP), lambda b: (0, 0))],
        out_specs=pl.BlockSpec((seq, D_GROUP), lambda b: (b, 0)),
        compiler_params=_cparams(("parallel",)),
        name="pool1d",
    )(u, jnp.asarray(pm, BF16), wbd, scale)


def _fill_halo(buf, cur, prev, nxt, i, tps, hb, tm, pre):
    first = (i % tps) == 0
    last = (i % tps) == tps - 1
    buf[0:hb, :] = jnp.where(first, 0.0, pre(prev[...]))
    buf[hb:hb + tm, :] = pre(cur[...])
    buf[hb + tm:hb + tm + hb, :] = jnp.where(last, 0.0, pre(nxt[...]))


SUBLANES = 8


def _tap_phases(taps, off):
    return sorted({(off + j) % SUBLANES for j in range(taps)} - {0})


def _fill_phases(sh, buf, phases):
    rows = sh.shape[1]
    for slot, s in enumerate(phases):
        sh[slot, :, :] = buf[s:s + rows, :]


def _dwconv(buf, sh, phases, w_ref, r0, rc, taps, off):
    acc = None
    for j in range(taps):
        s, q = (off + j) % SUBLANES, (off + j) // SUBLANES
        lo = r0 + SUBLANES * q
        src = buf[lo:lo + rc, :] if s == 0 else sh[phases.index(s), lo:lo + rc, :]
        term = src * w_ref[j:j + 1, :]
        acc = term if acc is None else acc + term
    return acc


def _conf_body(cur, prev, nxt, w_ref, b_ref, lg_ref, lb_ref, o_ref, buf, sh, *, tps):
    i = pl.program_id(0)
    tm = cur.shape[0]
    hb = prev.shape[0]

    def glu(u):
        u = u.astype(F32)
        return u[:, :D_GROUP] * jax.nn.sigmoid(u[:, D_GROUP:])

    _fill_halo(buf, cur, prev, nxt, i, tps, hb, tm, glu)
    rc = 128
    off = hb - (CONV_WIDTH - 1) // 2
    phases = _tap_phases(CONV_WIDTH, off)
    _fill_phases(sh, buf, phases)
    for r0 in range(0, tm, rc):
        h = _dwconv(buf, sh, phases, w_ref, r0, rc, CONV_WIDTH, off) + b_ref[...]
        mu = jnp.mean(h, axis=-1, keepdims=True)
        hc = h - mu
        var = jnp.mean(hc * hc, axis=-1, keepdims=True)
        o_ref[r0:r0 + rc, :] = _silu(hc * lax.rsqrt(var + EPS) * lg_ref[...] + lb_ref[...]).astype(o_ref.dtype)


def _halo_specs(tm, hb, width, nrows):
    r = tm // hb
    nhb = nrows // hb
    return [pl.BlockSpec((tm, width), lambda i: (i, 0)),
            pl.BlockSpec((hb, width), lambda i: (jnp.maximum(i * r - 1, 0), 0)),
            pl.BlockSpec((hb, width), lambda i: (jnp.minimum(i * r + r, nhb - 1), 0))]


def _conformer(u, w, b, lg, lb, *, seq, tm):
    n = u.shape[0]
    hb = 16
    nph = len(_tap_phases(CONV_WIDTH, hb - (CONV_WIDTH - 1) // 2))
    vec = pl.BlockSpec((1, D_GROUP), lambda i: (0, 0))
    return pl.pallas_call(
        functools.partial(_conf_body, tps=seq // tm),
        out_shape=jax.ShapeDtypeStruct((n, D_GROUP), BF16),
        grid=(n // tm,),
        in_specs=_halo_specs(tm, hb, 2 * D_GROUP, n) + [pl.BlockSpec((CONV_WIDTH, D_GROUP), lambda i: (0, 0)),
                                                        vec, vec, vec],
        out_specs=pl.BlockSpec((tm, D_GROUP), lambda i: (i, 0)),
        scratch_shapes=[pltpu.VMEM((tm + 2 * hb, D_GROUP), F32),
                        pltpu.VMEM((nph, tm + 2 * hb - SUBLANES, D_GROUP), F32)],
        compiler_params=_cparams(("parallel",), 48),
        name="conformer_conv",
    )(u, u, u, w, b.reshape(1, -1), lg.reshape(1, -1), lb.reshape(1, -1))


def _short_body(cur, prev, nxt, w_ref, b_ref, v_ref, x1_ref, x2_ref, buf, sh, *, tps):
    i = pl.program_id(0)
    tm = cur.shape[0]
    hb = prev.shape[0]
    _fill_halo(buf, cur, prev, nxt, i, tps, hb, tm, lambda u: u.astype(F32))
    rc = 128
    off = hb - (HY_SHORT - 1) // 2
    phases = _tap_phases(HY_SHORT, off)
    _fill_phases(sh, buf, phases)
    for r0 in range(0, tm, rc):
        uc = _dwconv(buf, sh, phases, w_ref, r0, rc, HY_SHORT, off) + b_ref[...]
        v_ref[r0:r0 + rc, :] = uc[:, :D_GROUP]
        x1_ref[r0:r0 + rc, :] = uc[:, D_GROUP:2 * D_GROUP].astype(x1_ref.dtype)
        x2_ref[r0:r0 + rc, :] = uc[:, 2 * D_GROUP:].astype(x2_ref.dtype)


def _hy_short(u, w, b, *, seq, tm):
    n = u.shape[0]
    hb = 16
    wd = 3 * D_GROUP
    nph = len(_tap_phases(HY_SHORT, hb - (HY_SHORT - 1) // 2))
    ospec = pl.BlockSpec((tm, D_GROUP), lambda i: (i, 0))
    return pl.pallas_call(
        functools.partial(_short_body, tps=seq // tm),
        out_shape=[jax.ShapeDtypeStruct((n, D_GROUP), dt) for dt in (F32, BF16, BF16)],
        grid=(n // tm,),
        in_specs=_halo_specs(tm, hb, wd, n) + [pl.BlockSpec((HY_SHORT, wd), lambda i: (0, 0)),
                                               pl.BlockSpec((1, wd), lambda i: (0, 0))],
        out_specs=[ospec, ospec, ospec],
        scratch_shapes=[pltpu.VMEM((tm + 2 * hb, wd), F32),
                        pltpu.VMEM((nph, tm + 2 * hb - SUBLANES, wd), F32)],
        compiler_params=_cparams(("parallel",), 48),
        name="hyena_short_conv",
    )(u, u, u, w, b.reshape(1, -1))


def _filter_features(n):
    i = np.arange(n, dtype=np.float64)
    t = np.linspace(0.0, 1.0, n, dtype=np.float32).astype(np.float64)
    wpos = ((2.0 * math.pi / n) * np.arange(n, dtype=np.float32)).astype(np.float32)
    bands = np.linspace(1e-4, HY_BANDS - 1, HY_BANDS, dtype=np.float32)
    arg = (bands[None, :] * wpos[:, None]).astype(np.float32).astype(np.float64)
    z = np.zeros((n, 64), np.float32)
    z[:, 0] = t
    z[:, 1:1 + HY_BANDS] = np.cos(arg)
    z[:, 1 + HY_BANDS:HY_EMB] = -np.sin(arg)
    del i
    return z


def _filter_body(z_ref, w1_ref, b1_ref, w2_ref, b2_ref, w3a_ref, w3b_ref, d_ref, hf_ref, hb_ref, s_ref):
    i = pl.program_id(0)
    half = z_ref.shape[0]
    z = z_ref[...]
    h = jnp.sin(jnp.dot(z, w1_ref[...], precision=HI, preferred_element_type=F32) + b1_ref[...])
    h = jnp.sin(jnp.dot(h, w2_ref[...], precision=HI, preferred_element_type=F32) + b2_ref[...])
    absd = jnp.abs(d_ref[...])
    h = h.astype(BF16)
    tot = None
    for part, (w3_ref, tcol) in enumerate(((w3a_ref, 0), (w3b_ref, 64))):
        hp = jnp.dot(h, w3_ref[...], preferred_element_type=F32)
        hp = hp * jnp.exp(-z[:, tcol:tcol + 1] * absd)
        rows = slice(part * half, (part + 1) * half)
        for o in range(2):
            c0 = 2 * D_GROUP * o
            hf_ref[rows, o * D_GROUP:(o + 1) * D_GROUP] = hp[:, c0:c0 + D_GROUP].astype(hf_ref.dtype)
            hb_ref[rows, o * D_GROUP:(o + 1) * D_GROUP] = hp[:, c0 + D_GROUP:c0 + 2 * D_GROUP].astype(hb_ref.dtype)
        part_sum = jnp.sum(jnp.abs(hp), axis=0, keepdims=True)
        tot = part_sum if tot is None else tot + part_sum

    @pl.when(i == 0)
    def _():
        s_ref[...] = jnp.zeros_like(s_ref)

    s_ref[...] += tot


def _hy_filter(n, w1, b1, w2, b2, w3, deltas, *, t2_major):
    tm = min(n, 1024)
    half = tm // 2
    z = _filter_features(n)
    if t2_major:
        z = z.reshape(n // DFT2, DFT2, 64).transpose(1, 0, 2).reshape(n, 64)
    zt = z.reshape(n // tm, 2, half, 64)
    z2 = jnp.asarray(np.concatenate([zt[:, 0], zt[:, 1]], axis=-1).reshape(n // 2, 2 * 64))
    w1p = jnp.zeros((64, HY_FFN), F32).at[:HY_EMB].set(w1)
    zero = jnp.zeros((64, HY_FFN), F32)
    bd = lambda w: jnp.concatenate([jnp.concatenate([w, zero], axis=1), jnp.concatenate([zero, w], axis=1)], axis=0)
    w3 = w3.astype(BF16)
    zero3 = jnp.zeros_like(w3)
    full = lambda shape: pl.BlockSpec(shape, lambda i: (0, 0))
    tok = pl.BlockSpec((tm, 2 * D_GROUP), lambda i: (i, 0))
    return pl.pallas_call(
        _filter_body,
        out_shape=[jax.ShapeDtypeStruct((n, 2 * D_GROUP), BF16), jax.ShapeDtypeStruct((n, 2 * D_GROUP), BF16),
                   jax.ShapeDtypeStruct((1, HY_FILTER_CH), F32)],
        grid=(n // tm,),
        in_specs=[pl.BlockSpec((half, 2 * 64), lambda i: (i, 0)), full((2 * 64, 2 * HY_FFN)), full((1, 2 * HY_FFN)),
                  full((2 * HY_FFN, 2 * HY_FFN)), full((1, 2 * HY_FFN)), full((2 * HY_FFN, HY_FILTER_CH)),
                  full((2 * HY_FFN, HY_FILTER_CH)), full((1, HY_FILTER_CH))],
        out_specs=[tok, tok, full((1, HY_FILTER_CH))],
        compiler_params=_cparams(("arbitrary",), 48),
        name="hyena_filter_mlp",
    )(z2, bd(w1p), jnp.tile(b1.reshape(1, -1), (1, 2)), bd(w2), jnp.tile(b2.reshape(1, -1), (1, 2)),
      jnp.concatenate([w3, zero3], axis=0), jnp.concatenate([zero3, w3], axis=0), deltas.reshape(1, -1))


@functools.lru_cache(maxsize=None)
def _dft_tables(n1):
    nn = n1 * DFT2
    j = np.arange(DFT2)
    th = 2.0 * np.pi * ((np.outer(j, j)) % DFT2) / DFT2
    fr, fi = np.cos(th), -np.sin(th)
    m_fwd = np.block([[fr, -fi], [fi, fr]])
    m_inv = np.block([[fr, fi], [-fi, fr]])
    hh = n1 // 2
    k1 = np.arange(n1)[None, :, None]
    t1 = np.arange(hh)[None, None, :]
    t2 = np.arange(DFT2)[:, None, None]
    ph = 2.0 * np.pi * ((k1 * (DFT2 * t1 + t2)) % nn) / nn
    ar, ai = np.cos(ph), -np.sin(ph)
    a_fwd = np.concatenate([np.concatenate([ar, -ai], axis=2), np.concatenate([ai, ar], axis=2)], axis=1)
    a_real = np.concatenate([ar, ai], axis=1)
    t1r = np.where(t2 >= 1, n1 - 1 - t1, (n1 - t1) % n1)
    phr = 2.0 * np.pi * ((k1 * (DFT2 * t1r + t2)) % nn) / nn
    a_rev = np.concatenate([np.cos(phr), -np.sin(phr)], axis=1)
    a_rev[0, :, 0] = 0.0
    pht = np.transpose(ph, (0, 2, 1))
    cr, ci = np.cos(pht) / nn, np.sin(pht) / nn
    a_inv = np.concatenate([np.concatenate([cr, -ci], axis=2), np.concatenate([ci, cr], axis=2)], axis=1)
    f32 = lambda a: np.ascontiguousarray(a, dtype=np.float32)
    return dict(m_fwd=f32(m_fwd), m_inv=f32(m_inv), a_fwd=f32(a_fwd), a_real=f32(a_real), a_rev=f32(a_rev),
                a_inv=f32(a_inv))


def _fft_a_body(x_ref, m_ref, o_ref, *, tj):
    n1 = o_ref.shape[2]
    for e in range(tj):
        x = jnp.concatenate([x_ref[0, :, e, :], x_ref[1, :, e, :]], axis=0).astype(BF16)
        a = jnp.dot(m_ref[e], x, preferred_element_type=F32)
        o_ref[0, e] = a[:n1]
        o_ref[1, e] = a[n1:]


def _fft_a(x4, m, *, n1, tj):
    c = x4.shape[3]
    return pl.pallas_call(
        functools.partial(_fft_a_body, tj=tj),
        out_shape=jax.ShapeDtypeStruct((2, DFT2, n1, c), F32),
        grid=(DFT2 // tj,),
        in_specs=[pl.BlockSpec((2, n1 // 2, tj, c), lambda j: (0, 0, j, 0)),
                  pl.BlockSpec((tj, 2 * n1, n1), lambda j: (j, 0, 0))],
        out_specs=pl.BlockSpec((2, tj, n1, c), lambda j: (0, j, 0, 0)),
        compiler_params=_cparams(("parallel",), 48),
        name="fft_stage_a",
    )(x4, m)


def _fft_b_body(a_ref, mf_ref, mi_ref, g_ref, o_ref, *, kg):
    for kk in range(kg):
        x = jnp.concatenate([a_ref[0, :, kk, :], a_ref[1, :, kk, :]], axis=0).astype(BF16)
        xf = jnp.dot(mf_ref[...], x, preferred_element_type=F32)
        xr, xi = xf[:DFT2], xf[DFT2:]
        gr, gi = g_ref[0, kk].astype(F32), g_ref[1, kk].astype(F32)
        y = jnp.concatenate([xr * gr - xi * gi, xr * gi + xi * gr], axis=0).astype(BF16)
        bf = jnp.dot(mi_ref[...], y, preferred_element_type=F32)
        o_ref[0, kk] = bf[:DFT2]
        o_ref[1, kk] = bf[DFT2:]


def _fft_b(a, g, order, tabs, *, n1):
    c = a.shape[3]
    kg = min(n1, FFT_GROUP_DATA)
    blk = pl.BlockSpec((2, kg, DFT2, c), lambda k: (0, k, 0, 0))
    mat = pl.BlockSpec((2 * DFT2, 2 * DFT2), lambda k: (0, 0))
    return pl.pallas_call(
        functools.partial(_fft_b_body, kg=kg),
        out_shape=jax.ShapeDtypeStruct((2, n1, DFT2, c), F32),
        grid=(n1 // kg,),
        in_specs=[pl.BlockSpec((2, DFT2, kg, c), lambda k: (0, 0, k, 0)), mat, mat,
                  pl.BlockSpec((2, kg, DFT2, c), lambda k: (0, k, 0, order))],
        out_specs=blk,
        compiler_params=_cparams(("parallel",), 56),
        name="fft_stage_b",
    )(a, tabs["m_fwd"], tabs["m_inv"], g)


def _fft_bf_body(a_ref, mf_ref, o_ref, *, kg):
    for kk in range(kg):
        x = jnp.concatenate([a_ref[0, :, kk, :], a_ref[1, :, kk, :]], axis=0).astype(BF16)
        xf = jnp.dot(mf_ref[...], x, preferred_element_type=F32)
        o_ref[0, kk] = xf[:DFT2].astype(BF16)
        o_ref[1, kk] = xf[DFT2:].astype(BF16)


def _fft_b_forward(a, tabs, *, n1):
    c = a.shape[3]
    kg = FFT_GROUP
    return pl.pallas_call(
        functools.partial(_fft_bf_body, kg=kg),
        out_shape=jax.ShapeDtypeStruct((2, n1, DFT2, c), BF16),
        grid=(n1 // kg,),
        in_specs=[pl.BlockSpec((2, DFT2, kg, c), lambda k: (0, 0, k, 0)),
                  pl.BlockSpec((2 * DFT2, 2 * DFT2), lambda k: (0, 0))],
        out_specs=pl.BlockSpec((2, kg, DFT2, c), lambda k: (0, k, 0, 0)),
        compiler_params=_cparams(("parallel",), 48),
        name="fft_stage_b_filter",
    )(a, tabs["m_fwd"])


def _fft_ai_body(b_ref, m_ref, v_ref, x_ref, bias_ref, o_ref, *, tj):
    h = o_ref.shape[1]
    for e in range(tj):
        b = jnp.concatenate([b_ref[0, :, e, :], b_ref[1, :, e, :]], axis=0).astype(BF16)
        y = jnp.dot(m_ref[e], b, preferred_element_type=F32)
        o_ref[0, :, e, :] = y[:h]
        o_ref[1, :, e, :] = y[h:]
    o_ref[...] = x_ref[...] * (o_ref[...] + v_ref[...] * bias_ref[...])


def _fft_a_inv(b, m, v4, xm4, bias, *, n1, tj):
    c = b.shape[3]
    half = pl.BlockSpec((2, n1 // 2, tj, c), lambda j: (0, 0, j, 0))
    return pl.pallas_call(
        functools.partial(_fft_ai_body, tj=tj),
        out_shape=jax.ShapeDtypeStruct((2, n1 // 2, DFT2, c), F32),
        grid=(DFT2 // tj,),
        in_specs=[pl.BlockSpec((2, n1, tj, c), lambda j: (0, 0, j, 0)),
                  pl.BlockSpec((tj, n1, 2 * n1), lambda j: (j, 0, 0)),
                  half, half, pl.BlockSpec((1, c), lambda j: (0, 0))],
        out_specs=half,
        compiler_params=_cparams(("parallel",), 56),
        name="fft_stage_a_inv",
    )(b, m, v4, xm4, bias)


def _filt_a_body(hj_ref, hz_ref, hr_ref, s_ref, mf_ref, mr_ref, o_ref, *, tj):
    n1 = o_ref.shape[2]
    s = s_ref[...]
    for e in range(tj):
        src = hz_ref if e == 0 else hr_ref
        me = 0 if e == 0 else tj - e
        for o in range(2):
            c0 = 2 * D_GROUP * o
            inv = 1.0 / (s[:, c0:c0 + D_GROUP] + s[:, c0 + D_GROUP:c0 + 2 * D_GROUP] + EPS)
            lsl = slice(o * D_GROUP, (o + 1) * D_GROUP)
            hf = hj_ref[e, :, lsl].astype(BF16)
            hb = src[me, :, lsl].astype(BF16)
            a = (jnp.dot(mf_ref[e], hf, preferred_element_type=F32)
                 + jnp.dot(mr_ref[e], hb, preferred_element_type=F32)) * inv
            o_ref[0, e, :, lsl] = a[:n1]
            o_ref[1, e, :, lsl] = a[n1:]


def _filt_a(hf, hb, colsum, tabs, *, n1):
    tj = FFT_GROUP
    nj = DFT2 // tj
    wd = 2 * D_GROUP
    shp = (DFT2, n1 // 2, wd)
    blk = lambda fn: pl.BlockSpec((tj, n1 // 2, wd), fn)
    mat = pl.BlockSpec((tj, 2 * n1, n1 // 2), lambda j: (j, 0, 0))
    return pl.pallas_call(
        functools.partial(_filt_a_body, tj=tj),
        out_shape=jax.ShapeDtypeStruct((2, DFT2, n1, wd), F32),
        grid=(nj,),
        in_specs=[blk(lambda j: (j, 0, 0)), blk(lambda j: ((nj - j) % nj, 0, 0)), blk(lambda j: (nj - 1 - j, 0, 0)),
                  pl.BlockSpec((1, HY_FILTER_CH), lambda j: (0, 0)), mat, mat],
        out_specs=pl.BlockSpec((2, tj, n1, wd), lambda j: (0, j, 0, 0)),
        compiler_params=_cparams(("parallel",), 48),
        name="filter_stage_a",
    )(hf.reshape(shp), hb.reshape(shp), hb.reshape(shp), colsum, tabs["a_real"], tabs["a_rev"])


def _hyena_long(v, x1, x2, hf, hb, colsum, bias, *, seq):
    n1 = 2 * seq // DFT2
    tabs = {k: jnp.asarray(a).astype(BF16) for k, a in _dft_tables(n1).items()}
    tj = FFT_GROUP_DATA
    fa = _filt_a(hf, hb, colsum, tabs, n1=n1)
    g = _fft_b_forward(fa, tabs, n1=n1)
    shp = (2, n1 // 2, DFT2, D_GROUP)
    z = v.reshape(shp)
    for order, xm in ((0, x1.reshape(shp)), (1, x2.reshape(shp))):
        a = _fft_a(z, tabs["a_fwd"], n1=n1, tj=tj)
        b = _fft_b(a, g, order, tabs, n1=n1)
        z = _fft_a_inv(b, tabs["a_inv"], z, xm, bias[order].reshape(1, D_GROUP), n1=n1, tj=tj)
    return z.reshape(2 * seq, D_GROUP)


def _hyena_ctx_body(v_ref, x1_ref, x2_ref, hf_ref, hb_ref, s_ref, bias_ref, cm_ref, sm_ref, ct_ref, st_ref, o_ref):
    n = v_ref.shape[1]
    cm, sm, ct, st = cm_ref[...], sm_ref[...], ct_ref[...], st_ref[...]
    s = s_ref[...]
    row0 = _iota((n, D_GROUP), 0) == 0
    dot = lambda a, b: jnp.dot(a, b.astype(BF16), preferred_element_type=F32)
    zr, zi = v_ref[0], v_ref[1]
    for order, xm in ((0, x1_ref), (1, x2_ref)):
        c0 = 2 * D_GROUP * order
        inv = 1.0 / (s[:, c0:c0 + D_GROUP] + s[:, c0 + D_GROUP:c0 + 2 * D_GROUP] + EPS)
        lsl = slice(order * D_GROUP, (order + 1) * D_GROUP)
        hf = hf_ref[:, lsl] * inv
        hb = jnp.where(row0, 0.0, hb_ref[:, lsl] * inv)
        gr = dot(cm, hf + hb)
        gi = dot(sm, hb - hf)
        xr = dot(cm, zr) + dot(sm, zi)
        xi = dot(cm, zi) - dot(sm, zr)
        yr = xr * gr - xi * gi
        yi = xr * gi + xi * gr
        cr = dot(ct, yr) - dot(st, yi)
        ci = dot(ct, yi) + dot(st, yr)
        bias = bias_ref[order:order + 1, :]
        zr = xm[0] * (cr + zr * bias)
        zi = xm[1] * (ci + zi * bias)
    o_ref[0] = zr
    o_ref[1] = zi


def _hyena_ctx(v, x1, x2, hf, hb, colsum, bias, *, seq):
    nn = 2 * seq
    k = np.arange(nn)
    t = np.arange(seq)
    th = 2.0 * np.pi * (np.outer(k, t) % nn) / nn
    cm, sm = np.cos(th), np.sin(th)
    consts = [jnp.asarray(a, F32).astype(BF16) for a in (cm, sm, cm.T / nn, sm.T / nn)]
    shp = (2, seq, D_GROUP)
    full3 = pl.BlockSpec(shp, lambda i: (0, 0, 0))
    f2 = lambda a: pl.BlockSpec(a.shape, lambda i: (0, 0))
    args = [hf, hb, colsum, bias] + consts
    out = pl.pallas_call(
        _hyena_ctx_body,
        out_shape=jax.ShapeDtypeStruct(shp, F32),
        grid=(1,),
        in_specs=[full3, full3, full3] + [f2(a) for a in args],
        out_specs=full3,
        compiler_params=_cparams(("arbitrary",)),
        name="hyena_ctx",
    )(v.reshape(shp), x1.reshape(shp), x2.reshape(shp), *args)
    return out.reshape(2 * seq, D_GROUP)


def _split_w_in(w_in):
    wb = w_in.astype(BF16)
    parts = dict(kq=jnp.concatenate([wb[:, COL_K:COL_V], wb[:, COL_Q:COL_R]], axis=1), v=wb[:, COL_V:COL_GF],
                 r=wb[:, COL_R:COL_POOL], pool=wb[:, COL_POOL:COL_HY], hy=wb[:, COL_HY:COL_CONV],
                 conv=wb[:, COL_CONV:P_IN])
    gate = jnp.pad(w_in[:, COL_GF:COL_Q], ((0, 0), (0, LANE - 2 * GLA_LOWRANK)))
    return parts, gate


def _mix(u, p, *, bsz, seq, is_ctx, states, filt):
    sf, sb = states
    cpb = min(16, seq // CHUNK)
    if is_ctx:
        pool = _pool1d(u["pool"], p["pool_wbd"], p["pool_scale"], bsz=bsz, seq=seq)
    else:
        pool = _pool2d(u["pool"], p["pool_wbd"], p["pool_scale"], bsz=bsz, seq=seq)
    tmc = min(seq, 2048)
    v, x1, x2 = _hy_short(u["hy"], p["hy_short_w"], p["hy_short_b"], seq=seq, tm=tmc)
    if is_ctx:
        hy = _hyena_ctx(v, x1, x2, *filt, p["hy_bias"], seq=seq)
    else:
        hy = _hyena_long(v, x1, x2, *filt, p["hy_bias"], seq=seq)
    gla = _gla_read(u["kq"], u["v"], u["kq"], u["r"], u["lf"], u["lb"], sf, sb, p["gla_ng"],
                    bsz=bsz, seq=seq, cpb=cpb)
    conv = _conformer(u["conv"], p["conv_dw_w"], p["conv_dw_b"], p["conv_ln_g"], p["conv_ln_b"], seq=seq, tm=tmc)
    return [pool, hy, gla, conv]


def kernel(x, c, ctx, c_ctx, ada_w, ada_b, ffn1_norm, ffn1_wi, ffn1_wo, mix_norm, w_in, w_out, pool_w, pool_scale, hy_short_w, hy_short_b, hy_w1, hy_b1, hy_w2, hy_b2, hy_w3, hy_deltas, hy_bias, gla_gw_f, gla_gb_f, gla_gw_b, gla_gb_b, gla_norm, conv_dw_w, conv_dw_b, conv_ln_g, conv_ln_b, ffn2_norm, ffn2_wi, ffn2_wo, final_norm):
    bsz, seq, d = x.shape
    clen = ctx.shape[1]
    depth = ada_w.shape[0]
    assert bsz == 2, "the Hyena transform packs exactly two batch rows into one complex signal"
    xs = x.reshape(bsz * seq, d)
    cs = ctx.reshape(bsz * clen, d)
    cc = jnp.concatenate([c, c_ctx[None, :], jnp.zeros((8 - bsz - 1, d), F32)], axis=0)
    mod = _modulation(cc, ada_w, ada_b).reshape(depth, 8, N_MOD, d)
    crow = bsz
    head_eye = np.kron(np.eye(len(POOL_WINDOWS), dtype=np.float32), np.ones((POOL_CH, POOL_CH), np.float32))
    tmx = 512
    tps_x = seq // tmx
    tmc = bsz * clen
    zero_state = jnp.zeros((bsz, GLA_HEADS * GLA_DV, GLA_QK), F32)
    wi1, wo1, wi2, wo2, wout = (w.astype(BF16) for w in (ffn1_wi, ffn1_wo, ffn2_wi, ffn2_wo, w_out))

    for l in range(depth):
        last = l == depth - 1
        wparts, wgate = _split_w_in(w_in[l])
        zlow = jnp.zeros((GLA_LOWRANK, GLA_QK), F32)
        gw = jnp.concatenate([jnp.concatenate([gla_gw_f[l], zlow], axis=1),
                              jnp.concatenate([zlow, gla_gw_b[l]], axis=1),
                              jnp.zeros((LANE - 2 * GLA_LOWRANK, 2 * GLA_QK), F32)], axis=0)
        gb = jnp.concatenate([gla_gb_f[l], gla_gb_b[l]]).reshape(1, -1)
        gw = _gate_fold(wgate, gw).astype(BF16)
        wbd = jnp.tile(pool_w[l].reshape(D_GROUP, POOL_CH), (1, len(POOL_WINDOWS))) * head_eye
        p = dict(pool_wbd=wbd.astype(BF16), pool_scale=pool_scale[l].reshape(1, -1),
                 hy_short_w=hy_short_w[l], hy_short_b=hy_short_b[l], hy_bias=hy_bias[l],
                 gla_ng=jnp.tile(gla_norm[l], GLA_HEADS).reshape(1, -1),
                 conv_dw_w=conv_dw_w[l], conv_dw_b=conv_dw_b[l], conv_ln_g=conv_ln_g[l], conv_ln_b=conv_ln_b[l])
        names = ["kq", "v", "r", "pool", "hy", "conv"]

        xs, *outs = _ffn_proj(xs, mod, ffn1_norm[l], wi1, wo1, mix_norm[l], [wparts[nm] for nm in names], gw, gb,
                              layer=l, row0=0, tm=tmx, tiles_per_seq=tps_x)
        ux = dict(zip(names + ["lf", "lb"], outs))
        cnames = ["kq", "v"] if last else names
        cs, *outs = _ffn_proj(cs, mod, ffn1_norm[l], wi1, wo1, mix_norm[l], [wparts[nm] for nm in cnames], gw, gb,
                              layer=l, row0=crow, tm=tmc, tiles_per_seq=1)
        uc = dict(zip(cnames + ["lf", "lb"], outs))

        ccpb = clen // CHUNK
        sfc, sbc, finf, finb = _gla_states(uc["kq"], uc["v"], uc["lf"], uc["lb"], zero_state, zero_state,
                                           bsz=bsz, seq=clen, cpb=ccpb)
        sfx, sbx, _, _ = _gla_states(ux["kq"], ux["v"], ux["lf"], ux["lb"], finf, finb, bsz=bsz, seq=seq,
                                     cpb=16)

        filt = _hy_filter(seq, hy_w1[l], hy_b1[l], hy_w2[l], hy_b2[l], hy_w3[l], hy_deltas[l], t2_major=True)
        mixed = _mix(ux, p, bsz=bsz, seq=seq, is_ctx=False, states=(sfx, sbx), filt=filt)
        xs = _out_ffn(xs, mod, mixed, wout, ffn2_norm[l], wi2, wo2, layer=l, row0=0, tm=tmx,
                      tiles_per_seq=tps_x, final_g=final_norm if last else None)
        if not last:
            filt_c = _hy_filter(clen, hy_w1[l], hy_b1[l], hy_w2[l], hy_b2[l], hy_w3[l], hy_deltas[l],
                                t2_major=False)
            mixed = _mix(uc, p, bsz=bsz, seq=clen, is_ctx=True, states=(sfc, sbc), filt=filt_c)
            cs = _out_ffn(cs, mod, mixed, wout, ffn2_norm[l], wi2, wo2, layer=l, row0=crow, tm=tmc,
                          tiles_per_seq=1)
    return xs.reshape(bsz, seq, d)
```

```python
import functools
import math

import numpy as np
import jax
import jax.numpy as jnp
from jax import lax
from jax.experimental import pallas as pl
from jax.experimental.pallas import tpu as pltpu

F32 = jnp.float32
BF16 = jnp.bfloat16
HI = lax.Precision.HIGHEST

GRID_W = 64
D_GROUP = 256
N_MOD = 9
EPS = 1e-6
POOL_WINDOWS = (2, 4, 8, 16)
POOL_CH = 64
HY_BANDS = 16
HY_EMB = 1 + 2 * HY_BANDS
HY_FFN = 64
HY_FILTER_CH = 4 * D_GROUP
GLA_HEADS = 4
GLA_DK = 32
GLA_DV = 64
GLA_QK = 128
GLA_LOWRANK = 16
GLA_TAU = 16.0
CHUNK = 64
CONV_WIDTH = 31
HY_SHORT = 3

COL_K = 0
COL_V = COL_K + GLA_QK
COL_GF = COL_V + D_GROUP
COL_GB = COL_GF + GLA_LOWRANK
COL_Q = COL_GB + GLA_LOWRANK
COL_R = COL_Q + GLA_QK
COL_POOL = COL_R + D_GROUP
COL_HY = COL_POOL + D_GROUP
COL_CONV = COL_HY + 3 * D_GROUP
P_IN = COL_CONV + 2 * D_GROUP

LANE = 128
SUBLANES = 8
MIB = 1024 * 1024

DFT2 = 128
FFT_GROUP = 16
FFT_GROUP_DATA = 32
MOD_ROWS = 3
FF_CHUNK = 256
ROW_SUB = 256
TOKEN_TILE = 512


def _cparams(sem, vmem_mib=None):
    kw = dict(dimension_semantics=sem)
    if vmem_mib is not None:
        kw["vmem_limit_bytes"] = vmem_mib * MIB
    return pltpu.CompilerParams(**kw)


def _silu(x):
    return x * jax.nn.sigmoid(x)


def _rms_mod(h, g, m):
    y = h * lax.rsqrt(jnp.mean(h * h, axis=-1, keepdims=True) + EPS) * g
    return y * (1.0 + m[1:2, :]) + m[0:1, :]


def _iota(shape, dim):
    return lax.broadcasted_iota(jnp.int32, shape, dim)


def _mod_body(at_ref, w_ref, b_ref, o_ref):
    d, tn = w_ref.shape[1], w_ref.shape[2]
    at = _silu(at_ref[...])
    cols = [jnp.broadcast_to(at[:, r:r + 1], (d, LANE)).reshape(d // SUBLANES, SUBLANES, LANE)
            for r in range(MOD_ROWS)]
    o_ref[0] = jnp.zeros((SUBLANES, tn), F32) + b_ref[0]
    for j in range(tn // LANE):
        lsl = slice(j * LANE, (j + 1) * LANE)
        w3 = w_ref[0, :, lsl].reshape(d // SUBLANES, SUBLANES, LANE)
        for r in range(MOD_ROWS):
            part = jnp.sum(w3 * cols[r], axis=0)
            o_ref[0, r:r + 1, lsl] += jnp.sum(part, axis=0, keepdims=True)


def _modulation(cc, ada_w, ada_b):
    nl, d, nm = ada_w.shape
    tn = 2304
    return pl.pallas_call(
        _mod_body,
        out_shape=jax.ShapeDtypeStruct((nl, 8, nm), F32),
        grid=(nl, nm // tn),
        in_specs=[pl.BlockSpec((d, 8), lambda l, j: (0, 0)),
                  pl.BlockSpec((1, d, tn), lambda l, j: (l, 0, j)),
                  pl.BlockSpec((1, 1, tn), lambda l, j: (l, 0, j))],
        out_specs=pl.BlockSpec((1, 8, tn), lambda l, j: (l, 0, j)),
        compiler_params=_cparams(("parallel", "parallel"), 40),
        name="adaln_mod",
    )(cc.T, ada_w, ada_b.reshape(nl, 1, nm))


def _swiglu(xn, wi_ref, wo_ref, hm_ref, rows):
    ff = wo_ref.shape[0]
    for c in range(0, ff, FF_CHUNK):
        a = jnp.dot(xn, wi_ref[:, c:c + FF_CHUNK], preferred_element_type=F32)
        g = jnp.dot(xn, wi_ref[:, ff + c:ff + c + FF_CHUNK], preferred_element_type=F32)
        hm_ref[rows, c:c + FF_CHUNK] = (_silu(g) * a).astype(BF16)
    return jnp.dot(hm_ref[rows, :], wo_ref[...], preferred_element_type=F32)


def _resident(shape):
    nd = len(shape)
    return pl.BlockSpec(shape, lambda i: (0,) * nd, pipeline_mode=pl.Buffered(1))


def _resident_layer(shape, layer):
    nd = len(shape)
    return pl.BlockSpec((None,) + tuple(shape), lambda i: (layer,) + (0,) * nd, pipeline_mode=pl.Buffered(1))


def _gate_fold_body(wg_ref, gw_ref, o_ref):
    o_ref[...] = jnp.dot(wg_ref[...], gw_ref[...], precision=HI, preferred_element_type=F32)


def _gate_fold(w_gate, gw):
    d = w_gate.shape[0]
    return pl.pallas_call(
        _gate_fold_body,
        out_shape=jax.ShapeDtypeStruct((d, gw.shape[1]), F32),
        grid=(1,),
        in_specs=[pl.BlockSpec(w_gate.shape, lambda i: (0, 0)), pl.BlockSpec(gw.shape, lambda i: (0, 0))],
        out_specs=pl.BlockSpec((d, gw.shape[1]), lambda i: (0, 0)),
        name="gate_fold",
    )(w_gate, gw)


def _chunk_prefix(x):
    pos = _iota(x.shape, 0) & (CHUNK - 1)
    shift = 1
    while shift < CHUNK:
        x = x + jnp.where(pos >= shift, pltpu.roll(x, shift, 0), 0.0)
        shift *= 2
    return x


def _ffn_proj_body(h_ref, m_ref, g1_ref, wi_ref, wo_ref, g2_ref, *rest, nparts):
    w_refs = rest[:nparts]
    gw_ref, gb_ref = rest[nparts:nparts + 2]
    x_ref = rest[nparts + 2]
    o_refs = rest[nparts + 3:2 * nparts + 5]
    hm_ref = rest[-1]
    m = m_ref[0]
    for r0 in range(0, h_ref.shape[0], ROW_SUB):
        rows = slice(r0, r0 + ROW_SUB)
        h = h_ref[rows, :]
        xn = _rms_mod(h, g1_ref[...], m[0:2]).astype(BF16)
        x1 = h + (0.5 * m[2:3, :]) * _swiglu(xn, wi_ref, wo_ref, hm_ref, rows)
        x_ref[rows, :] = x1
        xn2 = _rms_mod(x1, g2_ref[...], m[3:5]).astype(BF16)
        for w_ref, o_ref in zip(w_refs, o_refs[:nparts]):
            o_ref[rows, :] = jnp.dot(xn2, w_ref[...], preferred_element_type=F32).astype(o_ref.dtype)
        a = jnp.dot(xn2, gw_ref[...], preferred_element_type=F32) + gb_ref[...]
        ls = (jnp.minimum(a, 0.0) - jnp.log(1.0 + jnp.exp(-jnp.abs(a)))) * (1.0 / GLA_TAU)
        pre = _chunk_prefix(ls)
        o_refs[nparts][rows, :] = pre[:, :GLA_QK]
        pb = pre[:, GLA_QK:]
        tot = jnp.concatenate([jnp.broadcast_to(pb[c0 + CHUNK - 1:c0 + CHUNK, :], (CHUNK, GLA_QK))
                               for c0 in range(0, ROW_SUB, CHUNK)], axis=0)
        o_refs[nparts + 1][rows, :] = tot - pb + ls[:, GLA_QK:]


def _mod_spec(d, layer, row0, tiles_per_seq):
    return pl.BlockSpec((None, 1, N_MOD, d), lambda i: (layer, row0 + i // tiles_per_seq, 0, 0))


def _ffn_proj(h, mod, g1, wi, wo, g2, w_parts, gw, gb, *, layer, row0, tm, tiles_per_seq):
    n, d = h.shape
    ff = wo.shape[1]
    nparts = len(w_parts)
    widths = [w.shape[1] for w in w_parts]
    tok = lambda wd: pl.BlockSpec((tm, wd), lambda i: (i, 0))
    in_specs = [tok(d), _mod_spec(d, layer, row0, tiles_per_seq), _resident((1, d)),
                _resident_layer((d, 2 * ff), layer), _resident_layer((ff, d), layer), _resident((1, d))]
    in_specs += [_resident((d, wd)) for wd in widths]
    in_specs += [_resident((d, 2 * GLA_QK)), _resident((1, 2 * GLA_QK))]
    out_shape = [jax.ShapeDtypeStruct((n, d), F32)]
    out_shape += [jax.ShapeDtypeStruct((n, wd), BF16) for wd in widths]
    out_shape += [jax.ShapeDtypeStruct((n, GLA_QK), F32)] * 2
    return pl.pallas_call(
        functools.partial(_ffn_proj_body, nparts=nparts),
        out_shape=out_shape,
        grid=(n // tm,),
        in_specs=in_specs,
        out_specs=[tok(d)] + [tok(wd) for wd in widths] + [tok(GLA_QK)] * 2,
        scratch_shapes=[pltpu.VMEM((tm, ff), BF16)],
        compiler_params=_cparams(("parallel",), 56),
        name="ffn1_in_proj",
    )(h, mod, g1.reshape(1, d), wi, wo, g2.reshape(1, d), *w_parts, gw, gb)


def _out_ffn_body(x_ref, m_ref, p_ref, hy_ref, gl_ref, cv_ref, wout_ref, g_ref, wi_ref, wo_ref, fn_ref, o_ref,
                  hm_ref, *, final):
    m = m_ref[0]
    for r0 in range(0, x_ref.shape[0], ROW_SUB):
        rows = slice(r0, r0 + ROW_SUB)
        acc = jnp.dot(p_ref[rows, :].astype(BF16), wout_ref[0:D_GROUP, :], preferred_element_type=F32)
        acc += jnp.dot(hy_ref[rows, :].astype(BF16), wout_ref[D_GROUP:2 * D_GROUP, :], preferred_element_type=F32)
        acc += jnp.dot(gl_ref[rows, :].astype(BF16), wout_ref[2 * D_GROUP:3 * D_GROUP, :],
                       preferred_element_type=F32)
        acc += jnp.dot(cv_ref[rows, :].astype(BF16), wout_ref[3 * D_GROUP:, :], preferred_element_type=F32)
        x2 = x_ref[rows, :] + m[5:6, :] * acc
        xn = _rms_mod(x2, g_ref[...], m[6:8]).astype(BF16)
        out = x2 + (0.5 * m[8:9, :]) * _swiglu(xn, wi_ref, wo_ref, hm_ref, rows)
        if final:
            out = out * lax.rsqrt(jnp.mean(out * out, axis=-1, keepdims=True) + EPS) * fn_ref[...]
        o_ref[rows, :] = out


def _out_ffn(x, mod, parts, w_out, g, wi, wo, *, layer, row0, tm, tiles_per_seq, final_g=None):
    n, d = x.shape
    ff = wo.shape[1]
    final = final_g is not None
    fg = final_g if final else g
    tok = lambda wd: pl.BlockSpec((tm, wd), lambda i: (i, 0))
    return pl.pallas_call(
        functools.partial(_out_ffn_body, final=final),
        out_shape=jax.ShapeDtypeStruct((n, d), F32),
        grid=(n // tm,),
        in_specs=[tok(d), _mod_spec(d, layer, row0, tiles_per_seq),
                  tok(D_GROUP), tok(D_GROUP), tok(D_GROUP), tok(D_GROUP), _resident_layer((d, d), layer),
                  _resident((1, d)), _resident_layer((d, 2 * ff), layer), _resident_layer((ff, d), layer),
                  _resident((1, d))],
        out_specs=tok(d),
        scratch_shapes=[pltpu.VMEM((tm, ff), BF16)],
        compiler_params=_cparams(("parallel",), 56),
        name="out_proj_ffn2",
    )(x, mod, *parts, w_out, g.reshape(1, d), wi, wo, fg.reshape(1, d))


def _gla_state_body(kf, vf, bf, kb, vb, bb, s0f, s0b, sf_o, sb_o, ff_o, fb_o, stf, stb, *, cpb):
    i = pl.program_id(1)

    @pl.when(i == 0)
    def _():
        stf[...] = s0f[0]
        stb[...] = s0b[0]

    sshape = (GLA_HEADS * GLA_DV, GLA_QK)
    bmask = (_iota(sshape, 0) >> 6) == (_iota(sshape, 1) >> 5)
    tn_dims = (((0,), (0,)), ((), ()))

    def direction(k_ref, v_ref, b_ref, last, st, s_o, order):
        s = st[...]
        for ci in order:
            sl = slice(ci * CHUNK, (ci + 1) * CHUNK)
            b = b_ref[sl, :]
            tot = b[last:last + 1, :]
            kd = (k_ref[sl, :] * jnp.exp(tot - b)).astype(BF16)
            upd = lax.dot_general(v_ref[sl, :].astype(BF16), kd, tn_dims, preferred_element_type=F32)
            s_o[0, ci] = s.astype(BF16)
            s = s * jnp.exp(tot) + jnp.where(bmask, upd, 0.0)
        st[...] = s

    direction(kf, vf, bf, CHUNK - 1, stf, sf_o, range(cpb))
    direction(kb, vb, bb, 0, stb, sb_o, range(cpb - 1, -1, -1))
    ff_o[0] = stf[...]
    fb_o[0] = stb[...]


def _gla_states(k, v, lf, lb, s0f, s0b, *, bsz, seq, cpb):
    bt = cpb * CHUNK
    nb = seq // bt
    nc = seq // CHUNK
    srow = GLA_HEADS * GLA_DV

    def tf(b, i):
        return (b * nb + i, 0)

    def tb(b, i):
        return (b * nb + nb - 1 - i, 0)

    sblk = pl.BlockSpec((1, srow, GLA_QK), lambda b, i: (b, 0, 0))
    return pl.pallas_call(
        functools.partial(_gla_state_body, cpb=cpb),
        out_shape=[jax.ShapeDtypeStruct((bsz, nc, srow, GLA_QK), BF16),
                   jax.ShapeDtypeStruct((bsz, nc, srow, GLA_QK), BF16),
                   jax.ShapeDtypeStruct((bsz, srow, GLA_QK), F32),
                   jax.ShapeDtypeStruct((bsz, srow, GLA_QK), F32)],
        grid=(bsz, nb),
        in_specs=[pl.BlockSpec((bt, GLA_QK), tf), pl.BlockSpec((bt, D_GROUP), tf), pl.BlockSpec((bt, GLA_QK), tf),
                  pl.BlockSpec((bt, GLA_QK), tb), pl.BlockSpec((bt, D_GROUP), tb), pl.BlockSpec((bt, GLA_QK), tb),
                  sblk, sblk],
        out_specs=[pl.BlockSpec((1, cpb, srow, GLA_QK), lambda b, i: (b, i, 0, 0)),
                   pl.BlockSpec((1, cpb, srow, GLA_QK), lambda b, i: (b, nb - 1 - i, 0, 0)),
                   sblk, sblk],
        scratch_shapes=[pltpu.VMEM((srow, GLA_QK), F32), pltpu.VMEM((srow, GLA_QK), F32)],
        compiler_params=_cparams(("parallel", "arbitrary")),
        name="gla_states",
    )(k, v, lf, k, v, lb, s0f, s0b)


def _gla_read_body(k_ref, v_ref, q_ref, r_ref, bf_ref, bb_ref, sf_ref, sb_ref, ng_ref, o_ref, acc_ref, *, cpb):
    hrows = GLA_HEADS * CHUNK
    cpos = _iota((hrows, CHUNK), 0) & (CHUNK - 1)
    ccol = _iota((hrows, CHUNK), 1)
    lowm = cpos >= ccol
    upm = cpos <= ccol
    hq = (_iota((hrows, GLA_QK), 0) >> 6) == (_iota((hrows, GLA_QK), 1) >> 5)
    ho = (_iota((hrows, D_GROUP), 0) >> 6) == (_iota((hrows, D_GROUP), 1) >> 6)
    bavg = jnp.where((_iota((D_GROUP, D_GROUP), 0) >> 6) == (_iota((D_GROUP, D_GROUP), 1) >> 6),
                     1.0 / GLA_DV, 0.0).astype(BF16)
    nt_dims = (((1,), (1,)), ((), ()))

    bf_ = bf_ref[...]
    bb_ = bb_ref[...]
    qs = q_ref[...].astype(F32) * (GLA_DK ** -0.5)
    kk = k_ref[...]
    qef = (qs * jnp.exp(bf_)).astype(BF16)
    qeb = (qs * jnp.exp(bb_)).astype(BF16)
    kef = (kk * jnp.exp(-bf_)).astype(BF16)
    keb = (kk * jnp.exp(-bb_)).astype(BF16)
    vb16 = v_ref[...].astype(BF16)
    zero = jnp.zeros((), BF16)
    for ci in range(cpb):
        sl = slice(ci * CHUNK, (ci + 1) * CHUNK)
        qf4 = jnp.where(hq, jnp.concatenate([qef[sl]] * GLA_HEADS, axis=0), zero)
        qb4 = jnp.where(hq, jnp.concatenate([qeb[sl]] * GLA_HEADS, axis=0), zero)
        af = lax.dot_general(qf4, kef[sl], nt_dims, preferred_element_type=F32)
        ab = lax.dot_general(qb4, keb[sl], nt_dims, preferred_element_type=F32)
        att = (jnp.where(lowm, af, 0.0) + jnp.where(upm, ab, 0.0)).astype(BF16)
        oall = jnp.dot(att, vb16[sl], preferred_element_type=F32)
        om = jnp.where(ho, oall, 0.0)
        o = om[0:CHUNK] + om[CHUNK:2 * CHUNK] + om[2 * CHUNK:3 * CHUNK] + om[3 * CHUNK:4 * CHUNK]
        qcat = jnp.concatenate([qef[sl], qeb[sl]], axis=1)
        scat = jnp.concatenate([sf_ref[0, ci], sb_ref[0, ci]], axis=1)
        acc_ref[sl, :] = o + lax.dot_general(qcat, scat, nt_dims, preferred_element_type=F32)
    o = acc_ref[...]
    ms = _chunk_sum_rhs(o * o, bavg)
    o_ref[...] = (o * lax.rsqrt(ms + EPS) * ng_ref[...] * _silu(r_ref[...].astype(F32))).astype(o_ref.dtype)


def _chunk_sum_rhs(x, mat):
    hi, lo = _split_bf16(x)
    return jnp.dot(hi, mat, preferred_element_type=F32) + jnp.dot(lo, mat, preferred_element_type=F32)


def _gla_read(k, v, q, r, lf, lb, sf, sb, ng, *, bsz, seq, cpb):
    bt = cpb * CHUNK
    nb = seq // bt
    srow = GLA_HEADS * GLA_DV
    n = bsz * seq

    def tk(i):
        return (i, 0)

    sspec = pl.BlockSpec((1, cpb, srow, GLA_QK), lambda i: (i // nb, i % nb, 0, 0))
    return pl.pallas_call(
        functools.partial(_gla_read_body, cpb=cpb),
        out_shape=jax.ShapeDtypeStruct((n, D_GROUP), BF16),
        scratch_shapes=[pltpu.VMEM((bt, D_GROUP), F32)],
        grid=(bsz * nb,),
        in_specs=[pl.BlockSpec((bt, GLA_QK), tk), pl.BlockSpec((bt, D_GROUP), tk),
                  pl.BlockSpec((bt, GLA_QK), lambda i: (i, 1)), pl.BlockSpec((bt, D_GROUP), tk),
                  pl.BlockSpec((bt, GLA_QK), tk), pl.BlockSpec((bt, GLA_QK), tk),
                  sspec, sspec, pl.BlockSpec((1, D_GROUP), lambda i: (0, 0))],
        out_specs=pl.BlockSpec((bt, D_GROUP), tk),
        compiler_params=_cparams(("parallel",)),
        name="gla_readout",
    )(k, v, q, r, lf, lb, sf, sb, ng)


def _box_matrix(n, w):
    pos = np.arange(n)
    lo = np.clip(pos - w // 2, 0, n)
    hi = np.clip(pos - w // 2 + w, 0, n)
    col = np.arange(n)[None, :]
    return ((col >= lo[:, None]) & (col < hi[:, None])).astype(np.float32)


def _lane_windows(shape):
    w = jnp.left_shift(2, _iota(shape, 1) >> 6)
    return w, w >> 1


def _box_count(pos, w, half, n):
    return jnp.minimum(pos - half + w, n) - jnp.maximum(pos - half, 0)


def _split_bf16(x):
    hi = x.astype(BF16)
    lo = (x - hi.astype(F32)).astype(BF16)
    return hi, lo


def _pool2d_body(cur_ref, prev_ref, next_ref, pc_ref, w_ref, sc_ref, o_ref, ycol, *, tiles, rows):
    i = pl.program_id(1)
    tm = cur_ref.shape[0]
    hb = prev_ref.shape[0]
    sub = 2 * GRID_W
    pflag = jnp.where(i > 0, 1.0, 0.0)
    nflag = jnp.where(i < tiles - 1, 1.0, 0.0)

    def colpool(x):
        halves = []
        for half in range(2):
            lsl = slice(half * LANE, (half + 1) * LANE)
            ys = [jnp.dot(pc_ref[wi], x[:, lsl], preferred_element_type=F32) for wi in (2 * half, 2 * half + 1)]
            lane = _iota((sub, LANE), 1)
            halves.append(jnp.where(lane < POOL_CH, ys[0], ys[1]))
        return jnp.concatenate(halves, axis=1)

    for s in range(hb // sub):
        ycol[s * sub:(s + 1) * sub, :] = colpool(prev_ref[s * sub:(s + 1) * sub, :]) * pflag
    for s in range(tm // sub):
        ycol[hb + s * sub:hb + (s + 1) * sub, :] = colpool(cur_ref[s * sub:(s + 1) * sub, :])
    for s in range(hb // sub):
        ycol[hb + tm + s * sub:hb + tm + (s + 1) * sub, :] = colpool(next_ref[s * sub:(s + 1) * sub, :]) * nflag

    rc = 256
    wl, half = _lane_windows((rc, D_GROUP))
    narrow = _iota((rc, LANE), 1) < POOL_CH
    for r0 in range(0, tm, rc):
        def band(lo, hi, lsl):
            base = hb + r0
            acc = ycol[base + GRID_W * lo:base + GRID_W * lo + rc, lsl]
            for dd in range(lo + 1, hi):
                acc = acc + ycol[base + GRID_W * dd:base + GRID_W * dd + rc, lsl]
            return acc

        left, right = slice(0, LANE), slice(LANE, 2 * LANE)
        z2 = band(-1, 1, left)
        z4 = z2 + band(-2, -1, left) + band(1, 2, left)
        z8 = band(-4, 4, right)
        z16 = z8 + band(-8, -4, right) + band(4, 8, right)
        z = jnp.concatenate([jnp.where(narrow, z2, z4), jnp.where(narrow, z8, z16)], axis=1)
        tok = _iota((rc, D_GROUP), 0) + (i * tm + r0)
        rcnt = _box_count(tok >> 6, wl, half, rows)
        ccnt = _box_count(tok & (GRID_W - 1), wl, half, GRID_W)
        pooled = z / (rcnt * ccnt).astype(F32)
        dlt = (pooled - cur_ref[r0:r0 + rc, :]).astype(BF16)
        o_ref[r0:r0 + rc, :] = (jnp.dot(dlt, w_ref[...], preferred_element_type=F32) * sc_ref[...]).astype(o_ref.dtype)


def _pool2d(u, wbd, scale, *, bsz, seq):
    tm = 1024
    hb = 512
    tiles = seq // tm
    r = tm // hb
    nhb = seq // hb
    pc = np.stack([np.kron(np.eye(2, dtype=np.float32), _box_matrix(GRID_W, w)) for w in POOL_WINDOWS])
    return pl.pallas_call(
        functools.partial(_pool2d_body, tiles=tiles, rows=seq // GRID_W),
        out_shape=jax.ShapeDtypeStruct((bsz * seq, D_GROUP), BF16),
        grid=(bsz, tiles),
        in_specs=[pl.BlockSpec((tm, D_GROUP), lambda b, i: (b * tiles + i, 0)),
                  pl.BlockSpec((hb, D_GROUP), lambda b, i: (b * nhb + jnp.maximum(i * r - 1, 0), 0)),
                  pl.BlockSpec((hb, D_GROUP), lambda b, i: (b * nhb + jnp.minimum(i * r + r, nhb - 1), 0)),
                  pl.BlockSpec((4, 2 * GRID_W, 2 * GRID_W), lambda b, i: (0, 0, 0)),
                  pl.BlockSpec((D_GROUP, D_GROUP), lambda b, i: (0, 0)),
                  pl.BlockSpec((1, D_GROUP), lambda b, i: (0, 0))],
        out_specs=pl.BlockSpec((tm, D_GROUP), lambda b, i: (b * tiles + i, 0)),
        scratch_shapes=[pltpu.VMEM((tm + 2 * hb, D_GROUP), F32)],
        compiler_params=_cparams(("parallel", "parallel")),
        name="pool2d",
    )(u, u, u, jnp.asarray(pc, BF16), wbd, scale)


def _pool1d_body(x_ref, p_ref, w_ref, sc_ref, o_ref):
    x = x_ref[...]
    n = x.shape[0]
    ys = [jnp.dot(p_ref[wi], x, preferred_element_type=F32) for wi in range(4)]
    wl, half = _lane_windows((n, D_GROUP))
    z = jnp.where(wl == 2, ys[0], jnp.where(wl == 4, ys[1], jnp.where(wl == 8, ys[2], ys[3])))
    cnt = _box_count(_iota((n, D_GROUP), 0), wl, half, n)
    dlt = (z / cnt.astype(F32) - x).astype(BF16)
    o_ref[...] = (jnp.dot(dlt, w_ref[...], preferred_element_type=F32) * sc_ref[...]).astype(o_ref.dtype)


def _pool1d(u, wbd, scale, *, bsz, seq):
    pm = np.stack([_box_matrix(seq, w) for w in POOL_WINDOWS])
    return pl.pallas_call(
        _pool1d_body,
        out_shape=jax.ShapeDtypeStruct((bsz * seq, D_GROUP), BF16),
        grid=(bsz,),
        in_specs=[pl.BlockSpec((seq, D_GROUP), lambda b: (b, 0)),
                  pl.BlockSpec((4, seq, seq), lambda b: (0, 0, 0)),
                  pl.BlockSpec((D_GROUP, D_GROUP), lambda b: (0, 0)),
                  pl.BlockSpec((1, D_GROUP), lambda b: (0, 0))],
        out_specs=pl.BlockSpec((seq, D_GROUP), lambda b: (b, 0)),
        compiler_params=_cparams(("parallel",)),
        name="pool1d",
    )(u, jnp.asarray(pm, BF16), wbd, scale)


def _fill_halo(buf, cur, prev, nxt, i, tps, hb, tm, pre):
    first = (i % tps) == 0
    last = (i % tps) == tps - 1
    buf[0:hb, :] = jnp.where(first, 0.0, pre(prev[...]))
    buf[hb:hb + tm, :] = pre(cur[...])
    buf[hb + tm:hb + tm + hb, :] = jnp.where(last, 0.0, pre(nxt[...]))


def _tap_phases(taps, off):
    return sorted({(off + j) % SUBLANES for j in range(taps)} - {0})


def _fill_phases(sh, buf, phases):
    rows = sh.shape[1]
    for slot, s in enumerate(phases):
        sh[slot, :, :] = buf[s:s + rows, :]


def _dwconv(buf, sh, phases, w_ref, r0, rc, taps, off):
    acc = None
    for j in range(taps):
        s, q = (off + j) % SUBLANES, (off + j) // SUBLANES
        lo = r0 + SUBLANES * q
        src = buf[lo:lo + rc, :] if s == 0 else sh[phases.index(s), lo:lo + rc, :]
        term = src * w_ref[j:j + 1, :]
        acc = term if acc is None else acc + term
    return acc


def _conf_body(cur, prev, nxt, w_ref, b_ref, lg_ref, lb_ref, o_ref, buf, sh, *, tps):
    i = pl.program_id(0)
    tm = cur.shape[0]
    hb = prev.shape[0]

    def glu(u):
        u = u.astype(F32)
        return u[:, :D_GROUP] * jax.nn.sigmoid(u[:, D_GROUP:])

    _fill_halo(buf, cur, prev, nxt, i, tps, hb, tm, glu)
    rc = 128
    off = hb - (CONV_WIDTH - 1) // 2
    phases = _tap_phases(CONV_WIDTH, off)
    _fill_phases(sh, buf, phases)
    for r0 in range(0, tm, rc):
        h = _dwconv(buf, sh, phases, w_ref, r0, rc, CONV_WIDTH, off) + b_ref[...]
        mu = jnp.mean(h, axis=-1, keepdims=True)
        hc = h - mu
        var = jnp.mean(hc * hc, axis=-1, keepdims=True)
        o_ref[r0:r0 + rc, :] = _silu(hc * lax.rsqrt(var + EPS) * lg_ref[...] + lb_ref[...]).astype(o_ref.dtype)


def _halo_specs(tm, hb, width, nrows):
    r = tm // hb
    nhb = nrows // hb
    return [pl.BlockSpec((tm, width), lambda i: (i, 0)),
            pl.BlockSpec((hb, width), lambda i: (jnp.maximum(i * r - 1, 0), 0)),
            pl.BlockSpec((hb, width), lambda i: (jnp.minimum(i * r + r, nhb - 1), 0))]


def _conformer(u, w, b, lg, lb, *, seq, tm):
    n = u.shape[0]
    hb = 16
    nph = len(_tap_phases(CONV_WIDTH, hb - (CONV_WIDTH - 1) // 2))
    vec = pl.BlockSpec((1, D_GROUP), lambda i: (0, 0))
    return pl.pallas_call(
        functools.partial(_conf_body, tps=seq // tm),
        out_shape=jax.ShapeDtypeStruct((n, D_GROUP), BF16),
        grid=(n // tm,),
        in_specs=_halo_specs(tm, hb, 2 * D_GROUP, n) + [pl.BlockSpec((CONV_WIDTH, D_GROUP), lambda i: (0, 0)),
                                                        vec, vec, vec],
        out_specs=pl.BlockSpec((tm, D_GROUP), lambda i: (i, 0)),
        scratch_shapes=[pltpu.VMEM((tm + 2 * hb, D_GROUP), F32),
                        pltpu.VMEM((nph, tm + 2 * hb - SUBLANES, D_GROUP), F32)],
        compiler_params=_cparams(("parallel",), 48),
        name="conformer_conv",
    )(u, u, u, w, b.reshape(1, -1), lg.reshape(1, -1), lb.reshape(1, -1))


def _short_body(cur, prev, nxt, w_ref, b_ref, v_ref, x1_ref, x2_ref, buf, sh, *, tps):
    i = pl.program_id(0)
    tm = cur.shape[0]
    hb = prev.shape[0]
    _fill_halo(buf, cur, prev, nxt, i, tps, hb, tm, lambda u: u.astype(F32))
    rc = 128
    off = hb - (HY_SHORT - 1) // 2
    phases = _tap_phases(HY_SHORT, off)
    _fill_phases(sh, buf, phases)
    for r0 in range(0, tm, rc):
        uc = _dwconv(buf, sh, phases, w_ref, r0, rc, HY_SHORT, off) + b_ref[...]
        v_ref[r0:r0 + rc, :] = uc[:, :D_GROUP]
        x1_ref[r0:r0 + rc, :] = uc[:, D_GROUP:2 * D_GROUP].astype(x1_ref.dtype)
        x2_ref[r0:r0 + rc, :] = uc[:, 2 * D_GROUP:].astype(x2_ref.dtype)


def _hy_short(u, w, b, *, seq, tm):
    n = u.shape[0]
    hb = 16
    wd = 3 * D_GROUP
    nph = len(_tap_phases(HY_SHORT, hb - (HY_SHORT - 1) // 2))
    ospec = pl.BlockSpec((tm, D_GROUP), lambda i: (i, 0))
    return pl.pallas_call(
        functools.partial(_short_body, tps=seq // tm),
        out_shape=[jax.ShapeDtypeStruct((n, D_GROUP), dt) for dt in (F32, BF16, BF16)],
        grid=(n // tm,),
        in_specs=_halo_specs(tm, hb, wd, n) + [pl.BlockSpec((HY_SHORT, wd), lambda i: (0, 0)),
                                               pl.BlockSpec((1, wd), lambda i: (0, 0))],
        out_specs=[ospec, ospec, ospec],
        scratch_shapes=[pltpu.VMEM((tm + 2 * hb, wd), F32),
                        pltpu.VMEM((nph, tm + 2 * hb - SUBLANES, wd), F32)],
        compiler_params=_cparams(("parallel",), 48),
        name="hyena_short_conv",
    )(u, u, u, w, b.reshape(1, -1))


def _filter_features(n):
    i = np.arange(n, dtype=np.float64)
    t = np.linspace(0.0, 1.0, n, dtype=np.float32).astype(np.float64)
    wpos = ((2.0 * math.pi / n) * np.arange(n, dtype=np.float32)).astype(np.float32)
    bands = np.linspace(1e-4, HY_BANDS - 1, HY_BANDS, dtype=np.float32)
    arg = (bands[None, :] * wpos[:, None]).astype(np.float32).astype(np.float64)
    z = np.zeros((n, 64), np.float32)
    z[:, 0] = t
    z[:, 1:1 + HY_BANDS] = np.cos(arg)
    z[:, 1 + HY_BANDS:HY_EMB] = -np.sin(arg)
    del i
    return z


def _filter_body(z_ref, w1_ref, b1_ref, w2_ref, b2_ref, w3a_ref, w3b_ref, d_ref, hf_ref, hb_ref, s_ref):
    i = pl.program_id(0)
    half = z_ref.shape[0]
    z = z_ref[...]
    h = jnp.sin(jnp.dot(z, w1_ref[...], precision=HI, preferred_element_type=F32) + b1_ref[...])
    h = jnp.sin(jnp.dot(h, w2_ref[...], precision=HI, preferred_element_type=F32) + b2_ref[...])
    absd = jnp.abs(d_ref[...])
    h = h.astype(BF16)
    tot = None
    for part, (w3_ref, tcol) in enumerate(((w3a_ref, 0), (w3b_ref, 64))):
        hp = jnp.dot(h, w3_ref[...], preferred_element_type=F32)
        hp = hp * jnp.exp(-z[:, tcol:tcol + 1] * absd)
        rows = slice(part * half, (part + 1) * half)
        for o in range(2):
            c0 = 2 * D_GROUP * o
            hf_ref[rows, o * D_GROUP:(o + 1) * D_GROUP] = hp[:, c0:c0 + D_GROUP].astype(hf_ref.dtype)
            hb_ref[rows, o * D_GROUP:(o + 1) * D_GROUP] = hp[:, c0 + D_GROUP:c0 + 2 * D_GROUP].astype(hb_ref.dtype)
        part_sum = jnp.sum(jnp.abs(hp), axis=0, keepdims=True)
        tot = part_sum if tot is None else tot + part_sum

    @pl.when(i == 0)
    def _():
        s_ref[...] = jnp.zeros_like(s_ref)

    s_ref[...] += tot


def _hy_filter(n, w1, b1, w2, b2, w3, deltas, *, t2_major):
    tm = min(n, 1024)
    half = tm // 2
    z = _filter_features(n)
    if t2_major:
        z = z.reshape(n // DFT2, DFT2, 64).transpose(1, 0, 2).reshape(n, 64)
    zt = z.reshape(n // tm, 2, half, 64)
    z2 = jnp.asarray(np.concatenate([zt[:, 0], zt[:, 1]], axis=-1).reshape(n // 2, 2 * 64))
    w1p = jnp.zeros((64, HY_FFN), F32).at[:HY_EMB].set(w1)
    zero = jnp.zeros((64, HY_FFN), F32)
    bd = lambda w: jnp.concatenate([jnp.concatenate([w, zero], axis=1), jnp.concatenate([zero, w], axis=1)], axis=0)
    w3 = w3.astype(BF16)
    zero3 = jnp.zeros_like(w3)
    full = lambda shape: pl.BlockSpec(shape, lambda i: (0, 0))
    tok = pl.BlockSpec((tm, 2 * D_GROUP), lambda i: (i, 0))
    return pl.pallas_call(
        _filter_body,
        out_shape=[jax.ShapeDtypeStruct((n, 2 * D_GROUP), BF16), jax.ShapeDtypeStruct((n, 2 * D_GROUP), BF16),
                   jax.ShapeDtypeStruct((1, HY_FILTER_CH), F32)],
        grid=(n // tm,),
        in_specs=[pl.BlockSpec((half, 2 * 64), lambda i: (i, 0)), full((2 * 64, 2 * HY_FFN)), full((1, 2 * HY_FFN)),
                  full((2 * HY_FFN, 2 * HY_FFN)), full((1, 2 * HY_FFN)), full((2 * HY_FFN, HY_FILTER_CH)),
                  full((2 * HY_FFN, HY_FILTER_CH)), full((1, HY_FILTER_CH))],
        out_specs=[tok, tok, full((1, HY_FILTER_CH))],
        compiler_params=_cparams(("arbitrary",), 48),
        name="hyena_filter_mlp",
    )(z2, bd(w1p), jnp.tile(b1.reshape(1, -1), (1, 2)), bd(w2), jnp.tile(b2.reshape(1, -1), (1, 2)),
      jnp.concatenate([w3, zero3], axis=0), jnp.concatenate([zero3, w3], axis=0), deltas.reshape(1, -1))


@functools.lru_cache(maxsize=None)
def _dft_tables(n1):
    nn = n1 * DFT2
    j = np.arange(DFT2)
    th = 2.0 * np.pi * ((np.outer(j, j)) % DFT2) / DFT2
    fr, fi = np.cos(th), -np.sin(th)
    m_fwd = np.block([[fr, -fi], [fi, fr]])
    m_inv = np.block([[fr, fi], [-fi, fr]])
    hh = n1 // 2
    k1 = np.arange(n1)[None, :, None]
    t1 = np.arange(hh)[None, None, :]
    t2 = np.arange(DFT2)[:, None, None]
    ph = 2.0 * np.pi * ((k1 * (DFT2 * t1 + t2)) % nn) / nn
    ar, ai = np.cos(ph), -np.sin(ph)
    a_fwd = np.concatenate([np.concatenate([ar, -ai], axis=2), np.concatenate([ai, ar], axis=2)], axis=1)
    a_real = np.concatenate([ar, ai], axis=1)
    t1r = np.where(t2 >= 1, n1 - 1 - t1, (n1 - t1) % n1)
    phr = 2.0 * np.pi * ((k1 * (DFT2 * t1r + t2)) % nn) / nn
    a_rev = np.concatenate([np.cos(phr), -np.sin(phr)], axis=1)
    a_rev[0, :, 0] = 0.0
    pht = np.transpose(ph, (0, 2, 1))
    cr, ci = np.cos(pht) / nn, np.sin(pht) / nn
    a_inv = np.concatenate([np.concatenate([cr, -ci], axis=2), np.concatenate([ci, cr], axis=2)], axis=1)
    f32 = lambda a: np.ascontiguousarray(a, dtype=np.float32)
    return dict(m_fwd=f32(m_fwd), m_inv=f32(m_inv), a_fwd=f32(a_fwd), a_real=f32(a_real), a_rev=f32(a_rev),
                a_inv=f32(a_inv))


def _fft_a_body(x_ref, m_ref, o_ref, *, tj):
    n1 = o_ref.shape[2]
    for e in range(tj):
        x = jnp.concatenate([x_ref[0, :, e, :], x_ref[1, :, e, :]], axis=0).astype(BF16)
        a = jnp.dot(m_ref[e], x, preferred_element_type=F32)
        o_ref[0, e] = a[:n1]
        o_ref[1, e] = a[n1:]


def _fft_a(x4, m, *, n1, tj):
    c = x4.shape[3]
    return pl.pallas_call(
        functools.partial(_fft_a_body, tj=tj),
        out_shape=jax.ShapeDtypeStruct((2, DFT2, n1, c), F32),
        grid=(DFT2 // tj,),
        in_specs=[pl.BlockSpec((2, n1 // 2, tj, c), lambda j: (0, 0, j, 0)),
                  pl.BlockSpec((tj, 2 * n1, n1), lambda j: (j, 0, 0))],
        out_specs=pl.BlockSpec((2, tj, n1, c), lambda j: (0, j, 0, 0)),
        compiler_params=_cparams(("parallel",), 48),
        name="fft_stage_a",
    )(x4, m)


def _fft_b_body(a_ref, mf_ref, mi_ref, g_ref, o_ref, *, kg):
    for kk in range(kg):
        x = jnp.concatenate([a_ref[0, :, kk, :], a_ref[1, :, kk, :]], axis=0).astype(BF16)
        xf = jnp.dot(mf_ref[...], x, preferred_element_type=F32)
        xr, xi = xf[:DFT2], xf[DFT2:]
        gr, gi = g_ref[0, kk].astype(F32), g_ref[1, kk].astype(F32)
        y = jnp.concatenate([xr * gr - xi * gi, xr * gi + xi * gr], axis=0).astype(BF16)
        bf = jnp.dot(mi_ref[...], y, preferred_element_type=F32)
        o_ref[0, kk] = bf[:DFT2]
        o_ref[1, kk] = bf[DFT2:]


def _fft_b(a, g, order, tabs, *, n1):
    c = a.shape[3]
    kg = min(n1, FFT_GROUP_DATA)
    blk = pl.BlockSpec((2, kg, DFT2, c), lambda k: (0, k, 0, 0))
    mat = pl.BlockSpec((2 * DFT2, 2 * DFT2), lambda k: (0, 0))
    return pl.pallas_call(
        functools.partial(_fft_b_body, kg=kg),
        out_shape=jax.ShapeDtypeStruct((2, n1, DFT2, c), F32),
        grid=(n1 // kg,),
        in_specs=[pl.BlockSpec((2, DFT2, kg, c), lambda k: (0, 0, k, 0)), mat, mat,
                  pl.BlockSpec((2, kg, DFT2, c), lambda k: (0, k, 0, order))],
        out_specs=blk,
        compiler_params=_cparams(("parallel",), 56),
        name="fft_stage_b",
    )(a, tabs["m_fwd"], tabs["m_inv"], g)


def _fft_bf_body(a_ref, mf_ref, o_ref, *, kg):
    for kk in range(kg):
        x = jnp.concatenate([a_ref[0, :, kk, :], a_ref[1, :, kk, :]], axis=0).astype(BF16)
        xf = jnp.dot(mf_ref[...], x, preferred_element_type=F32)
        o_ref[0, kk] = xf[:DFT2].astype(BF16)
        o_ref[1, kk] = xf[DFT2:].astype(BF16)


def _fft_b_forward(a, tabs, *, n1):
    c = a.shape[3]
    kg = FFT_GROUP
    return pl.pallas_call(
        functools.partial(_fft_bf_body, kg=kg),
        out_shape=jax.ShapeDtypeStruct((2, n1, DFT2, c), BF16),
        grid=(n1 // kg,),
        in_specs=[pl.BlockSpec((2, DFT2, kg, c), lambda k: (0, 0, k, 0)),
                  pl.BlockSpec((2 * DFT2, 2 * DFT2), lambda k: (0, 0))],
        out_specs=pl.BlockSpec((2, kg, DFT2, c), lambda k: (0, k, 0, 0)),
        compiler_params=_cparams(("parallel",), 48),
        name="fft_stage_b_filter",
    )(a, tabs["m_fwd"])


def _fft_ai_body(b_ref, m_ref, v_ref, x_ref, bias_ref, o_ref, *, tj):
    h = o_ref.shape[1]
    for e in range(tj):
        b = jnp.concatenate([b_ref[0, :, e, :], b_ref[1, :, e, :]], axis=0).astype(BF16)
        y = jnp.dot(m_ref[e], b, preferred_element_type=F32)
        o_ref[0, :, e, :] = y[:h]
        o_ref[1, :, e, :] = y[h:]
    o_ref[...] = x_ref[...] * (o_ref[...] + v_ref[...] * bias_ref[...])


def _fft_a_inv(b, m, v4, xm4, bias, *, n1, tj):
    c = b.shape[3]
    half = pl.BlockSpec((2, n1 // 2, tj, c), lambda j: (0, 0, j, 0))
    return pl.pallas_call(
        functools.partial(_fft_ai_body, tj=tj),
        out_shape=jax.ShapeDtypeStruct((2, n1 // 2, DFT2, c), F32),
        grid=(DFT2 // tj,),
        in_specs=[pl.BlockSpec((2, n1, tj, c), lambda j: (0, 0, j, 0)),
                  pl.BlockSpec((tj, n1, 2 * n1), lambda j: (j, 0, 0)),
                  half, half, pl.BlockSpec((1, c), lambda j: (0, 0))],
        out_specs=half,
        compiler_params=_cparams(("parallel",), 56),
        name="fft_stage_a_inv",
    )(b, m, v4, xm4, bias)


def _filt_a_body(hj_ref, hz_ref, hr_ref, s_ref, mf_ref, mr_ref, o_ref, *, tj):
    n1 = o_ref.shape[2]
    s = s_ref[...]
    for e in range(tj):
        src = hz_ref if e == 0 else hr_ref
        me = 0 if e == 0 else tj - e
        for o in range(2):
            c0 = 2 * D_GROUP * o
            inv = 1.0 / (s[:, c0:c0 + D_GROUP] + s[:, c0 + D_GROUP:c0 + 2 * D_GROUP] + EPS)
            lsl = slice(o * D_GROUP, (o + 1) * D_GROUP)
            hf = hj_ref[e, :, lsl].astype(BF16)
            hb = src[me, :, lsl].astype(BF16)
            a = (jnp.dot(mf_ref[e], hf, preferred_element_type=F32)
                 + jnp.dot(mr_ref[e], hb, preferred_element_type=F32)) * inv
            o_ref[0, e, :, lsl] = a[:n1]
            o_ref[1, e, :, lsl] = a[n1:]


def _filt_a(hf, hb, colsum, tabs, *, n1):
    tj = FFT_GROUP
    nj = DFT2 // tj
    wd = 2 * D_GROUP
    shp = (DFT2, n1 // 2, wd)
    blk = lambda fn: pl.BlockSpec((tj, n1 // 2, wd), fn)
    mat = pl.BlockSpec((tj, 2 * n1, n1 // 2), lambda j: (j, 0, 0))
    return pl.pallas_call(
        functools.partial(_filt_a_body, tj=tj),
        out_shape=jax.ShapeDtypeStruct((2, DFT2, n1, wd), F32),
        grid=(nj,),
        in_specs=[blk(lambda j: (j, 0, 0)), blk(lambda j: ((nj - j) % nj, 0, 0)), blk(lambda j: (nj - 1 - j, 0, 0)),
                  pl.BlockSpec((1, HY_FILTER_CH), lambda j: (0, 0)), mat, mat],
        out_specs=pl.BlockSpec((2, tj, n1, wd), lambda j: (0, j, 0, 0)),
        compiler_params=_cparams(("parallel",), 48),
        name="filter_stage_a",
    )(hf.reshape(shp), hb.reshape(shp), hb.reshape(shp), colsum, tabs["a_real"], tabs["a_rev"])


def _hyena_long(v, x1, x2, hf, hb, colsum, bias, *, seq):
    n1 = 2 * seq // DFT2
    tabs = {k: jnp.asarray(a).astype(BF16) for k, a in _dft_tables(n1).items()}
    tj = FFT_GROUP_DATA
    fa = _filt_a(hf, hb, colsum, tabs, n1=n1)
    g = _fft_b_forward(fa, tabs, n1=n1)
    shp = (2, n1 // 2, DFT2, D_GROUP)
    z = v.reshape(shp)
    for order, xm in ((0, x1.reshape(shp)), (1, x2.reshape(shp))):
        a = _fft_a(z, tabs["a_fwd"], n1=n1, tj=tj)
        b = _fft_b(a, g, order, tabs, n1=n1)
        z = _fft_a_inv(b, tabs["a_inv"], z, xm, bias[order].reshape(1, D_GROUP), n1=n1, tj=tj)
    return z.reshape(2 * seq, D_GROUP)


def _hyena_ctx_body(v_ref, x1_ref, x2_ref, hf_ref, hb_ref, s_ref, bias_ref, cm_ref, sm_ref, ct_ref, st_ref, o_ref):
    n = v_ref.shape[1]
    cm, sm, ct, st = cm_ref[...], sm_ref[...], ct_ref[...], st_ref[...]
    s = s_ref[...]
    row0 = _iota((n, D_GROUP), 0) == 0
    dot = lambda a, b: jnp.dot(a, b.astype(BF16), preferred_element_type=F32)
    zr, zi = v_ref[0], v_ref[1]
    for order, xm in ((0, x1_ref), (1, x2_ref)):
        c0 = 2 * D_GROUP * order
        inv = 1.0 / (s[:, c0:c0 + D_GROUP] + s[:, c0 + D_GROUP:c0 + 2 * D_GROUP] + EPS)
        lsl = slice(order * D_GROUP, (order + 1) * D_GROUP)
        hf = hf_ref[:, lsl] * inv
        hb = jnp.where(row0, 0.0, hb_ref[:, lsl] * inv)
        gr = dot(cm, hf + hb)
        gi = dot(sm, hb - hf)
        xr = dot(cm, zr) + dot(sm, zi)
        xi = dot(cm, zi) - dot(sm, zr)
        yr = xr * gr - xi * gi
        yi = xr * gi + xi * gr
        cr = dot(ct, yr) - dot(st, yi)
        ci = dot(ct, yi) + dot(st, yr)
        bias = bias_ref[order:order + 1, :]
        zr = xm[0] * (cr + zr * bias)
        zi = xm[1] * (ci + zi * bias)
    o_ref[0] = zr
    o_ref[1] = zi


def _hyena_ctx(v, x1, x2, hf, hb, colsum, bias, *, seq):
    nn = 2 * seq
    k = np.arange(nn)
    t = np.arange(seq)
    th = 2.0 * np.pi * (np.outer(k, t) % nn) / nn
    cm, sm = np.cos(th), np.sin(th)
    consts = [jnp.asarray(a, F32).astype(BF16) for a in (cm, sm, cm.T / nn, sm.T / nn)]
    shp = (2, seq, D_GROUP)
    full3 = pl.BlockSpec(shp, lambda i: (0, 0, 0))
    f2 = lambda a: pl.BlockSpec(a.shape, lambda i: (0, 0))
    args = [hf, hb, colsum, bias] + consts
    out = pl.pallas_call(
        _hyena_ctx_body,
        out_shape=jax.ShapeDtypeStruct(shp, F32),
        grid=(1,),
        in_specs=[full3, full3, full3] + [f2(a) for a in args],
        out_specs=full3,
        compiler_params=_cparams(("arbitrary",)),
        name="hyena_ctx",
    )(v.reshape(shp), x1.reshape(shp), x2.reshape(shp), *args)
    return out.reshape(2 * seq, D_GROUP)


def _split_w_in(w_in):
    wb = w_in.astype(BF16)
    parts = dict(kq=jnp.concatenate([wb[:, COL_K:COL_V], wb[:, COL_Q:COL_R]], axis=1), v=wb[:, COL_V:COL_GF],
                 r=wb[:, COL_R:COL_POOL], pool=wb[:, COL_POOL:COL_HY], hy=wb[:, COL_HY:COL_CONV],
                 conv=wb[:, COL_CONV:P_IN])
    gate = jnp.pad(w_in[:, COL_GF:COL_Q], ((0, 0), (0, LANE - 2 * GLA_LOWRANK)))
    return parts, gate


def _mix(u, p, *, bsz, seq, is_ctx, states, filt):
    sf, sb = states
    cpb = min(16, seq // CHUNK)
    if is_ctx:
        pool = _pool1d(u["pool"], p["pool_wbd"], p["pool_scale"], bsz=bsz, seq=seq)
    else:
        pool = _pool2d(u["pool"], p["pool_wbd"], p["pool_scale"], bsz=bsz, seq=seq)
    tmc = min(seq, 2048)
    v, x1, x2 = _hy_short(u["hy"], p["hy_short_w"], p["hy_short_b"], seq=seq, tm=tmc)
    if is_ctx:
        hy = _hyena_ctx(v, x1, x2, *filt, p["hy_bias"], seq=seq)
    else:
        hy = _hyena_long(v, x1, x2, *filt, p["hy_bias"], seq=seq)
    gla = _gla_read(u["kq"], u["v"], u["kq"], u["r"], u["lf"], u["lb"], sf, sb, p["gla_ng"],
                    bsz=bsz, seq=seq, cpb=cpb)
    conv = _conformer(u["conv"], p["conv_dw_w"], p["conv_dw_b"], p["conv_ln_g"], p["conv_ln_b"], seq=seq, tm=tmc)
    return [pool, hy, gla, conv]


def kernel(x, c, ctx, c_ctx, ada_w, ada_b, ffn1_norm, ffn1_wi, ffn1_wo, mix_norm, w_in, w_out, pool_w, pool_scale, hy_short_w, hy_short_b, hy_w1, hy_b1, hy_w2, hy_b2, hy_w3, hy_deltas, hy_bias, gla_gw_f, gla_gb_f, gla_gw_b, gla_gb_b, gla_norm, conv_dw_w, conv_dw_b, conv_ln_g, conv_ln_b, ffn2_norm, ffn2_wi, ffn2_wo, final_norm):
    bsz, seq, d = x.shape
    clen = ctx.shape[1]
    depth = ada_w.shape[0]
    assert bsz == 2, "the Hyena transform packs exactly two batch rows into one complex signal"
    xs = x.reshape(bsz * seq, d)
    cs = ctx.reshape(bsz * clen, d)
    cc = jnp.concatenate([c, c_ctx[None, :], jnp.zeros((8 - bsz - 1, d), F32)], axis=0)
    mod = _modulation(cc, ada_w, ada_b).reshape(depth, 8, N_MOD, d)
    crow = bsz
    head_eye = np.kron(np.eye(len(POOL_WINDOWS), dtype=np.float32), np.ones((POOL_CH, POOL_CH), np.float32))
    tmx = TOKEN_TILE
    tps_x = seq // tmx
    tmc = bsz * clen
    zero_state = jnp.zeros((bsz, GLA_HEADS * GLA_DV, GLA_QK), F32)
    wi1, wo1, wi2, wo2, wout = (w.astype(BF16) for w in (ffn1_wi, ffn1_wo, ffn2_wi, ffn2_wo, w_out))

    for l in range(depth):
        last = l == depth - 1
        wparts, wgate = _split_w_in(w_in[l])
        zlow = jnp.zeros((GLA_LOWRANK, GLA_QK), F32)
        gw = jnp.concatenate([jnp.concatenate([gla_gw_f[l], zlow], axis=1),
                              jnp.concatenate([zlow, gla_gw_b[l]], axis=1),
                              jnp.zeros((LANE - 2 * GLA_LOWRANK, 2 * GLA_QK), F32)], axis=0)
        gb = jnp.concatenate([gla_gb_f[l], gla_gb_b[l]]).reshape(1, -1)
        gw = _gate_fold(wgate, gw).astype(BF16)
        wbd = jnp.tile(pool_w[l].reshape(D_GROUP, POOL_CH), (1, len(POOL_WINDOWS))) * head_eye
        p = dict(pool_wbd=wbd.astype(BF16), pool_scale=pool_scale[l].reshape(1, -1),
                 hy_short_w=hy_short_w[l], hy_short_b=hy_short_b[l], hy_bias=hy_bias[l],
                 gla_ng=jnp.tile(gla_norm[l], GLA_HEADS).reshape(1, -1),
                 conv_dw_w=conv_dw_w[l], conv_dw_b=conv_dw_b[l], conv_ln_g=conv_ln_g[l], conv_ln_b=conv_ln_b[l])
        names = ["kq", "v", "r", "pool", "hy", "conv"]

        xs, *outs = _ffn_proj(xs, mod, ffn1_norm[l], wi1, wo1, mix_norm[l], [wparts[nm] for nm in names], gw, gb,
                              layer=l, row0=0, tm=tmx, tiles_per_seq=tps_x)
        ux = dict(zip(names + ["lf", "lb"], outs))
        cnames = ["kq", "v"] if last else names
        cs, *outs = _ffn_proj(cs, mod, ffn1_norm[l], wi1, wo1, mix_norm[l], [wparts[nm] for nm in cnames], gw, gb,
                              layer=l, row0=crow, tm=tmc, tiles_per_seq=1)
        uc = dict(zip(cnames + ["lf", "lb"], outs))

        ccpb = clen // CHUNK
        sfc, sbc, finf, finb = _gla_states(uc["kq"], uc["v"], uc["lf"], uc["lb"], zero_state, zero_state,
                                           bsz=bsz, seq=clen, cpb=ccpb)
        sfx, sbx, _, _ = _gla_states(ux["kq"], ux["v"], ux["lf"], ux["lb"], finf, finb, bsz=bsz, seq=seq,
                                     cpb=16)

        filt = _hy_filter(seq, hy_w1[l], hy_b1[l], hy_w2[l], hy_b2[l], hy_w3[l], hy_deltas[l], t2_major=True)
        mixed = _mix(ux, p, bsz=bsz, seq=seq, is_ctx=False, states=(sfx, sbx), filt=filt)
        xs = _out_ffn(xs, mod, mixed, wout, ffn2_norm[l], wi2, wo2, layer=l, row0=0, tm=tmx,
                      tiles_per_seq=tps_x, final_g=final_norm if last else None)
        if not last:
            filt_c = _hy_filter(clen, hy_w1[l], hy_b1[l], hy_w2[l], hy_b2[l], hy_w3[l], hy_deltas[l],
                                t2_major=False)
            mixed = _mix(uc, p, bsz=bsz, seq=clen, is_ctx=True, states=(sfc, sbc), filt=filt_c)
            cs = _out_ffn(cs, mod, mixed, wout, ffn2_norm[l], wi2, wo2, layer=l, row0=crow, tm=tmc,
                          tiles_per_seq=1)
    return xs.reshape(bsz, seq, d)
```

```python
import functools
import math

import numpy as np
import jax
import jax.numpy as jnp
from jax import lax
from jax.experimental import pallas as pl
from jax.experimental.pallas import tpu as pltpu

F32 = jnp.float32
BF16 = jnp.bfloat16
HI = lax.Precision.HIGHEST

GRID_W = 64
D_GROUP = 256
N_MOD = 9
EPS = 1e-6
POOL_WINDOWS = (2, 4, 8, 16)
POOL_CH = 64
HY_BANDS = 16
HY_EMB = 1 + 2 * HY_BANDS
HY_FFN = 64
HY_FILTER_CH = 4 * D_GROUP
GLA_HEADS = 4
GLA_DK = 32
GLA_DV = 64
GLA_QK = 128
GLA_LOWRANK = 16
GLA_TAU = 16.0
CHUNK = 64
CONV_WIDTH = 31
HY_SHORT = 3

COL_K = 0
COL_V = COL_K + GLA_QK
COL_GF = COL_V + D_GROUP
COL_GB = COL_GF + GLA_LOWRANK
COL_Q = COL_GB + GLA_LOWRANK
COL_R = COL_Q + GLA_QK
COL_POOL = COL_R + D_GROUP
COL_HY = COL_POOL + D_GROUP
COL_CONV = COL_HY + 3 * D_GROUP
P_IN = COL_CONV + 2 * D_GROUP

LANE = 128
SUBLANES = 8
MIB = 1024 * 1024

DFT2 = 128
FFT_GROUP = 16
FFT_GROUP_DATA = 32
MOD_ROWS = 3
FF_CHUNK = 256
ROW_SUB = 256
TOKEN_TILE = 512


def _cparams(sem, vmem_mib=None):
    kw = dict(dimension_semantics=sem)
    if vmem_mib is not None:
        kw["vmem_limit_bytes"] = vmem_mib * MIB
    return pltpu.CompilerParams(**kw)


def _silu(x):
    return x * jax.nn.sigmoid(x)


def _rms_mod(h, g, m):
    y = h * lax.rsqrt(jnp.mean(h * h, axis=-1, keepdims=True) + EPS) * g
    return y * (1.0 + m[1:2, :]) + m[0:1, :]


def _iota(shape, dim):
    return lax.broadcasted_iota(jnp.int32, shape, dim)


def _mod_body(at_ref, w_ref, b_ref, o_ref):
    tk, nm = w_ref.shape[1], w_ref.shape[2]

    @pl.when(pl.program_id(1) == 0)
    def _():
        o_ref[0] = jnp.zeros((SUBLANES, nm), F32) + b_ref[0]

    at = _silu(at_ref[...])
    cols = [jnp.broadcast_to(at[:, r:r + 1], (tk, LANE)).reshape(tk // SUBLANES, SUBLANES, LANE)
            for r in range(MOD_ROWS)]
    for j in range(nm // LANE):
        lsl = slice(j * LANE, (j + 1) * LANE)
        w3 = w_ref[0, :, lsl].reshape(tk // SUBLANES, SUBLANES, LANE)
        for r in range(MOD_ROWS):
            part = jnp.sum(w3 * cols[r], axis=0)
            o_ref[0, r:r + 1, lsl] += jnp.sum(part, axis=0, keepdims=True)


def _modulation(cc, ada_w, ada_b):
    nl, d, nm = ada_w.shape
    tk = 256
    return pl.pallas_call(
        _mod_body,
        out_shape=jax.ShapeDtypeStruct((nl, 8, nm), F32),
        grid=(nl, d // tk),
        in_specs=[pl.BlockSpec((tk, 8), lambda l, j: (j, 0)),
                  pl.BlockSpec((1, tk, nm), lambda l, j: (l, j, 0)),
                  pl.BlockSpec((1, 1, nm), lambda l, j: (l, 0, 0))],
        out_specs=pl.BlockSpec((1, 8, nm), lambda l, j: (l, 0, 0)),
        compiler_params=_cparams(("parallel", "arbitrary"), 40),
        name="adaln_mod",
    )(cc.T, ada_w, ada_b.reshape(nl, 1, nm))


def _swiglu(xn, wi_ref, wo_ref, hm_ref, rows):
    ff = wo_ref.shape[0]
    for c in range(0, ff, FF_CHUNK):
        a = jnp.dot(xn, wi_ref[:, c:c + FF_CHUNK], preferred_element_type=F32)
        g = jnp.dot(xn, wi_ref[:, ff + c:ff + c + FF_CHUNK], preferred_element_type=F32)
        hm_ref[rows, c:c + FF_CHUNK] = (_silu(g) * a).astype(BF16)
    return jnp.dot(hm_ref[rows, :], wo_ref[...], preferred_element_type=F32)


def _resident(shape):
    nd = len(shape)
    return pl.BlockSpec(shape, lambda i: (0,) * nd, pipeline_mode=pl.Buffered(1))


def _resident_layer(shape, layer):
    nd = len(shape)
    return pl.BlockSpec((None,) + tuple(shape), lambda i: (layer,) + (0,) * nd, pipeline_mode=pl.Buffered(1))


def _gate_fold_body(wg_ref, gw_ref, o_ref):
    o_ref[...] = jnp.dot(wg_ref[...], gw_ref[...], precision=HI, preferred_element_type=F32)


def _gate_fold(w_gate, gw):
    d = w_gate.shape[0]
    return pl.pallas_call(
        _gate_fold_body,
        out_shape=jax.ShapeDtypeStruct((d, gw.shape[1]), F32),
        grid=(1,),
        in_specs=[pl.BlockSpec(w_gate.shape, lambda i: (0, 0)), pl.BlockSpec(gw.shape, lambda i: (0, 0))],
        out_specs=pl.BlockSpec((d, gw.shape[1]), lambda i: (0, 0)),
        name="gate_fold",
    )(w_gate, gw)


def _chunk_prefix(x):
    pos = _iota(x.shape, 0) & (CHUNK - 1)
    shift = 1
    while shift < CHUNK:
        x = x + jnp.where(pos >= shift, pltpu.roll(x, shift, 0), 0.0)
        shift *= 2
    return x


def _ffn_proj_body(h_ref, m_ref, g1_ref, wi_ref, wo_ref, g2_ref, *rest, nparts):
    w_refs = rest[:nparts]
    gw_ref, gb_ref = rest[nparts:nparts + 2]
    x_ref = rest[nparts + 2]
    o_refs = rest[nparts + 3:2 * nparts + 5]
    hm_ref = rest[-1]
    m = m_ref[0]
    for r0 in range(0, h_ref.shape[0], ROW_SUB):
        rows = slice(r0, r0 + ROW_SUB)
        h = h_ref[rows, :]
        xn = _rms_mod(h, g1_ref[...], m[0:2]).astype(BF16)
        x1 = h + (0.5 * m[2:3, :]) * _swiglu(xn, wi_ref, wo_ref, hm_ref, rows)
        x_ref[rows, :] = x1
        xn2 = _rms_mod(x1, g2_ref[...], m[3:5]).astype(BF16)
        for w_ref, o_ref in zip(w_refs, o_refs[:nparts]):
            o_ref[rows, :] = jnp.dot(xn2, w_ref[...], preferred_element_type=F32).astype(o_ref.dtype)
        a = jnp.dot(xn2, gw_ref[...], preferred_element_type=F32) + gb_ref[...]
        ls = (jnp.minimum(a, 0.0) - jnp.log(1.0 + jnp.exp(-jnp.abs(a)))) * (1.0 / GLA_TAU)
        pre = _chunk_prefix(ls)
        o_refs[nparts][rows, :] = pre[:, :GLA_QK]
        pb = pre[:, GLA_QK:]
        tot = jnp.concatenate([jnp.broadcast_to(pb[c0 + CHUNK - 1:c0 + CHUNK, :], (CHUNK, GLA_QK))
                               for c0 in range(0, ROW_SUB, CHUNK)], axis=0)
        o_refs[nparts + 1][rows, :] = tot - pb + ls[:, GLA_QK:]


def _mod_spec(d, layer, row0, tiles_per_seq):
    return pl.BlockSpec((None, 1, N_MOD, d), lambda i: (layer, row0 + i // tiles_per_seq, 0, 0))


def _ffn_proj(h, mod, g1, wi, wo, g2, w_parts, gw, gb, *, layer, row0, tm, tiles_per_seq):
    n, d = h.shape
    ff = wo.shape[1]
    nparts = len(w_parts)
    widths = [w.shape[1] for w in w_parts]
    tok = lambda wd: pl.BlockSpec((tm, wd), lambda i: (i, 0))
    in_specs = [tok(d), _mod_spec(d, layer, row0, tiles_per_seq), _resident((1, d)),
                _resident_layer((d, 2 * ff), layer), _resident_layer((ff, d), layer), _resident((1, d))]
    in_specs += [_resident((d, wd)) for wd in widths]
    in_specs += [_resident((d, 2 * GLA_QK)), _resident((1, 2 * GLA_QK))]
    out_shape = [jax.ShapeDtypeStruct((n, d), F32)]
    out_shape += [jax.ShapeDtypeStruct((n, wd), BF16) for wd in widths]
    out_shape += [jax.ShapeDtypeStruct((n, GLA_QK), F32)] * 2
    return pl.pallas_call(
        functools.partial(_ffn_proj_body, nparts=nparts),
        out_shape=out_shape,
        grid=(n // tm,),
        in_specs=in_specs,
        out_specs=[tok(d)] + [tok(wd) for wd in widths] + [tok(GLA_QK)] * 2,
        scratch_shapes=[pltpu.VMEM((tm, ff), BF16)],
        compiler_params=_cparams(("parallel",), 56),
        name="ffn1_in_proj",
    )(h, mod, g1.reshape(1, d), wi, wo, g2.reshape(1, d), *w_parts, gw, gb)


def _out_ffn_body(x_ref, m_ref, p_ref, hy_ref, gl_ref, cv_ref, wout_ref, g_ref, wi_ref, wo_ref, fn_ref, o_ref,
                  hm_ref, *, final, row_sub):
    m = m_ref[0]
    for r0 in range(0, x_ref.shape[0], row_sub):
        rows = slice(r0, r0 + row_sub)
        acc = jnp.dot(p_ref[rows, :].astype(BF16), wout_ref[0:D_GROUP, :], preferred_element_type=F32)
        acc += jnp.dot(hy_ref[rows, :].astype(BF16), wout_ref[D_GROUP:2 * D_GROUP, :], preferred_element_type=F32)
        acc += jnp.dot(gl_ref[rows, :].astype(BF16), wout_ref[2 * D_GROUP:3 * D_GROUP, :],
                       preferred_element_type=F32)
        acc += jnp.dot(cv_ref[rows, :].astype(BF16), wout_ref[3 * D_GROUP:, :], preferred_element_type=F32)
        x2 = x_ref[rows, :] + m[5:6, :] * acc
        xn = _rms_mod(x2, g_ref[...], m[6:8]).astype(BF16)
        out = x2 + (0.5 * m[8:9, :]) * _swiglu(xn, wi_ref, wo_ref, hm_ref, rows)
        if final:
            out = out * lax.rsqrt(jnp.mean(out * out, axis=-1, keepdims=True) + EPS) * fn_ref[...]
        o_ref[rows, :] = out


def _out_ffn(x, mod, parts, w_out, g, wi, wo, *, layer, row0, tm, tiles_per_seq, final_g=None):
    n, d = x.shape
    ff = wo.shape[1]
    final = final_g is not None
    fg = final_g if final else g
    tok = lambda wd: pl.BlockSpec((tm, wd), lambda i: (i, 0))
    return pl.pallas_call(
        functools.partial(_out_ffn_body, final=final, row_sub=min(tm, 2 * ROW_SUB)),
        out_shape=jax.ShapeDtypeStruct((n, d), F32),
        grid=(n // tm,),
        in_specs=[tok(d), _mod_spec(d, layer, row0, tiles_per_seq),
                  tok(D_GROUP), tok(D_GROUP), tok(D_GROUP), tok(D_GROUP), _resident_layer((d, d), layer),
                  _resident((1, d)), _resident_layer((d, 2 * ff), layer), _resident_layer((ff, d), layer),
                  _resident((1, d))],
        out_specs=tok(d),
        scratch_shapes=[pltpu.VMEM((tm, ff), BF16)],
        compiler_params=_cparams(("parallel",), 56),
        name="out_proj_ffn2",
    )(x, mod, *parts, w_out, g.reshape(1, d), wi, wo, fg.reshape(1, d))


def _gla_state_body(kf, vf, bf, kb, vb, bb, s0f, s0b, sf_o, sb_o, ff_o, fb_o, stf, stb, *, cpb):
    i = pl.program_id(1)

    @pl.when(i == 0)
    def _():
        stf[...] = s0f[0]
        stb[...] = s0b[0]

    sshape = (GLA_HEADS * GLA_DV, GLA_QK)
    bmask = (_iota(sshape, 0) >> 6) == (_iota(sshape, 1) >> 5)
    tn_dims = (((0,), (0,)), ((), ()))

    def direction(k_ref, v_ref, b_ref, last, st, s_o, order):
        s = st[...]
        for ci in order:
            sl = slice(ci * CHUNK, (ci + 1) * CHUNK)
            b = b_ref[sl, :]
            tot = b[last:last + 1, :]
            kd = (k_ref[sl, :] * jnp.exp(tot - b)).astype(BF16)
            upd = lax.dot_general(v_ref[sl, :].astype(BF16), kd, tn_dims, preferred_element_type=F32)
            s_o[0, ci] = s.astype(BF16)
            s = s * jnp.exp(tot) + jnp.where(bmask, upd, 0.0)
        st[...] = s

    direction(kf, vf, bf, CHUNK - 1, stf, sf_o, range(cpb))
    direction(kb, vb, bb, 0, stb, sb_o, range(cpb - 1, -1, -1))
    ff_o[0] = stf[...]
    fb_o[0] = stb[...]


def _gla_states(k, v, lf, lb, s0f, s0b, *, bsz, seq, cpb):
    bt = cpb * CHUNK
    nb = seq // bt
    nc = seq // CHUNK
    srow = GLA_HEADS * GLA_DV

    def tf(b, i):
        return (b * nb + i, 0)

    def tb(b, i):
        return (b * nb + nb - 1 - i, 0)

    sblk = pl.BlockSpec((1, srow, GLA_QK), lambda b, i: (b, 0, 0))
    return pl.pallas_call(
        functools.partial(_gla_state_body, cpb=cpb),
        out_shape=[jax.ShapeDtypeStruct((bsz, nc, srow, GLA_QK), BF16),
                   jax.ShapeDtypeStruct((bsz, nc, srow, GLA_QK), BF16),
                   jax.ShapeDtypeStruct((bsz, srow, GLA_QK), F32),
                   jax.ShapeDtypeStruct((bsz, srow, GLA_QK), F32)],
        grid=(bsz, nb),
        in_specs=[pl.BlockSpec((bt, GLA_QK), tf), pl.BlockSpec((bt, D_GROUP), tf), pl.BlockSpec((bt, GLA_QK), tf),
                  pl.BlockSpec((bt, GLA_QK), tb), pl.BlockSpec((bt, D_GROUP), tb), pl.BlockSpec((bt, GLA_QK), tb),
                  sblk, sblk],
        out_specs=[pl.BlockSpec((1, cpb, srow, GLA_QK), lambda b, i: (b, i, 0, 0)),
                   pl.BlockSpec((1, cpb, srow, GLA_QK), lambda b, i: (b, nb - 1 - i, 0, 0)),
                   sblk, sblk],
        scratch_shapes=[pltpu.VMEM((srow, GLA_QK), F32), pltpu.VMEM((srow, GLA_QK), F32)],
        compiler_params=_cparams(("parallel", "arbitrary")),
        name="gla_states",
    )(k, v, lf, k, v, lb, s0f, s0b)


def _gla_read_body(k_ref, v_ref, q_ref, r_ref, bf_ref, bb_ref, sf_ref, sb_ref, ng_ref, o_ref, acc_ref, *, cpb):
    hrows = GLA_HEADS * CHUNK
    cpos = _iota((hrows, CHUNK), 0) & (CHUNK - 1)
    ccol = _iota((hrows, CHUNK), 1)
    lowm = cpos >= ccol
    upm = cpos <= ccol
    hq = (_iota((hrows, GLA_QK), 0) >> 6) == (_iota((hrows, GLA_QK), 1) >> 5)
    ho = (_iota((hrows, D_GROUP), 0) >> 6) == (_iota((hrows, D_GROUP), 1) >> 6)
    bavg = jnp.where((_iota((D_GROUP, D_GROUP), 0) >> 6) == (_iota((D_GROUP, D_GROUP), 1) >> 6),
                     1.0 / GLA_DV, 0.0).astype(BF16)
    nt_dims = (((1,), (1,)), ((), ()))

    bf_ = bf_ref[...]
    bb_ = bb_ref[...]
    qs = q_ref[...].astype(F32) * (GLA_DK ** -0.5)
    kk = k_ref[...]
    qef = (qs * jnp.exp(bf_)).astype(BF16)
    qeb = (qs * jnp.exp(bb_)).astype(BF16)
    kef = (kk * jnp.exp(-bf_)).astype(BF16)
    keb = (kk * jnp.exp(-bb_)).astype(BF16)
    vb16 = v_ref[...].astype(BF16)
    zero = jnp.zeros((), BF16)
    for ci in range(cpb):
        sl = slice(ci * CHUNK, (ci + 1) * CHUNK)
        qf4 = jnp.where(hq, jnp.concatenate([qef[sl]] * GLA_HEADS, axis=0), zero)
        qb4 = jnp.where(hq, jnp.concatenate([qeb[sl]] * GLA_HEADS, axis=0), zero)
        af = lax.dot_general(qf4, kef[sl], nt_dims, preferred_element_type=F32)
        ab = lax.dot_general(qb4, keb[sl], nt_dims, preferred_element_type=F32)
        att = (jnp.where(lowm, af, 0.0) + jnp.where(upm, ab, 0.0)).astype(BF16)
        oall = jnp.dot(att, vb16[sl], preferred_element_type=F32)
        om = jnp.where(ho, oall, 0.0)
        o = om[0:CHUNK] + om[CHUNK:2 * CHUNK] + om[2 * CHUNK:3 * CHUNK] + om[3 * CHUNK:4 * CHUNK]
        qcat = jnp.concatenate([qef[sl], qeb[sl]], axis=1)
        scat = jnp.concatenate([sf_ref[0, ci], sb_ref[0, ci]], axis=1)
        acc_ref[sl, :] = o + lax.dot_general(qcat, scat, nt_dims, preferred_element_type=F32)
    o = acc_ref[...]
    ms = _chunk_sum_rhs(o * o, bavg)
    o_ref[...] = (o * lax.rsqrt(ms + EPS) * ng_ref[...] * _silu(r_ref[...].astype(F32))).astype(o_ref.dtype)


def _chunk_sum_rhs(x, mat):
    hi, lo = _split_bf16(x)
    return jnp.dot(hi, mat, preferred_element_type=F32) + jnp.dot(lo, mat, preferred_element_type=F32)


def _gla_read(k, v, q, r, lf, lb, sf, sb, ng, *, bsz, seq, cpb):
    bt = cpb * CHUNK
    nb = seq // bt
    srow = GLA_HEADS * GLA_DV
    n = bsz * seq

    def tk(i):
        return (i, 0)

    sspec = pl.BlockSpec((1, cpb, srow, GLA_QK), lambda i: (i // nb, i % nb, 0, 0))
    return pl.pallas_call(
        functools.partial(_gla_read_body, cpb=cpb),
        out_shape=jax.ShapeDtypeStruct((n, D_GROUP), BF16),
        scratch_shapes=[pltpu.VMEM((bt, D_GROUP), F32)],
        grid=(bsz * nb,),
        in_specs=[pl.BlockSpec((bt, GLA_QK), tk), pl.BlockSpec((bt, D_GROUP), tk),
                  pl.BlockSpec((bt, GLA_QK), lambda i: (i, 1)), pl.BlockSpec((bt, D_GROUP), tk),
                  pl.BlockSpec((bt, GLA_QK), tk), pl.BlockSpec((bt, GLA_QK), tk),
                  sspec, sspec, pl.BlockSpec((1, D_GROUP), lambda i: (0, 0))],
        out_specs=pl.BlockSpec((bt, D_GROUP), tk),
        compiler_params=_cparams(("parallel",)),
        name="gla_readout",
    )(k, v, q, r, lf, lb, sf, sb, ng)


def _box_matrix(n, w):
    pos = np.arange(n)
    lo = np.clip(pos - w // 2, 0, n)
    hi = np.clip(pos - w // 2 + w, 0, n)
    col = np.arange(n)[None, :]
    return ((col >= lo[:, None]) & (col < hi[:, None])).astype(np.float32)


def _lane_windows(shape):
    w = jnp.left_shift(2, _iota(shape, 1) >> 6)
    return w, w >> 1


def _box_count(pos, w, half, n):
    return jnp.minimum(pos - half + w, n) - jnp.maximum(pos - half, 0)


def _split_bf16(x):
    hi = x.astype(BF16)
    lo = (x - hi.astype(F32)).astype(BF16)
    return hi, lo


def _pool2d_body(cur_ref, prev_ref, next_ref, pc_ref, w_ref, sc_ref, o_ref, ycol, *, tiles, rows):
    i = pl.program_id(1)
    tm = cur_ref.shape[0]
    hb = prev_ref.shape[0]
    sub = 2 * GRID_W
    pflag = jnp.where(i > 0, 1.0, 0.0)
    nflag = jnp.where(i < tiles - 1, 1.0, 0.0)

    def colpool(x):
        halves = []
        for half in range(2):
            lsl = slice(half * LANE, (half + 1) * LANE)
            ys = [jnp.dot(pc_ref[wi], x[:, lsl], preferred_element_type=F32) for wi in (2 * half, 2 * half + 1)]
            lane = _iota((sub, LANE), 1)
            halves.append(jnp.where(lane < POOL_CH, ys[0], ys[1]))
        return jnp.concatenate(halves, axis=1)

    for s in range(hb // sub):
        ycol[s * sub:(s + 1) * sub, :] = colpool(prev_ref[s * sub:(s + 1) * sub, :]) * pflag
    for s in range(tm // sub):
        ycol[hb + s * sub:hb + (s + 1) * sub, :] = colpool(cur_ref[s * sub:(s + 1) * sub, :])
    for s in range(hb // sub):
        ycol[hb + tm + s * sub:hb + tm + (s + 1) * sub, :] = colpool(next_ref[s * sub:(s + 1) * sub, :]) * nflag

    rc = 256
    wl, half = _lane_windows((rc, D_GROUP))
    narrow = _iota((rc, LANE), 1) < POOL_CH
    for r0 in range(0, tm, rc):
        def band(lo, hi, lsl):
            base = hb + r0
            acc = ycol[base + GRID_W * lo:base + GRID_W * lo + rc, lsl]
            for dd in range(lo + 1, hi):
                acc = acc + ycol[base + GRID_W * dd:base + GRID_W * dd + rc, lsl]
            return acc

        left, right = slice(0, LANE), slice(LANE, 2 * LANE)
        z2 = band(-1, 1, left)
        z4 = z2 + band(-2, -1, left) + band(1, 2, left)
        z8 = band(-4, 4, right)
        z16 = z8 + band(-8, -4, right) + band(4, 8, right)
        z = jnp.concatenate([jnp.where(narrow, z2, z4), jnp.where(narrow, z8, z16)], axis=1)
        tok = _iota((rc, D_GROUP), 0) + (i * tm + r0)
        rcnt = _box_count(tok >> 6, wl, half, rows)
        ccnt = _box_count(tok & (GRID_W - 1), wl, half, GRID_W)
        pooled = z / (rcnt * ccnt).astype(F32)
        dlt = (pooled - cur_ref[r0:r0 + rc, :]).astype(BF16)
        o_ref[r0:r0 + rc, :] = (jnp.dot(dlt, w_ref[...], preferred_element_type=F32) * sc_ref[...]).astype(o_ref.dtype)


def _pool2d(u, wbd, scale, *, bsz, seq):
    tm = 1024
    hb = 512
    tiles = seq // tm
    r = tm // hb
    nhb = seq // hb
    pc = np.stack([np.kron(np.eye(2, dtype=np.float32), _box_matrix(GRID_W, w)) for w in POOL_WINDOWS])
    return pl.pallas_call(
        functools.partial(_pool2d_body, tiles=tiles, rows=seq // GRID_W),
        out_shape=jax.ShapeDtypeStruct((bsz * seq, D_GROUP), BF16),
        grid=(bsz, tiles),
        in_specs=[pl.BlockSpec((tm, D_GROUP), lambda b, i: (b * tiles + i, 0)),
                  pl.BlockSpec((hb, D_GROUP), lambda b, i: (b * nhb + jnp.maximum(i * r - 1, 0), 0)),
                  pl.BlockSpec((hb, D_GROUP), lambda b, i: (b * nhb + jnp.minimum(i * r + r, nhb - 1), 0)),
                  pl.BlockSpec((4, 2 * GRID_W, 2 * GRID_W), lambda b, i: (0, 0, 0)),
                  pl.BlockSpec((D_GROUP, D_GROUP), lambda b, i: (0, 0)),
                  pl.BlockSpec((1, D_GROUP), lambda b, i: (0, 0))],
        out_specs=pl.BlockSpec((tm, D_GROUP), lambda b, i: (b * tiles + i, 0)),
        scratch_shapes=[pltpu.VMEM((tm + 2 * hb, D_GROUP), F32)],
        compiler_params=_cparams(("parallel", "parallel")),
        name="pool2d",
    )(u, u, u, jnp.asarray(pc, BF16), wbd, scale)


def _pool1d_body(x_ref, p_ref, w_ref, sc_ref, o_ref):
    x = x_ref[...]
    n = x.shape[0]
    ys = [jnp.dot(p_ref[wi], x, preferred_element_type=F32) for wi in range(4)]
    wl, half = _lane_windows((n, D_GROUP))
    z = jnp.where(wl == 2, ys[0], jnp.where(wl == 4, ys[1], jnp.where(wl == 8, ys[2], ys[3])))
    cnt = _box_count(_iota((n, D_GROUP), 0), wl, half, n)
    dlt = (z / cnt.astype(F32) - x).astype(BF16)
    o_ref[...] = (jnp.dot(dlt, w_ref[...], preferred_element_type=F32) * sc_ref[...]).astype(o_ref.dtype)


def _pool1d(u, wbd, scale, *, bsz, seq):
    pm = np.stack([_box_matrix(seq, w) for w in POOL_WINDOWS])
    return pl.pallas_call(
        _pool1d_body,
        out_shape=jax.ShapeDtypeStruct((bsz * seq, D_GROUP), BF16),
        grid=(bsz,),
        in_specs=[pl.BlockSpec((seq, D_GROUP), lambda b: (b, 0)),
                  pl.BlockSpec((4, seq, seq), lambda b: (0, 0, 0)),
                  pl.BlockSpec((D_GROUP, D_GROUP), lambda b: (0, 0)),
                  pl.BlockSpec((1, D_GROUP), lambda b: (0, 0))],
        out_specs=pl.BlockSpec((seq, D_GROUP), lambda b: (b, 0)),
        compiler_params=_cparams(("parallel",)),
        name="pool1d",
    )(u, jnp.asarray(pm, BF16), wbd, scale)


def _fill_halo(buf, cur, prev, nxt, i, tps, hb, tm, pre):
    first = (i % tps) == 0
    last = (i % tps) == tps - 1
    buf[0:hb, :] = jnp.where(first, 0.0, pre(prev[...]))
    buf[hb:hb + tm, :] = pre(cur[...])
    buf[hb + tm:hb + tm + hb, :] = jnp.where(last, 0.0, pre(nxt[...]))


def _tap_phases(taps, off):
    return sorted({(off + j) % SUBLANES for j in range(taps)} - {0})


def _fill_phases(sh, buf, phases):
    rows = sh.shape[1]
    for slot, s in enumerate(phases):
        sh[slot, :, :] = buf[s:s + rows, :]


def _dwconv(buf, sh, phases, w_ref, r0, rc, taps, off):
    acc = None
    for j in range(taps):
        s, q = (off + j) % SUBLANES, (off + j) // SUBLANES
        lo = r0 + SUBLANES * q
        src = buf[lo:lo + rc, :] if s == 0 else sh[phases.index(s), lo:lo + rc, :]
        term = src * w_ref[j:j + 1, :]
        acc = term if acc is None else acc + term
    return acc


def _conf_body(cur, prev, nxt, w_ref, b_ref, lg_ref, lb_ref, o_ref, buf, sh, *, tps):
    i = pl.program_id(0)
    tm = cur.shape[0]
    hb = prev.shape[0]

    def glu(u):
        u = u.astype(F32)
        return u[:, :D_GROUP] * jax.nn.sigmoid(u[:, D_GROUP:])

    _fill_halo(buf, cur, prev, nxt, i, tps, hb, tm, glu)
    rc = 128
    off = hb - (CONV_WIDTH - 1) // 2
    phases = _tap_phases(CONV_WIDTH, off)
    _fill_phases(sh, buf, phases)
    for r0 in range(0, tm, rc):
        h = _dwconv(buf, sh, phases, w_ref, r0, rc, CONV_WIDTH, off) + b_ref[...]
        mu = jnp.mean(h, axis=-1, keepdims=True)
        hc = h - mu
        var = jnp.mean(hc * hc, axis=-1, keepdims=True)
        o_ref[r0:r0 + rc, :] = _silu(hc * lax.rsqrt(var + EPS) * lg_ref[...] + lb_ref[...]).astype(o_ref.dtype)


def _halo_specs(tm, hb, width, nrows):
    r = tm // hb
    nhb = nrows // hb
    return [pl.BlockSpec((tm, width), lambda i: (i, 0)),
            pl.BlockSpec((hb, width), lambda i: (jnp.maximum(i * r - 1, 0), 0)),
            pl.BlockSpec((hb, width), lambda i: (jnp.minimum(i * r + r, nhb - 1), 0))]


def _conformer(u, w, b, lg, lb, *, seq, tm):
    n = u.shape[0]
    hb = 16
    nph = len(_tap_phases(CONV_WIDTH, hb - (CONV_WIDTH - 1) // 2))
    vec = pl.BlockSpec((1, D_GROUP), lambda i: (0, 0))
    return pl.pallas_call(
        functools.partial(_conf_body, tps=seq // tm),
        out_shape=jax.ShapeDtypeStruct((n, D_GROUP), BF16),
        grid=(n // tm,),
        in_specs=_halo_specs(tm, hb, 2 * D_GROUP, n) + [pl.BlockSpec((CONV_WIDTH, D_GROUP), lambda i: (0, 0)),
                                                        vec, vec, vec],
        out_specs=pl.BlockSpec((tm, D_GROUP), lambda i: (i, 0)),
        scratch_shapes=[pltpu.VMEM((tm + 2 * hb, D_GROUP), F32),
                        pltpu.VMEM((nph, tm + 2 * hb - SUBLANES, D_GROUP), F32)],
        compiler_params=_cparams(("parallel",), 48),
        name="conformer_conv",
    )(u, u, u, w, b.reshape(1, -1), lg.reshape(1, -1), lb.reshape(1, -1))


def _short_body(cur, prev, nxt, w_ref, b_ref, v_ref, x1_ref, x2_ref, buf, sh, *, tps):
    i = pl.program_id(0)
    tm = cur.shape[0]
    hb = prev.shape[0]
    _fill_halo(buf, cur, prev, nxt, i, tps, hb, tm, lambda u: u.astype(F32))
    rc = 128
    off = hb - (HY_SHORT - 1) // 2
    phases = _tap_phases(HY_SHORT, off)
    _fill_phases(sh, buf, phases)
    for r0 in range(0, tm, rc):
        uc = _dwconv(buf, sh, phases, w_ref, r0, rc, HY_SHORT, off) + b_ref[...]
        v_ref[r0:r0 + rc, :] = uc[:, :D_GROUP]
        x1_ref[r0:r0 + rc, :] = uc[:, D_GROUP:2 * D_GROUP].astype(x1_ref.dtype)
        x2_ref[r0:r0 + rc, :] = uc[:, 2 * D_GROUP:].astype(x2_ref.dtype)


def _hy_short(u, w, b, *, seq, tm):
    n = u.shape[0]
    hb = 16
    wd = 3 * D_GROUP
    nph = len(_tap_phases(HY_SHORT, hb - (HY_SHORT - 1) // 2))
    ospec = pl.BlockSpec((tm, D_GROUP), lambda i: (i, 0))
    return pl.pallas_call(
        functools.partial(_short_body, tps=seq // tm),
        out_shape=[jax.ShapeDtypeStruct((n, D_GROUP), dt) for dt in (F32, BF16, BF16)],
        grid=(n // tm,),
        in_specs=_halo_specs(tm, hb, wd, n) + [pl.BlockSpec((HY_SHORT, wd), lambda i: (0, 0)),
                                               pl.BlockSpec((1, wd), lambda i: (0, 0))],
        out_specs=[ospec, ospec, ospec],
        scratch_shapes=[pltpu.VMEM((tm + 2 * hb, wd), F32),
                        pltpu.VMEM((nph, tm + 2 * hb - SUBLANES, wd), F32)],
        compiler_params=_cparams(("parallel",), 48),
        name="hyena_short_conv",
    )(u, u, u, w, b.reshape(1, -1))


def _filter_features(n):
    i = np.arange(n, dtype=np.float64)
    t = np.linspace(0.0, 1.0, n, dtype=np.float32).astype(np.float64)
    wpos = ((2.0 * math.pi / n) * np.arange(n, dtype=np.float32)).astype(np.float32)
    bands = np.linspace(1e-4, HY_BANDS - 1, HY_BANDS, dtype=np.float32)
    arg = (bands[None, :] * wpos[:, None]).astype(np.float32).astype(np.float64)
    z = np.zeros((n, 64), np.float32)
    z[:, 0] = t
    z[:, 1:1 + HY_BANDS] = np.cos(arg)
    z[:, 1 + HY_BANDS:HY_EMB] = -np.sin(arg)
    del i
    return z


def _filter_body(z_ref, w1_ref, b1_ref, w2_ref, b2_ref, w3a_ref, w3b_ref, d_ref, hf_ref, hb_ref, s_ref):
    i = pl.program_id(0)
    half = z_ref.shape[0]
    z = z_ref[...]
    h = jnp.sin(jnp.dot(z, w1_ref[...], precision=HI, preferred_element_type=F32) + b1_ref[...])
    h = jnp.sin(jnp.dot(h, w2_ref[...], precision=HI, preferred_element_type=F32) + b2_ref[...])
    absd = jnp.abs(d_ref[...])
    h = h.astype(BF16)
    tot = None
    for part, (w3_ref, tcol) in enumerate(((w3a_ref, 0), (w3b_ref, 64))):
        hp = jnp.dot(h, w3_ref[...], preferred_element_type=F32)
        hp = hp * jnp.exp(-z[:, tcol:tcol + 1] * absd)
        rows = slice(part * half, (part + 1) * half)
        for o in range(2):
            c0 = 2 * D_GROUP * o
            hf_ref[rows, o * D_GROUP:(o + 1) * D_GROUP] = hp[:, c0:c0 + D_GROUP].astype(hf_ref.dtype)
            hb_ref[rows, o * D_GROUP:(o + 1) * D_GROUP] = hp[:, c0 + D_GROUP:c0 + 2 * D_GROUP].astype(hb_ref.dtype)
        part_sum = jnp.sum(jnp.abs(hp), axis=0, keepdims=True)
        tot = part_sum if tot is None else tot + part_sum

    @pl.when(i == 0)
    def _():
        s_ref[...] = jnp.zeros_like(s_ref)

    s_ref[...] += tot


def _hy_filter(n, w1, b1, w2, b2, w3, deltas, *, t2_major):
    tm = min(n, 1024)
    half = tm // 2
    z = _filter_features(n)
    if t2_major:
        z = z.reshape(n // DFT2, DFT2, 64).transpose(1, 0, 2).reshape(n, 64)
    zt = z.reshape(n // tm, 2, half, 64)
    z2 = jnp.asarray(np.concatenate([zt[:, 0], zt[:, 1]], axis=-1).reshape(n // 2, 2 * 64))
    w1p = jnp.zeros((64, HY_FFN), F32).at[:HY_EMB].set(w1)
    zero = jnp.zeros((64, HY_FFN), F32)
    bd = lambda w: jnp.concatenate([jnp.concatenate([w, zero], axis=1), jnp.concatenate([zero, w], axis=1)], axis=0)
    w3 = w3.astype(BF16)
    zero3 = jnp.zeros_like(w3)
    full = lambda shape: pl.BlockSpec(shape, lambda i: (0, 0))
    tok = pl.BlockSpec((tm, 2 * D_GROUP), lambda i: (i, 0))
    return pl.pallas_call(
        _filter_body,
        out_shape=[jax.ShapeDtypeStruct((n, 2 * D_GROUP), BF16), jax.ShapeDtypeStruct((n, 2 * D_GROUP), BF16),
                   jax.ShapeDtypeStruct((1, HY_FILTER_CH), F32)],
        grid=(n // tm,),
        in_specs=[pl.BlockSpec((half, 2 * 64), lambda i: (i, 0)), full((2 * 64, 2 * HY_FFN)), full((1, 2 * HY_FFN)),
                  full((2 * HY_FFN, 2 * HY_FFN)), full((1, 2 * HY_FFN)), full((2 * HY_FFN, HY_FILTER_CH)),
                  full((2 * HY_FFN, HY_FILTER_CH)), full((1, HY_FILTER_CH))],
        out_specs=[tok, tok, full((1, HY_FILTER_CH))],
        compiler_params=_cparams(("arbitrary",), 48),
        name="hyena_filter_mlp",
    )(z2, bd(w1p), jnp.tile(b1.reshape(1, -1), (1, 2)), bd(w2), jnp.tile(b2.reshape(1, -1), (1, 2)),
      jnp.concatenate([w3, zero3], axis=0), jnp.concatenate([zero3, w3], axis=0), deltas.reshape(1, -1))


@functools.lru_cache(maxsize=None)
def _dft_tables(n1):
    nn = n1 * DFT2
    j = np.arange(DFT2)
    th = 2.0 * np.pi * ((np.outer(j, j)) % DFT2) / DFT2
    fr, fi = np.cos(th), -np.sin(th)
    m_fwd = np.block([[fr, -fi], [fi, fr]])
    m_inv = np.block([[fr, fi], [-fi, fr]])
    hh = n1 // 2
    k1 = np.arange(n1)[None, :, None]
    t1 = np.arange(hh)[None, None, :]
    t2 = np.arange(DFT2)[:, None, None]
    ph = 2.0 * np.pi * ((k1 * (DFT2 * t1 + t2)) % nn) / nn
    ar, ai = np.cos(ph), -np.sin(ph)
    a_fwd = np.concatenate([np.concatenate([ar, -ai], axis=2), np.concatenate([ai, ar], axis=2)], axis=1)
    a_real = np.concatenate([ar, ai], axis=1)
    t1r = np.where(t2 >= 1, n1 - 1 - t1, (n1 - t1) % n1)
    phr = 2.0 * np.pi * ((k1 * (DFT2 * t1r + t2)) % nn) / nn
    a_rev = np.concatenate([np.cos(phr), -np.sin(phr)], axis=1)
    a_rev[0, :, 0] = 0.0
    pht = np.transpose(ph, (0, 2, 1))
    cr, ci = np.cos(pht) / nn, np.sin(pht) / nn
    a_inv = np.concatenate([np.concatenate([cr, -ci], axis=2), np.concatenate([ci, cr], axis=2)], axis=1)
    f32 = lambda a: np.ascontiguousarray(a, dtype=np.float32)
    return dict(m_fwd=f32(m_fwd), m_inv=f32(m_inv), a_fwd=f32(a_fwd), a_real=f32(a_real), a_rev=f32(a_rev),
                a_inv=f32(a_inv))


def _fft_a_body(x_ref, m_ref, o_ref, *, tj):
    n1 = o_ref.shape[2]
    for e in range(tj):
        x = jnp.concatenate([x_ref[0, :, e, :], x_ref[1, :, e, :]], axis=0).astype(BF16)
        a = jnp.dot(m_ref[e], x, preferred_element_type=F32)
        o_ref[0, e] = a[:n1]
        o_ref[1, e] = a[n1:]


def _fft_a(x4, m, *, n1, tj):
    c = x4.shape[3]
    return pl.pallas_call(
        functools.partial(_fft_a_body, tj=tj),
        out_shape=jax.ShapeDtypeStruct((2, DFT2, n1, c), F32),
        grid=(DFT2 // tj,),
        in_specs=[pl.BlockSpec((2, n1 // 2, tj, c), lambda j: (0, 0, j, 0)),
                  pl.BlockSpec((tj, 2 * n1, n1), lambda j: (j, 0, 0))],
        out_specs=pl.BlockSpec((2, tj, n1, c), lambda j: (0, j, 0, 0)),
        compiler_params=_cparams(("parallel",), 48),
        name="fft_stage_a",
    )(x4, m)


def _fft_b_body(a_ref, mf_ref, mi_ref, g_ref, o_ref, *, kg):
    for kk in range(kg):
        x = jnp.concatenate([a_ref[0, :, kk, :], a_ref[1, :, kk, :]], axis=0).astype(BF16)
        xf = jnp.dot(mf_ref[...], x, preferred_element_type=F32)
        xr, xi = xf[:DFT2], xf[DFT2:]
        gr, gi = g_ref[0, kk].astype(F32), g_ref[1, kk].astype(F32)
        y = jnp.concatenate([xr * gr - xi * gi, xr * gi + xi * gr], axis=0).astype(BF16)
        bf = jnp.dot(mi_ref[...], y, preferred_element_type=F32)
        o_ref[0, kk] = bf[:DFT2]
        o_ref[1, kk] = bf[DFT2:]


def _fft_b(a, g, order, tabs, *, n1):
    c = a.shape[3]
    kg = min(n1, FFT_GROUP_DATA)
    blk = pl.BlockSpec((2, kg, DFT2, c), lambda k: (0, k, 0, 0))
    mat = pl.BlockSpec((2 * DFT2, 2 * DFT2), lambda k: (0, 0))
    return pl.pallas_call(
        functools.partial(_fft_b_body, kg=kg),
        out_shape=jax.ShapeDtypeStruct((2, n1, DFT2, c), F32),
        grid=(n1 // kg,),
        in_specs=[pl.BlockSpec((2, DFT2, kg, c), lambda k: (0, 0, k, 0)), mat, mat,
                  pl.BlockSpec((2, kg, DFT2, c), lambda k: (0, k, 0, order))],
        out_specs=blk,
        compiler_params=_cparams(("parallel",), 56),
        name="fft_stage_b",
    )(a, tabs["m_fwd"], tabs["m_inv"], g)


def _fft_bf_body(a_ref, mf_ref, o_ref, *, kg):
    for kk in range(kg):
        x = jnp.concatenate([a_ref[0, :, kk, :], a_ref[1, :, kk, :]], axis=0).astype(BF16)
        xf = jnp.dot(mf_ref[...], x, preferred_element_type=F32)
        o_ref[0, kk] = xf[:DFT2].astype(BF16)
        o_ref[1, kk] = xf[DFT2:].astype(BF16)


def _fft_b_forward(a, tabs, *, n1):
    c = a.shape[3]
    kg = FFT_GROUP
    return pl.pallas_call(
        functools.partial(_fft_bf_body, kg=kg),
        out_shape=jax.ShapeDtypeStruct((2, n1, DFT2, c), BF16),
        grid=(n1 // kg,),
        in_specs=[pl.BlockSpec((2, DFT2, kg, c), lambda k: (0, 0, k, 0)),
                  pl.BlockSpec((2 * DFT2, 2 * DFT2), lambda k: (0, 0))],
        out_specs=pl.BlockSpec((2, kg, DFT2, c), lambda k: (0, k, 0, 0)),
        compiler_params=_cparams(("parallel",), 48),
        name="fft_stage_b_filter",
    )(a, tabs["m_fwd"])


def _fft_ai_body(b_ref, m_ref, v_ref, x_ref, bias_ref, o_ref, *, tj):
    h = o_ref.shape[1]
    for e in range(tj):
        b = jnp.concatenate([b_ref[0, :, e, :], b_ref[1, :, e, :]], axis=0).astype(BF16)
        y = jnp.dot(m_ref[e], b, preferred_element_type=F32)
        o_ref[0, :, e, :] = y[:h]
        o_ref[1, :, e, :] = y[h:]
    o_ref[...] = x_ref[...] * (o_ref[...] + v_ref[...] * bias_ref[...])


def _fft_a_inv(b, m, v4, xm4, bias, *, n1, tj):
    c = b.shape[3]
    half = pl.BlockSpec((2, n1 // 2, tj, c), lambda j: (0, 0, j, 0))
    return pl.pallas_call(
        functools.partial(_fft_ai_body, tj=tj),
        out_shape=jax.ShapeDtypeStruct((2, n1 // 2, DFT2, c), F32),
        grid=(DFT2 // tj,),
        in_specs=[pl.BlockSpec((2, n1, tj, c), lambda j: (0, 0, j, 0)),
                  pl.BlockSpec((tj, n1, 2 * n1), lambda j: (j, 0, 0)),
                  half, half, pl.BlockSpec((1, c), lambda j: (0, 0))],
        out_specs=half,
        compiler_params=_cparams(("parallel",), 56),
        name="fft_stage_a_inv",
    )(b, m, v4, xm4, bias)


def _filt_a_body(hj_ref, hz_ref, hr_ref, s_ref, mf_ref, mr_ref, o_ref, *, tj):
    n1 = o_ref.shape[2]
    s = s_ref[...]
    for e in range(tj):
        src = hz_ref if e == 0 else hr_ref
        me = 0 if e == 0 else tj - e
        for o in range(2):
            c0 = 2 * D_GROUP * o
            inv = 1.0 / (s[:, c0:c0 + D_GROUP] + s[:, c0 + D_GROUP:c0 + 2 * D_GROUP] + EPS)
            lsl = slice(o * D_GROUP, (o + 1) * D_GROUP)
            hf = hj_ref[e, :, lsl].astype(BF16)
            hb = src[me, :, lsl].astype(BF16)
            a = (jnp.dot(mf_ref[e], hf, preferred_element_type=F32)
                 + jnp.dot(mr_ref[e], hb, preferred_element_type=F32)) * inv
            o_ref[0, e, :, lsl] = a[:n1]
            o_ref[1, e, :, lsl] = a[n1:]


def _filt_a(hf, hb, colsum, tabs, *, n1):
    tj = FFT_GROUP
    nj = DFT2 // tj
    wd = 2 * D_GROUP
    shp = (DFT2, n1 // 2, wd)
    blk = lambda fn: pl.BlockSpec((tj, n1 // 2, wd), fn)
    mat = pl.BlockSpec((tj, 2 * n1, n1 // 2), lambda j: (j, 0, 0))
    return pl.pallas_call(
        functools.partial(_filt_a_body, tj=tj),
        out_shape=jax.ShapeDtypeStruct((2, DFT2, n1, wd), F32),
        grid=(nj,),
        in_specs=[blk(lambda j: (j, 0, 0)), blk(lambda j: ((nj - j) % nj, 0, 0)), blk(lambda j: (nj - 1 - j, 0, 0)),
                  pl.BlockSpec((1, HY_FILTER_CH), lambda j: (0, 0)), mat, mat],
        out_specs=pl.BlockSpec((2, tj, n1, wd), lambda j: (0, j, 0, 0)),
        compiler_params=_cparams(("parallel",), 48),
        name="filter_stage_a",
    )(hf.reshape(shp), hb.reshape(shp), hb.reshape(shp), colsum, tabs["a_real"], tabs["a_rev"])


def _hyena_long(v, x1, x2, hf, hb, colsum, bias, *, seq):
    n1 = 2 * seq // DFT2
    tabs = {k: jnp.asarray(a).astype(BF16) for k, a in _dft_tables(n1).items()}
    tj = FFT_GROUP_DATA
    fa = _filt_a(hf, hb, colsum, tabs, n1=n1)
    g = _fft_b_forward(fa, tabs, n1=n1)
    shp = (2, n1 // 2, DFT2, D_GROUP)
    z = v.reshape(shp)
    for order, xm in ((0, x1.reshape(shp)), (1, x2.reshape(shp))):
        a = _fft_a(z, tabs["a_fwd"], n1=n1, tj=tj)
        b = _fft_b(a, g, order, tabs, n1=n1)
        z = _fft_a_inv(b, tabs["a_inv"], z, xm, bias[order].reshape(1, D_GROUP), n1=n1, tj=tj)
    return z.reshape(2 * seq, D_GROUP)


def _hyena_ctx_body(v_ref, x1_ref, x2_ref, hf_ref, hb_ref, s_ref, bias_ref, cm_ref, sm_ref, ct_ref, st_ref, o_ref):
    n = v_ref.shape[1]
    cm, sm, ct, st = cm_ref[...], sm_ref[...], ct_ref[...], st_ref[...]
    s = s_ref[...]
    row0 = _iota((n, D_GROUP), 0) == 0
    dot = lambda a, b: jnp.dot(a, b.astype(BF16), preferred_element_type=F32)
    zr, zi = v_ref[0], v_ref[1]
    for order, xm in ((0, x1_ref), (1, x2_ref)):
        c0 = 2 * D_GROUP * order
        inv = 1.0 / (s[:, c0:c0 + D_GROUP] + s[:, c0 + D_GROUP:c0 + 2 * D_GROUP] + EPS)
        lsl = slice(order * D_GROUP, (order + 1) * D_GROUP)
        hf = hf_ref[:, lsl] * inv
        hb = jnp.where(row0, 0.0, hb_ref[:, lsl] * inv)
        gr = dot(cm, hf + hb)
        gi = dot(sm, hb - hf)
        xr = dot(cm, zr) + dot(sm, zi)
        xi = dot(cm, zi) - dot(sm, zr)
        yr = xr * gr - xi * gi
        yi = xr * gi + xi * gr
        cr = dot(ct, yr) - dot(st, yi)
        ci = dot(ct, yi) + dot(st, yr)
        bias = bias_ref[order:order + 1, :]
        zr = xm[0] * (cr + zr * bias)
        zi = xm[1] * (ci + zi * bias)
    o_ref[0] = zr
    o_ref[1] = zi


def _hyena_ctx(v, x1, x2, hf, hb, colsum, bias, *, seq):
    nn = 2 * seq
    k = np.arange(nn)
    t = np.arange(seq)
    th = 2.0 * np.pi * (np.outer(k, t) % nn) / nn
    cm, sm = np.cos(th), np.sin(th)
    consts = [jnp.asarray(a, F32).astype(BF16) for a in (cm, sm, cm.T / nn, sm.T / nn)]
    shp = (2, seq, D_GROUP)
    full3 = pl.BlockSpec(shp, lambda i: (0, 0, 0))
    f2 = lambda a: pl.BlockSpec(a.shape, lambda i: (0, 0))
    args = [hf, hb, colsum, bias] + consts
    out = pl.pallas_call(
        _hyena_ctx_body,
        out_shape=jax.ShapeDtypeStruct(shp, F32),
        grid=(1,),
        in_specs=[full3, full3, full3] + [f2(a) for a in args],
        out_specs=full3,
        compiler_params=_cparams(("arbitrary",)),
        name="hyena_ctx",
    )(v.reshape(shp), x1.reshape(shp), x2.reshape(shp), *args)
    return out.reshape(2 * seq, D_GROUP)


def _split_w_in(w_in):
    wb = w_in.astype(BF16)
    parts = dict(kq=jnp.concatenate([wb[:, COL_K:COL_V], wb[:, COL_Q:COL_R]], axis=1), v=wb[:, COL_V:COL_GF],
                 r=wb[:, COL_R:COL_POOL], pool=wb[:, COL_POOL:COL_HY], hy=wb[:, COL_HY:COL_CONV],
                 conv=wb[:, COL_CONV:P_IN])
    gate = jnp.pad(w_in[:, COL_GF:COL_Q], ((0, 0), (0, LANE - 2 * GLA_LOWRANK)))
    return parts, gate


def _mix(u, p, *, bsz, seq, is_ctx, states, filt):
    sf, sb = states
    cpb = min(16, seq // CHUNK)
    if is_ctx:
        pool = _pool1d(u["pool"], p["pool_wbd"], p["pool_scale"], bsz=bsz, seq=seq)
    else:
        pool = _pool2d(u["pool"], p["pool_wbd"], p["pool_scale"], bsz=bsz, seq=seq)
    tmc = min(seq, 2048)
    v, x1, x2 = _hy_short(u["hy"], p["hy_short_w"], p["hy_short_b"], seq=seq, tm=tmc)
    if is_ctx:
        hy = _hyena_ctx(v, x1, x2, *filt, p["hy_bias"], seq=seq)
    else:
        hy = _hyena_long(v, x1, x2, *filt, p["hy_bias"], seq=seq)
    gla = _gla_read(u["kq"], u["v"], u["kq"], u["r"], u["lf"], u["lb"], sf, sb, p["gla_ng"],
                    bsz=bsz, seq=seq, cpb=cpb)
    conv = _conformer(u["conv"], p["conv_dw_w"], p["conv_dw_b"], p["conv_ln_g"], p["conv_ln_b"], seq=seq, tm=tmc)
    return [pool, hy, gla, conv]


def kernel(x, c, ctx, c_ctx, ada_w, ada_b, ffn1_norm, ffn1_wi, ffn1_wo, mix_norm, w_in, w_out, pool_w, pool_scale, hy_short_w, hy_short_b, hy_w1, hy_b1, hy_w2, hy_b2, hy_w3, hy_deltas, hy_bias, gla_gw_f, gla_gb_f, gla_gw_b, gla_gb_b, gla_norm, conv_dw_w, conv_dw_b, conv_ln_g, conv_ln_b, ffn2_norm, ffn2_wi, ffn2_wo, final_norm):
    bsz, seq, d = x.shape
    clen = ctx.shape[1]
    depth = ada_w.shape[0]
    assert bsz == 2, "the Hyena transform packs exactly two batch rows into one complex signal"
    xs = x.reshape(bsz * seq, d)
    cs = ctx.reshape(bsz * clen, d)
    cc = jnp.concatenate([c, c_ctx[None, :], jnp.zeros((8 - bsz - 1, d), F32)], axis=0)
    mod = _modulation(cc, ada_w, ada_b).reshape(depth, 8, N_MOD, d)
    crow = bsz
    head_eye = np.kron(np.eye(len(POOL_WINDOWS), dtype=np.float32), np.ones((POOL_CH, POOL_CH), np.float32))
    tmx = TOKEN_TILE
    tps_x = seq // tmx
    tmc = bsz * clen
    zero_state = jnp.zeros((bsz, GLA_HEADS * GLA_DV, GLA_QK), F32)
    wi1, wo1, wi2, wo2, wout = (w.astype(BF16) for w in (ffn1_wi, ffn1_wo, ffn2_wi, ffn2_wo, w_out))

    for l in range(depth):
        last = l == depth - 1
        wparts, wgate = _split_w_in(w_in[l])
        zlow = jnp.zeros((GLA_LOWRANK, GLA_QK), F32)
        gw = jnp.concatenate([jnp.concatenate([gla_gw_f[l], zlow], axis=1),
                              jnp.concatenate([zlow, gla_gw_b[l]], axis=1),
                              jnp.zeros((LANE - 2 * GLA_LOWRANK, 2 * GLA_QK), F32)], axis=0)
        gb = jnp.concatenate([gla_gb_f[l], gla_gb_b[l]]).reshape(1, -1)
        gw = _gate_fold(wgate, gw).astype(BF16)
        wbd = jnp.tile(pool_w[l].reshape(D_GROUP, POOL_CH), (1, len(POOL_WINDOWS))) * head_eye
        p = dict(pool_wbd=wbd.astype(BF16), pool_scale=pool_scale[l].reshape(1, -1),
                 hy_short_w=hy_short_w[l], hy_short_b=hy_short_b[l], hy_bias=hy_bias[l],
                 gla_ng=jnp.tile(gla_norm[l], GLA_HEADS).reshape(1, -1),
                 conv_dw_w=conv_dw_w[l], conv_dw_b=conv_dw_b[l], conv_ln_g=conv_ln_g[l], conv_ln_b=conv_ln_b[l])
        names = ["kq", "v", "r", "pool", "hy", "conv"]

        xs, *outs = _ffn_proj(xs, mod, ffn1_norm[l], wi1, wo1, mix_norm[l], [wparts[nm] for nm in names], gw, gb,
                              layer=l, row0=0, tm=tmx, tiles_per_seq=tps_x)
        ux = dict(zip(names + ["lf", "lb"], outs))
        cnames = ["kq", "v"] if last else names
        cs, *outs = _ffn_proj(cs, mod, ffn1_norm[l], wi1, wo1, mix_norm[l], [wparts[nm] for nm in cnames], gw, gb,
                              layer=l, row0=crow, tm=tmc, tiles_per_seq=1)
        uc = dict(zip(cnames + ["lf", "lb"], outs))

        ccpb = clen // CHUNK
        sfc, sbc, finf, finb = _gla_states(uc["kq"], uc["v"], uc["lf"], uc["lb"], zero_state, zero_state,
                                           bsz=bsz, seq=clen, cpb=ccpb)
        sfx, sbx, _, _ = _gla_states(ux["kq"], ux["v"], ux["lf"], ux["lb"], finf, finb, bsz=bsz, seq=seq,
                                     cpb=16)

        filt = _hy_filter(seq, hy_w1[l], hy_b1[l], hy_w2[l], hy_b2[l], hy_w3[l], hy_deltas[l], t2_major=True)
        mixed = _mix(ux, p, bsz=bsz, seq=seq, is_ctx=False, states=(sfx, sbx), filt=filt)
        xs = _out_ffn(xs, mod, mixed, wout, ffn2_norm[l], wi2, wo2, layer=l, row0=0, tm=2 * tmx,
                      tiles_per_seq=tps_x // 2, final_g=final_norm if last else None)
        if not last:
            filt_c = _hy_filter(clen, hy_w1[l], hy_b1[l], hy_w2[l], hy_b2[l], hy_w3[l], hy_deltas[l],
                                t2_major=False)
            mixed = _mix(uc, p, bsz=bsz, seq=clen, is_ctx=True, states=(sfc, sbc), filt=filt_c)
            cs = _out_ffn(cs, mod, mixed, wout, ffn2_norm[l], wi2, wo2, layer=l, row0=crow, tm=tmc,
                          tiles_per_seq=1)
    return xs.reshape(bsz, seq, d)
```

```python
import functools
import math

import numpy as np
import jax
import jax.numpy as jnp
from jax import lax
from jax.experimental import pallas as pl
from jax.experimental.pallas import tpu as pltpu

F32 = jnp.float32
BF16 = jnp.bfloat16
HI = lax.Precision.HIGHEST

GRID_W = 64
D_GROUP = 256
N_MOD = 9
EPS = 1e-6
POOL_WINDOWS = (2, 4, 8, 16)
POOL_CH = 64
HY_BANDS = 16
HY_EMB = 1 + 2 * HY_BANDS
HY_FFN = 64
HY_FILTER_CH = 4 * D_GROUP
GLA_HEADS = 4
GLA_DK = 32
GLA_DV = 64
GLA_QK = 128
GLA_LOWRANK = 16
GLA_TAU = 16.0
CHUNK = 64
CONV_WIDTH = 31
HY_SHORT = 3

COL_K = 0
COL_V = COL_K + GLA_QK
COL_GF = COL_V + D_GROUP
COL_GB = COL_GF + GLA_LOWRANK
COL_Q = COL_GB + GLA_LOWRANK
COL_R = COL_Q + GLA_QK
COL_POOL = COL_R + D_GROUP
COL_HY = COL_POOL + D_GROUP
COL_CONV = COL_HY + 3 * D_GROUP
P_IN = COL_CONV + 2 * D_GROUP

LANE = 128
SUBLANES = 8
MIB = 1024 * 1024

DFT2 = 128
FFT_GROUP = 16
FFT_GROUP_DATA = 32
MOD_ROWS = 3
FF_CHUNK = 256
ROW_SUB = 256
TOKEN_TILE = 512


def _cparams(sem, vmem_mib=None):
    kw = dict(dimension_semantics=sem)
    if vmem_mib is not None:
        kw["vmem_limit_bytes"] = vmem_mib * MIB
    return pltpu.CompilerParams(**kw)


def _silu(x):
    return x * jax.nn.sigmoid(x)


def _rms_mod(h, g, m):
    y = h * lax.rsqrt(jnp.mean(h * h, axis=-1, keepdims=True) + EPS) * g
    return y * (1.0 + m[1:2, :]) + m[0:1, :]


def _iota(shape, dim):
    return lax.broadcasted_iota(jnp.int32, shape, dim)


def _mod_body(at_ref, w_ref, b_ref, o_ref):
    tk, nm = w_ref.shape[1], w_ref.shape[2]

    @pl.when(pl.program_id(1) == 0)
    def _():
        o_ref[0] = jnp.zeros((SUBLANES, nm), F32) + b_ref[0]

    at = _silu(at_ref[...])
    cols = [jnp.broadcast_to(at[:, r:r + 1], (tk, LANE)).reshape(tk // SUBLANES, SUBLANES, LANE)
            for r in range(MOD_ROWS)]
    for j in range(nm // LANE):
        lsl = slice(j * LANE, (j + 1) * LANE)
        w3 = w_ref[0, :, lsl].reshape(tk // SUBLANES, SUBLANES, LANE)
        for r in range(MOD_ROWS):
            part = jnp.sum(w3 * cols[r], axis=0)
            o_ref[0, r:r + 1, lsl] += jnp.sum(part, axis=0, keepdims=True)


def _modulation(cc, ada_w, ada_b):
    nl, d, nm = ada_w.shape
    tk = 256
    return pl.pallas_call(
        _mod_body,
        out_shape=jax.ShapeDtypeStruct((nl, 8, nm), F32),
        grid=(nl, d // tk),
        in_specs=[pl.BlockSpec((tk, 8), lambda l, j: (j, 0)),
                  pl.BlockSpec((1, tk, nm), lambda l, j: (l, j, 0)),
                  pl.BlockSpec((1, 1, nm), lambda l, j: (l, 0, 0))],
        out_specs=pl.BlockSpec((1, 8, nm), lambda l, j: (l, 0, 0)),
        compiler_params=_cparams(("parallel", "arbitrary"), 40),
        name="adaln_mod",
    )(cc.T, ada_w, ada_b.reshape(nl, 1, nm))


def _swiglu(xn, wi_ref, wo_ref, hm_ref, rows):
    ff = wo_ref.shape[0]
    for c in range(0, ff, FF_CHUNK):
        a = jnp.dot(xn, wi_ref[:, c:c + FF_CHUNK], preferred_element_type=F32)
        g = jnp.dot(xn, wi_ref[:, ff + c:ff + c + FF_CHUNK], preferred_element_type=F32)
        hm_ref[rows, c:c + FF_CHUNK] = (_silu(g) * a).astype(BF16)
    return jnp.dot(hm_ref[rows, :], wo_ref[...], preferred_element_type=F32)


def _resident(shape):
    nd = len(shape)
    return pl.BlockSpec(shape, lambda i: (0,) * nd, pipeline_mode=pl.Buffered(1))


def _resident_layer(shape, layer):
    nd = len(shape)
    return pl.BlockSpec((None,) + tuple(shape), lambda i: (layer,) + (0,) * nd, pipeline_mode=pl.Buffered(1))


def _gate_fold_body(wg_ref, gw_ref, o_ref):
    o_ref[...] = jnp.dot(wg_ref[...], gw_ref[...], precision=HI, preferred_element_type=F32)


def _gate_fold(w_gate, gw):
    d = w_gate.shape[0]
    return pl.pallas_call(
        _gate_fold_body,
        out_shape=jax.ShapeDtypeStruct((d, gw.shape[1]), F32),
        grid=(1,),
        in_specs=[pl.BlockSpec(w_gate.shape, lambda i: (0, 0)), pl.BlockSpec(gw.shape, lambda i: (0, 0))],
        out_specs=pl.BlockSpec((d, gw.shape[1]), lambda i: (0, 0)),
        name="gate_fold",
    )(w_gate, gw)


def _chunk_prefix(x):
    pos = _iota(x.shape, 0) & (CHUNK - 1)
    shift = 1
    while shift < CHUNK:
        x = x + jnp.where(pos >= shift, pltpu.roll(x, shift, 0), 0.0)
        shift *= 2
    return x


def _ffn_proj_body(h_ref, m_ref, g1_ref, wi_ref, wo_ref, g2_ref, *rest, nparts, row_sub):
    w_refs = rest[:nparts]
    gw_ref, gb_ref = rest[nparts:nparts + 2]
    x_ref = rest[nparts + 2]
    o_refs = rest[nparts + 3:2 * nparts + 5]
    hm_ref = rest[-1]
    m = m_ref[0]
    for r0 in range(0, h_ref.shape[0], row_sub):
        rows = slice(r0, r0 + row_sub)
        h = h_ref[rows, :]
        xn = _rms_mod(h, g1_ref[...], m[0:2]).astype(BF16)
        x1 = h + (0.5 * m[2:3, :]) * _swiglu(xn, wi_ref, wo_ref, hm_ref, rows)
        x_ref[rows, :] = x1
        xn2 = _rms_mod(x1, g2_ref[...], m[3:5]).astype(BF16)
        for w_ref, o_ref in zip(w_refs, o_refs[:nparts]):
            o_ref[rows, :] = jnp.dot(xn2, w_ref[...], preferred_element_type=F32).astype(o_ref.dtype)
        a = jnp.dot(xn2, gw_ref[...], preferred_element_type=F32) + gb_ref[...]
        ls = (jnp.minimum(a, 0.0) - jnp.log(1.0 + jnp.exp(-jnp.abs(a)))) * (1.0 / GLA_TAU)
        pre = _chunk_prefix(ls)
        o_refs[nparts][rows, :] = pre[:, :GLA_QK]
        pb = pre[:, GLA_QK:]
        tot = jnp.concatenate([jnp.broadcast_to(pb[c0 + CHUNK - 1:c0 + CHUNK, :], (CHUNK, GLA_QK))
                               for c0 in range(0, row_sub, CHUNK)], axis=0)
        o_refs[nparts + 1][rows, :] = tot - pb + ls[:, GLA_QK:]


def _mod_spec(d, layer, row0, tiles_per_seq):
    return pl.BlockSpec((None, 1, N_MOD, d), lambda i: (layer, row0 + i // tiles_per_seq, 0, 0))


def _ffn_proj(h, mod, g1, wi, wo, g2, w_parts, gw, gb, *, layer, row0, tm, tiles_per_seq):
    n, d = h.shape
    ff = wo.shape[1]
    nparts = len(w_parts)
    widths = [w.shape[1] for w in w_parts]
    tok = lambda wd: pl.BlockSpec((tm, wd), lambda i: (i, 0))
    in_specs = [tok(d), _mod_spec(d, layer, row0, tiles_per_seq), _resident((1, d)),
                _resident_layer((d, 2 * ff), layer), _resident_layer((ff, d), layer), _resident((1, d))]
    in_specs += [_resident((d, wd)) for wd in widths]
    in_specs += [_resident((d, 2 * GLA_QK)), _resident((1, 2 * GLA_QK))]
    out_shape = [jax.ShapeDtypeStruct((n, d), F32)]
    out_shape += [jax.ShapeDtypeStruct((n, wd), BF16) for wd in widths]
    out_shape += [jax.ShapeDtypeStruct((n, GLA_QK), F32)] * 2
    return pl.pallas_call(
        functools.partial(_ffn_proj_body, nparts=nparts, row_sub=min(tm, 2 * ROW_SUB)),
        out_shape=out_shape,
        grid=(n // tm,),
        in_specs=in_specs,
        out_specs=[tok(d)] + [tok(wd) for wd in widths] + [tok(GLA_QK)] * 2,
        scratch_shapes=[pltpu.VMEM((tm, ff), BF16)],
        compiler_params=_cparams(("parallel",), 60),
        name="ffn1_in_proj",
    )(h, mod, g1.reshape(1, d), wi, wo, g2.reshape(1, d), *w_parts, gw, gb)


def _out_ffn_body(x_ref, m_ref, p_ref, hy_ref, gl_ref, cv_ref, wout_ref, g_ref, wi_ref, wo_ref, fn_ref, o_ref,
                  hm_ref, *, final, row_sub):
    m = m_ref[0]
    for r0 in range(0, x_ref.shape[0], row_sub):
        rows = slice(r0, r0 + row_sub)
        acc = jnp.dot(p_ref[rows, :].astype(BF16), wout_ref[0:D_GROUP, :], preferred_element_type=F32)
        acc += jnp.dot(hy_ref[rows, :].astype(BF16), wout_ref[D_GROUP:2 * D_GROUP, :], preferred_element_type=F32)
        acc += jnp.dot(gl_ref[rows, :].astype(BF16), wout_ref[2 * D_GROUP:3 * D_GROUP, :],
                       preferred_element_type=F32)
        acc += jnp.dot(cv_ref[rows, :].astype(BF16), wout_ref[3 * D_GROUP:, :], preferred_element_type=F32)
        x2 = x_ref[rows, :] + m[5:6, :] * acc
        xn = _rms_mod(x2, g_ref[...], m[6:8]).astype(BF16)
        out = x2 + (0.5 * m[8:9, :]) * _swiglu(xn, wi_ref, wo_ref, hm_ref, rows)
        if final:
            out = out * lax.rsqrt(jnp.mean(out * out, axis=-1, keepdims=True) + EPS) * fn_ref[...]
        o_ref[rows, :] = out


def _out_ffn(x, mod, parts, w_out, g, wi, wo, *, layer, row0, tm, tiles_per_seq, final_g=None):
    n, d = x.shape
    ff = wo.shape[1]
    final = final_g is not None
    fg = final_g if final else g
    tok = lambda wd: pl.BlockSpec((tm, wd), lambda i: (i, 0))
    return pl.pallas_call(
        functools.partial(_out_ffn_body, final=final, row_sub=min(tm, 2 * ROW_SUB)),
        out_shape=jax.ShapeDtypeStruct((n, d), F32),
        grid=(n // tm,),
        in_specs=[tok(d), _mod_spec(d, layer, row0, tiles_per_seq),
                  tok(D_GROUP), tok(D_GROUP), tok(D_GROUP), tok(D_GROUP), _resident_layer((d, d), layer),
                  _resident((1, d)), _resident_layer((d, 2 * ff), layer), _resident_layer((ff, d), layer),
                  _resident((1, d))],
        out_specs=tok(d),
        scratch_shapes=[pltpu.VMEM((tm, ff), BF16)],
        compiler_params=_cparams(("parallel",), 56),
        name="out_proj_ffn2",
    )(x, mod, *parts, w_out, g.reshape(1, d), wi, wo, fg.reshape(1, d))


def _gla_state_body(kf, vf, bf, kb, vb, bb, s0f, s0b, sf_o, sb_o, ff_o, fb_o, stf, stb, *, cpb):
    i = pl.program_id(1)

    @pl.when(i == 0)
    def _():
        stf[...] = s0f[0]
        stb[...] = s0b[0]

    sshape = (GLA_HEADS * GLA_DV, GLA_QK)
    bmask = (_iota(sshape, 0) >> 6) == (_iota(sshape, 1) >> 5)
    tn_dims = (((0,), (0,)), ((), ()))

    def direction(k_ref, v_ref, b_ref, last, st, s_o, order):
        s = st[...]
        for ci in order:
            sl = slice(ci * CHUNK, (ci + 1) * CHUNK)
            b = b_ref[sl, :]
            tot = b[last:last + 1, :]
            kd = (k_ref[sl, :] * jnp.exp(tot - b)).astype(BF16)
            upd = lax.dot_general(v_ref[sl, :].astype(BF16), kd, tn_dims, preferred_element_type=F32)
            s_o[0, ci] = s.astype(BF16)
            s = s * jnp.exp(tot) + jnp.where(bmask, upd, 0.0)
        st[...] = s

    direction(kf, vf, bf, CHUNK - 1, stf, sf_o, range(cpb))
    direction(kb, vb, bb, 0, stb, sb_o, range(cpb - 1, -1, -1))
    ff_o[0] = stf[...]
    fb_o[0] = stb[...]


def _gla_states(k, v, lf, lb, s0f, s0b, *, bsz, seq, cpb):
    bt = cpb * CHUNK
    nb = seq // bt
    nc = seq // CHUNK
    srow = GLA_HEADS * GLA_DV

    def tf(b, i):
        return (b * nb + i, 0)

    def tb(b, i):
        return (b * nb + nb - 1 - i, 0)

    sblk = pl.BlockSpec((1, srow, GLA_QK), lambda b, i: (b, 0, 0))
    return pl.pallas_call(
        functools.partial(_gla_state_body, cpb=cpb),
        out_shape=[jax.ShapeDtypeStruct((bsz, nc, srow, GLA_QK), BF16),
                   jax.ShapeDtypeStruct((bsz, nc, srow, GLA_QK), BF16),
                   jax.ShapeDtypeStruct((bsz, srow, GLA_QK), F32),
                   jax.ShapeDtypeStruct((bsz, srow, GLA_QK), F32)],
        grid=(bsz, nb),
        in_specs=[pl.BlockSpec((bt, GLA_QK), tf), pl.BlockSpec((bt, D_GROUP), tf), pl.BlockSpec((bt, GLA_QK), tf),
                  pl.BlockSpec((bt, GLA_QK), tb), pl.BlockSpec((bt, D_GROUP), tb), pl.BlockSpec((bt, GLA_QK), tb),
                  sblk, sblk],
        out_specs=[pl.BlockSpec((1, cpb, srow, GLA_QK), lambda b, i: (b, i, 0, 0)),
                   pl.BlockSpec((1, cpb, srow, GLA_QK), lambda b, i: (b, nb - 1 - i, 0, 0)),
                   sblk, sblk],
        scratch_shapes=[pltpu.VMEM((srow, GLA_QK), F32), pltpu.VMEM((srow, GLA_QK), F32)],
        compiler_params=_cparams(("parallel", "arbitrary")),
        name="gla_states",
    )(k, v, lf, k, v, lb, s0f, s0b)


def _gla_read_body(k_ref, v_ref, q_ref, r_ref, bf_ref, bb_ref, sf_ref, sb_ref, ng_ref, o_ref, acc_ref, *, cpb):
    hrows = GLA_HEADS * CHUNK
    cpos = _iota((hrows, CHUNK), 0) & (CHUNK - 1)
    ccol = _iota((hrows, CHUNK), 1)
    lowm = cpos >= ccol
    upm = cpos <= ccol
    hq = (_iota((hrows, GLA_QK), 0) >> 6) == (_iota((hrows, GLA_QK), 1) >> 5)
    ho = (_iota((hrows, D_GROUP), 0) >> 6) == (_iota((hrows, D_GROUP), 1) >> 6)
    bavg = jnp.where((_iota((D_GROUP, D_GROUP), 0) >> 6) == (_iota((D_GROUP, D_GROUP), 1) >> 6),
                     1.0 / GLA_DV, 0.0).astype(BF16)
    nt_dims = (((1,), (1,)), ((), ()))

    bf_ = bf_ref[...]
    bb_ = bb_ref[...]
    qs = q_ref[...].astype(F32) * (GLA_DK ** -0.5)
    kk = k_ref[...]
    qef = (qs * jnp.exp(bf_)).astype(BF16)
    qeb = (qs * jnp.exp(bb_)).astype(BF16)
    kef = (kk * jnp.exp(-bf_)).astype(BF16)
    keb = (kk * jnp.exp(-bb_)).astype(BF16)
    vb16 = v_ref[...].astype(BF16)
    zero = jnp.zeros((), BF16)
    for ci in range(cpb):
        sl = slice(ci * CHUNK, (ci + 1) * CHUNK)
        qf4 = jnp.where(hq, jnp.concatenate([qef[sl]] * GLA_HEADS, axis=0), zero)
        qb4 = jnp.where(hq, jnp.concatenate([qeb[sl]] * GLA_HEADS, axis=0), zero)
        af = lax.dot_general(qf4, kef[sl], nt_dims, preferred_element_type=F32)
        ab = lax.dot_general(qb4, keb[sl], nt_dims, preferred_element_type=F32)
        att = (jnp.where(lowm, af, 0.0) + jnp.where(upm, ab, 0.0)).astype(BF16)
        oall = jnp.dot(att, vb16[sl], preferred_element_type=F32)
        om = jnp.where(ho, oall, 0.0)
        o = om[0:CHUNK] + om[CHUNK:2 * CHUNK] + om[2 * CHUNK:3 * CHUNK] + om[3 * CHUNK:4 * CHUNK]
        qcat = jnp.concatenate([qef[sl], qeb[sl]], axis=1)
        scat = jnp.concatenate([sf_ref[0, ci], sb_ref[0, ci]], axis=1)
        acc_ref[sl, :] = o + lax.dot_general(qcat, scat, nt_dims, preferred_element_type=F32)
    o = acc_ref[...]
    ms = _chunk_sum_rhs(o * o, bavg)
    o_ref[...] = (o * lax.rsqrt(ms + EPS) * ng_ref[...] * _silu(r_ref[...].astype(F32))).astype(o_ref.dtype)


def _chunk_sum_rhs(x, mat):
    hi, lo = _split_bf16(x)
    return jnp.dot(hi, mat, preferred_element_type=F32) + jnp.dot(lo, mat, preferred_element_type=F32)


def _gla_read(k, v, q, r, lf, lb, sf, sb, ng, *, bsz, seq, cpb):
    bt = cpb * CHUNK
    nb = seq // bt
    srow = GLA_HEADS * GLA_DV
    n = bsz * seq

    def tk(i):
        return (i, 0)

    sspec = pl.BlockSpec((1, cpb, srow, GLA_QK), lambda i: (i // nb, i % nb, 0, 0))
    return pl.pallas_call(
        functools.partial(_gla_read_body, cpb=cpb),
        out_shape=jax.ShapeDtypeStruct((n, D_GROUP), BF16),
        scratch_shapes=[pltpu.VMEM((bt, D_GROUP), F32)],
        grid=(bsz * nb,),
        in_specs=[pl.BlockSpec((bt, GLA_QK), tk), pl.BlockSpec((bt, D_GROUP), tk),
                  pl.BlockSpec((bt, GLA_QK), lambda i: (i, 1)), pl.BlockSpec((bt, D_GROUP), tk),
                  pl.BlockSpec((bt, GLA_QK), tk), pl.BlockSpec((bt, GLA_QK), tk),
                  sspec, sspec, pl.BlockSpec((1, D_GROUP), lambda i: (0, 0))],
        out_specs=pl.BlockSpec((bt, D_GROUP), tk),
        compiler_params=_cparams(("parallel",)),
        name="gla_readout",
    )(k, v, q, r, lf, lb, sf, sb, ng)


def _box_matrix(n, w):
    pos = np.arange(n)
    lo = np.clip(pos - w // 2, 0, n)
    hi = np.clip(pos - w // 2 + w, 0, n)
    col = np.arange(n)[None, :]
    return ((col >= lo[:, None]) & (col < hi[:, None])).astype(np.float32)


def _lane_windows(shape):
    w = jnp.left_shift(2, _iota(shape, 1) >> 6)
    return w, w >> 1


def _box_count(pos, w, half, n):
    return jnp.minimum(pos - half + w, n) - jnp.maximum(pos - half, 0)


def _split_bf16(x):
    hi = x.astype(BF16)
    lo = (x - hi.astype(F32)).astype(BF16)
    return hi, lo


def _pool2d_body(cur_ref, prev_ref, next_ref, pc_ref, w_ref, sc_ref, o_ref, ycol, *, tiles, rows):
    i = pl.program_id(1)
    tm = cur_ref.shape[0]
    hb = prev_ref.shape[0]
    sub = 2 * GRID_W
    pflag = jnp.where(i > 0, 1.0, 0.0)
    nflag = jnp.where(i < tiles - 1, 1.0, 0.0)

    def colpool(x):
        halves = []
        for half in range(2):
            lsl = slice(half * LANE, (half + 1) * LANE)
            ys = [jnp.dot(pc_ref[wi], x[:, lsl], preferred_element_type=F32) for wi in (2 * half, 2 * half + 1)]
            lane = _iota((sub, LANE), 1)
            halves.append(jnp.where(lane < POOL_CH, ys[0], ys[1]))
        return jnp.concatenate(halves, axis=1)

    for s in range(hb // sub):
        ycol[s * sub:(s + 1) * sub, :] = colpool(prev_ref[s * sub:(s + 1) * sub, :]) * pflag
    for s in range(tm // sub):
        ycol[hb + s * sub:hb + (s + 1) * sub, :] = colpool(cur_ref[s * sub:(s + 1) * sub, :])
    for s in range(hb // sub):
        ycol[hb + tm + s * sub:hb + tm + (s + 1) * sub, :] = colpool(next_ref[s * sub:(s + 1) * sub, :]) * nflag

    rc = 256
    wl, half = _lane_windows((rc, D_GROUP))
    narrow = _iota((rc, LANE), 1) < POOL_CH
    for r0 in range(0, tm, rc):
        def band(lo, hi, lsl):
            base = hb + r0
            acc = ycol[base + GRID_W * lo:base + GRID_W * lo + rc, lsl]
            for dd in range(lo + 1, hi):
                acc = acc + ycol[base + GRID_W * dd:base + GRID_W * dd + rc, lsl]
            return acc

        left, right = slice(0, LANE), slice(LANE, 2 * LANE)
        z2 = band(-1, 1, left)
        z4 = z2 + band(-2, -1, left) + band(1, 2, left)
        z8 = band(-4, 4, right)
        z16 = z8 + band(-8, -4, right) + band(4, 8, right)
        z = jnp.concatenate([jnp.where(narrow, z2, z4), jnp.where(narrow, z8, z16)], axis=1)
        tok = _iota((rc, D_GROUP), 0) + (i * tm + r0)
        rcnt = _box_count(tok >> 6, wl, half, rows)
        ccnt = _box_count(tok & (GRID_W - 1), wl, half, GRID_W)
        pooled = z / (rcnt * ccnt).astype(F32)
        dlt = (pooled - cur_ref[r0:r0 + rc, :]).astype(BF16)
        o_ref[r0:r0 + rc, :] = (jnp.dot(dlt, w_ref[...], preferred_element_type=F32) * sc_ref[...]).astype(o_ref.dtype)


def _pool2d(u, wbd, scale, *, bsz, seq):
    tm = 1024
    hb = 512
    tiles = seq // tm
    r = tm // hb
    nhb = seq // hb
    pc = np.stack([np.kron(np.eye(2, dtype=np.float32), _box_matrix(GRID_W, w)) for w in POOL_WINDOWS])
    return pl.pallas_call(
        functools.partial(_pool2d_body, tiles=tiles, rows=seq // GRID_W),
        out_shape=jax.ShapeDtypeStruct((bsz * seq, D_GROUP), BF16),
        grid=(bsz, tiles),
        in_specs=[pl.BlockSpec((tm, D_GROUP), lambda b, i: (b * tiles + i, 0)),
                  pl.BlockSpec((hb, D_GROUP), lambda b, i: (b * nhb + jnp.maximum(i * r - 1, 0), 0)),
                  pl.BlockSpec((hb, D_GROUP), lambda b, i: (b * nhb + jnp.minimum(i * r + r, nhb - 1), 0)),
                  pl.BlockSpec((4, 2 * GRID_W, 2 * GRID_W), lambda b, i: (0, 0, 0)),
                  pl.BlockSpec((D_GROUP, D_GROUP), lambda b, i: (0, 0)),
                  pl.BlockSpec((1, D_GROUP), lambda b, i: (0, 0))],
        out_specs=pl.BlockSpec((tm, D_GROUP), lambda b, i: (b * tiles + i, 0)),
        scratch_shapes=[pltpu.VMEM((tm + 2 * hb, D_GROUP), F32)],
        compiler_params=_cparams(("parallel", "parallel")),
        name="pool2d",
    )(u, u, u, jnp.asarray(pc, BF16), wbd, scale)


def _pool1d_body(x_ref, p_ref, w_ref, sc_ref, o_ref):
    x = x_ref[...]
    n = x.shape[0]
    ys = [jnp.dot(p_ref[wi], x, preferred_element_type=F32) for wi in range(4)]
    wl, half = _lane_windows((n, D_GROUP))
    z = jnp.where(wl == 2, ys[0], jnp.where(wl == 4, ys[1], jnp.where(wl == 8, ys[2], ys[3])))
    cnt = _box_count(_iota((n, D_GROUP), 0), wl, half, n)
    dlt = (z / cnt.astype(F32) - x).astype(BF16)
    o_ref[...] = (jnp.dot(dlt, w_ref[...], preferred_element_type=F32) * sc_ref[...]).astype(o_ref.dtype)


def _pool1d(u, wbd, scale, *, bsz, seq):
    pm = np.stack([_box_matrix(seq, w) for w in POOL_WINDOWS])
    return pl.pallas_call(
        _pool1d_body,
        out_shape=jax.ShapeDtypeStruct((bsz * seq, D_GROUP), BF16),
        grid=(bsz,),
        in_specs=[pl.BlockSpec((seq, D_GROUP), lambda b: (b, 0)),
                  pl.BlockSpec((4, seq, seq), lambda b: (0, 0, 0)),
                  pl.BlockSpec((D_GROUP, D_GROUP), lambda b: (0, 0)),
                  pl.BlockSpec((1, D_GROUP), lambda b: (0, 0))],
        out_specs=pl.BlockSpec((seq, D_GROUP), lambda b: (b, 0)),
        compiler_params=_cparams(("parallel",)),
        name="pool1d",
    )(u, jnp.asarray(pm, BF16), wbd, scale)


def _fill_halo(buf, cur, prev, nxt, i, tps, hb, tm, pre):
    first = (i % tps) == 0
    last = (i % tps) == tps - 1
    buf[0:hb, :] = jnp.where(first, 0.0, pre(prev[...]))
    buf[hb:hb + tm, :] = pre(cur[...])
    buf[hb + tm:hb + tm + hb, :] = jnp.where(last, 0.0, pre(nxt[...]))


def _tap_phases(taps, off):
    return sorted({(off + j) % SUBLANES for j in range(taps)} - {0})


def _fill_phases(sh, buf, phases):
    rows = sh.shape[1]
    for slot, s in enumerate(phases):
        sh[slot, :, :] = buf[s:s + rows, :]


def _dwconv(buf, sh, phases, w_ref, r0, rc, taps, off):
    acc = None
    for j in range(taps):
        s, q = (off + j) % SUBLANES, (off + j) // SUBLANES
        lo = r0 + SUBLANES * q
        src = buf[lo:lo + rc, :] if s == 0 else sh[phases.index(s), lo:lo + rc, :]
        term = src * w_ref[j:j + 1, :]
        acc = term if acc is None else acc + term
    return acc


def _conf_body(cur, prev, nxt, w_ref, b_ref, lg_ref, lb_ref, o_ref, buf, sh, *, tps):
    i = pl.program_id(0)
    tm = cur.shape[0]
    hb = prev.shape[0]

    def glu(u):
        u = u.astype(F32)
        return u[:, :D_GROUP] * jax.nn.sigmoid(u[:, D_GROUP:])

    _fill_halo(buf, cur, prev, nxt, i, tps, hb, tm, glu)
    rc = 128
    off = hb - (CONV_WIDTH - 1) // 2
    phases = _tap_phases(CONV_WIDTH, off)
    _fill_phases(sh, buf, phases)
    for r0 in range(0, tm, rc):
        h = _dwconv(buf, sh, phases, w_ref, r0, rc, CONV_WIDTH, off) + b_ref[...]
        mu = jnp.mean(h, axis=-1, keepdims=True)
        hc = h - mu
        var = jnp.mean(hc * hc, axis=-1, keepdims=True)
        o_ref[r0:r0 + rc, :] = _silu(hc * lax.rsqrt(var + EPS) * lg_ref[...] + lb_ref[...]).astype(o_ref.dtype)


def _halo_specs(tm, hb, width, nrows):
    r = tm // hb
    nhb = nrows // hb
    return [pl.BlockSpec((tm, width), lambda i: (i, 0)),
            pl.BlockSpec((hb, width), lambda i: (jnp.maximum(i * r - 1, 0), 0)),
            pl.BlockSpec((hb, width), lambda i: (jnp.minimum(i * r + r, nhb - 1), 0))]


def _conformer(u, w, b, lg, lb, *, seq, tm):
    n = u.shape[0]
    hb = 16
    nph = len(_tap_phases(CONV_WIDTH, hb - (CONV_WIDTH - 1) // 2))
    vec = pl.BlockSpec((1, D_GROUP), lambda i: (0, 0))
    return pl.pallas_call(
        functools.partial(_conf_body, tps=seq // tm),
        out_shape=jax.ShapeDtypeStruct((n, D_GROUP), BF16),
        grid=(n // tm,),
        in_specs=_halo_specs(tm, hb, 2 * D_GROUP, n) + [pl.BlockSpec((CONV_WIDTH, D_GROUP), lambda i: (0, 0)),
                                                        vec, vec, vec],
        out_specs=pl.BlockSpec((tm, D_GROUP), lambda i: (i, 0)),
        scratch_shapes=[pltpu.VMEM((tm + 2 * hb, D_GROUP), F32),
                        pltpu.VMEM((nph, tm + 2 * hb - SUBLANES, D_GROUP), F32)],
        compiler_params=_cparams(("parallel",), 48),
        name="conformer_conv",
    )(u, u, u, w, b.reshape(1, -1), lg.reshape(1, -1), lb.reshape(1, -1))


def _short_body(cur, prev, nxt, w_ref, b_ref, v_ref, x1_ref, x2_ref, buf, sh, *, tps):
    i = pl.program_id(0)
    tm = cur.shape[0]
    hb = prev.shape[0]
    _fill_halo(buf, cur, prev, nxt, i, tps, hb, tm, lambda u: u.astype(F32))
    rc = 128
    off = hb - (HY_SHORT - 1) // 2
    phases = _tap_phases(HY_SHORT, off)
    _fill_phases(sh, buf, phases)
    for r0 in range(0, tm, rc):
        uc = _dwconv(buf, sh, phases, w_ref, r0, rc, HY_SHORT, off) + b_ref[...]
        v_ref[r0:r0 + rc, :] = uc[:, :D_GROUP]
        x1_ref[r0:r0 + rc, :] = uc[:, D_GROUP:2 * D_GROUP].astype(x1_ref.dtype)
        x2_ref[r0:r0 + rc, :] = uc[:, 2 * D_GROUP:].astype(x2_ref.dtype)


def _hy_short(u, w, b, *, seq, tm):
    n = u.shape[0]
    hb = 16
    wd = 3 * D_GROUP
    nph = len(_tap_phases(HY_SHORT, hb - (HY_SHORT - 1) // 2))
    ospec = pl.BlockSpec((tm, D_GROUP), lambda i: (i, 0))
    return pl.pallas_call(
        functools.partial(_short_body, tps=seq // tm),
        out_shape=[jax.ShapeDtypeStruct((n, D_GROUP), dt) for dt in (F32, BF16, BF16)],
        grid=(n // tm,),
        in_specs=_halo_specs(tm, hb, wd, n) + [pl.BlockSpec((HY_SHORT, wd), lambda i: (0, 0)),
                                               pl.BlockSpec((1, wd), lambda i: (0, 0))],
        out_specs=[ospec, ospec, ospec],
        scratch_shapes=[pltpu.VMEM((tm + 2 * hb, wd), F32),
                        pltpu.VMEM((nph, tm + 2 * hb - SUBLANES, wd), F32)],
        compiler_params=_cparams(("parallel",), 48),
        name="hyena_short_conv",
    )(u, u, u, w, b.reshape(1, -1))


def _filter_features(n):
    i = np.arange(n, dtype=np.float64)
    t = np.linspace(0.0, 1.0, n, dtype=np.float32).astype(np.float64)
    wpos = ((2.0 * math.pi / n) * np.arange(n, dtype=np.float32)).astype(np.float32)
    bands = np.linspace(1e-4, HY_BANDS - 1, HY_BANDS, dtype=np.float32)
    arg = (bands[None, :] * wpos[:, None]).astype(np.float32).astype(np.float64)
    z = np.zeros((n, 64), np.float32)
    z[:, 0] = t
    z[:, 1:1 + HY_BANDS] = np.cos(arg)
    z[:, 1 + HY_BANDS:HY_EMB] = -np.sin(arg)
    del i
    return z


def _filter_body(z_ref, w1_ref, b1_ref, w2_ref, b2_ref, w3a_ref, w3b_ref, d_ref, hf_ref, hb_ref, s_ref):
    i = pl.program_id(0)
    half = z_ref.shape[0]
    z = z_ref[...]
    h = jnp.sin(jnp.dot(z, w1_ref[...], precision=HI, preferred_element_type=F32) + b1_ref[...])
    h = jnp.sin(jnp.dot(h, w2_ref[...], precision=HI, preferred_element_type=F32) + b2_ref[...])
    absd = jnp.abs(d_ref[...])
    h = h.astype(BF16)
    tot = None
    for part, (w3_ref, tcol) in enumerate(((w3a_ref, 0), (w3b_ref, 64))):
        hp = jnp.dot(h, w3_ref[...], preferred_element_type=F32)
        hp = hp * jnp.exp(-z[:, tcol:tcol + 1] * absd)
        rows = slice(part * half, (part + 1) * half)
        for o in range(2):
            c0 = 2 * D_GROUP * o
            hf_ref[rows, o * D_GROUP:(o + 1) * D_GROUP] = hp[:, c0:c0 + D_GROUP].astype(hf_ref.dtype)
            hb_ref[rows, o * D_GROUP:(o + 1) * D_GROUP] = hp[:, c0 + D_GROUP:c0 + 2 * D_GROUP].astype(hb_ref.dtype)
        part_sum = jnp.sum(jnp.abs(hp), axis=0, keepdims=True)
        tot = part_sum if tot is None else tot + part_sum

    @pl.when(i == 0)
    def _():
        s_ref[...] = jnp.zeros_like(s_ref)

    s_ref[...] += tot


def _hy_filter(n, w1, b1, w2, b2, w3, deltas, *, t2_major):
    tm = min(n, 1024)
    half = tm // 2
    z = _filter_features(n)
    if t2_major:
        z = z.reshape(n // DFT2, DFT2, 64).transpose(1, 0, 2).reshape(n, 64)
    zt = z.reshape(n // tm, 2, half, 64)
    z2 = jnp.asarray(np.concatenate([zt[:, 0], zt[:, 1]], axis=-1).reshape(n // 2, 2 * 64))
    w1p = jnp.zeros((64, HY_FFN), F32).at[:HY_EMB].set(w1)
    zero = jnp.zeros((64, HY_FFN), F32)
    bd = lambda w: jnp.concatenate([jnp.concatenate([w, zero], axis=1), jnp.concatenate([zero, w], axis=1)], axis=0)
    w3 = w3.astype(BF16)
    zero3 = jnp.zeros_like(w3)
    full = lambda shape: pl.BlockSpec(shape, lambda i: (0, 0))
    tok = pl.BlockSpec((tm, 2 * D_GROUP), lambda i: (i, 0))
    return pl.pallas_call(
        _filter_body,
        out_shape=[jax.ShapeDtypeStruct((n, 2 * D_GROUP), BF16), jax.ShapeDtypeStruct((n, 2 * D_GROUP), BF16),
                   jax.ShapeDtypeStruct((1, HY_FILTER_CH), F32)],
        grid=(n // tm,),
        in_specs=[pl.BlockSpec((half, 2 * 64), lambda i: (i, 0)), full((2 * 64, 2 * HY_FFN)), full((1, 2 * HY_FFN)),
                  full((2 * HY_FFN, 2 * HY_FFN)), full((1, 2 * HY_FFN)), full((2 * HY_FFN, HY_FILTER_CH)),
                  full((2 * HY_FFN, HY_FILTER_CH)), full((1, HY_FILTER_CH))],
        out_specs=[tok, tok, full((1, HY_FILTER_CH))],
        compiler_params=_cparams(("arbitrary",), 48),
        name="hyena_filter_mlp",
    )(z2, bd(w1p), jnp.tile(b1.reshape(1, -1), (1, 2)), bd(w2), jnp.tile(b2.reshape(1, -1), (1, 2)),
      jnp.concatenate([w3, zero3], axis=0), jnp.concatenate([zero3, w3], axis=0), deltas.reshape(1, -1))


@functools.lru_cache(maxsize=None)
def _dft_tables(n1):
    nn = n1 * DFT2
    j = np.arange(DFT2)
    th = 2.0 * np.pi * ((np.outer(j, j)) % DFT2) / DFT2
    fr, fi = np.cos(th), -np.sin(th)
    m_fwd = np.block([[fr, -fi], [fi, fr]])
    m_inv = np.block([[fr, fi], [-fi, fr]])
    hh = n1 // 2
    k1 = np.arange(n1)[None, :, None]
    t1 = np.arange(hh)[None, None, :]
    t2 = np.arange(DFT2)[:, None, None]
    ph = 2.0 * np.pi * ((k1 * (DFT2 * t1 + t2)) % nn) / nn
    ar, ai = np.cos(ph), -np.sin(ph)
    a_fwd = np.concatenate([np.concatenate([ar, -ai], axis=2), np.concatenate([ai, ar], axis=2)], axis=1)
    a_real = np.concatenate([ar, ai], axis=1)
    t1r = np.where(t2 >= 1, n1 - 1 - t1, (n1 - t1) % n1)
    phr = 2.0 * np.pi * ((k1 * (DFT2 * t1r + t2)) % nn) / nn
    a_rev = np.concatenate([np.cos(phr), -np.sin(phr)], axis=1)
    a_rev[0, :, 0] = 0.0
    pht = np.transpose(ph, (0, 2, 1))
    cr, ci = np.cos(pht) / nn, np.sin(pht) / nn
    a_inv = np.concatenate([np.concatenate([cr, -ci], axis=2), np.concatenate([ci, cr], axis=2)], axis=1)
    f32 = lambda a: np.ascontiguousarray(a, dtype=np.float32)
    return dict(m_fwd=f32(m_fwd), m_inv=f32(m_inv), a_fwd=f32(a_fwd), a_real=f32(a_real), a_rev=f32(a_rev),
                a_inv=f32(a_inv))


def _fft_a_body(x_ref, m_ref, o_ref, *, tj):
    n1 = o_ref.shape[2]
    for e in range(tj):
        x = jnp.concatenate([x_ref[0, :, e, :], x_ref[1, :, e, :]], axis=0).astype(BF16)
        a = jnp.dot(m_ref[e], x, preferred_element_type=F32)
        o_ref[0, e] = a[:n1]
        o_ref[1, e] = a[n1:]


def _fft_a(x4, m, *, n1, tj):
    c = x4.shape[3]
    return pl.pallas_call(
        functools.partial(_fft_a_body, tj=tj),
        out_shape=jax.ShapeDtypeStruct((2, DFT2, n1, c), F32),
        grid=(DFT2 // tj,),
        in_specs=[pl.BlockSpec((2, n1 // 2, tj, c), lambda j: (0, 0, j, 0)),
                  pl.BlockSpec((tj, 2 * n1, n1), lambda j: (j, 0, 0))],
        out_specs=pl.BlockSpec((2, tj, n1, c), lambda j: (0, j, 0, 0)),
        compiler_params=_cparams(("parallel",), 48),
        name="fft_stage_a",
    )(x4, m)


def _fft_b_body(a_ref, mf_ref, mi_ref, g_ref, o_ref, *, kg):
    for kk in range(kg):
        x = jnp.concatenate([a_ref[0, :, kk, :], a_ref[1, :, kk, :]], axis=0).astype(BF16)
        xf = jnp.dot(mf_ref[...], x, preferred_element_type=F32)
        xr, xi = xf[:DFT2], xf[DFT2:]
        gr, gi = g_ref[0, kk].astype(F32), g_ref[1, kk].astype(F32)
        y = jnp.concatenate([xr * gr - xi * gi, xr * gi + xi * gr], axis=0).astype(BF16)
        bf = jnp.dot(mi_ref[...], y, preferred_element_type=F32)
        o_ref[0, kk] = bf[:DFT2]
        o_ref[1, kk] = bf[DFT2:]


def _fft_b(a, g, order, tabs, *, n1):
    c = a.shape[3]
    kg = min(n1, FFT_GROUP_DATA)
    blk = pl.BlockSpec((2, kg, DFT2, c), lambda k: (0, k, 0, 0))
    mat = pl.BlockSpec((2 * DFT2, 2 * DFT2), lambda k: (0, 0))
    return pl.pallas_call(
        functools.partial(_fft_b_body, kg=kg),
        out_shape=jax.ShapeDtypeStruct((2, n1, DFT2, c), F32),
        grid=(n1 // kg,),
        in_specs=[pl.BlockSpec((2, DFT2, kg, c), lambda k: (0, 0, k, 0)), mat, mat,
                  pl.BlockSpec((2, kg, DFT2, c), lambda k: (0, k, 0, order))],
        out_specs=blk,
        compiler_params=_cparams(("parallel",), 56),
        name="fft_stage_b",
    )(a, tabs["m_fwd"], tabs["m_inv"], g)


def _fft_bf_body(a_ref, mf_ref, o_ref, *, kg):
    for kk in range(kg):
        x = jnp.concatenate([a_ref[0, :, kk, :], a_ref[1, :, kk, :]], axis=0).astype(BF16)
        xf = jnp.dot(mf_ref[...], x, preferred_element_type=F32)
        o_ref[0, kk] = xf[:DFT2].astype(BF16)
        o_ref[1, kk] = xf[DFT2:].astype(BF16)


def _fft_b_forward(a, tabs, *, n1):
    c = a.shape[3]
    kg = FFT_GROUP
    return pl.pallas_call(
        functools.partial(_fft_bf_body, kg=kg),
        out_shape=jax.ShapeDtypeStruct((2, n1, DFT2, c), BF16),
        grid=(n1 // kg,),
        in_specs=[pl.BlockSpec((2, DFT2, kg, c), lambda k: (0, 0, k, 0)),
                  pl.BlockSpec((2 * DFT2, 2 * DFT2), lambda k: (0, 0))],
        out_specs=pl.BlockSpec((2, kg, DFT2, c), lambda k: (0, k, 0, 0)),
        compiler_params=_cparams(("parallel",), 48),
        name="fft_stage_b_filter",
    )(a, tabs["m_fwd"])


def _fft_ai_body(b_ref, m_ref, v_ref, x_ref, bias_ref, o_ref, *, tj):
    h = o_ref.shape[1]
    for e in range(tj):
        b = jnp.concatenate([b_ref[0, :, e, :], b_ref[1, :, e, :]], axis=0).astype(BF16)
        y = jnp.dot(m_ref[e], b, preferred_element_type=F32)
        o_ref[0, :, e, :] = y[:h]
        o_ref[1, :, e, :] = y[h:]
    o_ref[...] = x_ref[...] * (o_ref[...] + v_ref[...] * bias_ref[...])


def _fft_a_inv(b, m, v4, xm4, bias, *, n1, tj):
    c = b.shape[3]
    half = pl.BlockSpec((2, n1 // 2, tj, c), lambda j: (0, 0, j, 0))
    return pl.pallas_call(
        functools.partial(_fft_ai_body, tj=tj),
        out_shape=jax.ShapeDtypeStruct((2, n1 // 2, DFT2, c), F32),
        grid=(DFT2 // tj,),
        in_specs=[pl.BlockSpec((2, n1, tj, c), lambda j: (0, 0, j, 0)),
                  pl.BlockSpec((tj, n1, 2 * n1), lambda j: (j, 0, 0)),
                  half, half, pl.BlockSpec((1, c), lambda j: (0, 0))],
        out_specs=half,
        compiler_params=_cparams(("parallel",), 56),
        name="fft_stage_a_inv",
    )(b, m, v4, xm4, bias)


def _filt_a_body(hj_ref, hz_ref, hr_ref, s_ref, mf_ref, mr_ref, o_ref, *, tj):
    n1 = o_ref.shape[2]
    s = s_ref[...]
    for e in range(tj):
        src = hz_ref if e == 0 else hr_ref
        me = 0 if e == 0 else tj - e
        for o in range(2):
            c0 = 2 * D_GROUP * o
            inv = 1.0 / (s[:, c0:c0 + D_GROUP] + s[:, c0 + D_GROUP:c0 + 2 * D_GROUP] + EPS)
            lsl = slice(o * D_GROUP, (o + 1) * D_GROUP)
            hf = hj_ref[e, :, lsl].astype(BF16)
            hb = src[me, :, lsl].astype(BF16)
            a = (jnp.dot(mf_ref[e], hf, preferred_element_type=F32)
                 + jnp.dot(mr_ref[e], hb, preferred_element_type=F32)) * inv
            o_ref[0, e, :, lsl] = a[:n1]
            o_ref[1, e, :, lsl] = a[n1:]


def _filt_a(hf, hb, colsum, tabs, *, n1):
    tj = FFT_GROUP
    nj = DFT2 // tj
    wd = 2 * D_GROUP
    shp = (DFT2, n1 // 2, wd)
    blk = lambda fn: pl.BlockSpec((tj, n1 // 2, wd), fn)
    mat = pl.BlockSpec((tj, 2 * n1, n1 // 2), lambda j: (j, 0, 0))
    return pl.pallas_call(
        functools.partial(_filt_a_body, tj=tj),
        out_shape=jax.ShapeDtypeStruct((2, DFT2, n1, wd), F32),
        grid=(nj,),
        in_specs=[blk(lambda j: (j, 0, 0)), blk(lambda j: ((nj - j) % nj, 0, 0)), blk(lambda j: (nj - 1 - j, 0, 0)),
                  pl.BlockSpec((1, HY_FILTER_CH), lambda j: (0, 0)), mat, mat],
        out_specs=pl.BlockSpec((2, tj, n1, wd), lambda j: (0, j, 0, 0)),
        compiler_params=_cparams(("parallel",), 48),
        name="filter_stage_a",
    )(hf.reshape(shp), hb.reshape(shp), hb.reshape(shp), colsum, tabs["a_real"], tabs["a_rev"])


def _hyena_long(v, x1, x2, hf, hb, colsum, bias, *, seq):
    n1 = 2 * seq // DFT2
    tabs = {k: jnp.asarray(a).astype(BF16) for k, a in _dft_tables(n1).items()}
    tj = FFT_GROUP_DATA
    fa = _filt_a(hf, hb, colsum, tabs, n1=n1)
    g = _fft_b_forward(fa, tabs, n1=n1)
    shp = (2, n1 // 2, DFT2, D_GROUP)
    z = v.reshape(shp)
    for order, xm in ((0, x1.reshape(shp)), (1, x2.reshape(shp))):
        a = _fft_a(z, tabs["a_fwd"], n1=n1, tj=tj)
        b = _fft_b(a, g, order, tabs, n1=n1)
        z = _fft_a_inv(b, tabs["a_inv"], z, xm, bias[order].reshape(1, D_GROUP), n1=n1, tj=tj)
    return z.reshape(2 * seq, D_GROUP)


def _hyena_ctx_body(v_ref, x1_ref, x2_ref, hf_ref, hb_ref, s_ref, bias_ref, cm_ref, sm_ref, ct_ref, st_ref, o_ref):
    n = v_ref.shape[1]
    cm, sm, ct, st = cm_ref[...], sm_ref[...], ct_ref[...], st_ref[...]
    s = s_ref[...]
    row0 = _iota((n, D_GROUP), 0) == 0
    dot = lambda a, b: jnp.dot(a, b.astype(BF16), preferred_element_type=F32)
    zr, zi = v_ref[0], v_ref[1]
    for order, xm in ((0, x1_ref), (1, x2_ref)):
        c0 = 2 * D_GROUP * order
        inv = 1.0 / (s[:, c0:c0 + D_GROUP] + s[:, c0 + D_GROUP:c0 + 2 * D_GROUP] + EPS)
        lsl = slice(order * D_GROUP, (order + 1) * D_GROUP)
        hf = hf_ref[:, lsl] * inv
        hb = jnp.where(row0, 0.0, hb_ref[:, lsl] * inv)
        gr = dot(cm, hf + hb)
        gi = dot(sm, hb - hf)
        xr = dot(cm, zr) + dot(sm, zi)
        xi = dot(cm, zi) - dot(sm, zr)
        yr = xr * gr - xi * gi
        yi = xr * gi + xi * gr
        cr = dot(ct, yr) - dot(st, yi)
        ci = dot(ct, yi) + dot(st, yr)
        bias = bias_ref[order:order + 1, :]
        zr = xm[0] * (cr + zr * bias)
        zi = xm[1] * (ci + zi * bias)
    o_ref[0] = zr
    o_ref[1] = zi


def _hyena_ctx(v, x1, x2, hf, hb, colsum, bias, *, seq):
    nn = 2 * seq
    k = np.arange(nn)
    t = np.arange(seq)
    th = 2.0 * np.pi * (np.outer(k, t) % nn) / nn
    cm, sm = np.cos(th), np.sin(th)
    consts = [jnp.asarray(a, F32).astype(BF16) for a in (cm, sm, cm.T / nn, sm.T / nn)]
    shp = (2, seq, D_GROUP)
    full3 = pl.BlockSpec(shp, lambda i: (0, 0, 0))
    f2 = lambda a: pl.BlockSpec(a.shape, lambda i: (0, 0))
    args = [hf, hb, colsum, bias] + consts
    out = pl.pallas_call(
        _hyena_ctx_body,
        out_shape=jax.ShapeDtypeStruct(shp, F32),
        grid=(1,),
        in_specs=[full3, full3, full3] + [f2(a) for a in args],
        out_specs=full3,
        compiler_params=_cparams(("arbitrary",)),
        name="hyena_ctx",
    )(v.reshape(shp), x1.reshape(shp), x2.reshape(shp), *args)
    return out.reshape(2 * seq, D_GROUP)


def _split_w_in(w_in):
    wb = w_in.astype(BF16)
    parts = dict(kq=jnp.concatenate([wb[:, COL_K:COL_V], wb[:, COL_Q:COL_R]], axis=1), v=wb[:, COL_V:COL_GF],
                 r=wb[:, COL_R:COL_POOL], pool=wb[:, COL_POOL:COL_HY], hy=wb[:, COL_HY:COL_CONV],
                 conv=wb[:, COL_CONV:P_IN])
    gate = jnp.pad(w_in[:, COL_GF:COL_Q], ((0, 0), (0, LANE - 2 * GLA_LOWRANK)))
    return parts, gate


def _mix(u, p, *, bsz, seq, is_ctx, states, filt):
    sf, sb = states
    cpb = min(16, seq // CHUNK)
    if is_ctx:
        pool = _pool1d(u["pool"], p["pool_wbd"], p["pool_scale"], bsz=bsz, seq=seq)
    else:
        pool = _pool2d(u["pool"], p["pool_wbd"], p["pool_scale"], bsz=bsz, seq=seq)
    tmc = min(seq, 2048)
    v, x1, x2 = _hy_short(u["hy"], p["hy_short_w"], p["hy_short_b"], seq=seq, tm=tmc)
    if is_ctx:
        hy = _hyena_ctx(v, x1, x2, *filt, p["hy_bias"], seq=seq)
    else:
        hy = _hyena_long(v, x1, x2, *filt, p["hy_bias"], seq=seq)
    gla = _gla_read(u["kq"], u["v"], u["kq"], u["r"], u["lf"], u["lb"], sf, sb, p["gla_ng"],
                    bsz=bsz, seq=seq, cpb=cpb)
    conv = _conformer(u["conv"], p["conv_dw_w"], p["conv_dw_b"], p["conv_ln_g"], p["conv_ln_b"], seq=seq, tm=tmc)
    return [pool, hy, gla, conv]


def kernel(x, c, ctx, c_ctx, ada_w, ada_b, ffn1_norm, ffn1_wi, ffn1_wo, mix_norm, w_in, w_out, pool_w, pool_scale, hy_short_w, hy_short_b, hy_w1, hy_b1, hy_w2, hy_b2, hy_w3, hy_deltas, hy_bias, gla_gw_f, gla_gb_f, gla_gw_b, gla_gb_b, gla_norm, conv_dw_w, conv_dw_b, conv_ln_g, conv_ln_b, ffn2_norm, ffn2_wi, ffn2_wo, final_norm):
    bsz, seq, d = x.shape
    clen = ctx.shape[1]
    depth = ada_w.shape[0]
    assert bsz == 2, "the Hyena transform packs exactly two batch rows into one complex signal"
    xs = x.reshape(bsz * seq, d)
    cs = ctx.reshape(bsz * clen, d)
    cc = jnp.concatenate([c, c_ctx[None, :], jnp.zeros((8 - bsz - 1, d), F32)], axis=0)
    mod = _modulation(cc, ada_w, ada_b).reshape(depth, 8, N_MOD, d)
    crow = bsz
    head_eye = np.kron(np.eye(len(POOL_WINDOWS), dtype=np.float32), np.ones((POOL_CH, POOL_CH), np.float32))
    tmx = TOKEN_TILE
    tps_x = seq // tmx
    tmc = bsz * clen
    zero_state = jnp.zeros((bsz, GLA_HEADS * GLA_DV, GLA_QK), F32)
    wi1, wo1, wi2, wo2, wout = (w.astype(BF16) for w in (ffn1_wi, ffn1_wo, ffn2_wi, ffn2_wo, w_out))

    for l in range(depth):
        last = l == depth - 1
        wparts, wgate = _split_w_in(w_in[l])
        zlow = jnp.zeros((GLA_LOWRANK, GLA_QK), F32)
        gw = jnp.concatenate([jnp.concatenate([gla_gw_f[l], zlow], axis=1),
                              jnp.concatenate([zlow, gla_gw_b[l]], axis=1),
                              jnp.zeros((LANE - 2 * GLA_LOWRANK, 2 * GLA_QK), F32)], axis=0)
        gb = jnp.concatenate([gla_gb_f[l], gla_gb_b[l]]).reshape(1, -1)
        gw = _gate_fold(wgate, gw).astype(BF16)
        wbd = jnp.tile(pool_w[l].reshape(D_GROUP, POOL_CH), (1, len(POOL_WINDOWS))) * head_eye
        p = dict(pool_wbd=wbd.astype(BF16), pool_scale=pool_scale[l].reshape(1, -1),
                 hy_short_w=hy_short_w[l], hy_short_b=hy_short_b[l], hy_bias=hy_bias[l],
                 gla_ng=jnp.tile(gla_norm[l], GLA_HEADS).reshape(1, -1),
                 conv_dw_w=conv_dw_w[l], conv_dw_b=conv_dw_b[l], conv_ln_g=conv_ln_g[l], conv_ln_b=conv_ln_b[l])
        names = ["kq", "v", "r", "pool", "hy", "conv"]

        xs, *outs = _ffn_proj(xs, mod, ffn1_norm[l], wi1, wo1, mix_norm[l], [wparts[nm] for nm in names], gw, gb,
                              layer=l, row0=0, tm=2 * tmx, tiles_per_seq=tps_x // 2)
        ux = dict(zip(names + ["lf", "lb"], outs))
        cnames = ["kq", "v"] if last else names
        cs, *outs = _ffn_proj(cs, mod, ffn1_norm[l], wi1, wo1, mix_norm[l], [wparts[nm] for nm in cnames], gw, gb,
                              layer=l, row0=crow, tm=tmc, tiles_per_seq=1)
        uc = dict(zip(cnames + ["lf", "lb"], outs))

        ccpb = clen // CHUNK
        sfc, sbc, finf, finb = _gla_states(uc["kq"], uc["v"], uc["lf"], uc["lb"], zero_state, zero_state,
                                           bsz=bsz, seq=clen, cpb=ccpb)
        sfx, sbx, _, _ = _gla_states(ux["kq"], ux["v"], ux["lf"], ux["lb"], finf, finb, bsz=bsz, seq=seq,
                                     cpb=16)

        filt = _hy_filter(seq, hy_w1[l], hy_b1[l], hy_w2[l], hy_b2[l], hy_w3[l], hy_deltas[l], t2_major=True)
        mixed = _mix(ux, p, bsz=bsz, seq=seq, is_ctx=False, states=(sfx, sbx), filt=filt)
        xs = _out_ffn(xs, mod, mixed, wout, ffn2_norm[l], wi2, wo2, layer=l, row0=0, tm=2 * tmx,
                      tiles_per_seq=tps_x // 2, final_g=final_norm if last else None)
        if not last:
            filt_c = _hy_filter(clen, hy_w1[l], hy_b1[l], hy_w2[l], hy_b2[l], hy_w3[l], hy_deltas[l],
                                t2_major=False)
            mixed = _mix(uc, p, bsz=bsz, seq=clen, is_ctx=True, states=(sfc, sbc), filt=filt_c)
            cs = _out_ffn(cs, mod, mixed, wout, ffn2_norm[l], wi2, wo2, layer=l, row0=crow, tm=tmc,
                          tiles_per_seq=1)
    return xs.reshape(bsz, seq, d)
```

```python
import functools
import math

import numpy as np
import jax
import jax.numpy as jnp
from jax import lax
from jax.experimental import pallas as pl
from jax.experimental.pallas import tpu as pltpu

F32 = jnp.float32
BF16 = jnp.bfloat16
HI = lax.Precision.HIGHEST

GRID_W = 64
D_GROUP = 256
N_MOD = 9
EPS = 1e-6
POOL_WINDOWS = (2, 4, 8, 16)
POOL_CH = 64
HY_BANDS = 16
HY_EMB = 1 + 2 * HY_BANDS
HY_FFN = 64
HY_FILTER_CH = 4 * D_GROUP
GLA_HEADS = 4
GLA_DK = 32
GLA_DV = 64
GLA_QK = 128
GLA_LOWRANK = 16
GLA_TAU = 16.0
CHUNK = 64
CONV_WIDTH = 31
HY_SHORT = 3

COL_K = 0
COL_V = COL_K + GLA_QK
COL_GF = COL_V + D_GROUP
COL_GB = COL_GF + GLA_LOWRANK
COL_Q = COL_GB + GLA_LOWRANK
COL_R = COL_Q + GLA_QK
COL_POOL = COL_R + D_GROUP
COL_HY = COL_POOL + D_GROUP
COL_CONV = COL_HY + 3 * D_GROUP
P_IN = COL_CONV + 2 * D_GROUP

LANE = 128
SUBLANES = 8
MIB = 1024 * 1024

DFT2 = 128
FFT_GROUP = 16
FFT_GROUP_DATA = 32
MOD_ROWS = 3
FF_CHUNK = 256
ROW_SUB = 256
TOKEN_TILE = 512


def _cparams(sem, vmem_mib=None):
    kw = dict(dimension_semantics=sem)
    if vmem_mib is not None:
        kw["vmem_limit_bytes"] = vmem_mib * MIB
    return pltpu.CompilerParams(**kw)


def _silu(x):
    return x * jax.nn.sigmoid(x)


def _rms_mod(h, g, m):
    y = h * lax.rsqrt(jnp.mean(h * h, axis=-1, keepdims=True) + EPS) * g
    return y * (1.0 + m[1:2, :]) + m[0:1, :]


def _iota(shape, dim):
    return lax.broadcasted_iota(jnp.int32, shape, dim)


def _mod_body(at_ref, w_ref, b_ref, o_ref):
    tk, nm = w_ref.shape[1], w_ref.shape[2]

    @pl.when(pl.program_id(1) == 0)
    def _():
        o_ref[0] = jnp.zeros((SUBLANES, nm), F32) + b_ref[0]

    at = _silu(at_ref[...])
    cols = [jnp.broadcast_to(at[:, r:r + 1], (tk, LANE)).reshape(tk // SUBLANES, SUBLANES, LANE)
            for r in range(MOD_ROWS)]
    for j in range(nm // LANE):
        lsl = slice(j * LANE, (j + 1) * LANE)
        w3 = w_ref[0, :, lsl].reshape(tk // SUBLANES, SUBLANES, LANE)
        for r in range(MOD_ROWS):
            part = jnp.sum(w3 * cols[r], axis=0)
            o_ref[0, r:r + 1, lsl] += jnp.sum(part, axis=0, keepdims=True)


def _modulation(cc, ada_w, ada_b):
    nl, d, nm = ada_w.shape
    tk = 256
    return pl.pallas_call(
        _mod_body,
        out_shape=jax.ShapeDtypeStruct((nl, 8, nm), F32),
        grid=(nl, d // tk),
        in_specs=[pl.BlockSpec((tk, 8), lambda l, j: (j, 0)),
                  pl.BlockSpec((1, tk, nm), lambda l, j: (l, j, 0)),
                  pl.BlockSpec((1, 1, nm), lambda l, j: (l, 0, 0))],
        out_specs=pl.BlockSpec((1, 8, nm), lambda l, j: (l, 0, 0)),
        compiler_params=_cparams(("parallel", "arbitrary"), 40),
        name="adaln_mod",
    )(cc.T, ada_w, ada_b.reshape(nl, 1, nm))


def _swiglu(xn, wi_ref, wo_ref, hm_ref, rows):
    ff = wo_ref.shape[0]
    for c in range(0, ff, FF_CHUNK):
        a = jnp.dot(xn, wi_ref[:, c:c + FF_CHUNK], preferred_element_type=F32)
        g = jnp.dot(xn, wi_ref[:, ff + c:ff + c + FF_CHUNK], preferred_element_type=F32)
        hm_ref[rows, c:c + FF_CHUNK] = (_silu(g) * a).astype(BF16)
    return jnp.dot(hm_ref[rows, :], wo_ref[...], preferred_element_type=F32)


def _resident(shape):
    nd = len(shape)
    return pl.BlockSpec(shape, lambda i: (0,) * nd, pipeline_mode=pl.Buffered(1))


def _resident_layer(shape, layer):
    nd = len(shape)
    return pl.BlockSpec((None,) + tuple(shape), lambda i: (layer,) + (0,) * nd, pipeline_mode=pl.Buffered(1))


def _gate_fold_body(wg_ref, gw_ref, o_ref):
    o_ref[...] = jnp.dot(wg_ref[...], gw_ref[...], precision=HI, preferred_element_type=F32)


def _gate_fold(w_gate, gw):
    d = w_gate.shape[0]
    return pl.pallas_call(
        _gate_fold_body,
        out_shape=jax.ShapeDtypeStruct((d, gw.shape[1]), F32),
        grid=(1,),
        in_specs=[pl.BlockSpec(w_gate.shape, lambda i: (0, 0)), pl.BlockSpec(gw.shape, lambda i: (0, 0))],
        out_specs=pl.BlockSpec((d, gw.shape[1]), lambda i: (0, 0)),
        name="gate_fold",
    )(w_gate, gw)


def _chunk_prefix(x):
    pos = _iota(x.shape, 0) & (CHUNK - 1)
    shift = 1
    while shift < CHUNK:
        x = x + jnp.where(pos >= shift, pltpu.roll(x, shift, 0), 0.0)
        shift *= 2
    return x


def _ffn_proj_body(h_ref, m_ref, g1_ref, wi_ref, wo_ref, g2_ref, *rest, nparts, row_sub):
    w_refs = rest[:nparts]
    gw_ref, gb_ref = rest[nparts:nparts + 2]
    x_ref = rest[nparts + 2]
    o_refs = rest[nparts + 3:2 * nparts + 5]
    hm_ref = rest[-1]
    m = m_ref[0]
    for r0 in range(0, h_ref.shape[0], row_sub):
        rows = slice(r0, r0 + row_sub)
        h = h_ref[rows, :]
        xn = _rms_mod(h, g1_ref[...], m[0:2]).astype(BF16)
        x1 = h + (0.5 * m[2:3, :]) * _swiglu(xn, wi_ref, wo_ref, hm_ref, rows)
        x_ref[rows, :] = x1
        xn2 = _rms_mod(x1, g2_ref[...], m[3:5]).astype(BF16)
        for w_ref, o_ref in zip(w_refs, o_refs[:nparts]):
            o_ref[rows, :] = jnp.dot(xn2, w_ref[...], preferred_element_type=F32).astype(o_ref.dtype)
        a = jnp.dot(xn2, gw_ref[...], preferred_element_type=F32) + gb_ref[...]
        ls = (jnp.minimum(a, 0.0) - jnp.log(1.0 + jnp.exp(-jnp.abs(a)))) * (1.0 / GLA_TAU)
        pre = _chunk_prefix(ls)
        o_refs[nparts][rows, :] = pre[:, :GLA_QK]
        pb = pre[:, GLA_QK:]
        tot = jnp.concatenate([jnp.broadcast_to(pb[c0 + CHUNK - 1:c0 + CHUNK, :], (CHUNK, GLA_QK))
                               for c0 in range(0, row_sub, CHUNK)], axis=0)
        o_refs[nparts + 1][rows, :] = tot - pb + ls[:, GLA_QK:]


def _mod_spec(d, layer, row0, tiles_per_seq):
    return pl.BlockSpec((None, 1, N_MOD, d), lambda i: (layer, row0 + i // tiles_per_seq, 0, 0))


def _ffn_proj(h, mod, g1, wi, wo, g2, w_parts, gw, gb, *, layer, row0, tm, tiles_per_seq):
    n, d = h.shape
    ff = wo.shape[1]
    nparts = len(w_parts)
    widths = [w.shape[1] for w in w_parts]
    tok = lambda wd: pl.BlockSpec((tm, wd), lambda i: (i, 0))
    in_specs = [tok(d), _mod_spec(d, layer, row0, tiles_per_seq), _resident((1, d)),
                _resident_layer((d, 2 * ff), layer), _resident_layer((ff, d), layer), _resident((1, d))]
    in_specs += [_resident((d, wd)) for wd in widths]
    in_specs += [_resident((d, 2 * GLA_QK)), _resident((1, 2 * GLA_QK))]
    out_shape = [jax.ShapeDtypeStruct((n, d), F32)]
    out_shape += [jax.ShapeDtypeStruct((n, wd), BF16) for wd in widths]
    out_shape += [jax.ShapeDtypeStruct((n, GLA_QK), F32)] * 2
    return pl.pallas_call(
        functools.partial(_ffn_proj_body, nparts=nparts, row_sub=min(tm, 2 * ROW_SUB)),
        out_shape=out_shape,
        grid=(n // tm,),
        in_specs=in_specs,
        out_specs=[tok(d)] + [tok(wd) for wd in widths] + [tok(GLA_QK)] * 2,
        scratch_shapes=[pltpu.VMEM((tm, ff), BF16)],
        compiler_params=_cparams(("parallel",), 60),
        name="ffn1_in_proj",
    )(h, mod, g1.reshape(1, d), wi, wo, g2.reshape(1, d), *w_parts, gw, gb)


def _out_ffn_body(x_ref, m_ref, p_ref, hy_ref, gl_ref, cv_ref, wout_ref, g_ref, wi_ref, wo_ref, fn_ref, o_ref,
                  hm_ref, *, final, row_sub):
    m = m_ref[0]
    for r0 in range(0, x_ref.shape[0], row_sub):
        rows = slice(r0, r0 + row_sub)
        acc = jnp.dot(p_ref[rows, :].astype(BF16), wout_ref[0:D_GROUP, :], preferred_element_type=F32)
        acc += jnp.dot(hy_ref[rows, :].astype(BF16), wout_ref[D_GROUP:2 * D_GROUP, :], preferred_element_type=F32)
        acc += jnp.dot(gl_ref[rows, :].astype(BF16), wout_ref[2 * D_GROUP:3 * D_GROUP, :],
                       preferred_element_type=F32)
        acc += jnp.dot(cv_ref[rows, :].astype(BF16), wout_ref[3 * D_GROUP:, :], preferred_element_type=F32)
        x2 = x_ref[rows, :] + m[5:6, :] * acc
        xn = _rms_mod(x2, g_ref[...], m[6:8]).astype(BF16)
        out = x2 + (0.5 * m[8:9, :]) * _swiglu(xn, wi_ref, wo_ref, hm_ref, rows)
        if final:
            out = out * lax.rsqrt(jnp.mean(out * out, axis=-1, keepdims=True) + EPS) * fn_ref[...]
        o_ref[rows, :] = out


def _out_ffn(x, mod, parts, w_out, g, wi, wo, *, layer, row0, tm, tiles_per_seq, final_g=None):
    n, d = x.shape
    ff = wo.shape[1]
    final = final_g is not None
    fg = final_g if final else g
    tok = lambda wd: pl.BlockSpec((tm, wd), lambda i: (i, 0))
    return pl.pallas_call(
        functools.partial(_out_ffn_body, final=final, row_sub=min(tm, 2 * ROW_SUB)),
        out_shape=jax.ShapeDtypeStruct((n, d), F32),
        grid=(n // tm,),
        in_specs=[tok(d), _mod_spec(d, layer, row0, tiles_per_seq),
                  tok(D_GROUP), tok(D_GROUP), tok(D_GROUP), tok(D_GROUP), _resident_layer((d, d), layer),
                  _resident((1, d)), _resident_layer((d, 2 * ff), layer), _resident_layer((ff, d), layer),
                  _resident((1, d))],
        out_specs=tok(d),
        scratch_shapes=[pltpu.VMEM((tm, ff), BF16)],
        compiler_params=_cparams(("parallel",), 56),
        name="out_proj_ffn2",
    )(x, mod, *parts, w_out, g.reshape(1, d), wi, wo, fg.reshape(1, d))


def _gla_state_body(kf, vf, bf, kb, vb, bb, s0f, s0b, sf_o, sb_o, ff_o, fb_o, stf, stb, *, cpb):
    i = pl.program_id(1)

    @pl.when(i == 0)
    def _():
        stf[...] = s0f[0]
        stb[...] = s0b[0]

    sshape = (GLA_HEADS * GLA_DV, GLA_QK)
    bmask = (_iota(sshape, 0) >> 6) == (_iota(sshape, 1) >> 5)
    tn_dims = (((0,), (0,)), ((), ()))

    def direction(k_ref, v_ref, b_ref, last, st, s_o, order):
        s = st[...]
        for ci in order:
            sl = slice(ci * CHUNK, (ci + 1) * CHUNK)
            b = b_ref[sl, :]
            tot = b[last:last + 1, :]
            kd = (k_ref[sl, :] * jnp.exp(tot - b)).astype(BF16)
            upd = lax.dot_general(v_ref[sl, :].astype(BF16), kd, tn_dims, preferred_element_type=F32)
            s_o[0, ci] = s.astype(BF16)
            s = s * jnp.exp(tot) + jnp.where(bmask, upd, 0.0)
        st[...] = s

    direction(kf, vf, bf, CHUNK - 1, stf, sf_o, range(cpb))
    direction(kb, vb, bb, 0, stb, sb_o, range(cpb - 1, -1, -1))
    ff_o[0] = stf[...]
    fb_o[0] = stb[...]


def _gla_states(k, v, lf, lb, s0f, s0b, *, bsz, seq, cpb):
    bt = cpb * CHUNK
    nb = seq // bt
    nc = seq // CHUNK
    srow = GLA_HEADS * GLA_DV

    def tf(b, i):
        return (b * nb + i, 0)

    def tb(b, i):
        return (b * nb + nb - 1 - i, 0)

    sblk = pl.BlockSpec((1, srow, GLA_QK), lambda b, i: (b, 0, 0))
    return pl.pallas_call(
        functools.partial(_gla_state_body, cpb=cpb),
        out_shape=[jax.ShapeDtypeStruct((bsz, nc, srow, GLA_QK), BF16),
                   jax.ShapeDtypeStruct((bsz, nc, srow, GLA_QK), BF16),
                   jax.ShapeDtypeStruct((bsz, srow, GLA_QK), F32),
                   jax.ShapeDtypeStruct((bsz, srow, GLA_QK), F32)],
        grid=(bsz, nb),
        in_specs=[pl.BlockSpec((bt, GLA_QK), tf), pl.BlockSpec((bt, D_GROUP), tf), pl.BlockSpec((bt, GLA_QK), tf),
                  pl.BlockSpec((bt, GLA_QK), tb), pl.BlockSpec((bt, D_GROUP), tb), pl.BlockSpec((bt, GLA_QK), tb),
                  sblk, sblk],
        out_specs=[pl.BlockSpec((1, cpb, srow, GLA_QK), lambda b, i: (b, i, 0, 0)),
                   pl.BlockSpec((1, cpb, srow, GLA_QK), lambda b, i: (b, nb - 1 - i, 0, 0)),
                   sblk, sblk],
        scratch_shapes=[pltpu.VMEM((srow, GLA_QK), F32), pltpu.VMEM((srow, GLA_QK), F32)],
        compiler_params=_cparams(("parallel", "arbitrary")),
        name="gla_states",
    )(k, v, lf, k, v, lb, s0f, s0b)


def _gla_read_body(k_ref, v_ref, q_ref, r_ref, bf_ref, bb_ref, sf_ref, sb_ref, ng_ref, o_ref, acc_ref, *, cpb):
    hrows = GLA_HEADS * CHUNK
    cpos = _iota((hrows, CHUNK), 0) & (CHUNK - 1)
    ccol = _iota((hrows, CHUNK), 1)
    lowm = cpos >= ccol
    upm = cpos <= ccol
    hq = (_iota((hrows, GLA_QK), 0) >> 6) == (_iota((hrows, GLA_QK), 1) >> 5)
    ho = (_iota((hrows, D_GROUP), 0) >> 6) == (_iota((hrows, D_GROUP), 1) >> 6)
    bavg = jnp.where((_iota((D_GROUP, D_GROUP), 0) >> 6) == (_iota((D_GROUP, D_GROUP), 1) >> 6),
                     1.0 / GLA_DV, 0.0).astype(BF16)
    nt_dims = (((1,), (1,)), ((), ()))

    bf_ = bf_ref[...]
    bb_ = bb_ref[...]
    qs = q_ref[...].astype(F32) * (GLA_DK ** -0.5)
    kk = k_ref[...]
    qef = (qs * jnp.exp(bf_)).astype(BF16)
    qeb = (qs * jnp.exp(bb_)).astype(BF16)
    kef = (kk * jnp.exp(-bf_)).astype(BF16)
    keb = (kk * jnp.exp(-bb_)).astype(BF16)
    vb16 = v_ref[...].astype(BF16)
    zero = jnp.zeros((), BF16)
    for ci in range(cpb):
        sl = slice(ci * CHUNK, (ci + 1) * CHUNK)
        qf4 = jnp.where(hq, jnp.concatenate([qef[sl]] * GLA_HEADS, axis=0), zero)
        qb4 = jnp.where(hq, jnp.concatenate([qeb[sl]] * GLA_HEADS, axis=0), zero)
        af = lax.dot_general(qf4, kef[sl], nt_dims, preferred_element_type=F32)
        ab = lax.dot_general(qb4, keb[sl], nt_dims, preferred_element_type=F32)
        att = (jnp.where(lowm, af, 0.0) + jnp.where(upm, ab, 0.0)).astype(BF16)
        oall = jnp.dot(att, vb16[sl], preferred_element_type=F32)
        om = jnp.where(ho, oall, 0.0)
        o = om[0:CHUNK] + om[CHUNK:2 * CHUNK] + om[2 * CHUNK:3 * CHUNK] + om[3 * CHUNK:4 * CHUNK]
        qcat = jnp.concatenate([qef[sl], qeb[sl]], axis=1)
        scat = jnp.concatenate([sf_ref[0, ci], sb_ref[0, ci]], axis=1)
        acc_ref[sl, :] = o + lax.dot_general(qcat, scat, nt_dims, preferred_element_type=F32)
    o = acc_ref[...]
    ms = _chunk_sum_rhs(o * o, bavg)
    o_ref[...] = (o * lax.rsqrt(ms + EPS) * ng_ref[...] * _silu(r_ref[...].astype(F32))).astype(o_ref.dtype)


def _chunk_sum_rhs(x, mat):
    hi, lo = _split_bf16(x)
    return jnp.dot(hi, mat, preferred_element_type=F32) + jnp.dot(lo, mat, preferred_element_type=F32)


def _gla_read(k, v, q, r, lf, lb, sf, sb, ng, *, bsz, seq, cpb):
    bt = cpb * CHUNK
    nb = seq // bt
    srow = GLA_HEADS * GLA_DV
    n = bsz * seq

    def tk(i):
        return (i, 0)

    sspec = pl.BlockSpec((1, cpb, srow, GLA_QK), lambda i: (i // nb, i % nb, 0, 0))
    return pl.pallas_call(
        functools.partial(_gla_read_body, cpb=cpb),
        out_shape=jax.ShapeDtypeStruct((n, D_GROUP), BF16),
        scratch_shapes=[pltpu.VMEM((bt, D_GROUP), F32)],
        grid=(bsz * nb,),
        in_specs=[pl.BlockSpec((bt, GLA_QK), tk), pl.BlockSpec((bt, D_GROUP), tk),
                  pl.BlockSpec((bt, GLA_QK), lambda i: (i, 1)), pl.BlockSpec((bt, D_GROUP), tk),
                  pl.BlockSpec((bt, GLA_QK), tk), pl.BlockSpec((bt, GLA_QK), tk),
                  sspec, sspec, pl.BlockSpec((1, D_GROUP), lambda i: (0, 0))],
        out_specs=pl.BlockSpec((bt, D_GROUP), tk),
        compiler_params=_cparams(("parallel",)),
        name="gla_readout",
    )(k, v, q, r, lf, lb, sf, sb, ng)


def _box_matrix(n, w):
    pos = np.arange(n)
    lo = np.clip(pos - w // 2, 0, n)
    hi = np.clip(pos - w // 2 + w, 0, n)
    col = np.arange(n)[None, :]
    return ((col >= lo[:, None]) & (col < hi[:, None])).astype(np.float32)


def _lane_windows(shape):
    w = jnp.left_shift(2, _iota(shape, 1) >> 6)
    return w, w >> 1


def _box_count(pos, w, half, n):
    return jnp.minimum(pos - half + w, n) - jnp.maximum(pos - half, 0)


def _split_bf16(x):
    hi = x.astype(BF16)
    lo = (x - hi.astype(F32)).astype(BF16)
    return hi, lo


def _pool2d_body(cur_ref, prev_ref, next_ref, pc_ref, w_ref, sc_ref, o_ref, ycol, *, tiles, rows):
    i = pl.program_id(1)
    tm = cur_ref.shape[0]
    hb = prev_ref.shape[0]
    sub = 2 * GRID_W
    pflag = jnp.where(i > 0, 1.0, 0.0)
    nflag = jnp.where(i < tiles - 1, 1.0, 0.0)

    def colpool(x):
        halves = []
        for half in range(2):
            lsl = slice(half * LANE, (half + 1) * LANE)
            ys = [jnp.dot(pc_ref[wi], x[:, lsl], preferred_element_type=F32) for wi in (2 * half, 2 * half + 1)]
            lane = _iota((sub, LANE), 1)
            halves.append(jnp.where(lane < POOL_CH, ys[0], ys[1]))
        return jnp.concatenate(halves, axis=1)

    for s in range(hb // sub):
        ycol[s * sub:(s + 1) * sub, :] = colpool(prev_ref[s * sub:(s + 1) * sub, :]) * pflag
    for s in range(tm // sub):
        ycol[hb + s * sub:hb + (s + 1) * sub, :] = colpool(cur_ref[s * sub:(s + 1) * sub, :])
    for s in range(hb // sub):
        ycol[hb + tm + s * sub:hb + tm + (s + 1) * sub, :] = colpool(next_ref[s * sub:(s + 1) * sub, :]) * nflag

    rc = 256
    wl, half = _lane_windows((rc, D_GROUP))
    narrow = _iota((rc, LANE), 1) < POOL_CH
    for r0 in range(0, tm, rc):
        def band(lo, hi, lsl):
            base = hb + r0
            acc = ycol[base + GRID_W * lo:base + GRID_W * lo + rc, lsl]
            for dd in range(lo + 1, hi):
                acc = acc + ycol[base + GRID_W * dd:base + GRID_W * dd + rc, lsl]
            return acc

        left, right = slice(0, LANE), slice(LANE, 2 * LANE)
        z2 = band(-1, 1, left)
        z4 = z2 + band(-2, -1, left) + band(1, 2, left)
        z8 = band(-4, 4, right)
        z16 = z8 + band(-8, -4, right) + band(4, 8, right)
        z = jnp.concatenate([jnp.where(narrow, z2, z4), jnp.where(narrow, z8, z16)], axis=1)
        tok = _iota((rc, D_GROUP), 0) + (i * tm + r0)
        rcnt = _box_count(tok >> 6, wl, half, rows)
        ccnt = _box_count(tok & (GRID_W - 1), wl, half, GRID_W)
        pooled = z / (rcnt * ccnt).astype(F32)
        dlt = (pooled - cur_ref[r0:r0 + rc, :]).astype(BF16)
        o_ref[r0:r0 + rc, :] = (jnp.dot(dlt, w_ref[...], preferred_element_type=F32) * sc_ref[...]).astype(o_ref.dtype)


def _pool2d(u, wbd, scale, *, bsz, seq):
    tm = min(seq, 2048)
    hb = 512
    tiles = seq // tm
    r = tm // hb
    nhb = seq // hb
    pc = np.stack([np.kron(np.eye(2, dtype=np.float32), _box_matrix(GRID_W, w)) for w in POOL_WINDOWS])
    return pl.pallas_call(
        functools.partial(_pool2d_body, tiles=tiles, rows=seq // GRID_W),
        out_shape=jax.ShapeDtypeStruct((bsz * seq, D_GROUP), BF16),
        grid=(bsz, tiles),
        in_specs=[pl.BlockSpec((tm, D_GROUP), lambda b, i: (b * tiles + i, 0)),
                  pl.BlockSpec((hb, D_GROUP), lambda b, i: (b * nhb + jnp.maximum(i * r - 1, 0), 0)),
                  pl.BlockSpec((hb, D_GROUP), lambda b, i: (b * nhb + jnp.minimum(i * r + r, nhb - 1), 0)),
                  pl.BlockSpec((4, 2 * GRID_W, 2 * GRID_W), lambda b, i: (0, 0, 0)),
                  pl.BlockSpec((D_GROUP, D_GROUP), lambda b, i: (0, 0)),
                  pl.BlockSpec((1, D_GROUP), lambda b, i: (0, 0))],
        out_specs=pl.BlockSpec((tm, D_GROUP), lambda b, i: (b * tiles + i, 0)),
        scratch_shapes=[pltpu.VMEM((tm + 2 * hb, D_GROUP), F32)],
        compiler_params=_cparams(("parallel", "parallel")),
        name="pool2d",
    )(u, u, u, jnp.asarray(pc, BF16), wbd, scale)


def _pool1d_body(x_ref, p_ref, w_ref, sc_ref, o_ref):
    x = x_ref[...]
    n = x.shape[0]
    ys = [jnp.dot(p_ref[wi], x, preferred_element_type=F32) for wi in range(4)]
    wl, half = _lane_windows((n, D_GROUP))
    z = jnp.where(wl == 2, ys[0], jnp.where(wl == 4, ys[1], jnp.where(wl == 8, ys[2], ys[3])))
    cnt = _box_count(_iota((n, D_GROUP), 0), wl, half, n)
    dlt = (z / cnt.astype(F32) - x).astype(BF16)
    o_ref[...] = (jnp.dot(dlt, w_ref[...], preferred_element_type=F32) * sc_ref[...]).astype(o_ref.dtype)


def _pool1d(u, wbd, scale, *, bsz, seq):
    pm = np.stack([_box_matrix(seq, w) for w in POOL_WINDOWS])
    return pl.pallas_call(
        _pool1d_body,
        out_shape=jax.ShapeDtypeStruct((bsz * seq, D_GROUP), BF16),
        grid=(bsz,),
        in_specs=[pl.BlockSpec((seq, D_GROUP), lambda b: (b, 0)),
                  pl.BlockSpec((4, seq, seq), lambda b: (0, 0, 0)),
                  pl.BlockSpec((D_GROUP, D_GROUP), lambda b: (0, 0)),
                  pl.BlockSpec((1, D_GROUP), lambda b: (0, 0))],
        out_specs=pl.BlockSpec((seq, D_GROUP), lambda b: (b, 0)),
        compiler_params=_cparams(("parallel",)),
        name="pool1d",
    )(u, jnp.asarray(pm, BF16), wbd, scale)


def _fill_halo(buf, cur, prev, nxt, i, tps, hb, tm, pre):
    first = (i % tps) == 0
    last = (i % tps) == tps - 1
    buf[0:hb, :] = jnp.where(first, 0.0, pre(prev[...]))
    buf[hb:hb + tm, :] = pre(cur[...])
    buf[hb + tm:hb + tm + hb, :] = jnp.where(last, 0.0, pre(nxt[...]))


def _tap_phases(taps, off):
    return sorted({(off + j) % SUBLANES for j in range(taps)} - {0})


def _fill_phases(sh, buf, phases):
    rows = sh.shape[1]
    for slot, s in enumerate(phases):
        sh[slot, :, :] = buf[s:s + rows, :]


def _dwconv(buf, sh, phases, w_ref, r0, rc, taps, off):
    acc = None
    for j in range(taps):
        s, q = (off + j) % SUBLANES, (off + j) // SUBLANES
        lo = r0 + SUBLANES * q
        src = buf[lo:lo + rc, :] if s == 0 else sh[phases.index(s), lo:lo + rc, :]
        term = src * w_ref[j:j + 1, :]
        acc = term if acc is None else acc + term
    return acc


def _conf_body(cur, prev, nxt, w_ref, b_ref, lg_ref, lb_ref, o_ref, buf, sh, *, tps):
    i = pl.program_id(0)
    tm = cur.shape[0]
    hb = prev.shape[0]

    def glu(u):
        u = u.astype(F32)
        return u[:, :D_GROUP] * jax.nn.sigmoid(u[:, D_GROUP:])

    _fill_halo(buf, cur, prev, nxt, i, tps, hb, tm, glu)
    rc = 128
    off = hb - (CONV_WIDTH - 1) // 2
    phases = _tap_phases(CONV_WIDTH, off)
    _fill_phases(sh, buf, phases)
    for r0 in range(0, tm, rc):
        h = _dwconv(buf, sh, phases, w_ref, r0, rc, CONV_WIDTH, off) + b_ref[...]
        mu = jnp.mean(h, axis=-1, keepdims=True)
        hc = h - mu
        var = jnp.mean(hc * hc, axis=-1, keepdims=True)
        o_ref[r0:r0 + rc, :] = _silu(hc * lax.rsqrt(var + EPS) * lg_ref[...] + lb_ref[...]).astype(o_ref.dtype)


def _halo_specs(tm, hb, width, nrows):
    r = tm // hb
    nhb = nrows // hb
    return [pl.BlockSpec((tm, width), lambda i: (i, 0)),
            pl.BlockSpec((hb, width), lambda i: (jnp.maximum(i * r - 1, 0), 0)),
            pl.BlockSpec((hb, width), lambda i: (jnp.minimum(i * r + r, nhb - 1), 0))]


def _conformer(u, w, b, lg, lb, *, seq, tm):
    n = u.shape[0]
    hb = 16
    nph = len(_tap_phases(CONV_WIDTH, hb - (CONV_WIDTH - 1) // 2))
    vec = pl.BlockSpec((1, D_GROUP), lambda i: (0, 0))
    return pl.pallas_call(
        functools.partial(_conf_body, tps=seq // tm),
        out_shape=jax.ShapeDtypeStruct((n, D_GROUP), BF16),
        grid=(n // tm,),
        in_specs=_halo_specs(tm, hb, 2 * D_GROUP, n) + [pl.BlockSpec((CONV_WIDTH, D_GROUP), lambda i: (0, 0)),
                                                        vec, vec, vec],
        out_specs=pl.BlockSpec((tm, D_GROUP), lambda i: (i, 0)),
        scratch_shapes=[pltpu.VMEM((tm + 2 * hb, D_GROUP), F32),
                        pltpu.VMEM((nph, tm + 2 * hb - SUBLANES, D_GROUP), F32)],
        compiler_params=_cparams(("parallel",), 48),
        name="conformer_conv",
    )(u, u, u, w, b.reshape(1, -1), lg.reshape(1, -1), lb.reshape(1, -1))


def _short_body(cur, prev, nxt, w_ref, b_ref, v_ref, x1_ref, x2_ref, buf, sh, *, tps):
    i = pl.program_id(0)
    tm = cur.shape[0]
    hb = prev.shape[0]
    _fill_halo(buf, cur, prev, nxt, i, tps, hb, tm, lambda u: u.astype(F32))
    rc = 128
    off = hb - (HY_SHORT - 1) // 2
    phases = _tap_phases(HY_SHORT, off)
    _fill_phases(sh, buf, phases)
    for r0 in range(0, tm, rc):
        uc = _dwconv(buf, sh, phases, w_ref, r0, rc, HY_SHORT, off) + b_ref[...]
        v_ref[r0:r0 + rc, :] = uc[:, :D_GROUP]
        x1_ref[r0:r0 + rc, :] = uc[:, D_GROUP:2 * D_GROUP].astype(x1_ref.dtype)
        x2_ref[r0:r0 + rc, :] = uc[:, 2 * D_GROUP:].astype(x2_ref.dtype)


def _hy_short(u, w, b, *, seq, tm):
    n = u.shape[0]
    hb = 16
    wd = 3 * D_GROUP
    nph = len(_tap_phases(HY_SHORT, hb - (HY_SHORT - 1) // 2))
    ospec = pl.BlockSpec((tm, D_GROUP), lambda i: (i, 0))
    return pl.pallas_call(
        functools.partial(_short_body, tps=seq // tm),
        out_shape=[jax.ShapeDtypeStruct((n, D_GROUP), dt) for dt in (F32, BF16, BF16)],
        grid=(n // tm,),
        in_specs=_halo_specs(tm, hb, wd, n) + [pl.BlockSpec((HY_SHORT, wd), lambda i: (0, 0)),
                                               pl.BlockSpec((1, wd), lambda i: (0, 0))],
        out_specs=[ospec, ospec, ospec],
        scratch_shapes=[pltpu.VMEM((tm + 2 * hb, wd), F32),
                        pltpu.VMEM((nph, tm + 2 * hb - SUBLANES, wd), F32)],
        compiler_params=_cparams(("parallel",), 48),
        name="hyena_short_conv",
    )(u, u, u, w, b.reshape(1, -1))


def _filter_features(n):
    i = np.arange(n, dtype=np.float64)
    t = np.linspace(0.0, 1.0, n, dtype=np.float32).astype(np.float64)
    wpos = ((2.0 * math.pi / n) * np.arange(n, dtype=np.float32)).astype(np.float32)
    bands = np.linspace(1e-4, HY_BANDS - 1, HY_BANDS, dtype=np.float32)
    arg = (bands[None, :] * wpos[:, None]).astype(np.float32).astype(np.float64)
    z = np.zeros((n, 64), np.float32)
    z[:, 0] = t
    z[:, 1:1 + HY_BANDS] = np.cos(arg)
    z[:, 1 + HY_BANDS:HY_EMB] = -np.sin(arg)
    del i
    return z


def _filter_body(z_ref, w1_ref, b1_ref, w2_ref, b2_ref, w3a_ref, w3b_ref, d_ref, hf_ref, hb_ref, s_ref):
    i = pl.program_id(0)
    half = z_ref.shape[0]
    z = z_ref[...]
    h = jnp.sin(jnp.dot(z, w1_ref[...], precision=HI, preferred_element_type=F32) + b1_ref[...])
    h = jnp.sin(jnp.dot(h, w2_ref[...], precision=HI, preferred_element_type=F32) + b2_ref[...])
    absd = jnp.abs(d_ref[...])
    h = h.astype(BF16)
    tot = None
    for part, (w3_ref, tcol) in enumerate(((w3a_ref, 0), (w3b_ref, 64))):
        hp = jnp.dot(h, w3_ref[...], preferred_element_type=F32)
        hp = hp * jnp.exp(-z[:, tcol:tcol + 1] * absd)
        rows = slice(part * half, (part + 1) * half)
        for o in range(2):
            c0 = 2 * D_GROUP * o
            hf_ref[rows, o * D_GROUP:(o + 1) * D_GROUP] = hp[:, c0:c0 + D_GROUP].astype(hf_ref.dtype)
            hb_ref[rows, o * D_GROUP:(o + 1) * D_GROUP] = hp[:, c0 + D_GROUP:c0 + 2 * D_GROUP].astype(hb_ref.dtype)
        part_sum = jnp.sum(jnp.abs(hp), axis=0, keepdims=True)
        tot = part_sum if tot is None else tot + part_sum

    @pl.when(i == 0)
    def _():
        s_ref[...] = jnp.zeros_like(s_ref)

    s_ref[...] += tot


def _hy_filter(n, w1, b1, w2, b2, w3, deltas, *, t2_major):
    tm = min(n, 2048)
    half = tm // 2
    z = _filter_features(n)
    if t2_major:
        z = z.reshape(n // DFT2, DFT2, 64).transpose(1, 0, 2).reshape(n, 64)
    zt = z.reshape(n // tm, 2, half, 64)
    z2 = jnp.asarray(np.concatenate([zt[:, 0], zt[:, 1]], axis=-1).reshape(n // 2, 2 * 64))
    w1p = jnp.zeros((64, HY_FFN), F32).at[:HY_EMB].set(w1)
    zero = jnp.zeros((64, HY_FFN), F32)
    bd = lambda w: jnp.concatenate([jnp.concatenate([w, zero], axis=1), jnp.concatenate([zero, w], axis=1)], axis=0)
    w3 = w3.astype(BF16)
    zero3 = jnp.zeros_like(w3)
    full = lambda shape: pl.BlockSpec(shape, lambda i: (0, 0))
    tok = pl.BlockSpec((tm, 2 * D_GROUP), lambda i: (i, 0))
    return pl.pallas_call(
        _filter_body,
        out_shape=[jax.ShapeDtypeStruct((n, 2 * D_GROUP), BF16), jax.ShapeDtypeStruct((n, 2 * D_GROUP), BF16),
                   jax.ShapeDtypeStruct((1, HY_FILTER_CH), F32)],
        grid=(n // tm,),
        in_specs=[pl.BlockSpec((half, 2 * 64), lambda i: (i, 0)), full((2 * 64, 2 * HY_FFN)), full((1, 2 * HY_FFN)),
                  full((2 * HY_FFN, 2 * HY_FFN)), full((1, 2 * HY_FFN)), full((2 * HY_FFN, HY_FILTER_CH)),
                  full((2 * HY_FFN, HY_FILTER_CH)), full((1, HY_FILTER_CH))],
        out_specs=[tok, tok, full((1, HY_FILTER_CH))],
        compiler_params=_cparams(("arbitrary",), 48),
        name="hyena_filter_mlp",
    )(z2, bd(w1p), jnp.tile(b1.reshape(1, -1), (1, 2)), bd(w2), jnp.tile(b2.reshape(1, -1), (1, 2)),
      jnp.concatenate([w3, zero3], axis=0), jnp.concatenate([zero3, w3], axis=0), deltas.reshape(1, -1))


@functools.lru_cache(maxsize=None)
def _dft_tables(n1):
    nn = n1 * DFT2
    j = np.arange(DFT2)
    th = 2.0 * np.pi * ((np.outer(j, j)) % DFT2) / DFT2
    fr, fi = np.cos(th), -np.sin(th)
    m_fwd = np.block([[fr, -fi], [fi, fr]])
    m_inv = np.block([[fr, fi], [-fi, fr]])
    hh = n1 // 2
    k1 = np.arange(n1)[None, :, None]
    t1 = np.arange(hh)[None, None, :]
    t2 = np.arange(DFT2)[:, None, None]
    ph = 2.0 * np.pi * ((k1 * (DFT2 * t1 + t2)) % nn) / nn
    ar, ai = np.cos(ph), -np.sin(ph)
    a_fwd = np.concatenate([np.concatenate([ar, -ai], axis=2), np.concatenate([ai, ar], axis=2)], axis=1)
    a_real = np.concatenate([ar, ai], axis=1)
    t1r = np.where(t2 >= 1, n1 - 1 - t1, (n1 - t1) % n1)
    phr = 2.0 * np.pi * ((k1 * (DFT2 * t1r + t2)) % nn) / nn
    a_rev = np.concatenate([np.cos(phr), -np.sin(phr)], axis=1)
    a_rev[0, :, 0] = 0.0
    pht = np.transpose(ph, (0, 2, 1))
    cr, ci = np.cos(pht) / nn, np.sin(pht) / nn
    a_inv = np.concatenate([np.concatenate([cr, -ci], axis=2), np.concatenate([ci, cr], axis=2)], axis=1)
    f32 = lambda a: np.ascontiguousarray(a, dtype=np.float32)
    return dict(m_fwd=f32(m_fwd), m_inv=f32(m_inv), a_fwd=f32(a_fwd), a_real=f32(a_real), a_rev=f32(a_rev),
                a_inv=f32(a_inv))


def _fft_a_body(x_ref, m_ref, o_ref, *, tj):
    n1 = o_ref.shape[2]
    for e in range(tj):
        x = jnp.concatenate([x_ref[0, :, e, :], x_ref[1, :, e, :]], axis=0).astype(BF16)
        a = jnp.dot(m_ref[e], x, preferred_element_type=F32)
        o_ref[0, e] = a[:n1]
        o_ref[1, e] = a[n1:]


def _fft_a(x4, m, *, n1, tj):
    c = x4.shape[3]
    return pl.pallas_call(
        functools.partial(_fft_a_body, tj=tj),
        out_shape=jax.ShapeDtypeStruct((2, DFT2, n1, c), F32),
        grid=(DFT2 // tj,),
        in_specs=[pl.BlockSpec((2, n1 // 2, tj, c), lambda j: (0, 0, j, 0)),
                  pl.BlockSpec((tj, 2 * n1, n1), lambda j: (j, 0, 0))],
        out_specs=pl.BlockSpec((2, tj, n1, c), lambda j: (0, j, 0, 0)),
        compiler_params=_cparams(("parallel",), 48),
        name="fft_stage_a",
    )(x4, m)


def _fft_b_body(a_ref, mf_ref, mi_ref, g_ref, o_ref, *, kg):
    for kk in range(kg):
        x = jnp.concatenate([a_ref[0, :, kk, :], a_ref[1, :, kk, :]], axis=0).astype(BF16)
        xf = jnp.dot(mf_ref[...], x, preferred_element_type=F32)
        xr, xi = xf[:DFT2], xf[DFT2:]
        gr, gi = g_ref[0, kk].astype(F32), g_ref[1, kk].astype(F32)
        y = jnp.concatenate([xr * gr - xi * gi, xr * gi + xi * gr], axis=0).astype(BF16)
        bf = jnp.dot(mi_ref[...], y, preferred_element_type=F32)
        o_ref[0, kk] = bf[:DFT2]
        o_ref[1, kk] = bf[DFT2:]


def _fft_b(a, g, order, tabs, *, n1):
    c = a.shape[3]
    kg = min(n1, FFT_GROUP_DATA)
    blk = pl.BlockSpec((2, kg, DFT2, c), lambda k: (0, k, 0, 0))
    mat = pl.BlockSpec((2 * DFT2, 2 * DFT2), lambda k: (0, 0))
    return pl.pallas_call(
        functools.partial(_fft_b_body, kg=kg),
        out_shape=jax.ShapeDtypeStruct((2, n1, DFT2, c), F32),
        grid=(n1 // kg,),
        in_specs=[pl.BlockSpec((2, DFT2, kg, c), lambda k: (0, 0, k, 0)), mat, mat,
                  pl.BlockSpec((2, kg, DFT2, c), lambda k: (0, k, 0, order))],
        out_specs=blk,
        compiler_params=_cparams(("parallel",), 56),
        name="fft_stage_b",
    )(a, tabs["m_fwd"], tabs["m_inv"], g)


def _fft_bf_body(a_ref, mf_ref, o_ref, *, kg):
    for kk in range(kg):
        x = jnp.concatenate([a_ref[0, :, kk, :], a_ref[1, :, kk, :]], axis=0).astype(BF16)
        xf = jnp.dot(mf_ref[...], x, preferred_element_type=F32)
        o_ref[0, kk] = xf[:DFT2].astype(BF16)
        o_ref[1, kk] = xf[DFT2:].astype(BF16)


def _fft_b_forward(a, tabs, *, n1):
    c = a.shape[3]
    kg = FFT_GROUP
    return pl.pallas_call(
        functools.partial(_fft_bf_body, kg=kg),
        out_shape=jax.ShapeDtypeStruct((2, n1, DFT2, c), BF16),
        grid=(n1 // kg,),
        in_specs=[pl.BlockSpec((2, DFT2, kg, c), lambda k: (0, 0, k, 0)),
                  pl.BlockSpec((2 * DFT2, 2 * DFT2), lambda k: (0, 0))],
        out_specs=pl.BlockSpec((2, kg, DFT2, c), lambda k: (0, k, 0, 0)),
        compiler_params=_cparams(("parallel",), 48),
        name="fft_stage_b_filter",
    )(a, tabs["m_fwd"])


def _fft_ai_body(b_ref, m_ref, v_ref, x_ref, bias_ref, o_ref, *, tj):
    h = o_ref.shape[1]
    for e in range(tj):
        b = jnp.concatenate([b_ref[0, :, e, :], b_ref[1, :, e, :]], axis=0).astype(BF16)
        y = jnp.dot(m_ref[e], b, preferred_element_type=F32)
        o_ref[0, :, e, :] = y[:h]
        o_ref[1, :, e, :] = y[h:]
    o_ref[...] = x_ref[...] * (o_ref[...] + v_ref[...] * bias_ref[...])


def _fft_a_inv(b, m, v4, xm4, bias, *, n1, tj):
    c = b.shape[3]
    half = pl.BlockSpec((2, n1 // 2, tj, c), lambda j: (0, 0, j, 0))
    return pl.pallas_call(
        functools.partial(_fft_ai_body, tj=tj),
        out_shape=jax.ShapeDtypeStruct((2, n1 // 2, DFT2, c), F32),
        grid=(DFT2 // tj,),
        in_specs=[pl.BlockSpec((2, n1, tj, c), lambda j: (0, 0, j, 0)),
                  pl.BlockSpec((tj, n1, 2 * n1), lambda j: (j, 0, 0)),
                  half, half, pl.BlockSpec((1, c), lambda j: (0, 0))],
        out_specs=half,
        compiler_params=_cparams(("parallel",), 56),
        name="fft_stage_a_inv",
    )(b, m, v4, xm4, bias)


def _filt_a_body(hj_ref, hz_ref, hr_ref, s_ref, mf_ref, mr_ref, o_ref, *, tj):
    n1 = o_ref.shape[2]
    s = s_ref[...]
    for e in range(tj):
        src = hz_ref if e == 0 else hr_ref
        me = 0 if e == 0 else tj - e
        for o in range(2):
            c0 = 2 * D_GROUP * o
            inv = 1.0 / (s[:, c0:c0 + D_GROUP] + s[:, c0 + D_GROUP:c0 + 2 * D_GROUP] + EPS)
            lsl = slice(o * D_GROUP, (o + 1) * D_GROUP)
            hf = hj_ref[e, :, lsl].astype(BF16)
            hb = src[me, :, lsl].astype(BF16)
            a = (jnp.dot(mf_ref[e], hf, preferred_element_type=F32)
                 + jnp.dot(mr_ref[e], hb, preferred_element_type=F32)) * inv
            o_ref[0, e, :, lsl] = a[:n1]
            o_ref[1, e, :, lsl] = a[n1:]


def _filt_a(hf, hb, colsum, tabs, *, n1):
    tj = FFT_GROUP
    nj = DFT2 // tj
    wd = 2 * D_GROUP
    shp = (DFT2, n1 // 2, wd)
    blk = lambda fn: pl.BlockSpec((tj, n1 // 2, wd), fn)
    mat = pl.BlockSpec((tj, 2 * n1, n1 // 2), lambda j: (j, 0, 0))
    return pl.pallas_call(
        functools.partial(_filt_a_body, tj=tj),
        out_shape=jax.ShapeDtypeStruct((2, DFT2, n1, wd), F32),
        grid=(nj,),
        in_specs=[blk(lambda j: (j, 0, 0)), blk(lambda j: ((nj - j) % nj, 0, 0)), blk(lambda j: (nj - 1 - j, 0, 0)),
                  pl.BlockSpec((1, HY_FILTER_CH), lambda j: (0, 0)), mat, mat],
        out_specs=pl.BlockSpec((2, tj, n1, wd), lambda j: (0, j, 0, 0)),
        compiler_params=_cparams(("parallel",), 48),
        name="filter_stage_a",
    )(hf.reshape(shp), hb.reshape(shp), hb.reshape(shp), colsum, tabs["a_real"], tabs["a_rev"])


def _hyena_long(v, x1, x2, hf, hb, colsum, bias, *, seq):
    n1 = 2 * seq // DFT2
    tabs = {k: jnp.asarray(a).astype(BF16) for k, a in _dft_tables(n1).items()}
    tj = FFT_GROUP_DATA
    fa = _filt_a(hf, hb, colsum, tabs, n1=n1)
    g = _fft_b_forward(fa, tabs, n1=n1)
    shp = (2, n1 // 2, DFT2, D_GROUP)
    z = v.reshape(shp)
    for order, xm in ((0, x1.reshape(shp)), (1, x2.reshape(shp))):
        a = _fft_a(z, tabs["a_fwd"], n1=n1, tj=tj)
        b = _fft_b(a, g, order, tabs, n1=n1)
        z = _fft_a_inv(b, tabs["a_inv"], z, xm, bias[order].reshape(1, D_GROUP), n1=n1, tj=tj)
    return z.reshape(2 * seq, D_GROUP)


def _hyena_ctx_body(v_ref, x1_ref, x2_ref, hf_ref, hb_ref, s_ref, bias_ref, cm_ref, sm_ref, ct_ref, st_ref, o_ref):
    n = v_ref.shape[1]
    cm, sm, ct, st = cm_ref[...], sm_ref[...], ct_ref[...], st_ref[...]
    s = s_ref[...]
    row0 = _iota((n, D_GROUP), 0) == 0
    dot = lambda a, b: jnp.dot(a, b.astype(BF16), preferred_element_type=F32)
    zr, zi = v_ref[0], v_ref[1]
    for order, xm in ((0, x1_ref), (1, x2_ref)):
        c0 = 2 * D_GROUP * order
        inv = 1.0 / (s[:, c0:c0 + D_GROUP] + s[:, c0 + D_GROUP:c0 + 2 * D_GROUP] + EPS)
        lsl = slice(order * D_GROUP, (order + 1) * D_GROUP)
        hf = hf_ref[:, lsl] * inv
        hb = jnp.where(row0, 0.0, hb_ref[:, lsl] * inv)
        gr = dot(cm, hf + hb)
        gi = dot(sm, hb - hf)
        xr = dot(cm, zr) + dot(sm, zi)
        xi = dot(cm, zi) - dot(sm, zr)
        yr = xr * gr - xi * gi
        yi = xr * gi + xi * gr
        cr = dot(ct, yr) - dot(st, yi)
        ci = dot(ct, yi) + dot(st, yr)
        bias = bias_ref[order:order + 1, :]
        zr = xm[0] * (cr + zr * bias)
        zi = xm[1] * (ci + zi * bias)
    o_ref[0] = zr
    o_ref[1] = zi


def _hyena_ctx(v, x1, x2, hf, hb, colsum, bias, *, seq):
    nn = 2 * seq
    k = np.arange(nn)
    t = np.arange(seq)
    th = 2.0 * np.pi * (np.outer(k, t) % nn) / nn
    cm, sm = np.cos(th), np.sin(th)
    consts = [jnp.asarray(a, F32).astype(BF16) for a in (cm, sm, cm.T / nn, sm.T / nn)]
    shp = (2, seq, D_GROUP)
    full3 = pl.BlockSpec(shp, lambda i: (0, 0, 0))
    f2 = lambda a: pl.BlockSpec(a.shape, lambda i: (0, 0))
    args = [hf, hb, colsum, bias] + consts
    out = pl.pallas_call(
        _hyena_ctx_body,
        out_shape=jax.ShapeDtypeStruct(shp, F32),
        grid=(1,),
        in_specs=[full3, full3, full3] + [f2(a) for a in args],
        out_specs=full3,
        compiler_params=_cparams(("arbitrary",)),
        name="hyena_ctx",
    )(v.reshape(shp), x1.reshape(shp), x2.reshape(shp), *args)
    return out.reshape(2 * seq, D_GROUP)


def _split_w_in(w_in):
    wb = w_in.astype(BF16)
    parts = dict(kq=jnp.concatenate([wb[:, COL_K:COL_V], wb[:, COL_Q:COL_R]], axis=1), v=wb[:, COL_V:COL_GF],
                 r=wb[:, COL_R:COL_POOL], pool=wb[:, COL_POOL:COL_HY], hy=wb[:, COL_HY:COL_CONV],
                 conv=wb[:, COL_CONV:P_IN])
    gate = jnp.pad(w_in[:, COL_GF:COL_Q], ((0, 0), (0, LANE - 2 * GLA_LOWRANK)))
    return parts, gate


def _mix(u, p, *, bsz, seq, is_ctx, states, filt):
    sf, sb = states
    cpb = min(32, seq // CHUNK)
    if is_ctx:
        pool = _pool1d(u["pool"], p["pool_wbd"], p["pool_scale"], bsz=bsz, seq=seq)
    else:
        pool = _pool2d(u["pool"], p["pool_wbd"], p["pool_scale"], bsz=bsz, seq=seq)
    tmc = min(seq, 2048)
    v, x1, x2 = _hy_short(u["hy"], p["hy_short_w"], p["hy_short_b"], seq=seq, tm=tmc)
    if is_ctx:
        hy = _hyena_ctx(v, x1, x2, *filt, p["hy_bias"], seq=seq)
    else:
        hy = _hyena_long(v, x1, x2, *filt, p["hy_bias"], seq=seq)
    gla = _gla_read(u["kq"], u["v"], u["kq"], u["r"], u["lf"], u["lb"], sf, sb, p["gla_ng"],
                    bsz=bsz, seq=seq, cpb=cpb)
    conv = _conformer(u["conv"], p["conv_dw_w"], p["conv_dw_b"], p["conv_ln_g"], p["conv_ln_b"], seq=seq, tm=tmc)
    return [pool, hy, gla, conv]


def kernel(x, c, ctx, c_ctx, ada_w, ada_b, ffn1_norm, ffn1_wi, ffn1_wo, mix_norm, w_in, w_out, pool_w, pool_scale, hy_short_w, hy_short_b, hy_w1, hy_b1, hy_w2, hy_b2, hy_w3, hy_deltas, hy_bias, gla_gw_f, gla_gb_f, gla_gw_b, gla_gb_b, gla_norm, conv_dw_w, conv_dw_b, conv_ln_g, conv_ln_b, ffn2_norm, ffn2_wi, ffn2_wo, final_norm):
    bsz, seq, d = x.shape
    clen = ctx.shape[1]
    depth = ada_w.shape[0]
    assert bsz == 2, "the Hyena transform packs exactly two batch rows into one complex signal"
    xs = x.reshape(bsz * seq, d)
    cs = ctx.reshape(bsz * clen, d)
    cc = jnp.concatenate([c, c_ctx[None, :], jnp.zeros((8 - bsz - 1, d), F32)], axis=0)
    mod = _modulation(cc, ada_w, ada_b).reshape(depth, 8, N_MOD, d)
    crow = bsz
    head_eye = np.kron(np.eye(len(POOL_WINDOWS), dtype=np.float32), np.ones((POOL_CH, POOL_CH), np.float32))
    tmx = TOKEN_TILE
    tps_x = seq // tmx
    tmc = bsz * clen
    zero_state = jnp.zeros((bsz, GLA_HEADS * GLA_DV, GLA_QK), F32)
    wi1, wo1, wi2, wo2, wout = (w.astype(BF16) for w in (ffn1_wi, ffn1_wo, ffn2_wi, ffn2_wo, w_out))

    for l in range(depth):
        last = l == depth - 1
        wparts, wgate = _split_w_in(w_in[l])
        zlow = jnp.zeros((GLA_LOWRANK, GLA_QK), F32)
        gw = jnp.concatenate([jnp.concatenate([gla_gw_f[l], zlow], axis=1),
                              jnp.concatenate([zlow, gla_gw_b[l]], axis=1),
                              jnp.zeros((LANE - 2 * GLA_LOWRANK, 2 * GLA_QK), F32)], axis=0)
        gb = jnp.concatenate([gla_gb_f[l], gla_gb_b[l]]).reshape(1, -1)
        gw = _gate_fold(wgate, gw).astype(BF16)
        wbd = jnp.tile(pool_w[l].reshape(D_GROUP, POOL_CH), (1, len(POOL_WINDOWS))) * head_eye
        p = dict(pool_wbd=wbd.astype(BF16), pool_scale=pool_scale[l].reshape(1, -1),
                 hy_short_w=hy_short_w[l], hy_short_b=hy_short_b[l], hy_bias=hy_bias[l],
                 gla_ng=jnp.tile(gla_norm[l], GLA_HEADS).reshape(1, -1),
                 conv_dw_w=conv_dw_w[l], conv_dw_b=conv_dw_b[l], conv_ln_g=conv_ln_g[l], conv_ln_b=conv_ln_b[l])
        names = ["kq", "v", "r", "pool", "hy", "conv"]

        xs, *outs = _ffn_proj(xs, mod, ffn1_norm[l], wi1, wo1, mix_norm[l], [wparts[nm] for nm in names], gw, gb,
                              layer=l, row0=0, tm=2 * tmx, tiles_per_seq=tps_x // 2)
        ux = dict(zip(names + ["lf", "lb"], outs))
        cnames = ["kq", "v"] if last else names
        cs, *outs = _ffn_proj(cs, mod, ffn1_norm[l], wi1, wo1, mix_norm[l], [wparts[nm] for nm in cnames], gw, gb,
                              layer=l, row0=crow, tm=tmc, tiles_per_seq=1)
        uc = dict(zip(cnames + ["lf", "lb"], outs))

        ccpb = clen // CHUNK
        sfc, sbc, finf, finb = _gla_states(uc["kq"], uc["v"], uc["lf"], uc["lb"], zero_state, zero_state,
                                           bsz=bsz, seq=clen, cpb=ccpb)
        sfx, sbx, _, _ = _gla_states(ux["kq"], ux["v"], ux["lf"], ux["lb"], finf, finb, bsz=bsz, seq=seq,
                                     cpb=min(32, seq // CHUNK))

        filt = _hy_filter(seq, hy_w1[l], hy_b1[l], hy_w2[l], hy_b2[l], hy_w3[l], hy_deltas[l], t2_major=True)
        mixed = _mix(ux, p, bsz=bsz, seq=seq, is_ctx=False, states=(sfx, sbx), filt=filt)
        xs = _out_ffn(xs, mod, mixed, wout, ffn2_norm[l], wi2, wo2, layer=l, row0=0, tm=2 * tmx,
                      tiles_per_seq=tps_x // 2, final_g=final_norm if last else None)
        if not last:
            filt_c = _hy_filter(clen, hy_w1[l], hy_b1[l], hy_w2[l], hy_b2[l], hy_w3[l], hy_deltas[l],
                                t2_major=False)
            mixed = _mix(uc, p, bsz=bsz, seq=clen, is_ctx=True, states=(sfc, sbc), filt=filt_c)
            cs = _out_ffn(cs, mod, mixed, wout, ffn2_norm[l], wi2, wo2, layer=l, row0=crow, tm=tmc,
                          tiles_per_seq=1)
    return xs.reshape(bsz, seq, d)
```

```python
import functools
import math

import numpy as np
import jax
import jax.numpy as jnp
from jax import lax
from jax.experimental import pallas as pl
from jax.experimental.pallas import tpu as pltpu

F32 = jnp.float32
BF16 = jnp.bfloat16
HI = lax.Precision.HIGHEST

GRID_W = 64
D_GROUP = 256
N_MOD = 9
EPS = 1e-6
POOL_WINDOWS = (2, 4, 8, 16)
POOL_CH = 64
HY_BANDS = 16
HY_EMB = 1 + 2 * HY_BANDS
HY_FFN = 64
HY_FILTER_CH = 4 * D_GROUP
GLA_HEADS = 4
GLA_DK = 32
GLA_DV = 64
GLA_QK = 128
GLA_LOWRANK = 16
GLA_TAU = 16.0
CHUNK = 64
CONV_WIDTH = 31
HY_SHORT = 3

COL_K = 0
COL_V = COL_K + GLA_QK
COL_GF = COL_V + D_GROUP
COL_GB = COL_GF + GLA_LOWRANK
COL_Q = COL_GB + GLA_LOWRANK
COL_R = COL_Q + GLA_QK
COL_POOL = COL_R + D_GROUP
COL_HY = COL_POOL + D_GROUP
COL_CONV = COL_HY + 3 * D_GROUP
P_IN = COL_CONV + 2 * D_GROUP

LANE = 128
SUBLANES = 8
MIB = 1024 * 1024

DFT2 = 128
FFT_GROUP = 16
FFT_GROUP_DATA = 32
MOD_ROWS = 3
FF_CHUNK = 256
ROW_SUB = 256
TOKEN_TILE = 512


def _cparams(sem, vmem_mib=None):
    kw = dict(dimension_semantics=sem)
    if vmem_mib is not None:
        kw["vmem_limit_bytes"] = vmem_mib * MIB
    return pltpu.CompilerParams(**kw)


def _silu(x):
    return x * jax.nn.sigmoid(x)


def _rms_mod(h, g, m):
    y = h * lax.rsqrt(jnp.mean(h * h, axis=-1, keepdims=True) + EPS) * g
    return y * (1.0 + m[1:2, :]) + m[0:1, :]


def _iota(shape, dim):
    return lax.broadcasted_iota(jnp.int32, shape, dim)


def _mod_body(at_ref, w_ref, b_ref, o_ref):
    tk, nm = w_ref.shape[1], w_ref.shape[2]

    @pl.when(pl.program_id(1) == 0)
    def _():
        o_ref[0] = jnp.zeros((SUBLANES, nm), F32) + b_ref[0]

    at = _silu(at_ref[...])
    cols = [jnp.broadcast_to(at[:, r:r + 1], (tk, LANE)).reshape(tk // SUBLANES, SUBLANES, LANE)
            for r in range(MOD_ROWS)]
    for j in range(nm // LANE):
        lsl = slice(j * LANE, (j + 1) * LANE)
        w3 = w_ref[0, :, lsl].reshape(tk // SUBLANES, SUBLANES, LANE)
        for r in range(MOD_ROWS):
            part = jnp.sum(w3 * cols[r], axis=0)
            o_ref[0, r:r + 1, lsl] += jnp.sum(part, axis=0, keepdims=True)


def _modulation(cc, ada_w, ada_b):
    nl, d, nm = ada_w.shape
    tk = 256
    return pl.pallas_call(
        _mod_body,
        out_shape=jax.ShapeDtypeStruct((nl, 8, nm), F32),
        grid=(nl, d // tk),
        in_specs=[pl.BlockSpec((tk, 8), lambda l, j: (j, 0)),
                  pl.BlockSpec((1, tk, nm), lambda l, j: (l, j, 0)),
                  pl.BlockSpec((1, 1, nm), lambda l, j: (l, 0, 0))],
        out_specs=pl.BlockSpec((1, 8, nm), lambda l, j: (l, 0, 0)),
        compiler_params=_cparams(("parallel", "arbitrary"), 40),
        name="adaln_mod",
    )(cc.T, ada_w, ada_b.reshape(nl, 1, nm))


def _swiglu(xn, wi_ref, wo_ref, hm_ref, rows):
    ff = wo_ref.shape[0]
    for c in range(0, ff, FF_CHUNK):
        a = jnp.dot(xn, wi_ref[:, c:c + FF_CHUNK], preferred_element_type=F32)
        g = jnp.dot(xn, wi_ref[:, ff + c:ff + c + FF_CHUNK], preferred_element_type=F32)
        hm_ref[rows, c:c + FF_CHUNK] = (_silu(g) * a).astype(BF16)
    return jnp.dot(hm_ref[rows, :], wo_ref[...], preferred_element_type=F32)


def _resident(shape):
    nd = len(shape)
    return pl.BlockSpec(shape, lambda i: (0,) * nd, pipeline_mode=pl.Buffered(1))


def _resident_layer(shape, layer):
    nd = len(shape)
    return pl.BlockSpec((None,) + tuple(shape), lambda i: (layer,) + (0,) * nd, pipeline_mode=pl.Buffered(1))


def _gate_fold_body(wg_ref, gw_ref, o_ref):
    o_ref[...] = jnp.dot(wg_ref[...], gw_ref[...], precision=HI, preferred_element_type=F32)


def _gate_fold(w_gate, gw):
    d = w_gate.shape[0]
    return pl.pallas_call(
        _gate_fold_body,
        out_shape=jax.ShapeDtypeStruct((d, gw.shape[1]), F32),
        grid=(1,),
        in_specs=[pl.BlockSpec(w_gate.shape, lambda i: (0, 0)), pl.BlockSpec(gw.shape, lambda i: (0, 0))],
        out_specs=pl.BlockSpec((d, gw.shape[1]), lambda i: (0, 0)),
        name="gate_fold",
    )(w_gate, gw)


def _chunk_prefix(x):
    pos = _iota(x.shape, 0) & (CHUNK - 1)
    shift = 1
    while shift < CHUNK:
        x = x + jnp.where(pos >= shift, pltpu.roll(x, shift, 0), 0.0)
        shift *= 2
    return x


def _ffn_proj_body(h_ref, m_ref, g1_ref, wi_ref, wo_ref, g2_ref, *rest, nparts, row_sub):
    w_refs = rest[:nparts]
    gw_ref, gb_ref = rest[nparts:nparts + 2]
    x_ref = rest[nparts + 2]
    o_refs = rest[nparts + 3:2 * nparts + 5]
    hm_ref = rest[-1]
    m = m_ref[0]
    for r0 in range(0, h_ref.shape[0], row_sub):
        rows = slice(r0, r0 + row_sub)
        h = h_ref[rows, :]
        xn = _rms_mod(h, g1_ref[...], m[0:2]).astype(BF16)
        x1 = h + (0.5 * m[2:3, :]) * _swiglu(xn, wi_ref, wo_ref, hm_ref, rows)
        x_ref[rows, :] = x1
        xn2 = _rms_mod(x1, g2_ref[...], m[3:5]).astype(BF16)
        for w_ref, o_ref in zip(w_refs, o_refs[:nparts]):
            o_ref[rows, :] = jnp.dot(xn2, w_ref[...], preferred_element_type=F32).astype(o_ref.dtype)
        a = jnp.dot(xn2, gw_ref[...], preferred_element_type=F32) + gb_ref[...]
        ls = (jnp.minimum(a, 0.0) - jnp.log(1.0 + jnp.exp(-jnp.abs(a)))) * (1.0 / GLA_TAU)
        pre = _chunk_prefix(ls)
        o_refs[nparts][rows, :] = pre[:, :GLA_QK]
        pb = pre[:, GLA_QK:]
        tot = jnp.concatenate([jnp.broadcast_to(pb[c0 + CHUNK - 1:c0 + CHUNK, :], (CHUNK, GLA_QK))
                               for c0 in range(0, row_sub, CHUNK)], axis=0)
        o_refs[nparts + 1][rows, :] = tot - pb + ls[:, GLA_QK:]


def _mod_spec(d, layer, row0, tiles_per_seq):
    return pl.BlockSpec((None, 1, N_MOD, d), lambda i: (layer, row0 + i // tiles_per_seq, 0, 0))


def _ffn_proj(h, mod, g1, wi, wo, g2, w_parts, gw, gb, *, layer, row0, tm, tiles_per_seq):
    n, d = h.shape
    ff = wo.shape[1]
    nparts = len(w_parts)
    widths = [w.shape[1] for w in w_parts]
    tok = lambda wd: pl.BlockSpec((tm, wd), lambda i: (i, 0))
    in_specs = [tok(d), _mod_spec(d, layer, row0, tiles_per_seq), _resident((1, d)),
                _resident_layer((d, 2 * ff), layer), _resident_layer((ff, d), layer), _resident((1, d))]
    in_specs += [_resident((d, wd)) for wd in widths]
    in_specs += [_resident((d, 2 * GLA_QK)), _resident((1, 2 * GLA_QK))]
    out_shape = [jax.ShapeDtypeStruct((n, d), F32)]
    out_shape += [jax.ShapeDtypeStruct((n, wd), BF16) for wd in widths]
    out_shape += [jax.ShapeDtypeStruct((n, GLA_QK), F32)] * 2
    return pl.pallas_call(
        functools.partial(_ffn_proj_body, nparts=nparts, row_sub=min(tm, 2 * ROW_SUB)),
        out_shape=out_shape,
        grid=(n // tm,),
        in_specs=in_specs,
        out_specs=[tok(d)] + [tok(wd) for wd in widths] + [tok(GLA_QK)] * 2,
        scratch_shapes=[pltpu.VMEM((tm, ff), BF16)],
        compiler_params=_cparams(("parallel",), 60),
        name="ffn1_in_proj",
    )(h, mod, g1.reshape(1, d), wi, wo, g2.reshape(1, d), *w_parts, gw, gb)


def _out_ffn_body(x_ref, m_ref, p_ref, hy_ref, gl_ref, cv_ref, wout_ref, g_ref, wi_ref, wo_ref, fn_ref, o_ref,
                  hm_ref, *, final, row_sub):
    m = m_ref[0]
    for r0 in range(0, x_ref.shape[0], row_sub):
        rows = slice(r0, r0 + row_sub)
        acc = jnp.dot(p_ref[rows, :].astype(BF16), wout_ref[0:D_GROUP, :], preferred_element_type=F32)
        acc += jnp.dot(hy_ref[rows, :].astype(BF16), wout_ref[D_GROUP:2 * D_GROUP, :], preferred_element_type=F32)
        acc += jnp.dot(gl_ref[rows, :].astype(BF16), wout_ref[2 * D_GROUP:3 * D_GROUP, :],
                       preferred_element_type=F32)
        acc += jnp.dot(cv_ref[rows, :].astype(BF16), wout_ref[3 * D_GROUP:, :], preferred_element_type=F32)
        x2 = x_ref[rows, :] + m[5:6, :] * acc
        xn = _rms_mod(x2, g_ref[...], m[6:8]).astype(BF16)
        out = x2 + (0.5 * m[8:9, :]) * _swiglu(xn, wi_ref, wo_ref, hm_ref, rows)
        if final:
            out = out * lax.rsqrt(jnp.mean(out * out, axis=-1, keepdims=True) + EPS) * fn_ref[...]
        o_ref[rows, :] = out


def _out_ffn(x, mod, parts, w_out, g, wi, wo, *, layer, row0, tm, tiles_per_seq, final_g=None):
    n, d = x.shape
    ff = wo.shape[1]
    final = final_g is not None
    fg = final_g if final else g
    tok = lambda wd: pl.BlockSpec((tm, wd), lambda i: (i, 0))
    return pl.pallas_call(
        functools.partial(_out_ffn_body, final=final, row_sub=min(tm, 2 * ROW_SUB)),
        out_shape=jax.ShapeDtypeStruct((n, d), F32),
        grid=(n // tm,),
        in_specs=[tok(d), _mod_spec(d, layer, row0, tiles_per_seq),
                  tok(D_GROUP), tok(D_GROUP), tok(D_GROUP), tok(D_GROUP), _resident_layer((d, d), layer),
                  _resident((1, d)), _resident_layer((d, 2 * ff), layer), _resident_layer((ff, d), layer),
                  _resident((1, d))],
        out_specs=tok(d),
        scratch_shapes=[pltpu.VMEM((tm, ff), BF16)],
        compiler_params=_cparams(("parallel",), 56),
        name="out_proj_ffn2",
    )(x, mod, *parts, w_out, g.reshape(1, d), wi, wo, fg.reshape(1, d))


def _gla_state_body(kf, vf, bf, kb, vb, bb, s0f, s0b, sf_o, sb_o, ff_o, fb_o, stf, stb, *, cpb):
    i = pl.program_id(1)

    @pl.when(i == 0)
    def _():
        stf[...] = s0f[0]
        stb[...] = s0b[0]

    sshape = (GLA_HEADS * GLA_DV, GLA_QK)
    bmask = (_iota(sshape, 0) >> 6) == (_iota(sshape, 1) >> 5)
    tn_dims = (((0,), (0,)), ((), ()))

    def direction(k_ref, v_ref, b_ref, last, st, s_o, order):
        s = st[...]
        for ci in order:
            sl = slice(ci * CHUNK, (ci + 1) * CHUNK)
            b = b_ref[sl, :]
            tot = b[last:last + 1, :]
            kd = (k_ref[sl, :] * jnp.exp(tot - b)).astype(BF16)
            upd = lax.dot_general(v_ref[sl, :].astype(BF16), kd, tn_dims, preferred_element_type=F32)
            s_o[0, ci] = s.astype(BF16)
            s = s * jnp.exp(tot) + jnp.where(bmask, upd, 0.0)
        st[...] = s

    direction(kf, vf, bf, CHUNK - 1, stf, sf_o, range(cpb))
    direction(kb, vb, bb, 0, stb, sb_o, range(cpb - 1, -1, -1))
    ff_o[0] = stf[...]
    fb_o[0] = stb[...]


def _gla_states(k, v, lf, lb, s0f, s0b, *, bsz, seq, cpb):
    bt = cpb * CHUNK
    nb = seq // bt
    nc = seq // CHUNK
    srow = GLA_HEADS * GLA_DV

    def tf(b, i):
        return (b * nb + i, 0)

    def tb(b, i):
        return (b * nb + nb - 1 - i, 0)

    sblk = pl.BlockSpec((1, srow, GLA_QK), lambda b, i: (b, 0, 0))
    return pl.pallas_call(
        functools.partial(_gla_state_body, cpb=cpb),
        out_shape=[jax.ShapeDtypeStruct((bsz, nc, srow, GLA_QK), BF16),
                   jax.ShapeDtypeStruct((bsz, nc, srow, GLA_QK), BF16),
                   jax.ShapeDtypeStruct((bsz, srow, GLA_QK), F32),
                   jax.ShapeDtypeStruct((bsz, srow, GLA_QK), F32)],
        grid=(bsz, nb),
        in_specs=[pl.BlockSpec((bt, GLA_QK), tf), pl.BlockSpec((bt, D_GROUP), tf), pl.BlockSpec((bt, GLA_QK), tf),
                  pl.BlockSpec((bt, GLA_QK), tb), pl.BlockSpec((bt, D_GROUP), tb), pl.BlockSpec((bt, GLA_QK), tb),
                  sblk, sblk],
        out_specs=[pl.BlockSpec((1, cpb, srow, GLA_QK), lambda b, i: (b, i, 0, 0)),
                   pl.BlockSpec((1, cpb, srow, GLA_QK), lambda b, i: (b, nb - 1 - i, 0, 0)),
                   sblk, sblk],
        scratch_shapes=[pltpu.VMEM((srow, GLA_QK), F32), pltpu.VMEM((srow, GLA_QK), F32)],
        compiler_params=_cparams(("parallel", "arbitrary")),
        name="gla_states",
    )(k, v, lf, k, v, lb, s0f, s0b)


def _gla_read_body(k_ref, v_ref, q_ref, r_ref, bf_ref, bb_ref, sf_ref, sb_ref, ng_ref, o_ref, acc_ref, *, cpb):
    hrows = GLA_HEADS * CHUNK
    cpos = _iota((hrows, CHUNK), 0) & (CHUNK - 1)
    ccol = _iota((hrows, CHUNK), 1)
    lowm = cpos >= ccol
    upm = cpos <= ccol
    hq = (_iota((hrows, GLA_QK), 0) >> 6) == (_iota((hrows, GLA_QK), 1) >> 5)
    ho = (_iota((hrows, D_GROUP), 0) >> 6) == (_iota((hrows, D_GROUP), 1) >> 6)
    bavg = jnp.where((_iota((D_GROUP, D_GROUP), 0) >> 6) == (_iota((D_GROUP, D_GROUP), 1) >> 6),
                     1.0 / GLA_DV, 0.0).astype(BF16)
    nt_dims = (((1,), (1,)), ((), ()))

    bf_ = bf_ref[...]
    bb_ = bb_ref[...]
    qs = q_ref[...].astype(F32) * (GLA_DK ** -0.5)
    kk = k_ref[...]
    qef = (qs * jnp.exp(bf_)).astype(BF16)
    qeb = (qs * jnp.exp(bb_)).astype(BF16)
    kef = (kk * jnp.exp(-bf_)).astype(BF16)
    keb = (kk * jnp.exp(-bb_)).astype(BF16)
    vb16 = v_ref[...].astype(BF16)
    zero = jnp.zeros((), BF16)
    for ci in range(cpb):
        sl = slice(ci * CHUNK, (ci + 1) * CHUNK)
        qf4 = jnp.where(hq, jnp.concatenate([qef[sl]] * GLA_HEADS, axis=0), zero)
        qb4 = jnp.where(hq, jnp.concatenate([qeb[sl]] * GLA_HEADS, axis=0), zero)
        af = lax.dot_general(qf4, kef[sl], nt_dims, preferred_element_type=F32)
        ab = lax.dot_general(qb4, keb[sl], nt_dims, preferred_element_type=F32)
        att = (jnp.where(lowm, af, 0.0) + jnp.where(upm, ab, 0.0)).astype(BF16)
        oall = jnp.dot(att, vb16[sl], preferred_element_type=F32)
        om = jnp.where(ho, oall, 0.0)
        o = om[0:CHUNK] + om[CHUNK:2 * CHUNK] + om[2 * CHUNK:3 * CHUNK] + om[3 * CHUNK:4 * CHUNK]
        qcat = jnp.concatenate([qef[sl], qeb[sl]], axis=1)
        scat = jnp.concatenate([sf_ref[0, ci], sb_ref[0, ci]], axis=1)
        acc_ref[sl, :] = o + lax.dot_general(qcat, scat, nt_dims, preferred_element_type=F32)
    o = acc_ref[...]
    ms = _chunk_sum_rhs(o * o, bavg)
    o_ref[...] = (o * lax.rsqrt(ms + EPS) * ng_ref[...] * _silu(r_ref[...].astype(F32))).astype(o_ref.dtype)


def _chunk_sum_rhs(x, mat):
    hi, lo = _split_bf16(x)
    return jnp.dot(hi, mat, preferred_element_type=F32) + jnp.dot(lo, mat, preferred_element_type=F32)


def _gla_read(k, v, q, r, lf, lb, sf, sb, ng, *, bsz, seq, cpb):
    bt = cpb * CHUNK
    nb = seq // bt
    srow = GLA_HEADS * GLA_DV
    n = bsz * seq

    def tk(i):
        return (i, 0)

    sspec = pl.BlockSpec((1, cpb, srow, GLA_QK), lambda i: (i // nb, i % nb, 0, 0))
    return pl.pallas_call(
        functools.partial(_gla_read_body, cpb=cpb),
        out_shape=jax.ShapeDtypeStruct((n, D_GROUP), BF16),
        scratch_shapes=[pltpu.VMEM((bt, D_GROUP), F32)],
        grid=(bsz * nb,),
        in_specs=[pl.BlockSpec((bt, GLA_QK), tk), pl.BlockSpec((bt, D_GROUP), tk),
                  pl.BlockSpec((bt, GLA_QK), lambda i: (i, 1)), pl.BlockSpec((bt, D_GROUP), tk),
                  pl.BlockSpec((bt, GLA_QK), tk), pl.BlockSpec((bt, GLA_QK), tk),
                  sspec, sspec, pl.BlockSpec((1, D_GROUP), lambda i: (0, 0))],
        out_specs=pl.BlockSpec((bt, D_GROUP), tk),
        compiler_params=_cparams(("parallel",)),
        name="gla_readout",
    )(k, v, q, r, lf, lb, sf, sb, ng)


def _box_matrix(n, w):
    pos = np.arange(n)
    lo = np.clip(pos - w // 2, 0, n)
    hi = np.clip(pos - w // 2 + w, 0, n)
    col = np.arange(n)[None, :]
    return ((col >= lo[:, None]) & (col < hi[:, None])).astype(np.float32)


def _lane_windows(shape):
    w = jnp.left_shift(2, _iota(shape, 1) >> 6)
    return w, w >> 1


def _box_count(pos, w, half, n):
    return jnp.minimum(pos - half + w, n) - jnp.maximum(pos - half, 0)


def _split_bf16(x):
    hi = x.astype(BF16)
    lo = (x - hi.astype(F32)).astype(BF16)
    return hi, lo


def _pool2d_body(cur_ref, prev_ref, next_ref, pc_ref, w_ref, sc_ref, o_ref, ycol, *, tiles, rows):
    i = pl.program_id(1)
    tm = cur_ref.shape[0]
    hb = prev_ref.shape[0]
    sub = 2 * GRID_W
    pflag = jnp.where(i > 0, 1.0, 0.0)
    nflag = jnp.where(i < tiles - 1, 1.0, 0.0)

    def colpool(x):
        halves = []
        for half in range(2):
            lsl = slice(half * LANE, (half + 1) * LANE)
            ys = [jnp.dot(pc_ref[wi], x[:, lsl], preferred_element_type=F32) for wi in (2 * half, 2 * half + 1)]
            lane = _iota((sub, LANE), 1)
            halves.append(jnp.where(lane < POOL_CH, ys[0], ys[1]))
        return jnp.concatenate(halves, axis=1)

    for s in range(hb // sub):
        ycol[s * sub:(s + 1) * sub, :] = colpool(prev_ref[s * sub:(s + 1) * sub, :]) * pflag
    for s in range(tm // sub):
        ycol[hb + s * sub:hb + (s + 1) * sub, :] = colpool(cur_ref[s * sub:(s + 1) * sub, :])
    for s in range(hb // sub):
        ycol[hb + tm + s * sub:hb + tm + (s + 1) * sub, :] = colpool(next_ref[s * sub:(s + 1) * sub, :]) * nflag

    rc = 256
    wl, half = _lane_windows((rc, D_GROUP))
    narrow = _iota((rc, LANE), 1) < POOL_CH
    for r0 in range(0, tm, rc):
        def band(lo, hi, lsl):
            base = hb + r0
            acc = ycol[base + GRID_W * lo:base + GRID_W * lo + rc, lsl]
            for dd in range(lo + 1, hi):
                acc = acc + ycol[base + GRID_W * dd:base + GRID_W * dd + rc, lsl]
            return acc

        left, right = slice(0, LANE), slice(LANE, 2 * LANE)
        z2 = band(-1, 1, left)
        z4 = z2 + band(-2, -1, left) + band(1, 2, left)
        z8 = band(-4, 4, right)
        z16 = z8 + band(-8, -4, right) + band(4, 8, right)
        z = jnp.concatenate([jnp.where(narrow, z2, z4), jnp.where(narrow, z8, z16)], axis=1)
        tok = _iota((rc, D_GROUP), 0) + (i * tm + r0)
        rcnt = _box_count(tok >> 6, wl, half, rows)
        ccnt = _box_count(tok & (GRID_W - 1), wl, half, GRID_W)
        pooled = z / (rcnt * ccnt).astype(F32)
        dlt = (pooled - cur_ref[r0:r0 + rc, :]).astype(BF16)
        o_ref[r0:r0 + rc, :] = (jnp.dot(dlt, w_ref[...], preferred_element_type=F32) * sc_ref[...]).astype(o_ref.dtype)


def _pool2d(u, wbd, scale, *, bsz, seq):
    tm = min(seq, 4096)
    hb = 512
    tiles = seq // tm
    r = tm // hb
    nhb = seq // hb
    pc = np.stack([np.kron(np.eye(2, dtype=np.float32), _box_matrix(GRID_W, w)) for w in POOL_WINDOWS])
    return pl.pallas_call(
        functools.partial(_pool2d_body, tiles=tiles, rows=seq // GRID_W),
        out_shape=jax.ShapeDtypeStruct((bsz * seq, D_GROUP), BF16),
        grid=(bsz, tiles),
        in_specs=[pl.BlockSpec((tm, D_GROUP), lambda b, i: (b * tiles + i, 0)),
                  pl.BlockSpec((hb, D_GROUP), lambda b, i: (b * nhb + jnp.maximum(i * r - 1, 0), 0)),
                  pl.BlockSpec((hb, D_GROUP), lambda b, i: (b * nhb + jnp.minimum(i * r + r, nhb - 1), 0)),
                  pl.BlockSpec((4, 2 * GRID_W, 2 * GRID_W), lambda b, i: (0, 0, 0)),
                  pl.BlockSpec((D_GROUP, D_GROUP), lambda b, i: (0, 0)),
                  pl.BlockSpec((1, D_GROUP), lambda b, i: (0, 0))],
        out_specs=pl.BlockSpec((tm, D_GROUP), lambda b, i: (b * tiles + i, 0)),
        scratch_shapes=[pltpu.VMEM((tm + 2 * hb, D_GROUP), F32)],
        compiler_params=_cparams(("parallel", "parallel")),
        name="pool2d",
    )(u, u, u, jnp.asarray(pc, BF16), wbd, scale)


def _pool1d_body(x_ref, p_ref, w_ref, sc_ref, o_ref):
    x = x_ref[...]
    n = x.shape[0]
    ys = [jnp.dot(p_ref[wi], x, preferred_element_type=F32) for wi in range(4)]
    wl, half = _lane_windows((n, D_GROUP))
    z = jnp.where(wl == 2, ys[0], jnp.where(wl == 4, ys[1], jnp.where(wl == 8, ys[2], ys[3])))
    cnt = _box_count(_iota((n, D_GROUP), 0), wl, half, n)
    dlt = (z / cnt.astype(F32) - x).astype(BF16)
    o_ref[...] = (jnp.dot(dlt, w_ref[...], preferred_element_type=F32) * sc_ref[...]).astype(o_ref.dtype)


def _pool1d(u, wbd, scale, *, bsz, seq):
    pm = np.stack([_box_matrix(seq, w) for w in POOL_WINDOWS])
    return pl.pallas_call(
        _pool1d_body,
        out_shape=jax.ShapeDtypeStruct((bsz * seq, D_GROUP), BF16),
        grid=(bsz,),
        in_specs=[pl.BlockSpec((seq, D_GROUP), lambda b: (b, 0)),
                  pl.BlockSpec((4, seq, seq), lambda b: (0, 0, 0)),
                  pl.BlockSpec((D_GROUP, D_GROUP), lambda b: (0, 0)),
                  pl.BlockSpec((1, D_GROUP), lambda b: (0, 0))],
        out_specs=pl.BlockSpec((seq, D_GROUP), lambda b: (b, 0)),
        compiler_params=_cparams(("parallel",)),
        name="pool1d",
    )(u, jnp.asarray(pm, BF16), wbd, scale)


def _fill_halo(buf, cur, prev, nxt, i, tps, hb, tm, pre):
    first = (i % tps) == 0
    last = (i % tps) == tps - 1
    buf[0:hb, :] = jnp.where(first, 0.0, pre(prev[...]))
    buf[hb:hb + tm, :] = pre(cur[...])
    buf[hb + tm:hb + tm + hb, :] = jnp.where(last, 0.0, pre(nxt[...]))


def _tap_phases(taps, off):
    return sorted({(off + j) % SUBLANES for j in range(taps)} - {0})


def _fill_phases(sh, buf, phases):
    rows = sh.shape[1]
    for slot, s in enumerate(phases):
        sh[slot, :, :] = buf[s:s + rows, :]


def _dwconv(buf, sh, phases, w_ref, r0, rc, taps, off):
    acc = None
    for j in range(taps):
        s, q = (off + j) % SUBLANES, (off + j) // SUBLANES
        lo = r0 + SUBLANES * q
        src = buf[lo:lo + rc, :] if s == 0 else sh[phases.index(s), lo:lo + rc, :]
        term = src * w_ref[j:j + 1, :]
        acc = term if acc is None else acc + term
    return acc


def _conf_body(cur, prev, nxt, w_ref, b_ref, lg_ref, lb_ref, o_ref, buf, sh, *, tps):
    i = pl.program_id(0)
    tm = cur.shape[0]
    hb = prev.shape[0]

    def glu(u):
        u = u.astype(F32)
        return u[:, :D_GROUP] * jax.nn.sigmoid(u[:, D_GROUP:])

    _fill_halo(buf, cur, prev, nxt, i, tps, hb, tm, glu)
    rc = 128
    off = hb - (CONV_WIDTH - 1) // 2
    phases = _tap_phases(CONV_WIDTH, off)
    _fill_phases(sh, buf, phases)
    for r0 in range(0, tm, rc):
        h = _dwconv(buf, sh, phases, w_ref, r0, rc, CONV_WIDTH, off) + b_ref[...]
        mu = jnp.mean(h, axis=-1, keepdims=True)
        hc = h - mu
        var = jnp.mean(hc * hc, axis=-1, keepdims=True)
        o_ref[r0:r0 + rc, :] = _silu(hc * lax.rsqrt(var + EPS) * lg_ref[...] + lb_ref[...]).astype(o_ref.dtype)


def _halo_specs(tm, hb, width, nrows):
    r = tm // hb
    nhb = nrows // hb
    return [pl.BlockSpec((tm, width), lambda i: (i, 0)),
            pl.BlockSpec((hb, width), lambda i: (jnp.maximum(i * r - 1, 0), 0)),
            pl.BlockSpec((hb, width), lambda i: (jnp.minimum(i * r + r, nhb - 1), 0))]


def _conformer(u, w, b, lg, lb, *, seq, tm):
    n = u.shape[0]
    hb = 16
    nph = len(_tap_phases(CONV_WIDTH, hb - (CONV_WIDTH - 1) // 2))
    vec = pl.BlockSpec((1, D_GROUP), lambda i: (0, 0))
    return pl.pallas_call(
        functools.partial(_conf_body, tps=seq // tm),
        out_shape=jax.ShapeDtypeStruct((n, D_GROUP), BF16),
        grid=(n // tm,),
        in_specs=_halo_specs(tm, hb, 2 * D_GROUP, n) + [pl.BlockSpec((CONV_WIDTH, D_GROUP), lambda i: (0, 0)),
                                                        vec, vec, vec],
        out_specs=pl.BlockSpec((tm, D_GROUP), lambda i: (i, 0)),
        scratch_shapes=[pltpu.VMEM((tm + 2 * hb, D_GROUP), F32),
                        pltpu.VMEM((nph, tm + 2 * hb - SUBLANES, D_GROUP), F32)],
        compiler_params=_cparams(("parallel",), 48),
        name="conformer_conv",
    )(u, u, u, w, b.reshape(1, -1), lg.reshape(1, -1), lb.reshape(1, -1))


def _short_body(cur, prev, nxt, w_ref, b_ref, v_ref, x1_ref, x2_ref, buf, sh, *, tps):
    i = pl.program_id(0)
    tm = cur.shape[0]
    hb = prev.shape[0]
    _fill_halo(buf, cur, prev, nxt, i, tps, hb, tm, lambda u: u.astype(F32))
    rc = 128
    off = hb - (HY_SHORT - 1) // 2
    phases = _tap_phases(HY_SHORT, off)
    _fill_phases(sh, buf, phases)
    for r0 in range(0, tm, rc):
        uc = _dwconv(buf, sh, phases, w_ref, r0, rc, HY_SHORT, off) + b_ref[...]
        v_ref[r0:r0 + rc, :] = uc[:, :D_GROUP]
        x1_ref[r0:r0 + rc, :] = uc[:, D_GROUP:2 * D_GROUP].astype(x1_ref.dtype)
        x2_ref[r0:r0 + rc, :] = uc[:, 2 * D_GROUP:].astype(x2_ref.dtype)


def _hy_short(u, w, b, *, seq, tm):
    n = u.shape[0]
    hb = 16
    wd = 3 * D_GROUP
    nph = len(_tap_phases(HY_SHORT, hb - (HY_SHORT - 1) // 2))
    ospec = pl.BlockSpec((tm, D_GROUP), lambda i: (i, 0))
    return pl.pallas_call(
        functools.partial(_short_body, tps=seq // tm),
        out_shape=[jax.ShapeDtypeStruct((n, D_GROUP), dt) for dt in (F32, BF16, BF16)],
        grid=(n // tm,),
        in_specs=_halo_specs(tm, hb, wd, n) + [pl.BlockSpec((HY_SHORT, wd), lambda i: (0, 0)),
                                               pl.BlockSpec((1, wd), lambda i: (0, 0))],
        out_specs=[ospec, ospec, ospec],
        scratch_shapes=[pltpu.VMEM((tm + 2 * hb, wd), F32),
                        pltpu.VMEM((nph, tm + 2 * hb - SUBLANES, wd), F32)],
        compiler_params=_cparams(("parallel",), 48),
        name="hyena_short_conv",
    )(u, u, u, w, b.reshape(1, -1))


def _filter_features(n):
    i = np.arange(n, dtype=np.float64)
    t = np.linspace(0.0, 1.0, n, dtype=np.float32).astype(np.float64)
    wpos = ((2.0 * math.pi / n) * np.arange(n, dtype=np.float32)).astype(np.float32)
    bands = np.linspace(1e-4, HY_BANDS - 1, HY_BANDS, dtype=np.float32)
    arg = (bands[None, :] * wpos[:, None]).astype(np.float32).astype(np.float64)
    z = np.zeros((n, 64), np.float32)
    z[:, 0] = t
    z[:, 1:1 + HY_BANDS] = np.cos(arg)
    z[:, 1 + HY_BANDS:HY_EMB] = -np.sin(arg)
    del i
    return z


def _filter_body(z_ref, w1_ref, b1_ref, w2_ref, b2_ref, w3a_ref, w3b_ref, d_ref, hf_ref, hb_ref, s_ref):
    i = pl.program_id(0)
    half = z_ref.shape[0]
    z = z_ref[...]
    h = jnp.sin(jnp.dot(z, w1_ref[...], precision=HI, preferred_element_type=F32) + b1_ref[...])
    h = jnp.sin(jnp.dot(h, w2_ref[...], precision=HI, preferred_element_type=F32) + b2_ref[...])
    absd = jnp.abs(d_ref[...])
    h = h.astype(BF16)
    tot = None
    for part, (w3_ref, tcol) in enumerate(((w3a_ref, 0), (w3b_ref, 64))):
        hp = jnp.dot(h, w3_ref[...], preferred_element_type=F32)
        hp = hp * jnp.exp(-z[:, tcol:tcol + 1] * absd)
        rows = slice(part * half, (part + 1) * half)
        for o in range(2):
            c0 = 2 * D_GROUP * o
            hf_ref[rows, o * D_GROUP:(o + 1) * D_GROUP] = hp[:, c0:c0 + D_GROUP].astype(hf_ref.dtype)
            hb_ref[rows, o * D_GROUP:(o + 1) * D_GROUP] = hp[:, c0 + D_GROUP:c0 + 2 * D_GROUP].astype(hb_ref.dtype)
        part_sum = jnp.sum(jnp.abs(hp), axis=0, keepdims=True)
        tot = part_sum if tot is None else tot + part_sum

    @pl.when(i == 0)
    def _():
        s_ref[...] = jnp.zeros_like(s_ref)

    s_ref[...] += tot


def _hy_filter(n, w1, b1, w2, b2, w3, deltas, *, t2_major):
    tm = min(n, 2048)
    half = tm // 2
    z = _filter_features(n)
    if t2_major:
        z = z.reshape(n // DFT2, DFT2, 64).transpose(1, 0, 2).reshape(n, 64)
    zt = z.reshape(n // tm, 2, half, 64)
    z2 = jnp.asarray(np.concatenate([zt[:, 0], zt[:, 1]], axis=-1).reshape(n // 2, 2 * 64))
    w1p = jnp.zeros((64, HY_FFN), F32).at[:HY_EMB].set(w1)
    zero = jnp.zeros((64, HY_FFN), F32)
    bd = lambda w: jnp.concatenate([jnp.concatenate([w, zero], axis=1), jnp.concatenate([zero, w], axis=1)], axis=0)
    w3 = w3.astype(BF16)
    zero3 = jnp.zeros_like(w3)
    full = lambda shape: pl.BlockSpec(shape, lambda i: (0, 0))
    tok = pl.BlockSpec((tm, 2 * D_GROUP), lambda i: (i, 0))
    return pl.pallas_call(
        _filter_body,
        out_shape=[jax.ShapeDtypeStruct((n, 2 * D_GROUP), BF16), jax.ShapeDtypeStruct((n, 2 * D_GROUP), BF16),
                   jax.ShapeDtypeStruct((1, HY_FILTER_CH), F32)],
        grid=(n // tm,),
        in_specs=[pl.BlockSpec((half, 2 * 64), lambda i: (i, 0)), full((2 * 64, 2 * HY_FFN)), full((1, 2 * HY_FFN)),
                  full((2 * HY_FFN, 2 * HY_FFN)), full((1, 2 * HY_FFN)), full((2 * HY_FFN, HY_FILTER_CH)),
                  full((2 * HY_FFN, HY_FILTER_CH)), full((1, HY_FILTER_CH))],
        out_specs=[tok, tok, full((1, HY_FILTER_CH))],
        compiler_params=_cparams(("arbitrary",), 48),
        name="hyena_filter_mlp",
    )(z2, bd(w1p), jnp.tile(b1.reshape(1, -1), (1, 2)), bd(w2), jnp.tile(b2.reshape(1, -1), (1, 2)),
      jnp.concatenate([w3, zero3], axis=0), jnp.concatenate([zero3, w3], axis=0), deltas.reshape(1, -1))


@functools.lru_cache(maxsize=None)
def _dft_tables(n1):
    nn = n1 * DFT2
    j = np.arange(DFT2)
    th = 2.0 * np.pi * ((np.outer(j, j)) % DFT2) / DFT2
    fr, fi = np.cos(th), -np.sin(th)
    m_fwd = np.block([[fr, -fi], [fi, fr]])
    m_inv = np.block([[fr, fi], [-fi, fr]])
    hh = n1 // 2
    k1 = np.arange(n1)[None, :, None]
    t1 = np.arange(hh)[None, None, :]
    t2 = np.arange(DFT2)[:, None, None]
    ph = 2.0 * np.pi * ((k1 * (DFT2 * t1 + t2)) % nn) / nn
    ar, ai = np.cos(ph), -np.sin(ph)
    a_fwd = np.concatenate([np.concatenate([ar, -ai], axis=2), np.concatenate([ai, ar], axis=2)], axis=1)
    a_real = np.concatenate([ar, ai], axis=1)
    t1r = np.where(t2 >= 1, n1 - 1 - t1, (n1 - t1) % n1)
    phr = 2.0 * np.pi * ((k1 * (DFT2 * t1r + t2)) % nn) / nn
    a_rev = np.concatenate([np.cos(phr), -np.sin(phr)], axis=1)
    a_rev[0, :, 0] = 0.0
    pht = np.transpose(ph, (0, 2, 1))
    cr, ci = np.cos(pht) / nn, np.sin(pht) / nn
    a_inv = np.concatenate([np.concatenate([cr, -ci], axis=2), np.concatenate([ci, cr], axis=2)], axis=1)
    f32 = lambda a: np.ascontiguousarray(a, dtype=np.float32)
    return dict(m_fwd=f32(m_fwd), m_inv=f32(m_inv), a_fwd=f32(a_fwd), a_real=f32(a_real), a_rev=f32(a_rev),
                a_inv=f32(a_inv))


def _fft_a_body(x_ref, m_ref, o_ref, *, tj):
    n1 = o_ref.shape[2]
    for e in range(tj):
        x = jnp.concatenate([x_ref[0, :, e, :], x_ref[1, :, e, :]], axis=0).astype(BF16)
        a = jnp.dot(m_ref[e], x, preferred_element_type=F32)
        o_ref[0, e] = a[:n1]
        o_ref[1, e] = a[n1:]


def _fft_a(x4, m, *, n1, tj):
    c = x4.shape[3]
    return pl.pallas_call(
        functools.partial(_fft_a_body, tj=tj),
        out_shape=jax.ShapeDtypeStruct((2, DFT2, n1, c), F32),
        grid=(DFT2 // tj,),
        in_specs=[pl.BlockSpec((2, n1 // 2, tj, c), lambda j: (0, 0, j, 0)),
                  pl.BlockSpec((tj, 2 * n1, n1), lambda j: (j, 0, 0))],
        out_specs=pl.BlockSpec((2, tj, n1, c), lambda j: (0, j, 0, 0)),
        compiler_params=_cparams(("parallel",), 48),
        name="fft_stage_a",
    )(x4, m)


def _fft_b_body(a_ref, mf_ref, mi_ref, g_ref, o_ref, *, kg):
    for kk in range(kg):
        x = jnp.concatenate([a_ref[0, :, kk, :], a_ref[1, :, kk, :]], axis=0).astype(BF16)
        xf = jnp.dot(mf_ref[...], x, preferred_element_type=F32)
        xr, xi = xf[:DFT2], xf[DFT2:]
        gr, gi = g_ref[0, kk].astype(F32), g_ref[1, kk].astype(F32)
        y = jnp.concatenate([xr * gr - xi * gi, xr * gi + xi * gr], axis=0).astype(BF16)
        bf = jnp.dot(mi_ref[...], y, preferred_element_type=F32)
        o_ref[0, kk] = bf[:DFT2]
        o_ref[1, kk] = bf[DFT2:]


def _fft_b(a, g, order, tabs, *, n1):
    c = a.shape[3]
    kg = min(n1, FFT_GROUP_DATA)
    blk = pl.BlockSpec((2, kg, DFT2, c), lambda k: (0, k, 0, 0))
    mat = pl.BlockSpec((2 * DFT2, 2 * DFT2), lambda k: (0, 0))
    return pl.pallas_call(
        functools.partial(_fft_b_body, kg=kg),
        out_shape=jax.ShapeDtypeStruct((2, n1, DFT2, c), F32),
        grid=(n1 // kg,),
        in_specs=[pl.BlockSpec((2, DFT2, kg, c), lambda k: (0, 0, k, 0)), mat, mat,
                  pl.BlockSpec((2, kg, DFT2, c), lambda k: (0, k, 0, order))],
        out_specs=blk,
        compiler_params=_cparams(("parallel",), 56),
        name="fft_stage_b",
    )(a, tabs["m_fwd"], tabs["m_inv"], g)


def _fft_bf_body(a_ref, mf_ref, o_ref, *, kg):
    for kk in range(kg):
        x = jnp.concatenate([a_ref[0, :, kk, :], a_ref[1, :, kk, :]], axis=0).astype(BF16)
        xf = jnp.dot(mf_ref[...], x, preferred_element_type=F32)
        o_ref[0, kk] = xf[:DFT2].astype(BF16)
        o_ref[1, kk] = xf[DFT2:].astype(BF16)


def _fft_b_forward(a, tabs, *, n1):
    c = a.shape[3]
    kg = FFT_GROUP
    return pl.pallas_call(
        functools.partial(_fft_bf_body, kg=kg),
        out_shape=jax.ShapeDtypeStruct((2, n1, DFT2, c), BF16),
        grid=(n1 // kg,),
        in_specs=[pl.BlockSpec((2, DFT2, kg, c), lambda k: (0, 0, k, 0)),
                  pl.BlockSpec((2 * DFT2, 2 * DFT2), lambda k: (0, 0))],
        out_specs=pl.BlockSpec((2, kg, DFT2, c), lambda k: (0, k, 0, 0)),
        compiler_params=_cparams(("parallel",), 48),
        name="fft_stage_b_filter",
    )(a, tabs["m_fwd"])


def _fft_ai_body(b_ref, m_ref, v_ref, x_ref, bias_ref, o_ref, *, tj):
    h = o_ref.shape[1]
    for e in range(tj):
        b = jnp.concatenate([b_ref[0, :, e, :], b_ref[1, :, e, :]], axis=0).astype(BF16)
        y = jnp.dot(m_ref[e], b, preferred_element_type=F32)
        o_ref[0, :, e, :] = y[:h]
        o_ref[1, :, e, :] = y[h:]
    o_ref[...] = x_ref[...] * (o_ref[...] + v_ref[...] * bias_ref[...])


def _fft_a_inv(b, m, v4, xm4, bias, *, n1, tj):
    c = b.shape[3]
    half = pl.BlockSpec((2, n1 // 2, tj, c), lambda j: (0, 0, j, 0))
    return pl.pallas_call(
        functools.partial(_fft_ai_body, tj=tj),
        out_shape=jax.ShapeDtypeStruct((2, n1 // 2, DFT2, c), F32),
        grid=(DFT2 // tj,),
        in_specs=[pl.BlockSpec((2, n1, tj, c), lambda j: (0, 0, j, 0)),
                  pl.BlockSpec((tj, n1, 2 * n1), lambda j: (j, 0, 0)),
                  half, half, pl.BlockSpec((1, c), lambda j: (0, 0))],
        out_specs=half,
        compiler_params=_cparams(("parallel",), 56),
        name="fft_stage_a_inv",
    )(b, m, v4, xm4, bias)


def _filt_a_body(hj_ref, hz_ref, hr_ref, s_ref, mf_ref, mr_ref, o_ref, *, tj):
    n1 = o_ref.shape[2]
    s = s_ref[...]
    for e in range(tj):
        src = hz_ref if e == 0 else hr_ref
        me = 0 if e == 0 else tj - e
        for o in range(2):
            c0 = 2 * D_GROUP * o
            inv = 1.0 / (s[:, c0:c0 + D_GROUP] + s[:, c0 + D_GROUP:c0 + 2 * D_GROUP] + EPS)
            lsl = slice(o * D_GROUP, (o + 1) * D_GROUP)
            hf = hj_ref[e, :, lsl].astype(BF16)
            hb = src[me, :, lsl].astype(BF16)
            a = (jnp.dot(mf_ref[e], hf, preferred_element_type=F32)
                 + jnp.dot(mr_ref[e], hb, preferred_element_type=F32)) * inv
            o_ref[0, e, :, lsl] = a[:n1]
            o_ref[1, e, :, lsl] = a[n1:]


def _filt_a(hf, hb, colsum, tabs, *, n1):
    tj = FFT_GROUP
    nj = DFT2 // tj
    wd = 2 * D_GROUP
    shp = (DFT2, n1 // 2, wd)
    blk = lambda fn: pl.BlockSpec((tj, n1 // 2, wd), fn)
    mat = pl.BlockSpec((tj, 2 * n1, n1 // 2), lambda j: (j, 0, 0))
    return pl.pallas_call(
        functools.partial(_filt_a_body, tj=tj),
        out_shape=jax.ShapeDtypeStruct((2, DFT2, n1, wd), F32),
        grid=(nj,),
        in_specs=[blk(lambda j: (j, 0, 0)), blk(lambda j: ((nj - j) % nj, 0, 0)), blk(lambda j: (nj - 1 - j, 0, 0)),
                  pl.BlockSpec((1, HY_FILTER_CH), lambda j: (0, 0)), mat, mat],
        out_specs=pl.BlockSpec((2, tj, n1, wd), lambda j: (0, j, 0, 0)),
        compiler_params=_cparams(("parallel",), 48),
        name="filter_stage_a",
    )(hf.reshape(shp), hb.reshape(shp), hb.reshape(shp), colsum, tabs["a_real"], tabs["a_rev"])


def _hyena_long(v, x1, x2, hf, hb, colsum, bias, *, seq):
    n1 = 2 * seq // DFT2
    tabs = {k: jnp.asarray(a).astype(BF16) for k, a in _dft_tables(n1).items()}
    tj = FFT_GROUP_DATA
    fa = _filt_a(hf, hb, colsum, tabs, n1=n1)
    g = _fft_b_forward(fa, tabs, n1=n1)
    shp = (2, n1 // 2, DFT2, D_GROUP)
    z = v.reshape(shp)
    for order, xm in ((0, x1.reshape(shp)), (1, x2.reshape(shp))):
        a = _fft_a(z, tabs["a_fwd"], n1=n1, tj=tj)
        b = _fft_b(a, g, order, tabs, n1=n1)
        z = _fft_a_inv(b, tabs["a_inv"], z, xm, bias[order].reshape(1, D_GROUP), n1=n1, tj=tj)
    return z.reshape(2 * seq, D_GROUP)


def _hyena_ctx_body(v_ref, x1_ref, x2_ref, hf_ref, hb_ref, s_ref, bias_ref, cm_ref, sm_ref, ct_ref, st_ref, o_ref):
    n = v_ref.shape[1]
    cm, sm, ct, st = cm_ref[...], sm_ref[...], ct_ref[...], st_ref[...]
    s = s_ref[...]
    row0 = _iota((n, D_GROUP), 0) == 0
    dot = lambda a, b: jnp.dot(a, b.astype(BF16), preferred_element_type=F32)
    zr, zi = v_ref[0], v_ref[1]
    for order, xm in ((0, x1_ref), (1, x2_ref)):
        c0 = 2 * D_GROUP * order
        inv = 1.0 / (s[:, c0:c0 + D_GROUP] + s[:, c0 + D_GROUP:c0 + 2 * D_GROUP] + EPS)
        lsl = slice(order * D_GROUP, (order + 1) * D_GROUP)
        hf = hf_ref[:, lsl] * inv
        hb = jnp.where(row0, 0.0, hb_ref[:, lsl] * inv)
        gr = dot(cm, hf + hb)
        gi = dot(sm, hb - hf)
        xr = dot(cm, zr) + dot(sm, zi)
        xi = dot(cm, zi) - dot(sm, zr)
        yr = xr * gr - xi * gi
        yi = xr * gi + xi * gr
        cr = dot(ct, yr) - dot(st, yi)
        ci = dot(ct, yi) + dot(st, yr)
        bias = bias_ref[order:order + 1, :]
        zr = xm[0] * (cr + zr * bias)
        zi = xm[1] * (ci + zi * bias)
    o_ref[0] = zr
    o_ref[1] = zi


def _hyena_ctx(v, x1, x2, hf, hb, colsum, bias, *, seq):
    nn = 2 * seq
    k = np.arange(nn)
    t = np.arange(seq)
    th = 2.0 * np.pi * (np.outer(k, t) % nn) / nn
    cm, sm = np.cos(th), np.sin(th)
    consts = [jnp.asarray(a, F32).astype(BF16) for a in (cm, sm, cm.T / nn, sm.T / nn)]
    shp = (2, seq, D_GROUP)
    full3 = pl.BlockSpec(shp, lambda i: (0, 0, 0))
    f2 = lambda a: pl.BlockSpec(a.shape, lambda i: (0, 0))
    args = [hf, hb, colsum, bias] + consts
    out = pl.pallas_call(
        _hyena_ctx_body,
        out_shape=jax.ShapeDtypeStruct(shp, F32),
        grid=(1,),
        in_specs=[full3, full3, full3] + [f2(a) for a in args],
        out_specs=full3,
        compiler_params=_cparams(("arbitrary",)),
        name="hyena_ctx",
    )(v.reshape(shp), x1.reshape(shp), x2.reshape(shp), *args)
    return out.reshape(2 * seq, D_GROUP)


def _split_w_in(w_in):
    wb = w_in.astype(BF16)
    parts = dict(kq=jnp.concatenate([wb[:, COL_K:COL_V], wb[:, COL_Q:COL_R]], axis=1), v=wb[:, COL_V:COL_GF],
                 r=wb[:, COL_R:COL_POOL], pool=wb[:, COL_POOL:COL_HY], hy=wb[:, COL_HY:COL_CONV],
                 conv=wb[:, COL_CONV:P_IN])
    gate = jnp.pad(w_in[:, COL_GF:COL_Q], ((0, 0), (0, LANE - 2 * GLA_LOWRANK)))
    return parts, gate


def _mix(u, p, *, bsz, seq, is_ctx, states, filt):
    sf, sb = states
    cpb = min(32, seq // CHUNK)
    if is_ctx:
        pool = _pool1d(u["pool"], p["pool_wbd"], p["pool_scale"], bsz=bsz, seq=seq)
    else:
        pool = _pool2d(u["pool"], p["pool_wbd"], p["pool_scale"], bsz=bsz, seq=seq)
    tmc = min(seq, 4096)
    v, x1, x2 = _hy_short(u["hy"], p["hy_short_w"], p["hy_short_b"], seq=seq, tm=tmc)
    if is_ctx:
        hy = _hyena_ctx(v, x1, x2, *filt, p["hy_bias"], seq=seq)
    else:
        hy = _hyena_long(v, x1, x2, *filt, p["hy_bias"], seq=seq)
    gla = _gla_read(u["kq"], u["v"], u["kq"], u["r"], u["lf"], u["lb"], sf, sb, p["gla_ng"],
                    bsz=bsz, seq=seq, cpb=cpb)
    conv = _conformer(u["conv"], p["conv_dw_w"], p["conv_dw_b"], p["conv_ln_g"], p["conv_ln_b"], seq=seq, tm=tmc)
    return [pool, hy, gla, conv]


def kernel(x, c, ctx, c_ctx, ada_w, ada_b, ffn1_norm, ffn1_wi, ffn1_wo, mix_norm, w_in, w_out, pool_w, pool_scale, hy_short_w, hy_short_b, hy_w1, hy_b1, hy_w2, hy_b2, hy_w3, hy_deltas, hy_bias, gla_gw_f, gla_gb_f, gla_gw_b, gla_gb_b, gla_norm, conv_dw_w, conv_dw_b, conv_ln_g, conv_ln_b, ffn2_norm, ffn2_wi, ffn2_wo, final_norm):
    bsz, seq, d = x.shape
    clen = ctx.shape[1]
    depth = ada_w.shape[0]
    assert bsz == 2, "the Hyena transform packs exactly two batch rows into one complex signal"
    xs = x.reshape(bsz * seq, d)
    cs = ctx.reshape(bsz * clen, d)
    cc = jnp.concatenate([c, c_ctx[None, :], jnp.zeros((8 - bsz - 1, d), F32)], axis=0)
    mod = _modulation(cc, ada_w, ada_b).reshape(depth, 8, N_MOD, d)
    crow = bsz
    head_eye = np.kron(np.eye(len(POOL_WINDOWS), dtype=np.float32), np.ones((POOL_CH, POOL_CH), np.float32))
    tmx = TOKEN_TILE
    tps_x = seq // tmx
    tmc = bsz * clen
    zero_state = jnp.zeros((bsz, GLA_HEADS * GLA_DV, GLA_QK), F32)
    wi1, wo1, wi2, wo2, wout = (w.astype(BF16) for w in (ffn1_wi, ffn1_wo, ffn2_wi, ffn2_wo, w_out))

    for l in range(depth):
        last = l == depth - 1
        wparts, wgate = _split_w_in(w_in[l])
        zlow = jnp.zeros((GLA_LOWRANK, GLA_QK), F32)
        gw = jnp.concatenate([jnp.concatenate([gla_gw_f[l], zlow], axis=1),
                              jnp.concatenate([zlow, gla_gw_b[l]], axis=1),
                              jnp.zeros((LANE - 2 * GLA_LOWRANK, 2 * GLA_QK), F32)], axis=0)
        gb = jnp.concatenate([gla_gb_f[l], gla_gb_b[l]]).reshape(1, -1)
        gw = _gate_fold(wgate, gw).astype(BF16)
        wbd = jnp.tile(pool_w[l].reshape(D_GROUP, POOL_CH), (1, len(POOL_WINDOWS))) * head_eye
        p = dict(pool_wbd=wbd.astype(BF16), pool_scale=pool_scale[l].reshape(1, -1),
                 hy_short_w=hy_short_w[l], hy_short_b=hy_short_b[l], hy_bias=hy_bias[l],
                 gla_ng=jnp.tile(gla_norm[l], GLA_HEADS).reshape(1, -1),
                 conv_dw_w=conv_dw_w[l], conv_dw_b=conv_dw_b[l], conv_ln_g=conv_ln_g[l], conv_ln_b=conv_ln_b[l])
        names = ["kq", "v", "r", "pool", "hy", "conv"]

        xs, *outs = _ffn_proj(xs, mod, ffn1_norm[l], wi1, wo1, mix_norm[l], [wparts[nm] for nm in names], gw, gb,
                              layer=l, row0=0, tm=2 * tmx, tiles_per_seq=tps_x // 2)
        ux = dict(zip(names + ["lf", "lb"], outs))
        cnames = ["kq", "v"] if last else names
        cs, *outs = _ffn_proj(cs, mod, ffn1_norm[l], wi1, wo1, mix_norm[l], [wparts[nm] for nm in cnames], gw, gb,
                              layer=l, row0=crow, tm=tmc, tiles_per_seq=1)
        uc = dict(zip(cnames + ["lf", "lb"], outs))

        ccpb = clen // CHUNK
        sfc, sbc, finf, finb = _gla_states(uc["kq"], uc["v"], uc["lf"], uc["lb"], zero_state, zero_state,
                                           bsz=bsz, seq=clen, cpb=ccpb)
        sfx, sbx, _, _ = _gla_states(ux["kq"], ux["v"], ux["lf"], ux["lb"], finf, finb, bsz=bsz, seq=seq,
                                     cpb=min(32, seq // CHUNK))

        filt = _hy_filter(seq, hy_w1[l], hy_b1[l], hy_w2[l], hy_b2[l], hy_w3[l], hy_deltas[l], t2_major=True)
        mixed = _mix(ux, p, bsz=bsz, seq=seq, is_ctx=False, states=(sfx, sbx), filt=filt)
        xs = _out_ffn(xs, mod, mixed, wout, ffn2_norm[l], wi2, wo2, layer=l, row0=0, tm=2 * tmx,
                      tiles_per_seq=tps_x // 2, final_g=final_norm if last else None)
        if not last:
            filt_c = _hy_filter(clen, hy_w1[l], hy_b1[l], hy_w2[l], hy_b2[l], hy_w3[l], hy_deltas[l],
                                t2_major=False)
            mixed = _mix(uc, p, bsz=bsz, seq=clen, is_ctx=True, states=(sfc, sbc), filt=filt_c)
            cs = _out_ffn(cs, mod, mixed, wout, ffn2_norm[l], wi2, wo2, layer=l, row0=crow, tm=tmc,
                          tiles_per_seq=1)
    return xs.reshape(bsz, seq, d)
```
